```python
import jax, jax.numpy as jnp
from jax import lax
import numpy as np

D_MODEL = 1024
BATCH = 1
SEQ = 16384
DEPTH = 2

N_EVEN = (DEPTH + 1) // 2
N_ODD = DEPTH // 2
N_SUBLAYERS = 3
D_FF = 2816
EPS = 1e-6

A_WIDTH = D_MODEL // 2
A_GROUPS = 8
A_CONV = 31
B_WIDTH = D_MODEL // 2
B_CONV = 3
EVEN_IN = 2 * A_WIDTH + 3 * B_WIDTH
EVEN_MIX = A_WIDTH + B_WIDTH

POOL_WINDOWS = (2, 4, 8, 16)
C_GROUPS = len(POOL_WINDOWS)
C_WIDTH = D_MODEL // 2
C_GROUP_DIM = C_WIDTH // C_GROUPS
D_HEADS = 8
HEAD_DIM = 64
D_WIDTH = D_HEADS * HEAD_DIM
ODD_IN = C_WIDTH + 3 * D_WIDTH
ODD_MIX = C_WIDTH + D_WIDTH
MOBA_BLOCK = 256
MOBA_TOPK = 3
Q_CHUNK = 128

kernel_name = "hybrid_conv_pool_moba_macaron_adaln"


def rms_norm(x, g):
    xf = x.astype(jnp.float32)
    y = xf * lax.rsqrt(jnp.mean(xf * xf, axis=-1, keepdims=True) + EPS)
    return (y * g.astype(jnp.float32)).astype(x.dtype)


def modulate(h, shift, scale):
    return h * (1.0 + scale[:, None, :]) + shift[:, None, :]


def group_layer_norm(u, n_groups, g, b):
    bn, s, ch = u.shape
    uf = u.astype(jnp.float32).reshape(bn, s, n_groups, ch // n_groups)
    mu = jnp.mean(uf, axis=-1, keepdims=True)
    var = jnp.mean(jnp.square(uf - mu), axis=-1, keepdims=True)
    y = ((uf - mu) * lax.rsqrt(var + EPS)).reshape(bn, s, ch)
    return (y * g + b).astype(u.dtype)


def causal_depthwise_conv(u, w):
    k, ch = w.shape
    return lax.conv_general_dilated(
        u, w[:, None, :], window_strides=(1,), padding=[(k - 1, 0)],
        dimension_numbers=("NWC", "WIO", "NWC"), feature_group_count=ch)


def swiglu(h, w_gate, w_up, w_down):
    return (jax.nn.silu(h @ w_gate) * (h @ w_up)) @ w_down


def conv_mixers(h, w_in, conv_a_w, conv_a_b, ln_a_g, ln_a_b, conv_b_w, w_out):
    z = h @ w_in
    a_val, a_gate, b_gate, c_gate, b_val = jnp.split(
        z, [A_WIDTH, 2 * A_WIDTH, 2 * A_WIDTH + B_WIDTH, 2 * A_WIDTH + 2 * B_WIDTH], axis=-1)
    a = a_val * jax.nn.sigmoid(a_gate)
    a = causal_depthwise_conv(a, conv_a_w) + conv_a_b
    a = jax.nn.silu(group_layer_norm(a, A_GROUPS, ln_a_g, ln_a_b))
    bb = b_gate * causal_depthwise_conv(c_gate * b_val, conv_b_w)
    return jnp.concatenate([a, bb], axis=-1) @ w_out


def multiscale_pool(u, pool_w, pool_b, pool_scale):
    bn, s, _ = u.shape
    ug = u.reshape(bn, s, C_GROUPS, C_GROUP_DIM).astype(jnp.float32)
    cs0 = jnp.pad(jnp.cumsum(ug, axis=1), ((0, 0), (1, 0), (0, 0), (0, 0)))
    t1 = jnp.arange(1, s + 1, dtype=jnp.float32)
    means = []
    for g, w in enumerate(POOL_WINDOWS):
        upper = cs0[:, 1:, g]
        lower = jnp.pad(cs0[:, :s + 1 - w, g], ((0, 0), (w - 1, 0), (0, 0)))
        means.append((upper - lower) / jnp.minimum(t1, w)[None, :, None])
    pooled = (jnp.stack(means, axis=2) - ug).astype(u.dtype)
    mixed = jnp.einsum("bsgc,gcd->bsgd", pooled, pool_w) + pool_b
    return mixed.reshape(bn, s, C_WIDTH) * pool_scale


def moba_attention(q, k, v):
    bn, s, _ = q.shape
    s_pad = -(-s // MOBA_BLOCK) * MOBA_BLOCK
    nb = s_pad // MOBA_BLOCK
    topk = min(MOBA_TOPK, nb)

    def to_heads(t):
        return t.reshape(bn, s, D_HEADS, HEAD_DIM).transpose(0, 2, 1, 3)

    pad = ((0, 0), (0, 0), (0, s_pad - s), (0, 0))
    qh = to_heads(q) * (HEAD_DIM ** -0.5)
    kh = jnp.pad(to_heads(k), pad)
    vh = jnp.pad(to_heads(v), pad)
    k_blocks = kh.reshape(bn, D_HEADS, nb, MOBA_BLOCK, HEAD_DIM)
    v_blocks = vh.reshape(bn, D_HEADS, nb, MOBA_BLOCK, HEAD_DIM)
    k_mean = jnp.mean(k_blocks.astype(jnp.float32), axis=3)
    slopes = jnp.asarray(2.0 ** (-8.0 * np.arange(1, D_HEADS + 1) / D_HEADS), dtype=jnp.float32)
    b_idx = jnp.arange(bn)[:, None, None, None]
    h_idx = jnp.arange(D_HEADS)[None, :, None, None]
    offs = jnp.arange(MOBA_BLOCK)
    blk_ids = jnp.arange(nb)
    slot_ids = jnp.arange(topk)

    def chunk(ci):
        q0 = ci * Q_CHUNK
        own = q0 // MOBA_BLOCK
        qc = lax.dynamic_slice_in_dim(qh, q0, Q_CHUNK, axis=2)
        qpos = q0 + jnp.arange(Q_CHUNK)
        gate = jnp.einsum("bhqd,bhnd->bhqn", qc.astype(jnp.float32), k_mean)
        gate = jnp.where(blk_ids < own, gate, -jnp.inf)
        _, sel = lax.top_k(gate, topk)
        sel_valid = slot_ids < own
        k_sel = k_blocks[b_idx, h_idx, sel]
        v_sel = v_blocks[b_idx, h_idx, sel]
        s_sel = jnp.einsum("bhqd,bhqjkd->bhqjk", qc, k_sel).astype(jnp.float32)
        kpos_sel = sel[..., None] * MOBA_BLOCK + offs
        s_sel = s_sel - slopes[:, None, None, None] * (qpos[:, None, None] - kpos_sel)
        s_sel = jnp.where(sel_valid[:, None], s_sel, -jnp.inf)
        s_sel = s_sel.reshape(bn, D_HEADS, Q_CHUNK, topk * MOBA_BLOCK)
        k_own = lax.dynamic_slice_in_dim(kh, own * MOBA_BLOCK, MOBA_BLOCK, axis=2)
        v_own = lax.dynamic_slice_in_dim(vh, own * MOBA_BLOCK, MOBA_BLOCK, axis=2)
        dist = qpos[:, None] - (own * MOBA_BLOCK + offs)[None, :]
        s_own = jnp.einsum("bhqd,bhkd->bhqk", qc, k_own).astype(jnp.float32)
        s_own = jnp.where(dist >= 0, s_own - slopes[:, None, None] * dist, -jnp.inf)
        p = jax.nn.softmax(jnp.concatenate([s_sel, s_own], axis=-1), axis=-1).astype(v.dtype)
        p_sel, p_own = jnp.split(p, [topk * MOBA_BLOCK], axis=-1)
        p_sel = p_sel.reshape(bn, D_HEADS, Q_CHUNK, topk, MOBA_BLOCK)
        return (jnp.einsum("bhqjk,bhqjkd->bhqd", p_sel, v_sel)
                + jnp.einsum("bhqk,bhkd->bhqd", p_own, v_own))

    out = lax.map(chunk, jnp.arange(s // Q_CHUNK))
    return out.transpose(1, 0, 3, 2, 4).reshape(bn, s, D_WIDTH)


def pool_moba_mixers(h, w_in, pool_w, pool_b, pool_scale, w_out):
    z = h @ w_in
    u, q, k, v = jnp.split(z, [C_WIDTH, C_WIDTH + D_WIDTH, C_WIDTH + 2 * D_WIDTH], axis=-1)
    pooled = multiscale_pool(u, pool_w, pool_b, pool_scale)
    att = moba_attention(q, k, v)
    return jnp.concatenate([pooled, att], axis=-1) @ w_out


def setup_inputs(seed: int = 0) -> dict:
    key = jax.random.key(seed)
    ks = jax.random.split(key, 24)
    f32 = jnp.float32

    def nrm(k, shape, scale):
        return jax.random.normal(k, shape, dtype=f32) * scale

    return {
        "x": nrm(ks[0], (BATCH, SEQ, D_MODEL), 1.0),
        "c": nrm(ks[1], (BATCH, D_MODEL), 1.0),
        "ada_w": nrm(ks[2], (DEPTH, D_MODEL, N_SUBLAYERS * 3 * D_MODEL), D_MODEL ** -0.5),
        "ada_b": nrm(ks[3], (DEPTH, N_SUBLAYERS * 3 * D_MODEL), 0.02),
        "ffn_norm": 1.0 + nrm(ks[4], (DEPTH, 2, D_MODEL), 0.05),
        "ffn_w_gate": nrm(ks[5], (DEPTH, 2, D_MODEL, D_FF), D_MODEL ** -0.5),
        "ffn_w_up": nrm(ks[6], (DEPTH, 2, D_MODEL, D_FF), D_MODEL ** -0.5),
        "ffn_w_down": nrm(ks[7], (DEPTH, 2, D_FF, D_MODEL), D_FF ** -0.5),
        "mix_norm": 1.0 + nrm(ks[8], (DEPTH, D_MODEL), 0.05),
        "conv_w_in": nrm(ks[9], (N_EVEN, D_MODEL, EVEN_IN), D_MODEL ** -0.5),
        "conv_a_w": nrm(ks[10], (N_EVEN, A_CONV, A_WIDTH), A_CONV ** -0.5),
        "conv_a_b": nrm(ks[11], (N_EVEN, A_WIDTH), 0.02),
        "conv_a_ln_g": 1.0 + nrm(ks[12], (N_EVEN, A_WIDTH), 0.05),
        "conv_a_ln_b": nrm(ks[13], (N_EVEN, A_WIDTH), 0.02),
        "conv_b_w": nrm(ks[14], (N_EVEN, B_CONV, B_WIDTH), B_CONV ** -0.5),
        "conv_w_out": nrm(ks[15], (N_EVEN, EVEN_MIX, D_MODEL), EVEN_MIX ** -0.5),
        "pm_w_in": nrm(ks[16], (N_ODD, D_MODEL, ODD_IN), D_MODEL ** -0.5),
        "pool_w": nrm(ks[17], (N_ODD, C_GROUPS, C_GROUP_DIM, C_GROUP_DIM), C_GROUP_DIM ** -0.5),
        "pool_b": nrm(ks[18], (N_ODD, C_GROUPS, C_GROUP_DIM), 0.02),
        "pool_scale": 1.0 + nrm(ks[19], (N_ODD, C_WIDTH), 0.1),
        "pm_w_out": nrm(ks[20], (N_ODD, ODD_MIX, D_MODEL), ODD_MIX ** -0.5),
        "final_norm": 1.0 + nrm(ks[21], (D_MODEL,), 0.05),
    }


def reference(x, c, ada_w, ada_b, ffn_norm, ffn_w_gate, ffn_w_up, ffn_w_down, mix_norm,
              conv_w_in, conv_a_w, conv_a_b, conv_a_ln_g, conv_a_ln_b, conv_b_w, conv_w_out,
              pm_w_in, pool_w, pool_b, pool_scale, pm_w_out, final_norm):
    bn = x.shape[0]
    cond = jax.nn.silu(c)
    for i in range(DEPTH):
        mod = (cond @ ada_w[i] + ada_b[i]).reshape(bn, N_SUBLAYERS, 3, D_MODEL)
        h = modulate(rms_norm(x, ffn_norm[i, 0]), mod[:, 0, 0], mod[:, 0, 1])
        x = x + 0.5 * mod[:, 0, 2][:, None, :] * swiglu(
            h, ffn_w_gate[i, 0], ffn_w_up[i, 0], ffn_w_down[i, 0])
        h = modulate(rms_norm(x, mix_norm[i]), mod[:, 1, 0], mod[:, 1, 1])
        if i % 2 == 0:
            e = i // 2
            y = conv_mixers(h, conv_w_in[e], conv_a_w[e], conv_a_b[e], conv_a_ln_g[e],
                            conv_a_ln_b[e], conv_b_w[e], conv_w_out[e])
        else:
            o = i // 2
            y = pool_moba_mixers(h, pm_w_in[o], pool_w[o], pool_b[o], pool_scale[o], pm_w_out[o])
        x = x + mod[:, 1, 2][:, None, :] * y
        h = modulate(rms_norm(x, ffn_norm[i, 1]), mod[:, 2, 0], mod[:, 2, 1])
        x = x + 0.5 * mod[:, 2, 2][:, None, :] * swiglu(
            h, ffn_w_gate[i, 1], ffn_w_up[i, 1], ffn_w_down[i, 1])
    return rms_norm(x, final_norm)
```

```python
import functools

import numpy as np
import jax
import jax.numpy as jnp
from jax import lax
from jax.experimental import pallas as pl
from jax.experimental.pallas import tpu as pltpu

D_MODEL = 1024
SEQ = 16384
DEPTH = 2
N_SUBLAYERS = 3
D_FF = 2816
EPS = 1e-6

A_WIDTH = 512
A_GROUPS = 8
A_CONV = 31
B_WIDTH = 512
B_CONV = 3

POOL_WINDOWS = (2, 4, 8, 16)
C_WIDTH = 512
C_GROUP_DIM = 128
D_HEADS = 8
HEAD_DIM = 64
D_WIDTH = 512
MOBA_BLOCK = 256
MOBA_TOPK = 3
N_BLOCKS = SEQ // MOBA_BLOCK

V7X_SUBLANES = 8
V7X_VMEM_LIMIT_BYTES = 56 * 1024 * 1024

A_HALO = 32
B_HALO = 8
POOL_HALO = 16
NEG_BIG = -1e30

BF16 = jnp.bfloat16
F32 = jnp.float32


def _params(n_axes):
    return pltpu.CompilerParams(
        dimension_semantics=("arbitrary",) * n_axes,
        vmem_limit_bytes=V7X_VMEM_LIMIT_BYTES)


def _resident(block_shape, index_map):
    return pl.BlockSpec(block_shape, index_map, pipeline_mode=pl.Buffered(1))


def _sigmoid(v):
    return 1.0 / (1.0 + jnp.exp(-v))


def _norm_mod(x, g, mod):
    r = lax.rsqrt(jnp.mean(x * x, axis=-1, keepdims=True) + EPS)
    return (x * r) * (g * (1.0 + mod[1:2])) + mod[0:1]


def _split_dot(v, w):
    hi = v.astype(BF16)
    lo = (v - hi.astype(F32)).astype(BF16)
    return (jnp.dot(hi, w, preferred_element_type=F32)
            + jnp.dot(lo, w, preferred_element_type=F32))


def _mod_kernel(c_ref, w_ref, b_ref, o_ref):
    c = c_ref[...]
    cond = c * _sigmoid(c)
    o_ref[0] = jnp.sum(w_ref[0] * cond, axis=0, keepdims=True) + b_ref[0]


def _mod_call(c, ada_w, ada_b):
    n_out = N_SUBLAYERS * 3 * D_MODEL
    tn = D_MODEL
    out = pl.pallas_call(
        _mod_kernel,
        grid=(DEPTH, n_out // tn),
        in_specs=[
            pl.BlockSpec((D_MODEL, 1), lambda l, j: (0, 0)),
            pl.BlockSpec((1, D_MODEL, tn), lambda l, j: (l, 0, j)),
            pl.BlockSpec((1, 1, tn), lambda l, j: (l, 0, j)),
        ],
        out_specs=pl.BlockSpec((1, 1, tn), lambda l, j: (l, 0, j)),
        out_shape=jax.ShapeDtypeStruct((DEPTH, 1, n_out), F32),
        compiler_params=_params(2),
        name="adaln_mod",
    )(c.reshape(D_MODEL, 1), ada_w, ada_b.reshape(DEPTH, 1, n_out))
    return out.reshape(DEPTH, N_SUBLAYERS, 3, D_MODEL)


def _ffn_kernel(x_ref, mod_ref, g_ref, wg_ref, wu_ref, wd_ref, fin_ref, o_ref, *, final):
    x = x_ref[...]
    mod = mod_ref[...]
    h = _norm_mod(x, g_ref[...], mod).astype(BF16)
    gate = jnp.dot(h, wg_ref[...], preferred_element_type=F32)
    up = jnp.dot(h, wu_ref[...], preferred_element_type=F32)
    act = (gate * _sigmoid(gate) * up).astype(BF16)
    y = jnp.dot(act, wd_ref[...], preferred_element_type=F32)
    xn = x + (0.5 * mod[2:3]) * y
    if final:
        r = lax.rsqrt(jnp.mean(xn * xn, axis=-1, keepdims=True) + EPS)
        xn = xn * r * fin_ref[...]
    o_ref[...] = xn


def _ffn_call(x, mod, norm_g, wg, wu, wd, fin, *, layer, sub, final, tm=512):
    mod_sub = 0 if sub == 0 else 2
    return pl.pallas_call(
        functools.partial(_ffn_kernel, final=final),
        grid=(SEQ // tm,),
        in_specs=[
            pl.BlockSpec((tm, D_MODEL), lambda i: (i, 0)),
            _resident((None, None, 3, D_MODEL), lambda i: (layer, mod_sub, 0, 0)),
            _resident((None, None, 1, D_MODEL), lambda i: (layer, sub, 0, 0)),
            _resident((None, None, D_MODEL, D_FF), lambda i: (layer, sub, 0, 0)),
            _resident((None, None, D_MODEL, D_FF), lambda i: (layer, sub, 0, 0)),
            _resident((None, None, D_FF, D_MODEL), lambda i: (layer, sub, 0, 0)),
            _resident((1, D_MODEL), lambda i: (0, 0)),
        ],
        out_specs=pl.BlockSpec((tm, D_MODEL), lambda i: (i, 0)),
        out_shape=jax.ShapeDtypeStruct((SEQ, D_MODEL), F32),
        compiler_params=_params(1),
        name=f"ffn_l{layer}s{sub}",
    )(x, mod, norm_g, wg, wu, wd, fin)


def _conv_mix_kernel(x_ref, mod_ref, g_ref, win_ref, caw_ref, cab_ref, lng_ref, lnb_ref,
                     cbw_ref, wout_ref, seg_ref, o_ref, abuf, bbuf, *, tm):
    i = pl.program_id(0)

    @pl.when(i == 0)
    def _():
        abuf[0:A_HALO, :] = jnp.zeros((A_HALO, A_WIDTH), F32)
        bbuf[0:B_HALO, :] = jnp.zeros((B_HALO, B_WIDTH), F32)

    x = x_ref[...]
    mod = mod_ref[...]
    h = _norm_mod(x, g_ref[...], mod).astype(BF16)
    z = jnp.dot(h, win_ref[...], preferred_element_type=F32)

    a = z[:, 0:A_WIDTH] * _sigmoid(z[:, A_WIDTH:2 * A_WIDTH])
    abuf[A_HALO:A_HALO + tm, :] = a
    acc = jnp.zeros((tm, A_WIDTH), F32) + cab_ref[...]
    for k in range(A_CONV):
        acc = acc + caw_ref[k:k + 1, :] * abuf[pl.ds(A_HALO - (A_CONV - 1) + k, tm), :]
    abuf[0:A_HALO, :] = abuf[tm:tm + A_HALO, :]
    seg = seg_ref[...]
    mu = _split_dot(acc, seg)
    d = acc - mu
    var = _split_dot(d * d, seg)
    yn = d * lax.rsqrt(var + EPS) * lng_ref[...] + lnb_ref[...]
    a_out = yn * _sigmoid(yn)

    off = 2 * A_WIDTH
    cv = z[:, off + B_WIDTH:off + 2 * B_WIDTH] * z[:, off + 2 * B_WIDTH:off + 3 * B_WIDTH]
    bbuf[B_HALO:B_HALO + tm, :] = cv
    bacc = jnp.zeros((tm, B_WIDTH), F32)
    for k in range(B_CONV):
        bacc = bacc + cbw_ref[k:k + 1, :] * bbuf[pl.ds(B_HALO - (B_CONV - 1) + k, tm), :]
    bbuf[0:B_HALO, :] = bbuf[tm:tm + B_HALO, :]
    bb = z[:, off:off + B_WIDTH] * bacc

    y = (jnp.dot(a_out.astype(BF16), wout_ref[0:A_WIDTH, :], preferred_element_type=F32)
         + jnp.dot(bb.astype(BF16), wout_ref[A_WIDTH:A_WIDTH + B_WIDTH, :],
                   preferred_element_type=F32))
    o_ref[...] = x + mod[2:3] * y


def _conv_mix_call(x, mod, norm_g, w_in, caw, cab, lng, lnb, cbw, w_out, seg, *, layer, tm=512):
    even_in = w_in.shape[1]
    full = lambda shape: _resident(shape, lambda i: (0,) * len(shape))
    return pl.pallas_call(
        functools.partial(_conv_mix_kernel, tm=tm),
        grid=(SEQ // tm,),
        in_specs=[
            pl.BlockSpec((tm, D_MODEL), lambda i: (i, 0)),
            _resident((None, None, 3, D_MODEL), lambda i: (layer, 1, 0, 0)),
            _resident((None, 1, D_MODEL), lambda i: (layer, 0, 0)),
            full((D_MODEL, even_in)),
            full((A_CONV, A_WIDTH)),
            full((1, A_WIDTH)),
            full((1, A_WIDTH)),
            full((1, A_WIDTH)),
            full((B_CONV, B_WIDTH)),
            full((A_WIDTH + B_WIDTH, D_MODEL)),
            full((A_WIDTH, A_WIDTH)),
        ],
        out_specs=pl.BlockSpec((tm, D_MODEL), lambda i: (i, 0)),
        out_shape=jax.ShapeDtypeStruct((SEQ, D_MODEL), F32),
        scratch_shapes=[pltpu.VMEM((A_HALO + tm, A_WIDTH), F32),
                        pltpu.VMEM((B_HALO + tm, B_WIDTH), F32)],
        compiler_params=_params(1),
        name="conv_mixers",
    )(x, mod, norm_g, w_in, caw, cab, lng, lnb, cbw, w_out, seg)


def _pm_proj_kernel(x_ref, mod_ref, g_ref, wu_ref, wqt_ref, wk_ref, wvt_ref, pw_ref, pb_ref,
                    ps_ref, pooled_ref, qt_ref, k_ref, vt_ref, km_ref, ubuf, *, tm):
    i = pl.program_id(0)

    @pl.when(i == 0)
    def _():
        ubuf[0:POOL_HALO, :] = jnp.zeros((POOL_HALO, C_WIDTH), F32)

    x = x_ref[...]
    h = _norm_mod(x, g_ref[...], mod_ref[...]).astype(BF16)
    u = jnp.dot(h, wu_ref[...], preferred_element_type=F32)
    kk = jnp.dot(h, wk_ref[...], preferred_element_type=F32)
    nt = (((1,), (1,)), ((), ()))
    qt = lax.dot_general(wqt_ref[...], h, nt, preferred_element_type=F32)
    vt = lax.dot_general(wvt_ref[...], h, nt, preferred_element_type=F32)

    ubuf[POOL_HALO:POOL_HALO + tm, :] = u
    t1 = (i * tm + 1 + lax.broadcasted_iota(jnp.int32, (tm, C_GROUP_DIM), 0)).astype(F32)
    for gi, w in enumerate(POOL_WINDOWS):
        c0 = gi * C_GROUP_DIM
        s = u[:, c0:c0 + C_GROUP_DIM]
        for k in range(1, w):
            s = s + ubuf[pl.ds(POOL_HALO - k, tm), c0:c0 + C_GROUP_DIM]
        pooled = s / jnp.minimum(t1, float(w)) - u[:, c0:c0 + C_GROUP_DIM]
        mixed = (jnp.dot(pooled.astype(BF16), pw_ref[gi], preferred_element_type=F32)
                 + pb_ref[:, c0:c0 + C_GROUP_DIM])
        pooled_ref[:, c0:c0 + C_GROUP_DIM] = (
            mixed * ps_ref[:, c0:c0 + C_GROUP_DIM]).astype(BF16)
    ubuf[0:POOL_HALO, :] = ubuf[tm:tm + POOL_HALO, :]

    qt_ref[...] = (qt * (HEAD_DIM ** -0.5)).astype(BF16).reshape(D_HEADS, HEAD_DIM, tm)
    vt_ref[...] = vt.astype(BF16).reshape(D_HEADS, HEAD_DIM, tm)
    kb = kk.astype(BF16)
    for hh in range(D_HEADS):
        k_ref[hh] = kb[:, hh * HEAD_DIM:(hh + 1) * HEAD_DIM]
    per_tile = tm // MOBA_BLOCK
    for sb in range(per_tile):
        blk = kk[sb * MOBA_BLOCK:(sb + 1) * MOBA_BLOCK, :]
        km_ref[pl.ds(i * per_tile + sb, 1), :] = jnp.mean(blk, axis=0, keepdims=True)


def _pm_proj_call(x, mod, norm_g, wu, wqt, wk, wvt, pw, pb, ps, *, layer, tm=512):
    full = lambda shape: _resident(shape, lambda i: (0,) * len(shape))
    return pl.pallas_call(
        functools.partial(_pm_proj_kernel, tm=tm),
        grid=(SEQ // tm,),
        in_specs=[
            pl.BlockSpec((tm, D_MODEL), lambda i: (i, 0)),
            _resident((None, None, 3, D_MODEL), lambda i: (layer, 1, 0, 0)),
            _resident((None, 1, D_MODEL), lambda i: (layer, 0, 0)),
            full((D_MODEL, C_WIDTH)),
            full((D_WIDTH, D_MODEL)),
            full((D_MODEL, D_WIDTH)),
            full((D_WIDTH, D_MODEL)),
            full((len(POOL_WINDOWS), C_GROUP_DIM, C_GROUP_DIM)),
            full((1, C_WIDTH)),
            full((1, C_WIDTH)),
        ],
        out_specs=[
            pl.BlockSpec((tm, C_WIDTH), lambda i: (i, 0)),
            pl.BlockSpec((D_HEADS, HEAD_DIM, tm), lambda i: (0, 0, i)),
            pl.BlockSpec((D_HEADS, tm, HEAD_DIM), lambda i: (0, i, 0)),
            pl.BlockSpec((D_HEADS, HEAD_DIM, tm), lambda i: (0, 0, i)),
            pl.BlockSpec((N_BLOCKS, D_WIDTH), lambda i: (0, 0)),
        ],
        out_shape=[
            jax.ShapeDtypeStruct((SEQ, C_WIDTH), BF16),
            jax.ShapeDtypeStruct((D_HEADS, HEAD_DIM, SEQ), BF16),
            jax.ShapeDtypeStruct((D_HEADS, SEQ, HEAD_DIM), BF16),
            jax.ShapeDtypeStruct((D_HEADS, HEAD_DIM, SEQ), BF16),
            jax.ShapeDtypeStruct((N_BLOCKS, D_WIDTH), F32),
        ],
        scratch_shapes=[pltpu.VMEM((POOL_HALO + tm, C_WIDTH), F32)],
        compiler_params=_params(1),
        name="pool_qkv_proj",
    )(x, mod, norm_g, wu, wqt, wk, wvt, pw, pb, ps)


def _moba_kernel(slope_ref, q_ref, k_ref, v_ref, km_ref, o_ref, sel_ref):
    hd = pl.program_id(0)
    own = pl.program_id(1)
    slope = slope_ref[hd]
    qt = q_ref[...]

    gate = _split_dot(km_ref[...], qt)
    blk = lax.broadcasted_iota(jnp.int32, (N_BLOCKS, MOBA_BLOCK), 0)
    g = jnp.where(blk < own, gate, -jnp.inf)
    sel = jnp.zeros((N_BLOCKS, MOBA_BLOCK), F32)
    for s in range(MOBA_TOPK):
        best = jnp.max(g, axis=0, keepdims=True)
        idx = jnp.min(jnp.where(g == best, blk, N_BLOCKS), axis=0, keepdims=True)
        hit = blk == idx
        sel = jnp.where(jnp.logical_and(hit, s < own), 1.0, sel)
        g = jnp.where(hit, -jnp.inf, g)
    sel_ref[...] = sel

    row = lax.broadcasted_iota(jnp.int32, (MOBA_BLOCK, MOBA_BLOCK), 0)
    col = lax.broadcasted_iota(jnp.int32, (MOBA_BLOCK, MOBA_BLOCK), 1)
    dist = (col - row).astype(F32)

    base = pl.multiple_of(own * MOBA_BLOCK, MOBA_BLOCK)
    st = jnp.dot(k_ref[pl.ds(base, MOBA_BLOCK), :], qt, preferred_element_type=F32)
    st = jnp.where(dist >= 0, st - slope * dist, NEG_BIG)
    m0 = jnp.max(st, axis=0, keepdims=True)
    p = jnp.exp(st - m0)
    l0 = jnp.sum(p, axis=0, keepdims=True)
    acc0 = jnp.dot(v_ref[:, pl.ds(base, MOBA_BLOCK)], p.astype(BF16),
                   preferred_element_type=F32)

    def body(j, carry):
        m, l, acc = carry
        kb = pl.multiple_of(j * MOBA_BLOCK, MOBA_BLOCK)
        sj = jnp.dot(k_ref[pl.ds(kb, MOBA_BLOCK), :], qt, preferred_element_type=F32)
        gap = ((own - j) * MOBA_BLOCK).astype(F32)
        sj = sj - slope * (dist + gap)
        picked = sel_ref[pl.ds(j, 1), :] > 0.5
        sj = jnp.where(picked, sj, NEG_BIG)
        m_new = jnp.maximum(m, jnp.max(sj, axis=0, keepdims=True))
        alpha = jnp.exp(m - m_new)
        pj = jnp.exp(sj - m_new)
        l_new = alpha * l + jnp.sum(pj, axis=0, keepdims=True)
        acc_new = alpha * acc + jnp.dot(v_ref[:, pl.ds(kb, MOBA_BLOCK)], pj.astype(BF16),
                                        preferred_element_type=F32)
        return m_new, l_new, acc_new

    m, l, acc = lax.fori_loop(0, own, body, (m0, l0, acc0))
    o_ref[...] = (acc / l).astype(BF16)


def _moba_call(slopes, qt, k, vt, kmean_h):
    return pl.pallas_call(
        _moba_kernel,
        grid=(D_HEADS, N_BLOCKS),
        in_specs=[
            pl.BlockSpec(memory_space=pltpu.SMEM),
            pl.BlockSpec((None, HEAD_DIM, MOBA_BLOCK), lambda h, i: (h, 0, i)),
            pl.BlockSpec((None, SEQ, HEAD_DIM), lambda h, i: (h, 0, 0)),
            pl.BlockSpec((None, HEAD_DIM, SEQ), lambda h, i: (h, 0, 0)),
            pl.BlockSpec((None, N_BLOCKS, HEAD_DIM), lambda h, i: (h, 0, 0)),
        ],
        out_specs=pl.BlockSpec((None, HEAD_DIM, MOBA_BLOCK), lambda h, i: (h, 0, i)),
        out_shape=jax.ShapeDtypeStruct((D_HEADS, HEAD_DIM, SEQ), BF16),
        scratch_shapes=[pltpu.VMEM((N_BLOCKS, MOBA_BLOCK), F32)],
        compiler_params=_params(2),
        name="moba_attention",
    )(slopes, qt, k, vt, kmean_h)


def _pm_out_kernel(x_ref, mod_ref, pooled_ref, att_ref, wp_ref, wa_ref, o_ref, *, tm):
    att_t = att_ref[...].reshape(D_WIDTH, tm)
    tn = (((0,), (0,)), ((), ()))
    y = (jnp.dot(pooled_ref[...], wp_ref[...], preferred_element_type=F32)
         + lax.dot_general(att_t, wa_ref[...], tn, preferred_element_type=F32))
    o_ref[...] = x_ref[...] + mod_ref[2:3, :] * y


def _pm_out_call(x, mod, pooled, att_t, wp, wa, *, layer, tm=512):
    full = lambda shape: _resident(shape, lambda i: (0,) * len(shape))
    return pl.pallas_call(
        functools.partial(_pm_out_kernel, tm=tm),
        grid=(SEQ // tm,),
        in_specs=[
            pl.BlockSpec((tm, D_MODEL), lambda i: (i, 0)),
            _resident((None, None, 3, D_MODEL), lambda i: (layer, 1, 0, 0)),
            pl.BlockSpec((tm, C_WIDTH), lambda i: (i, 0)),
            pl.BlockSpec((D_HEADS, HEAD_DIM, tm), lambda i: (0, 0, i)),
            full((C_WIDTH, D_MODEL)),
            full((D_WIDTH, D_MODEL)),
        ],
        out_specs=pl.BlockSpec((tm, D_MODEL), lambda i: (i, 0)),
        out_shape=jax.ShapeDtypeStruct((SEQ, D_MODEL), F32),
        compiler_params=_params(1),
        name="pool_moba_out",
    )(x, mod, pooled, att_t, wp, wa)


def kernel(x, c, ada_w, ada_b, ffn_norm, ffn_w_gate, ffn_w_up, ffn_w_down, mix_norm, conv_w_in,
           conv_a_w, conv_a_b, conv_a_ln_g, conv_a_ln_b, conv_b_w, conv_w_out, pm_w_in, pool_w,
           pool_b, pool_scale, pm_w_out, final_norm):
    assert x.shape == (1, SEQ, D_MODEL) and c.shape == (1, D_MODEL)
    xs = x.reshape(SEQ, D_MODEL)
    mod = _mod_call(c, ada_w, ada_b)

    wg = ffn_w_gate.astype(BF16)
    wu = ffn_w_up.astype(BF16)
    wd = ffn_w_down.astype(BF16)
    ffn_g = ffn_norm.reshape(DEPTH, 2, 1, D_MODEL)
    mix_g = mix_norm.reshape(DEPTH, 1, D_MODEL)
    fin = final_norm.reshape(1, D_MODEL)
    ffn = functools.partial(_ffn_call, mod=mod, norm_g=ffn_g, wg=wg, wu=wu, wd=wd, fin=fin)

    xs = ffn(xs, layer=0, sub=0, final=False)
    grp = np.arange(A_WIDTH) // (A_WIDTH // A_GROUPS)
    seg = jnp.asarray((grp[:, None] == grp[None, :]) / (A_WIDTH // A_GROUPS), dtype=BF16)
    xs = _conv_mix_call(
        xs, mod, mix_g, conv_w_in[0].astype(BF16), conv_a_w[0], conv_a_b[0].reshape(1, A_WIDTH),
        conv_a_ln_g[0].reshape(1, A_WIDTH), conv_a_ln_b[0].reshape(1, A_WIDTH), conv_b_w[0],
        conv_w_out[0].astype(BF16), seg, layer=0)
    xs = ffn(xs, layer=0, sub=1, final=False)

    xs = ffn(xs, layer=1, sub=0, final=False)
    w_in = pm_w_in[0].astype(BF16)
    w_u = w_in[:, 0:C_WIDTH]
    w_qt = w_in[:, C_WIDTH:C_WIDTH + D_WIDTH].T
    w_k = w_in[:, C_WIDTH + D_WIDTH:C_WIDTH + 2 * D_WIDTH]
    w_vt = w_in[:, C_WIDTH + 2 * D_WIDTH:C_WIDTH + 3 * D_WIDTH].T
    pooled, qt, kh, vt, kmean = _pm_proj_call(
        xs, mod, mix_g, w_u, w_qt, w_k, w_vt, pool_w[0].astype(BF16),
        pool_b[0].reshape(1, C_WIDTH), pool_scale[0].reshape(1, C_WIDTH), layer=1)
    kmean_h = kmean.reshape(N_BLOCKS, D_HEADS, HEAD_DIM).transpose(1, 0, 2)
    slopes = jnp.asarray(2.0 ** (-8.0 * np.arange(1, D_HEADS + 1) / D_HEADS), dtype=F32)
    att_t = _moba_call(slopes, qt, kh, vt, kmean_h)
    w_out = pm_w_out[0].astype(BF16)
    xs = _pm_out_call(xs, mod, pooled, att_t, w_out[0:C_WIDTH], w_out[C_WIDTH:], layer=1)
    xs = ffn(xs, layer=1, sub=1, final=True)
    return xs.reshape(1, SEQ, D_MODEL)
```

```python
import functools

import numpy as np
import jax
import jax.numpy as jnp
from jax import lax
from jax.experimental import pallas as pl
from jax.experimental.pallas import tpu as pltpu
from jax.experimental.pallas import tpu_sc as plsc

D_MODEL = 1024
SEQ = 16384
DEPTH = 2
N_SUBLAYERS = 3
D_FF = 2816
EPS = 1e-6

A_WIDTH = 512
A_GROUPS = 8
A_CONV = 31
B_WIDTH = 512
B_CONV = 3

POOL_WINDOWS = (2, 4, 8, 16)
C_WIDTH = 512
C_GROUP_DIM = 128
D_HEADS = 8
HEAD_DIM = 64
D_WIDTH = 512
MOBA_BLOCK = 256
MOBA_TOPK = 3
N_BLOCKS = SEQ // MOBA_BLOCK

V7X_VMEM_LIMIT_BYTES = 56 * 1024 * 1024

A_HALO = 32
B_HALO = 8
POOL_HALO = 16
NEG_BIG = -1e30

HEAD_PAD = 128
ALIBI_COL = HEAD_DIM
PART_M, PART_L, PART_J = HEAD_DIM, HEAD_DIM + 1, HEAD_DIM + 2

GATE_HEADS_PER_STEP = 4
ROUTE_TILE = 256
N_PAIRS = D_HEADS * SEQ * MOBA_TOPK
N_GROUPS = D_HEADS * N_BLOCKS
MAX_TILES = N_PAIRS // ROUTE_TILE + N_GROUPS
TRASH_ROW0 = MAX_TILES * ROUTE_TILE
ROUTE_ROWS = (MAX_TILES + 1) * ROUTE_TILE
SC_WINDOW = 128

BF16 = jnp.bfloat16
F32 = jnp.float32
I32 = jnp.int32


def _params(n_axes):
    return pltpu.CompilerParams(
        dimension_semantics=("arbitrary",) * n_axes,
        vmem_limit_bytes=V7X_VMEM_LIMIT_BYTES)


def _resident(block_shape, index_map):
    return pl.BlockSpec(block_shape, index_map, pipeline_mode=pl.Buffered(1))


def _full(shape):
    return _resident(shape, lambda *_: (0,) * len(shape))


def _sigmoid(v):
    return 1.0 / (1.0 + jnp.exp(-v))


def _norm_mod(x, g, mod):
    r = lax.rsqrt(jnp.mean(x * x, axis=-1, keepdims=True) + EPS)
    return (x * r) * (g * (1.0 + mod[1:2])) + mod[0:1]


def _split(v):
    hi = v.astype(BF16)
    return hi, (v - hi.astype(F32)).astype(BF16)


def _split_dot(v, w):
    hi, lo = _split(v)
    return (jnp.dot(hi, w, preferred_element_type=F32)
            + jnp.dot(lo, w, preferred_element_type=F32))


_NT = (((1,), (1,)), ((), ()))


def _mod_kernel(c_ref, w_ref, b_ref, o_ref):
    c = c_ref[...]
    cond = c * _sigmoid(c)
    o_ref[0] = jnp.sum(w_ref[0] * cond, axis=0, keepdims=True) + b_ref[0]


def _mod_call(c, ada_w, ada_b):
    n_out = N_SUBLAYERS * 3 * D_MODEL
    tn = D_MODEL
    out = pl.pallas_call(
        _mod_kernel,
        grid=(DEPTH, n_out // tn),
        in_specs=[
            pl.BlockSpec((D_MODEL, 1), lambda l, j: (0, 0)),
            pl.BlockSpec((1, D_MODEL, tn), lambda l, j: (l, 0, j)),
            pl.BlockSpec((1, 1, tn), lambda l, j: (l, 0, j)),
        ],
        out_specs=pl.BlockSpec((1, 1, tn), lambda l, j: (l, 0, j)),
        out_shape=jax.ShapeDtypeStruct((DEPTH, 1, n_out), F32),
        compiler_params=_params(2),
        name="adaln_mod",
    )(c.reshape(D_MODEL, 1), ada_w, ada_b.reshape(DEPTH, 1, n_out))
    return out.reshape(DEPTH, N_SUBLAYERS, 3, D_MODEL)


def _ffn_kernel(x_ref, mod_ref, g_ref, wg_ref, wu_ref, wd_ref, fin_ref, o_ref, *, final):
    x = x_ref[...]
    mod = mod_ref[...]
    h = _norm_mod(x, g_ref[...], mod).astype(BF16)
    gate = jnp.dot(h, wg_ref[...], preferred_element_type=F32)
    up = jnp.dot(h, wu_ref[...], preferred_element_type=F32)
    act = (gate * _sigmoid(gate) * up).astype(BF16)
    y = jnp.dot(act, wd_ref[...], preferred_element_type=F32)
    xn = x + (0.5 * mod[2:3]) * y
    if final:
        r = lax.rsqrt(jnp.mean(xn * xn, axis=-1, keepdims=True) + EPS)
        xn = xn * r * fin_ref[...]
    o_ref[...] = xn


def _ffn_call(x, mod, norm_g, wg, wu, wd, fin, *, layer, sub, final, tm=512):
    mod_sub = 0 if sub == 0 else 2
    return pl.pallas_call(
        functools.partial(_ffn_kernel, final=final),
        grid=(SEQ // tm,),
        in_specs=[
            pl.BlockSpec((tm, D_MODEL), lambda i: (i, 0)),
            _resident((None, None, 3, D_MODEL), lambda i: (layer, mod_sub, 0, 0)),
            _resident((None, None, 1, D_MODEL), lambda i: (layer, sub, 0, 0)),
            _resident((None, None, D_MODEL, D_FF), lambda i: (layer, sub, 0, 0)),
            _resident((None, None, D_MODEL, D_FF), lambda i: (layer, sub, 0, 0)),
            _resident((None, None, D_FF, D_MODEL), lambda i: (layer, sub, 0, 0)),
            _resident((1, D_MODEL), lambda i: (0, 0)),
        ],
        out_specs=pl.BlockSpec((tm, D_MODEL), lambda i: (i, 0)),
        out_shape=jax.ShapeDtypeStruct((SEQ, D_MODEL), F32),
        compiler_params=_params(1),
        name=f"ffn_l{layer}s{sub}",
    )(x, mod, norm_g, wg, wu, wd, fin)


def _conv_mix_kernel(x_ref, mod_ref, g_ref, win_ref, caw_ref, cab_ref, lng_ref, lnb_ref,
                     cbw_ref, wout_ref, seg_ref, o_ref, abuf, bbuf, *, tm):
    i = pl.program_id(0)

    @pl.when(i == 0)
    def _():
        abuf[0:A_HALO, :] = jnp.zeros((A_HALO, A_WIDTH), F32)
        bbuf[0:B_HALO, :] = jnp.zeros((B_HALO, B_WIDTH), F32)

    x = x_ref[...]
    mod = mod_ref[...]
    h = _norm_mod(x, g_ref[...], mod).astype(BF16)
    z = jnp.dot(h, win_ref[...], preferred_element_type=F32)

    a = z[:, 0:A_WIDTH] * _sigmoid(z[:, A_WIDTH:2 * A_WIDTH])
    abuf[A_HALO:A_HALO + tm, :] = a
    acc = jnp.zeros((tm, A_WIDTH), F32) + cab_ref[...]
    for k in range(A_CONV):
        acc = acc + caw_ref[k:k + 1, :] * abuf[pl.ds(A_HALO - (A_CONV - 1) + k, tm), :]
    abuf[0:A_HALO, :] = abuf[tm:tm + A_HALO, :]
    seg = seg_ref[...]
    mu = _split_dot(acc, seg)
    d = acc - mu
    var = _split_dot(d * d, seg)
    yn = d * lax.rsqrt(var + EPS) * lng_ref[...] + lnb_ref[...]
    a_out = yn * _sigmoid(yn)

    off = 2 * A_WIDTH
    cv = z[:, off + B_WIDTH:off + 2 * B_WIDTH] * z[:, off + 2 * B_WIDTH:off + 3 * B_WIDTH]
    bbuf[B_HALO:B_HALO + tm, :] = cv
    bacc = jnp.zeros((tm, B_WIDTH), F32)
    for k in range(B_CONV):
        bacc = bacc + cbw_ref[k:k + 1, :] * bbuf[pl.ds(B_HALO - (B_CONV - 1) + k, tm), :]
    bbuf[0:B_HALO, :] = bbuf[tm:tm + B_HALO, :]
    bb = z[:, off:off + B_WIDTH] * bacc

    y = (jnp.dot(a_out.astype(BF16), wout_ref[0:A_WIDTH, :], preferred_element_type=F32)
         + jnp.dot(bb.astype(BF16), wout_ref[A_WIDTH:A_WIDTH + B_WIDTH, :],
                   preferred_element_type=F32))
    o_ref[...] = x + mod[2:3] * y


def _conv_mix_call(x, mod, norm_g, w_in, caw, cab, lng, lnb, cbw, w_out, seg, *, layer, tm=512):
    even_in = w_in.shape[1]
    return pl.pallas_call(
        functools.partial(_conv_mix_kernel, tm=tm),
        grid=(SEQ // tm,),
        in_specs=[
            pl.BlockSpec((tm, D_MODEL), lambda i: (i, 0)),
            _resident((None, None, 3, D_MODEL), lambda i: (layer, 1, 0, 0)),
            _resident((None, 1, D_MODEL), lambda i: (layer, 0, 0)),
            _full((D_MODEL, even_in)),
            _full((A_CONV, A_WIDTH)),
            _full((1, A_WIDTH)),
            _full((1, A_WIDTH)),
            _full((1, A_WIDTH)),
            _full((B_CONV, B_WIDTH)),
            _full((A_WIDTH + B_WIDTH, D_MODEL)),
            _full((A_WIDTH, A_WIDTH)),
        ],
        out_specs=pl.BlockSpec((tm, D_MODEL), lambda i: (i, 0)),
        out_shape=jax.ShapeDtypeStruct((SEQ, D_MODEL), F32),
        scratch_shapes=[pltpu.VMEM((A_HALO + tm, A_WIDTH), F32),
                        pltpu.VMEM((B_HALO + tm, B_WIDTH), F32)],
        compiler_params=_params(1),
        name="conv_mixers",
    )(x, mod, norm_g, w_in, caw, cab, lng, lnb, cbw, w_out, seg)


def _pm_proj_kernel(x_ref, mod_ref, g_ref, wu_ref, wq_ref, wqt_ref, wkt_ref, wv_ref, pw_ref,
                    pb_ref, ps_ref, pooled_ref, qrow_ref, qt_ref, kt_ref, v_ref, km_ref,
                    ubuf, hmean, *, tm):
    i = pl.program_id(0)
    hp = D_HEADS * HEAD_PAD

    @pl.when(i == 0)
    def _():
        ubuf[0:POOL_HALO, :] = jnp.zeros((POOL_HALO, C_WIDTH), F32)

    x = x_ref[...]
    h = _norm_mod(x, g_ref[...], mod_ref[...]).astype(BF16)
    u = jnp.dot(h, wu_ref[...], preferred_element_type=F32)
    qr = jnp.dot(h, wq_ref[...], preferred_element_type=F32)
    vv = jnp.dot(h, wv_ref[...], preferred_element_type=F32)
    qt = lax.dot_general(wqt_ref[...], h, _NT, preferred_element_type=F32)
    kt = lax.dot_general(wkt_ref[...], h, _NT, preferred_element_type=F32)

    ubuf[POOL_HALO:POOL_HALO + tm, :] = u
    t1 = (i * tm + 1 + lax.broadcasted_iota(I32, (tm, C_GROUP_DIM), 0)).astype(F32)
    for gi, w in enumerate(POOL_WINDOWS):
        c0 = gi * C_GROUP_DIM
        s = u[:, c0:c0 + C_GROUP_DIM]
        for k in range(1, w):
            s = s + ubuf[pl.ds(POOL_HALO - k, tm), c0:c0 + C_GROUP_DIM]
        pooled = s / jnp.minimum(t1, float(w)) - u[:, c0:c0 + C_GROUP_DIM]
        mixed = (jnp.dot(pooled.astype(BF16), pw_ref[gi], preferred_element_type=F32)
                 + pb_ref[:, c0:c0 + C_GROUP_DIM])
        pooled_ref[:, c0:c0 + C_GROUP_DIM] = (
            mixed * ps_ref[:, c0:c0 + C_GROUP_DIM]).astype(BF16)
    ubuf[0:POOL_HALO, :] = ubuf[tm:tm + POOL_HALO, :]

    scale = HEAD_DIM ** -0.5
    lane = lax.broadcasted_iota(I32, (tm, hp), 1) % HEAD_PAD
    qrows = jnp.where(lane == ALIBI_COL, 1.0, qr * scale)
    vb = vv.astype(BF16)
    for hh in range(D_HEADS):
        qrow_ref[hh] = qrows[:, hh * HEAD_PAD:(hh + 1) * HEAD_PAD]
        v_ref[hh] = vb[:, hh * HEAD_PAD:(hh + 1) * HEAD_PAD]
    qt_ref[...] = (qt * scale).astype(BF16).reshape(D_HEADS, HEAD_PAD, tm)
    row = lax.broadcasted_iota(I32, (hp, tm), 0)
    head = row // HEAD_PAD
    slope = lax.bitcast_convert_type((126 - head) << 23, F32)
    koff = (lax.broadcasted_iota(I32, (hp, tm), 1) % MOBA_BLOCK).astype(F32)
    ktp = jnp.where(row % HEAD_PAD == ALIBI_COL, slope * koff, kt)
    kt_ref[...] = ktp.astype(BF16).reshape(D_HEADS, HEAD_PAD, tm)

    per_tile = tm // MOBA_BLOCK
    for sb in range(per_tile):
        blk = h[sb * MOBA_BLOCK:(sb + 1) * MOBA_BLOCK, :].astype(F32)
        hmean[pl.ds(i * per_tile + sb, 1), :] = jnp.mean(blk, axis=0, keepdims=True)

    @pl.when(i == pl.num_programs(0) - 1)
    def _():
        hi, lo = _split(hmean[...])
        km_ref[...] = (lax.dot_general(hi, wkt_ref[...], _NT, preferred_element_type=F32)
                       + lax.dot_general(lo, wkt_ref[...], _NT, preferred_element_type=F32))


def _pm_proj_call(x, mod, norm_g, wu, wq, wqt, wkt, wv, pw, pb, ps, *, layer, tm=512):
    hp = D_HEADS * HEAD_PAD
    return pl.pallas_call(
        functools.partial(_pm_proj_kernel, tm=tm),
        grid=(SEQ // tm,),
        in_specs=[
            pl.BlockSpec((tm, D_MODEL), lambda i: (i, 0)),
            _resident((None, None, 3, D_MODEL), lambda i: (layer, 1, 0, 0)),
            _resident((None, 1, D_MODEL), lambda i: (layer, 0, 0)),
            _full((D_MODEL, C_WIDTH)),
            _full((D_MODEL, hp)),
            _full((hp, D_MODEL)),
            _full((hp, D_MODEL)),
            _full((D_MODEL, hp)),
            _full((len(POOL_WINDOWS), C_GROUP_DIM, C_GROUP_DIM)),
            _full((1, C_WIDTH)),
            _full((1, C_WIDTH)),
        ],
        out_specs=[
            pl.BlockSpec((tm, C_WIDTH), lambda i: (i, 0)),
            pl.BlockSpec((D_HEADS, tm, HEAD_PAD), lambda i: (0, i, 0)),
            pl.BlockSpec((D_HEADS, HEAD_PAD, tm), lambda i: (0, 0, i)),
            pl.BlockSpec((D_HEADS, HEAD_PAD, tm), lambda i: (0, 0, i)),
            pl.BlockSpec((D_HEADS, tm, HEAD_PAD), lambda i: (0, i, 0)),
            pl.BlockSpec((N_BLOCKS, hp), lambda i: (0, 0)),
        ],
        out_shape=[
            jax.ShapeDtypeStruct((SEQ, C_WIDTH), BF16),
            jax.ShapeDtypeStruct((D_HEADS, SEQ, HEAD_PAD), F32),
            jax.ShapeDtypeStruct((D_HEADS, HEAD_PAD, SEQ), BF16),
            jax.ShapeDtypeStruct((D_HEADS, HEAD_PAD, SEQ), BF16),
            jax.ShapeDtypeStruct((D_HEADS, SEQ, HEAD_PAD), BF16),
            jax.ShapeDtypeStruct((N_BLOCKS, hp), F32),
        ],
        scratch_shapes=[pltpu.VMEM((POOL_HALO + tm, C_WIDTH), F32),
                        pltpu.VMEM((N_BLOCKS, D_MODEL), F32)],
        compiler_params=_params(1),
        name="pool_qkv_proj",
    )(x, mod, norm_g, wu, wq, wqt, wkt, wv, pw, pb, ps)


def _partial_rows(o, m, l, j):
    lane = lax.broadcasted_iota(I32, o.shape, 1)
    return jnp.where(lane == PART_M, m,
                     jnp.where(lane == PART_L, l, jnp.where(lane == PART_J, j, o)))


def _block_attention(q, kt, v, mask):
    s = jnp.dot(q, kt, preferred_element_type=F32)
    if mask is not None:
        s = jnp.where(mask, s, NEG_BIG)
    m = jnp.max(s, axis=1, keepdims=True)
    p = jnp.exp(s - m)
    l = jnp.sum(p, axis=1, keepdims=True)
    o = jnp.dot(p.astype(BF16), v, preferred_element_type=F32)
    return o, m, l


def _onehot_pairs(sel_rows):
    blk = lax.broadcasted_iota(I32, (N_BLOCKS, MOBA_BLOCK), 0)
    return jnp.concatenate(
        [jnp.where(blk == sel_rows[s], 1.0, 0.0) for s in range(MOBA_TOPK)], axis=1)


def _gate_own_kernel(qt_ref, qrow_ref, kt_ref, v_ref, km_ref, tri_ref,
                     sel_ref, rank_ref, cnt_ref, own_ref, *, heads):
    own = pl.program_id(1)
    blk = lax.broadcasted_iota(I32, (N_BLOCKS, MOBA_BLOCK), 0)
    qi = lax.broadcasted_iota(I32, (MOBA_BLOCK, MOBA_BLOCK), 0)
    ki = lax.broadcasted_iota(I32, (MOBA_BLOCK, MOBA_BLOCK), 1)
    causal = ki <= qi
    ones = jnp.ones((8, MOBA_TOPK * MOBA_BLOCK), BF16)
    own_f = jnp.zeros((MOBA_BLOCK, 1), F32) + own.astype(F32)
    for a in range(heads):
        gate = _split_dot(km_ref[a], qt_ref[a])
        g = jnp.where(blk < own, gate, -jnp.inf)
        sel_rows = []
        for s in range(MOBA_TOPK):
            best = jnp.max(g, axis=0, keepdims=True)
            idx = jnp.min(jnp.where(g == best, blk, N_BLOCKS), axis=0, keepdims=True)
            g = jnp.where(blk == idx, -jnp.inf, g)
            sel_rows.append(jnp.where(s < own, idx, N_BLOCKS))
        sel_ref[a] = jnp.concatenate(sel_rows, axis=0)

        onehot = _onehot_pairs(sel_rows)
        oh = onehot.astype(BF16)
        before = jnp.dot(oh, tri_ref[...], preferred_element_type=F32)
        rank = jnp.sum(onehot * before, axis=0, keepdims=True).astype(I32)
        rank_ref[a] = jnp.concatenate(
            [rank[:, s * MOBA_BLOCK:(s + 1) * MOBA_BLOCK] for s in range(MOBA_TOPK)], axis=0)
        cnt = lax.dot_general(ones, oh, _NT, preferred_element_type=F32)
        cnt_ref[a] = cnt[0:1, :].astype(I32)

        o, m, l = _block_attention(qrow_ref[a].astype(BF16), kt_ref[a], v_ref[a], causal)
        own_ref[a] = _partial_rows(o, m, l, own_f)


def _gate_own_call(qt, qrows, kt, v, kmean_h, tri, *, heads=GATE_HEADS_PER_STEP):
    return pl.pallas_call(
        functools.partial(_gate_own_kernel, heads=heads),
        grid=(D_HEADS // heads, N_BLOCKS),
        in_specs=[
            pl.BlockSpec((heads, HEAD_PAD, MOBA_BLOCK), lambda h, b: (h, 0, b)),
            pl.BlockSpec((heads, MOBA_BLOCK, HEAD_PAD), lambda h, b: (h, b, 0)),
            pl.BlockSpec((heads, HEAD_PAD, MOBA_BLOCK), lambda h, b: (h, 0, b)),
            pl.BlockSpec((heads, MOBA_BLOCK, HEAD_PAD), lambda h, b: (h, b, 0)),
            pl.BlockSpec((heads, N_BLOCKS, HEAD_PAD), lambda h, b: (h, 0, 0)),
            _full((MOBA_TOPK * MOBA_BLOCK, MOBA_TOPK * MOBA_BLOCK)),
        ],
        out_specs=[
            pl.BlockSpec((heads, MOBA_TOPK, MOBA_BLOCK), lambda h, b: (h, 0, b)),
            pl.BlockSpec((heads, MOBA_TOPK, MOBA_BLOCK), lambda h, b: (h, 0, b)),
            pl.BlockSpec((heads, None, 1, N_BLOCKS), lambda h, b: (h, b, 0, 0)),
            pl.BlockSpec((heads, MOBA_BLOCK, HEAD_PAD), lambda h, b: (h, b, 0)),
        ],
        out_shape=[
            jax.ShapeDtypeStruct((D_HEADS, MOBA_TOPK, SEQ), I32),
            jax.ShapeDtypeStruct((D_HEADS, MOBA_TOPK, SEQ), I32),
            jax.ShapeDtypeStruct((D_HEADS, N_BLOCKS, 1, N_BLOCKS), I32),
            jax.ShapeDtypeStruct((D_HEADS, SEQ, HEAD_PAD), F32),
        ],
        compiler_params=_params(2),
        name="moba_gate_own",
    )(qt, qrows, kt, v, kmean_h, tri)


def _route_pos_kernel(sel_ref, rank_ref, base_ref, pos_ref):
    lane = lax.broadcasted_iota(I32, (1, MOBA_BLOCK), 1)
    for a in range(D_HEADS):
        sel = sel_ref[a]
        onehot = _onehot_pairs([sel[s:s + 1, :] for s in range(MOBA_TOPK)]).astype(BF16)
        dig = jnp.dot(base_ref[a].astype(BF16), onehot, preferred_element_type=F32)
        base = (dig[0:1] * 16384.0 + dig[1:2] * 128.0 + dig[2:3]).astype(I32)
        rows = []
        for s in range(MOBA_TOPK):
            p = base[:, s * MOBA_BLOCK:(s + 1) * MOBA_BLOCK] + rank_ref[a, s:s + 1, :]
            rows.append(jnp.where(sel[s:s + 1, :] < N_BLOCKS, p, TRASH_ROW0 + lane % SC_WINDOW))
        pos_ref[a] = jnp.concatenate(rows, axis=0)


def _route_pos_call(sel, rank, base_digits):
    return pl.pallas_call(
        _route_pos_kernel,
        grid=(N_BLOCKS,),
        in_specs=[
            pl.BlockSpec((D_HEADS, MOBA_TOPK, MOBA_BLOCK), lambda b: (0, 0, b)),
            pl.BlockSpec((D_HEADS, MOBA_TOPK, MOBA_BLOCK), lambda b: (0, 0, b)),
            pl.BlockSpec((D_HEADS, None, 8, N_BLOCKS), lambda b: (0, b, 0, 0)),
        ],
        out_specs=pl.BlockSpec((D_HEADS, MOBA_TOPK, MOBA_BLOCK), lambda b: (0, 0, b)),
        out_shape=jax.ShapeDtypeStruct((D_HEADS, MOBA_TOPK, SEQ), I32),
        compiler_params=_params(1),
        name="moba_route_pos",
    )(sel, rank, base_digits)


def _route_tables(cnt):
    cnt = cnt.reshape(D_HEADS, N_BLOCKS, N_BLOCKS)
    tiles = (cnt.sum(axis=1) + ROUTE_TILE - 1) // ROUTE_TILE
    tiles_flat = tiles.reshape(N_GROUPS)
    tile0 = jnp.cumsum(tiles_flat) - tiles_flat
    n_tiles = tiles_flat.sum().astype(I32)
    within = jnp.cumsum(cnt, axis=1) - cnt
    base = tile0.reshape(D_HEADS, 1, N_BLOCKS) * ROUTE_TILE + within
    digits = jnp.stack([base // 16384, (base // 128) % 128, base % 128], axis=2).astype(F32)
    digits = jnp.pad(digits, ((0, 0), (0, 0), (0, 8 - 3), (0, 0)))
    group_of_tile = jnp.repeat(jnp.arange(N_GROUPS, dtype=I32), tiles_flat,
                               total_repeat_length=MAX_TILES)
    t = jnp.arange(MAX_TILES, dtype=I32)
    last = group_of_tile[jnp.maximum(n_tiles - 1, 0)]
    group_of_tile = jnp.where(t < n_tiles, group_of_tile, last)
    return digits, group_of_tile // N_BLOCKS, group_of_tile % N_BLOCKS, n_tiles.reshape(1)


def _sc_mesh():
    return plsc.VectorSubcoreMesh(core_axis_name="core", subcore_axis_name="subcore")


def _dispatch_call(rows, pos_by_slot):
    n_rows = rows.shape[0]

    @functools.partial(
        pl.kernel, mesh=_sc_mesh(), scratch_types=[],
        out_type=jax.ShapeDtypeStruct((ROUTE_ROWS, HEAD_PAD), F32))
    def dispatch(x_hbm, i0_hbm, i1_hbm, i2_hbm, o_hbm):
        def body(x_vmem, i0_vmem, i1_vmem, i2_vmem):
            pltpu.sync_copy(x_vmem, o_hbm.at[i0_vmem.at[0]])
            pltpu.sync_copy(x_vmem, o_hbm.at[i1_vmem.at[0]])
            pltpu.sync_copy(x_vmem, o_hbm.at[i2_vmem.at[0]])

        idx_spec = pl.BlockSpec((1, SC_WINDOW), lambda i: (0, i))
        pltpu.emit_pipeline(
            body,
            grid=(n_rows // SC_WINDOW,),
            in_specs=[pl.BlockSpec((SC_WINDOW, HEAD_PAD), lambda i: (i, 0)),
                      idx_spec, idx_spec, idx_spec],
            out_specs=[],
            core_axis_name=("core", "subcore"),
            dimension_semantics=(pltpu.PARALLEL,),
        )(x_hbm, i0_hbm, i1_hbm, i2_hbm)

    return dispatch(rows, *pos_by_slot)


def _collect_call(table, idx):
    n_rows = idx.shape[1]

    @functools.partial(
        pl.kernel, mesh=_sc_mesh(), scratch_types=[],
        out_type=jax.ShapeDtypeStruct((n_rows, HEAD_PAD), F32))
    def collect(x_hbm, i_hbm, o_hbm):
        def body(i_vmem, o_vmem):
            pltpu.sync_copy(x_hbm.at[i_vmem.at[0]], o_vmem)

        pltpu.emit_pipeline(
            body,
            grid=(n_rows // SC_WINDOW,),
            in_specs=[pl.BlockSpec((1, SC_WINDOW), lambda i: (0, i))],
            out_specs=[pl.BlockSpec((SC_WINDOW, HEAD_PAD), lambda i: (i, 0))],
            core_axis_name=("core", "subcore"),
            dimension_semantics=(pltpu.PARALLEL,),
        )(i_hbm, o_hbm)

    return collect(table, idx)


def _routed_attn_kernel(th_ref, tj_ref, nt_ref, q_ref, kt_ref, v_ref, o_ref):
    t = pl.program_id(0)

    @pl.when(t < nt_ref[0])
    def _():
        o, m, l = _block_attention(q_ref[...].astype(BF16), kt_ref[...], v_ref[...], None)
        j = jnp.zeros((ROUTE_TILE, 1), F32) + tj_ref[t].astype(F32)
        o_ref[...] = _partial_rows(o, m, l, j)


def _routed_attn_call(tile_h, tile_j, n_tiles, routed_q, kt, v):
    live = lambda t, th, tj, nt: (jnp.minimum(t, nt[0]), 0)
    grid_spec = pltpu.PrefetchScalarGridSpec(
        num_scalar_prefetch=3,
        grid=(MAX_TILES,),
        in_specs=[
            pl.BlockSpec((ROUTE_TILE, HEAD_PAD), live),
            pl.BlockSpec((None, HEAD_PAD, MOBA_BLOCK), lambda t, th, tj, nt: (th[t], 0, tj[t])),
            pl.BlockSpec((None, MOBA_BLOCK, HEAD_PAD), lambda t, th, tj, nt: (th[t], tj[t], 0)),
        ],
        out_specs=pl.BlockSpec((ROUTE_TILE, HEAD_PAD), live),
    )
    return pl.pallas_call(
        _routed_attn_kernel,
        grid_spec=grid_spec,
        out_shape=jax.ShapeDtypeStruct((ROUTE_ROWS, HEAD_PAD), F32),
        compiler_params=_params(1),
        name="moba_routed_attn",
    )(tile_h, tile_j, n_tiles, routed_q, kt, v)


def _pm_out_kernel(slope_ref, x_ref, mod_ref, pooled_ref, own_ref, got_ref, wp_ref, wa_ref,
                   o_ref):
    own = pl.program_id(0)
    own_f = own.astype(F32)
    y = jnp.dot(pooled_ref[...], wp_ref[...], preferred_element_type=F32)
    for hh in range(D_HEADS):
        step = slope_ref[hh] * MOBA_BLOCK
        parts = [own_ref[hh]]
        maxes = [parts[0][:, PART_M:PART_M + 1]]
        for s in range(MOBA_TOPK):
            valid = s < own
            part = jnp.where(valid, got_ref[hh, s], 0.0)
            dist = own_f - part[:, PART_J:PART_J + 1]
            parts.append(part)
            maxes.append(jnp.where(valid, part[:, PART_M:PART_M + 1] - step * dist, NEG_BIG))
        top = functools.reduce(jnp.maximum, maxes)
        num = jnp.zeros((MOBA_BLOCK, HEAD_PAD), F32)
        den = jnp.zeros((MOBA_BLOCK, 1), F32)
        for part, mx in zip(parts, maxes):
            w = jnp.exp(mx - top)
            num = num + w * part
            den = den + w * part[:, PART_L:PART_L + 1]
        att = (num / den).astype(BF16)
        y = y + jnp.dot(att, wa_ref[hh], preferred_element_type=F32)
    o_ref[...] = x_ref[...] + mod_ref[2:3, :] * y


def _pm_out_call(slopes, x, mod, pooled, own_parts, got_parts, wp, wa, *, layer):
    tm = MOBA_BLOCK
    return pl.pallas_call(
        _pm_out_kernel,
        grid=(SEQ // tm,),
        in_specs=[
            pl.BlockSpec(memory_space=pltpu.SMEM),
            pl.BlockSpec((tm, D_MODEL), lambda i: (i, 0)),
            _resident((None, None, 3, D_MODEL), lambda i: (layer, 1, 0, 0)),
            pl.BlockSpec((tm, C_WIDTH), lambda i: (i, 0)),
            pl.BlockSpec((D_HEADS, tm, HEAD_PAD), lambda i: (0, i, 0)),
            pl.BlockSpec((D_HEADS, MOBA_TOPK, tm, HEAD_PAD), lambda i: (0, 0, i, 0)),
            _full((C_WIDTH, D_MODEL)),
            _full((D_HEADS, HEAD_PAD, D_MODEL)),
        ],
        out_specs=pl.BlockSpec((tm, D_MODEL), lambda i: (i, 0)),
        out_shape=jax.ShapeDtypeStruct((SEQ, D_MODEL), F32),
        compiler_params=_params(1),
        name="pool_moba_out",
    )(slopes, x, mod, pooled, own_parts, got_parts, wp, wa)


def _head_pad(w, axis):
    shape = list(w.shape)
    shape[axis:axis + 1] = [D_HEADS, HEAD_DIM]
    pad = [(0, 0)] * len(shape)
    pad[axis + 1] = (0, HEAD_PAD - HEAD_DIM)
    out = jnp.pad(w.reshape(shape), pad)
    shape[axis:axis + 2] = [D_HEADS * HEAD_PAD]
    return out.reshape(shape)


def kernel(x, c, ada_w, ada_b, ffn_norm, ffn_w_gate, ffn_w_up, ffn_w_down, mix_norm, conv_w_in,
           conv_a_w, conv_a_b, conv_a_ln_g, conv_a_ln_b, conv_b_w, conv_w_out, pm_w_in, pool_w,
           pool_b, pool_scale, pm_w_out, final_norm):
    assert x.shape == (1, SEQ, D_MODEL) and c.shape == (1, D_MODEL)
    xs = x.reshape(SEQ, D_MODEL)
    mod = _mod_call(c, ada_w, ada_b)

    wg = ffn_w_gate.astype(BF16)
    wu = ffn_w_up.astype(BF16)
    wd = ffn_w_down.astype(BF16)
    ffn_g = ffn_norm.reshape(DEPTH, 2, 1, D_MODEL)
    mix_g = mix_norm.reshape(DEPTH, 1, D_MODEL)
    fin = final_norm.reshape(1, D_MODEL)
    ffn = functools.partial(_ffn_call, mod=mod, norm_g=ffn_g, wg=wg, wu=wu, wd=wd, fin=fin)

    xs = ffn(xs, layer=0, sub=0, final=False)
    grp = np.arange(A_WIDTH) // (A_WIDTH // A_GROUPS)
    seg = jnp.asarray((grp[:, None] == grp[None, :]) / (A_WIDTH // A_GROUPS), dtype=BF16)
    xs = _conv_mix_call(
        xs, mod, mix_g, conv_w_in[0].astype(BF16), conv_a_w[0], conv_a_b[0].reshape(1, A_WIDTH),
        conv_a_ln_g[0].reshape(1, A_WIDTH), conv_a_ln_b[0].reshape(1, A_WIDTH), conv_b_w[0],
        conv_w_out[0].astype(BF16), seg, layer=0)
    xs = ffn(xs, layer=0, sub=1, final=False)

    xs = ffn(xs, layer=1, sub=0, final=False)
    w_in = pm_w_in[0].astype(BF16)
    w_u = w_in[:, 0:C_WIDTH]
    w_q = _head_pad(w_in[:, C_WIDTH:C_WIDTH + D_WIDTH], 1)
    w_k = _head_pad(w_in[:, C_WIDTH + D_WIDTH:C_WIDTH + 2 * D_WIDTH], 1)
    w_v = _head_pad(w_in[:, C_WIDTH + 2 * D_WIDTH:C_WIDTH + 3 * D_WIDTH], 1)
    pooled, qrows, qt, kt, v, kmean = _pm_proj_call(
        xs, mod, mix_g, w_u, w_q, w_q.T, w_k.T, w_v, pool_w[0].astype(BF16),
        pool_b[0].reshape(1, C_WIDTH), pool_scale[0].reshape(1, C_WIDTH), layer=1)
    kmean_h = kmean.reshape(N_BLOCKS, D_HEADS, HEAD_PAD).transpose(1, 0, 2)

    n_pair = MOBA_TOPK * MOBA_BLOCK
    tri = jnp.asarray(np.arange(n_pair)[:, None] < np.arange(n_pair)[None, :], dtype=BF16)
    sel, rank, cnt, own_parts = _gate_own_call(qt, qrows, kt, v, kmean_h, tri)
    base_digits, tile_h, tile_j, n_tiles = _route_tables(cnt)
    pos = _route_pos_call(sel, rank, base_digits)

    pos_by_slot = [pos[:, s, :].reshape(1, D_HEADS * SEQ) for s in range(MOBA_TOPK)]
    routed_q = _dispatch_call(qrows.reshape(D_HEADS * SEQ, HEAD_PAD), pos_by_slot)
    routed_parts = _routed_attn_call(tile_h, tile_j, n_tiles, routed_q, kt, v)
    got = _collect_call(routed_parts, pos.reshape(1, N_PAIRS))
    got = got.reshape(D_HEADS, MOBA_TOPK, SEQ, HEAD_PAD)

    slopes = jnp.asarray(2.0 ** (-8.0 * np.arange(1, D_HEADS + 1) / D_HEADS), dtype=F32)
    w_out = pm_w_out[0].astype(BF16)
    w_att = _head_pad(w_out[C_WIDTH:], 0).reshape(D_HEADS, HEAD_PAD, D_MODEL)
    xs = _pm_out_call(slopes, xs, mod, pooled, own_parts, got, w_out[0:C_WIDTH], w_att, layer=1)
    xs = ffn(xs, layer=1, sub=1, final=True)
    return xs.reshape(1, SEQ, D_MODEL)
```

```python
import functools

import numpy as np
import jax
import jax.numpy as jnp
from jax import lax
from jax.experimental import pallas as pl
from jax.experimental.pallas import tpu as pltpu
from jax.experimental.pallas import tpu_sc as plsc

D_MODEL = 1024
SEQ = 16384
DEPTH = 2
N_SUBLAYERS = 3
D_FF = 2816
EPS = 1e-6

A_WIDTH = 512
A_GROUPS = 8
A_CONV = 31
B_WIDTH = 512
B_CONV = 3

POOL_WINDOWS = (2, 4, 8, 16)
C_WIDTH = 512
C_GROUP_DIM = 128
D_HEADS = 8
HEAD_DIM = 64
D_WIDTH = 512
MOBA_BLOCK = 256
MOBA_TOPK = 3
N_BLOCKS = SEQ // MOBA_BLOCK

V7X_VMEM_LIMIT_BYTES = 56 * 1024 * 1024

A_HALO = 32
B_HALO = 8
POOL_HALO = 16
NEG_BIG = -1e30

HEAD_PAD = 128
ALIBI_COL = HEAD_DIM
OWN_COL = HEAD_DIM + 1

GATE_HEADS_PER_STEP = 4
ROUTE_TILE = 256
ROUTE_TILES_PER_STEP = 8
N_PAIRS = D_HEADS * SEQ * MOBA_TOPK
N_GROUPS = D_HEADS * N_BLOCKS
MAX_TILES = N_PAIRS // ROUTE_TILE + N_GROUPS
TRASH_ROW0 = MAX_TILES * ROUTE_TILE
ROUTE_ROWS = (MAX_TILES + ROUTE_TILES_PER_STEP) * ROUTE_TILE
SC_WINDOW = 128

BF16 = jnp.bfloat16
F32 = jnp.float32
I32 = jnp.int32


def _params(n_axes):
    return pltpu.CompilerParams(
        dimension_semantics=("arbitrary",) * n_axes,
        vmem_limit_bytes=V7X_VMEM_LIMIT_BYTES)


def _resident(block_shape, index_map):
    return pl.BlockSpec(block_shape, index_map, pipeline_mode=pl.Buffered(1))


def _full(shape):
    return _resident(shape, lambda *_: (0,) * len(shape))


def _sigmoid(v):
    return 1.0 / (1.0 + jnp.exp(-v))


def _norm_mod(x, g, mod):
    r = lax.rsqrt(jnp.mean(x * x, axis=-1, keepdims=True) + EPS)
    return (x * r) * (g * (1.0 + mod[1:2])) + mod[0:1]


def _split(v):
    hi = v.astype(BF16)
    return hi, (v - hi.astype(F32)).astype(BF16)


def _split_dot(v, w):
    hi, lo = _split(v)
    return (jnp.dot(hi, w, preferred_element_type=F32)
            + jnp.dot(lo, w, preferred_element_type=F32))


_NT = (((1,), (1,)), ((), ()))


def _mod_kernel(c_ref, w_ref, b_ref, o_ref):
    c = c_ref[...]
    cond = c * _sigmoid(c)
    o_ref[0] = jnp.sum(w_ref[0] * cond, axis=0, keepdims=True) + b_ref[0]


def _mod_call(c, ada_w, ada_b):
    n_out = N_SUBLAYERS * 3 * D_MODEL
    tn = D_MODEL
    out = pl.pallas_call(
        _mod_kernel,
        grid=(DEPTH, n_out // tn),
        in_specs=[
            pl.BlockSpec((D_MODEL, 1), lambda l, j: (0, 0)),
            pl.BlockSpec((1, D_MODEL, tn), lambda l, j: (l, 0, j)),
            pl.BlockSpec((1, 1, tn), lambda l, j: (l, 0, j)),
        ],
        out_specs=pl.BlockSpec((1, 1, tn), lambda l, j: (l, 0, j)),
        out_shape=jax.ShapeDtypeStruct((DEPTH, 1, n_out), F32),
        compiler_params=_params(2),
        name="adaln_mod",
    )(c.reshape(D_MODEL, 1), ada_w, ada_b.reshape(DEPTH, 1, n_out))
    return out.reshape(DEPTH, N_SUBLAYERS, 3, D_MODEL)


def _ffn_kernel(x_ref, mod_ref, g_ref, wg_ref, wu_ref, wd_ref, fin_ref, o_ref, *, final):
    x = x_ref[...]
    mod = mod_ref[...]
    h = _norm_mod(x, g_ref[...], mod).astype(BF16)
    gate = jnp.dot(h, wg_ref[...], preferred_element_type=F32)
    up = jnp.dot(h, wu_ref[...], preferred_element_type=F32)
    act = (gate * _sigmoid(gate) * up).astype(BF16)
    y = jnp.dot(act, wd_ref[...], preferred_element_type=F32)
    xn = x + (0.5 * mod[2:3]) * y
    if final:
        r = lax.rsqrt(jnp.mean(xn * xn, axis=-1, keepdims=True) + EPS)
        xn = xn * r * fin_ref[...]
    o_ref[...] = xn


def _ffn_call(x, mod, norm_g, wg, wu, wd, fin, *, layer, sub, final, tm=512):
    mod_sub = 0 if sub == 0 else 2
    return pl.pallas_call(
        functools.partial(_ffn_kernel, final=final),
        grid=(SEQ // tm,),
        in_specs=[
            pl.BlockSpec((tm, D_MODEL), lambda i: (i, 0)),
            _resident((None, None, 3, D_MODEL), lambda i: (layer, mod_sub, 0, 0)),
            _resident((None, None, 1, D_MODEL), lambda i: (layer, sub, 0, 0)),
            _resident((None, None, D_MODEL, D_FF), lambda i: (layer, sub, 0, 0)),
            _resident((None, None, D_MODEL, D_FF), lambda i: (layer, sub, 0, 0)),
            _resident((None, None, D_FF, D_MODEL), lambda i: (layer, sub, 0, 0)),
            _resident((1, D_MODEL), lambda i: (0, 0)),
        ],
        out_specs=pl.BlockSpec((tm, D_MODEL), lambda i: (i, 0)),
        out_shape=jax.ShapeDtypeStruct((SEQ, D_MODEL), F32),
        compiler_params=_params(1),
        name=f"ffn_l{layer}s{sub}",
    )(x, mod, norm_g, wg, wu, wd, fin)


def _conv_mix_kernel(x_ref, mod_ref, g_ref, win_ref, caw_ref, cab_ref, lng_ref, lnb_ref,
                     cbw_ref, wout_ref, seg_ref, o_ref, abuf, bbuf, *, tm):
    i = pl.program_id(0)

    @pl.when(i == 0)
    def _():
        abuf[0:A_HALO, :] = jnp.zeros((A_HALO, A_WIDTH), F32)
        bbuf[0:B_HALO, :] = jnp.zeros((B_HALO, B_WIDTH), F32)

    x = x_ref[...]
    mod = mod_ref[...]
    h = _norm_mod(x, g_ref[...], mod).astype(BF16)
    z = jnp.dot(h, win_ref[...], preferred_element_type=F32)

    a = z[:, 0:A_WIDTH] * _sigmoid(z[:, A_WIDTH:2 * A_WIDTH])
    abuf[A_HALO:A_HALO + tm, :] = a
    acc = jnp.zeros((tm, A_WIDTH), F32) + cab_ref[...]
    for k in range(A_CONV):
        acc = acc + caw_ref[k:k + 1, :] * abuf[pl.ds(A_HALO - (A_CONV - 1) + k, tm), :]
    abuf[0:A_HALO, :] = abuf[tm:tm + A_HALO, :]
    seg = seg_ref[...]
    mu = _split_dot(acc, seg)
    d = acc - mu
    var = _split_dot(d * d, seg)
    yn = d * lax.rsqrt(var + EPS) * lng_ref[...] + lnb_ref[...]
    a_out = yn * _sigmoid(yn)

    off = 2 * A_WIDTH
    cv = z[:, off + B_WIDTH:off + 2 * B_WIDTH] * z[:, off + 2 * B_WIDTH:off + 3 * B_WIDTH]
    bbuf[B_HALO:B_HALO + tm, :] = cv
    bacc = jnp.zeros((tm, B_WIDTH), F32)
    for k in range(B_CONV):
        bacc = bacc + cbw_ref[k:k + 1, :] * bbuf[pl.ds(B_HALO - (B_CONV - 1) + k, tm), :]
    bbuf[0:B_HALO, :] = bbuf[tm:tm + B_HALO, :]
    bb = z[:, off:off + B_WIDTH] * bacc

    y = (jnp.dot(a_out.astype(BF16), wout_ref[0:A_WIDTH, :], preferred_element_type=F32)
         + jnp.dot(bb.astype(BF16), wout_ref[A_WIDTH:A_WIDTH + B_WIDTH, :],
                   preferred_element_type=F32))
    o_ref[...] = x + mod[2:3] * y


def _conv_mix_call(x, mod, norm_g, w_in, caw, cab, lng, lnb, cbw, w_out, seg, *, layer, tm=512):
    even_in = w_in.shape[1]
    return pl.pallas_call(
        functools.partial(_conv_mix_kernel, tm=tm),
        grid=(SEQ // tm,),
        in_specs=[
            pl.BlockSpec((tm, D_MODEL), lambda i: (i, 0)),
            _resident((None, None, 3, D_MODEL), lambda i: (layer, 1, 0, 0)),
            _resident((None, 1, D_MODEL), lambda i: (layer, 0, 0)),
            _full((D_MODEL, even_in)),
            _full((A_CONV, A_WIDTH)),
            _full((1, A_WIDTH)),
            _full((1, A_WIDTH)),
            _full((1, A_WIDTH)),
            _full((B_CONV, B_WIDTH)),
            _full((A_WIDTH + B_WIDTH, D_MODEL)),
            _full((A_WIDTH, A_WIDTH)),
        ],
        out_specs=pl.BlockSpec((tm, D_MODEL), lambda i: (i, 0)),
        out_shape=jax.ShapeDtypeStruct((SEQ, D_MODEL), F32),
        scratch_shapes=[pltpu.VMEM((A_HALO + tm, A_WIDTH), F32),
                        pltpu.VMEM((B_HALO + tm, B_WIDTH), F32)],
        compiler_params=_params(1),
        name="conv_mixers",
    )(x, mod, norm_g, w_in, caw, cab, lng, lnb, cbw, w_out, seg)


def _pm_proj_kernel(x_ref, mod_ref, g_ref, wu_ref, wq_ref, wkt_ref, wv_ref, pw_ref,
                    pb_ref, ps_ref, pooled_ref, qrow_ref, kt_ref, v_ref, km_ref,
                    ubuf, hmean, *, tm):
    i = pl.program_id(0)
    hp = D_HEADS * HEAD_PAD

    @pl.when(i == 0)
    def _():
        ubuf[0:POOL_HALO, :] = jnp.zeros((POOL_HALO, C_WIDTH), F32)

    x = x_ref[...]
    h = _norm_mod(x, g_ref[...], mod_ref[...]).astype(BF16)
    u = jnp.dot(h, wu_ref[...], preferred_element_type=F32)
    qr = jnp.dot(h, wq_ref[...], preferred_element_type=F32)
    vv = jnp.dot(h, wv_ref[...], preferred_element_type=F32)
    kt = lax.dot_general(wkt_ref[...], h, _NT, preferred_element_type=F32)

    ubuf[POOL_HALO:POOL_HALO + tm, :] = u
    t1 = (i * tm + 1 + lax.broadcasted_iota(I32, (tm, C_GROUP_DIM), 0)).astype(F32)
    for gi, w in enumerate(POOL_WINDOWS):
        c0 = gi * C_GROUP_DIM
        s = u[:, c0:c0 + C_GROUP_DIM]
        for k in range(1, w):
            s = s + ubuf[pl.ds(POOL_HALO - k, tm), c0:c0 + C_GROUP_DIM]
        pooled = s / jnp.minimum(t1, float(w)) - u[:, c0:c0 + C_GROUP_DIM]
        mixed = (jnp.dot(pooled.astype(BF16), pw_ref[gi], preferred_element_type=F32)
                 + pb_ref[:, c0:c0 + C_GROUP_DIM])
        pooled_ref[:, c0:c0 + C_GROUP_DIM] = (
            mixed * ps_ref[:, c0:c0 + C_GROUP_DIM]).astype(BF16)
    ubuf[0:POOL_HALO, :] = ubuf[tm:tm + POOL_HALO, :]

    scale = HEAD_DIM ** -0.5
    lane = lax.broadcasted_iota(I32, (tm, hp), 1) % HEAD_PAD
    own = ((i * tm + lax.broadcasted_iota(I32, (tm, hp), 0)) // MOBA_BLOCK).astype(F32)
    qrows = jnp.where(lane == ALIBI_COL, 1.0, jnp.where(lane == OWN_COL, own, qr * scale))
    vb = jnp.where(lane >= HEAD_DIM, 1.0, vv).astype(BF16)
    for hh in range(D_HEADS):
        qrow_ref[hh] = qrows[:, hh * HEAD_PAD:(hh + 1) * HEAD_PAD]
        v_ref[hh] = vb[:, hh * HEAD_PAD:(hh + 1) * HEAD_PAD]
    row = lax.broadcasted_iota(I32, (hp, tm), 0)
    head = row // HEAD_PAD
    slope = lax.bitcast_convert_type((126 - head) << 23, F32)
    koff = (lax.broadcasted_iota(I32, (hp, tm), 1) % MOBA_BLOCK).astype(F32)
    ktp = jnp.where(row % HEAD_PAD == ALIBI_COL, slope * koff, kt)
    kt_ref[...] = ktp.astype(BF16).reshape(D_HEADS, HEAD_PAD, tm)

    per_tile = tm // MOBA_BLOCK
    for sb in range(per_tile):
        blk = h[sb * MOBA_BLOCK:(sb + 1) * MOBA_BLOCK, :].astype(F32)
        hmean[pl.ds(i * per_tile + sb, 1), :] = jnp.mean(blk, axis=0, keepdims=True)

    @pl.when(i == pl.num_programs(0) - 1)
    def _():
        hi, lo = _split(hmean[...])
        km_ref[...] = (lax.dot_general(hi, wkt_ref[...], _NT, preferred_element_type=F32)
                       + lax.dot_general(lo, wkt_ref[...], _NT, preferred_element_type=F32))


def _pm_proj_call(x, mod, norm_g, wu, wq, wkt, wv, pw, pb, ps, *, layer, tm=512):
    hp = D_HEADS * HEAD_PAD
    return pl.pallas_call(
        functools.partial(_pm_proj_kernel, tm=tm),
        grid=(SEQ // tm,),
        in_specs=[
            pl.BlockSpec((tm, D_MODEL), lambda i: (i, 0)),
            _resident((None, None, 3, D_MODEL), lambda i: (layer, 1, 0, 0)),
            _resident((None, 1, D_MODEL), lambda i: (layer, 0, 0)),
            _full((D_MODEL, C_WIDTH)),
            _full((D_MODEL, hp)),
            _full((hp, D_MODEL)),
            _full((D_MODEL, hp)),
            _full((len(POOL_WINDOWS), C_GROUP_DIM, C_GROUP_DIM)),
            _full((1, C_WIDTH)),
            _full((1, C_WIDTH)),
        ],
        out_specs=[
            pl.BlockSpec((tm, C_WIDTH), lambda i: (i, 0)),
            pl.BlockSpec((D_HEADS, tm, HEAD_PAD), lambda i: (0, i, 0)),
            pl.BlockSpec((D_HEADS, HEAD_PAD, tm), lambda i: (0, 0, i)),
            pl.BlockSpec((D_HEADS, tm, HEAD_PAD), lambda i: (0, i, 0)),
            pl.BlockSpec((N_BLOCKS, hp), lambda i: (0, 0)),
        ],
        out_shape=[
            jax.ShapeDtypeStruct((SEQ, C_WIDTH), BF16),
            jax.ShapeDtypeStruct((D_HEADS, SEQ, HEAD_PAD), F32),
            jax.ShapeDtypeStruct((D_HEADS, HEAD_PAD, SEQ), BF16),
            jax.ShapeDtypeStruct((D_HEADS, SEQ, HEAD_PAD), BF16),
            jax.ShapeDtypeStruct((N_BLOCKS, hp), F32),
        ],
        scratch_shapes=[pltpu.VMEM((POOL_HALO + tm, C_WIDTH), F32),
                        pltpu.VMEM((N_BLOCKS, D_MODEL), F32)],
        compiler_params=_params(1),
        name="pool_qkv_proj",
    )(x, mod, norm_g, wu, wq, wkt, wv, pw, pb, ps)


def _block_rows(qs, kts, vs, mask, lse_shift):
    n = qs[0].shape[0]
    s = jnp.concatenate(
        [jnp.dot(q, kt, preferred_element_type=F32) for q, kt in zip(qs, kts)], axis=0)
    if mask is not None:
        s = jnp.where(mask, s, NEG_BIG)
    m = jnp.max(s, axis=1, keepdims=True)
    p = jnp.exp(s - m).astype(BF16)
    o = jnp.concatenate(
        [jnp.dot(p[u * n:(u + 1) * n], v, preferred_element_type=F32)
         for u, v in enumerate(vs)], axis=0)
    low = lax.broadcasted_iota(I32, o.shape, 1) < HEAD_DIM
    tot = jnp.where(low, pltpu.roll(o, HEAD_DIM, axis=1), o)
    col = m if lse_shift is None else m - lse_shift
    return jnp.where(low, o / tot, col + jnp.log(tot))


def _onehot_pairs(sel_rows):
    blk = lax.broadcasted_iota(I32, (N_BLOCKS, MOBA_BLOCK), 0)
    return jnp.concatenate(
        [jnp.where(blk == sel_rows[s], 1.0, 0.0) for s in range(MOBA_TOPK)], axis=1)


def _gate_own_kernel(qrow_ref, kt_ref, v_ref, km_ref, tri_ref,
                     sel_ref, rank_ref, cnt_ref, own_ref, *, heads):
    own = pl.program_id(1)
    n_pair = MOBA_TOPK * MOBA_BLOCK
    qs = [qrow_ref[a].astype(BF16) for a in range(heads)]

    gates = []
    for a in range(heads):
        km_hi, km_lo = _split(km_ref[a])
        gates.append(lax.dot_general(km_hi, qs[a], _NT, preferred_element_type=F32)
                     + lax.dot_general(km_lo, qs[a], _NT, preferred_element_type=F32))
    blk = lax.broadcasted_iota(I32, (N_BLOCKS, heads * MOBA_BLOCK), 0)
    g = jnp.where(blk < own, jnp.concatenate(gates, axis=1), -jnp.inf)
    slots = []
    for s in range(MOBA_TOPK):
        best = jnp.max(g, axis=0, keepdims=True)
        idx = jnp.min(jnp.where(g == best, blk, N_BLOCKS), axis=0, keepdims=True)
        g = jnp.where(blk == idx, -jnp.inf, g)
        slots.append(jnp.where(s < own, idx, N_BLOCKS))

    onehots = []
    for a in range(heads):
        sel_rows = [slot[:, a * MOBA_BLOCK:(a + 1) * MOBA_BLOCK] for slot in slots]
        sel_ref[a] = jnp.concatenate(sel_rows, axis=0)
        onehots.append(_onehot_pairs(sel_rows))
    onehot = jnp.concatenate(onehots, axis=0)
    oh = onehot.astype(BF16)
    before = jnp.dot(oh, tri_ref[...], preferred_element_type=F32)
    hit = onehot * before
    ones = jnp.ones((8, n_pair), BF16)
    cnt = lax.dot_general(ones, oh, _NT, preferred_element_type=F32)
    for a in range(heads):
        rank = jnp.sum(hit[a * N_BLOCKS:(a + 1) * N_BLOCKS], axis=0, keepdims=True).astype(I32)
        rank_ref[a] = jnp.concatenate(
            [rank[:, s * MOBA_BLOCK:(s + 1) * MOBA_BLOCK] for s in range(MOBA_TOPK)], axis=0)
        cnt_ref[a] = cnt[0:1, a * N_BLOCKS:(a + 1) * N_BLOCKS].astype(I32)

    qi = lax.broadcasted_iota(I32, (heads * MOBA_BLOCK, MOBA_BLOCK), 0) % MOBA_BLOCK
    ki = lax.broadcasted_iota(I32, (heads * MOBA_BLOCK, MOBA_BLOCK), 1)
    rows = _block_rows(qs, [kt_ref[a] for a in range(heads)], [v_ref[a] for a in range(heads)],
                       ki <= qi, None)
    for a in range(heads):
        own_ref[a] = rows[a * MOBA_BLOCK:(a + 1) * MOBA_BLOCK]


def _gate_own_call(qrows, kt, v, kmean_h, tri, *, heads=GATE_HEADS_PER_STEP):
    return pl.pallas_call(
        functools.partial(_gate_own_kernel, heads=heads),
        grid=(D_HEADS // heads, N_BLOCKS),
        in_specs=[
            pl.BlockSpec((heads, MOBA_BLOCK, HEAD_PAD), lambda h, b: (h, b, 0)),
            pl.BlockSpec((heads, HEAD_PAD, MOBA_BLOCK), lambda h, b: (h, 0, b)),
            pl.BlockSpec((heads, MOBA_BLOCK, HEAD_PAD), lambda h, b: (h, b, 0)),
            pl.BlockSpec((heads, N_BLOCKS, HEAD_PAD), lambda h, b: (h, 0, 0)),
            _full((MOBA_TOPK * MOBA_BLOCK, MOBA_TOPK * MOBA_BLOCK)),
        ],
        out_specs=[
            pl.BlockSpec((heads, MOBA_TOPK, MOBA_BLOCK), lambda h, b: (h, 0, b)),
            pl.BlockSpec((heads, MOBA_TOPK, MOBA_BLOCK), lambda h, b: (h, 0, b)),
            pl.BlockSpec((heads, None, 1, N_BLOCKS), lambda h, b: (h, b, 0, 0)),
            pl.BlockSpec((heads, MOBA_BLOCK, HEAD_PAD), lambda h, b: (h, b, 0)),
        ],
        out_shape=[
            jax.ShapeDtypeStruct((D_HEADS, MOBA_TOPK, SEQ), I32),
            jax.ShapeDtypeStruct((D_HEADS, MOBA_TOPK, SEQ), I32),
            jax.ShapeDtypeStruct((D_HEADS, N_BLOCKS, 1, N_BLOCKS), I32),
            jax.ShapeDtypeStruct((D_HEADS, SEQ, HEAD_PAD), F32),
        ],
        compiler_params=_params(2),
        name="moba_gate_own",
    )(qrows, kt, v, kmean_h, tri)


def _route_pos_kernel(sel_ref, rank_ref, base_ref, pos_ref):
    lane = lax.broadcasted_iota(I32, (1, MOBA_BLOCK), 1)
    for a in range(D_HEADS):
        sel = sel_ref[a]
        onehot = _onehot_pairs([sel[s:s + 1, :] for s in range(MOBA_TOPK)]).astype(BF16)
        dig = jnp.dot(base_ref[a].astype(BF16), onehot, preferred_element_type=F32)
        base = (dig[0:1] * 16384.0 + dig[1:2] * 128.0 + dig[2:3]).astype(I32)
        rows = []
        for s in range(MOBA_TOPK):
            p = base[:, s * MOBA_BLOCK:(s + 1) * MOBA_BLOCK] + rank_ref[a, s:s + 1, :]
            rows.append(jnp.where(sel[s:s + 1, :] < N_BLOCKS, p, TRASH_ROW0 + lane % SC_WINDOW))
        pos_ref[a] = jnp.concatenate(rows, axis=0)


def _route_pos_call(sel, rank, base_digits):
    return pl.pallas_call(
        _route_pos_kernel,
        grid=(N_BLOCKS,),
        in_specs=[
            pl.BlockSpec((D_HEADS, MOBA_TOPK, MOBA_BLOCK), lambda b: (0, 0, b)),
            pl.BlockSpec((D_HEADS, MOBA_TOPK, MOBA_BLOCK), lambda b: (0, 0, b)),
            pl.BlockSpec((D_HEADS, None, 8, N_BLOCKS), lambda b: (0, b, 0, 0)),
        ],
        out_specs=pl.BlockSpec((D_HEADS, MOBA_TOPK, MOBA_BLOCK), lambda b: (0, 0, b)),
        out_shape=jax.ShapeDtypeStruct((D_HEADS, MOBA_TOPK, SEQ), I32),
        compiler_params=_params(1),
        name="moba_route_pos",
    )(sel, rank, base_digits)


def _route_tables(cnt):
    cnt = cnt.reshape(D_HEADS, N_BLOCKS, N_BLOCKS)
    tiles = (cnt.sum(axis=1) + ROUTE_TILE - 1) // ROUTE_TILE
    tiles_flat = tiles.reshape(N_GROUPS)
    tile0 = jnp.cumsum(tiles_flat) - tiles_flat
    n_tiles = tiles_flat.sum().astype(I32)
    within = jnp.cumsum(cnt, axis=1) - cnt
    base = tile0.reshape(D_HEADS, 1, N_BLOCKS) * ROUTE_TILE + within
    digits = jnp.stack([base // 16384, (base // 128) % 128, base % 128], axis=2).astype(F32)
    digits = jnp.pad(digits, ((0, 0), (0, 0), (0, 8 - 3), (0, 0)))
    group_of_tile = jnp.repeat(jnp.arange(N_GROUPS, dtype=I32), tiles_flat,
                               total_repeat_length=MAX_TILES)
    t = jnp.arange(MAX_TILES, dtype=I32)
    last = group_of_tile[jnp.maximum(n_tiles - 1, 0)]
    group_of_tile = jnp.where(t < n_tiles, group_of_tile, last)
    tile_h = group_of_tile // N_BLOCKS
    tile_step = jnp.asarray(2.0 ** (-8.0 * np.arange(1, D_HEADS + 1) / D_HEADS) * MOBA_BLOCK,
                            dtype=F32)[tile_h]
    return digits, tile_h, group_of_tile % N_BLOCKS, n_tiles.reshape(1), tile_step


def _sc_mesh():
    return plsc.VectorSubcoreMesh(core_axis_name="core", subcore_axis_name="subcore")


def _dispatch_call(rows, pos_by_slot):
    n_rows = rows.shape[0]

    @functools.partial(
        pl.kernel, mesh=_sc_mesh(), scratch_types=[],
        out_type=jax.ShapeDtypeStruct((ROUTE_ROWS, HEAD_PAD), F32))
    def dispatch(x_hbm, i0_hbm, i1_hbm, i2_hbm, o_hbm):
        def body(x_vmem, i0_vmem, i1_vmem, i2_vmem):
            pltpu.sync_copy(x_vmem, o_hbm.at[i0_vmem.at[0]])
            pltpu.sync_copy(x_vmem, o_hbm.at[i1_vmem.at[0]])
            pltpu.sync_copy(x_vmem, o_hbm.at[i2_vmem.at[0]])

        idx_spec = pl.BlockSpec((1, SC_WINDOW), lambda i: (0, i))
        pltpu.emit_pipeline(
            body,
            grid=(n_rows // SC_WINDOW,),
            in_specs=[pl.BlockSpec((SC_WINDOW, HEAD_PAD), lambda i: (i, 0)),
                      idx_spec, idx_spec, idx_spec],
            out_specs=[],
            core_axis_name=("core", "subcore"),
            dimension_semantics=(pltpu.PARALLEL,),
        )(x_hbm, i0_hbm, i1_hbm, i2_hbm)

    return dispatch(rows, *pos_by_slot)


def _collect_call(table, idx):
    n_rows = idx.shape[1]

    @functools.partial(
        pl.kernel, mesh=_sc_mesh(), scratch_types=[],
        out_type=jax.ShapeDtypeStruct((n_rows, HEAD_PAD), F32))
    def collect(x_hbm, i_hbm, o_hbm):
        def body(i_vmem, o_vmem):
            pltpu.sync_copy(x_hbm.at[i_vmem.at[0]], o_vmem)

        pltpu.emit_pipeline(
            body,
            grid=(n_rows // SC_WINDOW,),
            in_specs=[pl.BlockSpec((1, SC_WINDOW), lambda i: (0, i))],
            out_specs=[pl.BlockSpec((SC_WINDOW, HEAD_PAD), lambda i: (i, 0))],
            core_axis_name=("core", "subcore"),
            dimension_semantics=(pltpu.PARALLEL,),
        )(i_hbm, o_hbm)

    return collect(table, idx)


def _routed_attn_kernel(th_ref, tj_ref, nt_ref, step_ref, q_ref, *refs, tiles):
    kt_refs, v_refs, o_ref = refs[0:tiles], refs[tiles:2 * tiles], refs[2 * tiles]
    t0 = pl.program_id(0) * tiles

    @pl.when(t0 < nt_ref[0])
    def _():
        qs, shifts = [], []
        for u in range(tiles):
            q = q_ref[u * ROUTE_TILE:(u + 1) * ROUTE_TILE, :]
            dist = q[:, OWN_COL:OWN_COL + 1] - tj_ref[t0 + u].astype(F32)
            shifts.append(step_ref[t0 + u] * dist)
            qs.append(q.astype(BF16))
        o_ref[...] = _block_rows(qs, [r[...] for r in kt_refs], [r[...] for r in v_refs], None,
                                 jnp.concatenate(shifts, axis=0))


def _routed_attn_call(tile_h, tile_j, n_tiles, tile_step, routed_q, kt, v,
                      *, tiles=ROUTE_TILES_PER_STEP):
    live = lambda s, th, tj, nt: (jnp.minimum(s, (nt[0] + tiles - 1) // tiles), 0)

    def kt_spec(u):
        return pl.BlockSpec((None, HEAD_PAD, MOBA_BLOCK),
                            lambda s, th, tj, nt: (th[s * tiles + u], 0, tj[s * tiles + u]))

    def v_spec(u):
        return pl.BlockSpec((None, MOBA_BLOCK, HEAD_PAD),
                            lambda s, th, tj, nt: (th[s * tiles + u], tj[s * tiles + u], 0))

    grid_spec = pltpu.PrefetchScalarGridSpec(
        num_scalar_prefetch=3,
        grid=(MAX_TILES // tiles,),
        in_specs=([pl.BlockSpec(memory_space=pltpu.SMEM),
                   pl.BlockSpec((tiles * ROUTE_TILE, HEAD_PAD), live)]
                  + [kt_spec(u) for u in range(tiles)] + [v_spec(u) for u in range(tiles)]),
        out_specs=pl.BlockSpec((tiles * ROUTE_TILE, HEAD_PAD), live),
    )
    return pl.pallas_call(
        functools.partial(_routed_attn_kernel, tiles=tiles),
        grid_spec=grid_spec,
        out_shape=jax.ShapeDtypeStruct((ROUTE_ROWS, HEAD_PAD), F32),
        compiler_params=_params(1),
        name="moba_routed_attn",
    )(tile_h, tile_j, n_tiles, tile_step, routed_q, *([kt] * tiles), *([v] * tiles))


def _pm_out_kernel(x_ref, mod_ref, pooled_ref, own_ref, got_ref, wp_ref, wa_ref, o_ref):
    own = pl.program_id(0)
    low = lax.broadcasted_iota(I32, (MOBA_BLOCK, HEAD_PAD), 1) < HEAD_DIM

    def lse_of(part):
        return jnp.where(low, pltpu.roll(part, HEAD_DIM, axis=1), part)

    y = jnp.dot(pooled_ref[...], wp_ref[...], preferred_element_type=F32)
    for hh in range(D_HEADS):
        parts = [own_ref[hh]]
        lses = [lse_of(parts[0])]
        for s in range(MOBA_TOPK):
            valid = s < own
            part = got_ref[hh, s]
            parts.append(jnp.where(valid, part, 0.0))
            lses.append(jnp.where(valid, lse_of(part), NEG_BIG))
        top = functools.reduce(jnp.maximum, lses)
        num = jnp.zeros((MOBA_BLOCK, HEAD_PAD), F32)
        den = jnp.zeros((MOBA_BLOCK, HEAD_PAD), F32)
        for part, ls in zip(parts, lses):
            w = jnp.exp(ls - top)
            num = num + w * part
            den = den + w
        att = (num / den).astype(BF16)
        y = y + jnp.dot(att, wa_ref[hh], preferred_element_type=F32)
    o_ref[...] = x_ref[...] + mod_ref[2:3, :] * y


def _pm_out_call(x, mod, pooled, own_parts, got_parts, wp, wa, *, layer):
    tm = MOBA_BLOCK
    return pl.pallas_call(
        _pm_out_kernel,
        grid=(SEQ // tm,),
        in_specs=[
            pl.BlockSpec((tm, D_MODEL), lambda i: (i, 0)),
            _resident((None, None, 3, D_MODEL), lambda i: (layer, 1, 0, 0)),
            pl.BlockSpec((tm, C_WIDTH), lambda i: (i, 0)),
            pl.BlockSpec((D_HEADS, tm, HEAD_PAD), lambda i: (0, i, 0)),
            pl.BlockSpec((D_HEADS, MOBA_TOPK, tm, HEAD_PAD), lambda i: (0, 0, i, 0)),
            _full((C_WIDTH, D_MODEL)),
            _full((D_HEADS, HEAD_PAD, D_MODEL)),
        ],
        out_specs=pl.BlockSpec((tm, D_MODEL), lambda i: (i, 0)),
        out_shape=jax.ShapeDtypeStruct((SEQ, D_MODEL), F32),
        compiler_params=_params(1),
        name="pool_moba_out",
    )(x, mod, pooled, own_parts, got_parts, wp, wa)


def _head_pad(w, axis):
    shape = list(w.shape)
    shape[axis:axis + 1] = [D_HEADS, HEAD_DIM]
    pad = [(0, 0)] * len(shape)
    pad[axis + 1] = (0, HEAD_PAD - HEAD_DIM)
    out = jnp.pad(w.reshape(shape), pad)
    shape[axis:axis + 2] = [D_HEADS * HEAD_PAD]
    return out.reshape(shape)


def kernel(x, c, ada_w, ada_b, ffn_norm, ffn_w_gate, ffn_w_up, ffn_w_down, mix_norm, conv_w_in,
           conv_a_w, conv_a_b, conv_a_ln_g, conv_a_ln_b, conv_b_w, conv_w_out, pm_w_in, pool_w,
           pool_b, pool_scale, pm_w_out, final_norm):
    assert x.shape == (1, SEQ, D_MODEL) and c.shape == (1, D_MODEL)
    xs = x.reshape(SEQ, D_MODEL)
    mod = _mod_call(c, ada_w, ada_b)

    wg = ffn_w_gate.astype(BF16)
    wu = ffn_w_up.astype(BF16)
    wd = ffn_w_down.astype(BF16)
    ffn_g = ffn_norm.reshape(DEPTH, 2, 1, D_MODEL)
    mix_g = mix_norm.reshape(DEPTH, 1, D_MODEL)
    fin = final_norm.reshape(1, D_MODEL)
    ffn = functools.partial(_ffn_call, mod=mod, norm_g=ffn_g, wg=wg, wu=wu, wd=wd, fin=fin)

    xs = ffn(xs, layer=0, sub=0, final=False)
    grp = np.arange(A_WIDTH) // (A_WIDTH // A_GROUPS)
    seg = jnp.asarray((grp[:, None] == grp[None, :]) / (A_WIDTH // A_GROUPS), dtype=BF16)
    xs = _conv_mix_call(
        xs, mod, mix_g, conv_w_in[0].astype(BF16), conv_a_w[0], conv_a_b[0].reshape(1, A_WIDTH),
        conv_a_ln_g[0].reshape(1, A_WIDTH), conv_a_ln_b[0].reshape(1, A_WIDTH), conv_b_w[0],
        conv_w_out[0].astype(BF16), seg, layer=0)
    xs = ffn(xs, layer=0, sub=1, final=False)

    xs = ffn(xs, layer=1, sub=0, final=False)
    w_in = pm_w_in[0].astype(BF16)
    w_u = w_in[:, 0:C_WIDTH]
    w_q = _head_pad(w_in[:, C_WIDTH:C_WIDTH + D_WIDTH], 1)
    w_k = _head_pad(w_in[:, C_WIDTH + D_WIDTH:C_WIDTH + 2 * D_WIDTH], 1)
    w_v = _head_pad(w_in[:, C_WIDTH + 2 * D_WIDTH:C_WIDTH + 3 * D_WIDTH], 1)
    pooled, qrows, kt, v, kmean = _pm_proj_call(
        xs, mod, mix_g, w_u, w_q, w_k.T, w_v, pool_w[0].astype(BF16),
        pool_b[0].reshape(1, C_WIDTH), pool_scale[0].reshape(1, C_WIDTH), layer=1)
    kmean_h = kmean.reshape(N_BLOCKS, D_HEADS, HEAD_PAD).transpose(1, 0, 2)

    n_pair = MOBA_TOPK * MOBA_BLOCK
    tri = jnp.asarray(np.arange(n_pair)[:, None] < np.arange(n_pair)[None, :], dtype=BF16)
    sel, rank, cnt, own_parts = _gate_own_call(qrows, kt, v, kmean_h, tri)
    base_digits, tile_h, tile_j, n_tiles, tile_step = _route_tables(cnt)
    pos = _route_pos_call(sel, rank, base_digits)

    pos_by_slot = [pos[:, s, :].reshape(1, D_HEADS * SEQ) for s in range(MOBA_TOPK)]
    routed_q = _dispatch_call(qrows.reshape(D_HEADS * SEQ, HEAD_PAD), pos_by_slot)
    routed_parts = _routed_attn_call(tile_h, tile_j, n_tiles, tile_step, routed_q, kt, v)
    got = _collect_call(routed_parts, pos.reshape(1, N_PAIRS))
    got = got.reshape(D_HEADS, MOBA_TOPK, SEQ, HEAD_PAD)

    w_out = pm_w_out[0].astype(BF16)
    w_att = _head_pad(w_out[C_WIDTH:], 0).reshape(D_HEADS, HEAD_PAD, D_MODEL)
    xs = _pm_out_call(xs, mod, pooled, own_parts, got, w_out[0:C_WIDTH], w_att, layer=1)
    xs = ffn(xs, layer=1, sub=1, final=True)
    return xs.reshape(1, SEQ, D_MODEL)
```

```python
import functools

import numpy as np
import jax
import jax.numpy as jnp
from jax import lax
from jax.experimental import pallas as pl
from jax.experimental.pallas import tpu as pltpu
from jax.experimental.pallas import tpu_sc as plsc

D_MODEL = 1024
SEQ = 16384
DEPTH = 2
N_SUBLAYERS = 3
D_FF = 2816
EPS = 1e-6

A_WIDTH = 512
A_GROUPS = 8
A_CONV = 31
B_WIDTH = 512
B_CONV = 3

POOL_WINDOWS = (2, 4, 8, 16)
C_WIDTH = 512
C_GROUP_DIM = 128
D_HEADS = 8
HEAD_DIM = 64
D_WIDTH = 512
MOBA_BLOCK = 256
MOBA_TOPK = 3
N_BLOCKS = SEQ // MOBA_BLOCK

V7X_VMEM_LIMIT_BYTES = 56 * 1024 * 1024

A_HALO = 32
B_HALO = 8
POOL_HALO = 16
NEG_BIG = -1e30

HEAD_PAD = 128
ALIBI_COL = HEAD_DIM

GATE_HEADS_PER_STEP = 4
ROUTE_TILE = 256
ROUTE_TILES_PER_STEP = 8
N_PAIRS = D_HEADS * SEQ * MOBA_TOPK
N_GROUPS = D_HEADS * N_BLOCKS
MAX_TILES = N_PAIRS // ROUTE_TILE + N_GROUPS
TRASH_ROW0 = MAX_TILES * ROUTE_TILE
ROUTE_ROWS = (MAX_TILES + ROUTE_TILES_PER_STEP) * ROUTE_TILE
SC_WINDOW = 128

BF16 = jnp.bfloat16
F32 = jnp.float32
I32 = jnp.int32


def _params(n_axes):
    return pltpu.CompilerParams(
        dimension_semantics=("arbitrary",) * n_axes,
        vmem_limit_bytes=V7X_VMEM_LIMIT_BYTES)


def _resident(block_shape, index_map):
    return pl.BlockSpec(block_shape, index_map, pipeline_mode=pl.Buffered(1))


def _full(shape):
    return _resident(shape, lambda *_: (0,) * len(shape))


def _sigmoid(v):
    return 1.0 / (1.0 + jnp.exp(-v))


def _norm_mod(x, g, mod):
    r = lax.rsqrt(jnp.mean(x * x, axis=-1, keepdims=True) + EPS)
    return (x * r) * (g * (1.0 + mod[1:2])) + mod[0:1]


def _split(v):
    hi = v.astype(BF16)
    return hi, (v - hi.astype(F32)).astype(BF16)


def _split_dot(v, w):
    hi, lo = _split(v)
    return (jnp.dot(hi, w, preferred_element_type=F32)
            + jnp.dot(lo, w, preferred_element_type=F32))


_NT = (((1,), (1,)), ((), ()))


def _mod_kernel(c_ref, w_ref, b_ref, o_ref):
    c = c_ref[...]
    cond = c * _sigmoid(c)
    o_ref[0] = jnp.sum(w_ref[0] * cond, axis=0, keepdims=True) + b_ref[0]


def _mod_call(c, ada_w, ada_b):
    n_out = N_SUBLAYERS * 3 * D_MODEL
    tn = D_MODEL
    out = pl.pallas_call(
        _mod_kernel,
        grid=(DEPTH, n_out // tn),
        in_specs=[
            pl.BlockSpec((D_MODEL, 1), lambda l, j: (0, 0)),
            pl.BlockSpec((1, D_MODEL, tn), lambda l, j: (l, 0, j)),
            pl.BlockSpec((1, 1, tn), lambda l, j: (l, 0, j)),
        ],
        out_specs=pl.BlockSpec((1, 1, tn), lambda l, j: (l, 0, j)),
        out_shape=jax.ShapeDtypeStruct((DEPTH, 1, n_out), F32),
        compiler_params=_params(2),
        name="adaln_mod",
    )(c.reshape(D_MODEL, 1), ada_w, ada_b.reshape(DEPTH, 1, n_out))
    return out.reshape(DEPTH, N_SUBLAYERS, 3, D_MODEL)


def _ffn_kernel(x_ref, mod_ref, g_ref, wg_ref, wu_ref, wd_ref, fin_ref, o_ref, *, final):
    x = x_ref[...]
    mod = mod_ref[...]
    h = _norm_mod(x, g_ref[...], mod).astype(BF16)
    gate = jnp.dot(h, wg_ref[...], preferred_element_type=F32)
    up = jnp.dot(h, wu_ref[...], preferred_element_type=F32)
    act = (gate * _sigmoid(gate) * up).astype(BF16)
    y = jnp.dot(act, wd_ref[...], preferred_element_type=F32)
    xn = x + (0.5 * mod[2:3]) * y
    if final:
        r = lax.rsqrt(jnp.mean(xn * xn, axis=-1, keepdims=True) + EPS)
        xn = xn * r * fin_ref[...]
    o_ref[...] = xn


def _ffn_call(x, mod, norm_g, wg, wu, wd, fin, *, layer, sub, final, tm=512):
    mod_sub = 0 if sub == 0 else 2
    return pl.pallas_call(
        functools.partial(_ffn_kernel, final=final),
        grid=(SEQ // tm,),
        in_specs=[
            pl.BlockSpec((tm, D_MODEL), lambda i: (i, 0)),
            _resident((None, None, 3, D_MODEL), lambda i: (layer, mod_sub, 0, 0)),
            _resident((None, None, 1, D_MODEL), lambda i: (layer, sub, 0, 0)),
            _resident((None, None, D_MODEL, D_FF), lambda i: (layer, sub, 0, 0)),
            _resident((None, None, D_MODEL, D_FF), lambda i: (layer, sub, 0, 0)),
            _resident((None, None, D_FF, D_MODEL), lambda i: (layer, sub, 0, 0)),
            _resident((1, D_MODEL), lambda i: (0, 0)),
        ],
        out_specs=pl.BlockSpec((tm, D_MODEL), lambda i: (i, 0)),
        out_shape=jax.ShapeDtypeStruct((SEQ, D_MODEL), F32),
        compiler_params=_params(1),
        name=f"ffn_l{layer}s{sub}",
    )(x, mod, norm_g, wg, wu, wd, fin)


def _conv_mix_kernel(x_ref, mod_ref, g_ref, win_ref, caw_ref, cab_ref, lng_ref, lnb_ref,
                     cbw_ref, wout_ref, seg_ref, o_ref, abuf, bbuf, *, tm):
    i = pl.program_id(0)

    @pl.when(i == 0)
    def _():
        abuf[0:A_HALO, :] = jnp.zeros((A_HALO, A_WIDTH), F32)
        bbuf[0:B_HALO, :] = jnp.zeros((B_HALO, B_WIDTH), F32)

    x = x_ref[...]
    mod = mod_ref[...]
    h = _norm_mod(x, g_ref[...], mod).astype(BF16)
    z = jnp.dot(h, win_ref[...], preferred_element_type=F32)

    a = z[:, 0:A_WIDTH] * _sigmoid(z[:, A_WIDTH:2 * A_WIDTH])
    abuf[A_HALO:A_HALO + tm, :] = a
    acc = jnp.zeros((tm, A_WIDTH), F32) + cab_ref[...]
    for k in range(A_CONV):
        acc = acc + caw_ref[k:k + 1, :] * abuf[pl.ds(A_HALO - (A_CONV - 1) + k, tm), :]
    abuf[0:A_HALO, :] = abuf[tm:tm + A_HALO, :]
    seg = seg_ref[...]
    mu = _split_dot(acc, seg)
    d = acc - mu
    var = _split_dot(d * d, seg)
    yn = d * lax.rsqrt(var + EPS) * lng_ref[...] + lnb_ref[...]
    a_out = yn * _sigmoid(yn)

    off = 2 * A_WIDTH
    cv = z[:, off + B_WIDTH:off + 2 * B_WIDTH] * z[:, off + 2 * B_WIDTH:off + 3 * B_WIDTH]
    bbuf[B_HALO:B_HALO + tm, :] = cv
    bacc = jnp.zeros((tm, B_WIDTH), F32)
    for k in range(B_CONV):
        bacc = bacc + cbw_ref[k:k + 1, :] * bbuf[pl.ds(B_HALO - (B_CONV - 1) + k, tm), :]
    bbuf[0:B_HALO, :] = bbuf[tm:tm + B_HALO, :]
    bb = z[:, off:off + B_WIDTH] * bacc

    y = (jnp.dot(a_out.astype(BF16), wout_ref[0:A_WIDTH, :], preferred_element_type=F32)
         + jnp.dot(bb.astype(BF16), wout_ref[A_WIDTH:A_WIDTH + B_WIDTH, :],
                   preferred_element_type=F32))
    o_ref[...] = x + mod[2:3] * y


def _conv_mix_call(x, mod, norm_g, w_in, caw, cab, lng, lnb, cbw, w_out, seg, *, layer, tm=512):
    even_in = w_in.shape[1]
    return pl.pallas_call(
        functools.partial(_conv_mix_kernel, tm=tm),
        grid=(SEQ // tm,),
        in_specs=[
            pl.BlockSpec((tm, D_MODEL), lambda i: (i, 0)),
            _resident((None, None, 3, D_MODEL), lambda i: (layer, 1, 0, 0)),
            _resident((None, 1, D_MODEL), lambda i: (layer, 0, 0)),
            _full((D_MODEL, even_in)),
            _full((A_CONV, A_WIDTH)),
            _full((1, A_WIDTH)),
            _full((1, A_WIDTH)),
            _full((1, A_WIDTH)),
            _full((B_CONV, B_WIDTH)),
            _full((A_WIDTH + B_WIDTH, D_MODEL)),
            _full((A_WIDTH, A_WIDTH)),
        ],
        out_specs=pl.BlockSpec((tm, D_MODEL), lambda i: (i, 0)),
        out_shape=jax.ShapeDtypeStruct((SEQ, D_MODEL), F32),
        scratch_shapes=[pltpu.VMEM((A_HALO + tm, A_WIDTH), F32),
                        pltpu.VMEM((B_HALO + tm, B_WIDTH), F32)],
        compiler_params=_params(1),
        name="conv_mixers",
    )(x, mod, norm_g, w_in, caw, cab, lng, lnb, cbw, w_out, seg)


def _pm_proj_kernel(x_ref, mod_ref, g_ref, wu_ref, wq_ref, wkvt_ref, pw_ref,
                    pb_ref, ps_ref, pooled_ref, qrow_ref, kv_ref, km_ref,
                    ubuf, hmean, *, tm):
    i = pl.program_id(0)
    hp = D_HEADS * HEAD_PAD

    @pl.when(i == 0)
    def _():
        ubuf[0:POOL_HALO, :] = jnp.zeros((POOL_HALO, C_WIDTH), F32)

    x = x_ref[...]
    h = _norm_mod(x, g_ref[...], mod_ref[...]).astype(BF16)
    u = jnp.dot(h, wu_ref[...], preferred_element_type=F32)
    qr = jnp.dot(h, wq_ref[...], preferred_element_type=F32)
    kvt = lax.dot_general(wkvt_ref[...], h, _NT, preferred_element_type=F32)

    ubuf[POOL_HALO:POOL_HALO + tm, :] = u
    t1 = (i * tm + 1 + lax.broadcasted_iota(I32, (tm, C_GROUP_DIM), 0)).astype(F32)
    for gi, w in enumerate(POOL_WINDOWS):
        c0 = gi * C_GROUP_DIM
        s = u[:, c0:c0 + C_GROUP_DIM]
        for k in range(1, w):
            s = s + ubuf[pl.ds(POOL_HALO - k, tm), c0:c0 + C_GROUP_DIM]
        pooled = s / jnp.minimum(t1, float(w)) - u[:, c0:c0 + C_GROUP_DIM]
        mixed = (jnp.dot(pooled.astype(BF16), pw_ref[gi], preferred_element_type=F32)
                 + pb_ref[:, c0:c0 + C_GROUP_DIM])
        pooled_ref[:, c0:c0 + C_GROUP_DIM] = (
            mixed * ps_ref[:, c0:c0 + C_GROUP_DIM]).astype(BF16)
    ubuf[0:POOL_HALO, :] = ubuf[tm:tm + POOL_HALO, :]

    scale = HEAD_DIM ** -0.5
    lane = lax.broadcasted_iota(I32, (tm, hp), 1) % HEAD_PAD
    qrows = jnp.where(lane == ALIBI_COL, 1.0, qr * scale)
    for hh in range(D_HEADS):
        qrow_ref[hh] = qrows[:, hh * HEAD_PAD:(hh + 1) * HEAD_PAD]
    kvb = kvt.astype(BF16)
    first = lax.broadcasted_iota(I32, (HEAD_DIM, tm), 0) == 0
    koff = (lax.broadcasted_iota(I32, (HEAD_DIM, tm), 1) % MOBA_BLOCK).astype(F32)
    zeros = jnp.zeros((HEAD_DIM, tm), BF16)
    for hh in range(D_HEADS):
        kv_ref[hh, 0:HEAD_DIM, :] = kvb[hh * HEAD_DIM:(hh + 1) * HEAD_DIM]
        kv_ref[hh, HEAD_DIM:HEAD_PAD, :] = jnp.where(
            first, koff * (2.0 ** -(hh + 1)), 0.0).astype(BF16)
        kv_ref[hh, HEAD_PAD:HEAD_PAD + HEAD_DIM, :] = kvb[D_WIDTH + hh * HEAD_DIM:
                                                          D_WIDTH + (hh + 1) * HEAD_DIM]
        kv_ref[hh, HEAD_PAD + HEAD_DIM:2 * HEAD_PAD, :] = zeros

    per_tile = tm // MOBA_BLOCK
    for sb in range(per_tile):
        blk = h[sb * MOBA_BLOCK:(sb + 1) * MOBA_BLOCK, :].astype(F32)
        hmean[pl.ds(i * per_tile + sb, 1), :] = jnp.mean(blk, axis=0, keepdims=True)

    @pl.when(i == pl.num_programs(0) - 1)
    def _():
        hi, lo = _split(hmean[...])
        wkt = wkvt_ref[0:D_WIDTH, :]
        km_ref[...] = (lax.dot_general(hi, wkt, _NT, preferred_element_type=F32)
                       + lax.dot_general(lo, wkt, _NT, preferred_element_type=F32))


def _pm_proj_call(x, mod, norm_g, wu, wq, wkvt, pw, pb, ps, *, layer, tm=512):
    hp = D_HEADS * HEAD_PAD
    return pl.pallas_call(
        functools.partial(_pm_proj_kernel, tm=tm),
        grid=(SEQ // tm,),
        in_specs=[
            pl.BlockSpec((tm, D_MODEL), lambda i: (i, 0)),
            _resident((None, None, 3, D_MODEL), lambda i: (layer, 1, 0, 0)),
            _resident((None, 1, D_MODEL), lambda i: (layer, 0, 0)),
            _full((D_MODEL, C_WIDTH)),
            _full((D_MODEL, hp)),
            _full((2 * D_WIDTH, D_MODEL)),
            _full((len(POOL_WINDOWS), C_GROUP_DIM, C_GROUP_DIM)),
            _full((1, C_WIDTH)),
            _full((1, C_WIDTH)),
        ],
        out_specs=[
            pl.BlockSpec((tm, C_WIDTH), lambda i: (i, 0)),
            pl.BlockSpec((D_HEADS, tm, HEAD_PAD), lambda i: (0, i, 0)),
            pl.BlockSpec((D_HEADS, 2 * HEAD_PAD, tm), lambda i: (0, 0, i)),
            pl.BlockSpec((N_BLOCKS, D_WIDTH), lambda i: (0, 0)),
        ],
        out_shape=[
            jax.ShapeDtypeStruct((SEQ, C_WIDTH), BF16),
            jax.ShapeDtypeStruct((D_HEADS, SEQ, HEAD_PAD), F32),
            jax.ShapeDtypeStruct((D_HEADS, 2 * HEAD_PAD, SEQ), BF16),
            jax.ShapeDtypeStruct((N_BLOCKS, D_WIDTH), F32),
        ],
        scratch_shapes=[pltpu.VMEM((POOL_HALO + tm, C_WIDTH), F32),
                        pltpu.VMEM((N_BLOCKS, D_MODEL), F32)],
        compiler_params=_params(1),
        name="pool_qkv_proj",
    )(x, mod, norm_g, wu, wq, wkvt, pw, pb, ps)


def _block_rows(qs, slabs, mask, lse_add):
    n = qs[0].shape[0]
    s = jnp.concatenate(
        [jnp.dot(q, slab[0:HEAD_PAD, :], preferred_element_type=F32)
         for q, slab in zip(qs, slabs)], axis=0)
    if mask is not None:
        s = jnp.where(mask, s, NEG_BIG)
    m = jnp.max(s, axis=1, keepdims=True)
    p = jnp.exp(s - m).astype(BF16)
    o = jnp.concatenate(
        [lax.dot_general(p[u * n:(u + 1) * n], slab[HEAD_PAD:2 * HEAD_PAD, :], _NT,
                         preferred_element_type=F32)
         for u, slab in enumerate(slabs)], axis=0)
    tot = jnp.dot(p, jnp.ones((MOBA_BLOCK, HEAD_PAD), BF16), preferred_element_type=F32)
    lse = m + jnp.log(tot)
    if lse_add is not None:
        lse = jnp.concatenate(
            [lse[u * n:(u + 1) * n] + add for u, add in enumerate(lse_add)], axis=0)
    low = lax.broadcasted_iota(I32, o.shape, 1) < HEAD_DIM
    return jnp.where(low, o / tot, lse)


def _onehot_pairs(sel_rows):
    blk = lax.broadcasted_iota(I32, (N_BLOCKS, MOBA_BLOCK), 0)
    return jnp.concatenate(
        [jnp.where(blk == sel_rows[s], 1.0, 0.0) for s in range(MOBA_TOPK)], axis=1)


def _gate_own_kernel(qrow_ref, kv_ref, km_ref, tri_ref,
                     sel_ref, rank_ref, cnt_ref, own_ref, *, heads):
    own = pl.program_id(1)
    n_pair = MOBA_TOPK * MOBA_BLOCK
    qs = [qrow_ref[a].astype(BF16) for a in range(heads)]

    gates = []
    for a in range(heads):
        km_hi, km_lo = _split(km_ref[a])
        gates.append(lax.dot_general(km_hi, qs[a], _NT, preferred_element_type=F32)
                     + lax.dot_general(km_lo, qs[a], _NT, preferred_element_type=F32))
    blk = lax.broadcasted_iota(I32, (N_BLOCKS, heads * MOBA_BLOCK), 0)
    g = jnp.where(blk < own, jnp.concatenate(gates, axis=1), -jnp.inf)
    slots = []
    for s in range(MOBA_TOPK):
        best = jnp.max(g, axis=0, keepdims=True)
        idx = jnp.min(jnp.where(g == best, blk, N_BLOCKS), axis=0, keepdims=True)
        g = jnp.where(blk == idx, -jnp.inf, g)
        slots.append(jnp.where(s < own, idx, N_BLOCKS))

    onehots = []
    for a in range(heads):
        sel_rows = [slot[:, a * MOBA_BLOCK:(a + 1) * MOBA_BLOCK] for slot in slots]
        sel_ref[a] = jnp.concatenate(sel_rows, axis=0)
        onehots.append(_onehot_pairs(sel_rows))
    onehot = jnp.concatenate(onehots, axis=0)
    oh = onehot.astype(BF16)
    before = jnp.dot(oh, tri_ref[...], preferred_element_type=F32)
    hit = onehot * before
    ones = jnp.ones((8, n_pair), BF16)
    cnt = lax.dot_general(ones, oh, _NT, preferred_element_type=F32)
    for a in range(heads):
        rank = jnp.sum(hit[a * N_BLOCKS:(a + 1) * N_BLOCKS], axis=0, keepdims=True).astype(I32)
        rank_ref[a] = jnp.concatenate(
            [rank[:, s * MOBA_BLOCK:(s + 1) * MOBA_BLOCK] for s in range(MOBA_TOPK)], axis=0)
        cnt_ref[a] = cnt[0:1, a * N_BLOCKS:(a + 1) * N_BLOCKS].astype(I32)

    qi = lax.broadcasted_iota(I32, (heads * MOBA_BLOCK, MOBA_BLOCK), 0) % MOBA_BLOCK
    ki = lax.broadcasted_iota(I32, (heads * MOBA_BLOCK, MOBA_BLOCK), 1)
    rows = _block_rows(qs, [kv_ref.at[a] for a in range(heads)], ki <= qi, None)
    for a in range(heads):
        own_ref[a] = rows[a * MOBA_BLOCK:(a + 1) * MOBA_BLOCK]


def _gate_own_call(qrows, kv, kmean_h, tri, *, heads=GATE_HEADS_PER_STEP):
    return pl.pallas_call(
        functools.partial(_gate_own_kernel, heads=heads),
        grid=(D_HEADS // heads, N_BLOCKS),
        in_specs=[
            pl.BlockSpec((heads, MOBA_BLOCK, HEAD_PAD), lambda h, b: (h, b, 0)),
            pl.BlockSpec((heads, 2 * HEAD_PAD, MOBA_BLOCK), lambda h, b: (h, 0, b)),
            pl.BlockSpec((heads, N_BLOCKS, HEAD_PAD), lambda h, b: (h, 0, 0)),
            _full((MOBA_TOPK * MOBA_BLOCK, MOBA_TOPK * MOBA_BLOCK)),
        ],
        out_specs=[
            pl.BlockSpec((heads, MOBA_TOPK, MOBA_BLOCK), lambda h, b: (h, 0, b)),
            pl.BlockSpec((heads, MOBA_TOPK, MOBA_BLOCK), lambda h, b: (h, 0, b)),
            pl.BlockSpec((heads, None, 1, N_BLOCKS), lambda h, b: (h, b, 0, 0)),
            pl.BlockSpec((heads, MOBA_BLOCK, HEAD_PAD), lambda h, b: (h, b, 0)),
        ],
        out_shape=[
            jax.ShapeDtypeStruct((D_HEADS, MOBA_TOPK, SEQ), I32),
            jax.ShapeDtypeStruct((D_HEADS, MOBA_TOPK, SEQ), I32),
            jax.ShapeDtypeStruct((D_HEADS, N_BLOCKS, 1, N_BLOCKS), I32),
            jax.ShapeDtypeStruct((D_HEADS, SEQ, HEAD_PAD), F32),
        ],
        compiler_params=_params(2),
        name="moba_gate_own",
    )(qrows, kv, kmean_h, tri)


def _route_pos_kernel(sel_ref, rank_ref, base_ref, pos_ref):
    lane = lax.broadcasted_iota(I32, (1, MOBA_BLOCK), 1)
    for a in range(D_HEADS):
        sel = sel_ref[a]
        onehot = _onehot_pairs([sel[s:s + 1, :] for s in range(MOBA_TOPK)]).astype(BF16)
        dig = jnp.dot(base_ref[a].astype(BF16), onehot, preferred_element_type=F32)
        base = (dig[0:1] * 16384.0 + dig[1:2] * 128.0 + dig[2:3]).astype(I32)
        rows = []
        for s in range(MOBA_TOPK):
            p = base[:, s * MOBA_BLOCK:(s + 1) * MOBA_BLOCK] + rank_ref[a, s:s + 1, :]
            rows.append(jnp.where(sel[s:s + 1, :] < N_BLOCKS, p, TRASH_ROW0 + lane % SC_WINDOW))
        pos_ref[a] = jnp.concatenate(rows, axis=0)


def _route_pos_call(sel, rank, base_digits):
    return pl.pallas_call(
        _route_pos_kernel,
        grid=(N_BLOCKS,),
        in_specs=[
            pl.BlockSpec((D_HEADS, MOBA_TOPK, MOBA_BLOCK), lambda b: (0, 0, b)),
            pl.BlockSpec((D_HEADS, MOBA_TOPK, MOBA_BLOCK), lambda b: (0, 0, b)),
            pl.BlockSpec((D_HEADS, None, 8, N_BLOCKS), lambda b: (0, b, 0, 0)),
        ],
        out_specs=pl.BlockSpec((D_HEADS, MOBA_TOPK, MOBA_BLOCK), lambda b: (0, 0, b)),
        out_shape=jax.ShapeDtypeStruct((D_HEADS, MOBA_TOPK, SEQ), I32),
        compiler_params=_params(1),
        name="moba_route_pos",
    )(sel, rank, base_digits)


def _route_tables(cnt):
    cnt = cnt.reshape(D_HEADS, N_BLOCKS, N_BLOCKS)
    tiles = (cnt.sum(axis=1) + ROUTE_TILE - 1) // ROUTE_TILE
    tiles_flat = tiles.reshape(N_GROUPS)
    tile0 = jnp.cumsum(tiles_flat) - tiles_flat
    n_tiles = tiles_flat.sum().astype(I32)
    within = jnp.cumsum(cnt, axis=1) - cnt
    base = tile0.reshape(D_HEADS, 1, N_BLOCKS) * ROUTE_TILE + within
    digits = jnp.stack([base // 16384, (base // 128) % 128, base % 128], axis=2).astype(F32)
    digits = jnp.pad(digits, ((0, 0), (0, 0), (0, 8 - 3), (0, 0)))
    group_of_tile = jnp.repeat(jnp.arange(N_GROUPS, dtype=I32), tiles_flat,
                               total_repeat_length=MAX_TILES)
    t = jnp.arange(MAX_TILES, dtype=I32)
    last = group_of_tile[jnp.maximum(n_tiles - 1, 0)]
    group_of_tile = jnp.where(t < n_tiles, group_of_tile, last)
    tile_h = group_of_tile // N_BLOCKS
    tile_step = jnp.asarray(2.0 ** (-8.0 * np.arange(1, D_HEADS + 1) / D_HEADS) * MOBA_BLOCK,
                            dtype=F32)[tile_h]
    return digits, tile_h, group_of_tile % N_BLOCKS, n_tiles.reshape(1), tile_step


def _sc_mesh():
    return plsc.VectorSubcoreMesh(core_axis_name="core", subcore_axis_name="subcore")


def _dispatch_call(rows, pos_by_slot):
    n_rows = rows.shape[0]

    @functools.partial(
        pl.kernel, mesh=_sc_mesh(), scratch_types=[],
        out_type=jax.ShapeDtypeStruct((ROUTE_ROWS, HEAD_PAD), F32))
    def dispatch(x_hbm, i0_hbm, i1_hbm, i2_hbm, o_hbm):
        def body(x_vmem, i0_vmem, i1_vmem, i2_vmem):
            pltpu.sync_copy(x_vmem, o_hbm.at[i0_vmem.at[0]])
            pltpu.sync_copy(x_vmem, o_hbm.at[i1_vmem.at[0]])
            pltpu.sync_copy(x_vmem, o_hbm.at[i2_vmem.at[0]])

        idx_spec = pl.BlockSpec((1, SC_WINDOW), lambda i: (0, i))
        pltpu.emit_pipeline(
            body,
            grid=(n_rows // SC_WINDOW,),
            in_specs=[pl.BlockSpec((SC_WINDOW, HEAD_PAD), lambda i: (i, 0)),
                      idx_spec, idx_spec, idx_spec],
            out_specs=[],
            core_axis_name=("core", "subcore"),
            dimension_semantics=(pltpu.PARALLEL,),
        )(x_hbm, i0_hbm, i1_hbm, i2_hbm)

    return dispatch(rows, *pos_by_slot)


def _collect_call(table, idx):
    n_rows = idx.shape[1]

    @functools.partial(
        pl.kernel, mesh=_sc_mesh(), scratch_types=[],
        out_type=jax.ShapeDtypeStruct((n_rows, HEAD_PAD), F32))
    def collect(x_hbm, i_hbm, o_hbm):
        def body(i_vmem, o_vmem):
            pltpu.sync_copy(x_hbm.at[i_vmem.at[0]], o_vmem)

        pltpu.emit_pipeline(
            body,
            grid=(n_rows // SC_WINDOW,),
            in_specs=[pl.BlockSpec((1, SC_WINDOW), lambda i: (0, i))],
            out_specs=[pl.BlockSpec((SC_WINDOW, HEAD_PAD), lambda i: (i, 0))],
            core_axis_name=("core", "subcore"),
            dimension_semantics=(pltpu.PARALLEL,),
        )(i_hbm, o_hbm)

    return collect(table, idx)


def _routed_attn_kernel(th_ref, tj_ref, nt_ref, step_ref, q_ref, *refs, tiles):
    kv_refs, o_ref = refs[0:tiles], refs[tiles]
    t0 = pl.program_id(0) * tiles

    @pl.when(t0 < nt_ref[0])
    def _():
        qs = [q_ref[u * ROUTE_TILE:(u + 1) * ROUTE_TILE, :].astype(BF16) for u in range(tiles)]
        adds = [step_ref[t0 + u] * tj_ref[t0 + u].astype(F32) for u in range(tiles)]
        o_ref[...] = _block_rows(qs, kv_refs, None, adds)


def _routed_attn_call(tile_h, tile_j, n_tiles, tile_step, routed_q, kv,
                      *, tiles=ROUTE_TILES_PER_STEP):
    live = lambda s, th, tj, nt: (jnp.minimum(s, (nt[0] + tiles - 1) // tiles), 0)

    def kv_spec(u):
        return pl.BlockSpec((None, 2 * HEAD_PAD, MOBA_BLOCK),
                            lambda s, th, tj, nt: (th[s * tiles + u], 0, tj[s * tiles + u]))

    grid_spec = pltpu.PrefetchScalarGridSpec(
        num_scalar_prefetch=3,
        grid=(MAX_TILES // tiles,),
        in_specs=([pl.BlockSpec(memory_space=pltpu.SMEM),
                   pl.BlockSpec((tiles * ROUTE_TILE, HEAD_PAD), live)]
                  + [kv_spec(u) for u in range(tiles)]),
        out_specs=pl.BlockSpec((tiles * ROUTE_TILE, HEAD_PAD), live),
    )
    return pl.pallas_call(
        functools.partial(_routed_attn_kernel, tiles=tiles),
        grid_spec=grid_spec,
        out_shape=jax.ShapeDtypeStruct((ROUTE_ROWS, HEAD_PAD), F32),
        compiler_params=_params(1),
        name="moba_routed_attn",
    )(tile_h, tile_j, n_tiles, tile_step, routed_q, *([kv] * tiles))


def _pm_out_kernel(slope_ref, x_ref, mod_ref, pooled_ref, own_ref, got_ref, wp_ref, wa_ref,
                   o_ref):
    own = pl.program_id(0)
    low = lax.broadcasted_iota(I32, (MOBA_BLOCK, HEAD_PAD), 1) < HEAD_DIM

    def lse_of(part):
        return jnp.where(low, pltpu.roll(part, HEAD_DIM, axis=1), part)

    y = jnp.dot(pooled_ref[...], wp_ref[...], preferred_element_type=F32)
    for hh in range(D_HEADS):
        own_shift = slope_ref[hh] * (own * MOBA_BLOCK).astype(F32)
        parts = [own_ref[hh]]
        lses = [lse_of(parts[0])]
        for s in range(MOBA_TOPK):
            valid = s < own
            part = got_ref[hh, s]
            parts.append(jnp.where(valid, part, 0.0))
            lses.append(jnp.where(valid, lse_of(part) - own_shift, NEG_BIG))
        top = functools.reduce(jnp.maximum, lses)
        num = jnp.zeros((MOBA_BLOCK, HEAD_PAD), F32)
        den = jnp.zeros((MOBA_BLOCK, HEAD_PAD), F32)
        for part, ls in zip(parts, lses):
            w = jnp.exp(ls - top)
            num = num + w * part
            den = den + w
        att = (num / den).astype(BF16)
        y = y + jnp.dot(att, wa_ref[hh], preferred_element_type=F32)
    o_ref[...] = x_ref[...] + mod_ref[2:3, :] * y


def _pm_out_call(slopes, x, mod, pooled, own_parts, got_parts, wp, wa, *, layer):
    tm = MOBA_BLOCK
    return pl.pallas_call(
        _pm_out_kernel,
        grid=(SEQ // tm,),
        in_specs=[
            pl.BlockSpec(memory_space=pltpu.SMEM),
            pl.BlockSpec((tm, D_MODEL), lambda i: (i, 0)),
            _resident((None, None, 3, D_MODEL), lambda i: (layer, 1, 0, 0)),
            pl.BlockSpec((tm, C_WIDTH), lambda i: (i, 0)),
            pl.BlockSpec((D_HEADS, tm, HEAD_PAD), lambda i: (0, i, 0)),
            pl.BlockSpec((D_HEADS, MOBA_TOPK, tm, HEAD_PAD), lambda i: (0, 0, i, 0)),
            _full((C_WIDTH, D_MODEL)),
            _full((D_HEADS, HEAD_PAD, D_MODEL)),
        ],
        out_specs=pl.BlockSpec((tm, D_MODEL), lambda i: (i, 0)),
        out_shape=jax.ShapeDtypeStruct((SEQ, D_MODEL), F32),
        compiler_params=_params(1),
        name="pool_moba_out",
    )(slopes, x, mod, pooled, own_parts, got_parts, wp, wa)


def _head_pad(w, axis):
    shape = list(w.shape)
    shape[axis:axis + 1] = [D_HEADS, HEAD_DIM]
    pad = [(0, 0)] * len(shape)
    pad[axis + 1] = (0, HEAD_PAD - HEAD_DIM)
    out = jnp.pad(w.reshape(shape), pad)
    shape[axis:axis + 2] = [D_HEADS * HEAD_PAD]
    return out.reshape(shape)


def kernel(x, c, ada_w, ada_b, ffn_norm, ffn_w_gate, ffn_w_up, ffn_w_down, mix_norm, conv_w_in,
           conv_a_w, conv_a_b, conv_a_ln_g, conv_a_ln_b, conv_b_w, conv_w_out, pm_w_in, pool_w,
           pool_b, pool_scale, pm_w_out, final_norm):
    assert x.shape == (1, SEQ, D_MODEL) and c.shape == (1, D_MODEL)
    xs = x.reshape(SEQ, D_MODEL)
    mod = _mod_call(c, ada_w, ada_b)

    wg = ffn_w_gate.astype(BF16)
    wu = ffn_w_up.astype(BF16)
    wd = ffn_w_down.astype(BF16)
    ffn_g = ffn_norm.reshape(DEPTH, 2, 1, D_MODEL)
    mix_g = mix_norm.reshape(DEPTH, 1, D_MODEL)
    fin = final_norm.reshape(1, D_MODEL)
    ffn = functools.partial(_ffn_call, mod=mod, norm_g=ffn_g, wg=wg, wu=wu, wd=wd, fin=fin)

    xs = ffn(xs, layer=0, sub=0, final=False)
    grp = np.arange(A_WIDTH) // (A_WIDTH // A_GROUPS)
    seg = jnp.asarray((grp[:, None] == grp[None, :]) / (A_WIDTH // A_GROUPS), dtype=BF16)
    xs = _conv_mix_call(
        xs, mod, mix_g, conv_w_in[0].astype(BF16), conv_a_w[0], conv_a_b[0].reshape(1, A_WIDTH),
        conv_a_ln_g[0].reshape(1, A_WIDTH), conv_a_ln_b[0].reshape(1, A_WIDTH), conv_b_w[0],
        conv_w_out[0].astype(BF16), seg, layer=0)
    xs = ffn(xs, layer=0, sub=1, final=False)

    xs = ffn(xs, layer=1, sub=0, final=False)
    w_in = pm_w_in[0].astype(BF16)
    w_u = w_in[:, 0:C_WIDTH]
    w_q = _head_pad(w_in[:, C_WIDTH:C_WIDTH + D_WIDTH], 1)
    w_kvt = w_in[:, C_WIDTH + D_WIDTH:C_WIDTH + 3 * D_WIDTH].T
    pooled, qrows, kv, kmean = _pm_proj_call(
        xs, mod, mix_g, w_u, w_q, w_kvt, pool_w[0].astype(BF16),
        pool_b[0].reshape(1, C_WIDTH), pool_scale[0].reshape(1, C_WIDTH), layer=1)
    kmean_h = _head_pad(kmean, 1).reshape(N_BLOCKS, D_HEADS, HEAD_PAD).transpose(1, 0, 2)

    n_pair = MOBA_TOPK * MOBA_BLOCK
    tri = jnp.asarray(np.arange(n_pair)[:, None] < np.arange(n_pair)[None, :], dtype=BF16)
    sel, rank, cnt, own_parts = _gate_own_call(qrows, kv, kmean_h, tri)
    base_digits, tile_h, tile_j, n_tiles, tile_step = _route_tables(cnt)
    pos = _route_pos_call(sel, rank, base_digits)

    pos_by_slot = [pos[:, s, :].reshape(1, D_HEADS * SEQ) for s in range(MOBA_TOPK)]
    routed_q = _dispatch_call(qrows.reshape(D_HEADS * SEQ, HEAD_PAD), pos_by_slot)
    routed_parts = _routed_attn_call(tile_h, tile_j, n_tiles, tile_step, routed_q, kv)
    got = _collect_call(routed_parts, pos.reshape(1, N_PAIRS))
    got = got.reshape(D_HEADS, MOBA_TOPK, SEQ, HEAD_PAD)

    slopes = jnp.asarray(2.0 ** (-8.0 * np.arange(1, D_HEADS + 1) / D_HEADS), dtype=F32)
    w_out = pm_w_out[0].astype(BF16)
    w_att = _head_pad(w_out[C_WIDTH:], 0).reshape(D_HEADS, HEAD_PAD, D_MODEL)
    xs = _pm_out_call(slopes, xs, mod, pooled, own_parts, got, w_out[0:C_WIDTH], w_att,
                      layer=1)
    xs = ffn(xs, layer=1, sub=1, final=True)
    return xs.reshape(1, SEQ, D_MODEL)
```

```python
import functools

import numpy as np
import jax
import jax.numpy as jnp
from jax import lax
from jax.experimental import pallas as pl
from jax.experimental.pallas import tpu as pltpu
from jax.experimental.pallas import tpu_sc as plsc

D_MODEL = 1024
SEQ = 16384
DEPTH = 2
N_SUBLAYERS = 3
D_FF = 2816
EPS = 1e-6

A_WIDTH = 512
A_GROUPS = 8
A_CONV = 31
B_WIDTH = 512
B_CONV = 3

POOL_WINDOWS = (2, 4, 8, 16)
C_WIDTH = 512
C_GROUP_DIM = 128
D_HEADS = 8
HEAD_DIM = 64
D_WIDTH = 512
MOBA_BLOCK = 256
MOBA_TOPK = 3
N_BLOCKS = SEQ // MOBA_BLOCK

V7X_SUBLANES = 8
V7X_VMEM_LIMIT_BYTES = 56 * 1024 * 1024

A_HALO = 32
B_HALO = 8
POOL_HALO = 16
NEG_BIG = -1e30

HEAD_PAD = 128
ALIBI_COL = HEAD_DIM

SPLIT_HEADS = 4
N_SPLITS = D_HEADS // SPLIT_HEADS
ROUTE_TILE = 256
ROUTE_TILES_PER_STEP = 8
N_PAIRS = SPLIT_HEADS * SEQ * MOBA_TOPK
N_GROUPS = SPLIT_HEADS * N_BLOCKS
MAX_TILES = N_PAIRS // ROUTE_TILE + N_GROUPS
TRASH_ROW0 = MAX_TILES * ROUTE_TILE
ROUTE_ROWS = (MAX_TILES + ROUTE_TILES_PER_STEP) * ROUTE_TILE
SC_WINDOW = 128

BF16 = jnp.bfloat16
F32 = jnp.float32
I32 = jnp.int32


def _params(n_axes):
    return pltpu.CompilerParams(
        dimension_semantics=("arbitrary",) * n_axes,
        vmem_limit_bytes=V7X_VMEM_LIMIT_BYTES)


def _resident(block_shape, index_map):
    return pl.BlockSpec(block_shape, index_map, pipeline_mode=pl.Buffered(1))


def _full(shape):
    return _resident(shape, lambda *_: (0,) * len(shape))


def _sigmoid(v):
    return 1.0 / (1.0 + jnp.exp(-v))


def _norm_mod(x, g, mod):
    r = lax.rsqrt(jnp.mean(x * x, axis=-1, keepdims=True) + EPS)
    return (x * r) * (g * (1.0 + mod[1:2])) + mod[0:1]


def _split(v):
    hi = v.astype(BF16)
    return hi, (v - hi.astype(F32)).astype(BF16)


def _split_dot(v, w):
    hi, lo = _split(v)
    return (jnp.dot(hi, w, preferred_element_type=F32)
            + jnp.dot(lo, w, preferred_element_type=F32))


_NT = (((1,), (1,)), ((), ()))


def _mod_kernel(c_ref, w_ref, b_ref, o_ref):
    c = c_ref[...]
    cond = c * _sigmoid(c)
    o_ref[0] = jnp.sum(w_ref[0] * cond, axis=0, keepdims=True) + b_ref[0]


def _mod_call(c, ada_w, ada_b):
    n_out = N_SUBLAYERS * 3 * D_MODEL
    tn = D_MODEL
    out = pl.pallas_call(
        _mod_kernel,
        grid=(DEPTH, n_out // tn),
        in_specs=[
            pl.BlockSpec((D_MODEL, 1), lambda l, j: (0, 0)),
            pl.BlockSpec((1, D_MODEL, tn), lambda l, j: (l, 0, j)),
            pl.BlockSpec((1, 1, tn), lambda l, j: (l, 0, j)),
        ],
        out_specs=pl.BlockSpec((1, 1, tn), lambda l, j: (l, 0, j)),
        out_shape=jax.ShapeDtypeStruct((DEPTH, 1, n_out), F32),
        compiler_params=_params(2),
        name="adaln_mod",
    )(c.reshape(D_MODEL, 1), ada_w, ada_b.reshape(DEPTH, 1, n_out))
    return out.reshape(DEPTH, N_SUBLAYERS, 3, D_MODEL)


def _ffn_kernel(x_ref, mod_ref, g_ref, wg_ref, wu_ref, wd_ref, fin_ref, o_ref, *, final):
    x = x_ref[...]
    mod = mod_ref[...]
    h = _norm_mod(x, g_ref[...], mod).astype(BF16)
    gate = jnp.dot(h, wg_ref[...], preferred_element_type=F32)
    up = jnp.dot(h, wu_ref[...], preferred_element_type=F32)
    act = (gate * _sigmoid(gate) * up).astype(BF16)
    y = jnp.dot(act, wd_ref[...], preferred_element_type=F32)
    xn = x + (0.5 * mod[2:3]) * y
    if final:
        r = lax.rsqrt(jnp.mean(xn * xn, axis=-1, keepdims=True) + EPS)
        xn = xn * r * fin_ref[...]
    o_ref[...] = xn


def _ffn_call(x, mod, norm_g, wg, wu, wd, fin, *, layer, sub, final, tm=512):
    mod_sub = 0 if sub == 0 else 2
    return pl.pallas_call(
        functools.partial(_ffn_kernel, final=final),
        grid=(SEQ // tm,),
        in_specs=[
            pl.BlockSpec((tm, D_MODEL), lambda i: (i, 0)),
            _resident((None, None, 3, D_MODEL), lambda i: (layer, mod_sub, 0, 0)),
            _resident((None, None, 1, D_MODEL), lambda i: (layer, sub, 0, 0)),
            _resident((None, None, D_MODEL, D_FF), lambda i: (layer, sub, 0, 0)),
            _resident((None, None, D_MODEL, D_FF), lambda i: (layer, sub, 0, 0)),
            _resident((None, None, D_FF, D_MODEL), lambda i: (layer, sub, 0, 0)),
            _resident((1, D_MODEL), lambda i: (0, 0)),
        ],
        out_specs=pl.BlockSpec((tm, D_MODEL), lambda i: (i, 0)),
        out_shape=jax.ShapeDtypeStruct((SEQ, D_MODEL), F32),
        compiler_params=_params(1),
        name=f"ffn_l{layer}s{sub}",
    )(x, mod, norm_g, wg, wu, wd, fin)


def _conv_mix_kernel(x_ref, mod_ref, g_ref, win_ref, caw_ref, cab_ref, lng_ref, lnb_ref,
                     cbw_ref, wout_ref, seg_ref, o_ref, abuf, bbuf, sbuf, *, tm):
    i = pl.program_id(0)

    @pl.when(i == 0)
    def _():
        abuf[0:A_HALO, :] = jnp.zeros((A_HALO, A_WIDTH), F32)
        bbuf[0:B_HALO, :] = jnp.zeros((B_HALO, B_WIDTH), F32)

    x = x_ref[...]
    mod = mod_ref[...]
    h = _norm_mod(x, g_ref[...], mod).astype(BF16)
    z = jnp.dot(h, win_ref[...], preferred_element_type=F32)

    a = z[:, 0:A_WIDTH] * _sigmoid(z[:, A_WIDTH:2 * A_WIDTH])
    abuf[A_HALO:A_HALO + tm, :] = a
    acc = jnp.zeros((tm, A_WIDTH), F32) + cab_ref[...]
    ext = tm + V7X_SUBLANES
    for r in range(V7X_SUBLANES):
        part = None
        for q in range((A_CONV - 1 - r) // V7X_SUBLANES + 1):
            k = A_CONV - 1 - (V7X_SUBLANES * q + r)
            term = caw_ref[k:k + 1, :] * abuf[pl.ds(A_HALO - V7X_SUBLANES * (q + 1), ext), :]
            part = term if part is None else part + term
        if r == 0:
            acc = acc + part[V7X_SUBLANES:, :]
        else:
            sbuf[r - 1] = part
            acc = acc + sbuf[r - 1, pl.ds(V7X_SUBLANES - r, tm), :]
    abuf[0:A_HALO, :] = abuf[tm:tm + A_HALO, :]
    seg = seg_ref[...]
    mu = _split_dot(acc, seg)
    d = acc - mu
    var = _split_dot(d * d, seg)
    yn = d * lax.rsqrt(var + EPS) * lng_ref[...] + lnb_ref[...]
    a_out = yn * _sigmoid(yn)

    off = 2 * A_WIDTH
    cv = z[:, off + B_WIDTH:off + 2 * B_WIDTH] * z[:, off + 2 * B_WIDTH:off + 3 * B_WIDTH]
    bbuf[B_HALO:B_HALO + tm, :] = cv
    bacc = jnp.zeros((tm, B_WIDTH), F32)
    for k in range(B_CONV):
        bacc = bacc + cbw_ref[k:k + 1, :] * bbuf[pl.ds(B_HALO - (B_CONV - 1) + k, tm), :]
    bbuf[0:B_HALO, :] = bbuf[tm:tm + B_HALO, :]
    bb = z[:, off:off + B_WIDTH] * bacc

    y = (jnp.dot(a_out.astype(BF16), wout_ref[0:A_WIDTH, :], preferred_element_type=F32)
         + jnp.dot(bb.astype(BF16), wout_ref[A_WIDTH:A_WIDTH + B_WIDTH, :],
                   preferred_element_type=F32))
    o_ref[...] = x + mod[2:3] * y


def _conv_mix_call(x, mod, norm_g, w_in, caw, cab, lng, lnb, cbw, w_out, seg, *, layer, tm=512):
    even_in = w_in.shape[1]
    return pl.pallas_call(
        functools.partial(_conv_mix_kernel, tm=tm),
        grid=(SEQ // tm,),
        in_specs=[
            pl.BlockSpec((tm, D_MODEL), lambda i: (i, 0)),
            _resident((None, None, 3, D_MODEL), lambda i: (layer, 1, 0, 0)),
            _resident((None, 1, D_MODEL), lambda i: (layer, 0, 0)),
            _full((D_MODEL, even_in)),
            _full((A_CONV, A_WIDTH)),
            _full((1, A_WIDTH)),
            _full((1, A_WIDTH)),
            _full((1, A_WIDTH)),
            _full((B_CONV, B_WIDTH)),
            _full((A_WIDTH + B_WIDTH, D_MODEL)),
            _full((A_WIDTH, A_WIDTH)),
        ],
        out_specs=pl.BlockSpec((tm, D_MODEL), lambda i: (i, 0)),
        out_shape=jax.ShapeDtypeStruct((SEQ, D_MODEL), F32),
        scratch_shapes=[pltpu.VMEM((A_HALO + tm, A_WIDTH), F32),
                        pltpu.VMEM((B_HALO + tm, B_WIDTH), F32),
                        pltpu.VMEM((V7X_SUBLANES - 1, tm + V7X_SUBLANES, A_WIDTH), F32)],
        compiler_params=_params(1),
        name="conv_mixers",
    )(x, mod, norm_g, w_in, caw, cab, lng, lnb, cbw, w_out, seg)


def _pm_proj_kernel(x_ref, mod_ref, g_ref, wu_ref, wq_ref, wkvt_ref, pw_ref,
                    pb_ref, ps_ref, pooled_ref, qrow_ref, kv_ref, km_ref,
                    ubuf, hmean, *, tm):
    i = pl.program_id(0)
    hp = D_HEADS * HEAD_PAD

    @pl.when(i == 0)
    def _():
        ubuf[0:POOL_HALO, :] = jnp.zeros((POOL_HALO, C_WIDTH), F32)

    x = x_ref[...]
    h = _norm_mod(x, g_ref[...], mod_ref[...]).astype(BF16)
    u = jnp.dot(h, wu_ref[...], preferred_element_type=F32)
    qr = jnp.dot(h, wq_ref[...], preferred_element_type=F32)
    kvt = lax.dot_general(wkvt_ref[...], h, _NT, preferred_element_type=F32)

    ubuf[POOL_HALO:POOL_HALO + tm, :] = u
    t1 = (i * tm + 1 + lax.broadcasted_iota(I32, (tm, C_GROUP_DIM), 0)).astype(F32)
    for gi, w in enumerate(POOL_WINDOWS):
        c0 = gi * C_GROUP_DIM
        s = u[:, c0:c0 + C_GROUP_DIM]
        for k in range(1, w):
            s = s + ubuf[pl.ds(POOL_HALO - k, tm), c0:c0 + C_GROUP_DIM]
        pooled = s / jnp.minimum(t1, float(w)) - u[:, c0:c0 + C_GROUP_DIM]
        mixed = (jnp.dot(pooled.astype(BF16), pw_ref[gi], preferred_element_type=F32)
                 + pb_ref[:, c0:c0 + C_GROUP_DIM])
        pooled_ref[:, c0:c0 + C_GROUP_DIM] = (
            mixed * ps_ref[:, c0:c0 + C_GROUP_DIM]).astype(BF16)
    ubuf[0:POOL_HALO, :] = ubuf[tm:tm + POOL_HALO, :]

    scale = HEAD_DIM ** -0.5
    lane = lax.broadcasted_iota(I32, (tm, hp), 1) % HEAD_PAD
    qrows = jnp.where(lane == ALIBI_COL, 1.0, qr * scale)
    for hh in range(D_HEADS):
        qrow_ref[hh] = qrows[:, hh * HEAD_PAD:(hh + 1) * HEAD_PAD]
    kvb = kvt.astype(BF16)
    first = lax.broadcasted_iota(I32, (HEAD_DIM, tm), 0) == 0
    koff = (lax.broadcasted_iota(I32, (HEAD_DIM, tm), 1) % MOBA_BLOCK).astype(F32)
    zeros = jnp.zeros((HEAD_DIM, tm), BF16)
    for hh in range(D_HEADS):
        kv_ref[hh, 0:HEAD_DIM, :] = kvb[hh * HEAD_DIM:(hh + 1) * HEAD_DIM]
        kv_ref[hh, HEAD_DIM:HEAD_PAD, :] = jnp.where(
            first, koff * (2.0 ** -(hh + 1)), 0.0).astype(BF16)
        kv_ref[hh, HEAD_PAD:HEAD_PAD + HEAD_DIM, :] = kvb[D_WIDTH + hh * HEAD_DIM:
                                                          D_WIDTH + (hh + 1) * HEAD_DIM]
        kv_ref[hh, HEAD_PAD + HEAD_DIM:2 * HEAD_PAD, :] = zeros

    per_tile = tm // MOBA_BLOCK
    for sb in range(per_tile):
        blk = h[sb * MOBA_BLOCK:(sb + 1) * MOBA_BLOCK, :].astype(F32)
        hmean[pl.ds(i * per_tile + sb, 1), :] = jnp.mean(blk, axis=0, keepdims=True)

    @pl.when(i == pl.num_programs(0) - 1)
    def _():
        hi, lo = _split(hmean[...])
        wkt = wkvt_ref[0:D_WIDTH, :]
        km_ref[...] = (lax.dot_general(hi, wkt, _NT, preferred_element_type=F32)
                       + lax.dot_general(lo, wkt, _NT, preferred_element_type=F32))


def _pm_proj_call(x, mod, norm_g, wu, wq, wkvt, pw, pb, ps, *, layer, tm=512):
    hp = D_HEADS * HEAD_PAD
    return pl.pallas_call(
        functools.partial(_pm_proj_kernel, tm=tm),
        grid=(SEQ // tm,),
        in_specs=[
            pl.BlockSpec((tm, D_MODEL), lambda i: (i, 0)),
            _resident((None, None, 3, D_MODEL), lambda i: (layer, 1, 0, 0)),
            _resident((None, 1, D_MODEL), lambda i: (layer, 0, 0)),
            _full((D_MODEL, C_WIDTH)),
            _full((D_MODEL, hp)),
            _full((2 * D_WIDTH, D_MODEL)),
            _full((len(POOL_WINDOWS), C_GROUP_DIM, C_GROUP_DIM)),
            _full((1, C_WIDTH)),
            _full((1, C_WIDTH)),
        ],
        out_specs=[
            pl.BlockSpec((tm, C_WIDTH), lambda i: (i, 0)),
            pl.BlockSpec((D_HEADS, tm, HEAD_PAD), lambda i: (0, i, 0)),
            pl.BlockSpec((D_HEADS, 2 * HEAD_PAD, tm), lambda i: (0, 0, i)),
            pl.BlockSpec((N_BLOCKS, D_WIDTH), lambda i: (0, 0)),
        ],
        out_shape=[
            jax.ShapeDtypeStruct((SEQ, C_WIDTH), BF16),
            jax.ShapeDtypeStruct((D_HEADS, SEQ, HEAD_PAD), F32),
            jax.ShapeDtypeStruct((D_HEADS, 2 * HEAD_PAD, SEQ), BF16),
            jax.ShapeDtypeStruct((N_BLOCKS, D_WIDTH), F32),
        ],
        scratch_shapes=[pltpu.VMEM((POOL_HALO + tm, C_WIDTH), F32),
                        pltpu.VMEM((N_BLOCKS, D_MODEL), F32)],
        compiler_params=_params(1),
        name="pool_qkv_proj",
    )(x, mod, norm_g, wu, wq, wkvt, pw, pb, ps)


def _block_rows(qs, slabs, mask, lse_add):
    n = qs[0].shape[0]
    s = jnp.concatenate(
        [jnp.dot(q, slab[0:HEAD_PAD, :], preferred_element_type=F32)
         for q, slab in zip(qs, slabs)], axis=0)
    if mask is not None:
        s = jnp.where(mask, s, NEG_BIG)
    m = jnp.max(s, axis=1, keepdims=True)
    p = jnp.exp(s - m).astype(BF16)
    o = jnp.concatenate(
        [lax.dot_general(p[u * n:(u + 1) * n], slab[HEAD_PAD:2 * HEAD_PAD, :], _NT,
                         preferred_element_type=F32)
         for u, slab in enumerate(slabs)], axis=0)
    tot = jnp.dot(p, jnp.ones((MOBA_BLOCK, HEAD_PAD), BF16), preferred_element_type=F32)
    lse = m + jnp.log(tot)
    if lse_add is not None:
        lse = jnp.concatenate(
            [lse[u * n:(u + 1) * n] + add for u, add in enumerate(lse_add)], axis=0)
    low = lax.broadcasted_iota(I32, o.shape, 1) < HEAD_DIM
    return jnp.where(low, o / tot, lse)


def _onehot_pairs(sel_rows):
    blk = lax.broadcasted_iota(I32, (N_BLOCKS, MOBA_BLOCK), 0)
    return jnp.concatenate(
        [jnp.where(blk == sel_rows[s], 1.0, 0.0) for s in range(MOBA_TOPK)], axis=1)


def _gate_own_kernel(qrow_ref, kv_ref, km_ref, tri_ref,
                     sel_ref, rank_ref, cnt_ref, own_ref, *, heads):
    own = pl.program_id(0)
    n_pair = MOBA_TOPK * MOBA_BLOCK
    qs = [qrow_ref[a].astype(BF16) for a in range(heads)]

    gates = []
    for a in range(heads):
        km_hi, km_lo = _split(km_ref[a])
        gates.append(lax.dot_general(km_hi, qs[a], _NT, preferred_element_type=F32)
                     + lax.dot_general(km_lo, qs[a], _NT, preferred_element_type=F32))
    blk = lax.broadcasted_iota(I32, (N_BLOCKS, heads * MOBA_BLOCK), 0)
    g = jnp.where(blk < own, jnp.concatenate(gates, axis=1), -jnp.inf)
    slots = []
    for s in range(MOBA_TOPK):
        best = jnp.max(g, axis=0, keepdims=True)
        idx = jnp.min(jnp.where(g == best, blk, N_BLOCKS), axis=0, keepdims=True)
        g = jnp.where(blk == idx, -jnp.inf, g)
        slots.append(jnp.where(s < own, idx, N_BLOCKS))

    onehots = []
    for a in range(heads):
        sel_rows = [slot[:, a * MOBA_BLOCK:(a + 1) * MOBA_BLOCK] for slot in slots]
        sel_ref[a] = jnp.concatenate(sel_rows, axis=0)
        onehots.append(_onehot_pairs(sel_rows))
    onehot = jnp.concatenate(onehots, axis=0)
    oh = onehot.astype(BF16)
    before = jnp.dot(oh, tri_ref[...], preferred_element_type=F32)
    hit = onehot * before
    ones = jnp.ones((8, n_pair), BF16)
    cnt = lax.dot_general(ones, oh, _NT, preferred_element_type=F32)
    for a in range(heads):
        rank = jnp.sum(hit[a * N_BLOCKS:(a + 1) * N_BLOCKS], axis=0, keepdims=True).astype(I32)
        rank_ref[a] = jnp.concatenate(
            [rank[:, s * MOBA_BLOCK:(s + 1) * MOBA_BLOCK] for s in range(MOBA_TOPK)], axis=0)
        cnt_ref[a] = cnt[0:1, a * N_BLOCKS:(a + 1) * N_BLOCKS].astype(I32)

    qi = lax.broadcasted_iota(I32, (heads * MOBA_BLOCK, MOBA_BLOCK), 0) % MOBA_BLOCK
    ki = lax.broadcasted_iota(I32, (heads * MOBA_BLOCK, MOBA_BLOCK), 1)
    rows = _block_rows(qs, [kv_ref.at[a] for a in range(heads)], ki <= qi, None)
    for a in range(heads):
        own_ref[a] = rows[a * MOBA_BLOCK:(a + 1) * MOBA_BLOCK]


def _gate_own_call(qrows, kv, kmean_h, tri, *, split):
    heads = SPLIT_HEADS
    return pl.pallas_call(
        functools.partial(_gate_own_kernel, heads=heads),
        grid=(N_BLOCKS,),
        in_specs=[
            pl.BlockSpec((heads, MOBA_BLOCK, HEAD_PAD), lambda b: (split, b, 0)),
            pl.BlockSpec((heads, 2 * HEAD_PAD, MOBA_BLOCK), lambda b: (split, 0, b)),
            pl.BlockSpec((heads, N_BLOCKS, HEAD_PAD), lambda b: (split, 0, 0)),
            _full((MOBA_TOPK * MOBA_BLOCK, MOBA_TOPK * MOBA_BLOCK)),
        ],
        out_specs=[
            pl.BlockSpec((heads, MOBA_TOPK, MOBA_BLOCK), lambda b: (0, 0, b)),
            pl.BlockSpec((heads, MOBA_TOPK, MOBA_BLOCK), lambda b: (0, 0, b)),
            pl.BlockSpec((heads, None, 1, N_BLOCKS), lambda b: (0, b, 0, 0)),
            pl.BlockSpec((heads, MOBA_BLOCK, HEAD_PAD), lambda b: (0, b, 0)),
        ],
        out_shape=[
            jax.ShapeDtypeStruct((heads, MOBA_TOPK, SEQ), I32),
            jax.ShapeDtypeStruct((heads, MOBA_TOPK, SEQ), I32),
            jax.ShapeDtypeStruct((heads, N_BLOCKS, 1, N_BLOCKS), I32),
            jax.ShapeDtypeStruct((heads, SEQ, HEAD_PAD), F32),
        ],
        compiler_params=_params(1),
        name=f"moba_gate_own_{split}",
    )(qrows, kv, kmean_h, tri)


def _route_pos_kernel(sel_ref, rank_ref, base_ref, pos_ref):
    lane = lax.broadcasted_iota(I32, (1, MOBA_BLOCK), 1)
    for a in range(SPLIT_HEADS):
        sel = sel_ref[a]
        onehot = _onehot_pairs([sel[s:s + 1, :] for s in range(MOBA_TOPK)]).astype(BF16)
        dig = jnp.dot(base_ref[a].astype(BF16), onehot, preferred_element_type=F32)
        base = (dig[0:1] * 16384.0 + dig[1:2] * 128.0 + dig[2:3]).astype(I32)
        rows = []
        for s in range(MOBA_TOPK):
            p = base[:, s * MOBA_BLOCK:(s + 1) * MOBA_BLOCK] + rank_ref[a, s:s + 1, :]
            rows.append(jnp.where(sel[s:s + 1, :] < N_BLOCKS, p, TRASH_ROW0 + lane % SC_WINDOW))
        pos_ref[a] = jnp.concatenate(rows, axis=0)


def _route_pos_call(sel, rank, base_digits):
    return pl.pallas_call(
        _route_pos_kernel,
        grid=(N_BLOCKS,),
        in_specs=[
            pl.BlockSpec((SPLIT_HEADS, MOBA_TOPK, MOBA_BLOCK), lambda b: (0, 0, b)),
            pl.BlockSpec((SPLIT_HEADS, MOBA_TOPK, MOBA_BLOCK), lambda b: (0, 0, b)),
            pl.BlockSpec((SPLIT_HEADS, None, 8, N_BLOCKS), lambda b: (0, b, 0, 0)),
        ],
        out_specs=pl.BlockSpec((SPLIT_HEADS, MOBA_TOPK, MOBA_BLOCK), lambda b: (0, 0, b)),
        out_shape=jax.ShapeDtypeStruct((SPLIT_HEADS, MOBA_TOPK, SEQ), I32),
        compiler_params=_params(1),
        name="moba_route_pos",
    )(sel, rank, base_digits)


def _route_tables(cnt, split):
    cnt = cnt.reshape(SPLIT_HEADS, N_BLOCKS, N_BLOCKS)
    tiles = (cnt.sum(axis=1) + ROUTE_TILE - 1) // ROUTE_TILE
    tiles_flat = tiles.reshape(N_GROUPS)
    tile0 = jnp.cumsum(tiles_flat) - tiles_flat
    n_tiles = tiles_flat.sum().astype(I32)
    within = jnp.cumsum(cnt, axis=1) - cnt
    base = tile0.reshape(SPLIT_HEADS, 1, N_BLOCKS) * ROUTE_TILE + within
    digits = jnp.stack([base // 16384, (base // 128) % 128, base % 128], axis=2).astype(F32)
    digits = jnp.pad(digits, ((0, 0), (0, 0), (0, 8 - 3), (0, 0)))
    group_of_tile = jnp.repeat(jnp.arange(N_GROUPS, dtype=I32), tiles_flat,
                               total_repeat_length=MAX_TILES)
    t = jnp.arange(MAX_TILES, dtype=I32)
    last = group_of_tile[jnp.maximum(n_tiles - 1, 0)]
    group_of_tile = jnp.where(t < n_tiles, group_of_tile, last)
    tile_h = group_of_tile // N_BLOCKS
    heads = split * SPLIT_HEADS + np.arange(1, SPLIT_HEADS + 1)
    tile_step = jnp.asarray(2.0 ** (-8.0 * heads / D_HEADS) * MOBA_BLOCK, dtype=F32)[tile_h]
    return digits, tile_h, group_of_tile % N_BLOCKS, n_tiles.reshape(1), tile_step


def _sc_mesh():
    return plsc.VectorSubcoreMesh(core_axis_name="core", subcore_axis_name="subcore")


def _dispatch_call(rows, pos_by_slot, *, split):
    n_rows = SPLIT_HEADS * SEQ
    first_window = split * n_rows // SC_WINDOW

    @functools.partial(
        pl.kernel, mesh=_sc_mesh(), scratch_types=[],
        out_type=jax.ShapeDtypeStruct((ROUTE_ROWS, HEAD_PAD), F32))
    def dispatch(x_hbm, i0_hbm, i1_hbm, i2_hbm, o_hbm):
        def body(x_vmem, i0_vmem, i1_vmem, i2_vmem):
            pltpu.sync_copy(x_vmem, o_hbm.at[i0_vmem.at[0]])
            pltpu.sync_copy(x_vmem, o_hbm.at[i1_vmem.at[0]])
            pltpu.sync_copy(x_vmem, o_hbm.at[i2_vmem.at[0]])

        idx_spec = pl.BlockSpec((1, SC_WINDOW), lambda i: (0, i))
        pltpu.emit_pipeline(
            body,
            grid=(n_rows // SC_WINDOW,),
            in_specs=[pl.BlockSpec((SC_WINDOW, HEAD_PAD), lambda i: (first_window + i, 0)),
                      idx_spec, idx_spec, idx_spec],
            out_specs=[],
            core_axis_name=("core", "subcore"),
            dimension_semantics=(pltpu.PARALLEL,),
        )(x_hbm, i0_hbm, i1_hbm, i2_hbm)

    return dispatch(rows, *pos_by_slot)


def _collect_call(table, idx):
    n_rows = idx.shape[1]

    @functools.partial(
        pl.kernel, mesh=_sc_mesh(), scratch_types=[],
        out_type=jax.ShapeDtypeStruct((n_rows, HEAD_PAD), F32))
    def collect(x_hbm, i_hbm, o_hbm):
        def body(i_vmem, o_vmem):
            pltpu.sync_copy(x_hbm.at[i_vmem.at[0]], o_vmem)

        pltpu.emit_pipeline(
            body,
            grid=(n_rows // SC_WINDOW,),
            in_specs=[pl.BlockSpec((1, SC_WINDOW), lambda i: (0, i))],
            out_specs=[pl.BlockSpec((SC_WINDOW, HEAD_PAD), lambda i: (i, 0))],
            core_axis_name=("core", "subcore"),
            dimension_semantics=(pltpu.PARALLEL,),
        )(i_hbm, o_hbm)

    return collect(table, idx)


def _routed_attn_kernel(th_ref, tj_ref, nt_ref, step_ref, q_ref, *refs, tiles):
    kv_refs, o_ref = refs[0:tiles], refs[tiles]
    t0 = pl.program_id(0) * tiles

    @pl.when(t0 < nt_ref[0])
    def _():
        qs = [q_ref[u * ROUTE_TILE:(u + 1) * ROUTE_TILE, :].astype(BF16) for u in range(tiles)]
        adds = [step_ref[t0 + u] * tj_ref[t0 + u].astype(F32) for u in range(tiles)]
        o_ref[...] = _block_rows(qs, kv_refs, None, adds)


def _routed_attn_call(tile_h, tile_j, n_tiles, tile_step, routed_q, kv,
                      *, split, tiles=ROUTE_TILES_PER_STEP):
    live = lambda s, th, tj, nt: (jnp.minimum(s, (nt[0] + tiles - 1) // tiles), 0)
    head0 = split * SPLIT_HEADS

    def kv_spec(u):
        return pl.BlockSpec(
            (None, 2 * HEAD_PAD, MOBA_BLOCK),
            lambda s, th, tj, nt: (head0 + th[s * tiles + u], 0, tj[s * tiles + u]))

    grid_spec = pltpu.PrefetchScalarGridSpec(
        num_scalar_prefetch=3,
        grid=(MAX_TILES // tiles,),
        in_specs=([pl.BlockSpec(memory_space=pltpu.SMEM),
                   pl.BlockSpec((tiles * ROUTE_TILE, HEAD_PAD), live)]
                  + [kv_spec(u) for u in range(tiles)]),
        out_specs=pl.BlockSpec((tiles * ROUTE_TILE, HEAD_PAD), live),
    )
    return pl.pallas_call(
        functools.partial(_routed_attn_kernel, tiles=tiles),
        grid_spec=grid_spec,
        out_shape=jax.ShapeDtypeStruct((ROUTE_ROWS, HEAD_PAD), F32),
        compiler_params=_params(1),
        name=f"moba_routed_attn_{split}",
    )(tile_h, tile_j, n_tiles, tile_step, routed_q, *([kv] * tiles))


def _pm_out_kernel(slope_ref, x_ref, mod_ref, pooled_ref, *refs):
    own_refs, got_refs = refs[0:N_SPLITS], refs[N_SPLITS:2 * N_SPLITS]
    wp_ref, wa_ref, o_ref = refs[2 * N_SPLITS:]
    own = pl.program_id(0)
    low = lax.broadcasted_iota(I32, (MOBA_BLOCK, HEAD_PAD), 1) < HEAD_DIM

    def lse_of(part):
        return jnp.where(low, pltpu.roll(part, HEAD_DIM, axis=1), part)

    y = jnp.dot(pooled_ref[...], wp_ref[...], preferred_element_type=F32)
    for hh in range(D_HEADS):
        own_ref, got_ref, a = own_refs[hh // SPLIT_HEADS], got_refs[hh // SPLIT_HEADS], hh % SPLIT_HEADS
        own_shift = slope_ref[hh] * (own * MOBA_BLOCK).astype(F32)
        parts = [own_ref[a]]
        lses = [lse_of(parts[0])]
        for s in range(MOBA_TOPK):
            valid = s < own
            part = got_ref[a, s]
            parts.append(jnp.where(valid, part, 0.0))
            lses.append(jnp.where(valid, lse_of(part) - own_shift, NEG_BIG))
        top = functools.reduce(jnp.maximum, lses)
        num = jnp.zeros((MOBA_BLOCK, HEAD_PAD), F32)
        den = jnp.zeros((MOBA_BLOCK, HEAD_PAD), F32)
        for part, ls in zip(parts, lses):
            w = jnp.exp(ls - top)
            num = num + w * part
            den = den + w
        att = (num / den).astype(BF16)
        y = y + jnp.dot(att, wa_ref[hh], preferred_element_type=F32)
    o_ref[...] = x_ref[...] + mod_ref[2:3, :] * y


def _pm_out_call(slopes, x, mod, pooled, own_parts, got_parts, wp, wa, *, layer):
    tm = MOBA_BLOCK
    return pl.pallas_call(
        _pm_out_kernel,
        grid=(SEQ // tm,),
        in_specs=[
            pl.BlockSpec(memory_space=pltpu.SMEM),
            pl.BlockSpec((tm, D_MODEL), lambda i: (i, 0)),
            _resident((None, None, 3, D_MODEL), lambda i: (layer, 1, 0, 0)),
            pl.BlockSpec((tm, C_WIDTH), lambda i: (i, 0)),
        ] + [pl.BlockSpec((SPLIT_HEADS, tm, HEAD_PAD), lambda i: (0, i, 0))] * N_SPLITS
        + [pl.BlockSpec((SPLIT_HEADS, MOBA_TOPK, tm, HEAD_PAD), lambda i: (0, 0, i, 0))] * N_SPLITS
        + [
            _full((C_WIDTH, D_MODEL)),
            _full((D_HEADS, HEAD_PAD, D_MODEL)),
        ],
        out_specs=pl.BlockSpec((tm, D_MODEL), lambda i: (i, 0)),
        out_shape=jax.ShapeDtypeStruct((SEQ, D_MODEL), F32),
        compiler_params=_params(1),
        name="pool_moba_out",
    )(slopes, x, mod, pooled, *own_parts, *got_parts, wp, wa)


def _head_pad(w, axis):
    shape = list(w.shape)
    shape[axis:axis + 1] = [D_HEADS, HEAD_DIM]
    pad = [(0, 0)] * len(shape)
    pad[axis + 1] = (0, HEAD_PAD - HEAD_DIM)
    out = jnp.pad(w.reshape(shape), pad)
    shape[axis:axis + 2] = [D_HEADS * HEAD_PAD]
    return out.reshape(shape)


def kernel(x, c, ada_w, ada_b, ffn_norm, ffn_w_gate, ffn_w_up, ffn_w_down, mix_norm, conv_w_in,
           conv_a_w, conv_a_b, conv_a_ln_g, conv_a_ln_b, conv_b_w, conv_w_out, pm_w_in, pool_w,
           pool_b, pool_scale, pm_w_out, final_norm):
    assert x.shape == (1, SEQ, D_MODEL) and c.shape == (1, D_MODEL)
    xs = x.reshape(SEQ, D_MODEL)
    mod = _mod_call(c, ada_w, ada_b)

    wg = ffn_w_gate.astype(BF16)
    wu = ffn_w_up.astype(BF16)
    wd = ffn_w_down.astype(BF16)
    ffn_g = ffn_norm.reshape(DEPTH, 2, 1, D_MODEL)
    mix_g = mix_norm.reshape(DEPTH, 1, D_MODEL)
    fin = final_norm.reshape(1, D_MODEL)
    ffn = functools.partial(_ffn_call, mod=mod, norm_g=ffn_g, wg=wg, wu=wu, wd=wd, fin=fin)

    xs = ffn(xs, layer=0, sub=0, final=False)
    grp = np.arange(A_WIDTH) // (A_WIDTH // A_GROUPS)
    seg = jnp.asarray((grp[:, None] == grp[None, :]) / (A_WIDTH // A_GROUPS), dtype=BF16)
    xs = _conv_mix_call(
        xs, mod, mix_g, conv_w_in[0].astype(BF16), conv_a_w[0], conv_a_b[0].reshape(1, A_WIDTH),
        conv_a_ln_g[0].reshape(1, A_WIDTH), conv_a_ln_b[0].reshape(1, A_WIDTH), conv_b_w[0],
        conv_w_out[0].astype(BF16), seg, layer=0)
    xs = ffn(xs, layer=0, sub=1, final=False)

    xs = ffn(xs, layer=1, sub=0, final=False)
    w_in = pm_w_in[0].astype(BF16)
    w_u = w_in[:, 0:C_WIDTH]
    w_q = _head_pad(w_in[:, C_WIDTH:C_WIDTH + D_WIDTH], 1)
    w_kvt = w_in[:, C_WIDTH + D_WIDTH:C_WIDTH + 3 * D_WIDTH].T
    pooled, qrows, kv, kmean = _pm_proj_call(
        xs, mod, mix_g, w_u, w_q, w_kvt, pool_w[0].astype(BF16),
        pool_b[0].reshape(1, C_WIDTH), pool_scale[0].reshape(1, C_WIDTH), layer=1)
    kmean_h = _head_pad(kmean, 1).reshape(N_BLOCKS, D_HEADS, HEAD_PAD).transpose(1, 0, 2)

    n_pair = MOBA_TOPK * MOBA_BLOCK
    tri = jnp.asarray(np.arange(n_pair)[:, None] < np.arange(n_pair)[None, :], dtype=BF16)
    q_flat = qrows.reshape(D_HEADS * SEQ, HEAD_PAD)
    own_parts, got_parts = [], []
    for split in range(N_SPLITS):
        sel, rank, cnt, own_part = _gate_own_call(qrows, kv, kmean_h, tri, split=split)
        base_digits, tile_h, tile_j, n_tiles, tile_step = _route_tables(cnt, split)
        pos = _route_pos_call(sel, rank, base_digits)
        pos_by_slot = [pos[:, s, :].reshape(1, SPLIT_HEADS * SEQ) for s in range(MOBA_TOPK)]
        routed_q = _dispatch_call(q_flat, pos_by_slot, split=split)
        routed_parts = _routed_attn_call(tile_h, tile_j, n_tiles, tile_step, routed_q, kv,
                                         split=split)
        got = _collect_call(routed_parts, pos.reshape(1, N_PAIRS))
        own_parts.append(own_part)
        got_parts.append(got.reshape(SPLIT_HEADS, MOBA_TOPK, SEQ, HEAD_PAD))

    slopes = jnp.asarray(2.0 ** (-8.0 * np.arange(1, D_HEADS + 1) / D_HEADS), dtype=F32)
    w_out = pm_w_out[0].astype(BF16)
    w_att = _head_pad(w_out[C_WIDTH:], 0).reshape(D_HEADS, HEAD_PAD, D_MODEL)
    xs = _pm_out_call(slopes, xs, mod, pooled, own_parts, got_parts, w_out[0:C_WIDTH], w_att,
                      layer=1)
    xs = ffn(xs, layer=1, sub=1, final=True)
    return xs.reshape(1, SEQ, D_MODEL)
```

```python
import functools

import numpy as np
import jax
import jax.numpy as jnp
from jax import lax
from jax.experimental import pallas as pl
from jax.experimental.pallas import tpu as pltpu
from jax.experimental.pallas import tpu_sc as plsc

D_MODEL = 1024
SEQ = 16384
DEPTH = 2
N_SUBLAYERS = 3
D_FF = 2816
EPS = 1e-6

A_WIDTH = 512
A_GROUPS = 8
A_CONV = 31
B_WIDTH = 512
B_CONV = 3

POOL_WINDOWS = (2, 4, 8, 16)
C_WIDTH = 512
C_GROUP_DIM = 128
D_HEADS = 8
HEAD_DIM = 64
D_WIDTH = 512
MOBA_BLOCK = 256
MOBA_TOPK = 3
N_BLOCKS = SEQ // MOBA_BLOCK

V7X_SUBLANES = 8
V7X_VMEM_LIMIT_BYTES = 56 * 1024 * 1024

A_HALO = 32
B_HALO = 8
POOL_HALO = 16
NEG_BIG = -1e30

HEAD_PAD = 128
ALIBI_COL = HEAD_DIM

SPLIT_HEADS = 4
N_SPLITS = D_HEADS // SPLIT_HEADS
ROUTE_TILE = 256
ROUTE_TILES_PER_STEP = 8
N_PAIRS = SPLIT_HEADS * SEQ * MOBA_TOPK
N_GROUPS = SPLIT_HEADS * N_BLOCKS
MAX_TILES = N_PAIRS // ROUTE_TILE + N_GROUPS
TRASH_ROW0 = MAX_TILES * ROUTE_TILE
ROUTE_ROWS = (MAX_TILES + ROUTE_TILES_PER_STEP) * ROUTE_TILE
SC_WINDOW = 128

BF16 = jnp.bfloat16
F32 = jnp.float32
I32 = jnp.int32


def _params(n_axes):
    return pltpu.CompilerParams(
        dimension_semantics=("arbitrary",) * n_axes,
        vmem_limit_bytes=V7X_VMEM_LIMIT_BYTES)


def _resident(block_shape, index_map):
    return pl.BlockSpec(block_shape, index_map, pipeline_mode=pl.Buffered(1))


def _full(shape):
    return _resident(shape, lambda *_: (0,) * len(shape))


def _sigmoid(v):
    return 1.0 / (1.0 + jnp.exp(-v))


def _norm_mod(x, g, mod):
    r = lax.rsqrt(jnp.mean(x * x, axis=-1, keepdims=True) + EPS)
    return (x * r) * (g * (1.0 + mod[1:2])) + mod[0:1]


def _split(v):
    hi = v.astype(BF16)
    return hi, (v - hi.astype(F32)).astype(BF16)


def _split_dot(v, w):
    hi, lo = _split(v)
    return (jnp.dot(hi, w, preferred_element_type=F32)
            + jnp.dot(lo, w, preferred_element_type=F32))


_NT = (((1,), (1,)), ((), ()))


def _mod_kernel(c_ref, w_ref, b_ref, o_ref):
    c = c_ref[...]
    cond = c * _sigmoid(c)
    o_ref[0] = jnp.sum(w_ref[0] * cond, axis=0, keepdims=True) + b_ref[0]


def _mod_call(c, ada_w, ada_b):
    n_out = N_SUBLAYERS * 3 * D_MODEL
    tn = D_MODEL
    out = pl.pallas_call(
        _mod_kernel,
        grid=(DEPTH, n_out // tn),
        in_specs=[
            pl.BlockSpec((D_MODEL, 1), lambda l, j: (0, 0)),
            pl.BlockSpec((1, D_MODEL, tn), lambda l, j: (l, 0, j)),
            pl.BlockSpec((1, 1, tn), lambda l, j: (l, 0, j)),
        ],
        out_specs=pl.BlockSpec((1, 1, tn), lambda l, j: (l, 0, j)),
        out_shape=jax.ShapeDtypeStruct((DEPTH, 1, n_out), F32),
        compiler_params=_params(2),
        name="adaln_mod",
    )(c.reshape(D_MODEL, 1), ada_w, ada_b.reshape(DEPTH, 1, n_out))
    return out.reshape(DEPTH, N_SUBLAYERS, 3, D_MODEL)


def _ffn_kernel(x_ref, mod_ref, g_ref, wg_ref, wu_ref, wd_ref, fin_ref, o_ref, *, final):
    x = x_ref[...]
    mod = mod_ref[...]
    h = _norm_mod(x, g_ref[...], mod).astype(BF16)
    gate = jnp.dot(h, wg_ref[...], preferred_element_type=F32)
    up = jnp.dot(h, wu_ref[...], preferred_element_type=F32)
    act = (gate * _sigmoid(gate) * up).astype(BF16)
    y = jnp.dot(act, wd_ref[...], preferred_element_type=F32)
    xn = x + (0.5 * mod[2:3]) * y
    if final:
        r = lax.rsqrt(jnp.mean(xn * xn, axis=-1, keepdims=True) + EPS)
        xn = xn * r * fin_ref[...]
    o_ref[...] = xn


def _ffn_call(x, mod, norm_g, wg, wu, wd, fin, *, layer, sub, final, tm=512):
    mod_sub = 0 if sub == 0 else 2
    return pl.pallas_call(
        functools.partial(_ffn_kernel, final=final),
        grid=(SEQ // tm,),
        in_specs=[
            pl.BlockSpec((tm, D_MODEL), lambda i: (i, 0)),
            _resident((None, None, 3, D_MODEL), lambda i: (layer, mod_sub, 0, 0)),
            _resident((None, None, 1, D_MODEL), lambda i: (layer, sub, 0, 0)),
            _resident((None, None, D_MODEL, D_FF), lambda i: (layer, sub, 0, 0)),
            _resident((None, None, D_MODEL, D_FF), lambda i: (layer, sub, 0, 0)),
            _resident((None, None, D_FF, D_MODEL), lambda i: (layer, sub, 0, 0)),
            _resident((1, D_MODEL), lambda i: (0, 0)),
        ],
        out_specs=pl.BlockSpec((tm, D_MODEL), lambda i: (i, 0)),
        out_shape=jax.ShapeDtypeStruct((SEQ, D_MODEL), F32),
        compiler_params=_params(1),
        name=f"ffn_l{layer}s{sub}",
    )(x, mod, norm_g, wg, wu, wd, fin)


def _conv_mix_kernel(x_ref, mod_ref, g_ref, win_ref, caw_ref, cab_ref, lng_ref, lnb_ref,
                     cbw_ref, wout_ref, seg_ref, o_ref, abuf, bbuf, sbuf, *, tm):
    i = pl.program_id(0)

    @pl.when(i == 0)
    def _():
        abuf[0:A_HALO, :] = jnp.zeros((A_HALO, A_WIDTH), F32)
        bbuf[0:B_HALO, :] = jnp.zeros((B_HALO, B_WIDTH), F32)

    x = x_ref[...]
    mod = mod_ref[...]
    h = _norm_mod(x, g_ref[...], mod).astype(BF16)
    z = jnp.dot(h, win_ref[...], preferred_element_type=F32)

    a = z[:, 0:A_WIDTH] * _sigmoid(z[:, A_WIDTH:2 * A_WIDTH])
    abuf[A_HALO:A_HALO + tm, :] = a
    acc = jnp.zeros((tm, A_WIDTH), F32) + cab_ref[...]
    ext = tm + V7X_SUBLANES
    for r in range(V7X_SUBLANES):
        part = None
        for q in range((A_CONV - 1 - r) // V7X_SUBLANES + 1):
            k = A_CONV - 1 - (V7X_SUBLANES * q + r)
            term = caw_ref[k:k + 1, :] * abuf[pl.ds(A_HALO - V7X_SUBLANES * (q + 1), ext), :]
            part = term if part is None else part + term
        if r == 0:
            acc = acc + part[V7X_SUBLANES:, :]
        else:
            sbuf[r - 1] = part
            acc = acc + sbuf[r - 1, pl.ds(V7X_SUBLANES - r, tm), :]
    abuf[0:A_HALO, :] = abuf[tm:tm + A_HALO, :]
    seg = seg_ref[...]
    mu = _split_dot(acc, seg)
    d = acc - mu
    var = _split_dot(d * d, seg)
    yn = d * lax.rsqrt(var + EPS) * lng_ref[...] + lnb_ref[...]
    a_out = yn * _sigmoid(yn)

    off = 2 * A_WIDTH
    cv = z[:, off + B_WIDTH:off + 2 * B_WIDTH] * z[:, off + 2 * B_WIDTH:off + 3 * B_WIDTH]
    bbuf[B_HALO:B_HALO + tm, :] = cv
    bacc = jnp.zeros((tm, B_WIDTH), F32)
    for k in range(B_CONV):
        bacc = bacc + cbw_ref[k:k + 1, :] * bbuf[pl.ds(B_HALO - (B_CONV - 1) + k, tm), :]
    bbuf[0:B_HALO, :] = bbuf[tm:tm + B_HALO, :]
    bb = z[:, off:off + B_WIDTH] * bacc

    y = (jnp.dot(a_out.astype(BF16), wout_ref[0:A_WIDTH, :], preferred_element_type=F32)
         + jnp.dot(bb.astype(BF16), wout_ref[A_WIDTH:A_WIDTH + B_WIDTH, :],
                   preferred_element_type=F32))
    o_ref[...] = x + mod[2:3] * y


def _conv_mix_call(x, mod, norm_g, w_in, caw, cab, lng, lnb, cbw, w_out, seg, *, layer, tm=512):
    even_in = w_in.shape[1]
    return pl.pallas_call(
        functools.partial(_conv_mix_kernel, tm=tm),
        grid=(SEQ // tm,),
        in_specs=[
            pl.BlockSpec((tm, D_MODEL), lambda i: (i, 0)),
            _resident((None, None, 3, D_MODEL), lambda i: (layer, 1, 0, 0)),
            _resident((None, 1, D_MODEL), lambda i: (layer, 0, 0)),
            _full((D_MODEL, even_in)),
            _full((A_CONV, A_WIDTH)),
            _full((1, A_WIDTH)),
            _full((1, A_WIDTH)),
            _full((1, A_WIDTH)),
            _full((B_CONV, B_WIDTH)),
            _full((A_WIDTH + B_WIDTH, D_MODEL)),
            _full((A_WIDTH, A_WIDTH)),
        ],
        out_specs=pl.BlockSpec((tm, D_MODEL), lambda i: (i, 0)),
        out_shape=jax.ShapeDtypeStruct((SEQ, D_MODEL), F32),
        scratch_shapes=[pltpu.VMEM((A_HALO + tm, A_WIDTH), F32),
                        pltpu.VMEM((B_HALO + tm, B_WIDTH), F32),
                        pltpu.VMEM((V7X_SUBLANES - 1, tm + V7X_SUBLANES, A_WIDTH), F32)],
        compiler_params=_params(1),
        name="conv_mixers",
    )(x, mod, norm_g, w_in, caw, cab, lng, lnb, cbw, w_out, seg)


def _pm_proj_kernel(x_ref, mod_ref, g_ref, wu_ref, wq_ref, wkvt_ref, pw_ref,
                    pb_ref, ps_ref, pooled_ref, qrow_ref, kv_ref, km_ref,
                    ubuf, hmean, *, tm):
    i = pl.program_id(0)
    hp = D_HEADS * HEAD_PAD

    @pl.when(i == 0)
    def _():
        ubuf[0:POOL_HALO, :] = jnp.zeros((POOL_HALO, C_WIDTH), F32)

    x = x_ref[...]
    h = _norm_mod(x, g_ref[...], mod_ref[...]).astype(BF16)
    u = jnp.dot(h, wu_ref[...], preferred_element_type=F32)
    qr = jnp.dot(h, wq_ref[...], preferred_element_type=F32)
    kvt = lax.dot_general(wkvt_ref[...], h, _NT, preferred_element_type=F32)

    ubuf[POOL_HALO:POOL_HALO + tm, :] = u
    t1 = (i * tm + 1 + lax.broadcasted_iota(I32, (tm, C_GROUP_DIM), 0)).astype(F32)
    for gi, w in enumerate(POOL_WINDOWS):
        c0 = gi * C_GROUP_DIM
        s = u[:, c0:c0 + C_GROUP_DIM]
        for k in range(1, w):
            s = s + ubuf[pl.ds(POOL_HALO - k, tm), c0:c0 + C_GROUP_DIM]
        pooled = s / jnp.minimum(t1, float(w)) - u[:, c0:c0 + C_GROUP_DIM]
        mixed = (jnp.dot(pooled.astype(BF16), pw_ref[gi], preferred_element_type=F32)
                 + pb_ref[:, c0:c0 + C_GROUP_DIM])
        pooled_ref[:, c0:c0 + C_GROUP_DIM] = (
            mixed * ps_ref[:, c0:c0 + C_GROUP_DIM]).astype(BF16)
    ubuf[0:POOL_HALO, :] = ubuf[tm:tm + POOL_HALO, :]

    scale = HEAD_DIM ** -0.5
    lane = lax.broadcasted_iota(I32, (tm, hp), 1) % HEAD_PAD
    qrows = jnp.where(lane == ALIBI_COL, 1.0, qr * scale)
    for hh in range(D_HEADS):
        qrow_ref[hh] = qrows[:, hh * HEAD_PAD:(hh + 1) * HEAD_PAD]
    per_tile = tm // MOBA_BLOCK
    kvb = kvt.astype(BF16)
    first = lax.broadcasted_iota(I32, (HEAD_DIM, MOBA_BLOCK), 0) == 0
    koff = lax.broadcasted_iota(I32, (HEAD_DIM, MOBA_BLOCK), 1).astype(F32)
    zeros = jnp.zeros((HEAD_DIM, MOBA_BLOCK), BF16)
    for hh in range(D_HEADS):
        alibi = jnp.where(first, koff * (2.0 ** -(hh + 1)), 0.0).astype(BF16)
        for sb in range(per_tile):
            keys = slice(sb * MOBA_BLOCK, (sb + 1) * MOBA_BLOCK)
            kv_ref[hh, sb, 0:HEAD_DIM, :] = kvb[hh * HEAD_DIM:(hh + 1) * HEAD_DIM, keys]
            kv_ref[hh, sb, HEAD_DIM:HEAD_PAD, :] = alibi
            kv_ref[hh, sb, HEAD_PAD:HEAD_PAD + HEAD_DIM, :] = kvb[
                D_WIDTH + hh * HEAD_DIM:D_WIDTH + (hh + 1) * HEAD_DIM, keys]
            kv_ref[hh, sb, HEAD_PAD + HEAD_DIM:2 * HEAD_PAD, :] = zeros

    for sb in range(per_tile):
        blk = h[sb * MOBA_BLOCK:(sb + 1) * MOBA_BLOCK, :].astype(F32)
        hmean[pl.ds(i * per_tile + sb, 1), :] = jnp.mean(blk, axis=0, keepdims=True)

    @pl.when(i == pl.num_programs(0) - 1)
    def _():
        hi, lo = _split(hmean[...])
        wkt = wkvt_ref[0:D_WIDTH, :]
        km_ref[...] = (lax.dot_general(hi, wkt, _NT, preferred_element_type=F32)
                       + lax.dot_general(lo, wkt, _NT, preferred_element_type=F32))


def _pm_proj_call(x, mod, norm_g, wu, wq, wkvt, pw, pb, ps, *, layer, tm=512):
    hp = D_HEADS * HEAD_PAD
    return pl.pallas_call(
        functools.partial(_pm_proj_kernel, tm=tm),
        grid=(SEQ // tm,),
        in_specs=[
            pl.BlockSpec((tm, D_MODEL), lambda i: (i, 0)),
            _resident((None, None, 3, D_MODEL), lambda i: (layer, 1, 0, 0)),
            _resident((None, 1, D_MODEL), lambda i: (layer, 0, 0)),
            _full((D_MODEL, C_WIDTH)),
            _full((D_MODEL, hp)),
            _full((2 * D_WIDTH, D_MODEL)),
            _full((len(POOL_WINDOWS), C_GROUP_DIM, C_GROUP_DIM)),
            _full((1, C_WIDTH)),
            _full((1, C_WIDTH)),
        ],
        out_specs=[
            pl.BlockSpec((tm, C_WIDTH), lambda i: (i, 0)),
            pl.BlockSpec((D_HEADS, tm, HEAD_PAD), lambda i: (0, i, 0)),
            pl.BlockSpec((D_HEADS, tm // MOBA_BLOCK, 2 * HEAD_PAD, MOBA_BLOCK),
                         lambda i: (0, i, 0, 0)),
            pl.BlockSpec((N_BLOCKS, D_WIDTH), lambda i: (0, 0)),
        ],
        out_shape=[
            jax.ShapeDtypeStruct((SEQ, C_WIDTH), BF16),
            jax.ShapeDtypeStruct((D_HEADS, SEQ, HEAD_PAD), F32),
            jax.ShapeDtypeStruct((D_HEADS, N_BLOCKS, 2 * HEAD_PAD, MOBA_BLOCK), BF16),
            jax.ShapeDtypeStruct((N_BLOCKS, D_WIDTH), F32),
        ],
        scratch_shapes=[pltpu.VMEM((POOL_HALO + tm, C_WIDTH), F32),
                        pltpu.VMEM((N_BLOCKS, D_MODEL), F32)],
        compiler_params=_params(1),
        name="pool_qkv_proj",
    )(x, mod, norm_g, wu, wq, wkvt, pw, pb, ps)


def _block_rows(qs, slabs, mask, lse_add):
    n = qs[0].shape[0]
    s = jnp.concatenate(
        [jnp.dot(q, slab[0:HEAD_PAD, :], preferred_element_type=F32)
         for q, slab in zip(qs, slabs)], axis=0)
    if mask is not None:
        s = jnp.where(mask, s, NEG_BIG)
    m = jnp.max(s, axis=1, keepdims=True)
    p = jnp.exp(s - m).astype(BF16)
    ones = jnp.ones((HEAD_PAD, MOBA_BLOCK), BF16)
    ot = jnp.concatenate(
        [lax.dot_general(p[u * n:(u + 1) * n],
                         jnp.concatenate([slab[HEAD_PAD:2 * HEAD_PAD, :], ones], axis=0), _NT,
                         preferred_element_type=F32)
         for u, slab in enumerate(slabs)], axis=0)
    o, tot = ot[:, 0:HEAD_PAD], ot[:, HEAD_PAD:2 * HEAD_PAD]
    lse = m + jnp.log(tot)
    if lse_add is not None:
        lse = jnp.concatenate(
            [lse[u * n:(u + 1) * n] + add for u, add in enumerate(lse_add)], axis=0)
    low = lax.broadcasted_iota(I32, o.shape, 1) < HEAD_DIM
    return jnp.where(low, o / tot, lse)


def _onehot_pairs(sel_rows):
    blk = lax.broadcasted_iota(I32, (N_BLOCKS, MOBA_BLOCK), 0)
    return jnp.concatenate(
        [jnp.where(blk == sel_rows[s], 1.0, 0.0) for s in range(MOBA_TOPK)], axis=1)


def _gate_own_kernel(qrow_ref, kv_ref, km_ref, tri_ref,
                     sel_ref, rank_ref, cnt_ref, own_ref, *, heads):
    own = pl.program_id(0)
    n_pair = MOBA_TOPK * MOBA_BLOCK
    qs = [qrow_ref[a].astype(BF16) for a in range(heads)]

    gates = []
    for a in range(heads):
        km_hi, km_lo = _split(km_ref[a])
        gates.append(lax.dot_general(km_hi, qs[a], _NT, preferred_element_type=F32)
                     + lax.dot_general(km_lo, qs[a], _NT, preferred_element_type=F32))
    blk = lax.broadcasted_iota(I32, (N_BLOCKS, heads * MOBA_BLOCK), 0)
    g = jnp.where(blk < own, jnp.concatenate(gates, axis=1), -jnp.inf)
    slots = []
    for s in range(MOBA_TOPK):
        best = jnp.max(g, axis=0, keepdims=True)
        idx = jnp.min(jnp.where(g == best, blk, N_BLOCKS), axis=0, keepdims=True)
        g = jnp.where(blk == idx, -jnp.inf, g)
        slots.append(jnp.where(s < own, idx, N_BLOCKS))

    onehots = []
    for a in range(heads):
        sel_rows = [slot[:, a * MOBA_BLOCK:(a + 1) * MOBA_BLOCK] for slot in slots]
        sel_ref[a] = jnp.concatenate(sel_rows, axis=0)
        onehots.append(_onehot_pairs(sel_rows))
    onehot = jnp.concatenate(onehots, axis=0)
    oh = onehot.astype(BF16)
    before = jnp.dot(oh, tri_ref[...], preferred_element_type=F32)
    hit = onehot * before
    ones = jnp.ones((8, n_pair), BF16)
    cnt = lax.dot_general(ones, oh, _NT, preferred_element_type=F32)
    for a in range(heads):
        rank = jnp.sum(hit[a * N_BLOCKS:(a + 1) * N_BLOCKS], axis=0, keepdims=True).astype(I32)
        rank_ref[a] = jnp.concatenate(
            [rank[:, s * MOBA_BLOCK:(s + 1) * MOBA_BLOCK] for s in range(MOBA_TOPK)], axis=0)
        cnt_ref[a] = cnt[0:1, a * N_BLOCKS:(a + 1) * N_BLOCKS].astype(I32)

    qi = lax.broadcasted_iota(I32, (heads * MOBA_BLOCK, MOBA_BLOCK), 0) % MOBA_BLOCK
    ki = lax.broadcasted_iota(I32, (heads * MOBA_BLOCK, MOBA_BLOCK), 1)
    rows = _block_rows(qs, [kv_ref.at[a] for a in range(heads)], ki <= qi, None)
    for a in range(heads):
        own_ref[a] = rows[a * MOBA_BLOCK:(a + 1) * MOBA_BLOCK]


def _gate_own_call(qrows, kv, kmean_h, tri, *, split):
    heads = SPLIT_HEADS
    return pl.pallas_call(
        functools.partial(_gate_own_kernel, heads=heads),
        grid=(N_BLOCKS,),
        in_specs=[
            pl.BlockSpec((heads, MOBA_BLOCK, HEAD_PAD), lambda b: (split, b, 0)),
            pl.BlockSpec((heads, None, 2 * HEAD_PAD, MOBA_BLOCK), lambda b: (split, b, 0, 0)),
            pl.BlockSpec((heads, N_BLOCKS, HEAD_PAD), lambda b: (split, 0, 0)),
            _full((MOBA_TOPK * MOBA_BLOCK, MOBA_TOPK * MOBA_BLOCK)),
        ],
        out_specs=[
            pl.BlockSpec((heads, MOBA_TOPK, MOBA_BLOCK), lambda b: (0, 0, b)),
            pl.BlockSpec((heads, MOBA_TOPK, MOBA_BLOCK), lambda b: (0, 0, b)),
            pl.BlockSpec((heads, None, 1, N_BLOCKS), lambda b: (0, b, 0, 0)),
            pl.BlockSpec((heads, MOBA_BLOCK, HEAD_PAD), lambda b: (0, b, 0)),
        ],
        out_shape=[
            jax.ShapeDtypeStruct((heads, MOBA_TOPK, SEQ), I32),
            jax.ShapeDtypeStruct((heads, MOBA_TOPK, SEQ), I32),
            jax.ShapeDtypeStruct((heads, N_BLOCKS, 1, N_BLOCKS), I32),
            jax.ShapeDtypeStruct((heads, SEQ, HEAD_PAD), F32),
        ],
        compiler_params=_params(1),
        name=f"moba_gate_own_{split}",
    )(qrows, kv, kmean_h, tri)


def _route_pos_kernel(sel_ref, rank_ref, base_ref, pos_ref):
    lane = lax.broadcasted_iota(I32, (1, MOBA_BLOCK), 1)
    rows = [[] for _ in range(MOBA_TOPK)]
    for a in range(SPLIT_HEADS):
        sel = sel_ref[a]
        onehot = _onehot_pairs([sel[s:s + 1, :] for s in range(MOBA_TOPK)]).astype(BF16)
        dig = jnp.dot(base_ref[a].astype(BF16), onehot, preferred_element_type=F32)
        base = (dig[0:1] * 16384.0 + dig[1:2] * 128.0 + dig[2:3]).astype(I32)
        for s in range(MOBA_TOPK):
            p = base[:, s * MOBA_BLOCK:(s + 1) * MOBA_BLOCK] + rank_ref[a, s:s + 1, :]
            rows[s].append(
                jnp.where(sel[s:s + 1, :] < N_BLOCKS, p, TRASH_ROW0 + lane % SC_WINDOW))
    for s in range(MOBA_TOPK):
        pos_ref[s] = jnp.concatenate(rows[s], axis=0)


def _route_pos_call(sel, rank, base_digits):
    return pl.pallas_call(
        _route_pos_kernel,
        grid=(N_BLOCKS,),
        in_specs=[
            pl.BlockSpec((SPLIT_HEADS, MOBA_TOPK, MOBA_BLOCK), lambda b: (0, 0, b)),
            pl.BlockSpec((SPLIT_HEADS, MOBA_TOPK, MOBA_BLOCK), lambda b: (0, 0, b)),
            pl.BlockSpec((SPLIT_HEADS, None, 8, N_BLOCKS), lambda b: (0, b, 0, 0)),
        ],
        out_specs=pl.BlockSpec((MOBA_TOPK, SPLIT_HEADS, MOBA_BLOCK), lambda b: (0, 0, b)),
        out_shape=jax.ShapeDtypeStruct((MOBA_TOPK, SPLIT_HEADS, SEQ), I32),
        compiler_params=_params(1),
        name="moba_route_pos",
    )(sel, rank, base_digits)


def _route_tables(cnt, split):
    cnt = cnt.reshape(SPLIT_HEADS, N_BLOCKS, N_BLOCKS)
    tiles = (cnt.sum(axis=1) + ROUTE_TILE - 1) // ROUTE_TILE
    tiles_flat = tiles.reshape(N_GROUPS)
    tile0 = jnp.cumsum(tiles_flat) - tiles_flat
    n_tiles = tiles_flat.sum().astype(I32)
    within = jnp.cumsum(cnt, axis=1) - cnt
    base = tile0.reshape(SPLIT_HEADS, 1, N_BLOCKS) * ROUTE_TILE + within
    digits = jnp.stack([base // 16384, (base // 128) % 128, base % 128], axis=2).astype(F32)
    digits = jnp.pad(digits, ((0, 0), (0, 0), (0, 8 - 3), (0, 0)))
    t = jnp.arange(MAX_TILES, dtype=I32)
    ended = (tile0 + tiles_flat)[None, :] <= t[:, None]
    group_of_tile = jnp.minimum(ended.sum(axis=1).astype(I32), N_GROUPS - 1)
    tile_h = group_of_tile // N_BLOCKS
    head = split * SPLIT_HEADS + tile_h
    tile_step = lax.bitcast_convert_type((127 + 7 - head) << 23, F32)
    return digits, tile_h, group_of_tile % N_BLOCKS, n_tiles.reshape(1), tile_step


def _sc_mesh():
    return plsc.VectorSubcoreMesh(core_axis_name="core", subcore_axis_name="subcore")


def _dispatch_call(rows, pos_by_slot, *, split):
    n_rows = SPLIT_HEADS * SEQ
    first_window = split * n_rows // SC_WINDOW

    @functools.partial(
        pl.kernel, mesh=_sc_mesh(), scratch_types=[],
        out_type=jax.ShapeDtypeStruct((ROUTE_ROWS, HEAD_PAD), F32))
    def dispatch(x_hbm, i0_hbm, i1_hbm, i2_hbm, o_hbm):
        def body(x_vmem, i0_vmem, i1_vmem, i2_vmem):
            pltpu.sync_copy(x_vmem, o_hbm.at[i0_vmem.at[0]])
            pltpu.sync_copy(x_vmem, o_hbm.at[i1_vmem.at[0]])
            pltpu.sync_copy(x_vmem, o_hbm.at[i2_vmem.at[0]])

        idx_spec = pl.BlockSpec((1, SC_WINDOW), lambda i: (0, i))
        pltpu.emit_pipeline(
            body,
            grid=(n_rows // SC_WINDOW,),
            in_specs=[pl.BlockSpec((SC_WINDOW, HEAD_PAD), lambda i: (first_window + i, 0)),
                      idx_spec, idx_spec, idx_spec],
            out_specs=[],
            core_axis_name=("core", "subcore"),
            dimension_semantics=(pltpu.PARALLEL,),
        )(x_hbm, i0_hbm, i1_hbm, i2_hbm)

    return dispatch(rows, *pos_by_slot)


def _collect_call(table, idx):
    n_rows = idx.shape[1]

    @functools.partial(
        pl.kernel, mesh=_sc_mesh(), scratch_types=[],
        out_type=jax.ShapeDtypeStruct((n_rows, HEAD_PAD), F32))
    def collect(x_hbm, i_hbm, o_hbm):
        def body(i_vmem, o_vmem):
            pltpu.sync_copy(x_hbm.at[i_vmem.at[0]], o_vmem)

        pltpu.emit_pipeline(
            body,
            grid=(n_rows // SC_WINDOW,),
            in_specs=[pl.BlockSpec((1, SC_WINDOW), lambda i: (0, i))],
            out_specs=[pl.BlockSpec((SC_WINDOW, HEAD_PAD), lambda i: (i, 0))],
            core_axis_name=("core", "subcore"),
            dimension_semantics=(pltpu.PARALLEL,),
        )(i_hbm, o_hbm)

    return collect(table, idx)


def _routed_attn_kernel(th_ref, tj_ref, nt_ref, step_ref, q_ref, *refs, tiles):
    kv_refs, o_ref = refs[0:tiles], refs[tiles]
    t0 = pl.program_id(0) * tiles

    @pl.when(t0 < nt_ref[0])
    def _():
        qs = [q_ref[u * ROUTE_TILE:(u + 1) * ROUTE_TILE, :].astype(BF16) for u in range(tiles)]
        adds = [step_ref[t0 + u] * tj_ref[t0 + u].astype(F32) for u in range(tiles)]
        o_ref[...] = _block_rows(qs, kv_refs, None, adds)


def _routed_attn_call(tile_h, tile_j, n_tiles, tile_step, routed_q, kv,
                      *, split, tiles=ROUTE_TILES_PER_STEP):
    live = lambda s, th, tj, nt: (jnp.minimum(s, (nt[0] + tiles - 1) // tiles), 0)
    head0 = split * SPLIT_HEADS

    def kv_spec(u):
        return pl.BlockSpec(
            (None, None, 2 * HEAD_PAD, MOBA_BLOCK),
            lambda s, th, tj, nt: (head0 + th[s * tiles + u], tj[s * tiles + u], 0, 0))

    grid_spec = pltpu.PrefetchScalarGridSpec(
        num_scalar_prefetch=3,
        grid=(MAX_TILES // tiles,),
        in_specs=([pl.BlockSpec(memory_space=pltpu.SMEM),
                   pl.BlockSpec((tiles * ROUTE_TILE, HEAD_PAD), live)]
                  + [kv_spec(u) for u in range(tiles)]),
        out_specs=pl.BlockSpec((tiles * ROUTE_TILE, HEAD_PAD), live),
    )
    return pl.pallas_call(
        functools.partial(_routed_attn_kernel, tiles=tiles),
        grid_spec=grid_spec,
        out_shape=jax.ShapeDtypeStruct((ROUTE_ROWS, HEAD_PAD), F32),
        compiler_params=_params(1),
        name=f"moba_routed_attn_{split}",
    )(tile_h, tile_j, n_tiles, tile_step, routed_q, *([kv] * tiles))


def _pm_out_kernel(slope_ref, x_ref, mod_ref, pooled_ref, *refs):
    own_refs, got_refs = refs[0:N_SPLITS], refs[N_SPLITS:2 * N_SPLITS]
    wp_ref, wa_ref, o_ref = refs[2 * N_SPLITS:]
    own = pl.program_id(0)
    low = lax.broadcasted_iota(I32, (MOBA_BLOCK, HEAD_PAD), 1) < HEAD_DIM

    def lse_of(part):
        return jnp.where(low, pltpu.roll(part, HEAD_DIM, axis=1), part)

    y = jnp.dot(pooled_ref[...], wp_ref[...], preferred_element_type=F32)
    for hh in range(D_HEADS):
        own_ref, got_ref, a = own_refs[hh // SPLIT_HEADS], got_refs[hh // SPLIT_HEADS], hh % SPLIT_HEADS
        own_shift = slope_ref[hh] * (own * MOBA_BLOCK).astype(F32)
        parts = [own_ref[a]]
        lses = [lse_of(parts[0])]
        for s in range(MOBA_TOPK):
            valid = s < own
            part = got_ref[s, a]
            parts.append(jnp.where(valid, part, 0.0))
            lses.append(jnp.where(valid, lse_of(part) - own_shift, NEG_BIG))
        top = functools.reduce(jnp.maximum, lses)
        num = jnp.zeros((MOBA_BLOCK, HEAD_PAD), F32)
        den = jnp.zeros((MOBA_BLOCK, HEAD_PAD), F32)
        for part, ls in zip(parts, lses):
            w = jnp.exp(ls - top)
            num = num + w * part
            den = den + w
        att = (num / den).astype(BF16)
        y = y + jnp.dot(att, wa_ref[hh], preferred_element_type=F32)
    o_ref[...] = x_ref[...] + mod_ref[2:3, :] * y


def _pm_out_call(slopes, x, mod, pooled, own_parts, got_parts, wp, wa, *, layer):
    tm = MOBA_BLOCK
    return pl.pallas_call(
        _pm_out_kernel,
        grid=(SEQ // tm,),
        in_specs=[
            pl.BlockSpec(memory_space=pltpu.SMEM),
            pl.BlockSpec((tm, D_MODEL), lambda i: (i, 0)),
            _resident((None, None, 3, D_MODEL), lambda i: (layer, 1, 0, 0)),
            pl.BlockSpec((tm, C_WIDTH), lambda i: (i, 0)),
        ] + [pl.BlockSpec((SPLIT_HEADS, tm, HEAD_PAD), lambda i: (0, i, 0))] * N_SPLITS
        + [pl.BlockSpec((MOBA_TOPK, SPLIT_HEADS, tm, HEAD_PAD), lambda i: (0, 0, i, 0))] * N_SPLITS
        + [
            _full((C_WIDTH, D_MODEL)),
            _full((D_HEADS, HEAD_PAD, D_MODEL)),
        ],
        out_specs=pl.BlockSpec((tm, D_MODEL), lambda i: (i, 0)),
        out_shape=jax.ShapeDtypeStruct((SEQ, D_MODEL), F32),
        compiler_params=_params(1),
        name="pool_moba_out",
    )(slopes, x, mod, pooled, *own_parts, *got_parts, wp, wa)


def _head_pad(w, axis):
    shape = list(w.shape)
    shape[axis:axis + 1] = [D_HEADS, HEAD_DIM]
    pad = [(0, 0)] * len(shape)
    pad[axis + 1] = (0, HEAD_PAD - HEAD_DIM)
    out = jnp.pad(w.reshape(shape), pad)
    shape[axis:axis + 2] = [D_HEADS * HEAD_PAD]
    return out.reshape(shape)


def kernel(x, c, ada_w, ada_b, ffn_norm, ffn_w_gate, ffn_w_up, ffn_w_down, mix_norm, conv_w_in,
           conv_a_w, conv_a_b, conv_a_ln_g, conv_a_ln_b, conv_b_w, conv_w_out, pm_w_in, pool_w,
           pool_b, pool_scale, pm_w_out, final_norm):
    assert x.shape == (1, SEQ, D_MODEL) and c.shape == (1, D_MODEL)
    xs = x.reshape(SEQ, D_MODEL)
    mod = _mod_call(c, ada_w, ada_b)

    wg = ffn_w_gate.astype(BF16)
    wu = ffn_w_up.astype(BF16)
    wd = ffn_w_down.astype(BF16)
    ffn_g = ffn_norm.reshape(DEPTH, 2, 1, D_MODEL)
    mix_g = mix_norm.reshape(DEPTH, 1, D_MODEL)
    fin = final_norm.reshape(1, D_MODEL)
    ffn = functools.partial(_ffn_call, mod=mod, norm_g=ffn_g, wg=wg, wu=wu, wd=wd, fin=fin)

    xs = ffn(xs, layer=0, sub=0, final=False)
    grp = np.arange(A_WIDTH) // (A_WIDTH // A_GROUPS)
    seg = jnp.asarray((grp[:, None] == grp[None, :]) / (A_WIDTH // A_GROUPS), dtype=BF16)
    xs = _conv_mix_call(
        xs, mod, mix_g, conv_w_in[0].astype(BF16), conv_a_w[0], conv_a_b[0].reshape(1, A_WIDTH),
        conv_a_ln_g[0].reshape(1, A_WIDTH), conv_a_ln_b[0].reshape(1, A_WIDTH), conv_b_w[0],
        conv_w_out[0].astype(BF16), seg, layer=0)
    xs = ffn(xs, layer=0, sub=1, final=False)

    xs = ffn(xs, layer=1, sub=0, final=False)
    w_in = pm_w_in[0].astype(BF16)
    w_u = w_in[:, 0:C_WIDTH]
    w_q = _head_pad(w_in[:, C_WIDTH:C_WIDTH + D_WIDTH], 1)
    w_kvt = w_in[:, C_WIDTH + D_WIDTH:C_WIDTH + 3 * D_WIDTH].T
    pooled, qrows, kv, kmean = _pm_proj_call(
        xs, mod, mix_g, w_u, w_q, w_kvt, pool_w[0].astype(BF16),
        pool_b[0].reshape(1, C_WIDTH), pool_scale[0].reshape(1, C_WIDTH), layer=1)
    kmean_h = _head_pad(kmean, 1).reshape(N_BLOCKS, D_HEADS, HEAD_PAD).transpose(1, 0, 2)

    n_pair = MOBA_TOPK * MOBA_BLOCK
    tri = jnp.asarray(np.arange(n_pair)[:, None] < np.arange(n_pair)[None, :], dtype=BF16)
    q_flat = qrows.reshape(D_HEADS * SEQ, HEAD_PAD)
    own_parts, got_parts = [], []
    for split in range(N_SPLITS):
        sel, rank, cnt, own_part = _gate_own_call(qrows, kv, kmean_h, tri, split=split)
        base_digits, tile_h, tile_j, n_tiles, tile_step = _route_tables(cnt, split)
        pos = _route_pos_call(sel, rank, base_digits)
        pos_by_slot = [pos[s].reshape(1, SPLIT_HEADS * SEQ) for s in range(MOBA_TOPK)]
        routed_q = _dispatch_call(q_flat, pos_by_slot, split=split)
        routed_parts = _routed_attn_call(tile_h, tile_j, n_tiles, tile_step, routed_q, kv,
                                         split=split)
        got = _collect_call(routed_parts, pos.reshape(1, N_PAIRS))
        own_parts.append(own_part)
        got_parts.append(got.reshape(MOBA_TOPK, SPLIT_HEADS, SEQ, HEAD_PAD))

    slopes = jnp.asarray(2.0 ** (-8.0 * np.arange(1, D_HEADS + 1) / D_HEADS), dtype=F32)
    w_out = pm_w_out[0].astype(BF16)
    w_att = _head_pad(w_out[C_WIDTH:], 0).reshape(D_HEADS, HEAD_PAD, D_MODEL)
    xs = _pm_out_call(slopes, xs, mod, pooled, own_parts, got_parts, w_out[0:C_WIDTH], w_att,
                      layer=1)
    xs = ffn(xs, layer=1, sub=1, final=True)
    return xs.reshape(1, SEQ, D_MODEL)
```

```python
import functools

import numpy as np
import jax
import jax.numpy as jnp
from jax import lax
from jax.experimental import pallas as pl
from jax.experimental.pallas import tpu as pltpu
from jax.experimental.pallas import tpu_sc as plsc

D_MODEL = 1024
SEQ = 16384
DEPTH = 2
N_SUBLAYERS = 3
D_FF = 2816
EPS = 1e-6

A_WIDTH = 512
A_GROUPS = 8
A_CONV = 31
B_WIDTH = 512
B_CONV = 3

POOL_WINDOWS = (2, 4, 8, 16)
C_WIDTH = 512
C_GROUP_DIM = 128
D_HEADS = 8
HEAD_DIM = 64
D_WIDTH = 512
MOBA_BLOCK = 256
MOBA_TOPK = 3
N_BLOCKS = SEQ // MOBA_BLOCK

V7X_SUBLANES = 8
V7X_VMEM_LIMIT_BYTES = 56 * 1024 * 1024

A_HALO = 32
B_HALO = 8
POOL_HALO = 16
NEG_BIG = -1e30

HEAD_PAD = 128
ALIBI_COL = HEAD_DIM

SPLIT_HEADS = 4
N_SPLITS = D_HEADS // SPLIT_HEADS
ROUTE_TILE = 256
ROUTE_TILES_PER_STEP = 8
N_PAIRS = SPLIT_HEADS * SEQ * MOBA_TOPK
N_GROUPS = SPLIT_HEADS * N_BLOCKS
MAX_TILES = N_PAIRS // ROUTE_TILE + N_GROUPS
TRASH_ROW0 = MAX_TILES * ROUTE_TILE
ROUTE_ROWS = (MAX_TILES + ROUTE_TILES_PER_STEP) * ROUTE_TILE
SC_WINDOW = 128

BF16 = jnp.bfloat16
F32 = jnp.float32
I32 = jnp.int32


def _params(n_axes):
    return pltpu.CompilerParams(
        dimension_semantics=("arbitrary",) * n_axes,
        vmem_limit_bytes=V7X_VMEM_LIMIT_BYTES)


def _resident(block_shape, index_map):
    return pl.BlockSpec(block_shape, index_map, pipeline_mode=pl.Buffered(1))


def _full(shape):
    return _resident(shape, lambda *_: (0,) * len(shape))


def _sigmoid(v):
    return 1.0 / (1.0 + jnp.exp(-v))


def _norm_mod(x, g, mod):
    r = lax.rsqrt(jnp.mean(x * x, axis=-1, keepdims=True) + EPS)
    return (x * r) * (g * (1.0 + mod[1:2])) + mod[0:1]


def _split(v):
    hi = v.astype(BF16)
    return hi, (v - hi.astype(F32)).astype(BF16)


def _split_dot(v, w):
    hi, lo = _split(v)
    return (jnp.dot(hi, w, preferred_element_type=F32)
            + jnp.dot(lo, w, preferred_element_type=F32))


_NT = (((1,), (1,)), ((), ()))


def _mod_kernel(c_ref, w_ref, b_ref, o_ref):
    c = c_ref[...]
    cond = c * _sigmoid(c)
    o_ref[0] = jnp.sum(w_ref[0] * cond, axis=0, keepdims=True) + b_ref[0]


def _mod_call(c, ada_w, ada_b):
    n_out = N_SUBLAYERS * 3 * D_MODEL
    tn = D_MODEL
    out = pl.pallas_call(
        _mod_kernel,
        grid=(DEPTH, n_out // tn),
        in_specs=[
            pl.BlockSpec((D_MODEL, 1), lambda l, j: (0, 0)),
            pl.BlockSpec((1, D_MODEL, tn), lambda l, j: (l, 0, j)),
            pl.BlockSpec((1, 1, tn), lambda l, j: (l, 0, j)),
        ],
        out_specs=pl.BlockSpec((1, 1, tn), lambda l, j: (l, 0, j)),
        out_shape=jax.ShapeDtypeStruct((DEPTH, 1, n_out), F32),
        compiler_params=_params(2),
        name="adaln_mod",
    )(c.reshape(D_MODEL, 1), ada_w, ada_b.reshape(DEPTH, 1, n_out))
    return out.reshape(DEPTH, N_SUBLAYERS, 3, D_MODEL)


def _ffn_kernel(x_ref, mod_ref, g_ref, wg_ref, wu_ref, wd_ref, fin_ref, o_ref, *, final):
    x = x_ref[...]
    mod = mod_ref[...]
    h = _norm_mod(x, g_ref[...], mod).astype(BF16)
    gate = jnp.dot(h, wg_ref[...], preferred_element_type=F32)
    up = jnp.dot(h, wu_ref[...], preferred_element_type=F32)
    act = (gate * _sigmoid(gate) * up).astype(BF16)
    y = jnp.dot(act, wd_ref[...], preferred_element_type=F32)
    xn = x + (0.5 * mod[2:3]) * y
    if final:
        r = lax.rsqrt(jnp.mean(xn * xn, axis=-1, keepdims=True) + EPS)
        xn = xn * r * fin_ref[...]
    o_ref[...] = xn


def _ffn_call(x, mod, norm_g, wg, wu, wd, fin, *, layer, sub, final, tm=512):
    mod_sub = 0 if sub == 0 else 2
    return pl.pallas_call(
        functools.partial(_ffn_kernel, final=final),
        grid=(SEQ // tm,),
        in_specs=[
            pl.BlockSpec((tm, D_MODEL), lambda i: (i, 0)),
            _resident((None, None, 3, D_MODEL), lambda i: (layer, mod_sub, 0, 0)),
            _resident((None, None, 1, D_MODEL), lambda i: (layer, sub, 0, 0)),
            _resident((None, None, D_MODEL, D_FF), lambda i: (layer, sub, 0, 0)),
            _resident((None, None, D_MODEL, D_FF), lambda i: (layer, sub, 0, 0)),
            _resident((None, None, D_FF, D_MODEL), lambda i: (layer, sub, 0, 0)),
            _resident((1, D_MODEL), lambda i: (0, 0)),
        ],
        out_specs=pl.BlockSpec((tm, D_MODEL), lambda i: (i, 0)),
        out_shape=jax.ShapeDtypeStruct((SEQ, D_MODEL), F32),
        compiler_params=_params(1),
        name=f"ffn_l{layer}s{sub}",
    )(x, mod, norm_g, wg, wu, wd, fin)


def _conv_mix_kernel(x_ref, mod_ref, g_ref, win_ref, caw_ref, cab_ref, lng_ref, lnb_ref,
                     cbw_ref, wout_ref, seg_ref, o_ref, abuf, bbuf, sbuf, *, tm):
    i = pl.program_id(0)

    @pl.when(i == 0)
    def _():
        abuf[0:A_HALO, :] = jnp.zeros((A_HALO, A_WIDTH), F32)
        bbuf[0:B_HALO, :] = jnp.zeros((B_HALO, B_WIDTH), F32)

    x = x_ref[...]
    mod = mod_ref[...]
    h = _norm_mod(x, g_ref[...], mod).astype(BF16)
    z = jnp.dot(h, win_ref[...], preferred_element_type=F32)

    a = z[:, 0:A_WIDTH] * _sigmoid(z[:, A_WIDTH:2 * A_WIDTH])
    abuf[A_HALO:A_HALO + tm, :] = a
    acc = jnp.zeros((tm, A_WIDTH), F32) + cab_ref[...]
    ext = tm + V7X_SUBLANES
    for r in range(V7X_SUBLANES):
        part = None
        for q in range((A_CONV - 1 - r) // V7X_SUBLANES + 1):
            k = A_CONV - 1 - (V7X_SUBLANES * q + r)
            term = caw_ref[k:k + 1, :] * abuf[pl.ds(A_HALO - V7X_SUBLANES * (q + 1), ext), :]
            part = term if part is None else part + term
        if r == 0:
            acc = acc + part[V7X_SUBLANES:, :]
        else:
            sbuf[r - 1] = part
            acc = acc + sbuf[r - 1, pl.ds(V7X_SUBLANES - r, tm), :]
    abuf[0:A_HALO, :] = abuf[tm:tm + A_HALO, :]
    seg = seg_ref[...]
    mu = _split_dot(acc, seg)
    d = acc - mu
    var = _split_dot(d * d, seg)
    yn = d * lax.rsqrt(var + EPS) * lng_ref[...] + lnb_ref[...]
    a_out = yn * _sigmoid(yn)

    off = 2 * A_WIDTH
    cv = z[:, off + B_WIDTH:off + 2 * B_WIDTH] * z[:, off + 2 * B_WIDTH:off + 3 * B_WIDTH]
    bbuf[B_HALO:B_HALO + tm, :] = cv
    bacc = jnp.zeros((tm, B_WIDTH), F32)
    for k in range(B_CONV):
        bacc = bacc + cbw_ref[k:k + 1, :] * bbuf[pl.ds(B_HALO - (B_CONV - 1) + k, tm), :]
    bbuf[0:B_HALO, :] = bbuf[tm:tm + B_HALO, :]
    bb = z[:, off:off + B_WIDTH] * bacc

    y = (jnp.dot(a_out.astype(BF16), wout_ref[0:A_WIDTH, :], preferred_element_type=F32)
         + jnp.dot(bb.astype(BF16), wout_ref[A_WIDTH:A_WIDTH + B_WIDTH, :],
                   preferred_element_type=F32))
    o_ref[...] = x + mod[2:3] * y


def _conv_mix_call(x, mod, norm_g, w_in, caw, cab, lng, lnb, cbw, w_out, seg, *, layer, tm=512):
    even_in = w_in.shape[1]
    return pl.pallas_call(
        functools.partial(_conv_mix_kernel, tm=tm),
        grid=(SEQ // tm,),
        in_specs=[
            pl.BlockSpec((tm, D_MODEL), lambda i: (i, 0)),
            _resident((None, None, 3, D_MODEL), lambda i: (layer, 1, 0, 0)),
            _resident((None, 1, D_MODEL), lambda i: (layer, 0, 0)),
            _full((D_MODEL, even_in)),
            _full((A_CONV, A_WIDTH)),
            _full((1, A_WIDTH)),
            _full((1, A_WIDTH)),
            _full((1, A_WIDTH)),
            _full((B_CONV, B_WIDTH)),
            _full((A_WIDTH + B_WIDTH, D_MODEL)),
            _full((A_WIDTH, A_WIDTH)),
        ],
        out_specs=pl.BlockSpec((tm, D_MODEL), lambda i: (i, 0)),
        out_shape=jax.ShapeDtypeStruct((SEQ, D_MODEL), F32),
        scratch_shapes=[pltpu.VMEM((A_HALO + tm, A_WIDTH), F32),
                        pltpu.VMEM((B_HALO + tm, B_WIDTH), F32),
                        pltpu.VMEM((V7X_SUBLANES - 1, tm + V7X_SUBLANES, A_WIDTH), F32)],
        compiler_params=_params(1),
        name="conv_mixers",
    )(x, mod, norm_g, w_in, caw, cab, lng, lnb, cbw, w_out, seg)


def _pm_proj_kernel(x_ref, mod_ref, g_ref, wu_ref, wq_ref, wkvt_ref, pw_ref,
                    pb_ref, ps_ref, pooled_ref, qrow_ref, kv_ref, km_ref,
                    ubuf, hmean, *, tm):
    i = pl.program_id(0)
    hp = D_HEADS * HEAD_PAD

    @pl.when(i == 0)
    def _():
        ubuf[0:POOL_HALO, :] = jnp.zeros((POOL_HALO, C_WIDTH), F32)

    x = x_ref[...]
    h = _norm_mod(x, g_ref[...], mod_ref[...]).astype(BF16)
    u = jnp.dot(h, wu_ref[...], preferred_element_type=F32)
    qr = jnp.dot(h, wq_ref[...], preferred_element_type=F32)
    kvt = lax.dot_general(wkvt_ref[...], h, _NT, preferred_element_type=F32)

    ubuf[POOL_HALO:POOL_HALO + tm, :] = u
    t1 = (i * tm + 1 + lax.broadcasted_iota(I32, (tm, C_GROUP_DIM), 0)).astype(F32)
    for gi, w in enumerate(POOL_WINDOWS):
        c0 = gi * C_GROUP_DIM
        s = u[:, c0:c0 + C_GROUP_DIM]
        for k in range(1, w):
            s = s + ubuf[pl.ds(POOL_HALO - k, tm), c0:c0 + C_GROUP_DIM]
        pooled = s / jnp.minimum(t1, float(w)) - u[:, c0:c0 + C_GROUP_DIM]
        mixed = (jnp.dot(pooled.astype(BF16), pw_ref[gi], preferred_element_type=F32)
                 + pb_ref[:, c0:c0 + C_GROUP_DIM])
        pooled_ref[:, c0:c0 + C_GROUP_DIM] = (
            mixed * ps_ref[:, c0:c0 + C_GROUP_DIM]).astype(BF16)
    ubuf[0:POOL_HALO, :] = ubuf[tm:tm + POOL_HALO, :]

    scale = HEAD_DIM ** -0.5
    lane = lax.broadcasted_iota(I32, (tm, hp), 1) % HEAD_PAD
    qrows = jnp.where(lane == ALIBI_COL, 1.0, qr * scale)
    for hh in range(D_HEADS):
        qrow_ref[hh] = qrows[:, hh * HEAD_PAD:(hh + 1) * HEAD_PAD]
    per_tile = tm // MOBA_BLOCK
    kvb = kvt.astype(BF16)
    first = lax.broadcasted_iota(I32, (HEAD_DIM, MOBA_BLOCK), 0) == 0
    koff = lax.broadcasted_iota(I32, (HEAD_DIM, MOBA_BLOCK), 1).astype(F32)
    zeros = jnp.zeros((HEAD_DIM, MOBA_BLOCK), BF16)
    for hh in range(D_HEADS):
        alibi = jnp.where(first, koff * (2.0 ** -(hh + 1)), 0.0).astype(BF16)
        for sb in range(per_tile):
            keys = slice(sb * MOBA_BLOCK, (sb + 1) * MOBA_BLOCK)
            kv_ref[hh, sb, 0:HEAD_DIM, :] = kvb[hh * HEAD_DIM:(hh + 1) * HEAD_DIM, keys]
            kv_ref[hh, sb, HEAD_DIM:HEAD_PAD, :] = alibi
            kv_ref[hh, sb, HEAD_PAD:HEAD_PAD + HEAD_DIM, :] = kvb[
                D_WIDTH + hh * HEAD_DIM:D_WIDTH + (hh + 1) * HEAD_DIM, keys]
            kv_ref[hh, sb, HEAD_PAD + HEAD_DIM:2 * HEAD_PAD, :] = zeros

    for sb in range(per_tile):
        blk = h[sb * MOBA_BLOCK:(sb + 1) * MOBA_BLOCK, :].astype(F32)
        hmean[pl.ds(i * per_tile + sb, 1), :] = jnp.mean(blk, axis=0, keepdims=True)

    @pl.when(i == pl.num_programs(0) - 1)
    def _():
        hi, lo = _split(hmean[...])
        wkt = wkvt_ref[0:D_WIDTH, :]
        km_ref[...] = (lax.dot_general(hi, wkt, _NT, preferred_element_type=F32)
                       + lax.dot_general(lo, wkt, _NT, preferred_element_type=F32))


def _pm_proj_call(x, mod, norm_g, wu, wq, wkvt, pw, pb, ps, *, layer, tm=512):
    hp = D_HEADS * HEAD_PAD
    return pl.pallas_call(
        functools.partial(_pm_proj_kernel, tm=tm),
        grid=(SEQ // tm,),
        in_specs=[
            pl.BlockSpec((tm, D_MODEL), lambda i: (i, 0)),
            _resident((None, None, 3, D_MODEL), lambda i: (layer, 1, 0, 0)),
            _resident((None, 1, D_MODEL), lambda i: (layer, 0, 0)),
            _full((D_MODEL, C_WIDTH)),
            _full((D_MODEL, hp)),
            _full((2 * D_WIDTH, D_MODEL)),
            _full((len(POOL_WINDOWS), C_GROUP_DIM, C_GROUP_DIM)),
            _full((1, C_WIDTH)),
            _full((1, C_WIDTH)),
        ],
        out_specs=[
            pl.BlockSpec((tm, C_WIDTH), lambda i: (i, 0)),
            pl.BlockSpec((D_HEADS, tm, HEAD_PAD), lambda i: (0, i, 0)),
            pl.BlockSpec((D_HEADS, tm // MOBA_BLOCK, 2 * HEAD_PAD, MOBA_BLOCK),
                         lambda i: (0, i, 0, 0)),
            pl.BlockSpec((N_BLOCKS, D_WIDTH), lambda i: (0, 0)),
        ],
        out_shape=[
            jax.ShapeDtypeStruct((SEQ, C_WIDTH), BF16),
            jax.ShapeDtypeStruct((D_HEADS, SEQ, HEAD_PAD), F32),
            jax.ShapeDtypeStruct((D_HEADS, N_BLOCKS, 2 * HEAD_PAD, MOBA_BLOCK), BF16),
            jax.ShapeDtypeStruct((N_BLOCKS, D_WIDTH), F32),
        ],
        scratch_shapes=[pltpu.VMEM((POOL_HALO + tm, C_WIDTH), F32),
                        pltpu.VMEM((N_BLOCKS, D_MODEL), F32)],
        compiler_params=_params(1),
        name="pool_qkv_proj",
    )(x, mod, norm_g, wu, wq, wkvt, pw, pb, ps)


def _block_rows(qs, slabs, mask, lse_add):
    n = qs[0].shape[0]
    s = jnp.concatenate(
        [jnp.dot(q, slab[0:HEAD_PAD, :], preferred_element_type=F32)
         for q, slab in zip(qs, slabs)], axis=0)
    if mask is not None:
        s = jnp.where(mask, s, NEG_BIG)
    m = jnp.max(s, axis=1, keepdims=True)
    p = jnp.exp(s - m).astype(BF16)
    ones = jnp.ones((HEAD_PAD, MOBA_BLOCK), BF16)
    ot = jnp.concatenate(
        [lax.dot_general(p[u * n:(u + 1) * n],
                         jnp.concatenate([slab[HEAD_PAD:2 * HEAD_PAD, :], ones], axis=0), _NT,
                         preferred_element_type=F32)
         for u, slab in enumerate(slabs)], axis=0)
    o, tot = ot[:, 0:HEAD_PAD], ot[:, HEAD_PAD:2 * HEAD_PAD]
    lse = m + jnp.log(tot)
    if lse_add is not None:
        lse = jnp.concatenate(
            [lse[u * n:(u + 1) * n] + add for u, add in enumerate(lse_add)], axis=0)
    low = lax.broadcasted_iota(I32, o.shape, 1) < HEAD_DIM
    return jnp.where(low, o / tot, lse)


def _onehot_pairs(sel_rows):
    blk = lax.broadcasted_iota(I32, (N_BLOCKS, MOBA_BLOCK), 0)
    return jnp.concatenate(
        [jnp.where(blk == sel_rows[s], 1.0, 0.0) for s in range(MOBA_TOPK)], axis=1)


def _gate_own_kernel(qrow_ref, kv_ref, km_ref, tri_ref,
                     sel_ref, rank_ref, cnt_ref, own_ref, *, heads):
    own = pl.program_id(0)
    n_pair = MOBA_TOPK * MOBA_BLOCK
    qs = [qrow_ref[a].astype(BF16) for a in range(heads)]

    gates = []
    for a in range(heads):
        km_hi, km_lo = _split(km_ref[a])
        gates.append(lax.dot_general(km_hi, qs[a], _NT, preferred_element_type=F32)
                     + lax.dot_general(km_lo, qs[a], _NT, preferred_element_type=F32))
    blk = lax.broadcasted_iota(I32, (N_BLOCKS, heads * MOBA_BLOCK), 0)
    g = jnp.where(blk < own, jnp.concatenate(gates, axis=1), -jnp.inf)
    slots = []
    for s in range(MOBA_TOPK):
        best = jnp.max(g, axis=0, keepdims=True)
        idx = jnp.min(jnp.where(g == best, blk, N_BLOCKS), axis=0, keepdims=True)
        g = jnp.where(blk == idx, -jnp.inf, g)
        slots.append(jnp.where(s < own, idx, N_BLOCKS))

    onehots = []
    for a in range(heads):
        sel_rows = [slot[:, a * MOBA_BLOCK:(a + 1) * MOBA_BLOCK] for slot in slots]
        sel_ref[a] = jnp.concatenate(sel_rows, axis=0)
        onehots.append(_onehot_pairs(sel_rows))
    onehot = jnp.concatenate(onehots, axis=0)
    oh = onehot.astype(BF16)
    before = jnp.dot(oh, tri_ref[...], preferred_element_type=F32)
    hit = onehot * before
    ones = jnp.ones((8, n_pair), BF16)
    cnt = lax.dot_general(ones, oh, _NT, preferred_element_type=F32)
    for a in range(heads):
        rank = jnp.sum(hit[a * N_BLOCKS:(a + 1) * N_BLOCKS], axis=0, keepdims=True).astype(I32)
        rank_ref[a] = jnp.concatenate(
            [rank[:, s * MOBA_BLOCK:(s + 1) * MOBA_BLOCK] for s in range(MOBA_TOPK)], axis=0)
        cnt_ref[a] = cnt[0:1, a * N_BLOCKS:(a + 1) * N_BLOCKS].astype(I32)

    qi = lax.broadcasted_iota(I32, (heads * MOBA_BLOCK, MOBA_BLOCK), 0) % MOBA_BLOCK
    ki = lax.broadcasted_iota(I32, (heads * MOBA_BLOCK, MOBA_BLOCK), 1)
    rows = _block_rows(qs, [kv_ref.at[a] for a in range(heads)], ki <= qi, None)
    for a in range(heads):
        own_ref[a] = rows[a * MOBA_BLOCK:(a + 1) * MOBA_BLOCK]


def _gate_own_call(qrows, kv, kmean_h, tri, *, split):
    heads = SPLIT_HEADS
    return pl.pallas_call(
        functools.partial(_gate_own_kernel, heads=heads),
        grid=(N_BLOCKS,),
        in_specs=[
            pl.BlockSpec((heads, MOBA_BLOCK, HEAD_PAD), lambda b: (split, b, 0)),
            pl.BlockSpec((heads, None, 2 * HEAD_PAD, MOBA_BLOCK), lambda b: (split, b, 0, 0)),
            pl.BlockSpec((heads, N_BLOCKS, HEAD_PAD), lambda b: (split, 0, 0)),
            _full((MOBA_TOPK * MOBA_BLOCK, MOBA_TOPK * MOBA_BLOCK)),
        ],
        out_specs=[
            pl.BlockSpec((heads, MOBA_TOPK, MOBA_BLOCK), lambda b: (0, 0, b)),
            pl.BlockSpec((heads, MOBA_TOPK, MOBA_BLOCK), lambda b: (0, 0, b)),
            pl.BlockSpec((heads, None, 1, N_BLOCKS), lambda b: (0, b, 0, 0)),
            pl.BlockSpec((heads, MOBA_BLOCK, HEAD_PAD), lambda b: (0, b, 0)),
        ],
        out_shape=[
            jax.ShapeDtypeStruct((heads, MOBA_TOPK, SEQ), I32),
            jax.ShapeDtypeStruct((heads, MOBA_TOPK, SEQ), I32),
            jax.ShapeDtypeStruct((heads, N_BLOCKS, 1, N_BLOCKS), I32),
            jax.ShapeDtypeStruct((heads, SEQ, HEAD_PAD), F32),
        ],
        compiler_params=_params(1),
        name=f"moba_gate_own_{split}",
    )(qrows, kv, kmean_h, tri)


def _route_pos_kernel(sel_ref, rank_ref, base_ref, pos_ref, *, blocks):
    lane = lax.broadcasted_iota(I32, (1, MOBA_BLOCK), 1)
    for bb in range(blocks):
        qs = slice(bb * MOBA_BLOCK, (bb + 1) * MOBA_BLOCK)
        rows = [[] for _ in range(MOBA_TOPK)]
        for a in range(SPLIT_HEADS):
            sel = sel_ref[a, :, qs]
            onehot = _onehot_pairs([sel[s:s + 1, :] for s in range(MOBA_TOPK)]).astype(BF16)
            dig = jnp.dot(base_ref[a, bb].astype(BF16), onehot,
                          preferred_element_type=F32)
            base = (dig[0:1] * 16384.0 + dig[1:2] * 128.0 + dig[2:3]).astype(I32)
            for s in range(MOBA_TOPK):
                p = base[:, s * MOBA_BLOCK:(s + 1) * MOBA_BLOCK] + rank_ref[a, s:s + 1, qs]
                rows[s].append(
                    jnp.where(sel[s:s + 1, :] < N_BLOCKS, p, TRASH_ROW0 + lane % SC_WINDOW))
        for s in range(MOBA_TOPK):
            pos_ref[s, :, qs] = jnp.concatenate(rows[s], axis=0)


def _route_pos_call(sel, rank, base_digits, *, blocks=8):
    width = blocks * MOBA_BLOCK
    return pl.pallas_call(
        functools.partial(_route_pos_kernel, blocks=blocks),
        grid=(N_BLOCKS // blocks,),
        in_specs=[
            pl.BlockSpec((SPLIT_HEADS, MOBA_TOPK, width), lambda b: (0, 0, b)),
            pl.BlockSpec((SPLIT_HEADS, MOBA_TOPK, width), lambda b: (0, 0, b)),
            pl.BlockSpec((SPLIT_HEADS, blocks, 8, N_BLOCKS), lambda b: (0, b, 0, 0)),
        ],
        out_specs=pl.BlockSpec((MOBA_TOPK, SPLIT_HEADS, width), lambda b: (0, 0, b)),
        out_shape=jax.ShapeDtypeStruct((MOBA_TOPK, SPLIT_HEADS, SEQ), I32),
        compiler_params=_params(1),
        name="moba_route_pos",
    )(sel, rank, base_digits)


def _route_tables(cnt, split):
    cnt = cnt.reshape(SPLIT_HEADS, N_BLOCKS, N_BLOCKS)
    tiles = (cnt.sum(axis=1) + ROUTE_TILE - 1) // ROUTE_TILE
    tiles_flat = tiles.reshape(N_GROUPS)
    tile0 = jnp.cumsum(tiles_flat) - tiles_flat
    n_tiles = tiles_flat.sum().astype(I32)
    within = jnp.cumsum(cnt, axis=1) - cnt
    base = tile0.reshape(SPLIT_HEADS, 1, N_BLOCKS) * ROUTE_TILE + within
    digits = jnp.stack([base // 16384, (base // 128) % 128, base % 128], axis=2).astype(F32)
    digits = jnp.pad(digits, ((0, 0), (0, 0), (0, 8 - 3), (0, 0)))
    t = jnp.arange(MAX_TILES, dtype=I32)
    ended = (tile0 + tiles_flat)[None, :] <= t[:, None]
    group_of_tile = jnp.minimum(ended.sum(axis=1).astype(I32), N_GROUPS - 1)
    tile_h = group_of_tile // N_BLOCKS
    head = split * SPLIT_HEADS + tile_h
    tile_step = lax.bitcast_convert_type((127 + 7 - head) << 23, F32)
    return digits, tile_h, group_of_tile % N_BLOCKS, n_tiles.reshape(1), tile_step


def _sc_mesh():
    return plsc.VectorSubcoreMesh(core_axis_name="core", subcore_axis_name="subcore")


def _dispatch_call(rows, pos_by_slot, *, split):
    n_rows = SPLIT_HEADS * SEQ
    first_window = split * n_rows // SC_WINDOW

    @functools.partial(
        pl.kernel, mesh=_sc_mesh(), scratch_types=[],
        out_type=jax.ShapeDtypeStruct((ROUTE_ROWS, HEAD_PAD), F32))
    def dispatch(x_hbm, i0_hbm, i1_hbm, i2_hbm, o_hbm):
        def body(x_vmem, i0_vmem, i1_vmem, i2_vmem):
            pltpu.sync_copy(x_vmem, o_hbm.at[i0_vmem.at[0]])
            pltpu.sync_copy(x_vmem, o_hbm.at[i1_vmem.at[0]])
            pltpu.sync_copy(x_vmem, o_hbm.at[i2_vmem.at[0]])

        idx_spec = pl.BlockSpec((1, SC_WINDOW), lambda i: (0, i))
        pltpu.emit_pipeline(
            body,
            grid=(n_rows // SC_WINDOW,),
            in_specs=[pl.BlockSpec((SC_WINDOW, HEAD_PAD), lambda i: (first_window + i, 0)),
                      idx_spec, idx_spec, idx_spec],
            out_specs=[],
            core_axis_name=("core", "subcore"),
            dimension_semantics=(pltpu.PARALLEL,),
        )(x_hbm, i0_hbm, i1_hbm, i2_hbm)

    return dispatch(rows, *pos_by_slot)


def _collect_call(table, idx):
    n_rows = idx.shape[1]

    @functools.partial(
        pl.kernel, mesh=_sc_mesh(), scratch_types=[],
        out_type=jax.ShapeDtypeStruct((n_rows, HEAD_PAD), F32))
    def collect(x_hbm, i_hbm, o_hbm):
        def body(i_vmem, o_vmem):
            pltpu.sync_copy(x_hbm.at[i_vmem.at[0]], o_vmem)

        pltpu.emit_pipeline(
            body,
            grid=(n_rows // SC_WINDOW,),
            in_specs=[pl.BlockSpec((1, SC_WINDOW), lambda i: (0, i))],
            out_specs=[pl.BlockSpec((SC_WINDOW, HEAD_PAD), lambda i: (i, 0))],
            core_axis_name=("core", "subcore"),
            dimension_semantics=(pltpu.PARALLEL,),
        )(i_hbm, o_hbm)

    return collect(table, idx)


def _routed_attn_kernel(th_ref, tj_ref, nt_ref, step_ref, q_ref, *refs, tiles):
    kv_refs, o_ref = refs[0:tiles], refs[tiles]
    t0 = pl.program_id(0) * tiles

    @pl.when(t0 < nt_ref[0])
    def _():
        qs = [q_ref[u * ROUTE_TILE:(u + 1) * ROUTE_TILE, :].astype(BF16) for u in range(tiles)]
        adds = [step_ref[t0 + u] * tj_ref[t0 + u].astype(F32) for u in range(tiles)]
        o_ref[...] = _block_rows(qs, kv_refs, None, adds)


def _routed_attn_call(tile_h, tile_j, n_tiles, tile_step, routed_q, kv,
                      *, split, tiles=ROUTE_TILES_PER_STEP):
    live = lambda s, th, tj, nt: (jnp.minimum(s, (nt[0] + tiles - 1) // tiles), 0)
    head0 = split * SPLIT_HEADS

    def kv_spec(u):
        return pl.BlockSpec(
            (None, None, 2 * HEAD_PAD, MOBA_BLOCK),
            lambda s, th, tj, nt: (head0 + th[s * tiles + u], tj[s * tiles + u], 0, 0))

    grid_spec = pltpu.PrefetchScalarGridSpec(
        num_scalar_prefetch=3,
        grid=(MAX_TILES // tiles,),
        in_specs=([pl.BlockSpec(memory_space=pltpu.SMEM),
                   pl.BlockSpec((tiles * ROUTE_TILE, HEAD_PAD), live)]
                  + [kv_spec(u) for u in range(tiles)]),
        out_specs=pl.BlockSpec((tiles * ROUTE_TILE, HEAD_PAD), live),
    )
    return pl.pallas_call(
        functools.partial(_routed_attn_kernel, tiles=tiles),
        grid_spec=grid_spec,
        out_shape=jax.ShapeDtypeStruct((ROUTE_ROWS, HEAD_PAD), F32),
        compiler_params=_params(1),
        name=f"moba_routed_attn_{split}",
    )(tile_h, tile_j, n_tiles, tile_step, routed_q, *([kv] * tiles))


def _pm_out_kernel(slope_ref, x_ref, mod_ref, pooled_ref, wp_ref, own_ref, got_ref, wa_ref,
                   o_ref, *, split):
    own = pl.program_id(0)
    low = lax.broadcasted_iota(I32, (MOBA_BLOCK, HEAD_PAD), 1) < HEAD_DIM

    def lse_of(part):
        return jnp.where(low, pltpu.roll(part, HEAD_DIM, axis=1), part)

    if split == 0:
        y = jnp.dot(pooled_ref[...], wp_ref[...], preferred_element_type=F32)
    else:
        y = jnp.zeros((MOBA_BLOCK, D_MODEL), F32)
    for a in range(SPLIT_HEADS):
        hh = split * SPLIT_HEADS + a
        own_shift = slope_ref[hh] * (own * MOBA_BLOCK).astype(F32)
        parts = [own_ref[a]]
        lses = [lse_of(parts[0])]
        for s in range(MOBA_TOPK):
            valid = s < own
            part = got_ref[s, a]
            parts.append(jnp.where(valid, part, 0.0))
            lses.append(jnp.where(valid, lse_of(part) - own_shift, NEG_BIG))
        top = functools.reduce(jnp.maximum, lses)
        num = jnp.zeros((MOBA_BLOCK, HEAD_PAD), F32)
        den = jnp.zeros((MOBA_BLOCK, HEAD_PAD), F32)
        for part, ls in zip(parts, lses):
            w = jnp.exp(ls - top)
            num = num + w * part
            den = den + w
        att = (num / den).astype(BF16)
        y = y + jnp.dot(att, wa_ref[a], preferred_element_type=F32)
    o_ref[...] = x_ref[...] + mod_ref[2:3, :] * y


def _pm_out_call(slopes, x, mod, pooled, own_part, got_part, wp, wa, *, layer, split):
    tm = MOBA_BLOCK
    pooled_map = (lambda i: (i, 0)) if split == 0 else (lambda i: (0, 0))
    return pl.pallas_call(
        functools.partial(_pm_out_kernel, split=split),
        grid=(SEQ // tm,),
        in_specs=[
            pl.BlockSpec(memory_space=pltpu.SMEM),
            pl.BlockSpec((tm, D_MODEL), lambda i: (i, 0)),
            _resident((None, None, 3, D_MODEL), lambda i: (layer, 1, 0, 0)),
            pl.BlockSpec((tm, C_WIDTH), pooled_map),
            _full((C_WIDTH, D_MODEL)),
            pl.BlockSpec((SPLIT_HEADS, tm, HEAD_PAD), lambda i: (0, i, 0)),
            pl.BlockSpec((MOBA_TOPK, SPLIT_HEADS, tm, HEAD_PAD), lambda i: (0, 0, i, 0)),
            _resident((SPLIT_HEADS, HEAD_PAD, D_MODEL), lambda i: (split, 0, 0)),
        ],
        out_specs=pl.BlockSpec((tm, D_MODEL), lambda i: (i, 0)),
        out_shape=jax.ShapeDtypeStruct((SEQ, D_MODEL), F32),
        compiler_params=_params(1),
        name=f"pool_moba_out_{split}",
    )(slopes, x, mod, pooled, wp, own_part, got_part, wa)


def _head_pad(w, axis):
    shape = list(w.shape)
    shape[axis:axis + 1] = [D_HEADS, HEAD_DIM]
    pad = [(0, 0)] * len(shape)
    pad[axis + 1] = (0, HEAD_PAD - HEAD_DIM)
    out = jnp.pad(w.reshape(shape), pad)
    shape[axis:axis + 2] = [D_HEADS * HEAD_PAD]
    return out.reshape(shape)


def kernel(x, c, ada_w, ada_b, ffn_norm, ffn_w_gate, ffn_w_up, ffn_w_down, mix_norm, conv_w_in,
           conv_a_w, conv_a_b, conv_a_ln_g, conv_a_ln_b, conv_b_w, conv_w_out, pm_w_in, pool_w,
           pool_b, pool_scale, pm_w_out, final_norm):
    assert x.shape == (1, SEQ, D_MODEL) and c.shape == (1, D_MODEL)
    xs = x.reshape(SEQ, D_MODEL)
    mod = _mod_call(c, ada_w, ada_b)

    wg = ffn_w_gate.astype(BF16)
    wu = ffn_w_up.astype(BF16)
    wd = ffn_w_down.astype(BF16)
    ffn_g = ffn_norm.reshape(DEPTH, 2, 1, D_MODEL)
    mix_g = mix_norm.reshape(DEPTH, 1, D_MODEL)
    fin = final_norm.reshape(1, D_MODEL)
    ffn = functools.partial(_ffn_call, mod=mod, norm_g=ffn_g, wg=wg, wu=wu, wd=wd, fin=fin)

    xs = ffn(xs, layer=0, sub=0, final=False)
    grp = np.arange(A_WIDTH) // (A_WIDTH // A_GROUPS)
    seg = jnp.asarray((grp[:, None] == grp[None, :]) / (A_WIDTH // A_GROUPS), dtype=BF16)
    xs = _conv_mix_call(
        xs, mod, mix_g, conv_w_in[0].astype(BF16), conv_a_w[0], conv_a_b[0].reshape(1, A_WIDTH),
        conv_a_ln_g[0].reshape(1, A_WIDTH), conv_a_ln_b[0].reshape(1, A_WIDTH), conv_b_w[0],
        conv_w_out[0].astype(BF16), seg, layer=0)
    xs = ffn(xs, layer=0, sub=1, final=False)

    xs = ffn(xs, layer=1, sub=0, final=False)
    w_in = pm_w_in[0].astype(BF16)
    w_u = w_in[:, 0:C_WIDTH]
    w_q = _head_pad(w_in[:, C_WIDTH:C_WIDTH + D_WIDTH], 1)
    w_kvt = w_in[:, C_WIDTH + D_WIDTH:C_WIDTH + 3 * D_WIDTH].T
    pooled, qrows, kv, kmean = _pm_proj_call(
        xs, mod, mix_g, w_u, w_q, w_kvt, pool_w[0].astype(BF16),
        pool_b[0].reshape(1, C_WIDTH), pool_scale[0].reshape(1, C_WIDTH), layer=1)
    kmean_h = _head_pad(kmean, 1).reshape(N_BLOCKS, D_HEADS, HEAD_PAD).transpose(1, 0, 2)

    n_pair = MOBA_TOPK * MOBA_BLOCK
    tri = jnp.asarray(np.arange(n_pair)[:, None] < np.arange(n_pair)[None, :], dtype=BF16)
    q_flat = qrows.reshape(D_HEADS * SEQ, HEAD_PAD)
    own_parts, got_parts = [], []
    for split in range(N_SPLITS):
        sel, rank, cnt, own_part = _gate_own_call(qrows, kv, kmean_h, tri, split=split)
        base_digits, tile_h, tile_j, n_tiles, tile_step = _route_tables(cnt, split)
        pos = _route_pos_call(sel, rank, base_digits)
        pos_by_slot = [pos[s].reshape(1, SPLIT_HEADS * SEQ) for s in range(MOBA_TOPK)]
        routed_q = _dispatch_call(q_flat, pos_by_slot, split=split)
        routed_parts = _routed_attn_call(tile_h, tile_j, n_tiles, tile_step, routed_q, kv,
                                         split=split)
        got = _collect_call(routed_parts, pos.reshape(1, N_PAIRS))
        own_parts.append(own_part)
        got_parts.append(got.reshape(MOBA_TOPK, SPLIT_HEADS, SEQ, HEAD_PAD))

    slopes = jnp.asarray(2.0 ** (-8.0 * np.arange(1, D_HEADS + 1) / D_HEADS), dtype=F32)
    w_out = pm_w_out[0].astype(BF16)
    w_att = _head_pad(w_out[C_WIDTH:], 0).reshape(D_HEADS, HEAD_PAD, D_MODEL)
    for split in range(N_SPLITS):
        xs = _pm_out_call(slopes, xs, mod, pooled, own_parts[split], got_parts[split],
                          w_out[0:C_WIDTH], w_att, layer=1, split=split)
    xs = ffn(xs, layer=1, sub=1, final=True)
    return xs.reshape(1, SEQ, D_MODEL)
```

```python
import functools

import numpy as np
import jax
import jax.numpy as jnp
from jax import lax
from jax.experimental import pallas as pl
from jax.experimental.pallas import tpu as pltpu
from jax.experimental.pallas import tpu_sc as plsc

D_MODEL = 1024
SEQ = 16384
DEPTH = 2
N_SUBLAYERS = 3
D_FF = 2816
EPS = 1e-6

A_WIDTH = 512
A_GROUPS = 8
A_CONV = 31
B_WIDTH = 512
B_CONV = 3

POOL_WINDOWS = (2, 4, 8, 16)
C_WIDTH = 512
C_GROUP_DIM = 128
D_HEADS = 8
HEAD_DIM = 64
D_WIDTH = 512
MOBA_BLOCK = 256
MOBA_TOPK = 3
N_BLOCKS = SEQ // MOBA_BLOCK

V7X_SUBLANES = 8
V7X_VMEM_LIMIT_BYTES = 56 * 1024 * 1024

A_HALO = 32
B_HALO = 8
POOL_HALO = 16
NEG_BIG = -1e30

HEAD_PAD = 128
ALIBI_COL = HEAD_DIM
SLAB_K_ROWS = HEAD_DIM + 16
SLAB_ROWS = SLAB_K_ROWS + HEAD_DIM

SPLIT_HEADS = 4
N_SPLITS = D_HEADS // SPLIT_HEADS
ROUTE_TILE = 256
ROUTE_TILES_PER_STEP = 8
N_PAIRS = SPLIT_HEADS * SEQ * MOBA_TOPK
N_GROUPS = SPLIT_HEADS * N_BLOCKS
MAX_TILES = N_PAIRS // ROUTE_TILE + N_GROUPS
TRASH_ROW0 = MAX_TILES * ROUTE_TILE
ROUTE_ROWS = (MAX_TILES + ROUTE_TILES_PER_STEP) * ROUTE_TILE
SC_WINDOW = 128

BF16 = jnp.bfloat16
F32 = jnp.float32
I32 = jnp.int32


def _params(n_axes):
    return pltpu.CompilerParams(
        dimension_semantics=("arbitrary",) * n_axes,
        vmem_limit_bytes=V7X_VMEM_LIMIT_BYTES)


def _resident(block_shape, index_map):
    return pl.BlockSpec(block_shape, index_map, pipeline_mode=pl.Buffered(1))


def _full(shape):
    return _resident(shape, lambda *_: (0,) * len(shape))


def _sigmoid(v):
    return 1.0 / (1.0 + jnp.exp(-v))


def _norm_mod(x, g, mod):
    r = lax.rsqrt(jnp.mean(x * x, axis=-1, keepdims=True) + EPS)
    return (x * r) * (g * (1.0 + mod[1:2])) + mod[0:1]


def _split(v):
    hi = v.astype(BF16)
    return hi, (v - hi.astype(F32)).astype(BF16)


def _split_dot(v, w):
    hi, lo = _split(v)
    return (jnp.dot(hi, w, preferred_element_type=F32)
            + jnp.dot(lo, w, preferred_element_type=F32))


_NT = (((1,), (1,)), ((), ()))


def _mod_kernel(c_ref, w_ref, b_ref, o_ref):
    c = c_ref[...]
    cond = c * _sigmoid(c)
    o_ref[0] = jnp.sum(w_ref[0] * cond, axis=0, keepdims=True) + b_ref[0]


def _mod_call(c, ada_w, ada_b):
    n_out = N_SUBLAYERS * 3 * D_MODEL
    tn = D_MODEL
    out = pl.pallas_call(
        _mod_kernel,
        grid=(DEPTH, n_out // tn),
        in_specs=[
            pl.BlockSpec((D_MODEL, 1), lambda l, j: (0, 0)),
            pl.BlockSpec((1, D_MODEL, tn), lambda l, j: (l, 0, j)),
            pl.BlockSpec((1, 1, tn), lambda l, j: (l, 0, j)),
        ],
        out_specs=pl.BlockSpec((1, 1, tn), lambda l, j: (l, 0, j)),
        out_shape=jax.ShapeDtypeStruct((DEPTH, 1, n_out), F32),
        compiler_params=_params(2),
        name="adaln_mod",
    )(c.reshape(D_MODEL, 1), ada_w, ada_b.reshape(DEPTH, 1, n_out))
    return out.reshape(DEPTH, N_SUBLAYERS, 3, D_MODEL)


def _ffn_kernel(x_ref, mod_ref, g_ref, wg_ref, wu_ref, wd_ref, fin_ref, o_ref, *, final):
    x = x_ref[...]
    mod = mod_ref[...]
    h = _norm_mod(x, g_ref[...], mod).astype(BF16)
    gate = jnp.dot(h, wg_ref[...], preferred_element_type=F32)
    up = jnp.dot(h, wu_ref[...], preferred_element_type=F32)
    act = (gate * _sigmoid(gate) * up).astype(BF16)
    y = jnp.dot(act, wd_ref[...], preferred_element_type=F32)
    xn = x + (0.5 * mod[2:3]) * y
    if final:
        r = lax.rsqrt(jnp.mean(xn * xn, axis=-1, keepdims=True) + EPS)
        xn = xn * r * fin_ref[...]
    o_ref[...] = xn


def _ffn_call(x, mod, norm_g, wg, wu, wd, fin, *, layer, sub, final, tm=512):
    mod_sub = 0 if sub == 0 else 2
    return pl.pallas_call(
        functools.partial(_ffn_kernel, final=final),
        grid=(SEQ // tm,),
        in_specs=[
            pl.BlockSpec((tm, D_MODEL), lambda i: (i, 0)),
            _resident((None, None, 3, D_MODEL), lambda i: (layer, mod_sub, 0, 0)),
            _resident((None, None, 1, D_MODEL), lambda i: (layer, sub, 0, 0)),
            _resident((None, None, D_MODEL, D_FF), lambda i: (layer, sub, 0, 0)),
            _resident((None, None, D_MODEL, D_FF), lambda i: (layer, sub, 0, 0)),
            _resident((None, None, D_FF, D_MODEL), lambda i: (layer, sub, 0, 0)),
            _resident((1, D_MODEL), lambda i: (0, 0)),
        ],
        out_specs=pl.BlockSpec((tm, D_MODEL), lambda i: (i, 0)),
        out_shape=jax.ShapeDtypeStruct((SEQ, D_MODEL), F32),
        compiler_params=_params(1),
        name=f"ffn_l{layer}s{sub}",
    )(x, mod, norm_g, wg, wu, wd, fin)


def _conv_mix_kernel(x_ref, mod_ref, g_ref, win_ref, caw_ref, cab_ref, lng_ref, lnb_ref,
                     cbw_ref, wout_ref, seg_ref, o_ref, abuf, bbuf, sbuf, *, tm):
    i = pl.program_id(0)

    @pl.when(i == 0)
    def _():
        abuf[0:A_HALO, :] = jnp.zeros((A_HALO, A_WIDTH), F32)
        bbuf[0:B_HALO, :] = jnp.zeros((B_HALO, B_WIDTH), F32)

    x = x_ref[...]
    mod = mod_ref[...]
    h = _norm_mod(x, g_ref[...], mod).astype(BF16)
    z = jnp.dot(h, win_ref[...], preferred_element_type=F32)

    a = z[:, 0:A_WIDTH] * _sigmoid(z[:, A_WIDTH:2 * A_WIDTH])
    abuf[A_HALO:A_HALO + tm, :] = a
    acc = jnp.zeros((tm, A_WIDTH), F32) + cab_ref[...]
    ext = tm + V7X_SUBLANES
    for r in range(V7X_SUBLANES):
        part = None
        for q in range((A_CONV - 1 - r) // V7X_SUBLANES + 1):
            k = A_CONV - 1 - (V7X_SUBLANES * q + r)
            term = caw_ref[k:k + 1, :] * abuf[pl.ds(A_HALO - V7X_SUBLANES * (q + 1), ext), :]
            part = term if part is None else part + term
        if r == 0:
            acc = acc + part[V7X_SUBLANES:, :]
        else:
            sbuf[r - 1] = part
            acc = acc + sbuf[r - 1, pl.ds(V7X_SUBLANES - r, tm), :]
    abuf[0:A_HALO, :] = abuf[tm:tm + A_HALO, :]
    seg = seg_ref[...]
    mu = _split_dot(acc, seg)
    d = acc - mu
    var = _split_dot(d * d, seg)
    yn = d * lax.rsqrt(var + EPS) * lng_ref[...] + lnb_ref[...]
    a_out = yn * _sigmoid(yn)

    off = 2 * A_WIDTH
    cv = z[:, off + B_WIDTH:off + 2 * B_WIDTH] * z[:, off + 2 * B_WIDTH:off + 3 * B_WIDTH]
    bbuf[B_HALO:B_HALO + tm, :] = cv
    bacc = jnp.zeros((tm, B_WIDTH), F32)
    for k in range(B_CONV):
        bacc = bacc + cbw_ref[k:k + 1, :] * bbuf[pl.ds(B_HALO - (B_CONV - 1) + k, tm), :]
    bbuf[0:B_HALO, :] = bbuf[tm:tm + B_HALO, :]
    bb = z[:, off:off + B_WIDTH] * bacc

    y = (jnp.dot(a_out.astype(BF16), wout_ref[0:A_WIDTH, :], preferred_element_type=F32)
         + jnp.dot(bb.astype(BF16), wout_ref[A_WIDTH:A_WIDTH + B_WIDTH, :],
                   preferred_element_type=F32))
    o_ref[...] = x + mod[2:3] * y


def _conv_mix_call(x, mod, norm_g, w_in, caw, cab, lng, lnb, cbw, w_out, seg, *, layer, tm=512):
    even_in = w_in.shape[1]
    return pl.pallas_call(
        functools.partial(_conv_mix_kernel, tm=tm),
        grid=(SEQ // tm,),
        in_specs=[
            pl.BlockSpec((tm, D_MODEL), lambda i: (i, 0)),
            _resident((None, None, 3, D_MODEL), lambda i: (layer, 1, 0, 0)),
            _resident((None, 1, D_MODEL), lambda i: (layer, 0, 0)),
            _full((D_MODEL, even_in)),
            _full((A_CONV, A_WIDTH)),
            _full((1, A_WIDTH)),
            _full((1, A_WIDTH)),
            _full((1, A_WIDTH)),
            _full((B_CONV, B_WIDTH)),
            _full((A_WIDTH + B_WIDTH, D_MODEL)),
            _full((A_WIDTH, A_WIDTH)),
        ],
        out_specs=pl.BlockSpec((tm, D_MODEL), lambda i: (i, 0)),
        out_shape=jax.ShapeDtypeStruct((SEQ, D_MODEL), F32),
        scratch_shapes=[pltpu.VMEM((A_HALO + tm, A_WIDTH), F32),
                        pltpu.VMEM((B_HALO + tm, B_WIDTH), F32),
                        pltpu.VMEM((V7X_SUBLANES - 1, tm + V7X_SUBLANES, A_WIDTH), F32)],
        compiler_params=_params(1),
        name="conv_mixers",
    )(x, mod, norm_g, w_in, caw, cab, lng, lnb, cbw, w_out, seg)


def _pm_proj_kernel(x_ref, mod_ref, g_ref, wu_ref, wq_ref, wkvt_ref, pw_ref,
                    pb_ref, ps_ref, pooled_ref, qrow_ref, kv_ref, km_ref,
                    ubuf, hmean, *, tm):
    i = pl.program_id(0)

    @pl.when(i == 0)
    def _():
        ubuf[0:POOL_HALO, :] = jnp.zeros((POOL_HALO, C_WIDTH), F32)

    x = x_ref[...]
    h = _norm_mod(x, g_ref[...], mod_ref[...]).astype(BF16)
    u = jnp.dot(h, wu_ref[...], preferred_element_type=F32)
    qr = jnp.dot(h, wq_ref[...], preferred_element_type=F32)
    kvt = lax.dot_general(wkvt_ref[...], h, _NT, preferred_element_type=F32)

    ubuf[POOL_HALO:POOL_HALO + tm, :] = u
    t1 = (i * tm + 1 + lax.broadcasted_iota(I32, (tm, C_GROUP_DIM), 0)).astype(F32)
    for gi, w in enumerate(POOL_WINDOWS):
        c0 = gi * C_GROUP_DIM
        s = u[:, c0:c0 + C_GROUP_DIM]
        for k in range(1, w):
            s = s + ubuf[pl.ds(POOL_HALO - k, tm), c0:c0 + C_GROUP_DIM]
        pooled = s / jnp.minimum(t1, float(w)) - u[:, c0:c0 + C_GROUP_DIM]
        mixed = (jnp.dot(pooled.astype(BF16), pw_ref[gi], preferred_element_type=F32)
                 + pb_ref[:, c0:c0 + C_GROUP_DIM])
        pooled_ref[:, c0:c0 + C_GROUP_DIM] = (
            mixed * ps_ref[:, c0:c0 + C_GROUP_DIM]).astype(BF16)
    ubuf[0:POOL_HALO, :] = ubuf[tm:tm + POOL_HALO, :]

    lane = lax.broadcasted_iota(I32, (tm, HEAD_PAD), 1)
    tail = jnp.where(lane == ALIBI_COL, 1.0, 0.0)
    qs = qr * (HEAD_DIM ** -0.5)
    for pair in range(D_HEADS // 2):
        both = qs[:, pair * HEAD_PAD:(pair + 1) * HEAD_PAD]
        qrow_ref[2 * pair] = jnp.where(lane < HEAD_DIM, both, tail)
        qrow_ref[2 * pair + 1] = jnp.where(
            lane < HEAD_DIM, pltpu.roll(both, HEAD_DIM, axis=1), tail)
    per_tile = tm // MOBA_BLOCK
    kvb = kvt.astype(BF16)
    n_alibi = SLAB_K_ROWS - HEAD_DIM
    first = lax.broadcasted_iota(I32, (n_alibi, MOBA_BLOCK), 0) == 0
    koff = lax.broadcasted_iota(I32, (n_alibi, MOBA_BLOCK), 1).astype(F32)
    for hh in range(D_HEADS):
        alibi = jnp.where(first, koff * (2.0 ** -(hh + 1)), 0.0).astype(BF16)
        for sb in range(per_tile):
            keys = slice(sb * MOBA_BLOCK, (sb + 1) * MOBA_BLOCK)
            kv_ref[hh, sb, 0:HEAD_DIM, :] = kvb[hh * HEAD_DIM:(hh + 1) * HEAD_DIM, keys]
            kv_ref[hh, sb, HEAD_DIM:SLAB_K_ROWS, :] = alibi
            kv_ref[hh, sb, SLAB_K_ROWS:SLAB_ROWS, :] = kvb[
                D_WIDTH + hh * HEAD_DIM:D_WIDTH + (hh + 1) * HEAD_DIM, keys]

    for sb in range(per_tile):
        blk = h[sb * MOBA_BLOCK:(sb + 1) * MOBA_BLOCK, :].astype(F32)
        hmean[pl.ds(i * per_tile + sb, 1), :] = jnp.mean(blk, axis=0, keepdims=True)

    @pl.when(i == pl.num_programs(0) - 1)
    def _():
        hi, lo = _split(hmean[...])
        wkt = wkvt_ref[0:D_WIDTH, :]
        km_ref[...] = (lax.dot_general(hi, wkt, _NT, preferred_element_type=F32)
                       + lax.dot_general(lo, wkt, _NT, preferred_element_type=F32))


def _pm_proj_call(x, mod, norm_g, wu, wq, wkvt, pw, pb, ps, *, layer, tm=512):
    return pl.pallas_call(
        functools.partial(_pm_proj_kernel, tm=tm),
        grid=(SEQ // tm,),
        in_specs=[
            pl.BlockSpec((tm, D_MODEL), lambda i: (i, 0)),
            _resident((None, None, 3, D_MODEL), lambda i: (layer, 1, 0, 0)),
            _resident((None, 1, D_MODEL), lambda i: (layer, 0, 0)),
            _full((D_MODEL, C_WIDTH)),
            _full((D_MODEL, D_WIDTH)),
            _full((2 * D_WIDTH, D_MODEL)),
            _full((len(POOL_WINDOWS), C_GROUP_DIM, C_GROUP_DIM)),
            _full((1, C_WIDTH)),
            _full((1, C_WIDTH)),
        ],
        out_specs=[
            pl.BlockSpec((tm, C_WIDTH), lambda i: (i, 0)),
            pl.BlockSpec((D_HEADS, tm, HEAD_PAD), lambda i: (0, i, 0)),
            pl.BlockSpec((D_HEADS, tm // MOBA_BLOCK, SLAB_ROWS, MOBA_BLOCK),
                         lambda i: (0, i, 0, 0)),
            pl.BlockSpec((N_BLOCKS, D_WIDTH), lambda i: (0, 0)),
        ],
        out_shape=[
            jax.ShapeDtypeStruct((SEQ, C_WIDTH), BF16),
            jax.ShapeDtypeStruct((D_HEADS, SEQ, HEAD_PAD), F32),
            jax.ShapeDtypeStruct((D_HEADS, N_BLOCKS, SLAB_ROWS, MOBA_BLOCK), BF16),
            jax.ShapeDtypeStruct((N_BLOCKS, D_WIDTH), F32),
        ],
        scratch_shapes=[pltpu.VMEM((POOL_HALO + tm, C_WIDTH), F32),
                        pltpu.VMEM((N_BLOCKS, D_MODEL), F32)],
        compiler_params=_params(1),
        name="pool_qkv_proj",
    )(x, mod, norm_g, wu, wq, wkvt, pw, pb, ps)


def _block_rows(qs, slabs, mask, lse_add):
    n = qs[0].shape[0]
    kpad = jnp.zeros((HEAD_PAD - SLAB_K_ROWS, MOBA_BLOCK), BF16)
    s = jnp.concatenate(
        [jnp.dot(q, jnp.concatenate([slab[0:SLAB_K_ROWS, :], kpad], axis=0),
                 preferred_element_type=F32)
         for q, slab in zip(qs, slabs)], axis=0)
    if mask is not None:
        s = jnp.where(mask, s, NEG_BIG)
    m = jnp.max(s, axis=1, keepdims=True)
    p = jnp.exp(s - m).astype(BF16)
    vpad = jnp.zeros((HEAD_PAD - HEAD_DIM, MOBA_BLOCK), BF16)
    ones = jnp.ones((HEAD_PAD, MOBA_BLOCK), BF16)
    ot = jnp.concatenate(
        [lax.dot_general(p[u * n:(u + 1) * n],
                         jnp.concatenate([slab[SLAB_K_ROWS:SLAB_ROWS, :], vpad, ones], axis=0),
                         _NT, preferred_element_type=F32)
         for u, slab in enumerate(slabs)], axis=0)
    o, tot = ot[:, 0:HEAD_PAD], ot[:, HEAD_PAD:2 * HEAD_PAD]
    lse = m + jnp.log(tot)
    if lse_add is not None:
        lse = jnp.concatenate(
            [lse[u * n:(u + 1) * n] + add for u, add in enumerate(lse_add)], axis=0)
    low = lax.broadcasted_iota(I32, o.shape, 1) < HEAD_DIM
    return jnp.where(low, o / tot, lse)


def _onehot_pairs(sel_rows):
    blk = lax.broadcasted_iota(I32, (N_BLOCKS, MOBA_BLOCK), 0)
    return jnp.concatenate(
        [jnp.where(blk == sel_rows[s], 1.0, 0.0) for s in range(MOBA_TOPK)], axis=1)


def _gate_own_kernel(qrow_ref, kv_ref, km_ref, tri_ref,
                     sel_ref, rank_ref, cnt_ref, own_ref, *, heads):
    own = pl.program_id(0)
    n_pair = MOBA_TOPK * MOBA_BLOCK
    qs = [qrow_ref[a].astype(BF16) for a in range(heads)]

    gates = []
    for a in range(heads):
        km_hi, km_lo = _split(km_ref[a])
        gates.append(lax.dot_general(km_hi, qs[a], _NT, preferred_element_type=F32)
                     + lax.dot_general(km_lo, qs[a], _NT, preferred_element_type=F32))
    blk = lax.broadcasted_iota(I32, (N_BLOCKS, heads * MOBA_BLOCK), 0)
    g = jnp.where(blk < own, jnp.concatenate(gates, axis=1), -jnp.inf)
    slots = []
    for s in range(MOBA_TOPK):
        best = jnp.max(g, axis=0, keepdims=True)
        idx = jnp.min(jnp.where(g == best, blk, N_BLOCKS), axis=0, keepdims=True)
        g = jnp.where(blk == idx, -jnp.inf, g)
        slots.append(jnp.where(s < own, idx, N_BLOCKS))

    onehots = []
    for a in range(heads):
        sel_rows = [slot[:, a * MOBA_BLOCK:(a + 1) * MOBA_BLOCK] for slot in slots]
        sel_ref[a] = jnp.concatenate(sel_rows, axis=0)
        onehots.append(_onehot_pairs(sel_rows))
    onehot = jnp.concatenate(onehots, axis=0)
    oh = onehot.astype(BF16)
    before = jnp.dot(oh, tri_ref[...], preferred_element_type=F32)
    hit = onehot * before
    ones = jnp.ones((8, n_pair), BF16)
    cnt = lax.dot_general(ones, oh, _NT, preferred_element_type=F32)
    for a in range(heads):
        rank = jnp.sum(hit[a * N_BLOCKS:(a + 1) * N_BLOCKS], axis=0, keepdims=True).astype(I32)
        rank_ref[a] = jnp.concatenate(
            [rank[:, s * MOBA_BLOCK:(s + 1) * MOBA_BLOCK] for s in range(MOBA_TOPK)], axis=0)
        cnt_ref[a] = cnt[0:1, a * N_BLOCKS:(a + 1) * N_BLOCKS].astype(I32)

    qi = lax.broadcasted_iota(I32, (heads * MOBA_BLOCK, MOBA_BLOCK), 0) % MOBA_BLOCK
    ki = lax.broadcasted_iota(I32, (heads * MOBA_BLOCK, MOBA_BLOCK), 1)
    rows = _block_rows(qs, [kv_ref.at[a] for a in range(heads)], ki <= qi, None)
    for a in range(heads):
        own_ref[a] = rows[a * MOBA_BLOCK:(a + 1) * MOBA_BLOCK]


def _gate_own_call(qrows, kv, kmean_h, tri, *, split):
    heads = SPLIT_HEADS
    return pl.pallas_call(
        functools.partial(_gate_own_kernel, heads=heads),
        grid=(N_BLOCKS,),
        in_specs=[
            pl.BlockSpec((heads, MOBA_BLOCK, HEAD_PAD), lambda b: (split, b, 0)),
            pl.BlockSpec((heads, None, SLAB_ROWS, MOBA_BLOCK), lambda b: (split, b, 0, 0)),
            pl.BlockSpec((heads, N_BLOCKS, HEAD_PAD), lambda b: (split, 0, 0)),
            _full((MOBA_TOPK * MOBA_BLOCK, MOBA_TOPK * MOBA_BLOCK)),
        ],
        out_specs=[
            pl.BlockSpec((heads, MOBA_TOPK, MOBA_BLOCK), lambda b: (0, 0, b)),
            pl.BlockSpec((heads, MOBA_TOPK, MOBA_BLOCK), lambda b: (0, 0, b)),
            pl.BlockSpec((heads, None, 1, N_BLOCKS), lambda b: (0, b, 0, 0)),
            pl.BlockSpec((heads, MOBA_BLOCK, HEAD_PAD), lambda b: (0, b, 0)),
        ],
        out_shape=[
            jax.ShapeDtypeStruct((heads, MOBA_TOPK, SEQ), I32),
            jax.ShapeDtypeStruct((heads, MOBA_TOPK, SEQ), I32),
            jax.ShapeDtypeStruct((heads, N_BLOCKS, 1, N_BLOCKS), I32),
            jax.ShapeDtypeStruct((heads, SEQ, HEAD_PAD), F32),
        ],
        compiler_params=_params(1),
        name=f"moba_gate_own_{split}",
    )(qrows, kv, kmean_h, tri)


def _route_pos_kernel(sel_ref, rank_ref, base_ref, pos_ref, *, blocks):
    lane = lax.broadcasted_iota(I32, (1, MOBA_BLOCK), 1)
    for bb in range(blocks):
        qs = slice(bb * MOBA_BLOCK, (bb + 1) * MOBA_BLOCK)
        rows = [[] for _ in range(MOBA_TOPK)]
        for a in range(SPLIT_HEADS):
            sel = sel_ref[a, :, qs]
            onehot = _onehot_pairs([sel[s:s + 1, :] for s in range(MOBA_TOPK)]).astype(BF16)
            dig = jnp.dot(base_ref[a, bb].astype(BF16), onehot,
                          preferred_element_type=F32)
            base = (dig[0:1] * 16384.0 + dig[1:2] * 128.0 + dig[2:3]).astype(I32)
            for s in range(MOBA_TOPK):
                p = base[:, s * MOBA_BLOCK:(s + 1) * MOBA_BLOCK] + rank_ref[a, s:s + 1, qs]
                rows[s].append(
                    jnp.where(sel[s:s + 1, :] < N_BLOCKS, p, TRASH_ROW0 + lane % SC_WINDOW))
        for s in range(MOBA_TOPK):
            pos_ref[s, :, qs] = jnp.concatenate(rows[s], axis=0)


def _route_pos_call(sel, rank, base_digits, *, blocks=8):
    width = blocks * MOBA_BLOCK
    return pl.pallas_call(
        functools.partial(_route_pos_kernel, blocks=blocks),
        grid=(N_BLOCKS // blocks,),
        in_specs=[
            pl.BlockSpec((SPLIT_HEADS, MOBA_TOPK, width), lambda b: (0, 0, b)),
            pl.BlockSpec((SPLIT_HEADS, MOBA_TOPK, width), lambda b: (0, 0, b)),
            pl.BlockSpec((SPLIT_HEADS, blocks, 8, N_BLOCKS), lambda b: (0, b, 0, 0)),
        ],
        out_specs=pl.BlockSpec((MOBA_TOPK, SPLIT_HEADS, width), lambda b: (0, 0, b)),
        out_shape=jax.ShapeDtypeStruct((MOBA_TOPK, SPLIT_HEADS, SEQ), I32),
        compiler_params=_params(1),
        name="moba_route_pos",
    )(sel, rank, base_digits)


def _route_tables(cnt, split):
    cnt = cnt.reshape(SPLIT_HEADS, N_BLOCKS, N_BLOCKS)
    tiles = (cnt.sum(axis=1) + ROUTE_TILE - 1) // ROUTE_TILE
    tiles_flat = tiles.reshape(N_GROUPS)
    tile0 = jnp.cumsum(tiles_flat) - tiles_flat
    n_tiles = tiles_flat.sum().astype(I32)
    within = jnp.cumsum(cnt, axis=1) - cnt
    base = tile0.reshape(SPLIT_HEADS, 1, N_BLOCKS) * ROUTE_TILE + within
    digits = jnp.stack([base // 16384, (base // 128) % 128, base % 128], axis=2).astype(F32)
    digits = jnp.pad(digits, ((0, 0), (0, 0), (0, 8 - 3), (0, 0)))
    t = jnp.arange(MAX_TILES, dtype=I32)
    ended = (tile0 + tiles_flat)[None, :] <= t[:, None]
    group_of_tile = jnp.minimum(ended.sum(axis=1).astype(I32), N_GROUPS - 1)
    tile_h = group_of_tile // N_BLOCKS
    head = split * SPLIT_HEADS + tile_h
    tile_step = lax.bitcast_convert_type((127 + 7 - head) << 23, F32)
    return digits, tile_h, group_of_tile % N_BLOCKS, n_tiles.reshape(1), tile_step


def _sc_mesh():
    return plsc.VectorSubcoreMesh(core_axis_name="core", subcore_axis_name="subcore")


def _dispatch_call(rows, pos_by_slot, *, split):
    n_rows = SPLIT_HEADS * SEQ
    first_window = split * n_rows // SC_WINDOW

    @functools.partial(
        pl.kernel, mesh=_sc_mesh(), scratch_types=[],
        out_type=jax.ShapeDtypeStruct((ROUTE_ROWS, HEAD_PAD), F32))
    def dispatch(x_hbm, i0_hbm, i1_hbm, i2_hbm, o_hbm):
        def body(x_vmem, i0_vmem, i1_vmem, i2_vmem):
            pltpu.sync_copy(x_vmem, o_hbm.at[i0_vmem.at[0]])
            pltpu.sync_copy(x_vmem, o_hbm.at[i1_vmem.at[0]])
            pltpu.sync_copy(x_vmem, o_hbm.at[i2_vmem.at[0]])

        idx_spec = pl.BlockSpec((1, SC_WINDOW), lambda i: (0, i))
        pltpu.emit_pipeline(
            body,
            grid=(n_rows // SC_WINDOW,),
            in_specs=[pl.BlockSpec((SC_WINDOW, HEAD_PAD), lambda i: (first_window + i, 0)),
                      idx_spec, idx_spec, idx_spec],
            out_specs=[],
            core_axis_name=("core", "subcore"),
            dimension_semantics=(pltpu.PARALLEL,),
        )(x_hbm, i0_hbm, i1_hbm, i2_hbm)

    return dispatch(rows, *pos_by_slot)


def _collect_call(table, idx):
    n_rows = idx.shape[1]

    @functools.partial(
        pl.kernel, mesh=_sc_mesh(), scratch_types=[],
        out_type=jax.ShapeDtypeStruct((n_rows, HEAD_PAD), F32))
    def collect(x_hbm, i_hbm, o_hbm):
        def body(i_vmem, o_vmem):
            pltpu.sync_copy(x_hbm.at[i_vmem.at[0]], o_vmem)

        pltpu.emit_pipeline(
            body,
            grid=(n_rows // SC_WINDOW,),
            in_specs=[pl.BlockSpec((1, SC_WINDOW), lambda i: (0, i))],
            out_specs=[pl.BlockSpec((SC_WINDOW, HEAD_PAD), lambda i: (i, 0))],
            core_axis_name=("core", "subcore"),
            dimension_semantics=(pltpu.PARALLEL,),
        )(i_hbm, o_hbm)

    return collect(table, idx)


def _routed_attn_kernel(th_ref, tj_ref, nt_ref, step_ref, q_ref, *refs, tiles):
    kv_refs, o_ref = refs[0:tiles], refs[tiles]
    t0 = pl.program_id(0) * tiles

    @pl.when(t0 < nt_ref[0])
    def _():
        qs = [q_ref[u * ROUTE_TILE:(u + 1) * ROUTE_TILE, :].astype(BF16) for u in range(tiles)]
        adds = [step_ref[t0 + u] * tj_ref[t0 + u].astype(F32) for u in range(tiles)]
        o_ref[...] = _block_rows(qs, kv_refs, None, adds)


def _routed_attn_call(tile_h, tile_j, n_tiles, tile_step, routed_q, kv,
                      *, split, tiles=ROUTE_TILES_PER_STEP):
    live = lambda s, th, tj, nt: (jnp.minimum(s, (nt[0] + tiles - 1) // tiles), 0)
    head0 = split * SPLIT_HEADS

    def kv_spec(u):
        return pl.BlockSpec(
            (None, None, SLAB_ROWS, MOBA_BLOCK),
            lambda s, th, tj, nt: (head0 + th[s * tiles + u], tj[s * tiles + u], 0, 0))

    grid_spec = pltpu.PrefetchScalarGridSpec(
        num_scalar_prefetch=3,
        grid=(MAX_TILES // tiles,),
        in_specs=([pl.BlockSpec(memory_space=pltpu.SMEM),
                   pl.BlockSpec((tiles * ROUTE_TILE, HEAD_PAD), live)]
                  + [kv_spec(u) for u in range(tiles)]),
        out_specs=pl.BlockSpec((tiles * ROUTE_TILE, HEAD_PAD), live),
    )
    return pl.pallas_call(
        functools.partial(_routed_attn_kernel, tiles=tiles),
        grid_spec=grid_spec,
        out_shape=jax.ShapeDtypeStruct((ROUTE_ROWS, HEAD_PAD), F32),
        compiler_params=_params(1),
        name=f"moba_routed_attn_{split}",
    )(tile_h, tile_j, n_tiles, tile_step, routed_q, *([kv] * tiles))


def _pm_out_kernel(slope_ref, x_ref, mod_ref, pooled_ref, wp_ref, own_ref, got_ref, wa_ref,
                   o_ref, *, split):
    own = pl.program_id(0)
    low = lax.broadcasted_iota(I32, (MOBA_BLOCK, HEAD_PAD), 1) < HEAD_DIM

    def lse_of(part):
        return jnp.where(low, pltpu.roll(part, HEAD_DIM, axis=1), part)

    if split == 0:
        y = jnp.dot(pooled_ref[...], wp_ref[...], preferred_element_type=F32)
    else:
        y = jnp.zeros((MOBA_BLOCK, D_MODEL), F32)
    for a in range(SPLIT_HEADS):
        hh = split * SPLIT_HEADS + a
        own_shift = slope_ref[hh] * (own * MOBA_BLOCK).astype(F32)
        parts = [own_ref[a]]
        lses = [lse_of(parts[0])]
        for s in range(MOBA_TOPK):
            valid = s < own
            part = got_ref[s, a]
            parts.append(jnp.where(valid, part, 0.0))
            lses.append(jnp.where(valid, lse_of(part) - own_shift, NEG_BIG))
        top = functools.reduce(jnp.maximum, lses)
        num = jnp.zeros((MOBA_BLOCK, HEAD_PAD), F32)
        den = jnp.zeros((MOBA_BLOCK, HEAD_PAD), F32)
        for part, ls in zip(parts, lses):
            w = jnp.exp(ls - top)
            num = num + w * part
            den = den + w
        att = (num / den).astype(BF16)
        y = y + jnp.dot(att, wa_ref[a], preferred_element_type=F32)
    o_ref[...] = x_ref[...] + mod_ref[2:3, :] * y


def _pm_out_call(slopes, x, mod, pooled, own_part, got_part, wp, wa, *, layer, split):
    tm = MOBA_BLOCK
    pooled_map = (lambda i: (i, 0)) if split == 0 else (lambda i: (0, 0))
    return pl.pallas_call(
        functools.partial(_pm_out_kernel, split=split),
        grid=(SEQ // tm,),
        in_specs=[
            pl.BlockSpec(memory_space=pltpu.SMEM),
            pl.BlockSpec((tm, D_MODEL), lambda i: (i, 0)),
            _resident((None, None, 3, D_MODEL), lambda i: (layer, 1, 0, 0)),
            pl.BlockSpec((tm, C_WIDTH), pooled_map),
            _full((C_WIDTH, D_MODEL)),
            pl.BlockSpec((SPLIT_HEADS, tm, HEAD_PAD), lambda i: (0, i, 0)),
            pl.BlockSpec((MOBA_TOPK, SPLIT_HEADS, tm, HEAD_PAD), lambda i: (0, 0, i, 0)),
            _resident((SPLIT_HEADS, HEAD_PAD, D_MODEL), lambda i: (split, 0, 0)),
        ],
        out_specs=pl.BlockSpec((tm, D_MODEL), lambda i: (i, 0)),
        out_shape=jax.ShapeDtypeStruct((SEQ, D_MODEL), F32),
        compiler_params=_params(1),
        name=f"pool_moba_out_{split}",
    )(slopes, x, mod, pooled, wp, own_part, got_part, wa)


def _head_pad(w, axis):
    shape = list(w.shape)
    shape[axis:axis + 1] = [D_HEADS, HEAD_DIM]
    pad = [(0, 0)] * len(shape)
    pad[axis + 1] = (0, HEAD_PAD - HEAD_DIM)
    out = jnp.pad(w.reshape(shape), pad)
    shape[axis:axis + 2] = [D_HEADS * HEAD_PAD]
    return out.reshape(shape)


def kernel(x, c, ada_w, ada_b, ffn_norm, ffn_w_gate, ffn_w_up, ffn_w_down, mix_norm, conv_w_in,
           conv_a_w, conv_a_b, conv_a_ln_g, conv_a_ln_b, conv_b_w, conv_w_out, pm_w_in, pool_w,
           pool_b, pool_scale, pm_w_out, final_norm):
    assert x.shape == (1, SEQ, D_MODEL) and c.shape == (1, D_MODEL)
    xs = x.reshape(SEQ, D_MODEL)
    mod = _mod_call(c, ada_w, ada_b)

    wg = ffn_w_gate.astype(BF16)
    wu = ffn_w_up.astype(BF16)
    wd = ffn_w_down.astype(BF16)
    ffn_g = ffn_norm.reshape(DEPTH, 2, 1, D_MODEL)
    mix_g = mix_norm.reshape(DEPTH, 1, D_MODEL)
    fin = final_norm.reshape(1, D_MODEL)
    ffn = functools.partial(_ffn_call, mod=mod, norm_g=ffn_g, wg=wg, wu=wu, wd=wd, fin=fin)

    xs = ffn(xs, layer=0, sub=0, final=False)
    grp = np.arange(A_WIDTH) // (A_WIDTH // A_GROUPS)
    seg = jnp.asarray((grp[:, None] == grp[None, :]) / (A_WIDTH // A_GROUPS), dtype=BF16)
    xs = _conv_mix_call(
        xs, mod, mix_g, conv_w_in[0].astype(BF16), conv_a_w[0], conv_a_b[0].reshape(1, A_WIDTH),
        conv_a_ln_g[0].reshape(1, A_WIDTH), conv_a_ln_b[0].reshape(1, A_WIDTH), conv_b_w[0],
        conv_w_out[0].astype(BF16), seg, layer=0)
    xs = ffn(xs, layer=0, sub=1, final=False)

    xs = ffn(xs, layer=1, sub=0, final=False)
    w_in = pm_w_in[0].astype(BF16)
    w_u = w_in[:, 0:C_WIDTH]
    w_q = w_in[:, C_WIDTH:C_WIDTH + D_WIDTH]
    w_kvt = w_in[:, C_WIDTH + D_WIDTH:C_WIDTH + 3 * D_WIDTH].T
    pooled, qrows, kv, kmean = _pm_proj_call(
        xs, mod, mix_g, w_u, w_q, w_kvt, pool_w[0].astype(BF16),
        pool_b[0].reshape(1, C_WIDTH), pool_scale[0].reshape(1, C_WIDTH), layer=1)
    kmean_h = _head_pad(kmean, 1).reshape(N_BLOCKS, D_HEADS, HEAD_PAD).transpose(1, 0, 2)

    n_pair = MOBA_TOPK * MOBA_BLOCK
    tri = jnp.asarray(np.arange(n_pair)[:, None] < np.arange(n_pair)[None, :], dtype=BF16)
    q_flat = qrows.reshape(D_HEADS * SEQ, HEAD_PAD)
    own_parts, got_parts = [], []
    for split in range(N_SPLITS):
        sel, rank, cnt, own_part = _gate_own_call(qrows, kv, kmean_h, tri, split=split)
        base_digits, tile_h, tile_j, n_tiles, tile_step = _route_tables(cnt, split)
        pos = _route_pos_call(sel, rank, base_digits)
        pos_by_slot = [pos[s].reshape(1, SPLIT_HEADS * SEQ) for s in range(MOBA_TOPK)]
        routed_q = _dispatch_call(q_flat, pos_by_slot, split=split)
        routed_parts = _routed_attn_call(tile_h, tile_j, n_tiles, tile_step, routed_q, kv,
                                         split=split)
        got = _collect_call(routed_parts, pos.reshape(1, N_PAIRS))
        own_parts.append(own_part)
        got_parts.append(got.reshape(MOBA_TOPK, SPLIT_HEADS, SEQ, HEAD_PAD))

    slopes = jnp.asarray(2.0 ** (-8.0 * np.arange(1, D_HEADS + 1) / D_HEADS), dtype=F32)
    w_out = pm_w_out[0].astype(BF16)
    w_att = _head_pad(w_out[C_WIDTH:], 0).reshape(D_HEADS, HEAD_PAD, D_MODEL)
    for split in range(N_SPLITS):
        xs = _pm_out_call(slopes, xs, mod, pooled, own_parts[split], got_parts[split],
                          w_out[0:C_WIDTH], w_att, layer=1, split=split)
    xs = ffn(xs, layer=1, sub=1, final=True)
    return xs.reshape(1, SEQ, D_MODEL)
```

```python
import functools

import numpy as np
import jax
import jax.numpy as jnp
from jax import lax
from jax.experimental import pallas as pl
from jax.experimental.pallas import tpu as pltpu
from jax.experimental.pallas import tpu_sc as plsc

D_MODEL = 1024
SEQ = 16384
DEPTH = 2
N_SUBLAYERS = 3
D_FF = 2816
EPS = 1e-6

A_WIDTH = 512
A_GROUPS = 8
A_CONV = 31
B_WIDTH = 512
B_CONV = 3

POOL_WINDOWS = (2, 4, 8, 16)
C_WIDTH = 512
C_GROUP_DIM = 128
D_HEADS = 8
HEAD_DIM = 64
D_WIDTH = 512
MOBA_BLOCK = 256
MOBA_TOPK = 3
N_BLOCKS = SEQ // MOBA_BLOCK

V7X_SUBLANES = 8
V7X_VMEM_LIMIT_BYTES = 56 * 1024 * 1024

A_HALO = 32
B_HALO = 8
POOL_HALO = 16
NEG_BIG = -1e30

HEAD_PAD = 128
ALIBI_COL = HEAD_DIM
SLAB_K_ROWS = HEAD_DIM + 16
SLAB_ROWS = SLAB_K_ROWS + HEAD_DIM

SPLIT_HEADS = 4
N_SPLITS = D_HEADS // SPLIT_HEADS
ROUTE_TILE = 256
ROUTE_TILES_PER_STEP = 8
N_PAIRS = SPLIT_HEADS * SEQ * MOBA_TOPK
N_GROUPS = SPLIT_HEADS * N_BLOCKS
MAX_TILES = N_PAIRS // ROUTE_TILE + N_GROUPS
TRASH_ROW0 = MAX_TILES * ROUTE_TILE
ROUTE_ROWS = (MAX_TILES + ROUTE_TILES_PER_STEP) * ROUTE_TILE
SC_WINDOW = 128

BF16 = jnp.bfloat16
F32 = jnp.float32
I32 = jnp.int32


def _params(n_axes):
    return pltpu.CompilerParams(
        dimension_semantics=("arbitrary",) * n_axes,
        vmem_limit_bytes=V7X_VMEM_LIMIT_BYTES)


def _resident(block_shape, index_map):
    return pl.BlockSpec(block_shape, index_map, pipeline_mode=pl.Buffered(1))


def _full(shape):
    return _resident(shape, lambda *_: (0,) * len(shape))


def _sigmoid(v):
    return 1.0 / (1.0 + jnp.exp(-v))


def _norm_mod(x, g, mod):
    r = lax.rsqrt(jnp.mean(x * x, axis=-1, keepdims=True) + EPS)
    return (x * r) * (g * (1.0 + mod[1:2])) + mod[0:1]


def _split(v):
    hi = v.astype(BF16)
    return hi, (v - hi.astype(F32)).astype(BF16)


def _split_dot(v, w):
    hi, lo = _split(v)
    return (jnp.dot(hi, w, preferred_element_type=F32)
            + jnp.dot(lo, w, preferred_element_type=F32))


_NT = (((1,), (1,)), ((), ()))


def _mod_kernel(c_ref, w_ref, b_ref, o_ref):
    c = c_ref[...]
    cond = c * _sigmoid(c)
    o_ref[0] = jnp.sum(w_ref[0] * cond, axis=0, keepdims=True) + b_ref[0]


def _mod_call(c, ada_w, ada_b):
    n_out = N_SUBLAYERS * 3 * D_MODEL
    tn = D_MODEL
    out = pl.pallas_call(
        _mod_kernel,
        grid=(DEPTH, n_out // tn),
        in_specs=[
            pl.BlockSpec((D_MODEL, 1), lambda l, j: (0, 0)),
            pl.BlockSpec((1, D_MODEL, tn), lambda l, j: (l, 0, j)),
            pl.BlockSpec((1, 1, tn), lambda l, j: (l, 0, j)),
        ],
        out_specs=pl.BlockSpec((1, 1, tn), lambda l, j: (l, 0, j)),
        out_shape=jax.ShapeDtypeStruct((DEPTH, 1, n_out), F32),
        compiler_params=_params(2),
        name="adaln_mod",
    )(c.reshape(D_MODEL, 1), ada_w, ada_b.reshape(DEPTH, 1, n_out))
    return out.reshape(DEPTH, N_SUBLAYERS, 3, D_MODEL)


def _ffn_kernel(x_ref, mod_ref, g_ref, wg_ref, wu_ref, wd_ref, fin_ref, o_ref, *, final):
    x = x_ref[...]
    mod = mod_ref[...]
    h = _norm_mod(x, g_ref[...], mod).astype(BF16)
    gate = jnp.dot(h, wg_ref[...], preferred_element_type=F32)
    up = jnp.dot(h, wu_ref[...], preferred_element_type=F32)
    act = (gate * _sigmoid(gate) * up).astype(BF16)
    y = jnp.dot(act, wd_ref[...], preferred_element_type=F32)
    xn = x + (0.5 * mod[2:3]) * y
    if final:
        r = lax.rsqrt(jnp.mean(xn * xn, axis=-1, keepdims=True) + EPS)
        xn = xn * r * fin_ref[...]
    o_ref[...] = xn


def _ffn_call(x, mod, norm_g, wg, wu, wd, fin, *, layer, sub, final, tm=512):
    mod_sub = 0 if sub == 0 else 2
    return pl.pallas_call(
        functools.partial(_ffn_kernel, final=final),
        grid=(SEQ // tm,),
        in_specs=[
            pl.BlockSpec((tm, D_MODEL), lambda i: (i, 0)),
            _resident((None, None, 3, D_MODEL), lambda i: (layer, mod_sub, 0, 0)),
            _resident((None, None, 1, D_MODEL), lambda i: (layer, sub, 0, 0)),
            _resident((None, None, D_MODEL, D_FF), lambda i: (layer, sub, 0, 0)),
            _resident((None, None, D_MODEL, D_FF), lambda i: (layer, sub, 0, 0)),
            _resident((None, None, D_FF, D_MODEL), lambda i: (layer, sub, 0, 0)),
            _resident((1, D_MODEL), lambda i: (0, 0)),
        ],
        out_specs=pl.BlockSpec((tm, D_MODEL), lambda i: (i, 0)),
        out_shape=jax.ShapeDtypeStruct((SEQ, D_MODEL), F32),
        compiler_params=_params(1),
        name=f"ffn_l{layer}s{sub}",
    )(x, mod, norm_g, wg, wu, wd, fin)


def _conv_mix_kernel(x_ref, mod_ref, g_ref, win_ref, caw_ref, cab_ref, lng_ref, lnb_ref,
                     cbw_ref, wout_ref, seg_ref, o_ref, abuf, bbuf, sbuf, *, tm):
    i = pl.program_id(0)

    @pl.when(i == 0)
    def _():
        abuf[0:A_HALO, :] = jnp.zeros((A_HALO, A_WIDTH), F32)
        bbuf[0:B_HALO, :] = jnp.zeros((B_HALO, B_WIDTH), F32)

    x = x_ref[...]
    mod = mod_ref[...]
    h = _norm_mod(x, g_ref[...], mod).astype(BF16)
    z = jnp.dot(h, win_ref[...], preferred_element_type=F32)

    a = z[:, 0:A_WIDTH] * _sigmoid(z[:, A_WIDTH:2 * A_WIDTH])
    abuf[A_HALO:A_HALO + tm, :] = a
    acc = jnp.zeros((tm, A_WIDTH), F32) + cab_ref[...]
    ext = tm + V7X_SUBLANES
    for r in range(V7X_SUBLANES):
        part = None
        for q in range((A_CONV - 1 - r) // V7X_SUBLANES + 1):
            k = A_CONV - 1 - (V7X_SUBLANES * q + r)
            term = caw_ref[k:k + 1, :] * abuf[pl.ds(A_HALO - V7X_SUBLANES * (q + 1), ext), :]
            part = term if part is None else part + term
        if r == 0:
            acc = acc + part[V7X_SUBLANES:, :]
        else:
            sbuf[r - 1] = part
            acc = acc + sbuf[r - 1, pl.ds(V7X_SUBLANES - r, tm), :]
    abuf[0:A_HALO, :] = abuf[tm:tm + A_HALO, :]
    seg = seg_ref[...]
    mu = _split_dot(acc, seg)
    d = acc - mu
    var = _split_dot(d * d, seg)
    yn = d * lax.rsqrt(var + EPS) * lng_ref[...] + lnb_ref[...]
    a_out = yn * _sigmoid(yn)

    off = 2 * A_WIDTH
    cv = z[:, off + B_WIDTH:off + 2 * B_WIDTH] * z[:, off + 2 * B_WIDTH:off + 3 * B_WIDTH]
    bbuf[B_HALO:B_HALO + tm, :] = cv
    bacc = jnp.zeros((tm, B_WIDTH), F32)
    for k in range(B_CONV):
        bacc = bacc + cbw_ref[k:k + 1, :] * bbuf[pl.ds(B_HALO - (B_CONV - 1) + k, tm), :]
    bbuf[0:B_HALO, :] = bbuf[tm:tm + B_HALO, :]
    bb = z[:, off:off + B_WIDTH] * bacc

    y = (jnp.dot(a_out.astype(BF16), wout_ref[0:A_WIDTH, :], preferred_element_type=F32)
         + jnp.dot(bb.astype(BF16), wout_ref[A_WIDTH:A_WIDTH + B_WIDTH, :],
                   preferred_element_type=F32))
    o_ref[...] = x + mod[2:3] * y


def _conv_mix_call(x, mod, norm_g, w_in, caw, cab, lng, lnb, cbw, w_out, seg, *, layer, tm=512):
    even_in = w_in.shape[1]
    return pl.pallas_call(
        functools.partial(_conv_mix_kernel, tm=tm),
        grid=(SEQ // tm,),
        in_specs=[
            pl.BlockSpec((tm, D_MODEL), lambda i: (i, 0)),
            _resident((None, None, 3, D_MODEL), lambda i: (layer, 1, 0, 0)),
            _resident((None, 1, D_MODEL), lambda i: (layer, 0, 0)),
            _full((D_MODEL, even_in)),
            _full((A_CONV, A_WIDTH)),
            _full((1, A_WIDTH)),
            _full((1, A_WIDTH)),
            _full((1, A_WIDTH)),
            _full((B_CONV, B_WIDTH)),
            _full((A_WIDTH + B_WIDTH, D_MODEL)),
            _full((A_WIDTH, A_WIDTH)),
        ],
        out_specs=pl.BlockSpec((tm, D_MODEL), lambda i: (i, 0)),
        out_shape=jax.ShapeDtypeStruct((SEQ, D_MODEL), F32),
        scratch_shapes=[pltpu.VMEM((A_HALO + tm, A_WIDTH), F32),
                        pltpu.VMEM((B_HALO + tm, B_WIDTH), F32),
                        pltpu.VMEM((V7X_SUBLANES - 1, tm + V7X_SUBLANES, A_WIDTH), F32)],
        compiler_params=_params(1),
        name="conv_mixers",
    )(x, mod, norm_g, w_in, caw, cab, lng, lnb, cbw, w_out, seg)


def _pm_proj_kernel(x_ref, mod_ref, g_ref, wu_ref, wq_ref, wkvt_ref, pw_ref,
                    pb_ref, ps_ref, pooled_ref, qrow_ref, kv_ref, km_ref,
                    ubuf, hmean, *, tm):
    i = pl.program_id(0)

    @pl.when(i == 0)
    def _():
        ubuf[0:POOL_HALO, :] = jnp.zeros((POOL_HALO, C_WIDTH), F32)

    x = x_ref[...]
    h = _norm_mod(x, g_ref[...], mod_ref[...]).astype(BF16)
    u = jnp.dot(h, wu_ref[...], preferred_element_type=F32)
    qr = jnp.dot(h, wq_ref[...], preferred_element_type=F32)
    kvt = lax.dot_general(wkvt_ref[...], h, _NT, preferred_element_type=F32)

    ubuf[POOL_HALO:POOL_HALO + tm, :] = u
    t1 = (i * tm + 1 + lax.broadcasted_iota(I32, (tm, C_GROUP_DIM), 0)).astype(F32)
    for gi, w in enumerate(POOL_WINDOWS):
        c0 = gi * C_GROUP_DIM
        s = u[:, c0:c0 + C_GROUP_DIM]
        for k in range(1, w):
            s = s + ubuf[pl.ds(POOL_HALO - k, tm), c0:c0 + C_GROUP_DIM]
        pooled = s / jnp.minimum(t1, float(w)) - u[:, c0:c0 + C_GROUP_DIM]
        mixed = (jnp.dot(pooled.astype(BF16), pw_ref[gi], preferred_element_type=F32)
                 + pb_ref[:, c0:c0 + C_GROUP_DIM])
        pooled_ref[:, c0:c0 + C_GROUP_DIM] = (
            mixed * ps_ref[:, c0:c0 + C_GROUP_DIM]).astype(BF16)
    ubuf[0:POOL_HALO, :] = ubuf[tm:tm + POOL_HALO, :]

    lane = lax.broadcasted_iota(I32, (tm, HEAD_PAD), 1)
    tail = jnp.where(lane == ALIBI_COL, 1.0, 0.0)
    qs = qr * (HEAD_DIM ** -0.5)
    for pair in range(D_HEADS // 2):
        both = qs[:, pair * HEAD_PAD:(pair + 1) * HEAD_PAD]
        qrow_ref[2 * pair] = jnp.where(lane < HEAD_DIM, both, tail)
        qrow_ref[2 * pair + 1] = jnp.where(
            lane < HEAD_DIM, pltpu.roll(both, HEAD_DIM, axis=1), tail)
    per_tile = tm // MOBA_BLOCK
    kvb = kvt.astype(BF16)
    n_alibi = SLAB_K_ROWS - HEAD_DIM
    first = lax.broadcasted_iota(I32, (n_alibi, MOBA_BLOCK), 0) == 0
    koff = lax.broadcasted_iota(I32, (n_alibi, MOBA_BLOCK), 1).astype(F32)
    for hh in range(D_HEADS):
        alibi = jnp.where(first, koff * (2.0 ** -(hh + 1)), 0.0).astype(BF16)
        for sb in range(per_tile):
            keys = slice(sb * MOBA_BLOCK, (sb + 1) * MOBA_BLOCK)
            kv_ref[hh, sb, 0:HEAD_DIM, :] = kvb[hh * HEAD_DIM:(hh + 1) * HEAD_DIM, keys]
            kv_ref[hh, sb, HEAD_DIM:SLAB_K_ROWS, :] = alibi
            kv_ref[hh, sb, SLAB_K_ROWS:SLAB_ROWS, :] = kvb[
                D_WIDTH + hh * HEAD_DIM:D_WIDTH + (hh + 1) * HEAD_DIM, keys]

    for sb in range(per_tile):
        blk = h[sb * MOBA_BLOCK:(sb + 1) * MOBA_BLOCK, :].astype(F32)
        hmean[pl.ds(i * per_tile + sb, 1), :] = jnp.mean(blk, axis=0, keepdims=True)

    @pl.when(i == pl.num_programs(0) - 1)
    def _():
        hi, lo = _split(hmean[...])
        wkt = wkvt_ref[0:D_WIDTH, :]
        km_ref[...] = (lax.dot_general(hi, wkt, _NT, preferred_element_type=F32)
                       + lax.dot_general(lo, wkt, _NT, preferred_element_type=F32))


def _pm_proj_call(x, mod, norm_g, wu, wq, wkvt, pw, pb, ps, *, layer, tm=512):
    return pl.pallas_call(
        functools.partial(_pm_proj_kernel, tm=tm),
        grid=(SEQ // tm,),
        in_specs=[
            pl.BlockSpec((tm, D_MODEL), lambda i: (i, 0)),
            _resident((None, None, 3, D_MODEL), lambda i: (layer, 1, 0, 0)),
            _resident((None, 1, D_MODEL), lambda i: (layer, 0, 0)),
            _full((D_MODEL, C_WIDTH)),
            _full((D_MODEL, D_WIDTH)),
            _full((2 * D_WIDTH, D_MODEL)),
            _full((len(POOL_WINDOWS), C_GROUP_DIM, C_GROUP_DIM)),
            _full((1, C_WIDTH)),
            _full((1, C_WIDTH)),
        ],
        out_specs=[
            pl.BlockSpec((tm, C_WIDTH), lambda i: (i, 0)),
            pl.BlockSpec((D_HEADS, tm, HEAD_PAD), lambda i: (0, i, 0)),
            pl.BlockSpec((D_HEADS, tm // MOBA_BLOCK, SLAB_ROWS, MOBA_BLOCK),
                         lambda i: (0, i, 0, 0)),
            pl.BlockSpec((N_BLOCKS, D_WIDTH), lambda i: (0, 0)),
        ],
        out_shape=[
            jax.ShapeDtypeStruct((SEQ, C_WIDTH), BF16),
            jax.ShapeDtypeStruct((D_HEADS, SEQ, HEAD_PAD), F32),
            jax.ShapeDtypeStruct((D_HEADS, N_BLOCKS, SLAB_ROWS, MOBA_BLOCK), BF16),
            jax.ShapeDtypeStruct((N_BLOCKS, D_WIDTH), F32),
        ],
        scratch_shapes=[pltpu.VMEM((POOL_HALO + tm, C_WIDTH), F32),
                        pltpu.VMEM((N_BLOCKS, D_MODEL), F32)],
        compiler_params=_params(1),
        name="pool_qkv_proj",
    )(x, mod, norm_g, wu, wq, wkvt, pw, pb, ps)


def _block_rows(qs, slabs, mask, lse_add):
    n = qs[0].shape[0]
    kpad = jnp.zeros((HEAD_PAD - SLAB_K_ROWS, MOBA_BLOCK), BF16)
    s = jnp.concatenate(
        [jnp.dot(q, jnp.concatenate([slab[0:SLAB_K_ROWS, :], kpad], axis=0),
                 preferred_element_type=F32)
         for q, slab in zip(qs, slabs)], axis=0)
    if mask is not None:
        s = jnp.where(mask, s, NEG_BIG)
    m = jnp.max(s, axis=1, keepdims=True)
    p = jnp.exp(s - m).astype(BF16)
    vpad = jnp.zeros((HEAD_PAD - HEAD_DIM, MOBA_BLOCK), BF16)
    ones = jnp.ones((HEAD_PAD, MOBA_BLOCK), BF16)
    ot = jnp.concatenate(
        [lax.dot_general(p[u * n:(u + 1) * n],
                         jnp.concatenate([slab[SLAB_K_ROWS:SLAB_ROWS, :], vpad, ones], axis=0),
                         _NT, preferred_element_type=F32)
         for u, slab in enumerate(slabs)], axis=0)
    o, tot = ot[:, 0:HEAD_PAD], ot[:, HEAD_PAD:2 * HEAD_PAD]
    lse = m + jnp.log(tot)
    if lse_add is not None:
        lse = jnp.concatenate(
            [lse[u * n:(u + 1) * n] + add for u, add in enumerate(lse_add)], axis=0)
    low = lax.broadcasted_iota(I32, o.shape, 1) < HEAD_DIM
    return jnp.where(low, o / tot, lse)


def _onehot_pairs(sel_rows):
    blk = lax.broadcasted_iota(I32, (N_BLOCKS, MOBA_BLOCK), 0)
    return jnp.concatenate(
        [jnp.where(blk == sel_rows[s], 1.0, 0.0) for s in range(MOBA_TOPK)], axis=1)


def _gate_own_kernel(qrow_ref, kv_ref, km_ref, tri_ref,
                     sel_ref, rank_ref, cnt_ref, own_ref, *, heads):
    own = pl.program_id(0)
    n_pair = MOBA_TOPK * MOBA_BLOCK
    qs = [qrow_ref[a].astype(BF16) for a in range(heads)]

    gates = []
    for a in range(heads):
        km_hi, km_lo = _split(km_ref[a])
        gates.append(lax.dot_general(km_hi, qs[a], _NT, preferred_element_type=F32)
                     + lax.dot_general(km_lo, qs[a], _NT, preferred_element_type=F32))
    blk = lax.broadcasted_iota(I32, (N_BLOCKS, heads * MOBA_BLOCK), 0)
    g = jnp.where(blk < own, jnp.concatenate(gates, axis=1), -jnp.inf)
    slots = []
    for s in range(MOBA_TOPK):
        best = jnp.max(g, axis=0, keepdims=True)
        idx = jnp.min(jnp.where(g == best, blk, N_BLOCKS), axis=0, keepdims=True)
        g = jnp.where(blk == idx, -jnp.inf, g)
        slots.append(jnp.where(s < own, idx, N_BLOCKS))

    onehots = []
    for a in range(heads):
        sel_rows = [slot[:, a * MOBA_BLOCK:(a + 1) * MOBA_BLOCK] for slot in slots]
        sel_ref[a] = jnp.concatenate(sel_rows, axis=0)
        onehots.append(_onehot_pairs(sel_rows))
    onehot = jnp.concatenate(onehots, axis=0)
    oh = onehot.astype(BF16)
    before = jnp.dot(oh, tri_ref[...], preferred_element_type=F32)
    hit = onehot * before
    ones = jnp.ones((8, n_pair), BF16)
    cnt = lax.dot_general(ones, oh, _NT, preferred_element_type=F32)
    for a in range(heads):
        rank = jnp.sum(hit[a * N_BLOCKS:(a + 1) * N_BLOCKS], axis=0, keepdims=True).astype(I32)
        rank_ref[a] = jnp.concatenate(
            [rank[:, s * MOBA_BLOCK:(s + 1) * MOBA_BLOCK] for s in range(MOBA_TOPK)], axis=0)
        cnt_ref[a] = cnt[0:1, a * N_BLOCKS:(a + 1) * N_BLOCKS].astype(I32)

    qi = lax.broadcasted_iota(I32, (heads * MOBA_BLOCK, MOBA_BLOCK), 0) % MOBA_BLOCK
    ki = lax.broadcasted_iota(I32, (heads * MOBA_BLOCK, MOBA_BLOCK), 1)
    rows = _block_rows(qs, [kv_ref.at[a] for a in range(heads)], ki <= qi, None)
    for a in range(heads):
        own_ref[a] = rows[a * MOBA_BLOCK:(a + 1) * MOBA_BLOCK]


def _gate_own_call(qrows, kv, kmean_h, tri, *, split):
    heads = SPLIT_HEADS
    return pl.pallas_call(
        functools.partial(_gate_own_kernel, heads=heads),
        grid=(N_BLOCKS,),
        in_specs=[
            pl.BlockSpec((heads, MOBA_BLOCK, HEAD_PAD), lambda b: (split, b, 0)),
            pl.BlockSpec((heads, None, SLAB_ROWS, MOBA_BLOCK), lambda b: (split, b, 0, 0)),
            pl.BlockSpec((heads, N_BLOCKS, HEAD_PAD), lambda b: (split, 0, 0)),
            _full((MOBA_TOPK * MOBA_BLOCK, MOBA_TOPK * MOBA_BLOCK)),
        ],
        out_specs=[
            pl.BlockSpec((heads, MOBA_TOPK, MOBA_BLOCK), lambda b: (0, 0, b)),
            pl.BlockSpec((heads, MOBA_TOPK, MOBA_BLOCK), lambda b: (0, 0, b)),
            pl.BlockSpec((heads, None, 1, N_BLOCKS), lambda b: (0, b, 0, 0)),
            pl.BlockSpec((heads, MOBA_BLOCK, HEAD_PAD), lambda b: (0, b, 0)),
        ],
        out_shape=[
            jax.ShapeDtypeStruct((heads, MOBA_TOPK, SEQ), I32),
            jax.ShapeDtypeStruct((heads, MOBA_TOPK, SEQ), I32),
            jax.ShapeDtypeStruct((heads, N_BLOCKS, 1, N_BLOCKS), I32),
            jax.ShapeDtypeStruct((heads, SEQ, HEAD_PAD), F32),
        ],
        compiler_params=_params(1),
        name=f"moba_gate_own_{split}",
    )(qrows, kv, kmean_h, tri)


def _route_pos_kernel(sel_ref, rank_ref, base_ref, pos_ref, *, blocks):
    lane = lax.broadcasted_iota(I32, (1, MOBA_BLOCK), 1)
    for bb in range(blocks):
        qs = slice(bb * MOBA_BLOCK, (bb + 1) * MOBA_BLOCK)
        rows = [[] for _ in range(MOBA_TOPK)]
        for a in range(SPLIT_HEADS):
            sel = sel_ref[a, :, qs]
            onehot = _onehot_pairs([sel[s:s + 1, :] for s in range(MOBA_TOPK)]).astype(BF16)
            dig = jnp.dot(base_ref[a, bb].astype(BF16), onehot,
                          preferred_element_type=F32)
            base = (dig[0:1] * 16384.0 + dig[1:2] * 128.0 + dig[2:3]).astype(I32)
            for s in range(MOBA_TOPK):
                p = base[:, s * MOBA_BLOCK:(s + 1) * MOBA_BLOCK] + rank_ref[a, s:s + 1, qs]
                rows[s].append(
                    jnp.where(sel[s:s + 1, :] < N_BLOCKS, p, TRASH_ROW0 + lane % SC_WINDOW))
        for s in range(MOBA_TOPK):
            pos_ref[s, :, qs] = jnp.concatenate(rows[s], axis=0)


def _route_pos_call(sel, rank, base_digits, *, blocks=8):
    width = blocks * MOBA_BLOCK
    return pl.pallas_call(
        functools.partial(_route_pos_kernel, blocks=blocks),
        grid=(N_BLOCKS // blocks,),
        in_specs=[
            pl.BlockSpec((SPLIT_HEADS, MOBA_TOPK, width), lambda b: (0, 0, b)),
            pl.BlockSpec((SPLIT_HEADS, MOBA_TOPK, width), lambda b: (0, 0, b)),
            pl.BlockSpec((SPLIT_HEADS, blocks, 8, N_BLOCKS), lambda b: (0, b, 0, 0)),
        ],
        out_specs=pl.BlockSpec((MOBA_TOPK, SPLIT_HEADS, width), lambda b: (0, 0, b)),
        out_shape=jax.ShapeDtypeStruct((MOBA_TOPK, SPLIT_HEADS, SEQ), I32),
        compiler_params=_params(1),
        name="moba_route_pos",
    )(sel, rank, base_digits)


def _route_tables(cnt, split):
    cnt = cnt.reshape(SPLIT_HEADS, N_BLOCKS, N_BLOCKS)
    tiles = (cnt.sum(axis=1) + ROUTE_TILE - 1) // ROUTE_TILE
    tiles_flat = tiles.reshape(N_GROUPS)
    tile0 = jnp.cumsum(tiles_flat) - tiles_flat
    n_tiles = tiles_flat.sum().astype(I32)
    within = jnp.cumsum(cnt, axis=1) - cnt
    base = tile0.reshape(SPLIT_HEADS, 1, N_BLOCKS) * ROUTE_TILE + within
    digits = jnp.stack([base // 16384, (base // 128) % 128, base % 128], axis=2).astype(F32)
    digits = jnp.pad(digits, ((0, 0), (0, 0), (0, 8 - 3), (0, 0)))
    t = jnp.arange(MAX_TILES, dtype=I32)
    ended = (tile0 + tiles_flat)[None, :] <= t[:, None]
    group_of_tile = jnp.minimum(ended.sum(axis=1).astype(I32), N_GROUPS - 1)
    tile_h = group_of_tile // N_BLOCKS
    head = split * SPLIT_HEADS + tile_h
    tile_step = lax.bitcast_convert_type((127 + 7 - head) << 23, F32)
    return digits, tile_h, group_of_tile % N_BLOCKS, n_tiles.reshape(1), tile_step


def _sc_mesh():
    return plsc.VectorSubcoreMesh(core_axis_name="core", subcore_axis_name="subcore")


def _dispatch_call(rows, pos_by_slot, *, split):
    n_rows = SPLIT_HEADS * SEQ
    first_window = split * n_rows // SC_WINDOW

    @functools.partial(
        pl.kernel, mesh=_sc_mesh(), scratch_types=[],
        out_type=jax.ShapeDtypeStruct((ROUTE_ROWS, HEAD_PAD), F32))
    def dispatch(x_hbm, i0_hbm, i1_hbm, i2_hbm, o_hbm):
        def body(x_vmem, i0_vmem, i1_vmem, i2_vmem):
            pltpu.sync_copy(x_vmem, o_hbm.at[i0_vmem.at[0]])
            pltpu.sync_copy(x_vmem, o_hbm.at[i1_vmem.at[0]])
            pltpu.sync_copy(x_vmem, o_hbm.at[i2_vmem.at[0]])

        idx_spec = pl.BlockSpec((1, SC_WINDOW), lambda i: (0, i))
        pltpu.emit_pipeline(
            body,
            grid=(n_rows // SC_WINDOW,),
            in_specs=[pl.BlockSpec((SC_WINDOW, HEAD_PAD), lambda i: (first_window + i, 0)),
                      idx_spec, idx_spec, idx_spec],
            out_specs=[],
            core_axis_name=("core", "subcore"),
            dimension_semantics=(pltpu.PARALLEL,),
        )(x_hbm, i0_hbm, i1_hbm, i2_hbm)

    return dispatch(rows, *pos_by_slot)


def _collect_call(table, idx):
    n_rows = idx.shape[1]

    @functools.partial(
        pl.kernel, mesh=_sc_mesh(), scratch_types=[],
        out_type=jax.ShapeDtypeStruct((n_rows, HEAD_PAD), F32))
    def collect(x_hbm, i_hbm, o_hbm):
        def body(i_vmem, o_vmem):
            pltpu.sync_copy(x_hbm.at[i_vmem.at[0]], o_vmem)

        pltpu.emit_pipeline(
            body,
            grid=(n_rows // SC_WINDOW,),
            in_specs=[pl.BlockSpec((1, SC_WINDOW), lambda i: (0, i))],
            out_specs=[pl.BlockSpec((SC_WINDOW, HEAD_PAD), lambda i: (i, 0))],
            core_axis_name=("core", "subcore"),
            dimension_semantics=(pltpu.PARALLEL,),
        )(i_hbm, o_hbm)

    return collect(table, idx)


def _routed_attn_kernel(th_ref, tj_ref, nt_ref, step_ref, q_ref, *refs, tiles):
    kv_refs, o_ref = refs[0:tiles], refs[tiles]
    t0 = pl.program_id(0) * tiles

    @pl.when(t0 < nt_ref[0])
    def _():
        qs = [q_ref[u * ROUTE_TILE:(u + 1) * ROUTE_TILE, :].astype(BF16) for u in range(tiles)]
        adds = [step_ref[t0 + u] * tj_ref[t0 + u].astype(F32) for u in range(tiles)]
        o_ref[...] = _block_rows(qs, kv_refs, None, adds)


def _routed_attn_call(tile_h, tile_j, n_tiles, tile_step, routed_q, kv,
                      *, split, tiles=ROUTE_TILES_PER_STEP):
    live = lambda s, th, tj, nt: (jnp.minimum(s, (nt[0] + tiles - 1) // tiles), 0)
    head0 = split * SPLIT_HEADS

    def kv_spec(u):
        return pl.BlockSpec(
            (None, None, SLAB_ROWS, MOBA_BLOCK),
            lambda s, th, tj, nt: (head0 + th[s * tiles + u], tj[s * tiles + u], 0, 0))

    grid_spec = pltpu.PrefetchScalarGridSpec(
        num_scalar_prefetch=3,
        grid=(MAX_TILES // tiles,),
        in_specs=([pl.BlockSpec(memory_space=pltpu.SMEM),
                   pl.BlockSpec((tiles * ROUTE_TILE, HEAD_PAD), live)]
                  + [kv_spec(u) for u in range(tiles)]),
        out_specs=pl.BlockSpec((tiles * ROUTE_TILE, HEAD_PAD), live),
    )
    return pl.pallas_call(
        functools.partial(_routed_attn_kernel, tiles=tiles),
        grid_spec=grid_spec,
        out_shape=jax.ShapeDtypeStruct((ROUTE_ROWS, HEAD_PAD), F32),
        compiler_params=_params(1),
        name=f"moba_routed_attn_{split}",
    )(tile_h, tile_j, n_tiles, tile_step, routed_q, *([kv] * tiles))


def _pm_out_kernel(slope_ref, x_ref, mod_ref, pooled_ref, *refs):
    own_refs, got_refs = refs[0:N_SPLITS], refs[N_SPLITS:2 * N_SPLITS]
    wp_ref, wa_ref, o_ref = refs[2 * N_SPLITS:]
    own = pl.program_id(0)
    low = lax.broadcasted_iota(I32, (MOBA_BLOCK, HEAD_PAD), 1) < HEAD_DIM

    def lse_of(part):
        return jnp.where(low, pltpu.roll(part, HEAD_DIM, axis=1), part)

    y = jnp.dot(pooled_ref[...], wp_ref[...], preferred_element_type=F32)
    for hh in range(D_HEADS):
        own_ref, got_ref = own_refs[hh // SPLIT_HEADS], got_refs[hh // SPLIT_HEADS]
        a = hh % SPLIT_HEADS
        own_shift = slope_ref[hh] * (own * MOBA_BLOCK).astype(F32)
        parts = [own_ref[a]]
        lses = [lse_of(parts[0])]
        for s in range(MOBA_TOPK):
            valid = s < own
            part = got_ref[s, a]
            parts.append(jnp.where(valid, part, 0.0))
            lses.append(jnp.where(valid, lse_of(part) - own_shift, NEG_BIG))
        top = functools.reduce(jnp.maximum, lses)
        num = jnp.zeros((MOBA_BLOCK, HEAD_PAD), F32)
        den = jnp.zeros((MOBA_BLOCK, HEAD_PAD), F32)
        for part, ls in zip(parts, lses):
            w = jnp.exp(ls - top)
            num = num + w * part
            den = den + w
        att = (num / den).astype(BF16)
        y = y + jnp.dot(att, wa_ref[hh], preferred_element_type=F32)
    o_ref[...] = x_ref[...] + mod_ref[2:3, :] * y


def _pm_out_call(slopes, x, mod, pooled, own_parts, got_parts, wp, wa, *, layer):
    tm = MOBA_BLOCK
    return pl.pallas_call(
        _pm_out_kernel,
        grid=(SEQ // tm,),
        in_specs=[
            pl.BlockSpec(memory_space=pltpu.SMEM),
            pl.BlockSpec((tm, D_MODEL), lambda i: (i, 0)),
            _resident((None, None, 3, D_MODEL), lambda i: (layer, 1, 0, 0)),
            pl.BlockSpec((tm, C_WIDTH), lambda i: (i, 0)),
        ] + [pl.BlockSpec((SPLIT_HEADS, tm, HEAD_PAD), lambda i: (0, i, 0))] * N_SPLITS
        + [pl.BlockSpec((MOBA_TOPK, SPLIT_HEADS, tm, HEAD_PAD), lambda i: (0, 0, i, 0))] * N_SPLITS
        + [
            _full((C_WIDTH, D_MODEL)),
            _full((D_HEADS, HEAD_PAD, D_MODEL)),
        ],
        out_specs=pl.BlockSpec((tm, D_MODEL), lambda i: (i, 0)),
        out_shape=jax.ShapeDtypeStruct((SEQ, D_MODEL), F32),
        compiler_params=_params(1),
        name="pool_moba_out",
    )(slopes, x, mod, pooled, *own_parts, *got_parts, wp, wa)


def _head_pad(w, axis):
    shape = list(w.shape)
    shape[axis:axis + 1] = [D_HEADS, HEAD_DIM]
    pad = [(0, 0)] * len(shape)
    pad[axis + 1] = (0, HEAD_PAD - HEAD_DIM)
    out = jnp.pad(w.reshape(shape), pad)
    shape[axis:axis + 2] = [D_HEADS * HEAD_PAD]
    return out.reshape(shape)


def kernel(x, c, ada_w, ada_b, ffn_norm, ffn_w_gate, ffn_w_up, ffn_w_down, mix_norm, conv_w_in,
           conv_a_w, conv_a_b, conv_a_ln_g, conv_a_ln_b, conv_b_w, conv_w_out, pm_w_in, pool_w,
           pool_b, pool_scale, pm_w_out, final_norm):
    assert x.shape == (1, SEQ, D_MODEL) and c.shape == (1, D_MODEL)
    xs = x.reshape(SEQ, D_MODEL)
    mod = _mod_call(c, ada_w, ada_b)

    wg = ffn_w_gate.astype(BF16)
    wu = ffn_w_up.astype(BF16)
    wd = ffn_w_down.astype(BF16)
    ffn_g = ffn_norm.reshape(DEPTH, 2, 1, D_MODEL)
    mix_g = mix_norm.reshape(DEPTH, 1, D_MODEL)
    fin = final_norm.reshape(1, D_MODEL)
    ffn = functools.partial(_ffn_call, mod=mod, norm_g=ffn_g, wg=wg, wu=wu, wd=wd, fin=fin)

    xs = ffn(xs, layer=0, sub=0, final=False)
    grp = np.arange(A_WIDTH) // (A_WIDTH // A_GROUPS)
    seg = jnp.asarray((grp[:, None] == grp[None, :]) / (A_WIDTH // A_GROUPS), dtype=BF16)
    xs = _conv_mix_call(
        xs, mod, mix_g, conv_w_in[0].astype(BF16), conv_a_w[0], conv_a_b[0].reshape(1, A_WIDTH),
        conv_a_ln_g[0].reshape(1, A_WIDTH), conv_a_ln_b[0].reshape(1, A_WIDTH), conv_b_w[0],
        conv_w_out[0].astype(BF16), seg, layer=0)
    xs = ffn(xs, layer=0, sub=1, final=False)

    xs = ffn(xs, layer=1, sub=0, final=False)
    w_in = pm_w_in[0].astype(BF16)
    w_u = w_in[:, 0:C_WIDTH]
    w_q = w_in[:, C_WIDTH:C_WIDTH + D_WIDTH]
    w_kvt = w_in[:, C_WIDTH + D_WIDTH:C_WIDTH + 3 * D_WIDTH].T
    pooled, qrows, kv, kmean = _pm_proj_call(
        xs, mod, mix_g, w_u, w_q, w_kvt, pool_w[0].astype(BF16),
        pool_b[0].reshape(1, C_WIDTH), pool_scale[0].reshape(1, C_WIDTH), layer=1)
    kmean_h = _head_pad(kmean, 1).reshape(N_BLOCKS, D_HEADS, HEAD_PAD).transpose(1, 0, 2)

    n_pair = MOBA_TOPK * MOBA_BLOCK
    tri = jnp.asarray(np.arange(n_pair)[:, None] < np.arange(n_pair)[None, :], dtype=BF16)
    q_flat = qrows.reshape(D_HEADS * SEQ, HEAD_PAD)
    own_parts, got_parts = [], []
    for split in range(N_SPLITS):
        sel, rank, cnt, own_part = _gate_own_call(qrows, kv, kmean_h, tri, split=split)
        base_digits, tile_h, tile_j, n_tiles, tile_step = _route_tables(cnt, split)
        pos = _route_pos_call(sel, rank, base_digits)
        pos_by_slot = [pos[s].reshape(1, SPLIT_HEADS * SEQ) for s in range(MOBA_TOPK)]
        routed_q = _dispatch_call(q_flat, pos_by_slot, split=split)
        routed_parts = _routed_attn_call(tile_h, tile_j, n_tiles, tile_step, routed_q, kv,
                                         split=split)
        got = _collect_call(routed_parts, pos.reshape(1, N_PAIRS))
        own_parts.append(own_part)
        got_parts.append(got.reshape(MOBA_TOPK, SPLIT_HEADS, SEQ, HEAD_PAD))

    slopes = jnp.asarray(2.0 ** (-8.0 * np.arange(1, D_HEADS + 1) / D_HEADS), dtype=F32)
    w_out = pm_w_out[0].astype(BF16)
    w_att = _head_pad(w_out[C_WIDTH:], 0).reshape(D_HEADS, HEAD_PAD, D_MODEL)
    xs = _pm_out_call(slopes, xs, mod, pooled, own_parts, got_parts, w_out[0:C_WIDTH], w_att,
                      layer=1)
    xs = ffn(xs, layer=1, sub=1, final=True)
    return xs.reshape(1, SEQ, D_MODEL)
```

```python
import functools

import numpy as np
import jax
import jax.numpy as jnp
from jax import lax
from jax.experimental import pallas as pl
from jax.experimental.pallas import tpu as pltpu
from jax.experimental.pallas import tpu_sc as plsc

D_MODEL = 1024
SEQ = 16384
DEPTH = 2
N_SUBLAYERS = 3
D_FF = 2816
EPS = 1e-6

A_WIDTH = 512
A_GROUPS = 8
A_CONV = 31
B_WIDTH = 512
B_CONV = 3

POOL_WINDOWS = (2, 4, 8, 16)
C_WIDTH = 512
C_GROUP_DIM = 128
D_HEADS = 8
HEAD_DIM = 64
D_WIDTH = 512
MOBA_BLOCK = 256
MOBA_TOPK = 3
N_BLOCKS = SEQ // MOBA_BLOCK

V7X_SUBLANES = 8
V7X_VMEM_LIMIT_BYTES = 56 * 1024 * 1024

A_HALO = 32
B_HALO = 8
POOL_HALO = 16
NEG_BIG = -1e30

HEAD_PAD = 128
ALIBI_COL = HEAD_DIM
SLAB_K_ROWS = HEAD_DIM + 16
SLAB_ROWS = SLAB_K_ROWS + HEAD_DIM

SPLIT_HEADS = 4
N_SPLITS = D_HEADS // SPLIT_HEADS
ROUTE_TILE = 256
ROUTE_TILES_PER_STEP = 8
N_PAIRS = SPLIT_HEADS * SEQ * MOBA_TOPK
N_GROUPS = SPLIT_HEADS * N_BLOCKS
MAX_TILES = N_PAIRS // ROUTE_TILE + N_GROUPS
TRASH_ROW0 = MAX_TILES * ROUTE_TILE
ROUTE_ROWS = (MAX_TILES + ROUTE_TILES_PER_STEP) * ROUTE_TILE
SC_WINDOW = 128

BF16 = jnp.bfloat16
F32 = jnp.float32
I32 = jnp.int32


def _params(n_axes):
    return pltpu.CompilerParams(
        dimension_semantics=("arbitrary",) * n_axes,
        vmem_limit_bytes=V7X_VMEM_LIMIT_BYTES)


def _resident(block_shape, index_map):
    return pl.BlockSpec(block_shape, index_map, pipeline_mode=pl.Buffered(1))


def _full(shape):
    return _resident(shape, lambda *_: (0,) * len(shape))


def _sigmoid(v):
    return 1.0 / (1.0 + jnp.exp(-v))


def _norm_mod(x, g, mod):
    r = lax.rsqrt(jnp.mean(x * x, axis=-1, keepdims=True) + EPS)
    return (x * r) * (g * (1.0 + mod[1:2])) + mod[0:1]


def _split(v):
    hi = v.astype(BF16)
    return hi, (v - hi.astype(F32)).astype(BF16)


def _split_dot(v, w):
    hi, lo = _split(v)
    return (jnp.dot(hi, w, preferred_element_type=F32)
            + jnp.dot(lo, w, preferred_element_type=F32))


_NT = (((1,), (1,)), ((), ()))


def _mod_kernel(c_ref, w_ref, b_ref, o_ref):
    c = c_ref[...]
    cond = c * _sigmoid(c)
    o_ref[0] = jnp.sum(w_ref[0] * cond, axis=0, keepdims=True) + b_ref[0]


def _mod_call(c, ada_w, ada_b):
    n_out = N_SUBLAYERS * 3 * D_MODEL
    tn = D_MODEL
    out = pl.pallas_call(
        _mod_kernel,
        grid=(DEPTH, n_out // tn),
        in_specs=[
            pl.BlockSpec((D_MODEL, 1), lambda l, j: (0, 0)),
            pl.BlockSpec((1, D_MODEL, tn), lambda l, j: (l, 0, j)),
            pl.BlockSpec((1, 1, tn), lambda l, j: (l, 0, j)),
        ],
        out_specs=pl.BlockSpec((1, 1, tn), lambda l, j: (l, 0, j)),
        out_shape=jax.ShapeDtypeStruct((DEPTH, 1, n_out), F32),
        compiler_params=_params(2),
        name="adaln_mod",
    )(c.reshape(D_MODEL, 1), ada_w, ada_b.reshape(DEPTH, 1, n_out))
    return out.reshape(DEPTH, N_SUBLAYERS, 3, D_MODEL)


def _ffn_kernel(x_ref, mod_ref, g_ref, wg_ref, wu_ref, wd_ref, fin_ref, o_ref, *, final):
    x = x_ref[...]
    mod = mod_ref[...]
    h = _norm_mod(x, g_ref[...], mod).astype(BF16)
    gate = jnp.dot(h, wg_ref[...], preferred_element_type=F32)
    up = jnp.dot(h, wu_ref[...], preferred_element_type=F32)
    act = (gate * _sigmoid(gate) * up).astype(BF16)
    y = jnp.dot(act, wd_ref[...], preferred_element_type=F32)
    xn = x + (0.5 * mod[2:3]) * y
    if final:
        r = lax.rsqrt(jnp.mean(xn * xn, axis=-1, keepdims=True) + EPS)
        xn = xn * r * fin_ref[...]
    o_ref[...] = xn


def _ffn_call(x, mod, norm_g, w_gate, w_up, w_down, fin, *, layer, sub, final, tm=512):
    mod_sub = 0 if sub == 0 else 2
    wg, wu, wd = (w[layer, sub].astype(BF16) for w in (w_gate, w_up, w_down))
    return pl.pallas_call(
        functools.partial(_ffn_kernel, final=final),
        grid=(SEQ // tm,),
        in_specs=[
            pl.BlockSpec((tm, D_MODEL), lambda i: (i, 0)),
            _resident((None, None, 3, D_MODEL), lambda i: (layer, mod_sub, 0, 0)),
            _resident((None, None, 1, D_MODEL), lambda i: (layer, sub, 0, 0)),
            _full((D_MODEL, D_FF)),
            _full((D_MODEL, D_FF)),
            _full((D_FF, D_MODEL)),
            _resident((1, D_MODEL), lambda i: (0, 0)),
        ],
        out_specs=pl.BlockSpec((tm, D_MODEL), lambda i: (i, 0)),
        out_shape=jax.ShapeDtypeStruct((SEQ, D_MODEL), F32),
        compiler_params=_params(1),
        name=f"ffn_l{layer}s{sub}",
    )(x, mod, norm_g, wg, wu, wd, fin)


def _conv_mix_kernel(x_ref, mod_ref, g_ref, win_ref, caw_ref, cab_ref, lng_ref, lnb_ref,
                     cbw_ref, wout_ref, seg_ref, o_ref, abuf, bbuf, sbuf, *, tm):
    i = pl.program_id(0)

    @pl.when(i == 0)
    def _():
        abuf[0:A_HALO, :] = jnp.zeros((A_HALO, A_WIDTH), F32)
        bbuf[0:B_HALO, :] = jnp.zeros((B_HALO, B_WIDTH), F32)

    x = x_ref[...]
    mod = mod_ref[...]
    h = _norm_mod(x, g_ref[...], mod).astype(BF16)
    z = jnp.dot(h, win_ref[...], preferred_element_type=F32)

    a = z[:, 0:A_WIDTH] * _sigmoid(z[:, A_WIDTH:2 * A_WIDTH])
    abuf[A_HALO:A_HALO + tm, :] = a
    acc = jnp.zeros((tm, A_WIDTH), F32) + cab_ref[...]
    ext = tm + V7X_SUBLANES
    for r in range(V7X_SUBLANES):
        part = None
        for q in range((A_CONV - 1 - r) // V7X_SUBLANES + 1):
            k = A_CONV - 1 - (V7X_SUBLANES * q + r)
            term = caw_ref[k:k + 1, :] * abuf[pl.ds(A_HALO - V7X_SUBLANES * (q + 1), ext), :]
            part = term if part is None else part + term
        if r == 0:
            acc = acc + part[V7X_SUBLANES:, :]
        else:
            sbuf[r - 1] = part
            acc = acc + sbuf[r - 1, pl.ds(V7X_SUBLANES - r, tm), :]
    abuf[0:A_HALO, :] = abuf[tm:tm + A_HALO, :]
    seg = seg_ref[...]
    mu = _split_dot(acc, seg)
    d = acc - mu
    var = _split_dot(d * d, seg)
    yn = d * lax.rsqrt(var + EPS) * lng_ref[...] + lnb_ref[...]
    a_out = yn * _sigmoid(yn)

    off = 2 * A_WIDTH
    cv = z[:, off + B_WIDTH:off + 2 * B_WIDTH] * z[:, off + 2 * B_WIDTH:off + 3 * B_WIDTH]
    bbuf[B_HALO:B_HALO + tm, :] = cv
    bacc = jnp.zeros((tm, B_WIDTH), F32)
    for k in range(B_CONV):
        bacc = bacc + cbw_ref[k:k + 1, :] * bbuf[pl.ds(B_HALO - (B_CONV - 1) + k, tm), :]
    bbuf[0:B_HALO, :] = bbuf[tm:tm + B_HALO, :]
    bb = z[:, off:off + B_WIDTH] * bacc

    y = (jnp.dot(a_out.astype(BF16), wout_ref[0:A_WIDTH, :], preferred_element_type=F32)
         + jnp.dot(bb.astype(BF16), wout_ref[A_WIDTH:A_WIDTH + B_WIDTH, :],
                   preferred_element_type=F32))
    o_ref[...] = x + mod[2:3] * y


def _conv_mix_call(x, mod, norm_g, w_in, caw, cab, lng, lnb, cbw, w_out, seg, *, layer, tm=512):
    even_in = w_in.shape[1]
    return pl.pallas_call(
        functools.partial(_conv_mix_kernel, tm=tm),
        grid=(SEQ // tm,),
        in_specs=[
            pl.BlockSpec((tm, D_MODEL), lambda i: (i, 0)),
            _resident((None, None, 3, D_MODEL), lambda i: (layer, 1, 0, 0)),
            _resident((None, 1, D_MODEL), lambda i: (layer, 0, 0)),
            _full((D_MODEL, even_in)),
            _full((A_CONV, A_WIDTH)),
            _full((1, A_WIDTH)),
            _full((1, A_WIDTH)),
            _full((1, A_WIDTH)),
            _full((B_CONV, B_WIDTH)),
            _full((A_WIDTH + B_WIDTH, D_MODEL)),
            _full((A_WIDTH, A_WIDTH)),
        ],
        out_specs=pl.BlockSpec((tm, D_MODEL), lambda i: (i, 0)),
        out_shape=jax.ShapeDtypeStruct((SEQ, D_MODEL), F32),
        scratch_shapes=[pltpu.VMEM((A_HALO + tm, A_WIDTH), F32),
                        pltpu.VMEM((B_HALO + tm, B_WIDTH), F32),
                        pltpu.VMEM((V7X_SUBLANES - 1, tm + V7X_SUBLANES, A_WIDTH), F32)],
        compiler_params=_params(1),
        name="conv_mixers",
    )(x, mod, norm_g, w_in, caw, cab, lng, lnb, cbw, w_out, seg)


def _pm_proj_kernel(x_ref, mod_ref, g_ref, wu_ref, wq_ref, wkvt_ref, pw_ref,
                    pb_ref, ps_ref, pooled_ref, qrow_ref, kv_ref, km_ref,
                    ubuf, hmean, *, tm):
    i = pl.program_id(0)

    @pl.when(i == 0)
    def _():
        ubuf[0:POOL_HALO, :] = jnp.zeros((POOL_HALO, C_WIDTH), F32)

    x = x_ref[...]
    h = _norm_mod(x, g_ref[...], mod_ref[...]).astype(BF16)
    u = jnp.dot(h, wu_ref[...], preferred_element_type=F32)
    qr = jnp.dot(h, wq_ref[...], preferred_element_type=F32)
    kvt = lax.dot_general(wkvt_ref[...], h, _NT, preferred_element_type=F32)

    ubuf[POOL_HALO:POOL_HALO + tm, :] = u
    t1 = (i * tm + 1 + lax.broadcasted_iota(I32, (tm, C_GROUP_DIM), 0)).astype(F32)
    for gi, w in enumerate(POOL_WINDOWS):
        c0 = gi * C_GROUP_DIM
        s = u[:, c0:c0 + C_GROUP_DIM]
        for k in range(1, w):
            s = s + ubuf[pl.ds(POOL_HALO - k, tm), c0:c0 + C_GROUP_DIM]
        pooled = s / jnp.minimum(t1, float(w)) - u[:, c0:c0 + C_GROUP_DIM]
        mixed = (jnp.dot(pooled.astype(BF16), pw_ref[gi], preferred_element_type=F32)
                 + pb_ref[:, c0:c0 + C_GROUP_DIM])
        pooled_ref[:, c0:c0 + C_GROUP_DIM] = (
            mixed * ps_ref[:, c0:c0 + C_GROUP_DIM]).astype(BF16)
    ubuf[0:POOL_HALO, :] = ubuf[tm:tm + POOL_HALO, :]

    lane = lax.broadcasted_iota(I32, (tm, HEAD_PAD), 1)
    tail = jnp.where(lane == ALIBI_COL, 1.0, 0.0)
    qs = qr * (HEAD_DIM ** -0.5)
    for pair in range(D_HEADS // 2):
        both = qs[:, pair * HEAD_PAD:(pair + 1) * HEAD_PAD]
        qrow_ref[2 * pair] = jnp.where(lane < HEAD_DIM, both, tail)
        qrow_ref[2 * pair + 1] = jnp.where(
            lane < HEAD_DIM, pltpu.roll(both, HEAD_DIM, axis=1), tail)
    per_tile = tm // MOBA_BLOCK
    kvb = kvt.astype(BF16)
    n_alibi = SLAB_K_ROWS - HEAD_DIM
    first = lax.broadcasted_iota(I32, (n_alibi, MOBA_BLOCK), 0) == 0
    koff = lax.broadcasted_iota(I32, (n_alibi, MOBA_BLOCK), 1).astype(F32)
    for hh in range(D_HEADS):
        alibi = jnp.where(first, koff * (2.0 ** -(hh + 1)), 0.0).astype(BF16)
        for sb in range(per_tile):
            keys = slice(sb * MOBA_BLOCK, (sb + 1) * MOBA_BLOCK)
            kv_ref[hh, sb, 0:HEAD_DIM, :] = kvb[hh * HEAD_DIM:(hh + 1) * HEAD_DIM, keys]
            kv_ref[hh, sb, HEAD_DIM:SLAB_K_ROWS, :] = alibi
            kv_ref[hh, sb, SLAB_K_ROWS:SLAB_ROWS, :] = kvb[
                D_WIDTH + hh * HEAD_DIM:D_WIDTH + (hh + 1) * HEAD_DIM, keys]

    for sb in range(per_tile):
        blk = h[sb * MOBA_BLOCK:(sb + 1) * MOBA_BLOCK, :].astype(F32)
        hmean[pl.ds(i * per_tile + sb, 1), :] = jnp.mean(blk, axis=0, keepdims=True)

    @pl.when(i == pl.num_programs(0) - 1)
    def _():
        hi, lo = _split(hmean[...])
        wkt = wkvt_ref[0:D_WIDTH, :]
        km_ref[...] = (lax.dot_general(hi, wkt, _NT, preferred_element_type=F32)
                       + lax.dot_general(lo, wkt, _NT, preferred_element_type=F32))


def _pm_proj_call(x, mod, norm_g, wu, wq, wkvt, pw, pb, ps, *, layer, tm=512):
    return pl.pallas_call(
        functools.partial(_pm_proj_kernel, tm=tm),
        grid=(SEQ // tm,),
        in_specs=[
            pl.BlockSpec((tm, D_MODEL), lambda i: (i, 0)),
            _resident((None, None, 3, D_MODEL), lambda i: (layer, 1, 0, 0)),
            _resident((None, 1, D_MODEL), lambda i: (layer, 0, 0)),
            _full((D_MODEL, C_WIDTH)),
            _full((D_MODEL, D_WIDTH)),
            _full((2 * D_WIDTH, D_MODEL)),
            _full((len(POOL_WINDOWS), C_GROUP_DIM, C_GROUP_DIM)),
            _full((1, C_WIDTH)),
            _full((1, C_WIDTH)),
        ],
        out_specs=[
            pl.BlockSpec((tm, C_WIDTH), lambda i: (i, 0)),
            pl.BlockSpec((D_HEADS, tm, HEAD_PAD), lambda i: (0, i, 0)),
            pl.BlockSpec((D_HEADS, tm // MOBA_BLOCK, SLAB_ROWS, MOBA_BLOCK),
                         lambda i: (0, i, 0, 0)),
            pl.BlockSpec((N_BLOCKS, D_WIDTH), lambda i: (0, 0)),
        ],
        out_shape=[
            jax.ShapeDtypeStruct((SEQ, C_WIDTH), BF16),
            jax.ShapeDtypeStruct((D_HEADS, SEQ, HEAD_PAD), F32),
            jax.ShapeDtypeStruct((D_HEADS, N_BLOCKS, SLAB_ROWS, MOBA_BLOCK), BF16),
            jax.ShapeDtypeStruct((N_BLOCKS, D_WIDTH), F32),
        ],
        scratch_shapes=[pltpu.VMEM((POOL_HALO + tm, C_WIDTH), F32),
                        pltpu.VMEM((N_BLOCKS, D_MODEL), F32)],
        compiler_params=_params(1),
        name="pool_qkv_proj",
    )(x, mod, norm_g, wu, wq, wkvt, pw, pb, ps)


def _block_rows(qs, slabs, mask, lse_add):
    n = qs[0].shape[0]
    kpad = jnp.zeros((HEAD_PAD - SLAB_K_ROWS, MOBA_BLOCK), BF16)
    s = jnp.concatenate(
        [jnp.dot(q, jnp.concatenate([slab[0:SLAB_K_ROWS, :], kpad], axis=0),
                 preferred_element_type=F32)
         for q, slab in zip(qs, slabs)], axis=0)
    if mask is not None:
        s = jnp.where(mask, s, NEG_BIG)
    m = jnp.max(s, axis=1, keepdims=True)
    p = jnp.exp(s - m).astype(BF16)
    vpad = jnp.zeros((HEAD_PAD - HEAD_DIM, MOBA_BLOCK), BF16)
    ones = jnp.ones((HEAD_PAD, MOBA_BLOCK), BF16)
    ot = jnp.concatenate(
        [lax.dot_general(p[u * n:(u + 1) * n],
                         jnp.concatenate([slab[SLAB_K_ROWS:SLAB_ROWS, :], vpad, ones], axis=0),
                         _NT, preferred_element_type=F32)
         for u, slab in enumerate(slabs)], axis=0)
    o, tot = ot[:, 0:HEAD_PAD], ot[:, HEAD_PAD:2 * HEAD_PAD]
    lse = m + jnp.log(tot)
    if lse_add is not None:
        lse = jnp.concatenate(
            [lse[u * n:(u + 1) * n] + add for u, add in enumerate(lse_add)], axis=0)
    low = lax.broadcasted_iota(I32, o.shape, 1) < HEAD_DIM
    return jnp.where(low, o / tot, lse)


def _onehot_pairs(sel_rows):
    blk = lax.broadcasted_iota(I32, (N_BLOCKS, MOBA_BLOCK), 0)
    return jnp.concatenate(
        [jnp.where(blk == sel_rows[s], 1.0, 0.0) for s in range(MOBA_TOPK)], axis=1)


def _gate_own_kernel(qrow_ref, kv_ref, km_ref, tri_ref,
                     sel_ref, rank_ref, cnt_ref, own_ref, *, heads):
    own = pl.program_id(0)
    n_pair = MOBA_TOPK * MOBA_BLOCK
    qs = [qrow_ref[a].astype(BF16) for a in range(heads)]

    gates = []
    for a in range(heads):
        km_hi, km_lo = _split(km_ref[a])
        gates.append(lax.dot_general(km_hi, qs[a], _NT, preferred_element_type=F32)
                     + lax.dot_general(km_lo, qs[a], _NT, preferred_element_type=F32))
    blk = lax.broadcasted_iota(I32, (N_BLOCKS, heads * MOBA_BLOCK), 0)
    g = jnp.where(blk < own, jnp.concatenate(gates, axis=1), -jnp.inf)
    slots = []
    for s in range(MOBA_TOPK):
        best = jnp.max(g, axis=0, keepdims=True)
        idx = jnp.min(jnp.where(g == best, blk, N_BLOCKS), axis=0, keepdims=True)
        g = jnp.where(blk == idx, -jnp.inf, g)
        slots.append(jnp.where(s < own, idx, N_BLOCKS))

    onehots = []
    for a in range(heads):
        sel_rows = [slot[:, a * MOBA_BLOCK:(a + 1) * MOBA_BLOCK] for slot in slots]
        sel_ref[a] = jnp.concatenate(sel_rows, axis=0)
        onehots.append(_onehot_pairs(sel_rows))
    onehot = jnp.concatenate(onehots, axis=0)
    oh = onehot.astype(BF16)
    before = jnp.dot(oh, tri_ref[...], preferred_element_type=F32)
    hit = onehot * before
    ones = jnp.ones((8, n_pair), BF16)
    cnt = lax.dot_general(ones, oh, _NT, preferred_element_type=F32)
    for a in range(heads):
        rank = jnp.sum(hit[a * N_BLOCKS:(a + 1) * N_BLOCKS], axis=0, keepdims=True).astype(I32)
        rank_ref[a] = jnp.concatenate(
            [rank[:, s * MOBA_BLOCK:(s + 1) * MOBA_BLOCK] for s in range(MOBA_TOPK)], axis=0)
        cnt_ref[a] = cnt[0:1, a * N_BLOCKS:(a + 1) * N_BLOCKS].astype(I32)

    qi = lax.broadcasted_iota(I32, (heads * MOBA_BLOCK, MOBA_BLOCK), 0) % MOBA_BLOCK
    ki = lax.broadcasted_iota(I32, (heads * MOBA_BLOCK, MOBA_BLOCK), 1)
    rows = _block_rows(qs, [kv_ref.at[a] for a in range(heads)], ki <= qi, None)
    for a in range(heads):
        own_ref[a] = rows[a * MOBA_BLOCK:(a + 1) * MOBA_BLOCK]


def _gate_own_call(qrows, kv, kmean_h, tri, *, split):
    heads = SPLIT_HEADS
    return pl.pallas_call(
        functools.partial(_gate_own_kernel, heads=heads),
        grid=(N_BLOCKS,),
        in_specs=[
            pl.BlockSpec((heads, MOBA_BLOCK, HEAD_PAD), lambda b: (split, b, 0)),
            pl.BlockSpec((heads, None, SLAB_ROWS, MOBA_BLOCK), lambda b: (split, b, 0, 0)),
            pl.BlockSpec((heads, N_BLOCKS, HEAD_PAD), lambda b: (split, 0, 0)),
            _full((MOBA_TOPK * MOBA_BLOCK, MOBA_TOPK * MOBA_BLOCK)),
        ],
        out_specs=[
            pl.BlockSpec((heads, MOBA_TOPK, MOBA_BLOCK), lambda b: (0, 0, b)),
            pl.BlockSpec((heads, MOBA_TOPK, MOBA_BLOCK), lambda b: (0, 0, b)),
            pl.BlockSpec((heads, None, 1, N_BLOCKS), lambda b: (0, b, 0, 0)),
            pl.BlockSpec((heads, MOBA_BLOCK, HEAD_PAD), lambda b: (0, b, 0)),
        ],
        out_shape=[
            jax.ShapeDtypeStruct((heads, MOBA_TOPK, SEQ), I32),
            jax.ShapeDtypeStruct((heads, MOBA_TOPK, SEQ), I32),
            jax.ShapeDtypeStruct((heads, N_BLOCKS, 1, N_BLOCKS), I32),
            jax.ShapeDtypeStruct((heads, SEQ, HEAD_PAD), F32),
        ],
        compiler_params=_params(1),
        name=f"moba_gate_own_{split}",
    )(qrows, kv, kmean_h, tri)


def _route_pos_kernel(sel_ref, rank_ref, base_ref, pos_ref, *, blocks):
    lane = lax.broadcasted_iota(I32, (1, MOBA_BLOCK), 1)
    for bb in range(blocks):
        qs = slice(bb * MOBA_BLOCK, (bb + 1) * MOBA_BLOCK)
        rows = [[] for _ in range(MOBA_TOPK)]
        for a in range(SPLIT_HEADS):
            sel = sel_ref[a, :, qs]
            onehot = _onehot_pairs([sel[s:s + 1, :] for s in range(MOBA_TOPK)]).astype(BF16)
            dig = jnp.dot(base_ref[a, bb].astype(BF16), onehot,
                          preferred_element_type=F32)
            base = (dig[0:1] * 16384.0 + dig[1:2] * 128.0 + dig[2:3]).astype(I32)
            for s in range(MOBA_TOPK):
                p = base[:, s * MOBA_BLOCK:(s + 1) * MOBA_BLOCK] + rank_ref[a, s:s + 1, qs]
                rows[s].append(
                    jnp.where(sel[s:s + 1, :] < N_BLOCKS, p, TRASH_ROW0 + lane % SC_WINDOW))
        for s in range(MOBA_TOPK):
            pos_ref[s, :, qs] = jnp.concatenate(rows[s], axis=0)


def _route_pos_call(sel, rank, base_digits, *, blocks=8):
    width = blocks * MOBA_BLOCK
    return pl.pallas_call(
        functools.partial(_route_pos_kernel, blocks=blocks),
        grid=(N_BLOCKS // blocks,),
        in_specs=[
            pl.BlockSpec((SPLIT_HEADS, MOBA_TOPK, width), lambda b: (0, 0, b)),
            pl.BlockSpec((SPLIT_HEADS, MOBA_TOPK, width), lambda b: (0, 0, b)),
            pl.BlockSpec((SPLIT_HEADS, blocks, 8, N_BLOCKS), lambda b: (0, b, 0, 0)),
        ],
        out_specs=pl.BlockSpec((MOBA_TOPK, SPLIT_HEADS, width), lambda b: (0, 0, b)),
        out_shape=jax.ShapeDtypeStruct((MOBA_TOPK, SPLIT_HEADS, SEQ), I32),
        compiler_params=_params(1),
        name="moba_route_pos",
    )(sel, rank, base_digits)


def _route_tables(cnt, split):
    cnt = cnt.reshape(SPLIT_HEADS, N_BLOCKS, N_BLOCKS)
    tiles = (cnt.sum(axis=1) + ROUTE_TILE - 1) // ROUTE_TILE
    tiles_flat = tiles.reshape(N_GROUPS)
    tile0 = jnp.cumsum(tiles_flat) - tiles_flat
    n_tiles = tiles_flat.sum().astype(I32)
    within = jnp.cumsum(cnt, axis=1) - cnt
    base = tile0.reshape(SPLIT_HEADS, 1, N_BLOCKS) * ROUTE_TILE + within
    digits = jnp.stack([base // 16384, (base // 128) % 128, base % 128], axis=2).astype(F32)
    digits = jnp.pad(digits, ((0, 0), (0, 0), (0, 8 - 3), (0, 0)))
    t = jnp.arange(MAX_TILES, dtype=I32)
    ended = (tile0 + tiles_flat)[None, :] <= t[:, None]
    group_of_tile = jnp.minimum(ended.sum(axis=1).astype(I32), N_GROUPS - 1)
    tile_h = group_of_tile // N_BLOCKS
    head = split * SPLIT_HEADS + tile_h
    tile_step = lax.bitcast_convert_type((127 + 7 - head) << 23, F32)
    return digits, tile_h, group_of_tile % N_BLOCKS, n_tiles.reshape(1), tile_step


def _sc_mesh():
    return plsc.VectorSubcoreMesh(core_axis_name="core", subcore_axis_name="subcore")


def _dispatch_call(rows, pos_by_slot, *, split):
    n_rows = SPLIT_HEADS * SEQ
    first_window = split * n_rows // SC_WINDOW

    @functools.partial(
        pl.kernel, mesh=_sc_mesh(), scratch_types=[],
        out_type=jax.ShapeDtypeStruct((ROUTE_ROWS, HEAD_PAD), F32))
    def dispatch(x_hbm, i0_hbm, i1_hbm, i2_hbm, o_hbm):
        def body(x_vmem, i0_vmem, i1_vmem, i2_vmem):
            pltpu.sync_copy(x_vmem, o_hbm.at[i0_vmem.at[0]])
            pltpu.sync_copy(x_vmem, o_hbm.at[i1_vmem.at[0]])
            pltpu.sync_copy(x_vmem, o_hbm.at[i2_vmem.at[0]])

        idx_spec = pl.BlockSpec((1, SC_WINDOW), lambda i: (0, i))
        pltpu.emit_pipeline(
            body,
            grid=(n_rows // SC_WINDOW,),
            in_specs=[pl.BlockSpec((SC_WINDOW, HEAD_PAD), lambda i: (first_window + i, 0)),
                      idx_spec, idx_spec, idx_spec],
            out_specs=[],
            core_axis_name=("core", "subcore"),
            dimension_semantics=(pltpu.PARALLEL,),
        )(x_hbm, i0_hbm, i1_hbm, i2_hbm)

    return dispatch(rows, *pos_by_slot)


def _collect_call(table, idx):
    n_rows = idx.shape[1]

    @functools.partial(
        pl.kernel, mesh=_sc_mesh(), scratch_types=[],
        out_type=jax.ShapeDtypeStruct((n_rows, HEAD_PAD), F32))
    def collect(x_hbm, i_hbm, o_hbm):
        def body(i_vmem, o_vmem):
            pltpu.sync_copy(x_hbm.at[i_vmem.at[0]], o_vmem)

        pltpu.emit_pipeline(
            body,
            grid=(n_rows // SC_WINDOW,),
            in_specs=[pl.BlockSpec((1, SC_WINDOW), lambda i: (0, i))],
            out_specs=[pl.BlockSpec((SC_WINDOW, HEAD_PAD), lambda i: (i, 0))],
            core_axis_name=("core", "subcore"),
            dimension_semantics=(pltpu.PARALLEL,),
        )(i_hbm, o_hbm)

    return collect(table, idx)


def _routed_attn_kernel(th_ref, tj_ref, nt_ref, step_ref, q_ref, *refs, tiles):
    kv_refs, o_ref = refs[0:tiles], refs[tiles]
    t0 = pl.program_id(0) * tiles

    @pl.when(t0 < nt_ref[0])
    def _():
        qs = [q_ref[u * ROUTE_TILE:(u + 1) * ROUTE_TILE, :].astype(BF16) for u in range(tiles)]
        adds = [step_ref[t0 + u] * tj_ref[t0 + u].astype(F32) for u in range(tiles)]
        o_ref[...] = _block_rows(qs, kv_refs, None, adds)


def _routed_attn_call(tile_h, tile_j, n_tiles, tile_step, routed_q, kv,
                      *, split, tiles=ROUTE_TILES_PER_STEP):
    live = lambda s, th, tj, nt: (jnp.minimum(s, (nt[0] + tiles - 1) // tiles), 0)
    head0 = split * SPLIT_HEADS

    def kv_spec(u):
        return pl.BlockSpec(
            (None, None, SLAB_ROWS, MOBA_BLOCK),
            lambda s, th, tj, nt: (head0 + th[s * tiles + u], tj[s * tiles + u], 0, 0))

    grid_spec = pltpu.PrefetchScalarGridSpec(
        num_scalar_prefetch=3,
        grid=(MAX_TILES // tiles,),
        in_specs=([pl.BlockSpec(memory_space=pltpu.SMEM),
                   pl.BlockSpec((tiles * ROUTE_TILE, HEAD_PAD), live)]
                  + [kv_spec(u) for u in range(tiles)]),
        out_specs=pl.BlockSpec((tiles * ROUTE_TILE, HEAD_PAD), live),
    )
    return pl.pallas_call(
        functools.partial(_routed_attn_kernel, tiles=tiles),
        grid_spec=grid_spec,
        out_shape=jax.ShapeDtypeStruct((ROUTE_ROWS, HEAD_PAD), F32),
        compiler_params=_params(1),
        name=f"moba_routed_attn_{split}",
    )(tile_h, tile_j, n_tiles, tile_step, routed_q, *([kv] * tiles))


def _pm_out_kernel(slope_ref, x_ref, mod_ref, pooled_ref, *refs):
    own_refs, got_refs = refs[0:N_SPLITS], refs[N_SPLITS:2 * N_SPLITS]
    wp_ref, wa_ref, o_ref = refs[2 * N_SPLITS:]
    own = pl.program_id(0)
    low = lax.broadcasted_iota(I32, (MOBA_BLOCK, HEAD_PAD), 1) < HEAD_DIM

    def lse_of(part):
        return jnp.where(low, pltpu.roll(part, HEAD_DIM, axis=1), part)

    y = jnp.dot(pooled_ref[...], wp_ref[...], preferred_element_type=F32)
    for hh in range(D_HEADS):
        own_ref, got_ref = own_refs[hh // SPLIT_HEADS], got_refs[hh // SPLIT_HEADS]
        a = hh % SPLIT_HEADS
        own_shift = slope_ref[hh] * (own * MOBA_BLOCK).astype(F32)
        parts = [own_ref[a]]
        lses = [lse_of(parts[0])]
        for s in range(MOBA_TOPK):
            valid = s < own
            part = got_ref[s, a]
            parts.append(jnp.where(valid, part, 0.0))
            lses.append(jnp.where(valid, lse_of(part) - own_shift, NEG_BIG))
        top = functools.reduce(jnp.maximum, lses)
        num = jnp.zeros((MOBA_BLOCK, HEAD_PAD), F32)
        den = jnp.zeros((MOBA_BLOCK, HEAD_PAD), F32)
        for part, ls in zip(parts, lses):
            w = jnp.exp(ls - top)
            num = num + w * part
            den = den + w
        att = (num / den).astype(BF16)
        y = y + jnp.dot(att, wa_ref[hh], preferred_element_type=F32)
    o_ref[...] = x_ref[...] + mod_ref[2:3, :] * y


def _pm_out_call(slopes, x, mod, pooled, own_parts, got_parts, wp, wa, *, layer):
    tm = MOBA_BLOCK
    return pl.pallas_call(
        _pm_out_kernel,
        grid=(SEQ // tm,),
        in_specs=[
            pl.BlockSpec(memory_space=pltpu.SMEM),
            pl.BlockSpec((tm, D_MODEL), lambda i: (i, 0)),
            _resident((None, None, 3, D_MODEL), lambda i: (layer, 1, 0, 0)),
            pl.BlockSpec((tm, C_WIDTH), lambda i: (i, 0)),
        ] + [pl.BlockSpec((SPLIT_HEADS, tm, HEAD_PAD), lambda i: (0, i, 0))] * N_SPLITS
        + [pl.BlockSpec((MOBA_TOPK, SPLIT_HEADS, tm, HEAD_PAD), lambda i: (0, 0, i, 0))] * N_SPLITS
        + [
            _full((C_WIDTH, D_MODEL)),
            _full((D_HEADS, HEAD_PAD, D_MODEL)),
        ],
        out_specs=pl.BlockSpec((tm, D_MODEL), lambda i: (i, 0)),
        out_shape=jax.ShapeDtypeStruct((SEQ, D_MODEL), F32),
        compiler_params=_params(1),
        name="pool_moba_out",
    )(slopes, x, mod, pooled, *own_parts, *got_parts, wp, wa)


def _head_pad(w, axis):
    shape = list(w.shape)
    shape[axis:axis + 1] = [D_HEADS, HEAD_DIM]
    pad = [(0, 0)] * len(shape)
    pad[axis + 1] = (0, HEAD_PAD - HEAD_DIM)
    out = jnp.pad(w.reshape(shape), pad)
    shape[axis:axis + 2] = [D_HEADS * HEAD_PAD]
    return out.reshape(shape)


def kernel(x, c, ada_w, ada_b, ffn_norm, ffn_w_gate, ffn_w_up, ffn_w_down, mix_norm, conv_w_in,
           conv_a_w, conv_a_b, conv_a_ln_g, conv_a_ln_b, conv_b_w, conv_w_out, pm_w_in, pool_w,
           pool_b, pool_scale, pm_w_out, final_norm):
    assert x.shape == (1, SEQ, D_MODEL) and c.shape == (1, D_MODEL)
    xs = x.reshape(SEQ, D_MODEL)
    mod = _mod_call(c, ada_w, ada_b)

    ffn_g = ffn_norm.reshape(DEPTH, 2, 1, D_MODEL)
    mix_g = mix_norm.reshape(DEPTH, 1, D_MODEL)
    fin = final_norm.reshape(1, D_MODEL)
    ffn = functools.partial(_ffn_call, mod=mod, norm_g=ffn_g, w_gate=ffn_w_gate, w_up=ffn_w_up,
                            w_down=ffn_w_down, fin=fin)

    xs = ffn(xs, layer=0, sub=0, final=False)
    grp = np.arange(A_WIDTH) // (A_WIDTH // A_GROUPS)
    seg = jnp.asarray((grp[:, None] == grp[None, :]) / (A_WIDTH // A_GROUPS), dtype=BF16)
    xs = _conv_mix_call(
        xs, mod, mix_g, conv_w_in[0].astype(BF16), conv_a_w[0], conv_a_b[0].reshape(1, A_WIDTH),
        conv_a_ln_g[0].reshape(1, A_WIDTH), conv_a_ln_b[0].reshape(1, A_WIDTH), conv_b_w[0],
        conv_w_out[0].astype(BF16), seg, layer=0)
    xs = ffn(xs, layer=0, sub=1, final=False)

    xs = ffn(xs, layer=1, sub=0, final=False)
    w_in = pm_w_in[0].astype(BF16)
    w_u = w_in[:, 0:C_WIDTH]
    w_q = w_in[:, C_WIDTH:C_WIDTH + D_WIDTH]
    w_kvt = w_in[:, C_WIDTH + D_WIDTH:C_WIDTH + 3 * D_WIDTH].T
    pooled, qrows, kv, kmean = _pm_proj_call(
        xs, mod, mix_g, w_u, w_q, w_kvt, pool_w[0].astype(BF16),
        pool_b[0].reshape(1, C_WIDTH), pool_scale[0].reshape(1, C_WIDTH), layer=1)
    kmean_h = _head_pad(kmean, 1).reshape(N_BLOCKS, D_HEADS, HEAD_PAD).transpose(1, 0, 2)

    n_pair = MOBA_TOPK * MOBA_BLOCK
    tri = jnp.asarray(np.arange(n_pair)[:, None] < np.arange(n_pair)[None, :], dtype=BF16)
    q_flat = qrows.reshape(D_HEADS * SEQ, HEAD_PAD)
    own_parts, got_parts = [], []
    for split in range(N_SPLITS):
        sel, rank, cnt, own_part = _gate_own_call(qrows, kv, kmean_h, tri, split=split)
        base_digits, tile_h, tile_j, n_tiles, tile_step = _route_tables(cnt, split)
        pos = _route_pos_call(sel, rank, base_digits)
        pos_by_slot = [pos[s].reshape(1, SPLIT_HEADS * SEQ) for s in range(MOBA_TOPK)]
        routed_q = _dispatch_call(q_flat, pos_by_slot, split=split)
        routed_parts = _routed_attn_call(tile_h, tile_j, n_tiles, tile_step, routed_q, kv,
                                         split=split)
        got = _collect_call(routed_parts, pos.reshape(1, N_PAIRS))
        own_parts.append(own_part)
        got_parts.append(got.reshape(MOBA_TOPK, SPLIT_HEADS, SEQ, HEAD_PAD))

    slopes = jnp.asarray(2.0 ** (-8.0 * np.arange(1, D_HEADS + 1) / D_HEADS), dtype=F32)
    w_out = pm_w_out[0].astype(BF16)
    w_att = _head_pad(w_out[C_WIDTH:], 0).reshape(D_HEADS, HEAD_PAD, D_MODEL)
    xs = _pm_out_call(slopes, xs, mod, pooled, own_parts, got_parts, w_out[0:C_WIDTH], w_att,
                      layer=1)
    xs = ffn(xs, layer=1, sub=1, final=True)
    return xs.reshape(1, SEQ, D_MODEL)
```

```python
import functools

import numpy as np
import jax
import jax.numpy as jnp
from jax import lax
from jax.experimental import pallas as pl
from jax.experimental.pallas import tpu as pltpu
from jax.experimental.pallas import tpu_sc as plsc

D_MODEL = 1024
SEQ = 16384
DEPTH = 2
N_SUBLAYERS = 3
D_FF = 2816
EPS = 1e-6

A_WIDTH = 512
A_GROUPS = 8
A_CONV = 31
B_WIDTH = 512
B_CONV = 3

POOL_WINDOWS = (2, 4, 8, 16)
C_WIDTH = 512
C_GROUP_DIM = 128
D_HEADS = 8
HEAD_DIM = 64
D_WIDTH = 512
MOBA_BLOCK = 256
MOBA_TOPK = 3
N_BLOCKS = SEQ // MOBA_BLOCK

V7X_SUBLANES = 8
V7X_VMEM_LIMIT_BYTES = 56 * 1024 * 1024

A_HALO = 32
B_HALO = 8
POOL_HALO = 16
NEG_BIG = -1e30

HEAD_PAD = 128
ALIBI_COL = HEAD_DIM
SLAB_K_ROWS = HEAD_DIM + 16
SLAB_ROWS = SLAB_K_ROWS + HEAD_DIM

SPLIT_HEADS = 4
N_SPLITS = D_HEADS // SPLIT_HEADS
ROUTE_TILE = 256
ROUTE_TILES_PER_STEP = 16
N_PAIRS = SPLIT_HEADS * SEQ * MOBA_TOPK
N_GROUPS = SPLIT_HEADS * N_BLOCKS
MAX_TILES = N_PAIRS // ROUTE_TILE + N_GROUPS
TRASH_ROW0 = MAX_TILES * ROUTE_TILE
ROUTE_ROWS = (MAX_TILES + ROUTE_TILES_PER_STEP) * ROUTE_TILE
SC_WINDOW = 128

BF16 = jnp.bfloat16
F32 = jnp.float32
I32 = jnp.int32


def _params(n_axes):
    return pltpu.CompilerParams(
        dimension_semantics=("arbitrary",) * n_axes,
        vmem_limit_bytes=V7X_VMEM_LIMIT_BYTES)


def _resident(block_shape, index_map):
    return pl.BlockSpec(block_shape, index_map, pipeline_mode=pl.Buffered(1))


def _full(shape):
    return _resident(shape, lambda *_: (0,) * len(shape))


def _sigmoid(v):
    return 1.0 / (1.0 + jnp.exp(-v))


def _norm_mod(x, g, mod):
    r = lax.rsqrt(jnp.mean(x * x, axis=-1, keepdims=True) + EPS)
    return (x * r) * (g * (1.0 + mod[1:2])) + mod[0:1]


def _split(v):
    hi = v.astype(BF16)
    return hi, (v - hi.astype(F32)).astype(BF16)


def _split_dot(v, w):
    hi, lo = _split(v)
    return (jnp.dot(hi, w, preferred_element_type=F32)
            + jnp.dot(lo, w, preferred_element_type=F32))


_NT = (((1,), (1,)), ((), ()))


def _mod_kernel(c_ref, w_ref, b_ref, o_ref):
    c = c_ref[...]
    cond = c * _sigmoid(c)
    o_ref[0] = jnp.sum(w_ref[0] * cond, axis=0, keepdims=True) + b_ref[0]


def _mod_call(c, ada_w, ada_b):
    n_out = N_SUBLAYERS * 3 * D_MODEL
    tn = D_MODEL
    out = pl.pallas_call(
        _mod_kernel,
        grid=(DEPTH, n_out // tn),
        in_specs=[
            pl.BlockSpec((D_MODEL, 1), lambda l, j: (0, 0)),
            pl.BlockSpec((1, D_MODEL, tn), lambda l, j: (l, 0, j)),
            pl.BlockSpec((1, 1, tn), lambda l, j: (l, 0, j)),
        ],
        out_specs=pl.BlockSpec((1, 1, tn), lambda l, j: (l, 0, j)),
        out_shape=jax.ShapeDtypeStruct((DEPTH, 1, n_out), F32),
        compiler_params=_params(2),
        name="adaln_mod",
    )(c.reshape(D_MODEL, 1), ada_w, ada_b.reshape(DEPTH, 1, n_out))
    return out.reshape(DEPTH, N_SUBLAYERS, 3, D_MODEL)


def _ffn_kernel(x_ref, mod_ref, g_ref, wg_ref, wu_ref, wd_ref, fin_ref, o_ref, *, final):
    x = x_ref[...]
    mod = mod_ref[...]
    h = _norm_mod(x, g_ref[...], mod).astype(BF16)
    gate = jnp.dot(h, wg_ref[...], preferred_element_type=F32)
    up = jnp.dot(h, wu_ref[...], preferred_element_type=F32)
    act = (gate * _sigmoid(gate) * up).astype(BF16)
    y = jnp.dot(act, wd_ref[...], preferred_element_type=F32)
    xn = x + (0.5 * mod[2:3]) * y
    if final:
        r = lax.rsqrt(jnp.mean(xn * xn, axis=-1, keepdims=True) + EPS)
        xn = xn * r * fin_ref[...]
    o_ref[...] = xn


def _ffn_call(x, mod, norm_g, wg, wu, wd, fin, *, layer, sub, final, tm=512):
    mod_sub = 0 if sub == 0 else 2
    return pl.pallas_call(
        functools.partial(_ffn_kernel, final=final),
        grid=(SEQ // tm,),
        in_specs=[
            pl.BlockSpec((tm, D_MODEL), lambda i: (i, 0)),
            _resident((None, None, 3, D_MODEL), lambda i: (layer, mod_sub, 0, 0)),
            _resident((None, None, 1, D_MODEL), lambda i: (layer, sub, 0, 0)),
            _resident((None, None, D_MODEL, D_FF), lambda i: (layer, sub, 0, 0)),
            _resident((None, None, D_MODEL, D_FF), lambda i: (layer, sub, 0, 0)),
            _resident((None, None, D_FF, D_MODEL), lambda i: (layer, sub, 0, 0)),
            _resident((1, D_MODEL), lambda i: (0, 0)),
        ],
        out_specs=pl.BlockSpec((tm, D_MODEL), lambda i: (i, 0)),
        out_shape=jax.ShapeDtypeStruct((SEQ, D_MODEL), F32),
        compiler_params=_params(1),
        name=f"ffn_l{layer}s{sub}",
    )(x, mod, norm_g, wg, wu, wd, fin)


def _conv_mix_kernel(x_ref, mod_ref, g_ref, win_ref, caw_ref, cab_ref, lng_ref, lnb_ref,
                     cbw_ref, wout_ref, seg_ref, o_ref, abuf, bbuf, sbuf, *, tm):
    i = pl.program_id(0)

    @pl.when(i == 0)
    def _():
        abuf[0:A_HALO, :] = jnp.zeros((A_HALO, A_WIDTH), F32)
        bbuf[0:B_HALO, :] = jnp.zeros((B_HALO, B_WIDTH), F32)

    x = x_ref[...]
    mod = mod_ref[...]
    h = _norm_mod(x, g_ref[...], mod).astype(BF16)
    z = jnp.dot(h, win_ref[...], preferred_element_type=F32)

    a = z[:, 0:A_WIDTH] * _sigmoid(z[:, A_WIDTH:2 * A_WIDTH])
    abuf[A_HALO:A_HALO + tm, :] = a
    acc = jnp.zeros((tm, A_WIDTH), F32) + cab_ref[...]
    ext = tm + V7X_SUBLANES
    for r in range(V7X_SUBLANES):
        part = None
        for q in range((A_CONV - 1 - r) // V7X_SUBLANES + 1):
            k = A_CONV - 1 - (V7X_SUBLANES * q + r)
            term = caw_ref[k:k + 1, :] * abuf[pl.ds(A_HALO - V7X_SUBLANES * (q + 1), ext), :]
            part = term if part is None else part + term
        if r == 0:
            acc = acc + part[V7X_SUBLANES:, :]
        else:
            sbuf[r - 1] = part
            acc = acc + sbuf[r - 1, pl.ds(V7X_SUBLANES - r, tm), :]
    abuf[0:A_HALO, :] = abuf[tm:tm + A_HALO, :]
    seg = seg_ref[...]
    mu = _split_dot(acc, seg)
    d = acc - mu
    var = _split_dot(d * d, seg)
    yn = d * lax.rsqrt(var + EPS) * lng_ref[...] + lnb_ref[...]
    a_out = yn * _sigmoid(yn)

    off = 2 * A_WIDTH
    cv = z[:, off + B_WIDTH:off + 2 * B_WIDTH] * z[:, off + 2 * B_WIDTH:off + 3 * B_WIDTH]
    bbuf[B_HALO:B_HALO + tm, :] = cv
    bacc = jnp.zeros((tm, B_WIDTH), F32)
    for k in range(B_CONV):
        bacc = bacc + cbw_ref[k:k + 1, :] * bbuf[pl.ds(B_HALO - (B_CONV - 1) + k, tm), :]
    bbuf[0:B_HALO, :] = bbuf[tm:tm + B_HALO, :]
    bb = z[:, off:off + B_WIDTH] * bacc

    y = (jnp.dot(a_out.astype(BF16), wout_ref[0:A_WIDTH, :], preferred_element_type=F32)
         + jnp.dot(bb.astype(BF16), wout_ref[A_WIDTH:A_WIDTH + B_WIDTH, :],
                   preferred_element_type=F32))
    o_ref[...] = x + mod[2:3] * y


def _conv_mix_call(x, mod, norm_g, w_in, caw, cab, lng, lnb, cbw, w_out, seg, *, layer, tm=512):
    even_in = w_in.shape[1]
    return pl.pallas_call(
        functools.partial(_conv_mix_kernel, tm=tm),
        grid=(SEQ // tm,),
        in_specs=[
            pl.BlockSpec((tm, D_MODEL), lambda i: (i, 0)),
            _resident((None, None, 3, D_MODEL), lambda i: (layer, 1, 0, 0)),
            _resident((None, 1, D_MODEL), lambda i: (layer, 0, 0)),
            _full((D_MODEL, even_in)),
            _full((A_CONV, A_WIDTH)),
            _full((1, A_WIDTH)),
            _full((1, A_WIDTH)),
            _full((1, A_WIDTH)),
            _full((B_CONV, B_WIDTH)),
            _full((A_WIDTH + B_WIDTH, D_MODEL)),
            _full((A_WIDTH, A_WIDTH)),
        ],
        out_specs=pl.BlockSpec((tm, D_MODEL), lambda i: (i, 0)),
        out_shape=jax.ShapeDtypeStruct((SEQ, D_MODEL), F32),
        scratch_shapes=[pltpu.VMEM((A_HALO + tm, A_WIDTH), F32),
                        pltpu.VMEM((B_HALO + tm, B_WIDTH), F32),
                        pltpu.VMEM((V7X_SUBLANES - 1, tm + V7X_SUBLANES, A_WIDTH), F32)],
        compiler_params=_params(1),
        name="conv_mixers",
    )(x, mod, norm_g, w_in, caw, cab, lng, lnb, cbw, w_out, seg)


def _pm_proj_kernel(x_ref, mod_ref, g_ref, wu_ref, wq_ref, wkvt_ref, pw_ref,
                    pb_ref, ps_ref, pooled_ref, qrow_ref, kv_ref, km_ref,
                    ubuf, hmean, *, tm):
    i = pl.program_id(0)

    @pl.when(i == 0)
    def _():
        ubuf[0:POOL_HALO, :] = jnp.zeros((POOL_HALO, C_WIDTH), F32)

    x = x_ref[...]
    h = _norm_mod(x, g_ref[...], mod_ref[...]).astype(BF16)
    u = jnp.dot(h, wu_ref[...], preferred_element_type=F32)
    qr = jnp.dot(h, wq_ref[...], preferred_element_type=F32)
    kvt = lax.dot_general(wkvt_ref[...], h, _NT, preferred_element_type=F32)

    ubuf[POOL_HALO:POOL_HALO + tm, :] = u
    t1 = (i * tm + 1 + lax.broadcasted_iota(I32, (tm, C_GROUP_DIM), 0)).astype(F32)
    for gi, w in enumerate(POOL_WINDOWS):
        c0 = gi * C_GROUP_DIM
        s = u[:, c0:c0 + C_GROUP_DIM]
        for k in range(1, w):
            s = s + ubuf[pl.ds(POOL_HALO - k, tm), c0:c0 + C_GROUP_DIM]
        pooled = s / jnp.minimum(t1, float(w)) - u[:, c0:c0 + C_GROUP_DIM]
        mixed = (jnp.dot(pooled.astype(BF16), pw_ref[gi], preferred_element_type=F32)
                 + pb_ref[:, c0:c0 + C_GROUP_DIM])
        pooled_ref[:, c0:c0 + C_GROUP_DIM] = (
            mixed * ps_ref[:, c0:c0 + C_GROUP_DIM]).astype(BF16)
    ubuf[0:POOL_HALO, :] = ubuf[tm:tm + POOL_HALO, :]

    lane = lax.broadcasted_iota(I32, (tm, HEAD_PAD), 1)
    tail = jnp.where(lane == ALIBI_COL, 1.0, 0.0)
    qs = qr * (HEAD_DIM ** -0.5)
    for pair in range(D_HEADS // 2):
        both = qs[:, pair * HEAD_PAD:(pair + 1) * HEAD_PAD]
        qrow_ref[2 * pair] = jnp.where(lane < HEAD_DIM, both, tail)
        qrow_ref[2 * pair + 1] = jnp.where(
            lane < HEAD_DIM, pltpu.roll(both, HEAD_DIM, axis=1), tail)
    per_tile = tm // MOBA_BLOCK
    kvb = kvt.astype(BF16)
    n_alibi = SLAB_K_ROWS - HEAD_DIM
    first = lax.broadcasted_iota(I32, (n_alibi, MOBA_BLOCK), 0) == 0
    koff = lax.broadcasted_iota(I32, (n_alibi, MOBA_BLOCK), 1).astype(F32)
    for hh in range(D_HEADS):
        alibi = jnp.where(first, koff * (2.0 ** -(hh + 1)), 0.0).astype(BF16)
        for sb in range(per_tile):
            keys = slice(sb * MOBA_BLOCK, (sb + 1) * MOBA_BLOCK)
            kv_ref[hh, sb, 0:HEAD_DIM, :] = kvb[hh * HEAD_DIM:(hh + 1) * HEAD_DIM, keys]
            kv_ref[hh, sb, HEAD_DIM:SLAB_K_ROWS, :] = alibi
            kv_ref[hh, sb, SLAB_K_ROWS:SLAB_ROWS, :] = kvb[
                D_WIDTH + hh * HEAD_DIM:D_WIDTH + (hh + 1) * HEAD_DIM, keys]

    for sb in range(per_tile):
        blk = h[sb * MOBA_BLOCK:(sb + 1) * MOBA_BLOCK, :].astype(F32)
        hmean[pl.ds(i * per_tile + sb, 1), :] = jnp.mean(blk, axis=0, keepdims=True)

    @pl.when(i == pl.num_programs(0) - 1)
    def _():
        hi, lo = _split(hmean[...])
        wkt = wkvt_ref[0:D_WIDTH, :]
        km_ref[...] = (lax.dot_general(hi, wkt, _NT, preferred_element_type=F32)
                       + lax.dot_general(lo, wkt, _NT, preferred_element_type=F32))


def _pm_proj_call(x, mod, norm_g, wu, wq, wkvt, pw, pb, ps, *, layer, tm=512):
    return pl.pallas_call(
        functools.partial(_pm_proj_kernel, tm=tm),
        grid=(SEQ // tm,),
        in_specs=[
            pl.BlockSpec((tm, D_MODEL), lambda i: (i, 0)),
            _resident((None, None, 3, D_MODEL), lambda i: (layer, 1, 0, 0)),
            _resident((None, 1, D_MODEL), lambda i: (layer, 0, 0)),
            _full((D_MODEL, C_WIDTH)),
            _full((D_MODEL, D_WIDTH)),
            _full((2 * D_WIDTH, D_MODEL)),
            _full((len(POOL_WINDOWS), C_GROUP_DIM, C_GROUP_DIM)),
            _full((1, C_WIDTH)),
            _full((1, C_WIDTH)),
        ],
        out_specs=[
            pl.BlockSpec((tm, C_WIDTH), lambda i: (i, 0)),
            pl.BlockSpec((D_HEADS, tm, HEAD_PAD), lambda i: (0, i, 0)),
            pl.BlockSpec((D_HEADS, tm // MOBA_BLOCK, SLAB_ROWS, MOBA_BLOCK),
                         lambda i: (0, i, 0, 0)),
            pl.BlockSpec((N_BLOCKS, D_WIDTH), lambda i: (0, 0)),
        ],
        out_shape=[
            jax.ShapeDtypeStruct((SEQ, C_WIDTH), BF16),
            jax.ShapeDtypeStruct((D_HEADS, SEQ, HEAD_PAD), F32),
            jax.ShapeDtypeStruct((D_HEADS, N_BLOCKS, SLAB_ROWS, MOBA_BLOCK), BF16),
            jax.ShapeDtypeStruct((N_BLOCKS, D_WIDTH), F32),
        ],
        scratch_shapes=[pltpu.VMEM((POOL_HALO + tm, C_WIDTH), F32),
                        pltpu.VMEM((N_BLOCKS, D_MODEL), F32)],
        compiler_params=_params(1),
        name="pool_qkv_proj",
    )(x, mod, norm_g, wu, wq, wkvt, pw, pb, ps)


def _block_rows(qs, slabs, mask, lse_add):
    n = qs[0].shape[0]
    kpad = jnp.zeros((HEAD_PAD - SLAB_K_ROWS, MOBA_BLOCK), BF16)
    s = jnp.concatenate(
        [jnp.dot(q, jnp.concatenate([slab[0:SLAB_K_ROWS, :], kpad], axis=0),
                 preferred_element_type=F32)
         for q, slab in zip(qs, slabs)], axis=0)
    if mask is not None:
        s = jnp.where(mask, s, NEG_BIG)
    m = jnp.max(s, axis=1, keepdims=True)
    p = jnp.exp(s - m).astype(BF16)
    vpad = jnp.zeros((HEAD_PAD - HEAD_DIM, MOBA_BLOCK), BF16)
    ones = jnp.ones((HEAD_PAD, MOBA_BLOCK), BF16)
    ot = jnp.concatenate(
        [lax.dot_general(p[u * n:(u + 1) * n],
                         jnp.concatenate([slab[SLAB_K_ROWS:SLAB_ROWS, :], vpad, ones], axis=0),
                         _NT, preferred_element_type=F32)
         for u, slab in enumerate(slabs)], axis=0)
    o, tot = ot[:, 0:HEAD_PAD], ot[:, HEAD_PAD:2 * HEAD_PAD]
    lse = m + jnp.log(tot)
    if lse_add is not None:
        lse = jnp.concatenate(
            [lse[u * n:(u + 1) * n] + add for u, add in enumerate(lse_add)], axis=0)
    low = lax.broadcasted_iota(I32, o.shape, 1) < HEAD_DIM
    return jnp.where(low, o / tot, lse)


def _onehot_pairs(sel_rows):
    blk = lax.broadcasted_iota(I32, (N_BLOCKS, MOBA_BLOCK), 0)
    return jnp.concatenate(
        [jnp.where(blk == sel_rows[s], 1.0, 0.0) for s in range(MOBA_TOPK)], axis=1)


def _gate_own_kernel(qrow_ref, kv_ref, km_ref, tri_ref,
                     sel_ref, rank_ref, cnt_ref, own_ref, *, heads):
    own = pl.program_id(0)
    n_pair = MOBA_TOPK * MOBA_BLOCK
    qs = [qrow_ref[a].astype(BF16) for a in range(heads)]

    gates = []
    for a in range(heads):
        km_hi, km_lo = _split(km_ref[a])
        gates.append(lax.dot_general(km_hi, qs[a], _NT, preferred_element_type=F32)
                     + lax.dot_general(km_lo, qs[a], _NT, preferred_element_type=F32))
    blk = lax.broadcasted_iota(I32, (N_BLOCKS, heads * MOBA_BLOCK), 0)
    g = jnp.where(blk < own, jnp.concatenate(gates, axis=1), -jnp.inf)
    slots = []
    for s in range(MOBA_TOPK):
        best = jnp.max(g, axis=0, keepdims=True)
        idx = jnp.min(jnp.where(g == best, blk, N_BLOCKS), axis=0, keepdims=True)
        g = jnp.where(blk == idx, -jnp.inf, g)
        slots.append(jnp.where(s < own, idx, N_BLOCKS))

    onehots = []
    for a in range(heads):
        sel_rows = [slot[:, a * MOBA_BLOCK:(a + 1) * MOBA_BLOCK] for slot in slots]
        sel_ref[a] = jnp.concatenate(sel_rows, axis=0)
        onehots.append(_onehot_pairs(sel_rows))
    onehot = jnp.concatenate(onehots, axis=0)
    oh = onehot.astype(BF16)
    before = jnp.dot(oh, tri_ref[...], preferred_element_type=F32)
    hit = onehot * before
    ones = jnp.ones((8, n_pair), BF16)
    cnt = lax.dot_general(ones, oh, _NT, preferred_element_type=F32)
    for a in range(heads):
        rank = jnp.sum(hit[a * N_BLOCKS:(a + 1) * N_BLOCKS], axis=0, keepdims=True).astype(I32)
        rank_ref[a] = jnp.concatenate(
            [rank[:, s * MOBA_BLOCK:(s + 1) * MOBA_BLOCK] for s in range(MOBA_TOPK)], axis=0)
        cnt_ref[a] = cnt[0:1, a * N_BLOCKS:(a + 1) * N_BLOCKS].astype(I32)

    qi = lax.broadcasted_iota(I32, (heads * MOBA_BLOCK, MOBA_BLOCK), 0) % MOBA_BLOCK
    ki = lax.broadcasted_iota(I32, (heads * MOBA_BLOCK, MOBA_BLOCK), 1)
    rows = _block_rows(qs, [kv_ref.at[a] for a in range(heads)], ki <= qi, None)
    for a in range(heads):
        own_ref[a] = rows[a * MOBA_BLOCK:(a + 1) * MOBA_BLOCK]


def _gate_own_call(qrows, kv, kmean_h, tri, *, split):
    heads = SPLIT_HEADS
    return pl.pallas_call(
        functools.partial(_gate_own_kernel, heads=heads),
        grid=(N_BLOCKS,),
        in_specs=[
            pl.BlockSpec((heads, MOBA_BLOCK, HEAD_PAD), lambda b: (split, b, 0)),
            pl.BlockSpec((heads, None, SLAB_ROWS, MOBA_BLOCK), lambda b: (split, b, 0, 0)),
            pl.BlockSpec((heads, N_BLOCKS, HEAD_PAD), lambda b: (split, 0, 0)),
            _full((MOBA_TOPK * MOBA_BLOCK, MOBA_TOPK * MOBA_BLOCK)),
        ],
        out_specs=[
            pl.BlockSpec((heads, MOBA_TOPK, MOBA_BLOCK), lambda b: (0, 0, b)),
            pl.BlockSpec((heads, MOBA_TOPK, MOBA_BLOCK), lambda b: (0, 0, b)),
            pl.BlockSpec((heads, None, 1, N_BLOCKS), lambda b: (0, b, 0, 0)),
            pl.BlockSpec((heads, MOBA_BLOCK, HEAD_PAD), lambda b: (0, b, 0)),
        ],
        out_shape=[
            jax.ShapeDtypeStruct((heads, MOBA_TOPK, SEQ), I32),
            jax.ShapeDtypeStruct((heads, MOBA_TOPK, SEQ), I32),
            jax.ShapeDtypeStruct((heads, N_BLOCKS, 1, N_BLOCKS), I32),
            jax.ShapeDtypeStruct((heads, SEQ, HEAD_PAD), F32),
        ],
        compiler_params=_params(1),
        name=f"moba_gate_own_{split}",
    )(qrows, kv, kmean_h, tri)


def _route_pos_kernel(sel_ref, rank_ref, base_ref, pos_ref, *, blocks):
    lane = lax.broadcasted_iota(I32, (1, MOBA_BLOCK), 1)
    for bb in range(blocks):
        qs = slice(bb * MOBA_BLOCK, (bb + 1) * MOBA_BLOCK)
        rows = [[] for _ in range(MOBA_TOPK)]
        for a in range(SPLIT_HEADS):
            sel = sel_ref[a, :, qs]
            onehot = _onehot_pairs([sel[s:s + 1, :] for s in range(MOBA_TOPK)]).astype(BF16)
            dig = jnp.dot(base_ref[a, bb].astype(BF16), onehot,
                          preferred_element_type=F32)
            base = (dig[0:1] * 16384.0 + dig[1:2] * 128.0 + dig[2:3]).astype(I32)
            for s in range(MOBA_TOPK):
                p = base[:, s * MOBA_BLOCK:(s + 1) * MOBA_BLOCK] + rank_ref[a, s:s + 1, qs]
                rows[s].append(
                    jnp.where(sel[s:s + 1, :] < N_BLOCKS, p, TRASH_ROW0 + lane % SC_WINDOW))
        for s in range(MOBA_TOPK):
            pos_ref[s, :, qs] = jnp.concatenate(rows[s], axis=0)


def _route_pos_call(sel, rank, base_digits, *, blocks=8):
    width = blocks * MOBA_BLOCK
    return pl.pallas_call(
        functools.partial(_route_pos_kernel, blocks=blocks),
        grid=(N_BLOCKS // blocks,),
        in_specs=[
            pl.BlockSpec((SPLIT_HEADS, MOBA_TOPK, width), lambda b: (0, 0, b)),
            pl.BlockSpec((SPLIT_HEADS, MOBA_TOPK, width), lambda b: (0, 0, b)),
            pl.BlockSpec((SPLIT_HEADS, blocks, 8, N_BLOCKS), lambda b: (0, b, 0, 0)),
        ],
        out_specs=pl.BlockSpec((MOBA_TOPK, SPLIT_HEADS, width), lambda b: (0, 0, b)),
        out_shape=jax.ShapeDtypeStruct((MOBA_TOPK, SPLIT_HEADS, SEQ), I32),
        compiler_params=_params(1),
        name="moba_route_pos",
    )(sel, rank, base_digits)


def _route_tables(cnt, split):
    cnt = cnt.reshape(SPLIT_HEADS, N_BLOCKS, N_BLOCKS)
    tiles = (cnt.sum(axis=1) + ROUTE_TILE - 1) // ROUTE_TILE
    tiles_flat = tiles.reshape(N_GROUPS)
    tile0 = jnp.cumsum(tiles_flat) - tiles_flat
    n_tiles = tiles_flat.sum().astype(I32)
    within = jnp.cumsum(cnt, axis=1) - cnt
    base = tile0.reshape(SPLIT_HEADS, 1, N_BLOCKS) * ROUTE_TILE + within
    digits = jnp.stack([base // 16384, (base // 128) % 128, base % 128], axis=2).astype(F32)
    digits = jnp.pad(digits, ((0, 0), (0, 0), (0, 8 - 3), (0, 0)))
    t = jnp.arange(MAX_TILES, dtype=I32)
    ended = (tile0 + tiles_flat)[None, :] <= t[:, None]
    group_of_tile = jnp.minimum(ended.sum(axis=1).astype(I32), N_GROUPS - 1)
    tile_h = group_of_tile // N_BLOCKS
    head = split * SPLIT_HEADS + tile_h
    tile_step = lax.bitcast_convert_type((127 + 7 - head) << 23, F32)
    return digits, tile_h, group_of_tile % N_BLOCKS, n_tiles.reshape(1), tile_step


def _sc_mesh():
    return plsc.VectorSubcoreMesh(core_axis_name="core", subcore_axis_name="subcore")


def _dispatch_call(rows, pos_by_slot, *, split):
    n_rows = SPLIT_HEADS * SEQ
    first_window = split * n_rows // SC_WINDOW

    @functools.partial(
        pl.kernel, mesh=_sc_mesh(), scratch_types=[],
        out_type=jax.ShapeDtypeStruct((ROUTE_ROWS, HEAD_PAD), F32))
    def dispatch(x_hbm, i0_hbm, i1_hbm, i2_hbm, o_hbm):
        def body(x_vmem, i0_vmem, i1_vmem, i2_vmem):
            pltpu.sync_copy(x_vmem, o_hbm.at[i0_vmem.at[0]])
            pltpu.sync_copy(x_vmem, o_hbm.at[i1_vmem.at[0]])
            pltpu.sync_copy(x_vmem, o_hbm.at[i2_vmem.at[0]])

        idx_spec = pl.BlockSpec((1, SC_WINDOW), lambda i: (0, i))
        pltpu.emit_pipeline(
            body,
            grid=(n_rows // SC_WINDOW,),
            in_specs=[pl.BlockSpec((SC_WINDOW, HEAD_PAD), lambda i: (first_window + i, 0)),
                      idx_spec, idx_spec, idx_spec],
            out_specs=[],
            core_axis_name=("core", "subcore"),
            dimension_semantics=(pltpu.PARALLEL,),
        )(x_hbm, i0_hbm, i1_hbm, i2_hbm)

    return dispatch(rows, *pos_by_slot)


def _collect_call(table, idx):
    n_rows = idx.shape[1]

    @functools.partial(
        pl.kernel, mesh=_sc_mesh(), scratch_types=[],
        out_type=jax.ShapeDtypeStruct((n_rows, HEAD_PAD), F32))
    def collect(x_hbm, i_hbm, o_hbm):
        def body(i_vmem, o_vmem):
            pltpu.sync_copy(x_hbm.at[i_vmem.at[0]], o_vmem)

        pltpu.emit_pipeline(
            body,
            grid=(n_rows // SC_WINDOW,),
            in_specs=[pl.BlockSpec((1, SC_WINDOW), lambda i: (0, i))],
            out_specs=[pl.BlockSpec((SC_WINDOW, HEAD_PAD), lambda i: (i, 0))],
            core_axis_name=("core", "subcore"),
            dimension_semantics=(pltpu.PARALLEL,),
        )(i_hbm, o_hbm)

    return collect(table, idx)


def _routed_attn_kernel(th_ref, tj_ref, nt_ref, step_ref, q_ref, *refs, tiles):
    kv_refs, o_ref = refs[0:tiles], refs[tiles]
    t0 = pl.program_id(0) * tiles

    @pl.when(t0 < nt_ref[0])
    def _():
        qs = [q_ref[u * ROUTE_TILE:(u + 1) * ROUTE_TILE, :].astype(BF16) for u in range(tiles)]
        adds = [step_ref[t0 + u] * tj_ref[t0 + u].astype(F32) for u in range(tiles)]
        o_ref[...] = _block_rows(qs, kv_refs, None, adds)


def _routed_attn_call(tile_h, tile_j, n_tiles, tile_step, routed_q, kv,
                      *, split, tiles=ROUTE_TILES_PER_STEP):
    live = lambda s, th, tj, nt: (jnp.minimum(s, (nt[0] + tiles - 1) // tiles), 0)
    head0 = split * SPLIT_HEADS

    def kv_spec(u):
        return pl.BlockSpec(
            (None, None, SLAB_ROWS, MOBA_BLOCK),
            lambda s, th, tj, nt: (head0 + th[s * tiles + u], tj[s * tiles + u], 0, 0))

    grid_spec = pltpu.PrefetchScalarGridSpec(
        num_scalar_prefetch=3,
        grid=(MAX_TILES // tiles,),
        in_specs=([pl.BlockSpec(memory_space=pltpu.SMEM),
                   pl.BlockSpec((tiles * ROUTE_TILE, HEAD_PAD), live)]
                  + [kv_spec(u) for u in range(tiles)]),
        out_specs=pl.BlockSpec((tiles * ROUTE_TILE, HEAD_PAD), live),
    )
    return pl.pallas_call(
        functools.partial(_routed_attn_kernel, tiles=tiles),
        grid_spec=grid_spec,
        out_shape=jax.ShapeDtypeStruct((ROUTE_ROWS, HEAD_PAD), F32),
        compiler_params=_params(1),
        name=f"moba_routed_attn_{split}",
    )(tile_h, tile_j, n_tiles, tile_step, routed_q, *([kv] * tiles))


def _pm_out_kernel(slope_ref, x_ref, mod_ref, pooled_ref, *refs):
    own_refs, got_refs = refs[0:N_SPLITS], refs[N_SPLITS:2 * N_SPLITS]
    wp_ref, wa_ref, o_ref = refs[2 * N_SPLITS:]
    own = pl.program_id(0)
    low = lax.broadcasted_iota(I32, (MOBA_BLOCK, HEAD_PAD), 1) < HEAD_DIM

    def lse_of(part):
        return jnp.where(low, pltpu.roll(part, HEAD_DIM, axis=1), part)

    y = jnp.dot(pooled_ref[...], wp_ref[...], preferred_element_type=F32)
    for hh in range(D_HEADS):
        own_ref, got_ref = own_refs[hh // SPLIT_HEADS], got_refs[hh // SPLIT_HEADS]
        a = hh % SPLIT_HEADS
        own_shift = slope_ref[hh] * (own * MOBA_BLOCK).astype(F32)
        parts = [own_ref[a]]
        lses = [lse_of(parts[0])]
        for s in range(MOBA_TOPK):
            valid = s < own
            part = got_ref[s, a]
            parts.append(jnp.where(valid, part, 0.0))
            lses.append(jnp.where(valid, lse_of(part) - own_shift, NEG_BIG))
        top = functools.reduce(jnp.maximum, lses)
        num = jnp.zeros((MOBA_BLOCK, HEAD_PAD), F32)
        den = jnp.zeros((MOBA_BLOCK, HEAD_PAD), F32)
        for part, ls in zip(parts, lses):
            w = jnp.exp(ls - top)
            num = num + w * part
            den = den + w
        att = (num / den).astype(BF16)
        y = y + jnp.dot(att, wa_ref[hh], preferred_element_type=F32)
    o_ref[...] = x_ref[...] + mod_ref[2:3, :] * y


def _pm_out_call(slopes, x, mod, pooled, own_parts, got_parts, wp, wa, *, layer):
    tm = MOBA_BLOCK
    return pl.pallas_call(
        _pm_out_kernel,
        grid=(SEQ // tm,),
        in_specs=[
            pl.BlockSpec(memory_space=pltpu.SMEM),
            pl.BlockSpec((tm, D_MODEL), lambda i: (i, 0)),
            _resident((None, None, 3, D_MODEL), lambda i: (layer, 1, 0, 0)),
            pl.BlockSpec((tm, C_WIDTH), lambda i: (i, 0)),
        ] + [pl.BlockSpec((SPLIT_HEADS, tm, HEAD_PAD), lambda i: (0, i, 0))] * N_SPLITS
        + [pl.BlockSpec((MOBA_TOPK, SPLIT_HEADS, tm, HEAD_PAD), lambda i: (0, 0, i, 0))] * N_SPLITS
        + [
            _full((C_WIDTH, D_MODEL)),
            _full((D_HEADS, HEAD_PAD, D_MODEL)),
        ],
        out_specs=pl.BlockSpec((tm, D_MODEL), lambda i: (i, 0)),
        out_shape=jax.ShapeDtypeStruct((SEQ, D_MODEL), F32),
        compiler_params=_params(1),
        name="pool_moba_out",
    )(slopes, x, mod, pooled, *own_parts, *got_parts, wp, wa)


def _head_pad(w, axis):
    shape = list(w.shape)
    shape[axis:axis + 1] = [D_HEADS, HEAD_DIM]
    pad = [(0, 0)] * len(shape)
    pad[axis + 1] = (0, HEAD_PAD - HEAD_DIM)
    out = jnp.pad(w.reshape(shape), pad)
    shape[axis:axis + 2] = [D_HEADS * HEAD_PAD]
    return out.reshape(shape)


def kernel(x, c, ada_w, ada_b, ffn_norm, ffn_w_gate, ffn_w_up, ffn_w_down, mix_norm, conv_w_in,
           conv_a_w, conv_a_b, conv_a_ln_g, conv_a_ln_b, conv_b_w, conv_w_out, pm_w_in, pool_w,
           pool_b, pool_scale, pm_w_out, final_norm):
    assert x.shape == (1, SEQ, D_MODEL) and c.shape == (1, D_MODEL)
    xs = x.reshape(SEQ, D_MODEL)
    mod = _mod_call(c, ada_w, ada_b)

    wg = ffn_w_gate.astype(BF16)
    wu = ffn_w_up.astype(BF16)
    wd = ffn_w_down.astype(BF16)
    ffn_g = ffn_norm.reshape(DEPTH, 2, 1, D_MODEL)
    mix_g = mix_norm.reshape(DEPTH, 1, D_MODEL)
    fin = final_norm.reshape(1, D_MODEL)
    ffn = functools.partial(_ffn_call, mod=mod, norm_g=ffn_g, wg=wg, wu=wu, wd=wd, fin=fin)

    xs = ffn(xs, layer=0, sub=0, final=False)
    grp = np.arange(A_WIDTH) // (A_WIDTH // A_GROUPS)
    seg = jnp.asarray((grp[:, None] == grp[None, :]) / (A_WIDTH // A_GROUPS), dtype=BF16)
    xs = _conv_mix_call(
        xs, mod, mix_g, conv_w_in[0].astype(BF16), conv_a_w[0], conv_a_b[0].reshape(1, A_WIDTH),
        conv_a_ln_g[0].reshape(1, A_WIDTH), conv_a_ln_b[0].reshape(1, A_WIDTH), conv_b_w[0],
        conv_w_out[0].astype(BF16), seg, layer=0)
    xs = ffn(xs, layer=0, sub=1, final=False)

    xs = ffn(xs, layer=1, sub=0, final=False)
    w_in = pm_w_in[0].astype(BF16)
    w_u = w_in[:, 0:C_WIDTH]
    w_q = w_in[:, C_WIDTH:C_WIDTH + D_WIDTH]
    w_kvt = w_in[:, C_WIDTH + D_WIDTH:C_WIDTH + 3 * D_WIDTH].T
    pooled, qrows, kv, kmean = _pm_proj_call(
        xs, mod, mix_g, w_u, w_q, w_kvt, pool_w[0].astype(BF16),
        pool_b[0].reshape(1, C_WIDTH), pool_scale[0].reshape(1, C_WIDTH), layer=1)
    kmean_h = _head_pad(kmean, 1).reshape(N_BLOCKS, D_HEADS, HEAD_PAD).transpose(1, 0, 2)

    n_pair = MOBA_TOPK * MOBA_BLOCK
    tri = jnp.asarray(np.arange(n_pair)[:, None] < np.arange(n_pair)[None, :], dtype=BF16)
    q_flat = qrows.reshape(D_HEADS * SEQ, HEAD_PAD)
    own_parts, got_parts = [], []
    for split in range(N_SPLITS):
        sel, rank, cnt, own_part = _gate_own_call(qrows, kv, kmean_h, tri, split=split)
        base_digits, tile_h, tile_j, n_tiles, tile_step = _route_tables(cnt, split)
        pos = _route_pos_call(sel, rank, base_digits)
        pos_by_slot = [pos[s].reshape(1, SPLIT_HEADS * SEQ) for s in range(MOBA_TOPK)]
        routed_q = _dispatch_call(q_flat, pos_by_slot, split=split)
        routed_parts = _routed_attn_call(tile_h, tile_j, n_tiles, tile_step, routed_q, kv,
                                         split=split)
        got = _collect_call(routed_parts, pos.reshape(1, N_PAIRS))
        own_parts.append(own_part)
        got_parts.append(got.reshape(MOBA_TOPK, SPLIT_HEADS, SEQ, HEAD_PAD))

    slopes = jnp.asarray(2.0 ** (-8.0 * np.arange(1, D_HEADS + 1) / D_HEADS), dtype=F32)
    w_out = pm_w_out[0].astype(BF16)
    w_att = _head_pad(w_out[C_WIDTH:], 0).reshape(D_HEADS, HEAD_PAD, D_MODEL)
    xs = _pm_out_call(slopes, xs, mod, pooled, own_parts, got_parts, w_out[0:C_WIDTH], w_att,
                      layer=1)
    xs = ffn(xs, layer=1, sub=1, final=True)
    return xs.reshape(1, SEQ, D_MODEL)
```

```python
import functools

import numpy as np
import jax
import jax.numpy as jnp
from jax import lax
from jax.experimental import pallas as pl
from jax.experimental.pallas import tpu as pltpu
from jax.experimental.pallas import tpu_sc as plsc

D_MODEL = 1024
SEQ = 16384
DEPTH = 2
N_SUBLAYERS = 3
D_FF = 2816
EPS = 1e-6

A_WIDTH = 512
A_GROUPS = 8
A_CONV = 31
B_WIDTH = 512
B_CONV = 3

POOL_WINDOWS = (2, 4, 8, 16)
C_WIDTH = 512
C_GROUP_DIM = 128
D_HEADS = 8
HEAD_DIM = 64
D_WIDTH = 512
MOBA_BLOCK = 256
MOBA_TOPK = 3
N_BLOCKS = SEQ // MOBA_BLOCK

V7X_SUBLANES = 8
V7X_VMEM_LIMIT_BYTES = 56 * 1024 * 1024

A_HALO = 32
B_HALO = 8
POOL_HALO = 16
NEG_BIG = -1e30

HEAD_PAD = 128
ALIBI_COL = HEAD_DIM
SLAB_K_ROWS = HEAD_DIM + 16
SLAB_ROWS = SLAB_K_ROWS + HEAD_DIM

SPLIT_HEADS = 4
N_SPLITS = D_HEADS // SPLIT_HEADS
ROUTE_TILE = 256
ROUTE_TILES_PER_STEP = 32
N_PAIRS = SPLIT_HEADS * SEQ * MOBA_TOPK
N_GROUPS = SPLIT_HEADS * N_BLOCKS
MAX_TILES = N_PAIRS // ROUTE_TILE + N_GROUPS
TRASH_ROW0 = MAX_TILES * ROUTE_TILE
ROUTE_ROWS = (MAX_TILES + ROUTE_TILES_PER_STEP) * ROUTE_TILE
SC_WINDOW = 128

BF16 = jnp.bfloat16
F32 = jnp.float32
I32 = jnp.int32


def _params(n_axes):
    return pltpu.CompilerParams(
        dimension_semantics=("arbitrary",) * n_axes,
        vmem_limit_bytes=V7X_VMEM_LIMIT_BYTES)


def _resident(block_shape, index_map):
    return pl.BlockSpec(block_shape, index_map, pipeline_mode=pl.Buffered(1))


def _full(shape):
    return _resident(shape, lambda *_: (0,) * len(shape))


def _sigmoid(v):
    return 1.0 / (1.0 + jnp.exp(-v))


def _norm_mod(x, g, mod):
    r = lax.rsqrt(jnp.mean(x * x, axis=-1, keepdims=True) + EPS)
    return (x * r) * (g * (1.0 + mod[1:2])) + mod[0:1]


def _split(v):
    hi = v.astype(BF16)
    return hi, (v - hi.astype(F32)).astype(BF16)


def _split_dot(v, w):
    hi, lo = _split(v)
    return (jnp.dot(hi, w, preferred_element_type=F32)
            + jnp.dot(lo, w, preferred_element_type=F32))


_NT = (((1,), (1,)), ((), ()))


def _mod_kernel(c_ref, w_ref, b_ref, o_ref):
    c = c_ref[...]
    cond = c * _sigmoid(c)
    o_ref[0] = jnp.sum(w_ref[0] * cond, axis=0, keepdims=True) + b_ref[0]


def _mod_call(c, ada_w, ada_b):
    n_out = N_SUBLAYERS * 3 * D_MODEL
    tn = D_MODEL
    out = pl.pallas_call(
        _mod_kernel,
        grid=(DEPTH, n_out // tn),
        in_specs=[
            pl.BlockSpec((D_MODEL, 1), lambda l, j: (0, 0)),
            pl.BlockSpec((1, D_MODEL, tn), lambda l, j: (l, 0, j)),
            pl.BlockSpec((1, 1, tn), lambda l, j: (l, 0, j)),
        ],
        out_specs=pl.BlockSpec((1, 1, tn), lambda l, j: (l, 0, j)),
        out_shape=jax.ShapeDtypeStruct((DEPTH, 1, n_out), F32),
        compiler_params=_params(2),
        name="adaln_mod",
    )(c.reshape(D_MODEL, 1), ada_w, ada_b.reshape(DEPTH, 1, n_out))
    return out.reshape(DEPTH, N_SUBLAYERS, 3, D_MODEL)


def _ffn_kernel(x_ref, mod_ref, g_ref, wg_ref, wu_ref, wd_ref, fin_ref, o_ref, *, final):
    x = x_ref[...]
    mod = mod_ref[...]
    h = _norm_mod(x, g_ref[...], mod).astype(BF16)
    gate = jnp.dot(h, wg_ref[...], preferred_element_type=F32)
    up = jnp.dot(h, wu_ref[...], preferred_element_type=F32)
    act = (gate * _sigmoid(gate) * up).astype(BF16)
    y = jnp.dot(act, wd_ref[...], preferred_element_type=F32)
    xn = x + (0.5 * mod[2:3]) * y
    if final:
        r = lax.rsqrt(jnp.mean(xn * xn, axis=-1, keepdims=True) + EPS)
        xn = xn * r * fin_ref[...]
    o_ref[...] = xn


def _ffn_call(x, mod, norm_g, wg, wu, wd, fin, *, layer, sub, final, tm=512):
    mod_sub = 0 if sub == 0 else 2
    return pl.pallas_call(
        functools.partial(_ffn_kernel, final=final),
        grid=(SEQ // tm,),
        in_specs=[
            pl.BlockSpec((tm, D_MODEL), lambda i: (i, 0)),
            _resident((None, None, 3, D_MODEL), lambda i: (layer, mod_sub, 0, 0)),
            _resident((None, None, 1, D_MODEL), lambda i: (layer, sub, 0, 0)),
            _resident((None, None, D_MODEL, D_FF), lambda i: (layer, sub, 0, 0)),
            _resident((None, None, D_MODEL, D_FF), lambda i: (layer, sub, 0, 0)),
            _resident((None, None, D_FF, D_MODEL), lambda i: (layer, sub, 0, 0)),
            _resident((1, D_MODEL), lambda i: (0, 0)),
        ],
        out_specs=pl.BlockSpec((tm, D_MODEL), lambda i: (i, 0)),
        out_shape=jax.ShapeDtypeStruct((SEQ, D_MODEL), F32),
        compiler_params=_params(1),
        name=f"ffn_l{layer}s{sub}",
    )(x, mod, norm_g, wg, wu, wd, fin)


def _conv_mix_kernel(x_ref, mod_ref, g_ref, win_ref, caw_ref, cab_ref, lng_ref, lnb_ref,
                     cbw_ref, wout_ref, seg_ref, o_ref, abuf, bbuf, sbuf, *, tm):
    i = pl.program_id(0)

    @pl.when(i == 0)
    def _():
        abuf[0:A_HALO, :] = jnp.zeros((A_HALO, A_WIDTH), F32)
        bbuf[0:B_HALO, :] = jnp.zeros((B_HALO, B_WIDTH), F32)

    x = x_ref[...]
    mod = mod_ref[...]
    h = _norm_mod(x, g_ref[...], mod).astype(BF16)
    z = jnp.dot(h, win_ref[...], preferred_element_type=F32)

    a = z[:, 0:A_WIDTH] * _sigmoid(z[:, A_WIDTH:2 * A_WIDTH])
    abuf[A_HALO:A_HALO + tm, :] = a
    acc = jnp.zeros((tm, A_WIDTH), F32) + cab_ref[...]
    ext = tm + V7X_SUBLANES
    for r in range(V7X_SUBLANES):
        part = None
        for q in range((A_CONV - 1 - r) // V7X_SUBLANES + 1):
            k = A_CONV - 1 - (V7X_SUBLANES * q + r)
            term = caw_ref[k:k + 1, :] * abuf[pl.ds(A_HALO - V7X_SUBLANES * (q + 1), ext), :]
            part = term if part is None else part + term
        if r == 0:
            acc = acc + part[V7X_SUBLANES:, :]
        else:
            sbuf[r - 1] = part
            acc = acc + sbuf[r - 1, pl.ds(V7X_SUBLANES - r, tm), :]
    abuf[0:A_HALO, :] = abuf[tm:tm + A_HALO, :]
    seg = seg_ref[...]
    mu = _split_dot(acc, seg)
    d = acc - mu
    var = _split_dot(d * d, seg)
    yn = d * lax.rsqrt(var + EPS) * lng_ref[...] + lnb_ref[...]
    a_out = yn * _sigmoid(yn)

    off = 2 * A_WIDTH
    cv = z[:, off + B_WIDTH:off + 2 * B_WIDTH] * z[:, off + 2 * B_WIDTH:off + 3 * B_WIDTH]
    bbuf[B_HALO:B_HALO + tm, :] = cv
    bacc = jnp.zeros((tm, B_WIDTH), F32)
    for k in range(B_CONV):
        bacc = bacc + cbw_ref[k:k + 1, :] * bbuf[pl.ds(B_HALO - (B_CONV - 1) + k, tm), :]
    bbuf[0:B_HALO, :] = bbuf[tm:tm + B_HALO, :]
    bb = z[:, off:off + B_WIDTH] * bacc

    y = (jnp.dot(a_out.astype(BF16), wout_ref[0:A_WIDTH, :], preferred_element_type=F32)
         + jnp.dot(bb.astype(BF16), wout_ref[A_WIDTH:A_WIDTH + B_WIDTH, :],
                   preferred_element_type=F32))
    o_ref[...] = x + mod[2:3] * y


def _conv_mix_call(x, mod, norm_g, w_in, caw, cab, lng, lnb, cbw, w_out, seg, *, layer, tm=512):
    even_in = w_in.shape[1]
    return pl.pallas_call(
        functools.partial(_conv_mix_kernel, tm=tm),
        grid=(SEQ // tm,),
        in_specs=[
            pl.BlockSpec((tm, D_MODEL), lambda i: (i, 0)),
            _resident((None, None, 3, D_MODEL), lambda i: (layer, 1, 0, 0)),
            _resident((None, 1, D_MODEL), lambda i: (layer, 0, 0)),
            _full((D_MODEL, even_in)),
            _full((A_CONV, A_WIDTH)),
            _full((1, A_WIDTH)),
            _full((1, A_WIDTH)),
            _full((1, A_WIDTH)),
            _full((B_CONV, B_WIDTH)),
            _full((A_WIDTH + B_WIDTH, D_MODEL)),
            _full((A_WIDTH, A_WIDTH)),
        ],
        out_specs=pl.BlockSpec((tm, D_MODEL), lambda i: (i, 0)),
        out_shape=jax.ShapeDtypeStruct((SEQ, D_MODEL), F32),
        scratch_shapes=[pltpu.VMEM((A_HALO + tm, A_WIDTH), F32),
                        pltpu.VMEM((B_HALO + tm, B_WIDTH), F32),
                        pltpu.VMEM((V7X_SUBLANES - 1, tm + V7X_SUBLANES, A_WIDTH), F32)],
        compiler_params=_params(1),
        name="conv_mixers",
    )(x, mod, norm_g, w_in, caw, cab, lng, lnb, cbw, w_out, seg)


def _pm_proj_kernel(x_ref, mod_ref, g_ref, wu_ref, wq_ref, wkvt_ref, pw_ref,
                    pb_ref, ps_ref, pooled_ref, qrow_ref, kv_ref, km_ref,
                    ubuf, hmean, *, tm):
    i = pl.program_id(0)

    @pl.when(i == 0)
    def _():
        ubuf[0:POOL_HALO, :] = jnp.zeros((POOL_HALO, C_WIDTH), F32)

    x = x_ref[...]
    h = _norm_mod(x, g_ref[...], mod_ref[...]).astype(BF16)
    u = jnp.dot(h, wu_ref[...], preferred_element_type=F32)
    qr = jnp.dot(h, wq_ref[...], preferred_element_type=F32)
    kvt = lax.dot_general(wkvt_ref[...], h, _NT, preferred_element_type=F32)

    ubuf[POOL_HALO:POOL_HALO + tm, :] = u
    t1 = (i * tm + 1 + lax.broadcasted_iota(I32, (tm, C_GROUP_DIM), 0)).astype(F32)
    for gi, w in enumerate(POOL_WINDOWS):
        c0 = gi * C_GROUP_DIM
        s = u[:, c0:c0 + C_GROUP_DIM]
        for k in range(1, w):
            s = s + ubuf[pl.ds(POOL_HALO - k, tm), c0:c0 + C_GROUP_DIM]
        pooled = s / jnp.minimum(t1, float(w)) - u[:, c0:c0 + C_GROUP_DIM]
        mixed = (jnp.dot(pooled.astype(BF16), pw_ref[gi], preferred_element_type=F32)
                 + pb_ref[:, c0:c0 + C_GROUP_DIM])
        pooled_ref[:, c0:c0 + C_GROUP_DIM] = (
            mixed * ps_ref[:, c0:c0 + C_GROUP_DIM]).astype(BF16)
    ubuf[0:POOL_HALO, :] = ubuf[tm:tm + POOL_HALO, :]

    lane = lax.broadcasted_iota(I32, (tm, HEAD_PAD), 1)
    tail = jnp.where(lane == ALIBI_COL, 1.0, 0.0)
    qs = qr * (HEAD_DIM ** -0.5)
    for pair in range(D_HEADS // 2):
        both = qs[:, pair * HEAD_PAD:(pair + 1) * HEAD_PAD]
        qrow_ref[2 * pair] = jnp.where(lane < HEAD_DIM, both, tail)
        qrow_ref[2 * pair + 1] = jnp.where(
            lane < HEAD_DIM, pltpu.roll(both, HEAD_DIM, axis=1), tail)
    per_tile = tm // MOBA_BLOCK
    kvb = kvt.astype(BF16)
    n_alibi = SLAB_K_ROWS - HEAD_DIM
    first = lax.broadcasted_iota(I32, (n_alibi, MOBA_BLOCK), 0) == 0
    koff = lax.broadcasted_iota(I32, (n_alibi, MOBA_BLOCK), 1).astype(F32)
    for hh in range(D_HEADS):
        alibi = jnp.where(first, koff * (2.0 ** -(hh + 1)), 0.0).astype(BF16)
        for sb in range(per_tile):
            keys = slice(sb * MOBA_BLOCK, (sb + 1) * MOBA_BLOCK)
            kv_ref[hh, sb, 0:HEAD_DIM, :] = kvb[hh * HEAD_DIM:(hh + 1) * HEAD_DIM, keys]
            kv_ref[hh, sb, HEAD_DIM:SLAB_K_ROWS, :] = alibi
            kv_ref[hh, sb, SLAB_K_ROWS:SLAB_ROWS, :] = kvb[
                D_WIDTH + hh * HEAD_DIM:D_WIDTH + (hh + 1) * HEAD_DIM, keys]

    for sb in range(per_tile):
        blk = h[sb * MOBA_BLOCK:(sb + 1) * MOBA_BLOCK, :].astype(F32)
        hmean[pl.ds(i * per_tile + sb, 1), :] = jnp.mean(blk, axis=0, keepdims=True)

    @pl.when(i == pl.num_programs(0) - 1)
    def _():
        hi, lo = _split(hmean[...])
        wkt = wkvt_ref[0:D_WIDTH, :]
        km_ref[...] = (lax.dot_general(hi, wkt, _NT, preferred_element_type=F32)
                       + lax.dot_general(lo, wkt, _NT, preferred_element_type=F32))


def _pm_proj_call(x, mod, norm_g, wu, wq, wkvt, pw, pb, ps, *, layer, tm=512):
    return pl.pallas_call(
        functools.partial(_pm_proj_kernel, tm=tm),
        grid=(SEQ // tm,),
        in_specs=[
            pl.BlockSpec((tm, D_MODEL), lambda i: (i, 0)),
            _resident((None, None, 3, D_MODEL), lambda i: (layer, 1, 0, 0)),
            _resident((None, 1, D_MODEL), lambda i: (layer, 0, 0)),
            _full((D_MODEL, C_WIDTH)),
            _full((D_MODEL, D_WIDTH)),
            _full((2 * D_WIDTH, D_MODEL)),
            _full((len(POOL_WINDOWS), C_GROUP_DIM, C_GROUP_DIM)),
            _full((1, C_WIDTH)),
            _full((1, C_WIDTH)),
        ],
        out_specs=[
            pl.BlockSpec((tm, C_WIDTH), lambda i: (i, 0)),
            pl.BlockSpec((D_HEADS, tm, HEAD_PAD), lambda i: (0, i, 0)),
            pl.BlockSpec((D_HEADS, tm // MOBA_BLOCK, SLAB_ROWS, MOBA_BLOCK),
                         lambda i: (0, i, 0, 0)),
            pl.BlockSpec((N_BLOCKS, D_WIDTH), lambda i: (0, 0)),
        ],
        out_shape=[
            jax.ShapeDtypeStruct((SEQ, C_WIDTH), BF16),
            jax.ShapeDtypeStruct((D_HEADS, SEQ, HEAD_PAD), F32),
            jax.ShapeDtypeStruct((D_HEADS, N_BLOCKS, SLAB_ROWS, MOBA_BLOCK), BF16),
            jax.ShapeDtypeStruct((N_BLOCKS, D_WIDTH), F32),
        ],
        scratch_shapes=[pltpu.VMEM((POOL_HALO + tm, C_WIDTH), F32),
                        pltpu.VMEM((N_BLOCKS, D_MODEL), F32)],
        compiler_params=_params(1),
        name="pool_qkv_proj",
    )(x, mod, norm_g, wu, wq, wkvt, pw, pb, ps)


def _block_rows(qs, slabs, mask, lse_add):
    n = qs[0].shape[0]
    kpad = jnp.zeros((HEAD_PAD - SLAB_K_ROWS, MOBA_BLOCK), BF16)
    s = jnp.concatenate(
        [jnp.dot(q, jnp.concatenate([slab[0:SLAB_K_ROWS, :], kpad], axis=0),
                 preferred_element_type=F32)
         for q, slab in zip(qs, slabs)], axis=0)
    if mask is not None:
        s = jnp.where(mask, s, NEG_BIG)
    m = jnp.max(s, axis=1, keepdims=True)
    p = jnp.exp(s - m).astype(BF16)
    vpad = jnp.zeros((HEAD_PAD - HEAD_DIM, MOBA_BLOCK), BF16)
    ones = jnp.ones((HEAD_PAD, MOBA_BLOCK), BF16)
    ot = jnp.concatenate(
        [lax.dot_general(p[u * n:(u + 1) * n],
                         jnp.concatenate([slab[SLAB_K_ROWS:SLAB_ROWS, :], vpad, ones], axis=0),
                         _NT, preferred_element_type=F32)
         for u, slab in enumerate(slabs)], axis=0)
    o, tot = ot[:, 0:HEAD_PAD], ot[:, HEAD_PAD:2 * HEAD_PAD]
    lse = m + jnp.log(tot)
    if lse_add is not None:
        lse = jnp.concatenate(
            [lse[u * n:(u + 1) * n] + add for u, add in enumerate(lse_add)], axis=0)
    low = lax.broadcasted_iota(I32, o.shape, 1) < HEAD_DIM
    return jnp.where(low, o / tot, lse)


def _onehot_pairs(sel_rows):
    blk = lax.broadcasted_iota(I32, (N_BLOCKS, MOBA_BLOCK), 0)
    return jnp.concatenate(
        [jnp.where(blk == sel_rows[s], 1.0, 0.0) for s in range(MOBA_TOPK)], axis=1)


def _gate_own_kernel(qrow_ref, kv_ref, km_ref, tri_ref,
                     sel_ref, rank_ref, cnt_ref, own_ref, *, heads):
    own = pl.program_id(0)
    n_pair = MOBA_TOPK * MOBA_BLOCK
    qs = [qrow_ref[a].astype(BF16) for a in range(heads)]

    gates = []
    for a in range(heads):
        km_hi, km_lo = _split(km_ref[a])
        gates.append(lax.dot_general(km_hi, qs[a], _NT, preferred_element_type=F32)
                     + lax.dot_general(km_lo, qs[a], _NT, preferred_element_type=F32))
    blk = lax.broadcasted_iota(I32, (N_BLOCKS, heads * MOBA_BLOCK), 0)
    g = jnp.where(blk < own, jnp.concatenate(gates, axis=1), -jnp.inf)
    slots = []
    for s in range(MOBA_TOPK):
        best = jnp.max(g, axis=0, keepdims=True)
        idx = jnp.min(jnp.where(g == best, blk, N_BLOCKS), axis=0, keepdims=True)
        g = jnp.where(blk == idx, -jnp.inf, g)
        slots.append(jnp.where(s < own, idx, N_BLOCKS))

    onehots = []
    for a in range(heads):
        sel_rows = [slot[:, a * MOBA_BLOCK:(a + 1) * MOBA_BLOCK] for slot in slots]
        sel_ref[a] = jnp.concatenate(sel_rows, axis=0)
        onehots.append(_onehot_pairs(sel_rows))
    onehot = jnp.concatenate(onehots, axis=0)
    oh = onehot.astype(BF16)
    before = jnp.dot(oh, tri_ref[...], preferred_element_type=F32)
    hit = onehot * before
    ones = jnp.ones((8, n_pair), BF16)
    cnt = lax.dot_general(ones, oh, _NT, preferred_element_type=F32)
    for a in range(heads):
        rank = jnp.sum(hit[a * N_BLOCKS:(a + 1) * N_BLOCKS], axis=0, keepdims=True).astype(I32)
        rank_ref[a] = jnp.concatenate(
            [rank[:, s * MOBA_BLOCK:(s + 1) * MOBA_BLOCK] for s in range(MOBA_TOPK)], axis=0)
        cnt_ref[a] = cnt[0:1, a * N_BLOCKS:(a + 1) * N_BLOCKS].astype(I32)

    qi = lax.broadcasted_iota(I32, (heads * MOBA_BLOCK, MOBA_BLOCK), 0) % MOBA_BLOCK
    ki = lax.broadcasted_iota(I32, (heads * MOBA_BLOCK, MOBA_BLOCK), 1)
    rows = _block_rows(qs, [kv_ref.at[a] for a in range(heads)], ki <= qi, None)
    for a in range(heads):
        own_ref[a] = rows[a * MOBA_BLOCK:(a + 1) * MOBA_BLOCK]


def _gate_own_call(qrows, kv, kmean_h, tri, *, split):
    heads = SPLIT_HEADS
    return pl.pallas_call(
        functools.partial(_gate_own_kernel, heads=heads),
        grid=(N_BLOCKS,),
        in_specs=[
            pl.BlockSpec((heads, MOBA_BLOCK, HEAD_PAD), lambda b: (split, b, 0)),
            pl.BlockSpec((heads, None, SLAB_ROWS, MOBA_BLOCK), lambda b: (split, b, 0, 0)),
            pl.BlockSpec((heads, N_BLOCKS, HEAD_PAD), lambda b: (split, 0, 0)),
            _full((MOBA_TOPK * MOBA_BLOCK, MOBA_TOPK * MOBA_BLOCK)),
        ],
        out_specs=[
            pl.BlockSpec((heads, MOBA_TOPK, MOBA_BLOCK), lambda b: (0, 0, b)),
            pl.BlockSpec((heads, MOBA_TOPK, MOBA_BLOCK), lambda b: (0, 0, b)),
            pl.BlockSpec((heads, None, 1, N_BLOCKS), lambda b: (0, b, 0, 0)),
            pl.BlockSpec((heads, MOBA_BLOCK, HEAD_PAD), lambda b: (0, b, 0)),
        ],
        out_shape=[
            jax.ShapeDtypeStruct((heads, MOBA_TOPK, SEQ), I32),
            jax.ShapeDtypeStruct((heads, MOBA_TOPK, SEQ), I32),
            jax.ShapeDtypeStruct((heads, N_BLOCKS, 1, N_BLOCKS), I32),
            jax.ShapeDtypeStruct((heads, SEQ, HEAD_PAD), F32),
        ],
        compiler_params=_params(1),
        name=f"moba_gate_own_{split}",
    )(qrows, kv, kmean_h, tri)


def _route_pos_kernel(sel_ref, rank_ref, base_ref, pos_ref, *, blocks):
    lane = lax.broadcasted_iota(I32, (1, MOBA_BLOCK), 1)
    for bb in range(blocks):
        qs = slice(bb * MOBA_BLOCK, (bb + 1) * MOBA_BLOCK)
        rows = [[] for _ in range(MOBA_TOPK)]
        for a in range(SPLIT_HEADS):
            sel = sel_ref[a, :, qs]
            onehot = _onehot_pairs([sel[s:s + 1, :] for s in range(MOBA_TOPK)]).astype(BF16)
            dig = jnp.dot(base_ref[a, bb].astype(BF16), onehot,
                          preferred_element_type=F32)
            base = (dig[0:1] * 16384.0 + dig[1:2] * 128.0 + dig[2:3]).astype(I32)
            for s in range(MOBA_TOPK):
                p = base[:, s * MOBA_BLOCK:(s + 1) * MOBA_BLOCK] + rank_ref[a, s:s + 1, qs]
                rows[s].append(
                    jnp.where(sel[s:s + 1, :] < N_BLOCKS, p, TRASH_ROW0 + lane % SC_WINDOW))
        for s in range(MOBA_TOPK):
            pos_ref[s, :, qs] = jnp.concatenate(rows[s], axis=0)


def _route_pos_call(sel, rank, base_digits, *, blocks=8):
    width = blocks * MOBA_BLOCK
    return pl.pallas_call(
        functools.partial(_route_pos_kernel, blocks=blocks),
        grid=(N_BLOCKS // blocks,),
        in_specs=[
            pl.BlockSpec((SPLIT_HEADS, MOBA_TOPK, width), lambda b: (0, 0, b)),
            pl.BlockSpec((SPLIT_HEADS, MOBA_TOPK, width), lambda b: (0, 0, b)),
            pl.BlockSpec((SPLIT_HEADS, blocks, 8, N_BLOCKS), lambda b: (0, b, 0, 0)),
        ],
        out_specs=pl.BlockSpec((MOBA_TOPK, SPLIT_HEADS, width), lambda b: (0, 0, b)),
        out_shape=jax.ShapeDtypeStruct((MOBA_TOPK, SPLIT_HEADS, SEQ), I32),
        compiler_params=_params(1),
        name="moba_route_pos",
    )(sel, rank, base_digits)


def _route_tables(cnt, split):
    cnt = cnt.reshape(SPLIT_HEADS, N_BLOCKS, N_BLOCKS)
    tiles = (cnt.sum(axis=1) + ROUTE_TILE - 1) // ROUTE_TILE
    tiles_flat = tiles.reshape(N_GROUPS)
    tile0 = jnp.cumsum(tiles_flat) - tiles_flat
    n_tiles = tiles_flat.sum().astype(I32)
    within = jnp.cumsum(cnt, axis=1) - cnt
    base = tile0.reshape(SPLIT_HEADS, 1, N_BLOCKS) * ROUTE_TILE + within
    digits = jnp.stack([base // 16384, (base // 128) % 128, base % 128], axis=2).astype(F32)
    digits = jnp.pad(digits, ((0, 0), (0, 0), (0, 8 - 3), (0, 0)))
    t = jnp.arange(MAX_TILES, dtype=I32)
    ended = (tile0 + tiles_flat)[None, :] <= t[:, None]
    group_of_tile = jnp.minimum(ended.sum(axis=1).astype(I32), N_GROUPS - 1)
    tile_h = group_of_tile // N_BLOCKS
    head = split * SPLIT_HEADS + tile_h
    tile_step = lax.bitcast_convert_type((127 + 7 - head) << 23, F32)
    return digits, tile_h, group_of_tile % N_BLOCKS, n_tiles.reshape(1), tile_step


def _sc_mesh():
    return plsc.VectorSubcoreMesh(core_axis_name="core", subcore_axis_name="subcore")


def _dispatch_call(rows, pos_by_slot, *, split):
    n_rows = SPLIT_HEADS * SEQ
    first_window = split * n_rows // SC_WINDOW

    @functools.partial(
        pl.kernel, mesh=_sc_mesh(), scratch_types=[],
        out_type=jax.ShapeDtypeStruct((ROUTE_ROWS, HEAD_PAD), F32))
    def dispatch(x_hbm, i0_hbm, i1_hbm, i2_hbm, o_hbm):
        def body(x_vmem, i0_vmem, i1_vmem, i2_vmem):
            pltpu.sync_copy(x_vmem, o_hbm.at[i0_vmem.at[0]])
            pltpu.sync_copy(x_vmem, o_hbm.at[i1_vmem.at[0]])
            pltpu.sync_copy(x_vmem, o_hbm.at[i2_vmem.at[0]])

        idx_spec = pl.BlockSpec((1, SC_WINDOW), lambda i: (0, i))
        pltpu.emit_pipeline(
            body,
            grid=(n_rows // SC_WINDOW,),
            in_specs=[pl.BlockSpec((SC_WINDOW, HEAD_PAD), lambda i: (first_window + i, 0)),
                      idx_spec, idx_spec, idx_spec],
            out_specs=[],
            core_axis_name=("core", "subcore"),
            dimension_semantics=(pltpu.PARALLEL,),
        )(x_hbm, i0_hbm, i1_hbm, i2_hbm)

    return dispatch(rows, *pos_by_slot)


def _collect_call(table, idx):
    n_rows = idx.shape[1]

    @functools.partial(
        pl.kernel, mesh=_sc_mesh(), scratch_types=[],
        out_type=jax.ShapeDtypeStruct((n_rows, HEAD_PAD), F32))
    def collect(x_hbm, i_hbm, o_hbm):
        def body(i_vmem, o_vmem):
            pltpu.sync_copy(x_hbm.at[i_vmem.at[0]], o_vmem)

        pltpu.emit_pipeline(
            body,
            grid=(n_rows // SC_WINDOW,),
            in_specs=[pl.BlockSpec((1, SC_WINDOW), lambda i: (0, i))],
            out_specs=[pl.BlockSpec((SC_WINDOW, HEAD_PAD), lambda i: (i, 0))],
            core_axis_name=("core", "subcore"),
            dimension_semantics=(pltpu.PARALLEL,),
        )(i_hbm, o_hbm)

    return collect(table, idx)


def _routed_attn_kernel(th_ref, tj_ref, nt_ref, step_ref, q_ref, *refs, tiles):
    kv_refs, o_ref = refs[0:tiles], refs[tiles]
    t0 = pl.program_id(0) * tiles

    @pl.when(t0 < nt_ref[0])
    def _():
        qs = [q_ref[u * ROUTE_TILE:(u + 1) * ROUTE_TILE, :].astype(BF16) for u in range(tiles)]
        adds = [step_ref[t0 + u] * tj_ref[t0 + u].astype(F32) for u in range(tiles)]
        o_ref[...] = _block_rows(qs, kv_refs, None, adds)


def _routed_attn_call(tile_h, tile_j, n_tiles, tile_step, routed_q, kv,
                      *, split, tiles=ROUTE_TILES_PER_STEP):
    live = lambda s, th, tj, nt: (jnp.minimum(s, (nt[0] + tiles - 1) // tiles), 0)
    head0 = split * SPLIT_HEADS

    def kv_spec(u):
        return pl.BlockSpec(
            (None, None, SLAB_ROWS, MOBA_BLOCK),
            lambda s, th, tj, nt: (head0 + th[s * tiles + u], tj[s * tiles + u], 0, 0))

    grid_spec = pltpu.PrefetchScalarGridSpec(
        num_scalar_prefetch=3,
        grid=(MAX_TILES // tiles,),
        in_specs=([pl.BlockSpec(memory_space=pltpu.SMEM),
                   pl.BlockSpec((tiles * ROUTE_TILE, HEAD_PAD), live)]
                  + [kv_spec(u) for u in range(tiles)]),
        out_specs=pl.BlockSpec((tiles * ROUTE_TILE, HEAD_PAD), live),
    )
    return pl.pallas_call(
        functools.partial(_routed_attn_kernel, tiles=tiles),
        grid_spec=grid_spec,
        out_shape=jax.ShapeDtypeStruct((ROUTE_ROWS, HEAD_PAD), F32),
        compiler_params=_params(1),
        name=f"moba_routed_attn_{split}",
    )(tile_h, tile_j, n_tiles, tile_step, routed_q, *([kv] * tiles))


def _pm_out_kernel(slope_ref, x_ref, mod_ref, pooled_ref, *refs):
    own_refs, got_refs = refs[0:N_SPLITS], refs[N_SPLITS:2 * N_SPLITS]
    wp_ref, wa_ref, o_ref = refs[2 * N_SPLITS:]
    own = pl.program_id(0)
    low = lax.broadcasted_iota(I32, (MOBA_BLOCK, HEAD_PAD), 1) < HEAD_DIM

    def lse_of(part):
        return jnp.where(low, pltpu.roll(part, HEAD_DIM, axis=1), part)

    y = jnp.dot(pooled_ref[...], wp_ref[...], preferred_element_type=F32)
    for hh in range(D_HEADS):
        own_ref, got_ref = own_refs[hh // SPLIT_HEADS], got_refs[hh // SPLIT_HEADS]
        a = hh % SPLIT_HEADS
        own_shift = slope_ref[hh] * (own * MOBA_BLOCK).astype(F32)
        parts = [own_ref[a]]
        lses = [lse_of(parts[0])]
        for s in range(MOBA_TOPK):
            valid = s < own
            part = got_ref[s, a]
            parts.append(jnp.where(valid, part, 0.0))
            lses.append(jnp.where(valid, lse_of(part) - own_shift, NEG_BIG))
        top = functools.reduce(jnp.maximum, lses)
        num = jnp.zeros((MOBA_BLOCK, HEAD_PAD), F32)
        den = jnp.zeros((MOBA_BLOCK, HEAD_PAD), F32)
        for part, ls in zip(parts, lses):
            w = jnp.exp(ls - top)
            num = num + w * part
            den = den + w
        att = (num / den).astype(BF16)
        y = y + jnp.dot(att, wa_ref[hh], preferred_element_type=F32)
    o_ref[...] = x_ref[...] + mod_ref[2:3, :] * y


def _pm_out_call(slopes, x, mod, pooled, own_parts, got_parts, wp, wa, *, layer):
    tm = MOBA_BLOCK
    return pl.pallas_call(
        _pm_out_kernel,
        grid=(SEQ // tm,),
        in_specs=[
            pl.BlockSpec(memory_space=pltpu.SMEM),
            pl.BlockSpec((tm, D_MODEL), lambda i: (i, 0)),
            _resident((None, None, 3, D_MODEL), lambda i: (layer, 1, 0, 0)),
            pl.BlockSpec((tm, C_WIDTH), lambda i: (i, 0)),
        ] + [pl.BlockSpec((SPLIT_HEADS, tm, HEAD_PAD), lambda i: (0, i, 0))] * N_SPLITS
        + [pl.BlockSpec((MOBA_TOPK, SPLIT_HEADS, tm, HEAD_PAD), lambda i: (0, 0, i, 0))] * N_SPLITS
        + [
            _full((C_WIDTH, D_MODEL)),
            _full((D_HEADS, HEAD_PAD, D_MODEL)),
        ],
        out_specs=pl.BlockSpec((tm, D_MODEL), lambda i: (i, 0)),
        out_shape=jax.ShapeDtypeStruct((SEQ, D_MODEL), F32),
        compiler_params=_params(1),
        name="pool_moba_out",
    )(slopes, x, mod, pooled, *own_parts, *got_parts, wp, wa)


def _head_pad(w, axis):
    shape = list(w.shape)
    shape[axis:axis + 1] = [D_HEADS, HEAD_DIM]
    pad = [(0, 0)] * len(shape)
    pad[axis + 1] = (0, HEAD_PAD - HEAD_DIM)
    out = jnp.pad(w.reshape(shape), pad)
    shape[axis:axis + 2] = [D_HEADS * HEAD_PAD]
    return out.reshape(shape)


def kernel(x, c, ada_w, ada_b, ffn_norm, ffn_w_gate, ffn_w_up, ffn_w_down, mix_norm, conv_w_in,
           conv_a_w, conv_a_b, conv_a_ln_g, conv_a_ln_b, conv_b_w, conv_w_out, pm_w_in, pool_w,
           pool_b, pool_scale, pm_w_out, final_norm):
    assert x.shape == (1, SEQ, D_MODEL) and c.shape == (1, D_MODEL)
    xs = x.reshape(SEQ, D_MODEL)
    mod = _mod_call(c, ada_w, ada_b)

    wg = ffn_w_gate.astype(BF16)
    wu = ffn_w_up.astype(BF16)
    wd = ffn_w_down.astype(BF16)
    ffn_g = ffn_norm.reshape(DEPTH, 2, 1, D_MODEL)
    mix_g = mix_norm.reshape(DEPTH, 1, D_MODEL)
    fin = final_norm.reshape(1, D_MODEL)
    ffn = functools.partial(_ffn_call, mod=mod, norm_g=ffn_g, wg=wg, wu=wu, wd=wd, fin=fin)

    xs = ffn(xs, layer=0, sub=0, final=False)
    grp = np.arange(A_WIDTH) // (A_WIDTH // A_GROUPS)
    seg = jnp.asarray((grp[:, None] == grp[None, :]) / (A_WIDTH // A_GROUPS), dtype=BF16)
    xs = _conv_mix_call(
        xs, mod, mix_g, conv_w_in[0].astype(BF16), conv_a_w[0], conv_a_b[0].reshape(1, A_WIDTH),
        conv_a_ln_g[0].reshape(1, A_WIDTH), conv_a_ln_b[0].reshape(1, A_WIDTH), conv_b_w[0],
        conv_w_out[0].astype(BF16), seg, layer=0)
    xs = ffn(xs, layer=0, sub=1, final=False)

    xs = ffn(xs, layer=1, sub=0, final=False)
    w_in = pm_w_in[0].astype(BF16)
    w_u = w_in[:, 0:C_WIDTH]
    w_q = w_in[:, C_WIDTH:C_WIDTH + D_WIDTH]
    w_kvt = w_in[:, C_WIDTH + D_WIDTH:C_WIDTH + 3 * D_WIDTH].T
    pooled, qrows, kv, kmean = _pm_proj_call(
        xs, mod, mix_g, w_u, w_q, w_kvt, pool_w[0].astype(BF16),
        pool_b[0].reshape(1, C_WIDTH), pool_scale[0].reshape(1, C_WIDTH), layer=1)
    kmean_h = _head_pad(kmean, 1).reshape(N_BLOCKS, D_HEADS, HEAD_PAD).transpose(1, 0, 2)

    n_pair = MOBA_TOPK * MOBA_BLOCK
    tri = jnp.asarray(np.arange(n_pair)[:, None] < np.arange(n_pair)[None, :], dtype=BF16)
    q_flat = qrows.reshape(D_HEADS * SEQ, HEAD_PAD)
    own_parts, got_parts = [], []
    for split in range(N_SPLITS):
        sel, rank, cnt, own_part = _gate_own_call(qrows, kv, kmean_h, tri, split=split)
        base_digits, tile_h, tile_j, n_tiles, tile_step = _route_tables(cnt, split)
        pos = _route_pos_call(sel, rank, base_digits)
        pos_by_slot = [pos[s].reshape(1, SPLIT_HEADS * SEQ) for s in range(MOBA_TOPK)]
        routed_q = _dispatch_call(q_flat, pos_by_slot, split=split)
        routed_parts = _routed_attn_call(tile_h, tile_j, n_tiles, tile_step, routed_q, kv,
                                         split=split)
        got = _collect_call(routed_parts, pos.reshape(1, N_PAIRS))
        own_parts.append(own_part)
        got_parts.append(got.reshape(MOBA_TOPK, SPLIT_HEADS, SEQ, HEAD_PAD))

    slopes = jnp.asarray(2.0 ** (-8.0 * np.arange(1, D_HEADS + 1) / D_HEADS), dtype=F32)
    w_out = pm_w_out[0].astype(BF16)
    w_att = _head_pad(w_out[C_WIDTH:], 0).reshape(D_HEADS, HEAD_PAD, D_MODEL)
    xs = _pm_out_call(slopes, xs, mod, pooled, own_parts, got_parts, w_out[0:C_WIDTH], w_att,
                      layer=1)
    xs = ffn(xs, layer=1, sub=1, final=True)
    return xs.reshape(1, SEQ, D_MODEL)
```

```python
import functools

import numpy as np
import jax
import jax.numpy as jnp
from jax import lax
from jax.experimental import pallas as pl
from jax.experimental.pallas import tpu as pltpu
from jax.experimental.pallas import tpu_sc as plsc

D_MODEL = 1024
SEQ = 16384
DEPTH = 2
N_SUBLAYERS = 3
D_FF = 2816
EPS = 1e-6

A_WIDTH = 512
A_GROUPS = 8
A_CONV = 31
B_WIDTH = 512
B_CONV = 3

POOL_WINDOWS = (2, 4, 8, 16)
C_WIDTH = 512
C_GROUP_DIM = 128
D_HEADS = 8
HEAD_DIM = 64
D_WIDTH = 512
MOBA_BLOCK = 256
MOBA_TOPK = 3
N_BLOCKS = SEQ // MOBA_BLOCK

V7X_SUBLANES = 8
V7X_VMEM_LIMIT_BYTES = 56 * 1024 * 1024

A_HALO = 32
B_HALO = 8
POOL_HALO = 16
NEG_BIG = -1e30

HEAD_PAD = 128
ALIBI_COL = HEAD_DIM
SLAB_K_ROWS = HEAD_DIM + 16
SLAB_ROWS = SLAB_K_ROWS + HEAD_DIM

SPLIT_HEADS = 4
N_SPLITS = D_HEADS // SPLIT_HEADS
ROUTE_TILE = 256
ROUTE_TILES_PER_STEP = 32
GATE_BLOCKS_PER_STEP = 4
N_PAIRS = SPLIT_HEADS * SEQ * MOBA_TOPK
N_GROUPS = SPLIT_HEADS * N_BLOCKS
MAX_TILES = N_PAIRS // ROUTE_TILE + N_GROUPS
TRASH_ROW0 = MAX_TILES * ROUTE_TILE
ROUTE_ROWS = (MAX_TILES + ROUTE_TILES_PER_STEP) * ROUTE_TILE
SC_WINDOW = 128

BF16 = jnp.bfloat16
F32 = jnp.float32
I32 = jnp.int32


def _params(n_axes):
    return pltpu.CompilerParams(
        dimension_semantics=("arbitrary",) * n_axes,
        vmem_limit_bytes=V7X_VMEM_LIMIT_BYTES)


def _resident(block_shape, index_map):
    return pl.BlockSpec(block_shape, index_map, pipeline_mode=pl.Buffered(1))


def _full(shape):
    return _resident(shape, lambda *_: (0,) * len(shape))


def _sigmoid(v):
    return 1.0 / (1.0 + jnp.exp(-v))


def _norm_mod(x, g, mod):
    r = lax.rsqrt(jnp.mean(x * x, axis=-1, keepdims=True) + EPS)
    return (x * r) * (g * (1.0 + mod[1:2])) + mod[0:1]


def _split(v):
    hi = v.astype(BF16)
    return hi, (v - hi.astype(F32)).astype(BF16)


def _split_dot(v, w):
    hi, lo = _split(v)
    return (jnp.dot(hi, w, preferred_element_type=F32)
            + jnp.dot(lo, w, preferred_element_type=F32))


_NT = (((1,), (1,)), ((), ()))


def _mod_kernel(c_ref, w_ref, b_ref, o_ref):
    c = c_ref[...]
    cond = c * _sigmoid(c)
    o_ref[0] = jnp.sum(w_ref[0] * cond, axis=0, keepdims=True) + b_ref[0]


def _mod_call(c, ada_w, ada_b):
    n_out = N_SUBLAYERS * 3 * D_MODEL
    tn = D_MODEL
    out = pl.pallas_call(
        _mod_kernel,
        grid=(DEPTH, n_out // tn),
        in_specs=[
            pl.BlockSpec((D_MODEL, 1), lambda l, j: (0, 0)),
            pl.BlockSpec((1, D_MODEL, tn), lambda l, j: (l, 0, j)),
            pl.BlockSpec((1, 1, tn), lambda l, j: (l, 0, j)),
        ],
        out_specs=pl.BlockSpec((1, 1, tn), lambda l, j: (l, 0, j)),
        out_shape=jax.ShapeDtypeStruct((DEPTH, 1, n_out), F32),
        compiler_params=_params(2),
        name="adaln_mod",
    )(c.reshape(D_MODEL, 1), ada_w, ada_b.reshape(DEPTH, 1, n_out))
    return out.reshape(DEPTH, N_SUBLAYERS, 3, D_MODEL)


def _ffn_kernel(x_ref, mod_ref, g_ref, wg_ref, wu_ref, wd_ref, fin_ref, o_ref, *, final):
    x = x_ref[...]
    mod = mod_ref[...]
    h = _norm_mod(x, g_ref[...], mod).astype(BF16)
    gate = jnp.dot(h, wg_ref[...], preferred_element_type=F32)
    up = jnp.dot(h, wu_ref[...], preferred_element_type=F32)
    act = (gate * _sigmoid(gate) * up).astype(BF16)
    y = jnp.dot(act, wd_ref[...], preferred_element_type=F32)
    xn = x + (0.5 * mod[2:3]) * y
    if final:
        r = lax.rsqrt(jnp.mean(xn * xn, axis=-1, keepdims=True) + EPS)
        xn = xn * r * fin_ref[...]
    o_ref[...] = xn


def _ffn_call(x, mod, norm_g, wg, wu, wd, fin, *, layer, sub, final, tm=512):
    mod_sub = 0 if sub == 0 else 2
    return pl.pallas_call(
        functools.partial(_ffn_kernel, final=final),
        grid=(SEQ // tm,),
        in_specs=[
            pl.BlockSpec((tm, D_MODEL), lambda i: (i, 0)),
            _resident((None, None, 3, D_MODEL), lambda i: (layer, mod_sub, 0, 0)),
            _resident((None, None, 1, D_MODEL), lambda i: (layer, sub, 0, 0)),
            _resident((None, None, D_MODEL, D_FF), lambda i: (layer, sub, 0, 0)),
            _resident((None, None, D_MODEL, D_FF), lambda i: (layer, sub, 0, 0)),
            _resident((None, None, D_FF, D_MODEL), lambda i: (layer, sub, 0, 0)),
            _resident((1, D_MODEL), lambda i: (0, 0)),
        ],
        out_specs=pl.BlockSpec((tm, D_MODEL), lambda i: (i, 0)),
        out_shape=jax.ShapeDtypeStruct((SEQ, D_MODEL), F32),
        compiler_params=_params(1),
        name=f"ffn_l{layer}s{sub}",
    )(x, mod, norm_g, wg, wu, wd, fin)


def _conv_mix_kernel(x_ref, mod_ref, g_ref, win_ref, caw_ref, cab_ref, lng_ref, lnb_ref,
                     cbw_ref, wout_ref, seg_ref, o_ref, abuf, bbuf, sbuf, *, tm):
    i = pl.program_id(0)

    @pl.when(i == 0)
    def _():
        abuf[0:A_HALO, :] = jnp.zeros((A_HALO, A_WIDTH), F32)
        bbuf[0:B_HALO, :] = jnp.zeros((B_HALO, B_WIDTH), F32)

    x = x_ref[...]
    mod = mod_ref[...]
    h = _norm_mod(x, g_ref[...], mod).astype(BF16)
    z = jnp.dot(h, win_ref[...], preferred_element_type=F32)

    a = z[:, 0:A_WIDTH] * _sigmoid(z[:, A_WIDTH:2 * A_WIDTH])
    abuf[A_HALO:A_HALO + tm, :] = a
    acc = jnp.zeros((tm, A_WIDTH), F32) + cab_ref[...]
    ext = tm + V7X_SUBLANES
    for r in range(V7X_SUBLANES):
        part = None
        for q in range((A_CONV - 1 - r) // V7X_SUBLANES + 1):
            k = A_CONV - 1 - (V7X_SUBLANES * q + r)
            term = caw_ref[k:k + 1, :] * abuf[pl.ds(A_HALO - V7X_SUBLANES * (q + 1), ext), :]
            part = term if part is None else part + term
        if r == 0:
            acc = acc + part[V7X_SUBLANES:, :]
        else:
            sbuf[r - 1] = part
            acc = acc + sbuf[r - 1, pl.ds(V7X_SUBLANES - r, tm), :]
    abuf[0:A_HALO, :] = abuf[tm:tm + A_HALO, :]
    seg = seg_ref[...]
    mu = _split_dot(acc, seg)
    d = acc - mu
    var = _split_dot(d * d, seg)
    yn = d * lax.rsqrt(var + EPS) * lng_ref[...] + lnb_ref[...]
    a_out = yn * _sigmoid(yn)

    off = 2 * A_WIDTH
    cv = z[:, off + B_WIDTH:off + 2 * B_WIDTH] * z[:, off + 2 * B_WIDTH:off + 3 * B_WIDTH]
    bbuf[B_HALO:B_HALO + tm, :] = cv
    bacc = jnp.zeros((tm, B_WIDTH), F32)
    for k in range(B_CONV):
        bacc = bacc + cbw_ref[k:k + 1, :] * bbuf[pl.ds(B_HALO - (B_CONV - 1) + k, tm), :]
    bbuf[0:B_HALO, :] = bbuf[tm:tm + B_HALO, :]
    bb = z[:, off:off + B_WIDTH] * bacc

    y = (jnp.dot(a_out.astype(BF16), wout_ref[0:A_WIDTH, :], preferred_element_type=F32)
         + jnp.dot(bb.astype(BF16), wout_ref[A_WIDTH:A_WIDTH + B_WIDTH, :],
                   preferred_element_type=F32))
    o_ref[...] = x + mod[2:3] * y


def _conv_mix_call(x, mod, norm_g, w_in, caw, cab, lng, lnb, cbw, w_out, seg, *, layer, tm=512):
    even_in = w_in.shape[1]
    return pl.pallas_call(
        functools.partial(_conv_mix_kernel, tm=tm),
        grid=(SEQ // tm,),
        in_specs=[
            pl.BlockSpec((tm, D_MODEL), lambda i: (i, 0)),
            _resident((None, None, 3, D_MODEL), lambda i: (layer, 1, 0, 0)),
            _resident((None, 1, D_MODEL), lambda i: (layer, 0, 0)),
            _full((D_MODEL, even_in)),
            _full((A_CONV, A_WIDTH)),
            _full((1, A_WIDTH)),
            _full((1, A_WIDTH)),
            _full((1, A_WIDTH)),
            _full((B_CONV, B_WIDTH)),
            _full((A_WIDTH + B_WIDTH, D_MODEL)),
            _full((A_WIDTH, A_WIDTH)),
        ],
        out_specs=pl.BlockSpec((tm, D_MODEL), lambda i: (i, 0)),
        out_shape=jax.ShapeDtypeStruct((SEQ, D_MODEL), F32),
        scratch_shapes=[pltpu.VMEM((A_HALO + tm, A_WIDTH), F32),
                        pltpu.VMEM((B_HALO + tm, B_WIDTH), F32),
                        pltpu.VMEM((V7X_SUBLANES - 1, tm + V7X_SUBLANES, A_WIDTH), F32)],
        compiler_params=_params(1),
        name="conv_mixers",
    )(x, mod, norm_g, w_in, caw, cab, lng, lnb, cbw, w_out, seg)


def _pm_proj_kernel(x_ref, mod_ref, g_ref, wu_ref, wq_ref, wkvt_ref, pw_ref,
                    pb_ref, ps_ref, pooled_ref, qrow_ref, kv_ref, km_ref,
                    ubuf, hmean, *, tm):
    i = pl.program_id(0)

    @pl.when(i == 0)
    def _():
        ubuf[0:POOL_HALO, :] = jnp.zeros((POOL_HALO, C_WIDTH), F32)

    x = x_ref[...]
    h = _norm_mod(x, g_ref[...], mod_ref[...]).astype(BF16)
    u = jnp.dot(h, wu_ref[...], preferred_element_type=F32)
    qr = jnp.dot(h, wq_ref[...], preferred_element_type=F32)
    kvt = lax.dot_general(wkvt_ref[...], h, _NT, preferred_element_type=F32)

    ubuf[POOL_HALO:POOL_HALO + tm, :] = u
    t1 = (i * tm + 1 + lax.broadcasted_iota(I32, (tm, C_GROUP_DIM), 0)).astype(F32)
    for gi, w in enumerate(POOL_WINDOWS):
        c0 = gi * C_GROUP_DIM
        s = u[:, c0:c0 + C_GROUP_DIM]
        for k in range(1, w):
            s = s + ubuf[pl.ds(POOL_HALO - k, tm), c0:c0 + C_GROUP_DIM]
        pooled = s / jnp.minimum(t1, float(w)) - u[:, c0:c0 + C_GROUP_DIM]
        mixed = (jnp.dot(pooled.astype(BF16), pw_ref[gi], preferred_element_type=F32)
                 + pb_ref[:, c0:c0 + C_GROUP_DIM])
        pooled_ref[:, c0:c0 + C_GROUP_DIM] = (
            mixed * ps_ref[:, c0:c0 + C_GROUP_DIM]).astype(BF16)
    ubuf[0:POOL_HALO, :] = ubuf[tm:tm + POOL_HALO, :]

    lane = lax.broadcasted_iota(I32, (tm, HEAD_PAD), 1)
    tail = jnp.where(lane == ALIBI_COL, 1.0, 0.0)
    qs = qr * (HEAD_DIM ** -0.5)
    for pair in range(D_HEADS // 2):
        both = qs[:, pair * HEAD_PAD:(pair + 1) * HEAD_PAD]
        qrow_ref[2 * pair] = jnp.where(lane < HEAD_DIM, both, tail)
        qrow_ref[2 * pair + 1] = jnp.where(
            lane < HEAD_DIM, pltpu.roll(both, HEAD_DIM, axis=1), tail)
    per_tile = tm // MOBA_BLOCK
    kvb = kvt.astype(BF16)
    n_alibi = SLAB_K_ROWS - HEAD_DIM
    first = lax.broadcasted_iota(I32, (n_alibi, MOBA_BLOCK), 0) == 0
    koff = lax.broadcasted_iota(I32, (n_alibi, MOBA_BLOCK), 1).astype(F32)
    for hh in range(D_HEADS):
        alibi = jnp.where(first, koff * (2.0 ** -(hh + 1)), 0.0).astype(BF16)
        for sb in range(per_tile):
            keys = slice(sb * MOBA_BLOCK, (sb + 1) * MOBA_BLOCK)
            kv_ref[hh, sb, 0:HEAD_DIM, :] = kvb[hh * HEAD_DIM:(hh + 1) * HEAD_DIM, keys]
            kv_ref[hh, sb, HEAD_DIM:SLAB_K_ROWS, :] = alibi
            kv_ref[hh, sb, SLAB_K_ROWS:SLAB_ROWS, :] = kvb[
                D_WIDTH + hh * HEAD_DIM:D_WIDTH + (hh + 1) * HEAD_DIM, keys]

    for sb in range(per_tile):
        blk = h[sb * MOBA_BLOCK:(sb + 1) * MOBA_BLOCK, :].astype(F32)
        hmean[pl.ds(i * per_tile + sb, 1), :] = jnp.mean(blk, axis=0, keepdims=True)

    @pl.when(i == pl.num_programs(0) - 1)
    def _():
        hi, lo = _split(hmean[...])
        wkt = wkvt_ref[0:D_WIDTH, :]
        km_ref[...] = (lax.dot_general(hi, wkt, _NT, preferred_element_type=F32)
                       + lax.dot_general(lo, wkt, _NT, preferred_element_type=F32))


def _pm_proj_call(x, mod, norm_g, wu, wq, wkvt, pw, pb, ps, *, layer, tm=512):
    return pl.pallas_call(
        functools.partial(_pm_proj_kernel, tm=tm),
        grid=(SEQ // tm,),
        in_specs=[
            pl.BlockSpec((tm, D_MODEL), lambda i: (i, 0)),
            _resident((None, None, 3, D_MODEL), lambda i: (layer, 1, 0, 0)),
            _resident((None, 1, D_MODEL), lambda i: (layer, 0, 0)),
            _full((D_MODEL, C_WIDTH)),
            _full((D_MODEL, D_WIDTH)),
            _full((2 * D_WIDTH, D_MODEL)),
            _full((len(POOL_WINDOWS), C_GROUP_DIM, C_GROUP_DIM)),
            _full((1, C_WIDTH)),
            _full((1, C_WIDTH)),
        ],
        out_specs=[
            pl.BlockSpec((tm, C_WIDTH), lambda i: (i, 0)),
            pl.BlockSpec((D_HEADS, tm, HEAD_PAD), lambda i: (0, i, 0)),
            pl.BlockSpec((D_HEADS, tm // MOBA_BLOCK, SLAB_ROWS, MOBA_BLOCK),
                         lambda i: (0, i, 0, 0)),
            pl.BlockSpec((N_BLOCKS, D_WIDTH), lambda i: (0, 0)),
        ],
        out_shape=[
            jax.ShapeDtypeStruct((SEQ, C_WIDTH), BF16),
            jax.ShapeDtypeStruct((D_HEADS, SEQ, HEAD_PAD), F32),
            jax.ShapeDtypeStruct((D_HEADS, N_BLOCKS, SLAB_ROWS, MOBA_BLOCK), BF16),
            jax.ShapeDtypeStruct((N_BLOCKS, D_WIDTH), F32),
        ],
        scratch_shapes=[pltpu.VMEM((POOL_HALO + tm, C_WIDTH), F32),
                        pltpu.VMEM((N_BLOCKS, D_MODEL), F32)],
        compiler_params=_params(1),
        name="pool_qkv_proj",
    )(x, mod, norm_g, wu, wq, wkvt, pw, pb, ps)


def _block_rows(qs, slabs, mask, lse_add):
    n = qs[0].shape[0]
    kpad = jnp.zeros((HEAD_PAD - SLAB_K_ROWS, MOBA_BLOCK), BF16)
    s = jnp.concatenate(
        [jnp.dot(q, jnp.concatenate([slab[0:SLAB_K_ROWS, :], kpad], axis=0),
                 preferred_element_type=F32)
         for q, slab in zip(qs, slabs)], axis=0)
    if mask is not None:
        s = jnp.where(mask, s, NEG_BIG)
    m = jnp.max(s, axis=1, keepdims=True)
    p = jnp.exp(s - m).astype(BF16)
    vpad = jnp.zeros((HEAD_PAD - HEAD_DIM, MOBA_BLOCK), BF16)
    ones = jnp.ones((HEAD_PAD, MOBA_BLOCK), BF16)
    ot = jnp.concatenate(
        [lax.dot_general(p[u * n:(u + 1) * n],
                         jnp.concatenate([slab[SLAB_K_ROWS:SLAB_ROWS, :], vpad, ones], axis=0),
                         _NT, preferred_element_type=F32)
         for u, slab in enumerate(slabs)], axis=0)
    o, tot = ot[:, 0:HEAD_PAD], ot[:, HEAD_PAD:2 * HEAD_PAD]
    lse = m + jnp.log(tot)
    if lse_add is not None:
        lse = jnp.concatenate(
            [lse[u * n:(u + 1) * n] + add for u, add in enumerate(lse_add)], axis=0)
    low = lax.broadcasted_iota(I32, o.shape, 1) < HEAD_DIM
    return jnp.where(low, o / tot, lse)


def _onehot_pairs(sel_rows):
    blk = lax.broadcasted_iota(I32, (N_BLOCKS, MOBA_BLOCK), 0)
    return jnp.concatenate(
        [jnp.where(blk == sel_rows[s], 1.0, 0.0) for s in range(MOBA_TOPK)], axis=1)


def _gate_own_kernel(qrow_ref, kv_ref, km_ref, tri_ref,
                     sel_ref, rank_ref, cnt_ref, own_ref, *, heads, blocks):
    n_pair = MOBA_TOPK * MOBA_BLOCK
    tiles = [(bb, a) for bb in range(blocks) for a in range(heads)]
    rows_of = lambda bb: slice(bb * MOBA_BLOCK, (bb + 1) * MOBA_BLOCK)
    qs = [qrow_ref[a, rows_of(bb), :].astype(BF16) for bb, a in tiles]

    kms = [_split(km_ref[a]) for a in range(heads)]
    gates = [lax.dot_general(kms[a][0], q, _NT, preferred_element_type=F32)
             + lax.dot_general(kms[a][1], q, _NT, preferred_element_type=F32)
             for (bb, a), q in zip(tiles, qs)]
    width = len(tiles) * MOBA_BLOCK
    blk = lax.broadcasted_iota(I32, (N_BLOCKS, width), 0)
    lane = lax.broadcasted_iota(I32, (1, width), 1)
    own = pl.program_id(0) * blocks + lane // (heads * MOBA_BLOCK)
    g = jnp.where(blk < own, jnp.concatenate(gates, axis=1), -jnp.inf)
    slots = []
    for s in range(MOBA_TOPK):
        best = jnp.max(g, axis=0, keepdims=True)
        idx = jnp.min(jnp.where(g == best, blk, N_BLOCKS), axis=0, keepdims=True)
        g = jnp.where(blk == idx, -jnp.inf, g)
        slots.append(jnp.where(s < own, idx, N_BLOCKS))

    onehots = []
    for t, (bb, a) in enumerate(tiles):
        sel_rows = [slot[:, t * MOBA_BLOCK:(t + 1) * MOBA_BLOCK] for slot in slots]
        sel_ref[a, :, rows_of(bb)] = jnp.concatenate(sel_rows, axis=0)
        onehots.append(_onehot_pairs(sel_rows))
    onehot = jnp.concatenate(onehots, axis=0)
    oh = onehot.astype(BF16)
    before = jnp.dot(oh, tri_ref[...], preferred_element_type=F32)
    hit = onehot * before
    ones = jnp.ones((8, n_pair), BF16)
    cnt = lax.dot_general(ones, oh, _NT, preferred_element_type=F32)
    for t, (bb, a) in enumerate(tiles):
        rank = jnp.sum(hit[t * N_BLOCKS:(t + 1) * N_BLOCKS], axis=0, keepdims=True).astype(I32)
        rank_ref[a, :, rows_of(bb)] = jnp.concatenate(
            [rank[:, s * MOBA_BLOCK:(s + 1) * MOBA_BLOCK] for s in range(MOBA_TOPK)], axis=0)
        cnt_ref[a, bb] = cnt[0:1, t * N_BLOCKS:(t + 1) * N_BLOCKS].astype(I32)

    qi = lax.broadcasted_iota(I32, (width, MOBA_BLOCK), 0) % MOBA_BLOCK
    ki = lax.broadcasted_iota(I32, (width, MOBA_BLOCK), 1)
    rows = _block_rows(qs, [kv_ref.at[a, bb] for bb, a in tiles], ki <= qi, None)
    for t, (bb, a) in enumerate(tiles):
        own_ref[a, rows_of(bb), :] = rows[t * MOBA_BLOCK:(t + 1) * MOBA_BLOCK]


def _gate_own_call(qrows, kv, kmean_h, tri, *, split, blocks=GATE_BLOCKS_PER_STEP):
    heads = SPLIT_HEADS
    rows = blocks * MOBA_BLOCK
    return pl.pallas_call(
        functools.partial(_gate_own_kernel, heads=heads, blocks=blocks),
        grid=(N_BLOCKS // blocks,),
        in_specs=[
            pl.BlockSpec((heads, rows, HEAD_PAD), lambda b: (split, b, 0)),
            pl.BlockSpec((heads, blocks, SLAB_ROWS, MOBA_BLOCK), lambda b: (split, b, 0, 0)),
            pl.BlockSpec((heads, N_BLOCKS, HEAD_PAD), lambda b: (split, 0, 0)),
            _full((MOBA_TOPK * MOBA_BLOCK, MOBA_TOPK * MOBA_BLOCK)),
        ],
        out_specs=[
            pl.BlockSpec((heads, MOBA_TOPK, rows), lambda b: (0, 0, b)),
            pl.BlockSpec((heads, MOBA_TOPK, rows), lambda b: (0, 0, b)),
            pl.BlockSpec((heads, blocks, 1, N_BLOCKS), lambda b: (0, b, 0, 0)),
            pl.BlockSpec((heads, rows, HEAD_PAD), lambda b: (0, b, 0)),
        ],
        out_shape=[
            jax.ShapeDtypeStruct((heads, MOBA_TOPK, SEQ), I32),
            jax.ShapeDtypeStruct((heads, MOBA_TOPK, SEQ), I32),
            jax.ShapeDtypeStruct((heads, N_BLOCKS, 1, N_BLOCKS), I32),
            jax.ShapeDtypeStruct((heads, SEQ, HEAD_PAD), F32),
        ],
        compiler_params=_params(1),
        name=f"moba_gate_own_{split}",
    )(qrows, kv, kmean_h, tri)


def _route_pos_kernel(sel_ref, rank_ref, base_ref, pos_ref, *, blocks):
    lane = lax.broadcasted_iota(I32, (1, MOBA_BLOCK), 1)
    for bb in range(blocks):
        qs = slice(bb * MOBA_BLOCK, (bb + 1) * MOBA_BLOCK)
        rows = [[] for _ in range(MOBA_TOPK)]
        for a in range(SPLIT_HEADS):
            sel = sel_ref[a, :, qs]
            onehot = _onehot_pairs([sel[s:s + 1, :] for s in range(MOBA_TOPK)]).astype(BF16)
            dig = jnp.dot(base_ref[a, bb].astype(BF16), onehot,
                          preferred_element_type=F32)
            base = (dig[0:1] * 16384.0 + dig[1:2] * 128.0 + dig[2:3]).astype(I32)
            for s in range(MOBA_TOPK):
                p = base[:, s * MOBA_BLOCK:(s + 1) * MOBA_BLOCK] + rank_ref[a, s:s + 1, qs]
                rows[s].append(
                    jnp.where(sel[s:s + 1, :] < N_BLOCKS, p, TRASH_ROW0 + lane % SC_WINDOW))
        for s in range(MOBA_TOPK):
            pos_ref[s, :, qs] = jnp.concatenate(rows[s], axis=0)


def _route_pos_call(sel, rank, base_digits, *, blocks=8):
    width = blocks * MOBA_BLOCK
    return pl.pallas_call(
        functools.partial(_route_pos_kernel, blocks=blocks),
        grid=(N_BLOCKS // blocks,),
        in_specs=[
            pl.BlockSpec((SPLIT_HEADS, MOBA_TOPK, width), lambda b: (0, 0, b)),
            pl.BlockSpec((SPLIT_HEADS, MOBA_TOPK, width), lambda b: (0, 0, b)),
            pl.BlockSpec((SPLIT_HEADS, blocks, 8, N_BLOCKS), lambda b: (0, b, 0, 0)),
        ],
        out_specs=pl.BlockSpec((MOBA_TOPK, SPLIT_HEADS, width), lambda b: (0, 0, b)),
        out_shape=jax.ShapeDtypeStruct((MOBA_TOPK, SPLIT_HEADS, SEQ), I32),
        compiler_params=_params(1),
        name="moba_route_pos",
    )(sel, rank, base_digits)


def _route_tables(cnt, split):
    cnt = cnt.reshape(SPLIT_HEADS, N_BLOCKS, N_BLOCKS)
    tiles = (cnt.sum(axis=1) + ROUTE_TILE - 1) // ROUTE_TILE
    tiles_flat = tiles.reshape(N_GROUPS)
    tile0 = jnp.cumsum(tiles_flat) - tiles_flat
    n_tiles = tiles_flat.sum().astype(I32)
    within = jnp.cumsum(cnt, axis=1) - cnt
    base = tile0.reshape(SPLIT_HEADS, 1, N_BLOCKS) * ROUTE_TILE + within
    digits = jnp.stack([base // 16384, (base // 128) % 128, base % 128], axis=2).astype(F32)
    digits = jnp.pad(digits, ((0, 0), (0, 0), (0, 8 - 3), (0, 0)))
    t = jnp.arange(MAX_TILES, dtype=I32)
    ended = (tile0 + tiles_flat)[None, :] <= t[:, None]
    group_of_tile = jnp.minimum(ended.sum(axis=1).astype(I32), N_GROUPS - 1)
    tile_h = group_of_tile // N_BLOCKS
    head = split * SPLIT_HEADS + tile_h
    tile_step = lax.bitcast_convert_type((127 + 7 - head) << 23, F32)
    return digits, tile_h, group_of_tile % N_BLOCKS, n_tiles.reshape(1), tile_step


def _sc_mesh():
    return plsc.VectorSubcoreMesh(core_axis_name="core", subcore_axis_name="subcore")


def _dispatch_call(rows, pos_by_slot, *, split):
    n_rows = SPLIT_HEADS * SEQ
    first_window = split * n_rows // SC_WINDOW

    @functools.partial(
        pl.kernel, mesh=_sc_mesh(), scratch_types=[],
        out_type=jax.ShapeDtypeStruct((ROUTE_ROWS, HEAD_PAD), F32))
    def dispatch(x_hbm, i0_hbm, i1_hbm, i2_hbm, o_hbm):
        def body(x_vmem, i0_vmem, i1_vmem, i2_vmem):
            pltpu.sync_copy(x_vmem, o_hbm.at[i0_vmem.at[0]])
            pltpu.sync_copy(x_vmem, o_hbm.at[i1_vmem.at[0]])
            pltpu.sync_copy(x_vmem, o_hbm.at[i2_vmem.at[0]])

        idx_spec = pl.BlockSpec((1, SC_WINDOW), lambda i: (0, i))
        pltpu.emit_pipeline(
            body,
            grid=(n_rows // SC_WINDOW,),
            in_specs=[pl.BlockSpec((SC_WINDOW, HEAD_PAD), lambda i: (first_window + i, 0)),
                      idx_spec, idx_spec, idx_spec],
            out_specs=[],
            core_axis_name=("core", "subcore"),
            dimension_semantics=(pltpu.PARALLEL,),
        )(x_hbm, i0_hbm, i1_hbm, i2_hbm)

    return dispatch(rows, *pos_by_slot)


def _collect_call(table, idx):
    n_rows = idx.shape[1]

    @functools.partial(
        pl.kernel, mesh=_sc_mesh(), scratch_types=[],
        out_type=jax.ShapeDtypeStruct((n_rows, HEAD_PAD), F32))
    def collect(x_hbm, i_hbm, o_hbm):
        def body(i_vmem, o_vmem):
            pltpu.sync_copy(x_hbm.at[i_vmem.at[0]], o_vmem)

        pltpu.emit_pipeline(
            body,
            grid=(n_rows // SC_WINDOW,),
            in_specs=[pl.BlockSpec((1, SC_WINDOW), lambda i: (0, i))],
            out_specs=[pl.BlockSpec((SC_WINDOW, HEAD_PAD), lambda i: (i, 0))],
            core_axis_name=("core", "subcore"),
            dimension_semantics=(pltpu.PARALLEL,),
        )(i_hbm, o_hbm)

    return collect(table, idx)


def _routed_attn_kernel(th_ref, tj_ref, nt_ref, step_ref, q_ref, *refs, tiles):
    kv_refs, o_ref = refs[0:tiles], refs[tiles]
    t0 = pl.program_id(0) * tiles

    @pl.when(t0 < nt_ref[0])
    def _():
        qs = [q_ref[u * ROUTE_TILE:(u + 1) * ROUTE_TILE, :].astype(BF16) for u in range(tiles)]
        adds = [step_ref[t0 + u] * tj_ref[t0 + u].astype(F32) for u in range(tiles)]
        o_ref[...] = _block_rows(qs, kv_refs, None, adds)


def _routed_attn_call(tile_h, tile_j, n_tiles, tile_step, routed_q, kv,
                      *, split, tiles=ROUTE_TILES_PER_STEP):
    live = lambda s, th, tj, nt: (jnp.minimum(s, (nt[0] + tiles - 1) // tiles), 0)
    head0 = split * SPLIT_HEADS

    def kv_spec(u):
        return pl.BlockSpec(
            (None, None, SLAB_ROWS, MOBA_BLOCK),
            lambda s, th, tj, nt: (head0 + th[s * tiles + u], tj[s * tiles + u], 0, 0))

    grid_spec = pltpu.PrefetchScalarGridSpec(
        num_scalar_prefetch=3,
        grid=(MAX_TILES // tiles,),
        in_specs=([pl.BlockSpec(memory_space=pltpu.SMEM),
                   pl.BlockSpec((tiles * ROUTE_TILE, HEAD_PAD), live)]
                  + [kv_spec(u) for u in range(tiles)]),
        out_specs=pl.BlockSpec((tiles * ROUTE_TILE, HEAD_PAD), live),
    )
    return pl.pallas_call(
        functools.partial(_routed_attn_kernel, tiles=tiles),
        grid_spec=grid_spec,
        out_shape=jax.ShapeDtypeStruct((ROUTE_ROWS, HEAD_PAD), F32),
        compiler_params=_params(1),
        name=f"moba_routed_attn_{split}",
    )(tile_h, tile_j, n_tiles, tile_step, routed_q, *([kv] * tiles))


def _pm_out_kernel(slope_ref, x_ref, mod_ref, pooled_ref, *refs, blocks):
    own_refs, got_refs = refs[0:N_SPLITS], refs[N_SPLITS:2 * N_SPLITS]
    wp_ref, wa_ref, o_ref = refs[2 * N_SPLITS:]
    low = lax.broadcasted_iota(I32, (MOBA_BLOCK, HEAD_PAD), 1) < HEAD_DIM

    def lse_of(part):
        return jnp.where(low, pltpu.roll(part, HEAD_DIM, axis=1), part)

    y = jnp.dot(pooled_ref[...], wp_ref[...], preferred_element_type=F32)
    for hh in range(D_HEADS):
        own_ref, got_ref = own_refs[hh // SPLIT_HEADS], got_refs[hh // SPLIT_HEADS]
        a = hh % SPLIT_HEADS
        atts = []
        for bb in range(blocks):
            own = pl.program_id(0) * blocks + bb
            rows = slice(bb * MOBA_BLOCK, (bb + 1) * MOBA_BLOCK)
            own_shift = slope_ref[hh] * (own * MOBA_BLOCK).astype(F32)
            parts = [own_ref[a, rows, :]]
            lses = [lse_of(parts[0])]
            for s in range(MOBA_TOPK):
                valid = s < own
                part = got_ref[s, a, rows, :]
                parts.append(jnp.where(valid, part, 0.0))
                lses.append(jnp.where(valid, lse_of(part) - own_shift, NEG_BIG))
            top = functools.reduce(jnp.maximum, lses)
            num = jnp.zeros((MOBA_BLOCK, HEAD_PAD), F32)
            den = jnp.zeros((MOBA_BLOCK, HEAD_PAD), F32)
            for part, ls in zip(parts, lses):
                w = jnp.exp(ls - top)
                num = num + w * part
                den = den + w
            atts.append((num / den).astype(BF16))
        y = y + jnp.dot(jnp.concatenate(atts, axis=0), wa_ref[hh], preferred_element_type=F32)
    o_ref[...] = x_ref[...] + mod_ref[2:3, :] * y


def _pm_out_call(slopes, x, mod, pooled, own_parts, got_parts, wp, wa, *, layer, blocks=2):
    tm = blocks * MOBA_BLOCK
    return pl.pallas_call(
        functools.partial(_pm_out_kernel, blocks=blocks),
        grid=(SEQ // tm,),
        in_specs=[
            pl.BlockSpec(memory_space=pltpu.SMEM),
            pl.BlockSpec((tm, D_MODEL), lambda i: (i, 0)),
            _resident((None, None, 3, D_MODEL), lambda i: (layer, 1, 0, 0)),
            pl.BlockSpec((tm, C_WIDTH), lambda i: (i, 0)),
        ] + [pl.BlockSpec((SPLIT_HEADS, tm, HEAD_PAD), lambda i: (0, i, 0))] * N_SPLITS
        + [pl.BlockSpec((MOBA_TOPK, SPLIT_HEADS, tm, HEAD_PAD), lambda i: (0, 0, i, 0))] * N_SPLITS
        + [
            _full((C_WIDTH, D_MODEL)),
            _full((D_HEADS, HEAD_PAD, D_MODEL)),
        ],
        out_specs=pl.BlockSpec((tm, D_MODEL), lambda i: (i, 0)),
        out_shape=jax.ShapeDtypeStruct((SEQ, D_MODEL), F32),
        compiler_params=_params(1),
        name="pool_moba_out",
    )(slopes, x, mod, pooled, *own_parts, *got_parts, wp, wa)


def _head_pad(w, axis):
    shape = list(w.shape)
    shape[axis:axis + 1] = [D_HEADS, HEAD_DIM]
    pad = [(0, 0)] * len(shape)
    pad[axis + 1] = (0, HEAD_PAD - HEAD_DIM)
    out = jnp.pad(w.reshape(shape), pad)
    shape[axis:axis + 2] = [D_HEADS * HEAD_PAD]
    return out.reshape(shape)


def kernel(x, c, ada_w, ada_b, ffn_norm, ffn_w_gate, ffn_w_up, ffn_w_down, mix_norm, conv_w_in,
           conv_a_w, conv_a_b, conv_a_ln_g, conv_a_ln_b, conv_b_w, conv_w_out, pm_w_in, pool_w,
           pool_b, pool_scale, pm_w_out, final_norm):
    assert x.shape == (1, SEQ, D_MODEL) and c.shape == (1, D_MODEL)
    xs = x.reshape(SEQ, D_MODEL)
    mod = _mod_call(c, ada_w, ada_b)

    wg = ffn_w_gate.astype(BF16)
    wu = ffn_w_up.astype(BF16)
    wd = ffn_w_down.astype(BF16)
    ffn_g = ffn_norm.reshape(DEPTH, 2, 1, D_MODEL)
    mix_g = mix_norm.reshape(DEPTH, 1, D_MODEL)
    fin = final_norm.reshape(1, D_MODEL)
    ffn = functools.partial(_ffn_call, mod=mod, norm_g=ffn_g, wg=wg, wu=wu, wd=wd, fin=fin)

    xs = ffn(xs, layer=0, sub=0, final=False)
    grp = np.arange(A_WIDTH) // (A_WIDTH // A_GROUPS)
    seg = jnp.asarray((grp[:, None] == grp[None, :]) / (A_WIDTH // A_GROUPS), dtype=BF16)
    xs = _conv_mix_call(
        xs, mod, mix_g, conv_w_in[0].astype(BF16), conv_a_w[0], conv_a_b[0].reshape(1, A_WIDTH),
        conv_a_ln_g[0].reshape(1, A_WIDTH), conv_a_ln_b[0].reshape(1, A_WIDTH), conv_b_w[0],
        conv_w_out[0].astype(BF16), seg, layer=0)
    xs = ffn(xs, layer=0, sub=1, final=False)

    xs = ffn(xs, layer=1, sub=0, final=False)
    w_in = pm_w_in[0].astype(BF16)
    w_u = w_in[:, 0:C_WIDTH]
    w_q = w_in[:, C_WIDTH:C_WIDTH + D_WIDTH]
    w_kvt = w_in[:, C_WIDTH + D_WIDTH:C_WIDTH + 3 * D_WIDTH].T
    pooled, qrows, kv, kmean = _pm_proj_call(
        xs, mod, mix_g, w_u, w_q, w_kvt, pool_w[0].astype(BF16),
        pool_b[0].reshape(1, C_WIDTH), pool_scale[0].reshape(1, C_WIDTH), layer=1)
    kmean_h = _head_pad(kmean, 1).reshape(N_BLOCKS, D_HEADS, HEAD_PAD).transpose(1, 0, 2)

    n_pair = MOBA_TOPK * MOBA_BLOCK
    tri = jnp.asarray(np.arange(n_pair)[:, None] < np.arange(n_pair)[None, :], dtype=BF16)
    q_flat = qrows.reshape(D_HEADS * SEQ, HEAD_PAD)
    own_parts, got_parts = [], []
    for split in range(N_SPLITS):
        sel, rank, cnt, own_part = _gate_own_call(qrows, kv, kmean_h, tri, split=split)
        base_digits, tile_h, tile_j, n_tiles, tile_step = _route_tables(cnt, split)
        pos = _route_pos_call(sel, rank, base_digits)
        pos_by_slot = [pos[s].reshape(1, SPLIT_HEADS * SEQ) for s in range(MOBA_TOPK)]
        routed_q = _dispatch_call(q_flat, pos_by_slot, split=split)
        routed_parts = _routed_attn_call(tile_h, tile_j, n_tiles, tile_step, routed_q, kv,
                                         split=split)
        got = _collect_call(routed_parts, pos.reshape(1, N_PAIRS))
        own_parts.append(own_part)
        got_parts.append(got.reshape(MOBA_TOPK, SPLIT_HEADS, SEQ, HEAD_PAD))

    slopes = jnp.asarray(2.0 ** (-8.0 * np.arange(1, D_HEADS + 1) / D_HEADS), dtype=F32)
    w_out = pm_w_out[0].astype(BF16)
    w_att = _head_pad(w_out[C_WIDTH:], 0).reshape(D_HEADS, HEAD_PAD, D_MODEL)
    xs = _pm_out_call(slopes, xs, mod, pooled, own_parts, got_parts, w_out[0:C_WIDTH], w_att,
                      layer=1)
    xs = ffn(xs, layer=1, sub=1, final=True)
    return xs.reshape(1, SEQ, D_MODEL)
```

```python
import functools

import numpy as np
import jax
import jax.numpy as jnp
from jax import lax
from jax.experimental import pallas as pl
from jax.experimental.pallas import tpu as pltpu
from jax.experimental.pallas import tpu_sc as plsc

D_MODEL = 1024
SEQ = 16384
DEPTH = 2
N_SUBLAYERS = 3
D_FF = 2816
EPS = 1e-6

A_WIDTH = 512
A_GROUPS = 8
A_CONV = 31
B_WIDTH = 512
B_CONV = 3

POOL_WINDOWS = (2, 4, 8, 16)
C_WIDTH = 512
C_GROUP_DIM = 128
D_HEADS = 8
HEAD_DIM = 64
D_WIDTH = 512
MOBA_BLOCK = 256
MOBA_TOPK = 3
N_BLOCKS = SEQ // MOBA_BLOCK

V7X_SUBLANES = 8
V7X_BF16_SUBLANES = 16
V7X_VMEM_LIMIT_BYTES = 56 * 1024 * 1024

A_HALO = 32
B_HALO = 8
POOL_HALO = 16
NEG_BIG = -1e30

HEAD_PAD = 128
ALIBI_COL = HEAD_DIM
SLAB_K_ROWS = HEAD_DIM + V7X_BF16_SUBLANES
SLAB_ROWS = SLAB_K_ROWS + HEAD_DIM

SPLIT_HEADS = 4
N_SPLITS = D_HEADS // SPLIT_HEADS
ROUTE_TILE = 256
ROUTE_TILES_PER_STEP = 32
GATE_BLOCKS_PER_STEP = 4
N_PAIRS = SPLIT_HEADS * SEQ * MOBA_TOPK
N_GROUPS = SPLIT_HEADS * N_BLOCKS
MAX_TILES = N_PAIRS // ROUTE_TILE + N_GROUPS
TRASH_ROW0 = MAX_TILES * ROUTE_TILE
ROUTE_ROWS = (MAX_TILES + ROUTE_TILES_PER_STEP) * ROUTE_TILE
SC_WINDOW = 128
DIGIT = 128
assert MAX_TILES % ROUTE_TILES_PER_STEP == 0 and N_BLOCKS % GATE_BLOCKS_PER_STEP == 0
assert ROUTE_ROWS <= DIGIT ** 3 and MOBA_BLOCK <= ROUTE_TILE

BF16 = jnp.bfloat16
F32 = jnp.float32
I32 = jnp.int32


def _params(n_axes):
    return pltpu.CompilerParams(
        dimension_semantics=("arbitrary",) * n_axes,
        vmem_limit_bytes=V7X_VMEM_LIMIT_BYTES)


def _resident(block_shape, index_map):
    return pl.BlockSpec(block_shape, index_map, pipeline_mode=pl.Buffered(1))


def _full(shape):
    return _resident(shape, lambda *_: (0,) * len(shape))


def _sigmoid(v):
    return 1.0 / (1.0 + jnp.exp(-v))


def _norm_mod(x, g, mod):
    r = lax.rsqrt(jnp.mean(x * x, axis=-1, keepdims=True) + EPS)
    return (x * r) * (g * (1.0 + mod[1:2])) + mod[0:1]


def _split(v):
    hi = v.astype(BF16)
    return hi, (v - hi.astype(F32)).astype(BF16)


def _split_dot(v, w):
    hi, lo = _split(v)
    return (jnp.dot(hi, w, preferred_element_type=F32)
            + jnp.dot(lo, w, preferred_element_type=F32))


_NT = (((1,), (1,)), ((), ()))


def _mod_kernel(c_ref, w_ref, b_ref, o_ref):
    c = c_ref[...]
    cond = c * _sigmoid(c)
    o_ref[0] = jnp.sum(w_ref[0] * cond, axis=0, keepdims=True) + b_ref[0]


def _mod_call(c, ada_w, ada_b):
    n_out = N_SUBLAYERS * 3 * D_MODEL
    tn = D_MODEL
    out = pl.pallas_call(
        _mod_kernel,
        grid=(DEPTH, n_out // tn),
        in_specs=[
            pl.BlockSpec((D_MODEL, 1), lambda l, j: (0, 0)),
            pl.BlockSpec((1, D_MODEL, tn), lambda l, j: (l, 0, j)),
            pl.BlockSpec((1, 1, tn), lambda l, j: (l, 0, j)),
        ],
        out_specs=pl.BlockSpec((1, 1, tn), lambda l, j: (l, 0, j)),
        out_shape=jax.ShapeDtypeStruct((DEPTH, 1, n_out), F32),
        compiler_params=_params(2),
        name="adaln_mod",
    )(c.reshape(D_MODEL, 1), ada_w, ada_b.reshape(DEPTH, 1, n_out))
    return out.reshape(DEPTH, N_SUBLAYERS, 3, D_MODEL)


def _ffn_kernel(x_ref, mod_ref, g_ref, wg_ref, wu_ref, wd_ref, fin_ref, o_ref, *, final):
    x = x_ref[...]
    mod = mod_ref[...]
    h = _norm_mod(x, g_ref[...], mod).astype(BF16)
    gate = jnp.dot(h, wg_ref[...], preferred_element_type=F32)
    up = jnp.dot(h, wu_ref[...], preferred_element_type=F32)
    act = (gate * _sigmoid(gate) * up).astype(BF16)
    y = jnp.dot(act, wd_ref[...], preferred_element_type=F32)
    xn = x + (0.5 * mod[2:3]) * y
    if final:
        r = lax.rsqrt(jnp.mean(xn * xn, axis=-1, keepdims=True) + EPS)
        xn = xn * r * fin_ref[...]
    o_ref[...] = xn


def _ffn_call(x, mod, norm_g, wg, wu, wd, fin, *, layer, sub, final, tm=512):
    mod_sub = 0 if sub == 0 else 2
    return pl.pallas_call(
        functools.partial(_ffn_kernel, final=final),
        grid=(SEQ // tm,),
        in_specs=[
            pl.BlockSpec((tm, D_MODEL), lambda i: (i, 0)),
            _resident((None, None, 3, D_MODEL), lambda i: (layer, mod_sub, 0, 0)),
            _resident((None, None, 1, D_MODEL), lambda i: (layer, sub, 0, 0)),
            _resident((None, None, D_MODEL, D_FF), lambda i: (layer, sub, 0, 0)),
            _resident((None, None, D_MODEL, D_FF), lambda i: (layer, sub, 0, 0)),
            _resident((None, None, D_FF, D_MODEL), lambda i: (layer, sub, 0, 0)),
            _resident((1, D_MODEL), lambda i: (0, 0)),
        ],
        out_specs=pl.BlockSpec((tm, D_MODEL), lambda i: (i, 0)),
        out_shape=jax.ShapeDtypeStruct((SEQ, D_MODEL), F32),
        compiler_params=_params(1),
        name=f"ffn_l{layer}s{sub}",
    )(x, mod, norm_g, wg, wu, wd, fin)


def _conv_mix_kernel(x_ref, mod_ref, g_ref, win_ref, caw_ref, cab_ref, lng_ref, lnb_ref,
                     cbw_ref, wout_ref, seg_ref, o_ref, abuf, bbuf, sbuf, *, tm):
    i = pl.program_id(0)

    @pl.when(i == 0)
    def _():
        abuf[0:A_HALO, :] = jnp.zeros((A_HALO, A_WIDTH), F32)
        bbuf[0:B_HALO, :] = jnp.zeros((B_HALO, B_WIDTH), F32)

    x = x_ref[...]
    mod = mod_ref[...]
    h = _norm_mod(x, g_ref[...], mod).astype(BF16)
    z = jnp.dot(h, win_ref[...], preferred_element_type=F32)

    a = z[:, 0:A_WIDTH] * _sigmoid(z[:, A_WIDTH:2 * A_WIDTH])
    abuf[A_HALO:A_HALO + tm, :] = a
    acc = jnp.zeros((tm, A_WIDTH), F32) + cab_ref[...]
    ext = tm + V7X_SUBLANES
    for r in range(V7X_SUBLANES):
        part = None
        for q in range((A_CONV - 1 - r) // V7X_SUBLANES + 1):
            k = A_CONV - 1 - (V7X_SUBLANES * q + r)
            term = caw_ref[k:k + 1, :] * abuf[pl.ds(A_HALO - V7X_SUBLANES * (q + 1), ext), :]
            part = term if part is None else part + term
        if r == 0:
            acc = acc + part[V7X_SUBLANES:, :]
        else:
            sbuf[r - 1] = part
            acc = acc + sbuf[r - 1, pl.ds(V7X_SUBLANES - r, tm), :]
    abuf[0:A_HALO, :] = abuf[tm:tm + A_HALO, :]
    seg = seg_ref[...]
    mu = _split_dot(acc, seg)
    d = acc - mu
    var = _split_dot(d * d, seg)
    yn = d * lax.rsqrt(var + EPS) * lng_ref[...] + lnb_ref[...]
    a_out = yn * _sigmoid(yn)

    off = 2 * A_WIDTH
    cv = z[:, off + B_WIDTH:off + 2 * B_WIDTH] * z[:, off + 2 * B_WIDTH:off + 3 * B_WIDTH]
    bbuf[B_HALO:B_HALO + tm, :] = cv
    bacc = jnp.zeros((tm, B_WIDTH), F32)
    for k in range(B_CONV):
        bacc = bacc + cbw_ref[k:k + 1, :] * bbuf[pl.ds(B_HALO - (B_CONV - 1) + k, tm), :]
    bbuf[0:B_HALO, :] = bbuf[tm:tm + B_HALO, :]
    bb = z[:, off:off + B_WIDTH] * bacc

    y = (jnp.dot(a_out.astype(BF16), wout_ref[0:A_WIDTH, :], preferred_element_type=F32)
         + jnp.dot(bb.astype(BF16), wout_ref[A_WIDTH:A_WIDTH + B_WIDTH, :],
                   preferred_element_type=F32))
    o_ref[...] = x + mod[2:3] * y


def _conv_mix_call(x, mod, norm_g, w_in, caw, cab, lng, lnb, cbw, w_out, seg, *, layer, tm=512):
    even_in = w_in.shape[1]
    return pl.pallas_call(
        functools.partial(_conv_mix_kernel, tm=tm),
        grid=(SEQ // tm,),
        in_specs=[
            pl.BlockSpec((tm, D_MODEL), lambda i: (i, 0)),
            _resident((None, None, 3, D_MODEL), lambda i: (layer, 1, 0, 0)),
            _resident((None, 1, D_MODEL), lambda i: (layer, 0, 0)),
            _full((D_MODEL, even_in)),
            _full((A_CONV, A_WIDTH)),
            _full((1, A_WIDTH)),
            _full((1, A_WIDTH)),
            _full((1, A_WIDTH)),
            _full((B_CONV, B_WIDTH)),
            _full((A_WIDTH + B_WIDTH, D_MODEL)),
            _full((A_WIDTH, A_WIDTH)),
        ],
        out_specs=pl.BlockSpec((tm, D_MODEL), lambda i: (i, 0)),
        out_shape=jax.ShapeDtypeStruct((SEQ, D_MODEL), F32),
        scratch_shapes=[pltpu.VMEM((A_HALO + tm, A_WIDTH), F32),
                        pltpu.VMEM((B_HALO + tm, B_WIDTH), F32),
                        pltpu.VMEM((V7X_SUBLANES - 1, tm + V7X_SUBLANES, A_WIDTH), F32)],
        compiler_params=_params(1),
        name="conv_mixers",
    )(x, mod, norm_g, w_in, caw, cab, lng, lnb, cbw, w_out, seg)


def _pool_kernel(anchor_ref, x_ref, mod_ref, g_ref, wu_ref, pw_ref, pb_ref, ps_ref, wp_ref,
                 o_ref, ubuf, *, tm):
    del anchor_ref
    i = pl.program_id(0)

    @pl.when(i == 0)
    def _():
        ubuf[0:POOL_HALO, :] = jnp.zeros((POOL_HALO, C_WIDTH), F32)

    x = x_ref[...]
    mod = mod_ref[...]
    h = _norm_mod(x, g_ref[...], mod).astype(BF16)
    u = jnp.dot(h, wu_ref[...], preferred_element_type=F32)

    ubuf[POOL_HALO:POOL_HALO + tm, :] = u
    t1 = (i * tm + 1 + lax.broadcasted_iota(I32, (tm, C_GROUP_DIM), 0)).astype(F32)
    y = jnp.zeros((tm, D_MODEL), F32)
    for gi, w in enumerate(POOL_WINDOWS):
        c0 = gi * C_GROUP_DIM
        s = u[:, c0:c0 + C_GROUP_DIM]
        for k in range(1, w):
            s = s + ubuf[pl.ds(POOL_HALO - k, tm), c0:c0 + C_GROUP_DIM]
        pooled = s / jnp.minimum(t1, float(w)) - u[:, c0:c0 + C_GROUP_DIM]
        mixed = (jnp.dot(pooled.astype(BF16), pw_ref[gi], preferred_element_type=F32)
                 + pb_ref[:, c0:c0 + C_GROUP_DIM])
        mixed = (mixed * ps_ref[:, c0:c0 + C_GROUP_DIM]).astype(BF16)
        y = y + jnp.dot(mixed, wp_ref[c0:c0 + C_GROUP_DIM, :], preferred_element_type=F32)
    ubuf[0:POOL_HALO, :] = ubuf[tm:tm + POOL_HALO, :]
    o_ref[...] = x + mod[2:3] * y


def _pool_call(anchor, x, mod, norm_g, wu, pw, pb, ps, wp, *, layer, tm=512):
    return pl.pallas_call(
        functools.partial(_pool_kernel, tm=tm),
        grid=(SEQ // tm,),
        in_specs=[
            pl.BlockSpec(memory_space=pltpu.SMEM),
            pl.BlockSpec((tm, D_MODEL), lambda i: (i, 0)),
            _resident((None, None, 3, D_MODEL), lambda i: (layer, 1, 0, 0)),
            _resident((None, 1, D_MODEL), lambda i: (layer, 0, 0)),
            _full((D_MODEL, C_WIDTH)),
            _full((len(POOL_WINDOWS), C_GROUP_DIM, C_GROUP_DIM)),
            _full((1, C_WIDTH)),
            _full((1, C_WIDTH)),
            _full((C_WIDTH, D_MODEL)),
        ],
        out_specs=pl.BlockSpec((tm, D_MODEL), lambda i: (i, 0)),
        out_shape=jax.ShapeDtypeStruct((SEQ, D_MODEL), F32),
        scratch_shapes=[pltpu.VMEM((POOL_HALO + tm, C_WIDTH), F32)],
        compiler_params=_params(1),
        name="pool_mixer",
    )(anchor, x, mod, norm_g, wu, pw, pb, ps, wp)


def _pm_proj_kernel(x_ref, mod_ref, g_ref, wq_ref, wkvt_ref, qrow_ref, kv_ref, km_ref,
                    hmean, *, tm):
    i = pl.program_id(0)
    x = x_ref[...]
    h = _norm_mod(x, g_ref[...], mod_ref[...]).astype(BF16)
    qr = jnp.dot(h, wq_ref[...], preferred_element_type=F32)
    kvt = lax.dot_general(wkvt_ref[...], h, _NT, preferred_element_type=F32)

    lane = lax.broadcasted_iota(I32, (tm, HEAD_PAD), 1)
    tail = jnp.where(lane == ALIBI_COL, 1.0, 0.0)
    qs = qr * (HEAD_DIM ** -0.5)
    for pair in range(D_HEADS // 2):
        both = qs[:, pair * HEAD_PAD:(pair + 1) * HEAD_PAD]
        qrow_ref[2 * pair] = jnp.where(lane < HEAD_DIM, both, tail)
        qrow_ref[2 * pair + 1] = jnp.where(
            lane < HEAD_DIM, pltpu.roll(both, HEAD_DIM, axis=1), tail)
    per_tile = tm // MOBA_BLOCK
    kvb = kvt.astype(BF16)
    n_alibi = SLAB_K_ROWS - HEAD_DIM
    first = lax.broadcasted_iota(I32, (n_alibi, MOBA_BLOCK), 0) == 0
    koff = lax.broadcasted_iota(I32, (n_alibi, MOBA_BLOCK), 1).astype(F32)
    for hh in range(D_HEADS):
        alibi = jnp.where(first, koff * (2.0 ** -(hh + 1)), 0.0).astype(BF16)
        for sb in range(per_tile):
            keys = slice(sb * MOBA_BLOCK, (sb + 1) * MOBA_BLOCK)
            kv_ref[hh, sb, 0:HEAD_DIM, :] = kvb[hh * HEAD_DIM:(hh + 1) * HEAD_DIM, keys]
            kv_ref[hh, sb, HEAD_DIM:SLAB_K_ROWS, :] = alibi
            kv_ref[hh, sb, SLAB_K_ROWS:SLAB_ROWS, :] = kvb[
                D_WIDTH + hh * HEAD_DIM:D_WIDTH + (hh + 1) * HEAD_DIM, keys]

    for sb in range(per_tile):
        blk = h[sb * MOBA_BLOCK:(sb + 1) * MOBA_BLOCK, :].astype(F32)
        hmean[pl.ds(i * per_tile + sb, 1), :] = jnp.mean(blk, axis=0, keepdims=True)

    @pl.when(i == pl.num_programs(0) - 1)
    def _():
        hi, lo = _split(hmean[...])
        wkt = wkvt_ref[0:D_WIDTH, :]
        km_ref[...] = (lax.dot_general(hi, wkt, _NT, preferred_element_type=F32)
                       + lax.dot_general(lo, wkt, _NT, preferred_element_type=F32))


def _pm_proj_call(x, mod, norm_g, wq, wkvt, *, layer, tm=512):
    return pl.pallas_call(
        functools.partial(_pm_proj_kernel, tm=tm),
        grid=(SEQ // tm,),
        in_specs=[
            pl.BlockSpec((tm, D_MODEL), lambda i: (i, 0)),
            _resident((None, None, 3, D_MODEL), lambda i: (layer, 1, 0, 0)),
            _resident((None, 1, D_MODEL), lambda i: (layer, 0, 0)),
            _full((D_MODEL, D_WIDTH)),
            _full((2 * D_WIDTH, D_MODEL)),
        ],
        out_specs=[
            pl.BlockSpec((D_HEADS, tm, HEAD_PAD), lambda i: (0, i, 0)),
            pl.BlockSpec((D_HEADS, tm // MOBA_BLOCK, SLAB_ROWS, MOBA_BLOCK),
                         lambda i: (0, i, 0, 0)),
            pl.BlockSpec((N_BLOCKS, D_WIDTH), lambda i: (0, 0)),
        ],
        out_shape=[
            jax.ShapeDtypeStruct((D_HEADS, SEQ, HEAD_PAD), F32),
            jax.ShapeDtypeStruct((D_HEADS, N_BLOCKS, SLAB_ROWS, MOBA_BLOCK), BF16),
            jax.ShapeDtypeStruct((N_BLOCKS, D_WIDTH), F32),
        ],
        scratch_shapes=[pltpu.VMEM((N_BLOCKS, D_MODEL), F32)],
        compiler_params=_params(1),
        name="moba_qkv_proj",
    )(x, mod, norm_g, wq, wkvt)


def _block_rows(qs, slabs, mask, lse_add):
    n = qs[0].shape[0]
    kpad = jnp.zeros((HEAD_PAD - SLAB_K_ROWS, MOBA_BLOCK), BF16)
    s = jnp.concatenate(
        [jnp.dot(q, jnp.concatenate([slab[0:SLAB_K_ROWS, :], kpad], axis=0),
                 preferred_element_type=F32)
         for q, slab in zip(qs, slabs)], axis=0)
    if mask is not None:
        s = jnp.where(mask, s, NEG_BIG)
    m = jnp.max(s, axis=1, keepdims=True)
    p = jnp.exp(s - m).astype(BF16)
    vpad = jnp.zeros((HEAD_PAD - HEAD_DIM, MOBA_BLOCK), BF16)
    ones = jnp.ones((HEAD_PAD, MOBA_BLOCK), BF16)
    ot = jnp.concatenate(
        [lax.dot_general(p[u * n:(u + 1) * n],
                         jnp.concatenate([slab[SLAB_K_ROWS:SLAB_ROWS, :], vpad, ones], axis=0),
                         _NT, preferred_element_type=F32)
         for u, slab in enumerate(slabs)], axis=0)
    o, tot = ot[:, 0:HEAD_PAD], ot[:, HEAD_PAD:2 * HEAD_PAD]
    lse = m + jnp.log(tot)
    if lse_add is not None:
        lse = jnp.concatenate(
            [lse[u * n:(u + 1) * n] + add for u, add in enumerate(lse_add)], axis=0)
    low = lax.broadcasted_iota(I32, o.shape, 1) < HEAD_DIM
    return jnp.where(low, o / tot, lse)


def _onehot_pairs(sel_rows):
    blk = lax.broadcasted_iota(I32, (N_BLOCKS, MOBA_BLOCK), 0)
    return jnp.concatenate(
        [jnp.where(blk == sel_rows[s], 1.0, 0.0) for s in range(MOBA_TOPK)], axis=1)


def _gate_own_kernel(anchor_ref, qrow_ref, kv_ref, km_ref, tri_ref,
                     sel_ref, rank_ref, cnt_ref, own_ref, *, heads, blocks):
    del anchor_ref
    n_pair = MOBA_TOPK * MOBA_BLOCK
    tiles = [(bb, a) for bb in range(blocks) for a in range(heads)]
    rows_of = lambda bb: slice(bb * MOBA_BLOCK, (bb + 1) * MOBA_BLOCK)
    qs = [qrow_ref[a, rows_of(bb), :].astype(BF16) for bb, a in tiles]

    kms = [_split(km_ref[a]) for a in range(heads)]
    gates = [lax.dot_general(kms[a][0], q, _NT, preferred_element_type=F32)
             + lax.dot_general(kms[a][1], q, _NT, preferred_element_type=F32)
             for (bb, a), q in zip(tiles, qs)]
    width = len(tiles) * MOBA_BLOCK
    blk = lax.broadcasted_iota(I32, (N_BLOCKS, width), 0)
    lane = lax.broadcasted_iota(I32, (1, width), 1)
    own = pl.program_id(0) * blocks + lane // (heads * MOBA_BLOCK)
    g = jnp.where(blk < own, jnp.concatenate(gates, axis=1), -jnp.inf)
    slots = []
    for s in range(MOBA_TOPK):
        best = jnp.max(g, axis=0, keepdims=True)
        idx = jnp.min(jnp.where(g == best, blk, N_BLOCKS), axis=0, keepdims=True)
        g = jnp.where(blk == idx, -jnp.inf, g)
        slots.append(jnp.where(s < own, idx, N_BLOCKS))

    onehots = []
    for t, (bb, a) in enumerate(tiles):
        sel_rows = [slot[:, t * MOBA_BLOCK:(t + 1) * MOBA_BLOCK] for slot in slots]
        sel_ref[a, :, rows_of(bb)] = jnp.concatenate(sel_rows, axis=0)
        onehots.append(_onehot_pairs(sel_rows))
    onehot = jnp.concatenate(onehots, axis=0)
    oh = onehot.astype(BF16)
    before = jnp.dot(oh, tri_ref[...], preferred_element_type=F32)
    hit = onehot * before
    ones = jnp.ones((8, n_pair), BF16)
    cnt = lax.dot_general(ones, oh, _NT, preferred_element_type=F32)
    for t, (bb, a) in enumerate(tiles):
        rank = jnp.sum(hit[t * N_BLOCKS:(t + 1) * N_BLOCKS], axis=0, keepdims=True).astype(I32)
        rank_ref[a, :, rows_of(bb)] = jnp.concatenate(
            [rank[:, s * MOBA_BLOCK:(s + 1) * MOBA_BLOCK] for s in range(MOBA_TOPK)], axis=0)
        cnt_ref[a, bb] = cnt[0:1, t * N_BLOCKS:(t + 1) * N_BLOCKS].astype(I32)

    qi = lax.broadcasted_iota(I32, (width, MOBA_BLOCK), 0) % MOBA_BLOCK
    ki = lax.broadcasted_iota(I32, (width, MOBA_BLOCK), 1)
    rows = _block_rows(qs, [kv_ref.at[a, bb] for bb, a in tiles], ki <= qi, None)
    for t, (bb, a) in enumerate(tiles):
        own_ref[a, rows_of(bb), :] = rows[t * MOBA_BLOCK:(t + 1) * MOBA_BLOCK]


def _gate_own_call(anchor, qrows, kv, kmean_h, tri, *, split, blocks=GATE_BLOCKS_PER_STEP):
    heads = SPLIT_HEADS
    rows = blocks * MOBA_BLOCK
    return pl.pallas_call(
        functools.partial(_gate_own_kernel, heads=heads, blocks=blocks),
        grid=(N_BLOCKS // blocks,),
        in_specs=[
            pl.BlockSpec(memory_space=pltpu.SMEM),
            pl.BlockSpec((heads, rows, HEAD_PAD), lambda b: (split, b, 0)),
            pl.BlockSpec((heads, blocks, SLAB_ROWS, MOBA_BLOCK), lambda b: (split, b, 0, 0)),
            pl.BlockSpec((heads, N_BLOCKS, HEAD_PAD), lambda b: (split, 0, 0)),
            _full((MOBA_TOPK * MOBA_BLOCK, MOBA_TOPK * MOBA_BLOCK)),
        ],
        out_specs=[
            pl.BlockSpec((heads, MOBA_TOPK, rows), lambda b: (0, 0, b)),
            pl.BlockSpec((heads, MOBA_TOPK, rows), lambda b: (0, 0, b)),
            pl.BlockSpec((heads, blocks, 1, N_BLOCKS), lambda b: (0, b, 0, 0)),
            pl.BlockSpec((heads, rows, HEAD_PAD), lambda b: (0, b, 0)),
        ],
        out_shape=[
            jax.ShapeDtypeStruct((heads, MOBA_TOPK, SEQ), I32),
            jax.ShapeDtypeStruct((heads, MOBA_TOPK, SEQ), I32),
            jax.ShapeDtypeStruct((heads, N_BLOCKS, 1, N_BLOCKS), I32),
            jax.ShapeDtypeStruct((heads, SEQ, HEAD_PAD), F32),
        ],
        compiler_params=_params(1),
        name=f"moba_gate_own_{split}",
    )(anchor, qrows, kv, kmean_h, tri)


def _route_pos_kernel(sel_ref, rank_ref, base_ref, pos_ref, *, blocks):
    lane = lax.broadcasted_iota(I32, (1, MOBA_BLOCK), 1)
    for bb in range(blocks):
        qs = slice(bb * MOBA_BLOCK, (bb + 1) * MOBA_BLOCK)
        rows = [[] for _ in range(MOBA_TOPK)]
        for a in range(SPLIT_HEADS):
            sel = sel_ref[a, :, qs]
            onehot = _onehot_pairs([sel[s:s + 1, :] for s in range(MOBA_TOPK)]).astype(BF16)
            dig = jnp.dot(base_ref[a, bb].astype(BF16), onehot,
                          preferred_element_type=F32)
            base = (dig[0:1] * float(DIGIT * DIGIT) + dig[1:2] * float(DIGIT)
                    + dig[2:3]).astype(I32)
            for s in range(MOBA_TOPK):
                p = base[:, s * MOBA_BLOCK:(s + 1) * MOBA_BLOCK] + rank_ref[a, s:s + 1, qs]
                rows[s].append(
                    jnp.where(sel[s:s + 1, :] < N_BLOCKS, p, TRASH_ROW0 + lane % SC_WINDOW))
        for s in range(MOBA_TOPK):
            pos_ref[s, :, qs] = jnp.concatenate(rows[s], axis=0)


def _route_pos_call(sel, rank, base_digits, *, blocks=8):
    width = blocks * MOBA_BLOCK
    return pl.pallas_call(
        functools.partial(_route_pos_kernel, blocks=blocks),
        grid=(N_BLOCKS // blocks,),
        in_specs=[
            pl.BlockSpec((SPLIT_HEADS, MOBA_TOPK, width), lambda b: (0, 0, b)),
            pl.BlockSpec((SPLIT_HEADS, MOBA_TOPK, width), lambda b: (0, 0, b)),
            pl.BlockSpec((SPLIT_HEADS, blocks, 8, N_BLOCKS), lambda b: (0, b, 0, 0)),
        ],
        out_specs=pl.BlockSpec((MOBA_TOPK, SPLIT_HEADS, width), lambda b: (0, 0, b)),
        out_shape=jax.ShapeDtypeStruct((MOBA_TOPK, SPLIT_HEADS, SEQ), I32),
        compiler_params=_params(1),
        name="moba_route_pos",
    )(sel, rank, base_digits)


def _route_tables(cnt, split):
    cnt = cnt.reshape(SPLIT_HEADS, N_BLOCKS, N_BLOCKS)
    tiles = (cnt.sum(axis=1) + ROUTE_TILE - 1) // ROUTE_TILE
    tiles_flat = tiles.reshape(N_GROUPS)
    tile0 = jnp.cumsum(tiles_flat) - tiles_flat
    n_tiles = tiles_flat.sum().astype(I32)
    within = jnp.cumsum(cnt, axis=1) - cnt
    base = tile0.reshape(SPLIT_HEADS, 1, N_BLOCKS) * ROUTE_TILE + within
    digits = jnp.stack([base // (DIGIT * DIGIT), (base // DIGIT) % DIGIT, base % DIGIT],
                       axis=2).astype(F32)
    digits = jnp.pad(digits, ((0, 0), (0, 0), (0, 8 - 3), (0, 0)))
    t = jnp.arange(MAX_TILES, dtype=I32)
    ended = (tile0 + tiles_flat)[None, :] <= t[:, None]
    group_of_tile = jnp.minimum(ended.sum(axis=1).astype(I32), N_GROUPS - 1)
    tile_h = group_of_tile // N_BLOCKS
    head = split * SPLIT_HEADS + tile_h
    tile_step = lax.bitcast_convert_type((127 + 7 - head) << 23, F32)
    return digits, tile_h, group_of_tile % N_BLOCKS, n_tiles.reshape(1), tile_step


def _sc_mesh():
    return plsc.VectorSubcoreMesh(core_axis_name="core", subcore_axis_name="subcore")


def _dispatch_call(rows, pos_by_slot, *, split):
    n_rows = SPLIT_HEADS * SEQ
    first_window = split * n_rows // SC_WINDOW

    @functools.partial(
        pl.kernel, mesh=_sc_mesh(), scratch_types=[],
        out_type=jax.ShapeDtypeStruct((ROUTE_ROWS, HEAD_PAD), F32))
    def dispatch(x_hbm, i0_hbm, i1_hbm, i2_hbm, o_hbm):
        def body(x_vmem, i0_vmem, i1_vmem, i2_vmem):
            pltpu.sync_copy(x_vmem, o_hbm.at[i0_vmem.at[0]])
            pltpu.sync_copy(x_vmem, o_hbm.at[i1_vmem.at[0]])
            pltpu.sync_copy(x_vmem, o_hbm.at[i2_vmem.at[0]])

        idx_spec = pl.BlockSpec((1, SC_WINDOW), lambda i: (0, i))
        pltpu.emit_pipeline(
            body,
            grid=(n_rows // SC_WINDOW,),
            in_specs=[pl.BlockSpec((SC_WINDOW, HEAD_PAD), lambda i: (first_window + i, 0)),
                      idx_spec, idx_spec, idx_spec],
            out_specs=[],
            core_axis_name=("core", "subcore"),
            dimension_semantics=(pltpu.PARALLEL,),
        )(x_hbm, i0_hbm, i1_hbm, i2_hbm)

    return dispatch(rows, *pos_by_slot)


def _collect_call(table, idx):
    n_rows = idx.shape[1]

    @functools.partial(
        pl.kernel, mesh=_sc_mesh(), scratch_types=[],
        out_type=jax.ShapeDtypeStruct((n_rows, HEAD_PAD), F32))
    def collect(x_hbm, i_hbm, o_hbm):
        def body(i_vmem, o_vmem):
            pltpu.sync_copy(x_hbm.at[i_vmem.at[0]], o_vmem)

        pltpu.emit_pipeline(
            body,
            grid=(n_rows // SC_WINDOW,),
            in_specs=[pl.BlockSpec((1, SC_WINDOW), lambda i: (0, i))],
            out_specs=[pl.BlockSpec((SC_WINDOW, HEAD_PAD), lambda i: (i, 0))],
            core_axis_name=("core", "subcore"),
            dimension_semantics=(pltpu.PARALLEL,),
        )(i_hbm, o_hbm)

    return collect(table, idx)


def _routed_attn_kernel(th_ref, tj_ref, nt_ref, step_ref, q_ref, *refs, tiles):
    kv_refs, o_ref = refs[0:tiles], refs[tiles]
    t0 = pl.program_id(0) * tiles

    @pl.when(t0 < nt_ref[0])
    def _():
        qs = [q_ref[u * ROUTE_TILE:(u + 1) * ROUTE_TILE, :].astype(BF16) for u in range(tiles)]
        adds = [step_ref[t0 + u] * tj_ref[t0 + u].astype(F32) for u in range(tiles)]
        o_ref[...] = _block_rows(qs, kv_refs, None, adds)


def _routed_attn_call(tile_h, tile_j, n_tiles, tile_step, routed_q, kv,
                      *, split, tiles=ROUTE_TILES_PER_STEP):
    live = lambda s, th, tj, nt: (jnp.minimum(s, (nt[0] + tiles - 1) // tiles), 0)
    head0 = split * SPLIT_HEADS

    def kv_spec(u):
        return pl.BlockSpec(
            (None, None, SLAB_ROWS, MOBA_BLOCK),
            lambda s, th, tj, nt: (head0 + th[s * tiles + u], tj[s * tiles + u], 0, 0))

    grid_spec = pltpu.PrefetchScalarGridSpec(
        num_scalar_prefetch=3,
        grid=(MAX_TILES // tiles,),
        in_specs=([pl.BlockSpec(memory_space=pltpu.SMEM),
                   pl.BlockSpec((tiles * ROUTE_TILE, HEAD_PAD), live)]
                  + [kv_spec(u) for u in range(tiles)]),
        out_specs=pl.BlockSpec((tiles * ROUTE_TILE, HEAD_PAD), live),
    )
    return pl.pallas_call(
        functools.partial(_routed_attn_kernel, tiles=tiles),
        grid_spec=grid_spec,
        out_shape=jax.ShapeDtypeStruct((ROUTE_ROWS, HEAD_PAD), F32),
        compiler_params=_params(1),
        name=f"moba_routed_attn_{split}",
    )(tile_h, tile_j, n_tiles, tile_step, routed_q, *([kv] * tiles))


def _pm_out_kernel(slope_ref, x_ref, mod_ref, *refs, blocks):
    own_refs, got_refs = refs[0:N_SPLITS], refs[N_SPLITS:2 * N_SPLITS]
    wa_ref, o_ref = refs[2 * N_SPLITS:]
    low = lax.broadcasted_iota(I32, (MOBA_BLOCK, HEAD_PAD), 1) < HEAD_DIM

    def lse_of(part):
        return jnp.where(low, pltpu.roll(part, HEAD_DIM, axis=1), part)

    y = jnp.zeros((blocks * MOBA_BLOCK, D_MODEL), F32)
    for hh in range(D_HEADS):
        own_ref, got_ref = own_refs[hh // SPLIT_HEADS], got_refs[hh // SPLIT_HEADS]
        a = hh % SPLIT_HEADS
        atts = []
        for bb in range(blocks):
            own = pl.program_id(0) * blocks + bb
            rows = slice(bb * MOBA_BLOCK, (bb + 1) * MOBA_BLOCK)
            own_shift = slope_ref[hh] * (own * MOBA_BLOCK).astype(F32)
            parts = [own_ref[a, rows, :]]
            lses = [lse_of(parts[0])]
            for s in range(MOBA_TOPK):
                valid = s < own
                part = got_ref[s, a, rows, :]
                parts.append(jnp.where(valid, part, 0.0))
                lses.append(jnp.where(valid, lse_of(part) - own_shift, NEG_BIG))
            top = functools.reduce(jnp.maximum, lses)
            num = jnp.zeros((MOBA_BLOCK, HEAD_PAD), F32)
            den = jnp.zeros((MOBA_BLOCK, HEAD_PAD), F32)
            for part, ls in zip(parts, lses):
                w = jnp.exp(ls - top)
                num = num + w * part
                den = den + w
            atts.append((num / den).astype(BF16))
        y = y + jnp.dot(jnp.concatenate(atts, axis=0), wa_ref[hh], preferred_element_type=F32)
    o_ref[...] = x_ref[...] + mod_ref[2:3, :] * y


def _pm_out_call(slopes, x, mod, own_parts, got_parts, wa, *, layer, blocks=2):
    tm = blocks * MOBA_BLOCK
    return pl.pallas_call(
        functools.partial(_pm_out_kernel, blocks=blocks),
        grid=(SEQ // tm,),
        in_specs=[
            pl.BlockSpec(memory_space=pltpu.SMEM),
            pl.BlockSpec((tm, D_MODEL), lambda i: (i, 0)),
            _resident((None, None, 3, D_MODEL), lambda i: (layer, 1, 0, 0)),
        ] + [pl.BlockSpec((SPLIT_HEADS, tm, HEAD_PAD), lambda i: (0, i, 0))] * N_SPLITS
        + [pl.BlockSpec((MOBA_TOPK, SPLIT_HEADS, tm, HEAD_PAD), lambda i: (0, 0, i, 0))] * N_SPLITS
        + [_full((D_HEADS, HEAD_PAD, D_MODEL))],
        out_specs=pl.BlockSpec((tm, D_MODEL), lambda i: (i, 0)),
        out_shape=jax.ShapeDtypeStruct((SEQ, D_MODEL), F32),
        compiler_params=_params(1),
        name="moba_merge_out",
    )(slopes, x, mod, *own_parts, *got_parts, wa)


def _head_pad(w, axis):
    shape = list(w.shape)
    shape[axis:axis + 1] = [D_HEADS, HEAD_DIM]
    pad = [(0, 0)] * len(shape)
    pad[axis + 1] = (0, HEAD_PAD - HEAD_DIM)
    out = jnp.pad(w.reshape(shape), pad)
    shape[axis:axis + 2] = [D_HEADS * HEAD_PAD]
    return out.reshape(shape)


def kernel(x, c, ada_w, ada_b, ffn_norm, ffn_w_gate, ffn_w_up, ffn_w_down, mix_norm, conv_w_in,
           conv_a_w, conv_a_b, conv_a_ln_g, conv_a_ln_b, conv_b_w, conv_w_out, pm_w_in, pool_w,
           pool_b, pool_scale, pm_w_out, final_norm):
    assert x.shape == (1, SEQ, D_MODEL) and c.shape == (1, D_MODEL)
    xs = x.reshape(SEQ, D_MODEL)
    mod = _mod_call(c, ada_w, ada_b)

    wg = ffn_w_gate.astype(BF16)
    wu = ffn_w_up.astype(BF16)
    wd = ffn_w_down.astype(BF16)
    ffn_g = ffn_norm.reshape(DEPTH, 2, 1, D_MODEL)
    mix_g = mix_norm.reshape(DEPTH, 1, D_MODEL)
    fin = final_norm.reshape(1, D_MODEL)
    ffn = functools.partial(_ffn_call, mod=mod, norm_g=ffn_g, wg=wg, wu=wu, wd=wd, fin=fin)

    xs = ffn(xs, layer=0, sub=0, final=False)
    grp = np.arange(A_WIDTH) // (A_WIDTH // A_GROUPS)
    seg = jnp.asarray((grp[:, None] == grp[None, :]) / (A_WIDTH // A_GROUPS), dtype=BF16)
    xs = _conv_mix_call(
        xs, mod, mix_g, conv_w_in[0].astype(BF16), conv_a_w[0], conv_a_b[0].reshape(1, A_WIDTH),
        conv_a_ln_g[0].reshape(1, A_WIDTH), conv_a_ln_b[0].reshape(1, A_WIDTH), conv_b_w[0],
        conv_w_out[0].astype(BF16), seg, layer=0)
    xs = ffn(xs, layer=0, sub=1, final=False)

    xs = ffn(xs, layer=1, sub=0, final=False)
    w_in = pm_w_in[0].astype(BF16)
    w_u = w_in[:, 0:C_WIDTH]
    w_q = w_in[:, C_WIDTH:C_WIDTH + D_WIDTH]
    w_kvt = w_in[:, C_WIDTH + D_WIDTH:C_WIDTH + 3 * D_WIDTH].T
    qrows, kv, kmean = _pm_proj_call(xs, mod, mix_g, w_q, w_kvt, layer=1)
    kmean_h = _head_pad(kmean, 1).reshape(N_BLOCKS, D_HEADS, HEAD_PAD).transpose(1, 0, 2)

    n_pair = MOBA_TOPK * MOBA_BLOCK
    tri = jnp.asarray(np.arange(n_pair)[:, None] < np.arange(n_pair)[None, :], dtype=BF16)
    q_flat = qrows.reshape(D_HEADS * SEQ, HEAD_PAD)
    own_parts, got_parts = [], []
    anchor = jnp.zeros((1,), I32)
    for split in range(N_SPLITS):
        sel, rank, cnt, own_part = _gate_own_call(anchor, qrows, kv, kmean_h, tri, split=split)
        base_digits, tile_h, tile_j, n_tiles, tile_step = _route_tables(cnt, split)
        pos = _route_pos_call(sel, rank, base_digits)
        anchor = pos[0, 0, 0:1]
        pos_by_slot = [pos[s].reshape(1, SPLIT_HEADS * SEQ) for s in range(MOBA_TOPK)]
        routed_q = _dispatch_call(q_flat, pos_by_slot, split=split)
        routed_parts = _routed_attn_call(tile_h, tile_j, n_tiles, tile_step, routed_q, kv,
                                         split=split)
        got = _collect_call(routed_parts, pos.reshape(1, N_PAIRS))
        own_parts.append(own_part)
        got_parts.append(got.reshape(MOBA_TOPK, SPLIT_HEADS, SEQ, HEAD_PAD))

    w_out = pm_w_out[0].astype(BF16)
    xs = _pool_call(routed_parts[0, 0:1], xs, mod, mix_g, w_u, pool_w[0].astype(BF16),
                    pool_b[0].reshape(1, C_WIDTH), pool_scale[0].reshape(1, C_WIDTH),
                    w_out[0:C_WIDTH], layer=1)
    slopes = jnp.asarray(2.0 ** (-8.0 * np.arange(1, D_HEADS + 1) / D_HEADS), dtype=F32)
    w_att = _head_pad(w_out[C_WIDTH:], 0).reshape(D_HEADS, HEAD_PAD, D_MODEL)
    xs = _pm_out_call(slopes, xs, mod, own_parts, got_parts, w_att, layer=1)
    xs = ffn(xs, layer=1, sub=1, final=True)
    return xs.reshape(1, SEQ, D_MODEL)
```

```python
import functools

import numpy as np
import jax
import jax.numpy as jnp
from jax import lax
from jax.experimental import pallas as pl
from jax.experimental.pallas import tpu as pltpu
from jax.experimental.pallas import tpu_sc as plsc

D_MODEL = 1024
SEQ = 16384
DEPTH = 2
N_SUBLAYERS = 3
D_FF = 2816
EPS = 1e-6

A_WIDTH = 512
A_GROUPS = 8
A_CONV = 31
B_WIDTH = 512
B_CONV = 3

POOL_WINDOWS = (2, 4, 8, 16)
C_WIDTH = 512
C_GROUP_DIM = 128
D_HEADS = 8
HEAD_DIM = 64
D_WIDTH = 512
MOBA_BLOCK = 256
MOBA_TOPK = 3
N_BLOCKS = SEQ // MOBA_BLOCK

V7X_SUBLANES = 8
V7X_BF16_SUBLANES = 16
V7X_VMEM_LIMIT_BYTES = 56 * 1024 * 1024

A_HALO = 32
B_HALO = 8
POOL_HALO = 16
NEG_BIG = -1e30

HEAD_PAD = 128
ALIBI_COL = HEAD_DIM
SLAB_K_ROWS = HEAD_DIM + V7X_BF16_SUBLANES
SLAB_ROWS = SLAB_K_ROWS + HEAD_DIM

SPLIT_HEADS = 4
N_SPLITS = D_HEADS // SPLIT_HEADS
ROUTE_TILE = 256
ROUTE_TILES_PER_STEP = 32
GATE_BLOCKS_PER_STEP = 4
N_PAIRS = SPLIT_HEADS * SEQ * MOBA_TOPK
N_GROUPS = SPLIT_HEADS * N_BLOCKS
MAX_TILES = N_PAIRS // ROUTE_TILE + N_GROUPS
TRASH_ROW0 = MAX_TILES * ROUTE_TILE
ROUTE_ROWS = (MAX_TILES + ROUTE_TILES_PER_STEP) * ROUTE_TILE
SC_WINDOW = 128
DIGIT = 128
assert MAX_TILES % ROUTE_TILES_PER_STEP == 0 and N_BLOCKS % GATE_BLOCKS_PER_STEP == 0
assert ROUTE_ROWS <= DIGIT ** 3 and MOBA_BLOCK <= ROUTE_TILE

BF16 = jnp.bfloat16
F32 = jnp.float32
I32 = jnp.int32


def _params(n_axes):
    return pltpu.CompilerParams(
        dimension_semantics=("arbitrary",) * n_axes,
        vmem_limit_bytes=V7X_VMEM_LIMIT_BYTES)


def _resident(block_shape, index_map):
    return pl.BlockSpec(block_shape, index_map, pipeline_mode=pl.Buffered(1))


def _full(shape):
    return _resident(shape, lambda *_: (0,) * len(shape))


def _sigmoid(v):
    return 1.0 / (1.0 + jnp.exp(-v))


def _norm_mod(x, g, mod):
    r = lax.rsqrt(jnp.mean(x * x, axis=-1, keepdims=True) + EPS)
    return (x * r) * (g * (1.0 + mod[1:2])) + mod[0:1]


def _split(v):
    hi = v.astype(BF16)
    return hi, (v - hi.astype(F32)).astype(BF16)


def _split_dot(v, w):
    hi, lo = _split(v)
    return (jnp.dot(hi, w, preferred_element_type=F32)
            + jnp.dot(lo, w, preferred_element_type=F32))


_NT = (((1,), (1,)), ((), ()))


def _mod_kernel(c_ref, w_ref, b_ref, o_ref):
    c = c_ref[...]
    cond = c * _sigmoid(c)
    o_ref[0] = jnp.sum(w_ref[0] * cond, axis=0, keepdims=True) + b_ref[0]


def _mod_call(c, ada_w, ada_b):
    n_out = N_SUBLAYERS * 3 * D_MODEL
    tn = D_MODEL
    out = pl.pallas_call(
        _mod_kernel,
        grid=(DEPTH, n_out // tn),
        in_specs=[
            pl.BlockSpec((D_MODEL, 1), lambda l, j: (0, 0)),
            pl.BlockSpec((1, D_MODEL, tn), lambda l, j: (l, 0, j)),
            pl.BlockSpec((1, 1, tn), lambda l, j: (l, 0, j)),
        ],
        out_specs=pl.BlockSpec((1, 1, tn), lambda l, j: (l, 0, j)),
        out_shape=jax.ShapeDtypeStruct((DEPTH, 1, n_out), F32),
        compiler_params=_params(2),
        name="adaln_mod",
    )(c.reshape(D_MODEL, 1), ada_w, ada_b.reshape(DEPTH, 1, n_out))
    return out.reshape(DEPTH, N_SUBLAYERS, 3, D_MODEL)


def _ffn_kernel(x_ref, mod_ref, g_ref, wg_ref, wu_ref, wd_ref, fin_ref, o_ref, *, final):
    x = x_ref[...]
    mod = mod_ref[...]
    h = _norm_mod(x, g_ref[...], mod).astype(BF16)
    gate = jnp.dot(h, wg_ref[...], preferred_element_type=F32)
    up = jnp.dot(h, wu_ref[...], preferred_element_type=F32)
    act = (gate * _sigmoid(gate) * up).astype(BF16)
    y = jnp.dot(act, wd_ref[...], preferred_element_type=F32)
    xn = x + (0.5 * mod[2:3]) * y
    if final:
        r = lax.rsqrt(jnp.mean(xn * xn, axis=-1, keepdims=True) + EPS)
        xn = xn * r * fin_ref[...]
    o_ref[...] = xn


def _ffn_call(x, mod, norm_g, wg, wu, wd, fin, *, layer, sub, final, tm=512):
    mod_sub = 0 if sub == 0 else 2
    return pl.pallas_call(
        functools.partial(_ffn_kernel, final=final),
        grid=(SEQ // tm,),
        in_specs=[
            pl.BlockSpec((tm, D_MODEL), lambda i: (i, 0)),
            _resident((None, None, 3, D_MODEL), lambda i: (layer, mod_sub, 0, 0)),
            _resident((None, None, 1, D_MODEL), lambda i: (layer, sub, 0, 0)),
            _resident((None, None, D_MODEL, D_FF), lambda i: (layer, sub, 0, 0)),
            _resident((None, None, D_MODEL, D_FF), lambda i: (layer, sub, 0, 0)),
            _resident((None, None, D_FF, D_MODEL), lambda i: (layer, sub, 0, 0)),
            _resident((1, D_MODEL), lambda i: (0, 0)),
        ],
        out_specs=pl.BlockSpec((tm, D_MODEL), lambda i: (i, 0)),
        out_shape=jax.ShapeDtypeStruct((SEQ, D_MODEL), F32),
        compiler_params=_params(1),
        name=f"ffn_l{layer}s{sub}",
    )(x, mod, norm_g, wg, wu, wd, fin)


def _conv_mix_kernel(x_ref, mod_ref, g_ref, win_ref, caw_ref, cab_ref, lng_ref, lnb_ref,
                     cbw_ref, wout_ref, seg_ref, o_ref, abuf, bbuf, sbuf, *, tm):
    i = pl.program_id(0)

    @pl.when(i == 0)
    def _():
        abuf[0:A_HALO, :] = jnp.zeros((A_HALO, A_WIDTH), F32)
        bbuf[0:B_HALO, :] = jnp.zeros((B_HALO, B_WIDTH), F32)

    x = x_ref[...]
    mod = mod_ref[...]
    h = _norm_mod(x, g_ref[...], mod).astype(BF16)
    z = jnp.dot(h, win_ref[...], preferred_element_type=F32)

    a = z[:, 0:A_WIDTH] * _sigmoid(z[:, A_WIDTH:2 * A_WIDTH])
    abuf[A_HALO:A_HALO + tm, :] = a
    acc = jnp.zeros((tm, A_WIDTH), F32) + cab_ref[...]
    ext = tm + V7X_SUBLANES
    for r in range(V7X_SUBLANES):
        part = None
        for q in range((A_CONV - 1 - r) // V7X_SUBLANES + 1):
            k = A_CONV - 1 - (V7X_SUBLANES * q + r)
            term = caw_ref[k:k + 1, :] * abuf[pl.ds(A_HALO - V7X_SUBLANES * (q + 1), ext), :]
            part = term if part is None else part + term
        if r == 0:
            acc = acc + part[V7X_SUBLANES:, :]
        else:
            sbuf[r - 1] = part
            acc = acc + sbuf[r - 1, pl.ds(V7X_SUBLANES - r, tm), :]
    abuf[0:A_HALO, :] = abuf[tm:tm + A_HALO, :]
    seg = seg_ref[...]
    mu = _split_dot(acc, seg)
    d = acc - mu
    var = _split_dot(d * d, seg)
    yn = d * lax.rsqrt(var + EPS) * lng_ref[...] + lnb_ref[...]
    a_out = yn * _sigmoid(yn)

    off = 2 * A_WIDTH
    cv = z[:, off + B_WIDTH:off + 2 * B_WIDTH] * z[:, off + 2 * B_WIDTH:off + 3 * B_WIDTH]
    bbuf[B_HALO:B_HALO + tm, :] = cv
    bacc = jnp.zeros((tm, B_WIDTH), F32)
    for k in range(B_CONV):
        bacc = bacc + cbw_ref[k:k + 1, :] * bbuf[pl.ds(B_HALO - (B_CONV - 1) + k, tm), :]
    bbuf[0:B_HALO, :] = bbuf[tm:tm + B_HALO, :]
    bb = z[:, off:off + B_WIDTH] * bacc

    y = (jnp.dot(a_out.astype(BF16), wout_ref[0:A_WIDTH, :], preferred_element_type=F32)
         + jnp.dot(bb.astype(BF16), wout_ref[A_WIDTH:A_WIDTH + B_WIDTH, :],
                   preferred_element_type=F32))
    o_ref[...] = x + mod[2:3] * y


def _conv_mix_call(x, mod, norm_g, w_in, caw, cab, lng, lnb, cbw, w_out, seg, *, layer, tm=512):
    even_in = w_in.shape[1]
    return pl.pallas_call(
        functools.partial(_conv_mix_kernel, tm=tm),
        grid=(SEQ // tm,),
        in_specs=[
            pl.BlockSpec((tm, D_MODEL), lambda i: (i, 0)),
            _resident((None, None, 3, D_MODEL), lambda i: (layer, 1, 0, 0)),
            _resident((None, 1, D_MODEL), lambda i: (layer, 0, 0)),
            _full((D_MODEL, even_in)),
            _full((A_CONV, A_WIDTH)),
            _full((1, A_WIDTH)),
            _full((1, A_WIDTH)),
            _full((1, A_WIDTH)),
            _full((B_CONV, B_WIDTH)),
            _full((A_WIDTH + B_WIDTH, D_MODEL)),
            _full((A_WIDTH, A_WIDTH)),
        ],
        out_specs=pl.BlockSpec((tm, D_MODEL), lambda i: (i, 0)),
        out_shape=jax.ShapeDtypeStruct((SEQ, D_MODEL), F32),
        scratch_shapes=[pltpu.VMEM((A_HALO + tm, A_WIDTH), F32),
                        pltpu.VMEM((B_HALO + tm, B_WIDTH), F32),
                        pltpu.VMEM((V7X_SUBLANES - 1, tm + V7X_SUBLANES, A_WIDTH), F32)],
        compiler_params=_params(1),
        name="conv_mixers",
    )(x, mod, norm_g, w_in, caw, cab, lng, lnb, cbw, w_out, seg)


def _pm_proj_kernel(x_ref, mod_ref, g_ref, wu_ref, wq_ref, wkvt_ref, pw_ref,
                    pb_ref, ps_ref, pooled_ref, qrow_ref, kv_ref, km_ref,
                    ubuf, hmean, *, tm):
    i = pl.program_id(0)

    @pl.when(i == 0)
    def _():
        ubuf[0:POOL_HALO, :] = jnp.zeros((POOL_HALO, C_WIDTH), F32)

    x = x_ref[...]
    h = _norm_mod(x, g_ref[...], mod_ref[...]).astype(BF16)
    u = jnp.dot(h, wu_ref[...], preferred_element_type=F32)
    qr = jnp.dot(h, wq_ref[...], preferred_element_type=F32)
    kvt = lax.dot_general(wkvt_ref[...], h, _NT, preferred_element_type=F32)

    ubuf[POOL_HALO:POOL_HALO + tm, :] = u
    t1 = (i * tm + 1 + lax.broadcasted_iota(I32, (tm, C_GROUP_DIM), 0)).astype(F32)
    for gi, w in enumerate(POOL_WINDOWS):
        c0 = gi * C_GROUP_DIM
        s = u[:, c0:c0 + C_GROUP_DIM]
        for k in range(1, w):
            s = s + ubuf[pl.ds(POOL_HALO - k, tm), c0:c0 + C_GROUP_DIM]
        pooled = s / jnp.minimum(t1, float(w)) - u[:, c0:c0 + C_GROUP_DIM]
        mixed = (jnp.dot(pooled.astype(BF16), pw_ref[gi], preferred_element_type=F32)
                 + pb_ref[:, c0:c0 + C_GROUP_DIM])
        pooled_ref[:, c0:c0 + C_GROUP_DIM] = (
            mixed * ps_ref[:, c0:c0 + C_GROUP_DIM]).astype(BF16)
    ubuf[0:POOL_HALO, :] = ubuf[tm:tm + POOL_HALO, :]

    lane = lax.broadcasted_iota(I32, (tm, HEAD_PAD), 1)
    tail = jnp.where(lane == ALIBI_COL, 1.0, 0.0)
    qs = qr * (HEAD_DIM ** -0.5)
    for pair in range(D_HEADS // 2):
        both = qs[:, pair * HEAD_PAD:(pair + 1) * HEAD_PAD]
        qrow_ref[2 * pair] = jnp.where(lane < HEAD_DIM, both, tail)
        qrow_ref[2 * pair + 1] = jnp.where(
            lane < HEAD_DIM, pltpu.roll(both, HEAD_DIM, axis=1), tail)
    per_tile = tm // MOBA_BLOCK
    kvb = kvt.astype(BF16)
    n_alibi = SLAB_K_ROWS - HEAD_DIM
    first = lax.broadcasted_iota(I32, (n_alibi, MOBA_BLOCK), 0) == 0
    koff = lax.broadcasted_iota(I32, (n_alibi, MOBA_BLOCK), 1).astype(F32)
    for hh in range(D_HEADS):
        alibi = jnp.where(first, koff * (2.0 ** -(hh + 1)), 0.0).astype(BF16)
        for sb in range(per_tile):
            keys = slice(sb * MOBA_BLOCK, (sb + 1) * MOBA_BLOCK)
            kv_ref[hh, sb, 0:HEAD_DIM, :] = kvb[hh * HEAD_DIM:(hh + 1) * HEAD_DIM, keys]
            kv_ref[hh, sb, HEAD_DIM:SLAB_K_ROWS, :] = alibi
            kv_ref[hh, sb, SLAB_K_ROWS:SLAB_ROWS, :] = kvb[
                D_WIDTH + hh * HEAD_DIM:D_WIDTH + (hh + 1) * HEAD_DIM, keys]

    for sb in range(per_tile):
        blk = h[sb * MOBA_BLOCK:(sb + 1) * MOBA_BLOCK, :].astype(F32)
        hmean[pl.ds(i * per_tile + sb, 1), :] = jnp.mean(blk, axis=0, keepdims=True)

    @pl.when(i == pl.num_programs(0) - 1)
    def _():
        hi, lo = _split(hmean[...])
        wkt = wkvt_ref[0:D_WIDTH, :]
        km_ref[...] = (lax.dot_general(hi, wkt, _NT, preferred_element_type=F32)
                       + lax.dot_general(lo, wkt, _NT, preferred_element_type=F32))


def _pm_proj_call(x, mod, norm_g, wu, wq, wkvt, pw, pb, ps, *, layer, tm=512):
    return pl.pallas_call(
        functools.partial(_pm_proj_kernel, tm=tm),
        grid=(SEQ // tm,),
        in_specs=[
            pl.BlockSpec((tm, D_MODEL), lambda i: (i, 0)),
            _resident((None, None, 3, D_MODEL), lambda i: (layer, 1, 0, 0)),
            _resident((None, 1, D_MODEL), lambda i: (layer, 0, 0)),
            _full((D_MODEL, C_WIDTH)),
            _full((D_MODEL, D_WIDTH)),
            _full((2 * D_WIDTH, D_MODEL)),
            _full((len(POOL_WINDOWS), C_GROUP_DIM, C_GROUP_DIM)),
            _full((1, C_WIDTH)),
            _full((1, C_WIDTH)),
        ],
        out_specs=[
            pl.BlockSpec((tm, C_WIDTH), lambda i: (i, 0)),
            pl.BlockSpec((D_HEADS, tm, HEAD_PAD), lambda i: (0, i, 0)),
            pl.BlockSpec((D_HEADS, tm // MOBA_BLOCK, SLAB_ROWS, MOBA_BLOCK),
                         lambda i: (0, i, 0, 0)),
            pl.BlockSpec((N_BLOCKS, D_WIDTH), lambda i: (0, 0)),
        ],
        out_shape=[
            jax.ShapeDtypeStruct((SEQ, C_WIDTH), BF16),
            jax.ShapeDtypeStruct((D_HEADS, SEQ, HEAD_PAD), F32),
            jax.ShapeDtypeStruct((D_HEADS, N_BLOCKS, SLAB_ROWS, MOBA_BLOCK), BF16),
            jax.ShapeDtypeStruct((N_BLOCKS, D_WIDTH), F32),
        ],
        scratch_shapes=[pltpu.VMEM((POOL_HALO + tm, C_WIDTH), F32),
                        pltpu.VMEM((N_BLOCKS, D_MODEL), F32)],
        compiler_params=_params(1),
        name="pool_qkv_proj",
    )(x, mod, norm_g, wu, wq, wkvt, pw, pb, ps)


def _block_rows(qs, slabs, mask, lse_add):
    n = qs[0].shape[0]
    kpad = jnp.zeros((HEAD_PAD - SLAB_K_ROWS, MOBA_BLOCK), BF16)
    s = jnp.concatenate(
        [jnp.dot(q, jnp.concatenate([slab[0:SLAB_K_ROWS, :], kpad], axis=0),
                 preferred_element_type=F32)
         for q, slab in zip(qs, slabs)], axis=0)
    if mask is not None:
        s = jnp.where(mask, s, NEG_BIG)
    m = jnp.max(s, axis=1, keepdims=True)
    p = jnp.exp(s - m).astype(BF16)
    vpad = jnp.zeros((HEAD_PAD - HEAD_DIM, MOBA_BLOCK), BF16)
    ones = jnp.ones((HEAD_PAD, MOBA_BLOCK), BF16)
    ot = jnp.concatenate(
        [lax.dot_general(p[u * n:(u + 1) * n],
                         jnp.concatenate([slab[SLAB_K_ROWS:SLAB_ROWS, :], vpad, ones], axis=0),
                         _NT, preferred_element_type=F32)
         for u, slab in enumerate(slabs)], axis=0)
    o, tot = ot[:, 0:HEAD_PAD], ot[:, HEAD_PAD:2 * HEAD_PAD]
    lse = m + jnp.log(tot)
    if lse_add is not None:
        lse = jnp.concatenate(
            [lse[u * n:(u + 1) * n] + add for u, add in enumerate(lse_add)], axis=0)
    low = lax.broadcasted_iota(I32, o.shape, 1) < HEAD_DIM
    return jnp.where(low, o / tot, lse)


def _onehot_pairs(sel_rows):
    blk = lax.broadcasted_iota(I32, (N_BLOCKS, MOBA_BLOCK), 0)
    return jnp.concatenate(
        [jnp.where(blk == sel_rows[s], 1.0, 0.0) for s in range(MOBA_TOPK)], axis=1)


def _gate_own_kernel(anchor_ref, qrow_ref, kv_ref, km_ref, tri_ref,
                     sel_ref, rank_ref, cnt_ref, own_ref, *, heads, blocks):
    del anchor_ref
    n_pair = MOBA_TOPK * MOBA_BLOCK
    tiles = [(bb, a) for bb in range(blocks) for a in range(heads)]
    rows_of = lambda bb: slice(bb * MOBA_BLOCK, (bb + 1) * MOBA_BLOCK)
    qs = [qrow_ref[a, rows_of(bb), :].astype(BF16) for bb, a in tiles]

    kms = [_split(km_ref[a]) for a in range(heads)]
    gates = [lax.dot_general(kms[a][0], q, _NT, preferred_element_type=F32)
             + lax.dot_general(kms[a][1], q, _NT, preferred_element_type=F32)
             for (bb, a), q in zip(tiles, qs)]
    width = len(tiles) * MOBA_BLOCK
    blk = lax.broadcasted_iota(I32, (N_BLOCKS, width), 0)
    lane = lax.broadcasted_iota(I32, (1, width), 1)
    own = pl.program_id(0) * blocks + lane // (heads * MOBA_BLOCK)
    g = jnp.where(blk < own, jnp.concatenate(gates, axis=1), -jnp.inf)
    slots = []
    for s in range(MOBA_TOPK):
        best = jnp.max(g, axis=0, keepdims=True)
        idx = jnp.min(jnp.where(g == best, blk, N_BLOCKS), axis=0, keepdims=True)
        g = jnp.where(blk == idx, -jnp.inf, g)
        slots.append(jnp.where(s < own, idx, N_BLOCKS))

    onehots = []
    for t, (bb, a) in enumerate(tiles):
        sel_rows = [slot[:, t * MOBA_BLOCK:(t + 1) * MOBA_BLOCK] for slot in slots]
        sel_ref[a, :, rows_of(bb)] = jnp.concatenate(sel_rows, axis=0)
        onehots.append(_onehot_pairs(sel_rows))
    onehot = jnp.concatenate(onehots, axis=0)
    oh = onehot.astype(BF16)
    before = jnp.dot(oh, tri_ref[...], preferred_element_type=F32)
    hit = onehot * before
    ones = jnp.ones((8, n_pair), BF16)
    cnt = lax.dot_general(ones, oh, _NT, preferred_element_type=F32)
    for t, (bb, a) in enumerate(tiles):
        rank = jnp.sum(hit[t * N_BLOCKS:(t + 1) * N_BLOCKS], axis=0, keepdims=True).astype(I32)
        rank_ref[a, :, rows_of(bb)] = jnp.concatenate(
            [rank[:, s * MOBA_BLOCK:(s + 1) * MOBA_BLOCK] for s in range(MOBA_TOPK)], axis=0)
        cnt_ref[a, bb] = cnt[0:1, t * N_BLOCKS:(t + 1) * N_BLOCKS].astype(I32)

    qi = lax.broadcasted_iota(I32, (width, MOBA_BLOCK), 0) % MOBA_BLOCK
    ki = lax.broadcasted_iota(I32, (width, MOBA_BLOCK), 1)
    rows = _block_rows(qs, [kv_ref.at[a, bb] for bb, a in tiles], ki <= qi, None)
    for t, (bb, a) in enumerate(tiles):
        own_ref[a, rows_of(bb), :] = rows[t * MOBA_BLOCK:(t + 1) * MOBA_BLOCK]


def _gate_own_call(anchor, qrows, kv, kmean_h, tri, *, split, blocks=GATE_BLOCKS_PER_STEP):
    heads = SPLIT_HEADS
    rows = blocks * MOBA_BLOCK
    return pl.pallas_call(
        functools.partial(_gate_own_kernel, heads=heads, blocks=blocks),
        grid=(N_BLOCKS // blocks,),
        in_specs=[
            pl.BlockSpec(memory_space=pltpu.SMEM),
            pl.BlockSpec((heads, rows, HEAD_PAD), lambda b: (split, b, 0)),
            pl.BlockSpec((heads, blocks, SLAB_ROWS, MOBA_BLOCK), lambda b: (split, b, 0, 0)),
            pl.BlockSpec((heads, N_BLOCKS, HEAD_PAD), lambda b: (split, 0, 0)),
            _full((MOBA_TOPK * MOBA_BLOCK, MOBA_TOPK * MOBA_BLOCK)),
        ],
        out_specs=[
            pl.BlockSpec((heads, MOBA_TOPK, rows), lambda b: (0, 0, b)),
            pl.BlockSpec((heads, MOBA_TOPK, rows), lambda b: (0, 0, b)),
            pl.BlockSpec((heads, blocks, 1, N_BLOCKS), lambda b: (0, b, 0, 0)),
            pl.BlockSpec((heads, rows, HEAD_PAD), lambda b: (0, b, 0)),
        ],
        out_shape=[
            jax.ShapeDtypeStruct((heads, MOBA_TOPK, SEQ), I32),
            jax.ShapeDtypeStruct((heads, MOBA_TOPK, SEQ), I32),
            jax.ShapeDtypeStruct((heads, N_BLOCKS, 1, N_BLOCKS), I32),
            jax.ShapeDtypeStruct((heads, SEQ, HEAD_PAD), F32),
        ],
        compiler_params=_params(1),
        name=f"moba_gate_own_{split}",
    )(anchor, qrows, kv, kmean_h, tri)


def _route_pos_kernel(sel_ref, rank_ref, base_ref, pos_ref, *, blocks):
    lane = lax.broadcasted_iota(I32, (1, MOBA_BLOCK), 1)
    for bb in range(blocks):
        qs = slice(bb * MOBA_BLOCK, (bb + 1) * MOBA_BLOCK)
        rows = [[] for _ in range(MOBA_TOPK)]
        for a in range(SPLIT_HEADS):
            sel = sel_ref[a, :, qs]
            onehot = _onehot_pairs([sel[s:s + 1, :] for s in range(MOBA_TOPK)]).astype(BF16)
            dig = jnp.dot(base_ref[a, bb].astype(BF16), onehot,
                          preferred_element_type=F32)
            base = (dig[0:1] * float(DIGIT * DIGIT) + dig[1:2] * float(DIGIT)
                    + dig[2:3]).astype(I32)
            for s in range(MOBA_TOPK):
                p = base[:, s * MOBA_BLOCK:(s + 1) * MOBA_BLOCK] + rank_ref[a, s:s + 1, qs]
                rows[s].append(
                    jnp.where(sel[s:s + 1, :] < N_BLOCKS, p, TRASH_ROW0 + lane % SC_WINDOW))
        for s in range(MOBA_TOPK):
            pos_ref[s, :, qs] = jnp.concatenate(rows[s], axis=0)


def _route_pos_call(sel, rank, base_digits, *, blocks=8):
    width = blocks * MOBA_BLOCK
    return pl.pallas_call(
        functools.partial(_route_pos_kernel, blocks=blocks),
        grid=(N_BLOCKS // blocks,),
        in_specs=[
            pl.BlockSpec((SPLIT_HEADS, MOBA_TOPK, width), lambda b: (0, 0, b)),
            pl.BlockSpec((SPLIT_HEADS, MOBA_TOPK, width), lambda b: (0, 0, b)),
            pl.BlockSpec((SPLIT_HEADS, blocks, 8, N_BLOCKS), lambda b: (0, b, 0, 0)),
        ],
        out_specs=pl.BlockSpec((MOBA_TOPK, SPLIT_HEADS, width), lambda b: (0, 0, b)),
        out_shape=jax.ShapeDtypeStruct((MOBA_TOPK, SPLIT_HEADS, SEQ), I32),
        compiler_params=_params(1),
        name="moba_route_pos",
    )(sel, rank, base_digits)


def _route_tables(cnt, split):
    cnt = cnt.reshape(SPLIT_HEADS, N_BLOCKS, N_BLOCKS)
    tiles = (cnt.sum(axis=1) + ROUTE_TILE - 1) // ROUTE_TILE
    tiles_flat = tiles.reshape(N_GROUPS)
    tile0 = jnp.cumsum(tiles_flat) - tiles_flat
    n_tiles = tiles_flat.sum().astype(I32)
    within = jnp.cumsum(cnt, axis=1) - cnt
    base = tile0.reshape(SPLIT_HEADS, 1, N_BLOCKS) * ROUTE_TILE + within
    digits = jnp.stack([base // (DIGIT * DIGIT), (base // DIGIT) % DIGIT, base % DIGIT],
                       axis=2).astype(F32)
    digits = jnp.pad(digits, ((0, 0), (0, 0), (0, 8 - 3), (0, 0)))
    t = jnp.arange(MAX_TILES, dtype=I32)
    ended = (tile0 + tiles_flat)[None, :] <= t[:, None]
    group_of_tile = jnp.minimum(ended.sum(axis=1).astype(I32), N_GROUPS - 1)
    tile_h = group_of_tile // N_BLOCKS
    head = split * SPLIT_HEADS + tile_h
    tile_step = lax.bitcast_convert_type((127 + 7 - head) << 23, F32)
    return digits, tile_h, group_of_tile % N_BLOCKS, n_tiles.reshape(1), tile_step


def _sc_mesh():
    return plsc.VectorSubcoreMesh(core_axis_name="core", subcore_axis_name="subcore")


def _dispatch_call(rows, pos_by_slot, *, split):
    n_rows = SPLIT_HEADS * SEQ
    first_window = split * n_rows // SC_WINDOW

    @functools.partial(
        pl.kernel, mesh=_sc_mesh(), scratch_types=[],
        out_type=jax.ShapeDtypeStruct((ROUTE_ROWS, HEAD_PAD), F32))
    def dispatch(x_hbm, i0_hbm, i1_hbm, i2_hbm, o_hbm):
        def body(x_vmem, i0_vmem, i1_vmem, i2_vmem):
            pltpu.sync_copy(x_vmem, o_hbm.at[i0_vmem.at[0]])
            pltpu.sync_copy(x_vmem, o_hbm.at[i1_vmem.at[0]])
            pltpu.sync_copy(x_vmem, o_hbm.at[i2_vmem.at[0]])

        idx_spec = pl.BlockSpec((1, SC_WINDOW), lambda i: (0, i))
        pltpu.emit_pipeline(
            body,
            grid=(n_rows // SC_WINDOW,),
            in_specs=[pl.BlockSpec((SC_WINDOW, HEAD_PAD), lambda i: (first_window + i, 0)),
                      idx_spec, idx_spec, idx_spec],
            out_specs=[],
            core_axis_name=("core", "subcore"),
            dimension_semantics=(pltpu.PARALLEL,),
        )(x_hbm, i0_hbm, i1_hbm, i2_hbm)

    return dispatch(rows, *pos_by_slot)


def _collect_call(table, idx):
    n_rows = idx.shape[1]

    @functools.partial(
        pl.kernel, mesh=_sc_mesh(), scratch_types=[],
        out_type=jax.ShapeDtypeStruct((n_rows, HEAD_PAD), F32))
    def collect(x_hbm, i_hbm, o_hbm):
        def body(i_vmem, o_vmem):
            pltpu.sync_copy(x_hbm.at[i_vmem.at[0]], o_vmem)

        pltpu.emit_pipeline(
            body,
            grid=(n_rows // SC_WINDOW,),
            in_specs=[pl.BlockSpec((1, SC_WINDOW), lambda i: (0, i))],
            out_specs=[pl.BlockSpec((SC_WINDOW, HEAD_PAD), lambda i: (i, 0))],
            core_axis_name=("core", "subcore"),
            dimension_semantics=(pltpu.PARALLEL,),
        )(i_hbm, o_hbm)

    return collect(table, idx)


def _routed_attn_kernel(th_ref, tj_ref, nt_ref, step_ref, q_ref, *refs, tiles):
    kv_refs, o_ref = refs[0:tiles], refs[tiles]
    t0 = pl.program_id(0) * tiles

    @pl.when(t0 < nt_ref[0])
    def _():
        qs = [q_ref[u * ROUTE_TILE:(u + 1) * ROUTE_TILE, :].astype(BF16) for u in range(tiles)]
        adds = [step_ref[t0 + u] * tj_ref[t0 + u].astype(F32) for u in range(tiles)]
        o_ref[...] = _block_rows(qs, kv_refs, None, adds)


def _routed_attn_call(tile_h, tile_j, n_tiles, tile_step, routed_q, kv,
                      *, split, tiles=ROUTE_TILES_PER_STEP):
    live = lambda s, th, tj, nt: (jnp.minimum(s, (nt[0] + tiles - 1) // tiles), 0)
    head0 = split * SPLIT_HEADS

    def kv_spec(u):
        return pl.BlockSpec(
            (None, None, SLAB_ROWS, MOBA_BLOCK),
            lambda s, th, tj, nt: (head0 + th[s * tiles + u], tj[s * tiles + u], 0, 0))

    grid_spec = pltpu.PrefetchScalarGridSpec(
        num_scalar_prefetch=3,
        grid=(MAX_TILES // tiles,),
        in_specs=([pl.BlockSpec(memory_space=pltpu.SMEM),
                   pl.BlockSpec((tiles * ROUTE_TILE, HEAD_PAD), live)]
                  + [kv_spec(u) for u in range(tiles)]),
        out_specs=pl.BlockSpec((tiles * ROUTE_TILE, HEAD_PAD), live),
    )
    return pl.pallas_call(
        functools.partial(_routed_attn_kernel, tiles=tiles),
        grid_spec=grid_spec,
        out_shape=jax.ShapeDtypeStruct((ROUTE_ROWS, HEAD_PAD), F32),
        compiler_params=_params(1),
        name=f"moba_routed_attn_{split}",
    )(tile_h, tile_j, n_tiles, tile_step, routed_q, *([kv] * tiles))


def _pm_out_kernel(slope_ref, x_ref, mod_ref, pooled_ref, *refs, blocks):
    own_refs, got_refs = refs[0:N_SPLITS], refs[N_SPLITS:2 * N_SPLITS]
    wp_ref, wa_ref, o_ref = refs[2 * N_SPLITS:]
    low = lax.broadcasted_iota(I32, (MOBA_BLOCK, HEAD_PAD), 1) < HEAD_DIM

    def lse_of(part):
        return jnp.where(low, pltpu.roll(part, HEAD_DIM, axis=1), part)

    y = jnp.dot(pooled_ref[...], wp_ref[...], preferred_element_type=F32)
    for hh in range(D_HEADS):
        own_ref, got_ref = own_refs[hh // SPLIT_HEADS], got_refs[hh // SPLIT_HEADS]
        a = hh % SPLIT_HEADS
        atts = []
        for bb in range(blocks):
            own = pl.program_id(0) * blocks + bb
            rows = slice(bb * MOBA_BLOCK, (bb + 1) * MOBA_BLOCK)
            own_shift = slope_ref[hh] * (own * MOBA_BLOCK).astype(F32)
            parts = [own_ref[a, rows, :]]
            lses = [lse_of(parts[0])]
            for s in range(MOBA_TOPK):
                valid = s < own
                part = got_ref[s, a, rows, :]
                parts.append(jnp.where(valid, part, 0.0))
                lses.append(jnp.where(valid, lse_of(part) - own_shift, NEG_BIG))
            top = functools.reduce(jnp.maximum, lses)
            num = jnp.zeros((MOBA_BLOCK, HEAD_PAD), F32)
            den = jnp.zeros((MOBA_BLOCK, HEAD_PAD), F32)
            for part, ls in zip(parts, lses):
                w = jnp.exp(ls - top)
                num = num + w * part
                den = den + w
            atts.append((num / den).astype(BF16))
        y = y + jnp.dot(jnp.concatenate(atts, axis=0), wa_ref[hh], preferred_element_type=F32)
    o_ref[...] = x_ref[...] + mod_ref[2:3, :] * y


def _pm_out_call(slopes, x, mod, pooled, own_parts, got_parts, wp, wa, *, layer, blocks=2):
    tm = blocks * MOBA_BLOCK
    return pl.pallas_call(
        functools.partial(_pm_out_kernel, blocks=blocks),
        grid=(SEQ // tm,),
        in_specs=[
            pl.BlockSpec(memory_space=pltpu.SMEM),
            pl.BlockSpec((tm, D_MODEL), lambda i: (i, 0)),
            _resident((None, None, 3, D_MODEL), lambda i: (layer, 1, 0, 0)),
            pl.BlockSpec((tm, C_WIDTH), lambda i: (i, 0)),
        ] + [pl.BlockSpec((SPLIT_HEADS, tm, HEAD_PAD), lambda i: (0, i, 0))] * N_SPLITS
        + [pl.BlockSpec((MOBA_TOPK, SPLIT_HEADS, tm, HEAD_PAD), lambda i: (0, 0, i, 0))] * N_SPLITS
        + [
            _full((C_WIDTH, D_MODEL)),
            _full((D_HEADS, HEAD_PAD, D_MODEL)),
        ],
        out_specs=pl.BlockSpec((tm, D_MODEL), lambda i: (i, 0)),
        out_shape=jax.ShapeDtypeStruct((SEQ, D_MODEL), F32),
        compiler_params=_params(1),
        name="pool_moba_out",
    )(slopes, x, mod, pooled, *own_parts, *got_parts, wp, wa)


def _head_pad(w, axis):
    shape = list(w.shape)
    shape[axis:axis + 1] = [D_HEADS, HEAD_DIM]
    pad = [(0, 0)] * len(shape)
    pad[axis + 1] = (0, HEAD_PAD - HEAD_DIM)
    out = jnp.pad(w.reshape(shape), pad)
    shape[axis:axis + 2] = [D_HEADS * HEAD_PAD]
    return out.reshape(shape)


def kernel(x, c, ada_w, ada_b, ffn_norm, ffn_w_gate, ffn_w_up, ffn_w_down, mix_norm, conv_w_in,
           conv_a_w, conv_a_b, conv_a_ln_g, conv_a_ln_b, conv_b_w, conv_w_out, pm_w_in, pool_w,
           pool_b, pool_scale, pm_w_out, final_norm):
    assert x.shape == (1, SEQ, D_MODEL) and c.shape == (1, D_MODEL)
    xs = x.reshape(SEQ, D_MODEL)
    mod = _mod_call(c, ada_w, ada_b)

    wg = ffn_w_gate.astype(BF16)
    wu = ffn_w_up.astype(BF16)
    wd = ffn_w_down.astype(BF16)
    ffn_g = ffn_norm.reshape(DEPTH, 2, 1, D_MODEL)
    mix_g = mix_norm.reshape(DEPTH, 1, D_MODEL)
    fin = final_norm.reshape(1, D_MODEL)
    ffn = functools.partial(_ffn_call, mod=mod, norm_g=ffn_g, wg=wg, wu=wu, wd=wd, fin=fin)

    xs = ffn(xs, layer=0, sub=0, final=False)
    grp = np.arange(A_WIDTH) // (A_WIDTH // A_GROUPS)
    seg = jnp.asarray((grp[:, None] == grp[None, :]) / (A_WIDTH // A_GROUPS), dtype=BF16)
    xs = _conv_mix_call(
        xs, mod, mix_g, conv_w_in[0].astype(BF16), conv_a_w[0], conv_a_b[0].reshape(1, A_WIDTH),
        conv_a_ln_g[0].reshape(1, A_WIDTH), conv_a_ln_b[0].reshape(1, A_WIDTH), conv_b_w[0],
        conv_w_out[0].astype(BF16), seg, layer=0)
    xs = ffn(xs, layer=0, sub=1, final=False)

    xs = ffn(xs, layer=1, sub=0, final=False)
    w_in = pm_w_in[0].astype(BF16)
    w_u = w_in[:, 0:C_WIDTH]
    w_q = w_in[:, C_WIDTH:C_WIDTH + D_WIDTH]
    w_kvt = w_in[:, C_WIDTH + D_WIDTH:C_WIDTH + 3 * D_WIDTH].T
    pooled, qrows, kv, kmean = _pm_proj_call(
        xs, mod, mix_g, w_u, w_q, w_kvt, pool_w[0].astype(BF16),
        pool_b[0].reshape(1, C_WIDTH), pool_scale[0].reshape(1, C_WIDTH), layer=1)
    kmean_h = _head_pad(kmean, 1).reshape(N_BLOCKS, D_HEADS, HEAD_PAD).transpose(1, 0, 2)

    n_pair = MOBA_TOPK * MOBA_BLOCK
    tri = jnp.asarray(np.arange(n_pair)[:, None] < np.arange(n_pair)[None, :], dtype=BF16)
    q_flat = qrows.reshape(D_HEADS * SEQ, HEAD_PAD)
    own_parts, got_parts = [], []
    anchor = jnp.zeros((1,), I32)
    for split in range(N_SPLITS):
        sel, rank, cnt, own_part = _gate_own_call(anchor, qrows, kv, kmean_h, tri, split=split)
        base_digits, tile_h, tile_j, n_tiles, tile_step = _route_tables(cnt, split)
        pos = _route_pos_call(sel, rank, base_digits)
        anchor = pos[0, 0, 0:1]
        pos_by_slot = [pos[s].reshape(1, SPLIT_HEADS * SEQ) for s in range(MOBA_TOPK)]
        routed_q = _dispatch_call(q_flat, pos_by_slot, split=split)
        routed_parts = _routed_attn_call(tile_h, tile_j, n_tiles, tile_step, routed_q, kv,
                                         split=split)
        got = _collect_call(routed_parts, pos.reshape(1, N_PAIRS))
        own_parts.append(own_part)
        got_parts.append(got.reshape(MOBA_TOPK, SPLIT_HEADS, SEQ, HEAD_PAD))

    slopes = jnp.asarray(2.0 ** (-8.0 * np.arange(1, D_HEADS + 1) / D_HEADS), dtype=F32)
    w_out = pm_w_out[0].astype(BF16)
    w_att = _head_pad(w_out[C_WIDTH:], 0).reshape(D_HEADS, HEAD_PAD, D_MODEL)
    xs = _pm_out_call(slopes, xs, mod, pooled, own_parts, got_parts, w_out[0:C_WIDTH], w_att,
                      layer=1)
    xs = ffn(xs, layer=1, sub=1, final=True)
    return xs.reshape(1, SEQ, D_MODEL)
```

```python
import functools

import numpy as np
import jax
import jax.numpy as jnp
from jax import lax
from jax.experimental import pallas as pl
from jax.experimental.pallas import tpu as pltpu
from jax.experimental.pallas import tpu_sc as plsc

D_MODEL = 1024
SEQ = 16384
DEPTH = 2
N_SUBLAYERS = 3
D_FF = 2816
EPS = 1e-6

A_WIDTH = 512
A_GROUPS = 8
A_CONV = 31
B_WIDTH = 512
B_CONV = 3

POOL_WINDOWS = (2, 4, 8, 16)
C_WIDTH = 512
C_GROUP_DIM = 128
D_HEADS = 8
HEAD_DIM = 64
D_WIDTH = 512
MOBA_BLOCK = 256
MOBA_TOPK = 3
N_BLOCKS = SEQ // MOBA_BLOCK

V7X_SUBLANES = 8
V7X_BF16_SUBLANES = 16
V7X_VMEM_LIMIT_BYTES = 56 * 1024 * 1024

FFN_ROWS = 1024
FFN_CHUNK = 1024
A_HALO = 32
B_HALO = 8
POOL_HALO = 16
NEG_BIG = -1e30

HEAD_PAD = 128
ALIBI_COL = HEAD_DIM
SLAB_K_ROWS = HEAD_DIM + V7X_BF16_SUBLANES
SLAB_ROWS = SLAB_K_ROWS + HEAD_DIM

SPLIT_HEADS = 4
N_SPLITS = D_HEADS // SPLIT_HEADS
ROUTE_TILE = 256
ROUTE_TILES_PER_STEP = 32
GATE_BLOCKS_PER_STEP = 4
N_PAIRS = SPLIT_HEADS * SEQ * MOBA_TOPK
N_GROUPS = SPLIT_HEADS * N_BLOCKS
MAX_TILES = N_PAIRS // ROUTE_TILE + N_GROUPS
TRASH_ROW0 = MAX_TILES * ROUTE_TILE
ROUTE_ROWS = (MAX_TILES + ROUTE_TILES_PER_STEP) * ROUTE_TILE
SC_WINDOW = 128
DIGIT = 128
assert MAX_TILES % ROUTE_TILES_PER_STEP == 0 and N_BLOCKS % GATE_BLOCKS_PER_STEP == 0
assert ROUTE_ROWS <= DIGIT ** 3 and MOBA_BLOCK <= ROUTE_TILE

BF16 = jnp.bfloat16
F32 = jnp.float32
I32 = jnp.int32


def _params(n_axes):
    return pltpu.CompilerParams(
        dimension_semantics=("arbitrary",) * n_axes,
        vmem_limit_bytes=V7X_VMEM_LIMIT_BYTES)


def _resident(block_shape, index_map):
    return pl.BlockSpec(block_shape, index_map, pipeline_mode=pl.Buffered(1))


def _full(shape):
    return _resident(shape, lambda *_: (0,) * len(shape))


def _sigmoid(v):
    return 1.0 / (1.0 + jnp.exp(-v))


def _norm_mod(x, g, mod):
    r = lax.rsqrt(jnp.mean(x * x, axis=-1, keepdims=True) + EPS)
    return (x * r) * (g * (1.0 + mod[1:2])) + mod[0:1]


def _split(v):
    hi = v.astype(BF16)
    return hi, (v - hi.astype(F32)).astype(BF16)


def _split_dot(v, w):
    hi, lo = _split(v)
    return (jnp.dot(hi, w, preferred_element_type=F32)
            + jnp.dot(lo, w, preferred_element_type=F32))


_NT = (((1,), (1,)), ((), ()))


def _mod_kernel(c_ref, w_ref, b_ref, o_ref):
    c = c_ref[...]
    cond = c * _sigmoid(c)
    o_ref[0] = jnp.sum(w_ref[0] * cond, axis=0, keepdims=True) + b_ref[0]


def _mod_call(c, ada_w, ada_b):
    n_out = N_SUBLAYERS * 3 * D_MODEL
    tn = D_MODEL
    out = pl.pallas_call(
        _mod_kernel,
        grid=(DEPTH, n_out // tn),
        in_specs=[
            pl.BlockSpec((D_MODEL, 1), lambda l, j: (0, 0)),
            pl.BlockSpec((1, D_MODEL, tn), lambda l, j: (l, 0, j)),
            pl.BlockSpec((1, 1, tn), lambda l, j: (l, 0, j)),
        ],
        out_specs=pl.BlockSpec((1, 1, tn), lambda l, j: (l, 0, j)),
        out_shape=jax.ShapeDtypeStruct((DEPTH, 1, n_out), F32),
        compiler_params=_params(2),
        name="adaln_mod",
    )(c.reshape(D_MODEL, 1), ada_w, ada_b.reshape(DEPTH, 1, n_out))
    return out.reshape(DEPTH, N_SUBLAYERS, 3, D_MODEL)


def _ffn_kernel(x_ref, mod_ref, g_ref, wg_ref, wu_ref, wd_ref, fin_ref, o_ref, *, final):
    x = x_ref[...]
    mod = mod_ref[...]
    h = _norm_mod(x, g_ref[...], mod).astype(BF16)
    y = None
    for c0 in range(0, D_FF, FFN_CHUNK):
        c1 = min(c0 + FFN_CHUNK, D_FF)
        gate = jnp.dot(h, wg_ref[:, c0:c1], preferred_element_type=F32)
        up = jnp.dot(h, wu_ref[:, c0:c1], preferred_element_type=F32)
        act = (gate * _sigmoid(gate) * up).astype(BF16)
        part = jnp.dot(act, wd_ref[c0:c1, :], preferred_element_type=F32)
        y = part if y is None else y + part
    xn = x + (0.5 * mod[2:3]) * y
    if final:
        r = lax.rsqrt(jnp.mean(xn * xn, axis=-1, keepdims=True) + EPS)
        xn = xn * r * fin_ref[...]
    o_ref[...] = xn


def _ffn_call(x, mod, norm_g, wg, wu, wd, fin, *, layer, sub, final, tm=FFN_ROWS):
    mod_sub = 0 if sub == 0 else 2
    return pl.pallas_call(
        functools.partial(_ffn_kernel, final=final),
        grid=(SEQ // tm,),
        in_specs=[
            pl.BlockSpec((tm, D_MODEL), lambda i: (i, 0)),
            _resident((None, None, 3, D_MODEL), lambda i: (layer, mod_sub, 0, 0)),
            _resident((None, None, 1, D_MODEL), lambda i: (layer, sub, 0, 0)),
            _resident((None, None, D_MODEL, D_FF), lambda i: (layer, sub, 0, 0)),
            _resident((None, None, D_MODEL, D_FF), lambda i: (layer, sub, 0, 0)),
            _resident((None, None, D_FF, D_MODEL), lambda i: (layer, sub, 0, 0)),
            _resident((1, D_MODEL), lambda i: (0, 0)),
        ],
        out_specs=pl.BlockSpec((tm, D_MODEL), lambda i: (i, 0)),
        out_shape=jax.ShapeDtypeStruct((SEQ, D_MODEL), F32),
        compiler_params=_params(1),
        name=f"ffn_l{layer}s{sub}",
    )(x, mod, norm_g, wg, wu, wd, fin)


def _conv_mix_kernel(x_ref, mod_ref, g_ref, win_ref, caw_ref, cab_ref, lng_ref, lnb_ref,
                     cbw_ref, wout_ref, seg_ref, o_ref, abuf, bbuf, sbuf, *, tm):
    i = pl.program_id(0)

    @pl.when(i == 0)
    def _():
        abuf[0:A_HALO, :] = jnp.zeros((A_HALO, A_WIDTH), F32)
        bbuf[0:B_HALO, :] = jnp.zeros((B_HALO, B_WIDTH), F32)

    x = x_ref[...]
    mod = mod_ref[...]
    h = _norm_mod(x, g_ref[...], mod).astype(BF16)
    z = jnp.dot(h, win_ref[...], preferred_element_type=F32)

    a = z[:, 0:A_WIDTH] * _sigmoid(z[:, A_WIDTH:2 * A_WIDTH])
    abuf[A_HALO:A_HALO + tm, :] = a
    acc = jnp.zeros((tm, A_WIDTH), F32) + cab_ref[...]
    ext = tm + V7X_SUBLANES
    for r in range(V7X_SUBLANES):
        part = None
        for q in range((A_CONV - 1 - r) // V7X_SUBLANES + 1):
            k = A_CONV - 1 - (V7X_SUBLANES * q + r)
            term = caw_ref[k:k + 1, :] * abuf[pl.ds(A_HALO - V7X_SUBLANES * (q + 1), ext), :]
            part = term if part is None else part + term
        if r == 0:
            acc = acc + part[V7X_SUBLANES:, :]
        else:
            sbuf[r - 1] = part
            acc = acc + sbuf[r - 1, pl.ds(V7X_SUBLANES - r, tm), :]
    abuf[0:A_HALO, :] = abuf[tm:tm + A_HALO, :]
    seg = seg_ref[...]
    mu = _split_dot(acc, seg)
    d = acc - mu
    var = _split_dot(d * d, seg)
    yn = d * lax.rsqrt(var + EPS) * lng_ref[...] + lnb_ref[...]
    a_out = yn * _sigmoid(yn)

    off = 2 * A_WIDTH
    cv = z[:, off + B_WIDTH:off + 2 * B_WIDTH] * z[:, off + 2 * B_WIDTH:off + 3 * B_WIDTH]
    bbuf[B_HALO:B_HALO + tm, :] = cv
    bacc = jnp.zeros((tm, B_WIDTH), F32)
    for k in range(B_CONV):
        bacc = bacc + cbw_ref[k:k + 1, :] * bbuf[pl.ds(B_HALO - (B_CONV - 1) + k, tm), :]
    bbuf[0:B_HALO, :] = bbuf[tm:tm + B_HALO, :]
    bb = z[:, off:off + B_WIDTH] * bacc

    y = (jnp.dot(a_out.astype(BF16), wout_ref[0:A_WIDTH, :], preferred_element_type=F32)
         + jnp.dot(bb.astype(BF16), wout_ref[A_WIDTH:A_WIDTH + B_WIDTH, :],
                   preferred_element_type=F32))
    o_ref[...] = x + mod[2:3] * y


def _conv_mix_call(x, mod, norm_g, w_in, caw, cab, lng, lnb, cbw, w_out, seg, *, layer, tm=512):
    even_in = w_in.shape[1]
    return pl.pallas_call(
        functools.partial(_conv_mix_kernel, tm=tm),
        grid=(SEQ // tm,),
        in_specs=[
            pl.BlockSpec((tm, D_MODEL), lambda i: (i, 0)),
            _resident((None, None, 3, D_MODEL), lambda i: (layer, 1, 0, 0)),
            _resident((None, 1, D_MODEL), lambda i: (layer, 0, 0)),
            _full((D_MODEL, even_in)),
            _full((A_CONV, A_WIDTH)),
            _full((1, A_WIDTH)),
            _full((1, A_WIDTH)),
            _full((1, A_WIDTH)),
            _full((B_CONV, B_WIDTH)),
            _full((A_WIDTH + B_WIDTH, D_MODEL)),
            _full((A_WIDTH, A_WIDTH)),
        ],
        out_specs=pl.BlockSpec((tm, D_MODEL), lambda i: (i, 0)),
        out_shape=jax.ShapeDtypeStruct((SEQ, D_MODEL), F32),
        scratch_shapes=[pltpu.VMEM((A_HALO + tm, A_WIDTH), F32),
                        pltpu.VMEM((B_HALO + tm, B_WIDTH), F32),
                        pltpu.VMEM((V7X_SUBLANES - 1, tm + V7X_SUBLANES, A_WIDTH), F32)],
        compiler_params=_params(1),
        name="conv_mixers",
    )(x, mod, norm_g, w_in, caw, cab, lng, lnb, cbw, w_out, seg)


def _pm_proj_kernel(x_ref, mod_ref, g_ref, wu_ref, wq_ref, wkvt_ref, pw_ref,
                    pb_ref, ps_ref, pooled_ref, qrow_ref, kv_ref, km_ref,
                    ubuf, hmean, *, tm):
    i = pl.program_id(0)

    @pl.when(i == 0)
    def _():
        ubuf[0:POOL_HALO, :] = jnp.zeros((POOL_HALO, C_WIDTH), F32)

    x = x_ref[...]
    h = _norm_mod(x, g_ref[...], mod_ref[...]).astype(BF16)
    u = jnp.dot(h, wu_ref[...], preferred_element_type=F32)
    qr = jnp.dot(h, wq_ref[...], preferred_element_type=F32)
    kvt = lax.dot_general(wkvt_ref[...], h, _NT, preferred_element_type=F32)

    ubuf[POOL_HALO:POOL_HALO + tm, :] = u
    t1 = (i * tm + 1 + lax.broadcasted_iota(I32, (tm, C_GROUP_DIM), 0)).astype(F32)
    for gi, w in enumerate(POOL_WINDOWS):
        c0 = gi * C_GROUP_DIM
        s = u[:, c0:c0 + C_GROUP_DIM]
        for k in range(1, w):
            s = s + ubuf[pl.ds(POOL_HALO - k, tm), c0:c0 + C_GROUP_DIM]
        pooled = s / jnp.minimum(t1, float(w)) - u[:, c0:c0 + C_GROUP_DIM]
        mixed = (jnp.dot(pooled.astype(BF16), pw_ref[gi], preferred_element_type=F32)
                 + pb_ref[:, c0:c0 + C_GROUP_DIM])
        pooled_ref[:, c0:c0 + C_GROUP_DIM] = (
            mixed * ps_ref[:, c0:c0 + C_GROUP_DIM]).astype(BF16)
    ubuf[0:POOL_HALO, :] = ubuf[tm:tm + POOL_HALO, :]

    lane = lax.broadcasted_iota(I32, (tm, HEAD_PAD), 1)
    tail = jnp.where(lane == ALIBI_COL, 1.0, 0.0)
    qs = qr * (HEAD_DIM ** -0.5)
    for pair in range(D_HEADS // 2):
        both = qs[:, pair * HEAD_PAD:(pair + 1) * HEAD_PAD]
        qrow_ref[2 * pair] = jnp.where(lane < HEAD_DIM, both, tail)
        qrow_ref[2 * pair + 1] = jnp.where(
            lane < HEAD_DIM, pltpu.roll(both, HEAD_DIM, axis=1), tail)
    per_tile = tm // MOBA_BLOCK
    kvb = kvt.astype(BF16)
    n_alibi = SLAB_K_ROWS - HEAD_DIM
    first = lax.broadcasted_iota(I32, (n_alibi, MOBA_BLOCK), 0) == 0
    koff = lax.broadcasted_iota(I32, (n_alibi, MOBA_BLOCK), 1).astype(F32)
    for hh in range(D_HEADS):
        alibi = jnp.where(first, koff * (2.0 ** -(hh + 1)), 0.0).astype(BF16)
        for sb in range(per_tile):
            keys = slice(sb * MOBA_BLOCK, (sb + 1) * MOBA_BLOCK)
            kv_ref[hh, sb, 0:HEAD_DIM, :] = kvb[hh * HEAD_DIM:(hh + 1) * HEAD_DIM, keys]
            kv_ref[hh, sb, HEAD_DIM:SLAB_K_ROWS, :] = alibi
            kv_ref[hh, sb, SLAB_K_ROWS:SLAB_ROWS, :] = kvb[
                D_WIDTH + hh * HEAD_DIM:D_WIDTH + (hh + 1) * HEAD_DIM, keys]

    for sb in range(per_tile):
        blk = h[sb * MOBA_BLOCK:(sb + 1) * MOBA_BLOCK, :].astype(F32)
        hmean[pl.ds(i * per_tile + sb, 1), :] = jnp.mean(blk, axis=0, keepdims=True)

    @pl.when(i == pl.num_programs(0) - 1)
    def _():
        hi, lo = _split(hmean[...])
        wkt = wkvt_ref[0:D_WIDTH, :]
        km_ref[...] = (lax.dot_general(hi, wkt, _NT, preferred_element_type=F32)
                       + lax.dot_general(lo, wkt, _NT, preferred_element_type=F32))


def _pm_proj_call(x, mod, norm_g, wu, wq, wkvt, pw, pb, ps, *, layer, tm=512):
    return pl.pallas_call(
        functools.partial(_pm_proj_kernel, tm=tm),
        grid=(SEQ // tm,),
        in_specs=[
            pl.BlockSpec((tm, D_MODEL), lambda i: (i, 0)),
            _resident((None, None, 3, D_MODEL), lambda i: (layer, 1, 0, 0)),
            _resident((None, 1, D_MODEL), lambda i: (layer, 0, 0)),
            _full((D_MODEL, C_WIDTH)),
            _full((D_MODEL, D_WIDTH)),
            _full((2 * D_WIDTH, D_MODEL)),
            _full((len(POOL_WINDOWS), C_GROUP_DIM, C_GROUP_DIM)),
            _full((1, C_WIDTH)),
            _full((1, C_WIDTH)),
        ],
        out_specs=[
            pl.BlockSpec((tm, C_WIDTH), lambda i: (i, 0)),
            pl.BlockSpec((D_HEADS, tm, HEAD_PAD), lambda i: (0, i, 0)),
            pl.BlockSpec((D_HEADS, tm // MOBA_BLOCK, SLAB_ROWS, MOBA_BLOCK),
                         lambda i: (0, i, 0, 0)),
            pl.BlockSpec((N_BLOCKS, D_WIDTH), lambda i: (0, 0)),
        ],
        out_shape=[
            jax.ShapeDtypeStruct((SEQ, C_WIDTH), BF16),
            jax.ShapeDtypeStruct((D_HEADS, SEQ, HEAD_PAD), F32),
            jax.ShapeDtypeStruct((D_HEADS, N_BLOCKS, SLAB_ROWS, MOBA_BLOCK), BF16),
            jax.ShapeDtypeStruct((N_BLOCKS, D_WIDTH), F32),
        ],
        scratch_shapes=[pltpu.VMEM((POOL_HALO + tm, C_WIDTH), F32),
                        pltpu.VMEM((N_BLOCKS, D_MODEL), F32)],
        compiler_params=_params(1),
        name="pool_qkv_proj",
    )(x, mod, norm_g, wu, wq, wkvt, pw, pb, ps)


def _block_rows(qs, slabs, mask, lse_add):
    n = qs[0].shape[0]
    kpad = jnp.zeros((HEAD_PAD - SLAB_K_ROWS, MOBA_BLOCK), BF16)
    s = jnp.concatenate(
        [jnp.dot(q, jnp.concatenate([slab[0:SLAB_K_ROWS, :], kpad], axis=0),
                 preferred_element_type=F32)
         for q, slab in zip(qs, slabs)], axis=0)
    if mask is not None:
        s = jnp.where(mask, s, NEG_BIG)
    m = jnp.max(s, axis=1, keepdims=True)
    p = jnp.exp(s - m).astype(BF16)
    vpad = jnp.zeros((HEAD_PAD - HEAD_DIM, MOBA_BLOCK), BF16)
    ones = jnp.ones((HEAD_PAD, MOBA_BLOCK), BF16)
    ot = jnp.concatenate(
        [lax.dot_general(p[u * n:(u + 1) * n],
                         jnp.concatenate([slab[SLAB_K_ROWS:SLAB_ROWS, :], vpad, ones], axis=0),
                         _NT, preferred_element_type=F32)
         for u, slab in enumerate(slabs)], axis=0)
    o, tot = ot[:, 0:HEAD_PAD], ot[:, HEAD_PAD:2 * HEAD_PAD]
    lse = m + jnp.log(tot)
    if lse_add is not None:
        lse = jnp.concatenate(
            [lse[u * n:(u + 1) * n] + add for u, add in enumerate(lse_add)], axis=0)
    low = lax.broadcasted_iota(I32, o.shape, 1) < HEAD_DIM
    return jnp.where(low, o / tot, lse)


def _onehot_pairs(sel_rows):
    blk = lax.broadcasted_iota(I32, (N_BLOCKS, MOBA_BLOCK), 0)
    return jnp.concatenate(
        [jnp.where(blk == sel_rows[s], 1.0, 0.0) for s in range(MOBA_TOPK)], axis=1)


def _gate_own_kernel(anchor_ref, qrow_ref, kv_ref, km_ref, tri_ref,
                     sel_ref, rank_ref, cnt_ref, own_ref, *, heads, blocks):
    del anchor_ref
    n_pair = MOBA_TOPK * MOBA_BLOCK
    tiles = [(bb, a) for bb in range(blocks) for a in range(heads)]
    rows_of = lambda bb: slice(bb * MOBA_BLOCK, (bb + 1) * MOBA_BLOCK)
    qs = [qrow_ref[a, rows_of(bb), :].astype(BF16) for bb, a in tiles]

    kms = [_split(km_ref[a]) for a in range(heads)]
    gates = [lax.dot_general(kms[a][0], q, _NT, preferred_element_type=F32)
             + lax.dot_general(kms[a][1], q, _NT, preferred_element_type=F32)
             for (bb, a), q in zip(tiles, qs)]
    width = len(tiles) * MOBA_BLOCK
    blk = lax.broadcasted_iota(I32, (N_BLOCKS, width), 0)
    lane = lax.broadcasted_iota(I32, (1, width), 1)
    own = pl.program_id(0) * blocks + lane // (heads * MOBA_BLOCK)
    g = jnp.where(blk < own, jnp.concatenate(gates, axis=1), -jnp.inf)
    slots = []
    for s in range(MOBA_TOPK):
        best = jnp.max(g, axis=0, keepdims=True)
        idx = jnp.min(jnp.where(g == best, blk, N_BLOCKS), axis=0, keepdims=True)
        g = jnp.where(blk == idx, -jnp.inf, g)
        slots.append(jnp.where(s < own, idx, N_BLOCKS))

    onehots = []
    for t, (bb, a) in enumerate(tiles):
        sel_rows = [slot[:, t * MOBA_BLOCK:(t + 1) * MOBA_BLOCK] for slot in slots]
        sel_ref[a, :, rows_of(bb)] = jnp.concatenate(sel_rows, axis=0)
        onehots.append(_onehot_pairs(sel_rows))
    onehot = jnp.concatenate(onehots, axis=0)
    oh = onehot.astype(BF16)
    before = jnp.dot(oh, tri_ref[...], preferred_element_type=F32)
    hit = onehot * before
    ones = jnp.ones((8, n_pair), BF16)
    cnt = lax.dot_general(ones, oh, _NT, preferred_element_type=F32)
    for t, (bb, a) in enumerate(tiles):
        rank = jnp.sum(hit[t * N_BLOCKS:(t + 1) * N_BLOCKS], axis=0, keepdims=True).astype(I32)
        rank_ref[a, :, rows_of(bb)] = jnp.concatenate(
            [rank[:, s * MOBA_BLOCK:(s + 1) * MOBA_BLOCK] for s in range(MOBA_TOPK)], axis=0)
        cnt_ref[a, bb] = cnt[0:1, t * N_BLOCKS:(t + 1) * N_BLOCKS].astype(I32)

    qi = lax.broadcasted_iota(I32, (width, MOBA_BLOCK), 0) % MOBA_BLOCK
    ki = lax.broadcasted_iota(I32, (width, MOBA_BLOCK), 1)
    rows = _block_rows(qs, [kv_ref.at[a, bb] for bb, a in tiles], ki <= qi, None)
    for t, (bb, a) in enumerate(tiles):
        own_ref[a, rows_of(bb), :] = rows[t * MOBA_BLOCK:(t + 1) * MOBA_BLOCK]


def _gate_own_call(anchor, qrows, kv, kmean_h, tri, *, split, blocks=GATE_BLOCKS_PER_STEP):
    heads = SPLIT_HEADS
    rows = blocks * MOBA_BLOCK
    return pl.pallas_call(
        functools.partial(_gate_own_kernel, heads=heads, blocks=blocks),
        grid=(N_BLOCKS // blocks,),
        in_specs=[
            pl.BlockSpec(memory_space=pltpu.SMEM),
            pl.BlockSpec((heads, rows, HEAD_PAD), lambda b: (split, b, 0)),
            pl.BlockSpec((heads, blocks, SLAB_ROWS, MOBA_BLOCK), lambda b: (split, b, 0, 0)),
            pl.BlockSpec((heads, N_BLOCKS, HEAD_PAD), lambda b: (split, 0, 0)),
            _full((MOBA_TOPK * MOBA_BLOCK, MOBA_TOPK * MOBA_BLOCK)),
        ],
        out_specs=[
            pl.BlockSpec((heads, MOBA_TOPK, rows), lambda b: (0, 0, b)),
            pl.BlockSpec((heads, MOBA_TOPK, rows), lambda b: (0, 0, b)),
            pl.BlockSpec((heads, blocks, 1, N_BLOCKS), lambda b: (0, b, 0, 0)),
            pl.BlockSpec((heads, rows, HEAD_PAD), lambda b: (0, b, 0)),
        ],
        out_shape=[
            jax.ShapeDtypeStruct((heads, MOBA_TOPK, SEQ), I32),
            jax.ShapeDtypeStruct((heads, MOBA_TOPK, SEQ), I32),
            jax.ShapeDtypeStruct((heads, N_BLOCKS, 1, N_BLOCKS), I32),
            jax.ShapeDtypeStruct((heads, SEQ, HEAD_PAD), F32),
        ],
        compiler_params=_params(1),
        name=f"moba_gate_own_{split}",
    )(anchor, qrows, kv, kmean_h, tri)


def _route_pos_kernel(sel_ref, rank_ref, base_ref, pos_ref, *, blocks):
    lane = lax.broadcasted_iota(I32, (1, MOBA_BLOCK), 1)
    for bb in range(blocks):
        qs = slice(bb * MOBA_BLOCK, (bb + 1) * MOBA_BLOCK)
        rows = [[] for _ in range(MOBA_TOPK)]
        for a in range(SPLIT_HEADS):
            sel = sel_ref[a, :, qs]
            onehot = _onehot_pairs([sel[s:s + 1, :] for s in range(MOBA_TOPK)]).astype(BF16)
            dig = jnp.dot(base_ref[a, bb].astype(BF16), onehot,
                          preferred_element_type=F32)
            base = (dig[0:1] * float(DIGIT * DIGIT) + dig[1:2] * float(DIGIT)
                    + dig[2:3]).astype(I32)
            for s in range(MOBA_TOPK):
                p = base[:, s * MOBA_BLOCK:(s + 1) * MOBA_BLOCK] + rank_ref[a, s:s + 1, qs]
                rows[s].append(
                    jnp.where(sel[s:s + 1, :] < N_BLOCKS, p, TRASH_ROW0 + lane % SC_WINDOW))
        for s in range(MOBA_TOPK):
            pos_ref[s, :, qs] = jnp.concatenate(rows[s], axis=0)


def _route_pos_call(sel, rank, base_digits, *, blocks=8):
    width = blocks * MOBA_BLOCK
    return pl.pallas_call(
        functools.partial(_route_pos_kernel, blocks=blocks),
        grid=(N_BLOCKS // blocks,),
        in_specs=[
            pl.BlockSpec((SPLIT_HEADS, MOBA_TOPK, width), lambda b: (0, 0, b)),
            pl.BlockSpec((SPLIT_HEADS, MOBA_TOPK, width), lambda b: (0, 0, b)),
            pl.BlockSpec((SPLIT_HEADS, blocks, 8, N_BLOCKS), lambda b: (0, b, 0, 0)),
        ],
        out_specs=pl.BlockSpec((MOBA_TOPK, SPLIT_HEADS, width), lambda b: (0, 0, b)),
        out_shape=jax.ShapeDtypeStruct((MOBA_TOPK, SPLIT_HEADS, SEQ), I32),
        compiler_params=_params(1),
        name="moba_route_pos",
    )(sel, rank, base_digits)


def _route_tables(cnt, split):
    cnt = cnt.reshape(SPLIT_HEADS, N_BLOCKS, N_BLOCKS)
    tiles = (cnt.sum(axis=1) + ROUTE_TILE - 1) // ROUTE_TILE
    tiles_flat = tiles.reshape(N_GROUPS)
    tile0 = jnp.cumsum(tiles_flat) - tiles_flat
    n_tiles = tiles_flat.sum().astype(I32)
    within = jnp.cumsum(cnt, axis=1) - cnt
    base = tile0.reshape(SPLIT_HEADS, 1, N_BLOCKS) * ROUTE_TILE + within
    digits = jnp.stack([base // (DIGIT * DIGIT), (base // DIGIT) % DIGIT, base % DIGIT],
                       axis=2).astype(F32)
    digits = jnp.pad(digits, ((0, 0), (0, 0), (0, 8 - 3), (0, 0)))
    t = jnp.arange(MAX_TILES, dtype=I32)
    ended = (tile0 + tiles_flat)[None, :] <= t[:, None]
    group_of_tile = jnp.minimum(ended.sum(axis=1).astype(I32), N_GROUPS - 1)
    tile_h = group_of_tile // N_BLOCKS
    head = split * SPLIT_HEADS + tile_h
    tile_step = lax.bitcast_convert_type((127 + 7 - head) << 23, F32)
    return digits, tile_h, group_of_tile % N_BLOCKS, n_tiles.reshape(1), tile_step


def _sc_mesh():
    return plsc.VectorSubcoreMesh(core_axis_name="core", subcore_axis_name="subcore")


def _dispatch_call(rows, pos_by_slot, *, split):
    n_rows = SPLIT_HEADS * SEQ
    first_window = split * n_rows // SC_WINDOW

    @functools.partial(
        pl.kernel, mesh=_sc_mesh(), scratch_types=[],
        out_type=jax.ShapeDtypeStruct((ROUTE_ROWS, HEAD_PAD), F32))
    def dispatch(x_hbm, i0_hbm, i1_hbm, i2_hbm, o_hbm):
        def body(x_vmem, i0_vmem, i1_vmem, i2_vmem):
            pltpu.sync_copy(x_vmem, o_hbm.at[i0_vmem.at[0]])
            pltpu.sync_copy(x_vmem, o_hbm.at[i1_vmem.at[0]])
            pltpu.sync_copy(x_vmem, o_hbm.at[i2_vmem.at[0]])

        idx_spec = pl.BlockSpec((1, SC_WINDOW), lambda i: (0, i))
        pltpu.emit_pipeline(
            body,
            grid=(n_rows // SC_WINDOW,),
            in_specs=[pl.BlockSpec((SC_WINDOW, HEAD_PAD), lambda i: (first_window + i, 0)),
                      idx_spec, idx_spec, idx_spec],
            out_specs=[],
            core_axis_name=("core", "subcore"),
            dimension_semantics=(pltpu.PARALLEL,),
        )(x_hbm, i0_hbm, i1_hbm, i2_hbm)

    return dispatch(rows, *pos_by_slot)


def _collect_call(table, idx):
    n_rows = idx.shape[1]

    @functools.partial(
        pl.kernel, mesh=_sc_mesh(), scratch_types=[],
        out_type=jax.ShapeDtypeStruct((n_rows, HEAD_PAD), F32))
    def collect(x_hbm, i_hbm, o_hbm):
        def body(i_vmem, o_vmem):
            pltpu.sync_copy(x_hbm.at[i_vmem.at[0]], o_vmem)

        pltpu.emit_pipeline(
            body,
            grid=(n_rows // SC_WINDOW,),
            in_specs=[pl.BlockSpec((1, SC_WINDOW), lambda i: (0, i))],
            out_specs=[pl.BlockSpec((SC_WINDOW, HEAD_PAD), lambda i: (i, 0))],
            core_axis_name=("core", "subcore"),
            dimension_semantics=(pltpu.PARALLEL,),
        )(i_hbm, o_hbm)

    return collect(table, idx)


def _routed_attn_kernel(th_ref, tj_ref, nt_ref, step_ref, q_ref, *refs, tiles):
    kv_refs, o_ref = refs[0:tiles], refs[tiles]
    t0 = pl.program_id(0) * tiles

    @pl.when(t0 < nt_ref[0])
    def _():
        qs = [q_ref[u * ROUTE_TILE:(u + 1) * ROUTE_TILE, :].astype(BF16) for u in range(tiles)]
        adds = [step_ref[t0 + u] * tj_ref[t0 + u].astype(F32) for u in range(tiles)]
        o_ref[...] = _block_rows(qs, kv_refs, None, adds)


def _routed_attn_call(tile_h, tile_j, n_tiles, tile_step, routed_q, kv,
                      *, split, tiles=ROUTE_TILES_PER_STEP):
    live = lambda s, th, tj, nt: (jnp.minimum(s, (nt[0] + tiles - 1) // tiles), 0)
    head0 = split * SPLIT_HEADS

    def kv_spec(u):
        return pl.BlockSpec(
            (None, None, SLAB_ROWS, MOBA_BLOCK),
            lambda s, th, tj, nt: (head0 + th[s * tiles + u], tj[s * tiles + u], 0, 0))

    grid_spec = pltpu.PrefetchScalarGridSpec(
        num_scalar_prefetch=3,
        grid=(MAX_TILES // tiles,),
        in_specs=([pl.BlockSpec(memory_space=pltpu.SMEM),
                   pl.BlockSpec((tiles * ROUTE_TILE, HEAD_PAD), live)]
                  + [kv_spec(u) for u in range(tiles)]),
        out_specs=pl.BlockSpec((tiles * ROUTE_TILE, HEAD_PAD), live),
    )
    return pl.pallas_call(
        functools.partial(_routed_attn_kernel, tiles=tiles),
        grid_spec=grid_spec,
        out_shape=jax.ShapeDtypeStruct((ROUTE_ROWS, HEAD_PAD), F32),
        compiler_params=_params(1),
        name=f"moba_routed_attn_{split}",
    )(tile_h, tile_j, n_tiles, tile_step, routed_q, *([kv] * tiles))


def _pm_out_kernel(slope_ref, x_ref, mod_ref, pooled_ref, *refs, blocks):
    own_refs, got_refs = refs[0:N_SPLITS], refs[N_SPLITS:2 * N_SPLITS]
    wp_ref, wa_ref, o_ref = refs[2 * N_SPLITS:]
    low = lax.broadcasted_iota(I32, (MOBA_BLOCK, HEAD_PAD), 1) < HEAD_DIM

    def lse_of(part):
        return jnp.where(low, pltpu.roll(part, HEAD_DIM, axis=1), part)

    y = jnp.dot(pooled_ref[...], wp_ref[...], preferred_element_type=F32)
    for hh in range(D_HEADS):
        own_ref, got_ref = own_refs[hh // SPLIT_HEADS], got_refs[hh // SPLIT_HEADS]
        a = hh % SPLIT_HEADS
        atts = []
        for bb in range(blocks):
            own = pl.program_id(0) * blocks + bb
            rows = slice(bb * MOBA_BLOCK, (bb + 1) * MOBA_BLOCK)
            own_shift = slope_ref[hh] * (own * MOBA_BLOCK).astype(F32)
            parts = [own_ref[a, rows, :]]
            lses = [lse_of(parts[0])]
            for s in range(MOBA_TOPK):
                valid = s < own
                part = got_ref[s, a, rows, :]
                parts.append(jnp.where(valid, part, 0.0))
                lses.append(jnp.where(valid, lse_of(part) - own_shift, NEG_BIG))
            top = functools.reduce(jnp.maximum, lses)
            num = jnp.zeros((MOBA_BLOCK, HEAD_PAD), F32)
            den = jnp.zeros((MOBA_BLOCK, HEAD_PAD), F32)
            for part, ls in zip(parts, lses):
                w = jnp.exp(ls - top)
                num = num + w * part
                den = den + w
            atts.append((num / den).astype(BF16))
        y = y + jnp.dot(jnp.concatenate(atts, axis=0), wa_ref[hh], preferred_element_type=F32)
    o_ref[...] = x_ref[...] + mod_ref[2:3, :] * y


def _pm_out_call(slopes, x, mod, pooled, own_parts, got_parts, wp, wa, *, layer, blocks=2):
    tm = blocks * MOBA_BLOCK
    return pl.pallas_call(
        functools.partial(_pm_out_kernel, blocks=blocks),
        grid=(SEQ // tm,),
        in_specs=[
            pl.BlockSpec(memory_space=pltpu.SMEM),
            pl.BlockSpec((tm, D_MODEL), lambda i: (i, 0)),
            _resident((None, None, 3, D_MODEL), lambda i: (layer, 1, 0, 0)),
            pl.BlockSpec((tm, C_WIDTH), lambda i: (i, 0)),
        ] + [pl.BlockSpec((SPLIT_HEADS, tm, HEAD_PAD), lambda i: (0, i, 0))] * N_SPLITS
        + [pl.BlockSpec((MOBA_TOPK, SPLIT_HEADS, tm, HEAD_PAD), lambda i: (0, 0, i, 0))] * N_SPLITS
        + [
            _full((C_WIDTH, D_MODEL)),
            _full((D_HEADS, HEAD_PAD, D_MODEL)),
        ],
        out_specs=pl.BlockSpec((tm, D_MODEL), lambda i: (i, 0)),
        out_shape=jax.ShapeDtypeStruct((SEQ, D_MODEL), F32),
        compiler_params=_params(1),
        name="pool_moba_out",
    )(slopes, x, mod, pooled, *own_parts, *got_parts, wp, wa)


def _head_pad(w, axis):
    shape = list(w.shape)
    shape[axis:axis + 1] = [D_HEADS, HEAD_DIM]
    pad = [(0, 0)] * len(shape)
    pad[axis + 1] = (0, HEAD_PAD - HEAD_DIM)
    out = jnp.pad(w.reshape(shape), pad)
    shape[axis:axis + 2] = [D_HEADS * HEAD_PAD]
    return out.reshape(shape)


def kernel(x, c, ada_w, ada_b, ffn_norm, ffn_w_gate, ffn_w_up, ffn_w_down, mix_norm, conv_w_in,
           conv_a_w, conv_a_b, conv_a_ln_g, conv_a_ln_b, conv_b_w, conv_w_out, pm_w_in, pool_w,
           pool_b, pool_scale, pm_w_out, final_norm):
    assert x.shape == (1, SEQ, D_MODEL) and c.shape == (1, D_MODEL)
    xs = x.reshape(SEQ, D_MODEL)
    mod = _mod_call(c, ada_w, ada_b)

    wg = ffn_w_gate.astype(BF16)
    wu = ffn_w_up.astype(BF16)
    wd = ffn_w_down.astype(BF16)
    ffn_g = ffn_norm.reshape(DEPTH, 2, 1, D_MODEL)
    mix_g = mix_norm.reshape(DEPTH, 1, D_MODEL)
    fin = final_norm.reshape(1, D_MODEL)
    ffn = functools.partial(_ffn_call, mod=mod, norm_g=ffn_g, wg=wg, wu=wu, wd=wd, fin=fin)

    xs = ffn(xs, layer=0, sub=0, final=False)
    grp = np.arange(A_WIDTH) // (A_WIDTH // A_GROUPS)
    seg = jnp.asarray((grp[:, None] == grp[None, :]) / (A_WIDTH // A_GROUPS), dtype=BF16)
    xs = _conv_mix_call(
        xs, mod, mix_g, conv_w_in[0].astype(BF16), conv_a_w[0], conv_a_b[0].reshape(1, A_WIDTH),
        conv_a_ln_g[0].reshape(1, A_WIDTH), conv_a_ln_b[0].reshape(1, A_WIDTH), conv_b_w[0],
        conv_w_out[0].astype(BF16), seg, layer=0)
    xs = ffn(xs, layer=0, sub=1, final=False)

    xs = ffn(xs, layer=1, sub=0, final=False)
    w_in = pm_w_in[0].astype(BF16)
    w_u = w_in[:, 0:C_WIDTH]
    w_q = w_in[:, C_WIDTH:C_WIDTH + D_WIDTH]
    w_kvt = w_in[:, C_WIDTH + D_WIDTH:C_WIDTH + 3 * D_WIDTH].T
    pooled, qrows, kv, kmean = _pm_proj_call(
        xs, mod, mix_g, w_u, w_q, w_kvt, pool_w[0].astype(BF16),
        pool_b[0].reshape(1, C_WIDTH), pool_scale[0].reshape(1, C_WIDTH), layer=1)
    kmean_h = _head_pad(kmean, 1).reshape(N_BLOCKS, D_HEADS, HEAD_PAD).transpose(1, 0, 2)

    n_pair = MOBA_TOPK * MOBA_BLOCK
    tri = jnp.asarray(np.arange(n_pair)[:, None] < np.arange(n_pair)[None, :], dtype=BF16)
    q_flat = qrows.reshape(D_HEADS * SEQ, HEAD_PAD)
    own_parts, got_parts = [], []
    anchor = jnp.zeros((1,), I32)
    for split in range(N_SPLITS):
        sel, rank, cnt, own_part = _gate_own_call(anchor, qrows, kv, kmean_h, tri, split=split)
        base_digits, tile_h, tile_j, n_tiles, tile_step = _route_tables(cnt, split)
        pos = _route_pos_call(sel, rank, base_digits)
        anchor = pos[0, 0, 0:1]
        pos_by_slot = [pos[s].reshape(1, SPLIT_HEADS * SEQ) for s in range(MOBA_TOPK)]
        routed_q = _dispatch_call(q_flat, pos_by_slot, split=split)
        routed_parts = _routed_attn_call(tile_h, tile_j, n_tiles, tile_step, routed_q, kv,
                                         split=split)
        got = _collect_call(routed_parts, pos.reshape(1, N_PAIRS))
        own_parts.append(own_part)
        got_parts.append(got.reshape(MOBA_TOPK, SPLIT_HEADS, SEQ, HEAD_PAD))

    slopes = jnp.asarray(2.0 ** (-8.0 * np.arange(1, D_HEADS + 1) / D_HEADS), dtype=F32)
    w_out = pm_w_out[0].astype(BF16)
    w_att = _head_pad(w_out[C_WIDTH:], 0).reshape(D_HEADS, HEAD_PAD, D_MODEL)
    xs = _pm_out_call(slopes, xs, mod, pooled, own_parts, got_parts, w_out[0:C_WIDTH], w_att,
                      layer=1)
    xs = ffn(xs, layer=1, sub=1, final=True)
    return xs.reshape(1, SEQ, D_MODEL)
```

```python
import functools

import numpy as np
import jax
import jax.numpy as jnp
from jax import lax
from jax.experimental import pallas as pl
from jax.experimental.pallas import tpu as pltpu
from jax.experimental.pallas import tpu_sc as plsc

D_MODEL = 1024
SEQ = 16384
DEPTH = 2
N_SUBLAYERS = 3
D_FF = 2816
EPS = 1e-6

A_WIDTH = 512
A_GROUPS = 8
A_CONV = 31
B_WIDTH = 512
B_CONV = 3

POOL_WINDOWS = (2, 4, 8, 16)
C_WIDTH = 512
C_GROUP_DIM = 128
D_HEADS = 8
HEAD_DIM = 64
D_WIDTH = 512
MOBA_BLOCK = 256
MOBA_TOPK = 3
N_BLOCKS = SEQ // MOBA_BLOCK

V7X_SUBLANES = 8
V7X_BF16_SUBLANES = 16
V7X_VMEM_LIMIT_BYTES = 56 * 1024 * 1024

A_HALO = 32
B_HALO = 8
POOL_HALO = 16
NEG_BIG = -1e30

HEAD_PAD = 128
ALIBI_COL = HEAD_DIM
SLAB_K_ROWS = HEAD_DIM + V7X_BF16_SUBLANES
SLAB_ROWS = SLAB_K_ROWS + HEAD_DIM

SPLIT_HEADS = 4
N_SPLITS = D_HEADS // SPLIT_HEADS
ROUTE_TILE = 256
ROUTE_TILES_PER_STEP = 32
GATE_BLOCKS_PER_STEP = 4
OWN_BLOCKS_PER_STEP = 4
N_PAIRS = SPLIT_HEADS * SEQ * MOBA_TOPK
N_GROUPS = SPLIT_HEADS * N_BLOCKS
MAX_TILES = N_PAIRS // ROUTE_TILE + N_GROUPS
TRASH_ROW0 = MAX_TILES * ROUTE_TILE
ROUTE_ROWS = (MAX_TILES + ROUTE_TILES_PER_STEP) * ROUTE_TILE
SC_WINDOW = 128
DIGIT = 128
assert MAX_TILES % ROUTE_TILES_PER_STEP == 0
assert N_BLOCKS % GATE_BLOCKS_PER_STEP == 0 and N_BLOCKS % OWN_BLOCKS_PER_STEP == 0
assert ROUTE_ROWS <= DIGIT ** 3 and MOBA_BLOCK <= ROUTE_TILE

BF16 = jnp.bfloat16
F32 = jnp.float32
I32 = jnp.int32


def _params(n_axes):
    return pltpu.CompilerParams(
        dimension_semantics=("arbitrary",) * n_axes,
        vmem_limit_bytes=V7X_VMEM_LIMIT_BYTES)


def _resident(block_shape, index_map):
    return pl.BlockSpec(block_shape, index_map, pipeline_mode=pl.Buffered(1))


def _full(shape):
    return _resident(shape, lambda *_: (0,) * len(shape))


def _sigmoid(v):
    return 1.0 / (1.0 + jnp.exp(-v))


def _norm_mod(x, g, mod):
    r = lax.rsqrt(jnp.mean(x * x, axis=-1, keepdims=True) + EPS)
    return (x * r) * (g * (1.0 + mod[1:2])) + mod[0:1]


def _split(v):
    hi = v.astype(BF16)
    return hi, (v - hi.astype(F32)).astype(BF16)


def _split_dot(v, w):
    hi, lo = _split(v)
    return (jnp.dot(hi, w, preferred_element_type=F32)
            + jnp.dot(lo, w, preferred_element_type=F32))


_NT = (((1,), (1,)), ((), ()))


def _mod_kernel(c_ref, w_ref, b_ref, o_ref):
    c = c_ref[...]
    cond = c * _sigmoid(c)
    o_ref[0] = jnp.sum(w_ref[0] * cond, axis=0, keepdims=True) + b_ref[0]


def _mod_call(c, ada_w, ada_b):
    n_out = N_SUBLAYERS * 3 * D_MODEL
    tn = D_MODEL
    out = pl.pallas_call(
        _mod_kernel,
        grid=(DEPTH, n_out // tn),
        in_specs=[
            pl.BlockSpec((D_MODEL, 1), lambda l, j: (0, 0)),
            pl.BlockSpec((1, D_MODEL, tn), lambda l, j: (l, 0, j)),
            pl.BlockSpec((1, 1, tn), lambda l, j: (l, 0, j)),
        ],
        out_specs=pl.BlockSpec((1, 1, tn), lambda l, j: (l, 0, j)),
        out_shape=jax.ShapeDtypeStruct((DEPTH, 1, n_out), F32),
        compiler_params=_params(2),
        name="adaln_mod",
    )(c.reshape(D_MODEL, 1), ada_w, ada_b.reshape(DEPTH, 1, n_out))
    return out.reshape(DEPTH, N_SUBLAYERS, 3, D_MODEL)


def _ffn_kernel(x_ref, mod_ref, g_ref, wg_ref, wu_ref, wd_ref, fin_ref, o_ref, *, final):
    x = x_ref[...]
    mod = mod_ref[...]
    h = _norm_mod(x, g_ref[...], mod).astype(BF16)
    gate = jnp.dot(h, wg_ref[...], preferred_element_type=F32)
    up = jnp.dot(h, wu_ref[...], preferred_element_type=F32)
    act = (gate * _sigmoid(gate) * up).astype(BF16)
    y = jnp.dot(act, wd_ref[...], preferred_element_type=F32)
    xn = x + (0.5 * mod[2:3]) * y
    if final:
        r = lax.rsqrt(jnp.mean(xn * xn, axis=-1, keepdims=True) + EPS)
        xn = xn * r * fin_ref[...]
    o_ref[...] = xn


def _ffn_call(x, mod, norm_g, wg, wu, wd, fin, *, layer, sub, final, tm=512):
    mod_sub = 0 if sub == 0 else 2
    return pl.pallas_call(
        functools.partial(_ffn_kernel, final=final),
        grid=(SEQ // tm,),
        in_specs=[
            pl.BlockSpec((tm, D_MODEL), lambda i: (i, 0)),
            _resident((None, None, 3, D_MODEL), lambda i: (layer, mod_sub, 0, 0)),
            _resident((None, None, 1, D_MODEL), lambda i: (layer, sub, 0, 0)),
            _resident((None, None, D_MODEL, D_FF), lambda i: (layer, sub, 0, 0)),
            _resident((None, None, D_MODEL, D_FF), lambda i: (layer, sub, 0, 0)),
            _resident((None, None, D_FF, D_MODEL), lambda i: (layer, sub, 0, 0)),
            _resident((1, D_MODEL), lambda i: (0, 0)),
        ],
        out_specs=pl.BlockSpec((tm, D_MODEL), lambda i: (i, 0)),
        out_shape=jax.ShapeDtypeStruct((SEQ, D_MODEL), F32),
        compiler_params=_params(1),
        name=f"ffn_l{layer}s{sub}",
    )(x, mod, norm_g, wg, wu, wd, fin)


def _conv_mix_kernel(x_ref, mod_ref, g_ref, win_ref, caw_ref, cab_ref, lng_ref, lnb_ref,
                     cbw_ref, wout_ref, seg_ref, o_ref, abuf, bbuf, sbuf, *, tm):
    i = pl.program_id(0)

    @pl.when(i == 0)
    def _():
        abuf[0:A_HALO, :] = jnp.zeros((A_HALO, A_WIDTH), F32)
        bbuf[0:B_HALO, :] = jnp.zeros((B_HALO, B_WIDTH), F32)

    x = x_ref[...]
    mod = mod_ref[...]
    h = _norm_mod(x, g_ref[...], mod).astype(BF16)
    z = jnp.dot(h, win_ref[...], preferred_element_type=F32)

    a = z[:, 0:A_WIDTH] * _sigmoid(z[:, A_WIDTH:2 * A_WIDTH])
    abuf[A_HALO:A_HALO + tm, :] = a
    acc = jnp.zeros((tm, A_WIDTH), F32) + cab_ref[...]
    ext = tm + V7X_SUBLANES
    for r in range(V7X_SUBLANES):
        part = None
        for q in range((A_CONV - 1 - r) // V7X_SUBLANES + 1):
            k = A_CONV - 1 - (V7X_SUBLANES * q + r)
            term = caw_ref[k:k + 1, :] * abuf[pl.ds(A_HALO - V7X_SUBLANES * (q + 1), ext), :]
            part = term if part is None else part + term
        if r == 0:
            acc = acc + part[V7X_SUBLANES:, :]
        else:
            sbuf[r - 1] = part
            acc = acc + sbuf[r - 1, pl.ds(V7X_SUBLANES - r, tm), :]
    abuf[0:A_HALO, :] = abuf[tm:tm + A_HALO, :]
    seg = seg_ref[...]
    mu = _split_dot(acc, seg)
    d = acc - mu
    var = _split_dot(d * d, seg)
    yn = d * lax.rsqrt(var + EPS) * lng_ref[...] + lnb_ref[...]
    a_out = yn * _sigmoid(yn)

    off = 2 * A_WIDTH
    cv = z[:, off + B_WIDTH:off + 2 * B_WIDTH] * z[:, off + 2 * B_WIDTH:off + 3 * B_WIDTH]
    bbuf[B_HALO:B_HALO + tm, :] = cv
    bacc = jnp.zeros((tm, B_WIDTH), F32)
    for k in range(B_CONV):
        bacc = bacc + cbw_ref[k:k + 1, :] * bbuf[pl.ds(B_HALO - (B_CONV - 1) + k, tm), :]
    bbuf[0:B_HALO, :] = bbuf[tm:tm + B_HALO, :]
    bb = z[:, off:off + B_WIDTH] * bacc

    y = (jnp.dot(a_out.astype(BF16), wout_ref[0:A_WIDTH, :], preferred_element_type=F32)
         + jnp.dot(bb.astype(BF16), wout_ref[A_WIDTH:A_WIDTH + B_WIDTH, :],
                   preferred_element_type=F32))
    o_ref[...] = x + mod[2:3] * y


def _conv_mix_call(x, mod, norm_g, w_in, caw, cab, lng, lnb, cbw, w_out, seg, *, layer, tm=512):
    even_in = w_in.shape[1]
    return pl.pallas_call(
        functools.partial(_conv_mix_kernel, tm=tm),
        grid=(SEQ // tm,),
        in_specs=[
            pl.BlockSpec((tm, D_MODEL), lambda i: (i, 0)),
            _resident((None, None, 3, D_MODEL), lambda i: (layer, 1, 0, 0)),
            _resident((None, 1, D_MODEL), lambda i: (layer, 0, 0)),
            _full((D_MODEL, even_in)),
            _full((A_CONV, A_WIDTH)),
            _full((1, A_WIDTH)),
            _full((1, A_WIDTH)),
            _full((1, A_WIDTH)),
            _full((B_CONV, B_WIDTH)),
            _full((A_WIDTH + B_WIDTH, D_MODEL)),
            _full((A_WIDTH, A_WIDTH)),
        ],
        out_specs=pl.BlockSpec((tm, D_MODEL), lambda i: (i, 0)),
        out_shape=jax.ShapeDtypeStruct((SEQ, D_MODEL), F32),
        scratch_shapes=[pltpu.VMEM((A_HALO + tm, A_WIDTH), F32),
                        pltpu.VMEM((B_HALO + tm, B_WIDTH), F32),
                        pltpu.VMEM((V7X_SUBLANES - 1, tm + V7X_SUBLANES, A_WIDTH), F32)],
        compiler_params=_params(1),
        name="conv_mixers",
    )(x, mod, norm_g, w_in, caw, cab, lng, lnb, cbw, w_out, seg)


def _pm_proj_kernel(x_ref, mod_ref, g_ref, wu_ref, wq_ref, wkvt_ref, pw_ref,
                    pb_ref, ps_ref, pooled_ref, qrow_ref, kv_ref, km_ref,
                    ubuf, hmean, *, tm):
    i = pl.program_id(0)

    @pl.when(i == 0)
    def _():
        ubuf[0:POOL_HALO, :] = jnp.zeros((POOL_HALO, C_WIDTH), F32)

    x = x_ref[...]
    h = _norm_mod(x, g_ref[...], mod_ref[...]).astype(BF16)
    u = jnp.dot(h, wu_ref[...], preferred_element_type=F32)
    qr = jnp.dot(h, wq_ref[...], preferred_element_type=F32)
    kvt = lax.dot_general(wkvt_ref[...], h, _NT, preferred_element_type=F32)

    ubuf[POOL_HALO:POOL_HALO + tm, :] = u
    t1 = (i * tm + 1 + lax.broadcasted_iota(I32, (tm, C_GROUP_DIM), 0)).astype(F32)
    for gi, w in enumerate(POOL_WINDOWS):
        c0 = gi * C_GROUP_DIM
        s = u[:, c0:c0 + C_GROUP_DIM]
        for k in range(1, w):
            s = s + ubuf[pl.ds(POOL_HALO - k, tm), c0:c0 + C_GROUP_DIM]
        pooled = s / jnp.minimum(t1, float(w)) - u[:, c0:c0 + C_GROUP_DIM]
        mixed = (jnp.dot(pooled.astype(BF16), pw_ref[gi], preferred_element_type=F32)
                 + pb_ref[:, c0:c0 + C_GROUP_DIM])
        pooled_ref[:, c0:c0 + C_GROUP_DIM] = (
            mixed * ps_ref[:, c0:c0 + C_GROUP_DIM]).astype(BF16)
    ubuf[0:POOL_HALO, :] = ubuf[tm:tm + POOL_HALO, :]

    lane = lax.broadcasted_iota(I32, (tm, HEAD_PAD), 1)
    tail = jnp.where(lane == ALIBI_COL, 1.0, 0.0)
    qs = qr * (HEAD_DIM ** -0.5)
    for pair in range(D_HEADS // 2):
        both = qs[:, pair * HEAD_PAD:(pair + 1) * HEAD_PAD]
        qrow_ref[2 * pair] = jnp.where(lane < HEAD_DIM, both, tail)
        qrow_ref[2 * pair + 1] = jnp.where(
            lane < HEAD_DIM, pltpu.roll(both, HEAD_DIM, axis=1), tail)
    per_tile = tm // MOBA_BLOCK
    kvb = kvt.astype(BF16)
    n_alibi = SLAB_K_ROWS - HEAD_DIM
    first = lax.broadcasted_iota(I32, (n_alibi, MOBA_BLOCK), 0) == 0
    koff = lax.broadcasted_iota(I32, (n_alibi, MOBA_BLOCK), 1).astype(F32)
    for hh in range(D_HEADS):
        alibi = jnp.where(first, koff * (2.0 ** -(hh + 1)), 0.0).astype(BF16)
        for sb in range(per_tile):
            keys = slice(sb * MOBA_BLOCK, (sb + 1) * MOBA_BLOCK)
            kv_ref[hh, sb, 0:HEAD_DIM, :] = kvb[hh * HEAD_DIM:(hh + 1) * HEAD_DIM, keys]
            kv_ref[hh, sb, HEAD_DIM:SLAB_K_ROWS, :] = alibi
            kv_ref[hh, sb, SLAB_K_ROWS:SLAB_ROWS, :] = kvb[
                D_WIDTH + hh * HEAD_DIM:D_WIDTH + (hh + 1) * HEAD_DIM, keys]

    for sb in range(per_tile):
        blk = h[sb * MOBA_BLOCK:(sb + 1) * MOBA_BLOCK, :].astype(F32)
        hmean[pl.ds(i * per_tile + sb, 1), :] = jnp.mean(blk, axis=0, keepdims=True)

    @pl.when(i == pl.num_programs(0) - 1)
    def _():
        hi, lo = _split(hmean[...])
        wkt = wkvt_ref[0:D_WIDTH, :]
        km_ref[...] = (lax.dot_general(hi, wkt, _NT, preferred_element_type=F32)
                       + lax.dot_general(lo, wkt, _NT, preferred_element_type=F32))


def _pm_proj_call(x, mod, norm_g, wu, wq, wkvt, pw, pb, ps, *, layer, tm=512):
    return pl.pallas_call(
        functools.partial(_pm_proj_kernel, tm=tm),
        grid=(SEQ // tm,),
        in_specs=[
            pl.BlockSpec((tm, D_MODEL), lambda i: (i, 0)),
            _resident((None, None, 3, D_MODEL), lambda i: (layer, 1, 0, 0)),
            _resident((None, 1, D_MODEL), lambda i: (layer, 0, 0)),
            _full((D_MODEL, C_WIDTH)),
            _full((D_MODEL, D_WIDTH)),
            _full((2 * D_WIDTH, D_MODEL)),
            _full((len(POOL_WINDOWS), C_GROUP_DIM, C_GROUP_DIM)),
            _full((1, C_WIDTH)),
            _full((1, C_WIDTH)),
        ],
        out_specs=[
            pl.BlockSpec((tm, C_WIDTH), lambda i: (i, 0)),
            pl.BlockSpec((D_HEADS, tm, HEAD_PAD), lambda i: (0, i, 0)),
            pl.BlockSpec((D_HEADS, tm // MOBA_BLOCK, SLAB_ROWS, MOBA_BLOCK),
                         lambda i: (0, i, 0, 0)),
            pl.BlockSpec((N_BLOCKS, D_WIDTH), lambda i: (0, 0)),
        ],
        out_shape=[
            jax.ShapeDtypeStruct((SEQ, C_WIDTH), BF16),
            jax.ShapeDtypeStruct((D_HEADS, SEQ, HEAD_PAD), F32),
            jax.ShapeDtypeStruct((D_HEADS, N_BLOCKS, SLAB_ROWS, MOBA_BLOCK), BF16),
            jax.ShapeDtypeStruct((N_BLOCKS, D_WIDTH), F32),
        ],
        scratch_shapes=[pltpu.VMEM((POOL_HALO + tm, C_WIDTH), F32),
                        pltpu.VMEM((N_BLOCKS, D_MODEL), F32)],
        compiler_params=_params(1),
        name="pool_qkv_proj",
    )(x, mod, norm_g, wu, wq, wkvt, pw, pb, ps)


def _block_rows(qs, slabs, mask, lse_add):
    n = qs[0].shape[0]
    kpad = jnp.zeros((HEAD_PAD - SLAB_K_ROWS, MOBA_BLOCK), BF16)
    s = jnp.concatenate(
        [jnp.dot(q, jnp.concatenate([slab[0:SLAB_K_ROWS, :], kpad], axis=0),
                 preferred_element_type=F32)
         for q, slab in zip(qs, slabs)], axis=0)
    if mask is not None:
        s = jnp.where(mask, s, NEG_BIG)
    m = jnp.max(s, axis=1, keepdims=True)
    p = jnp.exp(s - m).astype(BF16)
    vpad = jnp.zeros((HEAD_PAD - HEAD_DIM, MOBA_BLOCK), BF16)
    ones = jnp.ones((HEAD_PAD, MOBA_BLOCK), BF16)
    ot = jnp.concatenate(
        [lax.dot_general(p[u * n:(u + 1) * n],
                         jnp.concatenate([slab[SLAB_K_ROWS:SLAB_ROWS, :], vpad, ones], axis=0),
                         _NT, preferred_element_type=F32)
         for u, slab in enumerate(slabs)], axis=0)
    o, tot = ot[:, 0:HEAD_PAD], ot[:, HEAD_PAD:2 * HEAD_PAD]
    lse = m + jnp.log(tot)
    if lse_add is not None:
        lse = jnp.concatenate(
            [lse[u * n:(u + 1) * n] + add for u, add in enumerate(lse_add)], axis=0)
    low = lax.broadcasted_iota(I32, o.shape, 1) < HEAD_DIM
    return jnp.where(low, o / tot, lse)


def _onehot_pairs(sel_rows):
    blk = lax.broadcasted_iota(I32, (N_BLOCKS, MOBA_BLOCK), 0)
    return jnp.concatenate(
        [jnp.where(blk == sel_rows[s], 1.0, 0.0) for s in range(MOBA_TOPK)], axis=1)


def _gate_kernel(anchor_ref, qrow_ref, km_ref, tri_ref, sel_ref, rank_ref, cnt_ref,
                 *, heads, blocks):
    del anchor_ref
    n_pair = MOBA_TOPK * MOBA_BLOCK
    tiles = [(bb, a) for bb in range(blocks) for a in range(heads)]
    rows_of = lambda bb: slice(bb * MOBA_BLOCK, (bb + 1) * MOBA_BLOCK)
    qs = [qrow_ref[a, rows_of(bb), :].astype(BF16) for bb, a in tiles]

    kms = [_split(km_ref[a]) for a in range(heads)]
    gates = [lax.dot_general(kms[a][0], q, _NT, preferred_element_type=F32)
             + lax.dot_general(kms[a][1], q, _NT, preferred_element_type=F32)
             for (bb, a), q in zip(tiles, qs)]
    width = len(tiles) * MOBA_BLOCK
    blk = lax.broadcasted_iota(I32, (N_BLOCKS, width), 0)
    lane = lax.broadcasted_iota(I32, (1, width), 1)
    own = pl.program_id(0) * blocks + lane // (heads * MOBA_BLOCK)
    g = jnp.where(blk < own, jnp.concatenate(gates, axis=1), -jnp.inf)
    slots = []
    for s in range(MOBA_TOPK):
        best = jnp.max(g, axis=0, keepdims=True)
        idx = jnp.min(jnp.where(g == best, blk, N_BLOCKS), axis=0, keepdims=True)
        g = jnp.where(blk == idx, -jnp.inf, g)
        slots.append(jnp.where(s < own, idx, N_BLOCKS))

    onehots = []
    for t, (bb, a) in enumerate(tiles):
        sel_rows = [slot[:, t * MOBA_BLOCK:(t + 1) * MOBA_BLOCK] for slot in slots]
        sel_ref[a, :, rows_of(bb)] = jnp.concatenate(sel_rows, axis=0)
        onehots.append(_onehot_pairs(sel_rows))
    onehot = jnp.concatenate(onehots, axis=0)
    oh = onehot.astype(BF16)
    before = jnp.dot(oh, tri_ref[...], preferred_element_type=F32)
    hit = onehot * before
    ones = jnp.ones((8, n_pair), BF16)
    cnt = lax.dot_general(ones, oh, _NT, preferred_element_type=F32)
    for t, (bb, a) in enumerate(tiles):
        rank = jnp.sum(hit[t * N_BLOCKS:(t + 1) * N_BLOCKS], axis=0, keepdims=True).astype(I32)
        rank_ref[a, :, rows_of(bb)] = jnp.concatenate(
            [rank[:, s * MOBA_BLOCK:(s + 1) * MOBA_BLOCK] for s in range(MOBA_TOPK)], axis=0)
        cnt_ref[a, bb] = cnt[0:1, t * N_BLOCKS:(t + 1) * N_BLOCKS].astype(I32)


def _gate_call(anchor, qrows, kmean_h, tri, *, split, blocks=GATE_BLOCKS_PER_STEP):
    heads = SPLIT_HEADS
    rows = blocks * MOBA_BLOCK
    return pl.pallas_call(
        functools.partial(_gate_kernel, heads=heads, blocks=blocks),
        grid=(N_BLOCKS // blocks,),
        in_specs=[
            pl.BlockSpec(memory_space=pltpu.SMEM),
            pl.BlockSpec((heads, rows, HEAD_PAD), lambda b: (split, b, 0)),
            pl.BlockSpec((heads, N_BLOCKS, HEAD_PAD), lambda b: (split, 0, 0)),
            _full((MOBA_TOPK * MOBA_BLOCK, MOBA_TOPK * MOBA_BLOCK)),
        ],
        out_specs=[
            pl.BlockSpec((heads, MOBA_TOPK, rows), lambda b: (0, 0, b)),
            pl.BlockSpec((heads, MOBA_TOPK, rows), lambda b: (0, 0, b)),
            pl.BlockSpec((heads, blocks, 1, N_BLOCKS), lambda b: (0, b, 0, 0)),
        ],
        out_shape=[
            jax.ShapeDtypeStruct((heads, MOBA_TOPK, SEQ), I32),
            jax.ShapeDtypeStruct((heads, MOBA_TOPK, SEQ), I32),
            jax.ShapeDtypeStruct((heads, N_BLOCKS, 1, N_BLOCKS), I32),
        ],
        compiler_params=_params(1),
        name=f"moba_gate_{split}",
    )(anchor, qrows, kmean_h, tri)


def _own_block_kernel(anchor_ref, qrow_ref, kv_ref, own_ref, *, blocks):
    del anchor_ref
    tiles = [(bb, hh) for bb in range(blocks) for hh in range(D_HEADS)]
    rows_of = lambda bb: slice(bb * MOBA_BLOCK, (bb + 1) * MOBA_BLOCK)
    qs = [qrow_ref[hh, rows_of(bb), :].astype(BF16) for bb, hh in tiles]
    n_rows = len(tiles) * MOBA_BLOCK
    qi = lax.broadcasted_iota(I32, (n_rows, MOBA_BLOCK), 0) % MOBA_BLOCK
    ki = lax.broadcasted_iota(I32, (n_rows, MOBA_BLOCK), 1)
    rows = _block_rows(qs, [kv_ref.at[hh, bb] for bb, hh in tiles], ki <= qi, None)
    for t, (bb, hh) in enumerate(tiles):
        own_ref[hh, rows_of(bb), :] = rows[t * MOBA_BLOCK:(t + 1) * MOBA_BLOCK]


def _own_block_call(anchor, qrows, kv, *, blocks=OWN_BLOCKS_PER_STEP):
    rows = blocks * MOBA_BLOCK
    return pl.pallas_call(
        functools.partial(_own_block_kernel, blocks=blocks),
        grid=(N_BLOCKS // blocks,),
        in_specs=[
            pl.BlockSpec(memory_space=pltpu.SMEM),
            pl.BlockSpec((D_HEADS, rows, HEAD_PAD), lambda b: (0, b, 0)),
            pl.BlockSpec((D_HEADS, blocks, SLAB_ROWS, MOBA_BLOCK), lambda b: (0, b, 0, 0)),
        ],
        out_specs=pl.BlockSpec((D_HEADS, rows, HEAD_PAD), lambda b: (0, b, 0)),
        out_shape=jax.ShapeDtypeStruct((D_HEADS, SEQ, HEAD_PAD), F32),
        compiler_params=_params(1),
        name="moba_own_block",
    )(anchor, qrows, kv)


def _route_pos_kernel(sel_ref, rank_ref, base_ref, pos_ref, *, blocks):
    lane = lax.broadcasted_iota(I32, (1, MOBA_BLOCK), 1)
    for bb in range(blocks):
        qs = slice(bb * MOBA_BLOCK, (bb + 1) * MOBA_BLOCK)
        rows = [[] for _ in range(MOBA_TOPK)]
        for a in range(SPLIT_HEADS):
            sel = sel_ref[a, :, qs]
            onehot = _onehot_pairs([sel[s:s + 1, :] for s in range(MOBA_TOPK)]).astype(BF16)
            dig = jnp.dot(base_ref[a, bb].astype(BF16), onehot,
                          preferred_element_type=F32)
            base = (dig[0:1] * float(DIGIT * DIGIT) + dig[1:2] * float(DIGIT)
                    + dig[2:3]).astype(I32)
            for s in range(MOBA_TOPK):
                p = base[:, s * MOBA_BLOCK:(s + 1) * MOBA_BLOCK] + rank_ref[a, s:s + 1, qs]
                rows[s].append(
                    jnp.where(sel[s:s + 1, :] < N_BLOCKS, p, TRASH_ROW0 + lane % SC_WINDOW))
        for s in range(MOBA_TOPK):
            pos_ref[s, :, qs] = jnp.concatenate(rows[s], axis=0)


def _route_pos_call(sel, rank, base_digits, *, blocks=8):
    width = blocks * MOBA_BLOCK
    return pl.pallas_call(
        functools.partial(_route_pos_kernel, blocks=blocks),
        grid=(N_BLOCKS // blocks,),
        in_specs=[
            pl.BlockSpec((SPLIT_HEADS, MOBA_TOPK, width), lambda b: (0, 0, b)),
            pl.BlockSpec((SPLIT_HEADS, MOBA_TOPK, width), lambda b: (0, 0, b)),
            pl.BlockSpec((SPLIT_HEADS, blocks, 8, N_BLOCKS), lambda b: (0, b, 0, 0)),
        ],
        out_specs=pl.BlockSpec((MOBA_TOPK, SPLIT_HEADS, width), lambda b: (0, 0, b)),
        out_shape=jax.ShapeDtypeStruct((MOBA_TOPK, SPLIT_HEADS, SEQ), I32),
        compiler_params=_params(1),
        name="moba_route_pos",
    )(sel, rank, base_digits)


def _route_tables(cnt, split):
    cnt = cnt.reshape(SPLIT_HEADS, N_BLOCKS, N_BLOCKS)
    tiles = (cnt.sum(axis=1) + ROUTE_TILE - 1) // ROUTE_TILE
    tiles_flat = tiles.reshape(N_GROUPS)
    tile0 = jnp.cumsum(tiles_flat) - tiles_flat
    n_tiles = tiles_flat.sum().astype(I32)
    within = jnp.cumsum(cnt, axis=1) - cnt
    base = tile0.reshape(SPLIT_HEADS, 1, N_BLOCKS) * ROUTE_TILE + within
    digits = jnp.stack([base // (DIGIT * DIGIT), (base // DIGIT) % DIGIT, base % DIGIT],
                       axis=2).astype(F32)
    digits = jnp.pad(digits, ((0, 0), (0, 0), (0, 8 - 3), (0, 0)))
    t = jnp.arange(MAX_TILES, dtype=I32)
    ended = (tile0 + tiles_flat)[None, :] <= t[:, None]
    group_of_tile = jnp.minimum(ended.sum(axis=1).astype(I32), N_GROUPS - 1)
    tile_h = group_of_tile // N_BLOCKS
    head = split * SPLIT_HEADS + tile_h
    tile_step = lax.bitcast_convert_type((127 + 7 - head) << 23, F32)
    return digits, tile_h, group_of_tile % N_BLOCKS, n_tiles.reshape(1), tile_step


def _sc_mesh():
    return plsc.VectorSubcoreMesh(core_axis_name="core", subcore_axis_name="subcore")


def _dispatch_call(rows, pos_by_slot, *, split):
    n_rows = SPLIT_HEADS * SEQ
    first_window = split * n_rows // SC_WINDOW

    @functools.partial(
        pl.kernel, mesh=_sc_mesh(), scratch_types=[],
        out_type=jax.ShapeDtypeStruct((ROUTE_ROWS, HEAD_PAD), F32))
    def dispatch(x_hbm, i0_hbm, i1_hbm, i2_hbm, o_hbm):
        def body(x_vmem, i0_vmem, i1_vmem, i2_vmem):
            pltpu.sync_copy(x_vmem, o_hbm.at[i0_vmem.at[0]])
            pltpu.sync_copy(x_vmem, o_hbm.at[i1_vmem.at[0]])
            pltpu.sync_copy(x_vmem, o_hbm.at[i2_vmem.at[0]])

        idx_spec = pl.BlockSpec((1, SC_WINDOW), lambda i: (0, i))
        pltpu.emit_pipeline(
            body,
            grid=(n_rows // SC_WINDOW,),
            in_specs=[pl.BlockSpec((SC_WINDOW, HEAD_PAD), lambda i: (first_window + i, 0)),
                      idx_spec, idx_spec, idx_spec],
            out_specs=[],
            core_axis_name=("core", "subcore"),
            dimension_semantics=(pltpu.PARALLEL,),
        )(x_hbm, i0_hbm, i1_hbm, i2_hbm)

    return dispatch(rows, *pos_by_slot)


def _collect_call(table, idx):
    n_rows = idx.shape[1]

    @functools.partial(
        pl.kernel, mesh=_sc_mesh(), scratch_types=[],
        out_type=jax.ShapeDtypeStruct((n_rows, HEAD_PAD), F32))
    def collect(x_hbm, i_hbm, o_hbm):
        def body(i_vmem, o_vmem):
            pltpu.sync_copy(x_hbm.at[i_vmem.at[0]], o_vmem)

        pltpu.emit_pipeline(
            body,
            grid=(n_rows // SC_WINDOW,),
            in_specs=[pl.BlockSpec((1, SC_WINDOW), lambda i: (0, i))],
            out_specs=[pl.BlockSpec((SC_WINDOW, HEAD_PAD), lambda i: (i, 0))],
            core_axis_name=("core", "subcore"),
            dimension_semantics=(pltpu.PARALLEL,),
        )(i_hbm, o_hbm)

    return collect(table, idx)


def _routed_attn_kernel(th_ref, tj_ref, nt_ref, step_ref, q_ref, *refs, tiles):
    kv_refs, o_ref = refs[0:tiles], refs[tiles]
    t0 = pl.program_id(0) * tiles

    @pl.when(t0 < nt_ref[0])
    def _():
        qs = [q_ref[u * ROUTE_TILE:(u + 1) * ROUTE_TILE, :].astype(BF16) for u in range(tiles)]
        adds = [step_ref[t0 + u] * tj_ref[t0 + u].astype(F32) for u in range(tiles)]
        o_ref[...] = _block_rows(qs, kv_refs, None, adds)


def _routed_attn_call(tile_h, tile_j, n_tiles, tile_step, routed_q, kv,
                      *, split, tiles=ROUTE_TILES_PER_STEP):
    live = lambda s, th, tj, nt: (jnp.minimum(s, (nt[0] + tiles - 1) // tiles), 0)
    head0 = split * SPLIT_HEADS

    def kv_spec(u):
        return pl.BlockSpec(
            (None, None, SLAB_ROWS, MOBA_BLOCK),
            lambda s, th, tj, nt: (head0 + th[s * tiles + u], tj[s * tiles + u], 0, 0))

    grid_spec = pltpu.PrefetchScalarGridSpec(
        num_scalar_prefetch=3,
        grid=(MAX_TILES // tiles,),
        in_specs=([pl.BlockSpec(memory_space=pltpu.SMEM),
                   pl.BlockSpec((tiles * ROUTE_TILE, HEAD_PAD), live)]
                  + [kv_spec(u) for u in range(tiles)]),
        out_specs=pl.BlockSpec((tiles * ROUTE_TILE, HEAD_PAD), live),
    )
    return pl.pallas_call(
        functools.partial(_routed_attn_kernel, tiles=tiles),
        grid_spec=grid_spec,
        out_shape=jax.ShapeDtypeStruct((ROUTE_ROWS, HEAD_PAD), F32),
        compiler_params=_params(1),
        name=f"moba_routed_attn_{split}",
    )(tile_h, tile_j, n_tiles, tile_step, routed_q, *([kv] * tiles))


def _pm_out_kernel(slope_ref, x_ref, mod_ref, pooled_ref, own_ref, *refs, blocks):
    got_refs = refs[0:N_SPLITS]
    wp_ref, wa_ref, o_ref = refs[N_SPLITS:]
    low = lax.broadcasted_iota(I32, (MOBA_BLOCK, HEAD_PAD), 1) < HEAD_DIM

    def lse_of(part):
        return jnp.where(low, pltpu.roll(part, HEAD_DIM, axis=1), part)

    y = jnp.dot(pooled_ref[...], wp_ref[...], preferred_element_type=F32)
    for hh in range(D_HEADS):
        got_ref, a = got_refs[hh // SPLIT_HEADS], hh % SPLIT_HEADS
        atts = []
        for bb in range(blocks):
            own = pl.program_id(0) * blocks + bb
            rows = slice(bb * MOBA_BLOCK, (bb + 1) * MOBA_BLOCK)
            own_shift = slope_ref[hh] * (own * MOBA_BLOCK).astype(F32)
            parts = [own_ref[hh, rows, :]]
            lses = [lse_of(parts[0])]
            for s in range(MOBA_TOPK):
                valid = s < own
                part = got_ref[s, a, rows, :]
                parts.append(jnp.where(valid, part, 0.0))
                lses.append(jnp.where(valid, lse_of(part) - own_shift, NEG_BIG))
            top = functools.reduce(jnp.maximum, lses)
            num = jnp.zeros((MOBA_BLOCK, HEAD_PAD), F32)
            den = jnp.zeros((MOBA_BLOCK, HEAD_PAD), F32)
            for part, ls in zip(parts, lses):
                w = jnp.exp(ls - top)
                num = num + w * part
                den = den + w
            atts.append((num / den).astype(BF16))
        y = y + jnp.dot(jnp.concatenate(atts, axis=0), wa_ref[hh], preferred_element_type=F32)
    o_ref[...] = x_ref[...] + mod_ref[2:3, :] * y


def _pm_out_call(slopes, x, mod, pooled, own_part, got_parts, wp, wa, *, layer, blocks=2):
    tm = blocks * MOBA_BLOCK
    return pl.pallas_call(
        functools.partial(_pm_out_kernel, blocks=blocks),
        grid=(SEQ // tm,),
        in_specs=[
            pl.BlockSpec(memory_space=pltpu.SMEM),
            pl.BlockSpec((tm, D_MODEL), lambda i: (i, 0)),
            _resident((None, None, 3, D_MODEL), lambda i: (layer, 1, 0, 0)),
            pl.BlockSpec((tm, C_WIDTH), lambda i: (i, 0)),
            pl.BlockSpec((D_HEADS, tm, HEAD_PAD), lambda i: (0, i, 0)),
        ]
        + [pl.BlockSpec((MOBA_TOPK, SPLIT_HEADS, tm, HEAD_PAD), lambda i: (0, 0, i, 0))] * N_SPLITS
        + [
            _full((C_WIDTH, D_MODEL)),
            _full((D_HEADS, HEAD_PAD, D_MODEL)),
        ],
        out_specs=pl.BlockSpec((tm, D_MODEL), lambda i: (i, 0)),
        out_shape=jax.ShapeDtypeStruct((SEQ, D_MODEL), F32),
        compiler_params=_params(1),
        name="pool_moba_out",
    )(slopes, x, mod, pooled, own_part, *got_parts, wp, wa)


def _head_pad(w, axis):
    shape = list(w.shape)
    shape[axis:axis + 1] = [D_HEADS, HEAD_DIM]
    pad = [(0, 0)] * len(shape)
    pad[axis + 1] = (0, HEAD_PAD - HEAD_DIM)
    out = jnp.pad(w.reshape(shape), pad)
    shape[axis:axis + 2] = [D_HEADS * HEAD_PAD]
    return out.reshape(shape)


def kernel(x, c, ada_w, ada_b, ffn_norm, ffn_w_gate, ffn_w_up, ffn_w_down, mix_norm, conv_w_in,
           conv_a_w, conv_a_b, conv_a_ln_g, conv_a_ln_b, conv_b_w, conv_w_out, pm_w_in, pool_w,
           pool_b, pool_scale, pm_w_out, final_norm):
    assert x.shape == (1, SEQ, D_MODEL) and c.shape == (1, D_MODEL)
    xs = x.reshape(SEQ, D_MODEL)
    mod = _mod_call(c, ada_w, ada_b)

    wg = ffn_w_gate.astype(BF16)
    wu = ffn_w_up.astype(BF16)
    wd = ffn_w_down.astype(BF16)
    ffn_g = ffn_norm.reshape(DEPTH, 2, 1, D_MODEL)
    mix_g = mix_norm.reshape(DEPTH, 1, D_MODEL)
    fin = final_norm.reshape(1, D_MODEL)
    ffn = functools.partial(_ffn_call, mod=mod, norm_g=ffn_g, wg=wg, wu=wu, wd=wd, fin=fin)

    xs = ffn(xs, layer=0, sub=0, final=False)
    grp = np.arange(A_WIDTH) // (A_WIDTH // A_GROUPS)
    seg = jnp.asarray((grp[:, None] == grp[None, :]) / (A_WIDTH // A_GROUPS), dtype=BF16)
    xs = _conv_mix_call(
        xs, mod, mix_g, conv_w_in[0].astype(BF16), conv_a_w[0], conv_a_b[0].reshape(1, A_WIDTH),
        conv_a_ln_g[0].reshape(1, A_WIDTH), conv_a_ln_b[0].reshape(1, A_WIDTH), conv_b_w[0],
        conv_w_out[0].astype(BF16), seg, layer=0)
    xs = ffn(xs, layer=0, sub=1, final=False)

    xs = ffn(xs, layer=1, sub=0, final=False)
    w_in = pm_w_in[0].astype(BF16)
    w_u = w_in[:, 0:C_WIDTH]
    w_q = w_in[:, C_WIDTH:C_WIDTH + D_WIDTH]
    w_kvt = w_in[:, C_WIDTH + D_WIDTH:C_WIDTH + 3 * D_WIDTH].T
    pooled, qrows, kv, kmean = _pm_proj_call(
        xs, mod, mix_g, w_u, w_q, w_kvt, pool_w[0].astype(BF16),
        pool_b[0].reshape(1, C_WIDTH), pool_scale[0].reshape(1, C_WIDTH), layer=1)
    kmean_h = _head_pad(kmean, 1).reshape(N_BLOCKS, D_HEADS, HEAD_PAD).transpose(1, 0, 2)

    n_pair = MOBA_TOPK * MOBA_BLOCK
    tri = jnp.asarray(np.arange(n_pair)[:, None] < np.arange(n_pair)[None, :], dtype=BF16)
    q_flat = qrows.reshape(D_HEADS * SEQ, HEAD_PAD)
    got_parts = []
    anchor = jnp.zeros((1,), I32)
    for split in range(N_SPLITS):
        sel, rank, cnt = _gate_call(anchor, qrows, kmean_h, tri, split=split)
        base_digits, tile_h, tile_j, n_tiles, tile_step = _route_tables(cnt, split)
        pos = _route_pos_call(sel, rank, base_digits)
        anchor = pos[0, 0, 0:1]
        pos_by_slot = [pos[s].reshape(1, SPLIT_HEADS * SEQ) for s in range(MOBA_TOPK)]
        routed_q = _dispatch_call(q_flat, pos_by_slot, split=split)
        routed_parts = _routed_attn_call(tile_h, tile_j, n_tiles, tile_step, routed_q, kv,
                                         split=split)
        got = _collect_call(routed_parts, pos.reshape(1, N_PAIRS))
        got_parts.append(got.reshape(MOBA_TOPK, SPLIT_HEADS, SEQ, HEAD_PAD))
    own_part = _own_block_call(routed_parts[0, 0:1], qrows, kv)

    slopes =jnp.asarray(2.0 ** (-8.0 * np.arange(1, D_HEADS + 1) / D_HEADS), dtype=F32)
    w_out = pm_w_out[0].astype(BF16)
    w_att = _head_pad(w_out[C_WIDTH:], 0).reshape(D_HEADS, HEAD_PAD, D_MODEL)
    xs = _pm_out_call(slopes, xs, mod, pooled, own_part, got_parts, w_out[0:C_WIDTH], w_att,
                      layer=1)
    xs = ffn(xs, layer=1, sub=1, final=True)
    return xs.reshape(1, SEQ, D_MODEL)
```

```python
import functools

import numpy as np
import jax
import jax.numpy as jnp
from jax import lax
from jax.experimental import pallas as pl
from jax.experimental.pallas import tpu as pltpu
from jax.experimental.pallas import tpu_sc as plsc

D_MODEL = 1024
SEQ = 16384
DEPTH = 2
N_SUBLAYERS = 3
D_FF = 2816
EPS = 1e-6

A_WIDTH = 512
A_GROUPS = 8
A_CONV = 31
B_WIDTH = 512
B_CONV = 3

POOL_WINDOWS = (2, 4, 8, 16)
C_WIDTH = 512
C_GROUP_DIM = 128
D_HEADS = 8
HEAD_DIM = 64
D_WIDTH = 512
MOBA_BLOCK = 256
MOBA_TOPK = 3
N_BLOCKS = SEQ // MOBA_BLOCK

V7X_SUBLANES = 8
V7X_BF16_SUBLANES = 16
V7X_VMEM_LIMIT_BYTES = 56 * 1024 * 1024

A_HALO = 32
B_HALO = 8
POOL_HALO = 16
NEG_BIG = -1e30

HEAD_PAD = 128
ALIBI_COL = HEAD_DIM
SLAB_K_ROWS = HEAD_DIM + V7X_BF16_SUBLANES
SLAB_ROWS = SLAB_K_ROWS + HEAD_DIM

SPLIT_HEADS = 4
N_SPLITS = D_HEADS // SPLIT_HEADS
ROUTE_TILE = 256
ROUTE_TILES_PER_STEP = 32
GATE_BLOCKS_PER_STEP = 4
N_PAIRS = SPLIT_HEADS * SEQ * MOBA_TOPK
N_GROUPS = SPLIT_HEADS * N_BLOCKS
MAX_TILES = N_PAIRS // ROUTE_TILE + N_GROUPS
TRASH_ROW0 = MAX_TILES * ROUTE_TILE
ROUTE_ROWS = (MAX_TILES + ROUTE_TILES_PER_STEP) * ROUTE_TILE
SC_WINDOW = 128
DIGIT = 128
assert MAX_TILES % ROUTE_TILES_PER_STEP == 0 and N_BLOCKS % GATE_BLOCKS_PER_STEP == 0
assert ROUTE_ROWS <= DIGIT ** 3 and MOBA_BLOCK <= ROUTE_TILE

BF16 = jnp.bfloat16
F32 = jnp.float32
I32 = jnp.int32


def _params(n_axes):
    return pltpu.CompilerParams(
        dimension_semantics=("arbitrary",) * n_axes,
        vmem_limit_bytes=V7X_VMEM_LIMIT_BYTES)


def _resident(block_shape, index_map):
    return pl.BlockSpec(block_shape, index_map, pipeline_mode=pl.Buffered(1))


def _full(shape):
    return _resident(shape, lambda *_: (0,) * len(shape))


def _sigmoid(v):
    return 1.0 / (1.0 + jnp.exp(-v))


def _norm_mod(x, g, mod):
    r = lax.rsqrt(jnp.mean(x * x, axis=-1, keepdims=True) + EPS)
    return (x * r) * (g * (1.0 + mod[1:2])) + mod[0:1]


def _split(v):
    hi = v.astype(BF16)
    return hi, (v - hi.astype(F32)).astype(BF16)


def _split_dot(v, w):
    hi, lo = _split(v)
    return (jnp.dot(hi, w, preferred_element_type=F32)
            + jnp.dot(lo, w, preferred_element_type=F32))


_NT = (((1,), (1,)), ((), ()))


def _mod_kernel(c_ref, w_ref, b_ref, o_ref):
    c = c_ref[...]
    cond = c * _sigmoid(c)
    o_ref[0] = jnp.sum(w_ref[0] * cond, axis=0, keepdims=True) + b_ref[0]


def _mod_call(c, ada_w, ada_b):
    n_out = N_SUBLAYERS * 3 * D_MODEL
    tn = D_MODEL
    out = pl.pallas_call(
        _mod_kernel,
        grid=(DEPTH, n_out // tn),
        in_specs=[
            pl.BlockSpec((D_MODEL, 1), lambda l, j: (0, 0)),
            pl.BlockSpec((1, D_MODEL, tn), lambda l, j: (l, 0, j)),
            pl.BlockSpec((1, 1, tn), lambda l, j: (l, 0, j)),
        ],
        out_specs=pl.BlockSpec((1, 1, tn), lambda l, j: (l, 0, j)),
        out_shape=jax.ShapeDtypeStruct((DEPTH, 1, n_out), F32),
        compiler_params=_params(2),
        name="adaln_mod",
    )(c.reshape(D_MODEL, 1), ada_w, ada_b.reshape(DEPTH, 1, n_out))
    return out.reshape(DEPTH, N_SUBLAYERS, 3, D_MODEL)


def _ffn_kernel(x_ref, mod_ref, g_ref, wg_ref, wu_ref, wd_ref, fin_ref, *refs, final):
    o_ref = refs[len(refs) // 2]
    for src_ref, dst_ref in zip(refs[:len(refs) // 2], refs[len(refs) // 2 + 1:]):
        dst_ref[...] = src_ref[...].astype(BF16)
    x = x_ref[...]
    mod = mod_ref[...]
    h = _norm_mod(x, g_ref[...], mod).astype(BF16)
    gate = jnp.dot(h, wg_ref[...], preferred_element_type=F32)
    up = jnp.dot(h, wu_ref[...], preferred_element_type=F32)
    act = (gate * _sigmoid(gate) * up).astype(BF16)
    y = jnp.dot(act, wd_ref[...], preferred_element_type=F32)
    xn = x + (0.5 * mod[2:3]) * y
    if final:
        r = lax.rsqrt(jnp.mean(xn * xn, axis=-1, keepdims=True) + EPS)
        xn = xn * r * fin_ref[...]
    o_ref[...] = xn


def _ffn_call(x, mod, norm_g, weights, fin, next_weights, *, layer, sub, final, tm=512):
    mod_sub = 0 if sub == 0 else 2
    steps = SEQ // tm
    in_specs = [
        pl.BlockSpec((tm, D_MODEL), lambda i: (i, 0)),
        _resident((None, None, 3, D_MODEL), lambda i: (layer, mod_sub, 0, 0)),
        _resident((None, None, 1, D_MODEL), lambda i: (layer, sub, 0, 0)),
        _full((D_MODEL, D_FF)),
        _full((D_MODEL, D_FF)),
        _full((D_FF, D_MODEL)),
        _full((1, D_MODEL)),
    ]
    out_specs = [pl.BlockSpec((tm, D_MODEL), lambda i: (i, 0))]
    out_shape = [jax.ShapeDtypeStruct((SEQ, D_MODEL), F32)]
    operands = [x, mod, norm_g, *weights, fin]
    if next_weights is not None:
        (nl, ns), *stacks = next_weights
        for w in stacks:
            n_rows, n_cols = w.shape[2:]
            share = 1 if n_rows % (steps * V7X_BF16_SUBLANES) == 0 else 2
            rows = n_rows * share // steps
            assert n_rows * share % steps == 0 and rows % V7X_BF16_SUBLANES == 0
            in_specs.append(pl.BlockSpec((None, None, rows, n_cols),
                                         lambda i, share=share: (nl, ns, i // share, 0)))
            out_specs.append(pl.BlockSpec((rows, n_cols), lambda i, share=share: (i // share, 0)))
            out_shape.append(jax.ShapeDtypeStruct((n_rows, n_cols), BF16))
            operands.append(w)
    out = pl.pallas_call(
        functools.partial(_ffn_kernel, final=final),
        grid=(steps,),
        in_specs=in_specs,
        out_specs=out_specs,
        out_shape=out_shape,
        compiler_params=_params(1),
        name=f"ffn_l{layer}s{sub}",
    )(*operands)
    return out[0], tuple(out[1:])


def _conv_mix_kernel(x_ref, mod_ref, g_ref, win_ref, caw_ref, cab_ref, lng_ref, lnb_ref,
                     cbw_ref, wout_ref, seg_ref, o_ref, abuf, bbuf, sbuf, *, tm):
    i = pl.program_id(0)

    @pl.when(i == 0)
    def _():
        abuf[0:A_HALO, :] = jnp.zeros((A_HALO, A_WIDTH), F32)
        bbuf[0:B_HALO, :] = jnp.zeros((B_HALO, B_WIDTH), F32)

    x = x_ref[...]
    mod = mod_ref[...]
    h = _norm_mod(x, g_ref[...], mod).astype(BF16)
    z = jnp.dot(h, win_ref[...], preferred_element_type=F32)

    a = z[:, 0:A_WIDTH] * _sigmoid(z[:, A_WIDTH:2 * A_WIDTH])
    abuf[A_HALO:A_HALO + tm, :] = a
    acc = jnp.zeros((tm, A_WIDTH), F32) + cab_ref[...]
    ext = tm + V7X_SUBLANES
    for r in range(V7X_SUBLANES):
        part = None
        for q in range((A_CONV - 1 - r) // V7X_SUBLANES + 1):
            k = A_CONV - 1 - (V7X_SUBLANES * q + r)
            term = caw_ref[k:k + 1, :] * abuf[pl.ds(A_HALO - V7X_SUBLANES * (q + 1), ext), :]
            part = term if part is None else part + term
        if r == 0:
            acc = acc + part[V7X_SUBLANES:, :]
        else:
            sbuf[r - 1] = part
            acc = acc + sbuf[r - 1, pl.ds(V7X_SUBLANES - r, tm), :]
    abuf[0:A_HALO, :] = abuf[tm:tm + A_HALO, :]
    seg = seg_ref[...]
    mu = _split_dot(acc, seg)
    d = acc - mu
    var = _split_dot(d * d, seg)
    yn = d * lax.rsqrt(var + EPS) * lng_ref[...] + lnb_ref[...]
    a_out = yn * _sigmoid(yn)

    off = 2 * A_WIDTH
    cv = z[:, off + B_WIDTH:off + 2 * B_WIDTH] * z[:, off + 2 * B_WIDTH:off + 3 * B_WIDTH]
    bbuf[B_HALO:B_HALO + tm, :] = cv
    bacc = jnp.zeros((tm, B_WIDTH), F32)
    for k in range(B_CONV):
        bacc = bacc + cbw_ref[k:k + 1, :] * bbuf[pl.ds(B_HALO - (B_CONV - 1) + k, tm), :]
    bbuf[0:B_HALO, :] = bbuf[tm:tm + B_HALO, :]
    bb = z[:, off:off + B_WIDTH] * bacc

    y = (jnp.dot(a_out.astype(BF16), wout_ref[0:A_WIDTH, :], preferred_element_type=F32)
         + jnp.dot(bb.astype(BF16), wout_ref[A_WIDTH:A_WIDTH + B_WIDTH, :],
                   preferred_element_type=F32))
    o_ref[...] = x + mod[2:3] * y


def _conv_mix_call(x, mod, norm_g, w_in, caw, cab, lng, lnb, cbw, w_out, seg, *, layer, tm=512):
    even_in = w_in.shape[1]
    return pl.pallas_call(
        functools.partial(_conv_mix_kernel, tm=tm),
        grid=(SEQ // tm,),
        in_specs=[
            pl.BlockSpec((tm, D_MODEL), lambda i: (i, 0)),
            _resident((None, None, 3, D_MODEL), lambda i: (layer, 1, 0, 0)),
            _resident((None, 1, D_MODEL), lambda i: (layer, 0, 0)),
            _full((D_MODEL, even_in)),
            _full((A_CONV, A_WIDTH)),
            _full((1, A_WIDTH)),
            _full((1, A_WIDTH)),
            _full((1, A_WIDTH)),
            _full((B_CONV, B_WIDTH)),
            _full((A_WIDTH + B_WIDTH, D_MODEL)),
            _full((A_WIDTH, A_WIDTH)),
        ],
        out_specs=pl.BlockSpec((tm, D_MODEL), lambda i: (i, 0)),
        out_shape=jax.ShapeDtypeStruct((SEQ, D_MODEL), F32),
        scratch_shapes=[pltpu.VMEM((A_HALO + tm, A_WIDTH), F32),
                        pltpu.VMEM((B_HALO + tm, B_WIDTH), F32),
                        pltpu.VMEM((V7X_SUBLANES - 1, tm + V7X_SUBLANES, A_WIDTH), F32)],
        compiler_params=_params(1),
        name="conv_mixers",
    )(x, mod, norm_g, w_in, caw, cab, lng, lnb, cbw, w_out, seg)


def _pm_proj_kernel(x_ref, mod_ref, g_ref, wu_ref, wq_ref, wkvt_ref, pw_ref,
                    pb_ref, ps_ref, pooled_ref, qrow_ref, kv_ref, km_ref,
                    ubuf, hmean, *, tm):
    i = pl.program_id(0)

    @pl.when(i == 0)
    def _():
        ubuf[0:POOL_HALO, :] = jnp.zeros((POOL_HALO, C_WIDTH), F32)

    x = x_ref[...]
    h = _norm_mod(x, g_ref[...], mod_ref[...]).astype(BF16)
    u = jnp.dot(h, wu_ref[...], preferred_element_type=F32)
    qr = jnp.dot(h, wq_ref[...], preferred_element_type=F32)
    kvt = lax.dot_general(wkvt_ref[...], h, _NT, preferred_element_type=F32)

    ubuf[POOL_HALO:POOL_HALO + tm, :] = u
    t1 = (i * tm + 1 + lax.broadcasted_iota(I32, (tm, C_GROUP_DIM), 0)).astype(F32)
    for gi, w in enumerate(POOL_WINDOWS):
        c0 = gi * C_GROUP_DIM
        s = u[:, c0:c0 + C_GROUP_DIM]
        for k in range(1, w):
            s = s + ubuf[pl.ds(POOL_HALO - k, tm), c0:c0 + C_GROUP_DIM]
        pooled = s / jnp.minimum(t1, float(w)) - u[:, c0:c0 + C_GROUP_DIM]
        mixed = (jnp.dot(pooled.astype(BF16), pw_ref[gi], preferred_element_type=F32)
                 + pb_ref[:, c0:c0 + C_GROUP_DIM])
        pooled_ref[:, c0:c0 + C_GROUP_DIM] = (
            mixed * ps_ref[:, c0:c0 + C_GROUP_DIM]).astype(BF16)
    ubuf[0:POOL_HALO, :] = ubuf[tm:tm + POOL_HALO, :]

    lane = lax.broadcasted_iota(I32, (tm, HEAD_PAD), 1)
    tail = jnp.where(lane == ALIBI_COL, 1.0, 0.0)
    qs = qr * (HEAD_DIM ** -0.5)
    for pair in range(D_HEADS // 2):
        both = qs[:, pair * HEAD_PAD:(pair + 1) * HEAD_PAD]
        qrow_ref[2 * pair] = jnp.where(lane < HEAD_DIM, both, tail)
        qrow_ref[2 * pair + 1] = jnp.where(
            lane < HEAD_DIM, pltpu.roll(both, HEAD_DIM, axis=1), tail)
    per_tile = tm // MOBA_BLOCK
    kvb = kvt.astype(BF16)
    n_alibi = SLAB_K_ROWS - HEAD_DIM
    first = lax.broadcasted_iota(I32, (n_alibi, MOBA_BLOCK), 0) == 0
    koff = lax.broadcasted_iota(I32, (n_alibi, MOBA_BLOCK), 1).astype(F32)
    for hh in range(D_HEADS):
        alibi = jnp.where(first, koff * (2.0 ** -(hh + 1)), 0.0).astype(BF16)
        for sb in range(per_tile):
            keys = slice(sb * MOBA_BLOCK, (sb + 1) * MOBA_BLOCK)
            kv_ref[hh, sb, 0:HEAD_DIM, :] = kvb[hh * HEAD_DIM:(hh + 1) * HEAD_DIM, keys]
            kv_ref[hh, sb, HEAD_DIM:SLAB_K_ROWS, :] = alibi
            kv_ref[hh, sb, SLAB_K_ROWS:SLAB_ROWS, :] = kvb[
                D_WIDTH + hh * HEAD_DIM:D_WIDTH + (hh + 1) * HEAD_DIM, keys]

    for sb in range(per_tile):
        blk = h[sb * MOBA_BLOCK:(sb + 1) * MOBA_BLOCK, :].astype(F32)
        hmean[pl.ds(i * per_tile + sb, 1), :] = jnp.mean(blk, axis=0, keepdims=True)

    @pl.when(i == pl.num_programs(0) - 1)
    def _():
        hi, lo = _split(hmean[...])
        wkt = wkvt_ref[0:D_WIDTH, :]
        km_ref[...] = (lax.dot_general(hi, wkt, _NT, preferred_element_type=F32)
                       + lax.dot_general(lo, wkt, _NT, preferred_element_type=F32))


def _pm_proj_call(x, mod, norm_g, wu, wq, wkvt, pw, pb, ps, *, layer, tm=512):
    return pl.pallas_call(
        functools.partial(_pm_proj_kernel, tm=tm),
        grid=(SEQ // tm,),
        in_specs=[
            pl.BlockSpec((tm, D_MODEL), lambda i: (i, 0)),
            _resident((None, None, 3, D_MODEL), lambda i: (layer, 1, 0, 0)),
            _resident((None, 1, D_MODEL), lambda i: (layer, 0, 0)),
            _full((D_MODEL, C_WIDTH)),
            _full((D_MODEL, D_WIDTH)),
            _full((2 * D_WIDTH, D_MODEL)),
            _full((len(POOL_WINDOWS), C_GROUP_DIM, C_GROUP_DIM)),
            _full((1, C_WIDTH)),
            _full((1, C_WIDTH)),
        ],
        out_specs=[
            pl.BlockSpec((tm, C_WIDTH), lambda i: (i, 0)),
            pl.BlockSpec((D_HEADS, tm, HEAD_PAD), lambda i: (0, i, 0)),
            pl.BlockSpec((D_HEADS, tm // MOBA_BLOCK, SLAB_ROWS, MOBA_BLOCK),
                         lambda i: (0, i, 0, 0)),
            pl.BlockSpec((N_BLOCKS, D_WIDTH), lambda i: (0, 0)),
        ],
        out_shape=[
            jax.ShapeDtypeStruct((SEQ, C_WIDTH), BF16),
            jax.ShapeDtypeStruct((D_HEADS, SEQ, HEAD_PAD), F32),
            jax.ShapeDtypeStruct((D_HEADS, N_BLOCKS, SLAB_ROWS, MOBA_BLOCK), BF16),
            jax.ShapeDtypeStruct((N_BLOCKS, D_WIDTH), F32),
        ],
        scratch_shapes=[pltpu.VMEM((POOL_HALO + tm, C_WIDTH), F32),
                        pltpu.VMEM((N_BLOCKS, D_MODEL), F32)],
        compiler_params=_params(1),
        name="pool_qkv_proj",
    )(x, mod, norm_g, wu, wq, wkvt, pw, pb, ps)


def _block_rows(qs, slabs, mask, lse_add):
    n = qs[0].shape[0]
    kpad = jnp.zeros((HEAD_PAD - SLAB_K_ROWS, MOBA_BLOCK), BF16)
    s = jnp.concatenate(
        [jnp.dot(q, jnp.concatenate([slab[0:SLAB_K_ROWS, :], kpad], axis=0),
                 preferred_element_type=F32)
         for q, slab in zip(qs, slabs)], axis=0)
    if mask is not None:
        s = jnp.where(mask, s, NEG_BIG)
    m = jnp.max(s, axis=1, keepdims=True)
    p = jnp.exp(s - m).astype(BF16)
    vpad = jnp.zeros((HEAD_PAD - HEAD_DIM, MOBA_BLOCK), BF16)
    ones = jnp.ones((HEAD_PAD, MOBA_BLOCK), BF16)
    ot = jnp.concatenate(
        [lax.dot_general(p[u * n:(u + 1) * n],
                         jnp.concatenate([slab[SLAB_K_ROWS:SLAB_ROWS, :], vpad, ones], axis=0),
                         _NT, preferred_element_type=F32)
         for u, slab in enumerate(slabs)], axis=0)
    o, tot = ot[:, 0:HEAD_PAD], ot[:, HEAD_PAD:2 * HEAD_PAD]
    lse = m + jnp.log(tot)
    if lse_add is not None:
        lse = jnp.concatenate(
            [lse[u * n:(u + 1) * n] + add for u, add in enumerate(lse_add)], axis=0)
    low = lax.broadcasted_iota(I32, o.shape, 1) < HEAD_DIM
    return jnp.where(low, o / tot, lse)


def _onehot_pairs(sel_rows):
    blk = lax.broadcasted_iota(I32, (N_BLOCKS, MOBA_BLOCK), 0)
    return jnp.concatenate(
        [jnp.where(blk == sel_rows[s], 1.0, 0.0) for s in range(MOBA_TOPK)], axis=1)


def _gate_own_kernel(anchor_ref, qrow_ref, kv_ref, km_ref, tri_ref,
                     sel_ref, rank_ref, cnt_ref, own_ref, *, heads, blocks):
    del anchor_ref
    n_pair = MOBA_TOPK * MOBA_BLOCK
    tiles = [(bb, a) for bb in range(blocks) for a in range(heads)]
    rows_of = lambda bb: slice(bb * MOBA_BLOCK, (bb + 1) * MOBA_BLOCK)
    qs = [qrow_ref[a, rows_of(bb), :].astype(BF16) for bb, a in tiles]

    kms = [_split(km_ref[a]) for a in range(heads)]
    gates = [lax.dot_general(kms[a][0], q, _NT, preferred_element_type=F32)
             + lax.dot_general(kms[a][1], q, _NT, preferred_element_type=F32)
             for (bb, a), q in zip(tiles, qs)]
    width = len(tiles) * MOBA_BLOCK
    blk = lax.broadcasted_iota(I32, (N_BLOCKS, width), 0)
    lane = lax.broadcasted_iota(I32, (1, width), 1)
    own = pl.program_id(0) * blocks + lane // (heads * MOBA_BLOCK)
    g = jnp.where(blk < own, jnp.concatenate(gates, axis=1), -jnp.inf)
    slots = []
    for s in range(MOBA_TOPK):
        best = jnp.max(g, axis=0, keepdims=True)
        idx = jnp.min(jnp.where(g == best, blk, N_BLOCKS), axis=0, keepdims=True)
        g = jnp.where(blk == idx, -jnp.inf, g)
        slots.append(jnp.where(s < own, idx, N_BLOCKS))

    onehots = []
    for t, (bb, a) in enumerate(tiles):
        sel_rows = [slot[:, t * MOBA_BLOCK:(t + 1) * MOBA_BLOCK] for slot in slots]
        sel_ref[a, :, rows_of(bb)] = jnp.concatenate(sel_rows, axis=0)
        onehots.append(_onehot_pairs(sel_rows))
    onehot = jnp.concatenate(onehots, axis=0)
    oh = onehot.astype(BF16)
    before = jnp.dot(oh, tri_ref[...], preferred_element_type=F32)
    hit = onehot * before
    ones = jnp.ones((8, n_pair), BF16)
    cnt = lax.dot_general(ones, oh, _NT, preferred_element_type=F32)
    for t, (bb, a) in enumerate(tiles):
        rank = jnp.sum(hit[t * N_BLOCKS:(t + 1) * N_BLOCKS], axis=0, keepdims=True).astype(I32)
        rank_ref[a, :, rows_of(bb)] = jnp.concatenate(
            [rank[:, s * MOBA_BLOCK:(s + 1) * MOBA_BLOCK] for s in range(MOBA_TOPK)], axis=0)
        cnt_ref[a, bb] = cnt[0:1, t * N_BLOCKS:(t + 1) * N_BLOCKS].astype(I32)

    qi = lax.broadcasted_iota(I32, (width, MOBA_BLOCK), 0) % MOBA_BLOCK
    ki = lax.broadcasted_iota(I32, (width, MOBA_BLOCK), 1)
    rows = _block_rows(qs, [kv_ref.at[a, bb] for bb, a in tiles], ki <= qi, None)
    for t, (bb, a) in enumerate(tiles):
        own_ref[a, rows_of(bb), :] = rows[t * MOBA_BLOCK:(t + 1) * MOBA_BLOCK]


def _gate_own_call(anchor, qrows, kv, kmean_h, tri, *, split, blocks=GATE_BLOCKS_PER_STEP):
    heads = SPLIT_HEADS
    rows = blocks * MOBA_BLOCK
    return pl.pallas_call(
        functools.partial(_gate_own_kernel, heads=heads, blocks=blocks),
        grid=(N_BLOCKS // blocks,),
        in_specs=[
            pl.BlockSpec(memory_space=pltpu.SMEM),
            pl.BlockSpec((heads, rows, HEAD_PAD), lambda b: (split, b, 0)),
            pl.BlockSpec((heads, blocks, SLAB_ROWS, MOBA_BLOCK), lambda b: (split, b, 0, 0)),
            pl.BlockSpec((heads, N_BLOCKS, HEAD_PAD), lambda b: (split, 0, 0)),
            _full((MOBA_TOPK * MOBA_BLOCK, MOBA_TOPK * MOBA_BLOCK)),
        ],
        out_specs=[
            pl.BlockSpec((heads, MOBA_TOPK, rows), lambda b: (0, 0, b)),
            pl.BlockSpec((heads, MOBA_TOPK, rows), lambda b: (0, 0, b)),
            pl.BlockSpec((heads, blocks, 1, N_BLOCKS), lambda b: (0, b, 0, 0)),
            pl.BlockSpec((heads, rows, HEAD_PAD), lambda b: (0, b, 0)),
        ],
        out_shape=[
            jax.ShapeDtypeStruct((heads, MOBA_TOPK, SEQ), I32),
            jax.ShapeDtypeStruct((heads, MOBA_TOPK, SEQ), I32),
            jax.ShapeDtypeStruct((heads, N_BLOCKS, 1, N_BLOCKS), I32),
            jax.ShapeDtypeStruct((heads, SEQ, HEAD_PAD), F32),
        ],
        compiler_params=_params(1),
        name=f"moba_gate_own_{split}",
    )(anchor, qrows, kv, kmean_h, tri)


def _route_pos_kernel(sel_ref, rank_ref, base_ref, pos_ref, *, blocks):
    lane = lax.broadcasted_iota(I32, (1, MOBA_BLOCK), 1)
    for bb in range(blocks):
        qs = slice(bb * MOBA_BLOCK, (bb + 1) * MOBA_BLOCK)
        rows = [[] for _ in range(MOBA_TOPK)]
        for a in range(SPLIT_HEADS):
            sel = sel_ref[a, :, qs]
            onehot = _onehot_pairs([sel[s:s + 1, :] for s in range(MOBA_TOPK)]).astype(BF16)
            dig = jnp.dot(base_ref[a, bb].astype(BF16), onehot,
                          preferred_element_type=F32)
            base = (dig[0:1] * float(DIGIT * DIGIT) + dig[1:2] * float(DIGIT)
                    + dig[2:3]).astype(I32)
            for s in range(MOBA_TOPK):
                p = base[:, s * MOBA_BLOCK:(s + 1) * MOBA_BLOCK] + rank_ref[a, s:s + 1, qs]
                rows[s].append(
                    jnp.where(sel[s:s + 1, :] < N_BLOCKS, p, TRASH_ROW0 + lane % SC_WINDOW))
        for s in range(MOBA_TOPK):
            pos_ref[s, :, qs] = jnp.concatenate(rows[s], axis=0)


def _route_pos_call(sel, rank, base_digits, *, blocks=8):
    width = blocks * MOBA_BLOCK
    return pl.pallas_call(
        functools.partial(_route_pos_kernel, blocks=blocks),
        grid=(N_BLOCKS // blocks,),
        in_specs=[
            pl.BlockSpec((SPLIT_HEADS, MOBA_TOPK, width), lambda b: (0, 0, b)),
            pl.BlockSpec((SPLIT_HEADS, MOBA_TOPK, width), lambda b: (0, 0, b)),
            pl.BlockSpec((SPLIT_HEADS, blocks, 8, N_BLOCKS), lambda b: (0, b, 0, 0)),
        ],
        out_specs=pl.BlockSpec((MOBA_TOPK, SPLIT_HEADS, width), lambda b: (0, 0, b)),
        out_shape=jax.ShapeDtypeStruct((MOBA_TOPK, SPLIT_HEADS, SEQ), I32),
        compiler_params=_params(1),
        name="moba_route_pos",
    )(sel, rank, base_digits)


def _route_tables(cnt, split):
    cnt = cnt.reshape(SPLIT_HEADS, N_BLOCKS, N_BLOCKS)
    tiles = (cnt.sum(axis=1) + ROUTE_TILE - 1) // ROUTE_TILE
    tiles_flat = tiles.reshape(N_GROUPS)
    tile0 = jnp.cumsum(tiles_flat) - tiles_flat
    n_tiles = tiles_flat.sum().astype(I32)
    within = jnp.cumsum(cnt, axis=1) - cnt
    base = tile0.reshape(SPLIT_HEADS, 1, N_BLOCKS) * ROUTE_TILE + within
    digits = jnp.stack([base // (DIGIT * DIGIT), (base // DIGIT) % DIGIT, base % DIGIT],
                       axis=2).astype(F32)
    digits = jnp.pad(digits, ((0, 0), (0, 0), (0, 8 - 3), (0, 0)))
    t = jnp.arange(MAX_TILES, dtype=I32)
    ended = (tile0 + tiles_flat)[None, :] <= t[:, None]
    group_of_tile = jnp.minimum(ended.sum(axis=1).astype(I32), N_GROUPS - 1)
    tile_h = group_of_tile // N_BLOCKS
    head = split * SPLIT_HEADS + tile_h
    tile_step = lax.bitcast_convert_type((127 + 7 - head) << 23, F32)
    return digits, tile_h, group_of_tile % N_BLOCKS, n_tiles.reshape(1), tile_step


def _sc_mesh():
    return plsc.VectorSubcoreMesh(core_axis_name="core", subcore_axis_name="subcore")


def _dispatch_call(rows, pos_by_slot, *, split):
    n_rows = SPLIT_HEADS * SEQ
    first_window = split * n_rows // SC_WINDOW

    @functools.partial(
        pl.kernel, mesh=_sc_mesh(), scratch_types=[],
        out_type=jax.ShapeDtypeStruct((ROUTE_ROWS, HEAD_PAD), F32))
    def dispatch(x_hbm, i0_hbm, i1_hbm, i2_hbm, o_hbm):
        def body(x_vmem, i0_vmem, i1_vmem, i2_vmem):
            pltpu.sync_copy(x_vmem, o_hbm.at[i0_vmem.at[0]])
            pltpu.sync_copy(x_vmem, o_hbm.at[i1_vmem.at[0]])
            pltpu.sync_copy(x_vmem, o_hbm.at[i2_vmem.at[0]])

        idx_spec = pl.BlockSpec((1, SC_WINDOW), lambda i: (0, i))
        pltpu.emit_pipeline(
            body,
            grid=(n_rows // SC_WINDOW,),
            in_specs=[pl.BlockSpec((SC_WINDOW, HEAD_PAD), lambda i: (first_window + i, 0)),
                      idx_spec, idx_spec, idx_spec],
            out_specs=[],
            core_axis_name=("core", "subcore"),
            dimension_semantics=(pltpu.PARALLEL,),
        )(x_hbm, i0_hbm, i1_hbm, i2_hbm)

    return dispatch(rows, *pos_by_slot)


def _collect_call(table, idx):
    n_rows = idx.shape[1]

    @functools.partial(
        pl.kernel, mesh=_sc_mesh(), scratch_types=[],
        out_type=jax.ShapeDtypeStruct((n_rows, HEAD_PAD), F32))
    def collect(x_hbm, i_hbm, o_hbm):
        def body(i_vmem, o_vmem):
            pltpu.sync_copy(x_hbm.at[i_vmem.at[0]], o_vmem)

        pltpu.emit_pipeline(
            body,
            grid=(n_rows // SC_WINDOW,),
            in_specs=[pl.BlockSpec((1, SC_WINDOW), lambda i: (0, i))],
            out_specs=[pl.BlockSpec((SC_WINDOW, HEAD_PAD), lambda i: (i, 0))],
            core_axis_name=("core", "subcore"),
            dimension_semantics=(pltpu.PARALLEL,),
        )(i_hbm, o_hbm)

    return collect(table, idx)


def _routed_attn_kernel(th_ref, tj_ref, nt_ref, step_ref, q_ref, *refs, tiles):
    kv_refs, o_ref = refs[0:tiles], refs[tiles]
    t0 = pl.program_id(0) * tiles

    @pl.when(t0 < nt_ref[0])
    def _():
        qs = [q_ref[u * ROUTE_TILE:(u + 1) * ROUTE_TILE, :].astype(BF16) for u in range(tiles)]
        adds = [step_ref[t0 + u] * tj_ref[t0 + u].astype(F32) for u in range(tiles)]
        o_ref[...] = _block_rows(qs, kv_refs, None, adds)


def _routed_attn_call(tile_h, tile_j, n_tiles, tile_step, routed_q, kv,
                      *, split, tiles=ROUTE_TILES_PER_STEP):
    live = lambda s, th, tj, nt: (jnp.minimum(s, (nt[0] + tiles - 1) // tiles), 0)
    head0 = split * SPLIT_HEADS

    def kv_spec(u):
        return pl.BlockSpec(
            (None, None, SLAB_ROWS, MOBA_BLOCK),
            lambda s, th, tj, nt: (head0 + th[s * tiles + u], tj[s * tiles + u], 0, 0))

    grid_spec = pltpu.PrefetchScalarGridSpec(
        num_scalar_prefetch=3,
        grid=(MAX_TILES // tiles,),
        in_specs=([pl.BlockSpec(memory_space=pltpu.SMEM),
                   pl.BlockSpec((tiles * ROUTE_TILE, HEAD_PAD), live)]
                  + [kv_spec(u) for u in range(tiles)]),
        out_specs=pl.BlockSpec((tiles * ROUTE_TILE, HEAD_PAD), live),
    )
    return pl.pallas_call(
        functools.partial(_routed_attn_kernel, tiles=tiles),
        grid_spec=grid_spec,
        out_shape=jax.ShapeDtypeStruct((ROUTE_ROWS, HEAD_PAD), F32),
        compiler_params=_params(1),
        name=f"moba_routed_attn_{split}",
    )(tile_h, tile_j, n_tiles, tile_step, routed_q, *([kv] * tiles))


def _pm_out_kernel(slope_ref, x_ref, mod_ref, pooled_ref, *refs, blocks):
    own_refs, got_refs = refs[0:N_SPLITS], refs[N_SPLITS:2 * N_SPLITS]
    wp_ref, wa_ref, o_ref = refs[2 * N_SPLITS:]
    low = lax.broadcasted_iota(I32, (MOBA_BLOCK, HEAD_PAD), 1) < HEAD_DIM

    def lse_of(part):
        return jnp.where(low, pltpu.roll(part, HEAD_DIM, axis=1), part)

    y = jnp.dot(pooled_ref[...], wp_ref[...], preferred_element_type=F32)
    for hh in range(D_HEADS):
        own_ref, got_ref = own_refs[hh // SPLIT_HEADS], got_refs[hh // SPLIT_HEADS]
        a = hh % SPLIT_HEADS
        atts = []
        for bb in range(blocks):
            own = pl.program_id(0) * blocks + bb
            rows = slice(bb * MOBA_BLOCK, (bb + 1) * MOBA_BLOCK)
            own_shift = slope_ref[hh] * (own * MOBA_BLOCK).astype(F32)
            parts = [own_ref[a, rows, :]]
            lses = [lse_of(parts[0])]
            for s in range(MOBA_TOPK):
                valid = s < own
                part = got_ref[s, a, rows, :]
                parts.append(jnp.where(valid, part, 0.0))
                lses.append(jnp.where(valid, lse_of(part) - own_shift, NEG_BIG))
            top = functools.reduce(jnp.maximum, lses)
            num = jnp.zeros((MOBA_BLOCK, HEAD_PAD), F32)
            den = jnp.zeros((MOBA_BLOCK, HEAD_PAD), F32)
            for part, ls in zip(parts, lses):
                w = jnp.exp(ls - top)
                num = num + w * part
                den = den + w
            atts.append((num / den).astype(BF16))
        y = y + jnp.dot(jnp.concatenate(atts, axis=0), wa_ref[hh], preferred_element_type=F32)
    o_ref[...] = x_ref[...] + mod_ref[2:3, :] * y


def _pm_out_call(slopes, x, mod, pooled, own_parts, got_parts, wp, wa, *, layer, blocks=2):
    tm = blocks * MOBA_BLOCK
    return pl.pallas_call(
        functools.partial(_pm_out_kernel, blocks=blocks),
        grid=(SEQ // tm,),
        in_specs=[
            pl.BlockSpec(memory_space=pltpu.SMEM),
            pl.BlockSpec((tm, D_MODEL), lambda i: (i, 0)),
            _resident((None, None, 3, D_MODEL), lambda i: (layer, 1, 0, 0)),
            pl.BlockSpec((tm, C_WIDTH), lambda i: (i, 0)),
        ] + [pl.BlockSpec((SPLIT_HEADS, tm, HEAD_PAD), lambda i: (0, i, 0))] * N_SPLITS
        + [pl.BlockSpec((MOBA_TOPK, SPLIT_HEADS, tm, HEAD_PAD), lambda i: (0, 0, i, 0))] * N_SPLITS
        + [
            _full((C_WIDTH, D_MODEL)),
            _full((D_HEADS, HEAD_PAD, D_MODEL)),
        ],
        out_specs=pl.BlockSpec((tm, D_MODEL), lambda i: (i, 0)),
        out_shape=jax.ShapeDtypeStruct((SEQ, D_MODEL), F32),
        compiler_params=_params(1),
        name="pool_moba_out",
    )(slopes, x, mod, pooled, *own_parts, *got_parts, wp, wa)


def _head_pad(w, axis):
    shape = list(w.shape)
    shape[axis:axis + 1] = [D_HEADS, HEAD_DIM]
    pad = [(0, 0)] * len(shape)
    pad[axis + 1] = (0, HEAD_PAD - HEAD_DIM)
    out = jnp.pad(w.reshape(shape), pad)
    shape[axis:axis + 2] = [D_HEADS * HEAD_PAD]
    return out.reshape(shape)


def kernel(x, c, ada_w, ada_b, ffn_norm, ffn_w_gate, ffn_w_up, ffn_w_down, mix_norm, conv_w_in,
           conv_a_w, conv_a_b, conv_a_ln_g, conv_a_ln_b, conv_b_w, conv_w_out, pm_w_in, pool_w,
           pool_b, pool_scale, pm_w_out, final_norm):
    assert x.shape == (1, SEQ, D_MODEL) and c.shape == (1, D_MODEL)
    xs = x.reshape(SEQ, D_MODEL)
    mod = _mod_call(c, ada_w, ada_b)

    ffn_g = ffn_norm.reshape(DEPTH, 2, 1, D_MODEL)
    mix_g = mix_norm.reshape(DEPTH, 1, D_MODEL)
    fin = final_norm.reshape(1, D_MODEL)
    ffn_stacks = (ffn_w_gate, ffn_w_up, ffn_w_down)

    def ffn(xs, weights, layer, sub):
        last = (layer, sub) == (DEPTH - 1, 1)
        after = None if last else ((layer + sub, 1 - sub), *ffn_stacks)
        return _ffn_call(xs, mod, ffn_g, weights, fin, after, layer=layer, sub=sub, final=last)

    xs, ffn_w = ffn(xs, tuple(w[0, 0].astype(BF16) for w in ffn_stacks), layer=0, sub=0)
    grp = np.arange(A_WIDTH) // (A_WIDTH // A_GROUPS)
    seg = jnp.asarray((grp[:, None] == grp[None, :]) / (A_WIDTH // A_GROUPS), dtype=BF16)
    xs = _conv_mix_call(
        xs, mod, mix_g, conv_w_in[0].astype(BF16), conv_a_w[0], conv_a_b[0].reshape(1, A_WIDTH),
        conv_a_ln_g[0].reshape(1, A_WIDTH), conv_a_ln_b[0].reshape(1, A_WIDTH), conv_b_w[0],
        conv_w_out[0].astype(BF16), seg, layer=0)
    xs, ffn_w = ffn(xs, ffn_w, layer=0, sub=1)

    xs, ffn_w = ffn(xs, ffn_w, layer=1, sub=0)
    w_in = pm_w_in[0].astype(BF16)
    w_u = w_in[:, 0:C_WIDTH]
    w_q = w_in[:, C_WIDTH:C_WIDTH + D_WIDTH]
    w_kvt = w_in[:, C_WIDTH + D_WIDTH:C_WIDTH + 3 * D_WIDTH].T
    pooled, qrows, kv, kmean = _pm_proj_call(
        xs, mod, mix_g, w_u, w_q, w_kvt, pool_w[0].astype(BF16),
        pool_b[0].reshape(1, C_WIDTH), pool_scale[0].reshape(1, C_WIDTH), layer=1)
    kmean_h = _head_pad(kmean, 1).reshape(N_BLOCKS, D_HEADS, HEAD_PAD).transpose(1, 0, 2)

    n_pair = MOBA_TOPK * MOBA_BLOCK
    tri = jnp.asarray(np.arange(n_pair)[:, None] < np.arange(n_pair)[None, :], dtype=BF16)
    q_flat = qrows.reshape(D_HEADS * SEQ, HEAD_PAD)
    own_parts, got_parts = [], []
    anchor = jnp.zeros((1,), I32)
    for split in range(N_SPLITS):
        sel, rank, cnt, own_part = _gate_own_call(anchor, qrows, kv, kmean_h, tri, split=split)
        base_digits, tile_h, tile_j, n_tiles, tile_step = _route_tables(cnt, split)
        pos = _route_pos_call(sel, rank, base_digits)
        anchor = pos[0, 0, 0:1]
        pos_by_slot = [pos[s].reshape(1, SPLIT_HEADS * SEQ) for s in range(MOBA_TOPK)]
        routed_q = _dispatch_call(q_flat, pos_by_slot, split=split)
        routed_parts = _routed_attn_call(tile_h, tile_j, n_tiles, tile_step, routed_q, kv,
                                         split=split)
        got = _collect_call(routed_parts, pos.reshape(1, N_PAIRS))
        own_parts.append(own_part)
        got_parts.append(got.reshape(MOBA_TOPK, SPLIT_HEADS, SEQ, HEAD_PAD))

    slopes = jnp.asarray(2.0 ** (-8.0 * np.arange(1, D_HEADS + 1) / D_HEADS), dtype=F32)
    w_out = pm_w_out[0].astype(BF16)
    w_att = _head_pad(w_out[C_WIDTH:], 0).reshape(D_HEADS, HEAD_PAD, D_MODEL)
    xs = _pm_out_call(slopes, xs, mod, pooled, own_parts, got_parts, w_out[0:C_WIDTH], w_att,
                      layer=1)
    xs, _ = ffn(xs, ffn_w, layer=1, sub=1)
    return xs.reshape(1, SEQ, D_MODEL)
```

```python
import functools

import numpy as np
import jax
import jax.numpy as jnp
from jax import lax
from jax.experimental import pallas as pl
from jax.experimental.pallas import tpu as pltpu
from jax.experimental.pallas import tpu_sc as plsc

D_MODEL = 1024
SEQ = 16384
DEPTH = 2
N_SUBLAYERS = 3
D_FF = 2816
EPS = 1e-6

A_WIDTH = 512
A_GROUPS = 8
A_CONV = 31
B_WIDTH = 512
B_CONV = 3

POOL_WINDOWS = (2, 4, 8, 16)
C_WIDTH = 512
C_GROUP_DIM = 128
D_HEADS = 8
HEAD_DIM = 64
D_WIDTH = 512
MOBA_BLOCK = 256
MOBA_TOPK = 3
N_BLOCKS = SEQ // MOBA_BLOCK

V7X_SUBLANES = 8
V7X_BF16_SUBLANES = 16
V7X_VMEM_LIMIT_BYTES = 56 * 1024 * 1024

A_HALO = 32
B_HALO = 8
POOL_HALO = 16
NEG_BIG = -1e30

HEAD_PAD = 128
ALIBI_COL = HEAD_DIM
SLAB_K_ROWS = HEAD_DIM + V7X_BF16_SUBLANES
SLAB_ROWS = SLAB_K_ROWS + HEAD_DIM

SPLIT_HEADS = 4
N_SPLITS = D_HEADS // SPLIT_HEADS
PART_HEADS = 2
N_PARTS = SPLIT_HEADS // PART_HEADS
ROUTE_TILE = 256
ROUTE_TILES_PER_STEP = 32
GATE_BLOCKS_PER_STEP = 4
N_PAIRS = SPLIT_HEADS * SEQ * MOBA_TOPK
N_GROUPS = SPLIT_HEADS * N_BLOCKS
MAX_TILES = N_PAIRS // ROUTE_TILE + N_GROUPS
MAX_PART_TILES = MAX_TILES // N_PARTS
TRASH_ROW0 = MAX_TILES * ROUTE_TILE
ROUTE_ROWS = (MAX_TILES + ROUTE_TILES_PER_STEP) * ROUTE_TILE
SC_WINDOW = 128
DIGIT = 128
assert MAX_PART_TILES % ROUTE_TILES_PER_STEP == 0 and N_BLOCKS % GATE_BLOCKS_PER_STEP == 0
assert ROUTE_ROWS <= DIGIT ** 3 and MOBA_BLOCK <= ROUTE_TILE

BF16 = jnp.bfloat16
F32 = jnp.float32
I32 = jnp.int32


def _params(n_axes):
    return pltpu.CompilerParams(
        dimension_semantics=("arbitrary",) * n_axes,
        vmem_limit_bytes=V7X_VMEM_LIMIT_BYTES)


def _resident(block_shape, index_map):
    return pl.BlockSpec(block_shape, index_map, pipeline_mode=pl.Buffered(1))


def _full(shape):
    return _resident(shape, lambda *_: (0,) * len(shape))


def _sigmoid(v):
    return 1.0 / (1.0 + jnp.exp(-v))


def _norm_mod(x, g, mod):
    r = lax.rsqrt(jnp.mean(x * x, axis=-1, keepdims=True) + EPS)
    return (x * r) * (g * (1.0 + mod[1:2])) + mod[0:1]


def _split(v):
    hi = v.astype(BF16)
    return hi, (v - hi.astype(F32)).astype(BF16)


def _split_dot(v, w):
    hi, lo = _split(v)
    return (jnp.dot(hi, w, preferred_element_type=F32)
            + jnp.dot(lo, w, preferred_element_type=F32))


_NT = (((1,), (1,)), ((), ()))


def _mod_kernel(c_ref, w_ref, b_ref, o_ref):
    c = c_ref[...]
    cond = c * _sigmoid(c)
    o_ref[0] = jnp.sum(w_ref[0] * cond, axis=0, keepdims=True) + b_ref[0]


def _mod_call(c, ada_w, ada_b):
    n_out = N_SUBLAYERS * 3 * D_MODEL
    tn = D_MODEL
    out = pl.pallas_call(
        _mod_kernel,
        grid=(DEPTH, n_out // tn),
        in_specs=[
            pl.BlockSpec((D_MODEL, 1), lambda l, j: (0, 0)),
            pl.BlockSpec((1, D_MODEL, tn), lambda l, j: (l, 0, j)),
            pl.BlockSpec((1, 1, tn), lambda l, j: (l, 0, j)),
        ],
        out_specs=pl.BlockSpec((1, 1, tn), lambda l, j: (l, 0, j)),
        out_shape=jax.ShapeDtypeStruct((DEPTH, 1, n_out), F32),
        compiler_params=_params(2),
        name="adaln_mod",
    )(c.reshape(D_MODEL, 1), ada_w, ada_b.reshape(DEPTH, 1, n_out))
    return out.reshape(DEPTH, N_SUBLAYERS, 3, D_MODEL)


def _ffn_kernel(x_ref, mod_ref, g_ref, wg_ref, wu_ref, wd_ref, fin_ref, *refs, final):
    o_ref = refs[len(refs) // 2]
    for src_ref, dst_ref in zip(refs[:len(refs) // 2], refs[len(refs) // 2 + 1:]):
        dst_ref[...] = src_ref[...].astype(BF16)
    x = x_ref[...]
    mod = mod_ref[...]
    h = _norm_mod(x, g_ref[...], mod).astype(BF16)
    gate = jnp.dot(h, wg_ref[...], preferred_element_type=F32)
    up = jnp.dot(h, wu_ref[...], preferred_element_type=F32)
    act = (gate * _sigmoid(gate) * up).astype(BF16)
    y = jnp.dot(act, wd_ref[...], preferred_element_type=F32)
    xn = x + (0.5 * mod[2:3]) * y
    if final:
        r = lax.rsqrt(jnp.mean(xn * xn, axis=-1, keepdims=True) + EPS)
        xn = xn * r * fin_ref[...]
    o_ref[...] = xn


def _ffn_call(x, mod, norm_g, weights, fin, next_weights, *, layer, sub, final, tm=512):
    mod_sub = 0 if sub == 0 else 2
    steps = SEQ // tm
    in_specs = [
        pl.BlockSpec((tm, D_MODEL), lambda i: (i, 0)),
        _resident((None, None, 3, D_MODEL), lambda i: (layer, mod_sub, 0, 0)),
        _resident((None, None, 1, D_MODEL), lambda i: (layer, sub, 0, 0)),
        _full((D_MODEL, D_FF)),
        _full((D_MODEL, D_FF)),
        _full((D_FF, D_MODEL)),
        _full((1, D_MODEL)),
    ]
    out_specs = [pl.BlockSpec((tm, D_MODEL), lambda i: (i, 0))]
    out_shape = [jax.ShapeDtypeStruct((SEQ, D_MODEL), F32)]
    operands = [x, mod, norm_g, *weights, fin]
    if next_weights is not None:
        (nl, ns), *stacks = next_weights
        for w in stacks:
            n_rows, n_cols = w.shape[2:]
            share = 1 if n_rows % (steps * V7X_BF16_SUBLANES) == 0 else 2
            rows = n_rows * share // steps
            assert n_rows * share % steps == 0 and rows % V7X_BF16_SUBLANES == 0
            in_specs.append(pl.BlockSpec((None, None, rows, n_cols),
                                         lambda i, share=share: (nl, ns, i // share, 0)))
            out_specs.append(pl.BlockSpec((rows, n_cols), lambda i, share=share: (i // share, 0)))
            out_shape.append(jax.ShapeDtypeStruct((n_rows, n_cols), BF16))
            operands.append(w)
    out = pl.pallas_call(
        functools.partial(_ffn_kernel, final=final),
        grid=(steps,),
        in_specs=in_specs,
        out_specs=out_specs,
        out_shape=out_shape,
        compiler_params=_params(1),
        name=f"ffn_l{layer}s{sub}",
    )(*operands)
    return out[0], tuple(out[1:])


def _conv_mix_kernel(x_ref, mod_ref, g_ref, win_ref, caw_ref, cab_ref, lng_ref, lnb_ref,
                     cbw_ref, wout_ref, seg_ref, o_ref, abuf, bbuf, sbuf, *, tm):
    i = pl.program_id(0)

    @pl.when(i == 0)
    def _():
        abuf[0:A_HALO, :] = jnp.zeros((A_HALO, A_WIDTH), F32)
        bbuf[0:B_HALO, :] = jnp.zeros((B_HALO, B_WIDTH), F32)

    x = x_ref[...]
    mod = mod_ref[...]
    h = _norm_mod(x, g_ref[...], mod).astype(BF16)
    z = jnp.dot(h, win_ref[...], preferred_element_type=F32)

    a = z[:, 0:A_WIDTH] * _sigmoid(z[:, A_WIDTH:2 * A_WIDTH])
    abuf[A_HALO:A_HALO + tm, :] = a
    acc = jnp.zeros((tm, A_WIDTH), F32) + cab_ref[...]
    ext = tm + V7X_SUBLANES
    for r in range(V7X_SUBLANES):
        part = None
        for q in range((A_CONV - 1 - r) // V7X_SUBLANES + 1):
            k = A_CONV - 1 - (V7X_SUBLANES * q + r)
            term = caw_ref[k:k + 1, :] * abuf[pl.ds(A_HALO - V7X_SUBLANES * (q + 1), ext), :]
            part = term if part is None else part + term
        if r == 0:
            acc = acc + part[V7X_SUBLANES:, :]
        else:
            sbuf[r - 1] = part
            acc = acc + sbuf[r - 1, pl.ds(V7X_SUBLANES - r, tm), :]
    abuf[0:A_HALO, :] = abuf[tm:tm + A_HALO, :]
    seg = seg_ref[...]
    mu = _split_dot(acc, seg)
    d = acc - mu
    var = _split_dot(d * d, seg)
    yn = d * lax.rsqrt(var + EPS) * lng_ref[...] + lnb_ref[...]
    a_out = yn * _sigmoid(yn)

    off = 2 * A_WIDTH
    cv = z[:, off + B_WIDTH:off + 2 * B_WIDTH] * z[:, off + 2 * B_WIDTH:off + 3 * B_WIDTH]
    bbuf[B_HALO:B_HALO + tm, :] = cv
    bacc = jnp.zeros((tm, B_WIDTH), F32)
    for k in range(B_CONV):
        bacc = bacc + cbw_ref[k:k + 1, :] * bbuf[pl.ds(B_HALO - (B_CONV - 1) + k, tm), :]
    bbuf[0:B_HALO, :] = bbuf[tm:tm + B_HALO, :]
    bb = z[:, off:off + B_WIDTH] * bacc

    y = (jnp.dot(a_out.astype(BF16), wout_ref[0:A_WIDTH, :], preferred_element_type=F32)
         + jnp.dot(bb.astype(BF16), wout_ref[A_WIDTH:A_WIDTH + B_WIDTH, :],
                   preferred_element_type=F32))
    o_ref[...] = x + mod[2:3] * y


def _conv_mix_call(x, mod, norm_g, w_in, caw, cab, lng, lnb, cbw, w_out, seg, *, layer, tm=512):
    even_in = w_in.shape[1]
    return pl.pallas_call(
        functools.partial(_conv_mix_kernel, tm=tm),
        grid=(SEQ // tm,),
        in_specs=[
            pl.BlockSpec((tm, D_MODEL), lambda i: (i, 0)),
            _resident((None, None, 3, D_MODEL), lambda i: (layer, 1, 0, 0)),
            _resident((None, 1, D_MODEL), lambda i: (layer, 0, 0)),
            _full((D_MODEL, even_in)),
            _full((A_CONV, A_WIDTH)),
            _full((1, A_WIDTH)),
            _full((1, A_WIDTH)),
            _full((1, A_WIDTH)),
            _full((B_CONV, B_WIDTH)),
            _full((A_WIDTH + B_WIDTH, D_MODEL)),
            _full((A_WIDTH, A_WIDTH)),
        ],
        out_specs=pl.BlockSpec((tm, D_MODEL), lambda i: (i, 0)),
        out_shape=jax.ShapeDtypeStruct((SEQ, D_MODEL), F32),
        scratch_shapes=[pltpu.VMEM((A_HALO + tm, A_WIDTH), F32),
                        pltpu.VMEM((B_HALO + tm, B_WIDTH), F32),
                        pltpu.VMEM((V7X_SUBLANES - 1, tm + V7X_SUBLANES, A_WIDTH), F32)],
        compiler_params=_params(1),
        name="conv_mixers",
    )(x, mod, norm_g, w_in, caw, cab, lng, lnb, cbw, w_out, seg)


def _pm_proj_kernel(x_ref, mod_ref, g_ref, wu_ref, wq_ref, wkvt_ref, pw_ref,
                    pb_ref, ps_ref, pooled_ref, qrow_ref, kv_ref, km_ref,
                    ubuf, hmean, *, tm):
    i = pl.program_id(0)

    @pl.when(i == 0)
    def _():
        ubuf[0:POOL_HALO, :] = jnp.zeros((POOL_HALO, C_WIDTH), F32)

    x = x_ref[...]
    h = _norm_mod(x, g_ref[...], mod_ref[...]).astype(BF16)
    u = jnp.dot(h, wu_ref[...], preferred_element_type=F32)
    qr = jnp.dot(h, wq_ref[...], preferred_element_type=F32)
    kvt = lax.dot_general(wkvt_ref[...], h, _NT, preferred_element_type=F32)

    ubuf[POOL_HALO:POOL_HALO + tm, :] = u
    t1 = (i * tm + 1 + lax.broadcasted_iota(I32, (tm, C_GROUP_DIM), 0)).astype(F32)
    for gi, w in enumerate(POOL_WINDOWS):
        c0 = gi * C_GROUP_DIM
        s = u[:, c0:c0 + C_GROUP_DIM]
        for k in range(1, w):
            s = s + ubuf[pl.ds(POOL_HALO - k, tm), c0:c0 + C_GROUP_DIM]
        pooled = s / jnp.minimum(t1, float(w)) - u[:, c0:c0 + C_GROUP_DIM]
        mixed = (jnp.dot(pooled.astype(BF16), pw_ref[gi], preferred_element_type=F32)
                 + pb_ref[:, c0:c0 + C_GROUP_DIM])
        pooled_ref[:, c0:c0 + C_GROUP_DIM] = (
            mixed * ps_ref[:, c0:c0 + C_GROUP_DIM]).astype(BF16)
    ubuf[0:POOL_HALO, :] = ubuf[tm:tm + POOL_HALO, :]

    lane = lax.broadcasted_iota(I32, (tm, HEAD_PAD), 1)
    tail = jnp.where(lane == ALIBI_COL, 1.0, 0.0)
    qs = qr * (HEAD_DIM ** -0.5)
    for pair in range(D_HEADS // 2):
        both = qs[:, pair * HEAD_PAD:(pair + 1) * HEAD_PAD]
        qrow_ref[2 * pair] = jnp.where(lane < HEAD_DIM, both, tail)
        qrow_ref[2 * pair + 1] = jnp.where(
            lane < HEAD_DIM, pltpu.roll(both, HEAD_DIM, axis=1), tail)
    per_tile = tm // MOBA_BLOCK
    kvb = kvt.astype(BF16)
    n_alibi = SLAB_K_ROWS - HEAD_DIM
    first = lax.broadcasted_iota(I32, (n_alibi, MOBA_BLOCK), 0) == 0
    koff = lax.broadcasted_iota(I32, (n_alibi, MOBA_BLOCK), 1).astype(F32)
    for hh in range(D_HEADS):
        alibi = jnp.where(first, koff * (2.0 ** -(hh + 1)), 0.0).astype(BF16)
        for sb in range(per_tile):
            keys = slice(sb * MOBA_BLOCK, (sb + 1) * MOBA_BLOCK)
            kv_ref[hh, sb, 0:HEAD_DIM, :] = kvb[hh * HEAD_DIM:(hh + 1) * HEAD_DIM, keys]
            kv_ref[hh, sb, HEAD_DIM:SLAB_K_ROWS, :] = alibi
            kv_ref[hh, sb, SLAB_K_ROWS:SLAB_ROWS, :] = kvb[
                D_WIDTH + hh * HEAD_DIM:D_WIDTH + (hh + 1) * HEAD_DIM, keys]

    for sb in range(per_tile):
        blk = h[sb * MOBA_BLOCK:(sb + 1) * MOBA_BLOCK, :].astype(F32)
        hmean[pl.ds(i * per_tile + sb, 1), :] = jnp.mean(blk, axis=0, keepdims=True)

    @pl.when(i == pl.num_programs(0) - 1)
    def _():
        hi, lo = _split(hmean[...])
        wkt = wkvt_ref[0:D_WIDTH, :]
        km_ref[...] = (lax.dot_general(hi, wkt, _NT, preferred_element_type=F32)
                       + lax.dot_general(lo, wkt, _NT, preferred_element_type=F32))


def _pm_proj_call(x, mod, norm_g, wu, wq, wkvt, pw, pb, ps, *, layer, tm=512):
    return pl.pallas_call(
        functools.partial(_pm_proj_kernel, tm=tm),
        grid=(SEQ // tm,),
        in_specs=[
            pl.BlockSpec((tm, D_MODEL), lambda i: (i, 0)),
            _resident((None, None, 3, D_MODEL), lambda i: (layer, 1, 0, 0)),
            _resident((None, 1, D_MODEL), lambda i: (layer, 0, 0)),
            _full((D_MODEL, C_WIDTH)),
            _full((D_MODEL, D_WIDTH)),
            _full((2 * D_WIDTH, D_MODEL)),
            _full((len(POOL_WINDOWS), C_GROUP_DIM, C_GROUP_DIM)),
            _full((1, C_WIDTH)),
            _full((1, C_WIDTH)),
        ],
        out_specs=[
            pl.BlockSpec((tm, C_WIDTH), lambda i: (i, 0)),
            pl.BlockSpec((D_HEADS, tm, HEAD_PAD), lambda i: (0, i, 0)),
            pl.BlockSpec((D_HEADS, tm // MOBA_BLOCK, SLAB_ROWS, MOBA_BLOCK),
                         lambda i: (0, i, 0, 0)),
            pl.BlockSpec((N_BLOCKS, D_WIDTH), lambda i: (0, 0)),
        ],
        out_shape=[
            jax.ShapeDtypeStruct((SEQ, C_WIDTH), BF16),
            jax.ShapeDtypeStruct((D_HEADS, SEQ, HEAD_PAD), F32),
            jax.ShapeDtypeStruct((D_HEADS, N_BLOCKS, SLAB_ROWS, MOBA_BLOCK), BF16),
            jax.ShapeDtypeStruct((N_BLOCKS, D_WIDTH), F32),
        ],
        scratch_shapes=[pltpu.VMEM((POOL_HALO + tm, C_WIDTH), F32),
                        pltpu.VMEM((N_BLOCKS, D_MODEL), F32)],
        compiler_params=_params(1),
        name="pool_qkv_proj",
    )(x, mod, norm_g, wu, wq, wkvt, pw, pb, ps)


def _block_rows(qs, slabs, mask, lse_add):
    n = qs[0].shape[0]
    kpad = jnp.zeros((HEAD_PAD - SLAB_K_ROWS, MOBA_BLOCK), BF16)
    s = jnp.concatenate(
        [jnp.dot(q, jnp.concatenate([slab[0:SLAB_K_ROWS, :], kpad], axis=0),
                 preferred_element_type=F32)
         for q, slab in zip(qs, slabs)], axis=0)
    if mask is not None:
        s = jnp.where(mask, s, NEG_BIG)
    m = jnp.max(s, axis=1, keepdims=True)
    p = jnp.exp(s - m).astype(BF16)
    vpad = jnp.zeros((HEAD_PAD - HEAD_DIM, MOBA_BLOCK), BF16)
    ones = jnp.ones((HEAD_PAD, MOBA_BLOCK), BF16)
    ot = jnp.concatenate(
        [lax.dot_general(p[u * n:(u + 1) * n],
                         jnp.concatenate([slab[SLAB_K_ROWS:SLAB_ROWS, :], vpad, ones], axis=0),
                         _NT, preferred_element_type=F32)
         for u, slab in enumerate(slabs)], axis=0)
    o, tot = ot[:, 0:HEAD_PAD], ot[:, HEAD_PAD:2 * HEAD_PAD]
    lse = m + jnp.log(tot)
    if lse_add is not None:
        lse = jnp.concatenate(
            [lse[u * n:(u + 1) * n] + add for u, add in enumerate(lse_add)], axis=0)
    low = lax.broadcasted_iota(I32, o.shape, 1) < HEAD_DIM
    return jnp.where(low, o / tot, lse)


def _onehot_pairs(sel_rows):
    blk = lax.broadcasted_iota(I32, (N_BLOCKS, MOBA_BLOCK), 0)
    return jnp.concatenate(
        [jnp.where(blk == sel_rows[s], 1.0, 0.0) for s in range(MOBA_TOPK)], axis=1)


def _gate_own_kernel(anchor_ref, qrow_ref, kv_ref, km_ref, tri_ref,
                     sel_ref, rank_ref, cnt_ref, own_ref, *, heads, blocks):
    del anchor_ref
    n_pair = MOBA_TOPK * MOBA_BLOCK
    tiles = [(bb, a) for bb in range(blocks) for a in range(heads)]
    rows_of = lambda bb: slice(bb * MOBA_BLOCK, (bb + 1) * MOBA_BLOCK)
    qs = [qrow_ref[a, rows_of(bb), :].astype(BF16) for bb, a in tiles]

    kms = [_split(km_ref[a]) for a in range(heads)]
    gates = [lax.dot_general(kms[a][0], q, _NT, preferred_element_type=F32)
             + lax.dot_general(kms[a][1], q, _NT, preferred_element_type=F32)
             for (bb, a), q in zip(tiles, qs)]
    width = len(tiles) * MOBA_BLOCK
    blk = lax.broadcasted_iota(I32, (N_BLOCKS, width), 0)
    lane = lax.broadcasted_iota(I32, (1, width), 1)
    own = pl.program_id(0) * blocks + lane // (heads * MOBA_BLOCK)
    g = jnp.where(blk < own, jnp.concatenate(gates, axis=1), -jnp.inf)
    slots = []
    for s in range(MOBA_TOPK):
        best = jnp.max(g, axis=0, keepdims=True)
        idx = jnp.min(jnp.where(g == best, blk, N_BLOCKS), axis=0, keepdims=True)
        g = jnp.where(blk == idx, -jnp.inf, g)
        slots.append(jnp.where(s < own, idx, N_BLOCKS))

    onehots = []
    for t, (bb, a) in enumerate(tiles):
        sel_rows = [slot[:, t * MOBA_BLOCK:(t + 1) * MOBA_BLOCK] for slot in slots]
        sel_ref[a, :, rows_of(bb)] = jnp.concatenate(sel_rows, axis=0)
        onehots.append(_onehot_pairs(sel_rows))
    onehot = jnp.concatenate(onehots, axis=0)
    oh = onehot.astype(BF16)
    before = jnp.dot(oh, tri_ref[...], preferred_element_type=F32)
    hit = onehot * before
    ones = jnp.ones((8, n_pair), BF16)
    cnt = lax.dot_general(ones, oh, _NT, preferred_element_type=F32)
    for t, (bb, a) in enumerate(tiles):
        rank = jnp.sum(hit[t * N_BLOCKS:(t + 1) * N_BLOCKS], axis=0, keepdims=True).astype(I32)
        rank_ref[a, :, rows_of(bb)] = jnp.concatenate(
            [rank[:, s * MOBA_BLOCK:(s + 1) * MOBA_BLOCK] for s in range(MOBA_TOPK)], axis=0)
        cnt_ref[a, bb] = cnt[0:1, t * N_BLOCKS:(t + 1) * N_BLOCKS].astype(I32)

    qi = lax.broadcasted_iota(I32, (width, MOBA_BLOCK), 0) % MOBA_BLOCK
    ki = lax.broadcasted_iota(I32, (width, MOBA_BLOCK), 1)
    rows = _block_rows(qs, [kv_ref.at[a, bb] for bb, a in tiles], ki <= qi, None)
    for t, (bb, a) in enumerate(tiles):
        own_ref[a, rows_of(bb), :] = rows[t * MOBA_BLOCK:(t + 1) * MOBA_BLOCK]


def _gate_own_call(anchor, qrows, kv, kmean_h, tri, *, split, blocks=GATE_BLOCKS_PER_STEP):
    heads = SPLIT_HEADS
    rows = blocks * MOBA_BLOCK
    return pl.pallas_call(
        functools.partial(_gate_own_kernel, heads=heads, blocks=blocks),
        grid=(N_BLOCKS // blocks,),
        in_specs=[
            pl.BlockSpec(memory_space=pltpu.SMEM),
            pl.BlockSpec((heads, rows, HEAD_PAD), lambda b: (split, b, 0)),
            pl.BlockSpec((heads, blocks, SLAB_ROWS, MOBA_BLOCK), lambda b: (split, b, 0, 0)),
            pl.BlockSpec((heads, N_BLOCKS, HEAD_PAD), lambda b: (split, 0, 0)),
            _full((MOBA_TOPK * MOBA_BLOCK, MOBA_TOPK * MOBA_BLOCK)),
        ],
        out_specs=[
            pl.BlockSpec((heads, MOBA_TOPK, rows), lambda b: (0, 0, b)),
            pl.BlockSpec((heads, MOBA_TOPK, rows), lambda b: (0, 0, b)),
            pl.BlockSpec((heads, blocks, 1, N_BLOCKS), lambda b: (0, b, 0, 0)),
            pl.BlockSpec((heads, rows, HEAD_PAD), lambda b: (0, b, 0)),
        ],
        out_shape=[
            jax.ShapeDtypeStruct((heads, MOBA_TOPK, SEQ), I32),
            jax.ShapeDtypeStruct((heads, MOBA_TOPK, SEQ), I32),
            jax.ShapeDtypeStruct((heads, N_BLOCKS, 1, N_BLOCKS), I32),
            jax.ShapeDtypeStruct((heads, SEQ, HEAD_PAD), F32),
        ],
        compiler_params=_params(1),
        name=f"moba_gate_own_{split}",
    )(anchor, qrows, kv, kmean_h, tri)


def _route_pos_kernel(sel_ref, rank_ref, base_ref, pos_ref, *, blocks):
    lane = lax.broadcasted_iota(I32, (1, MOBA_BLOCK), 1)
    for bb in range(blocks):
        qs = slice(bb * MOBA_BLOCK, (bb + 1) * MOBA_BLOCK)
        rows = [[] for _ in range(MOBA_TOPK)]
        for a in range(SPLIT_HEADS):
            sel = sel_ref[a, :, qs]
            onehot = _onehot_pairs([sel[s:s + 1, :] for s in range(MOBA_TOPK)]).astype(BF16)
            dig = jnp.dot(base_ref[a, bb].astype(BF16), onehot,
                          preferred_element_type=F32)
            base = (dig[0:1] * float(DIGIT * DIGIT) + dig[1:2] * float(DIGIT)
                    + dig[2:3]).astype(I32)
            for s in range(MOBA_TOPK):
                p = base[:, s * MOBA_BLOCK:(s + 1) * MOBA_BLOCK] + rank_ref[a, s:s + 1, qs]
                rows[s].append(
                    jnp.where(sel[s:s + 1, :] < N_BLOCKS, p, TRASH_ROW0 + lane % SC_WINDOW))
        for s in range(MOBA_TOPK):
            pos_ref[s, :, qs] = jnp.concatenate(rows[s], axis=0)


def _route_pos_call(sel, rank, base_digits, *, blocks=8):
    width = blocks * MOBA_BLOCK
    return pl.pallas_call(
        functools.partial(_route_pos_kernel, blocks=blocks),
        grid=(N_BLOCKS // blocks,),
        in_specs=[
            pl.BlockSpec((SPLIT_HEADS, MOBA_TOPK, width), lambda b: (0, 0, b)),
            pl.BlockSpec((SPLIT_HEADS, MOBA_TOPK, width), lambda b: (0, 0, b)),
            pl.BlockSpec((SPLIT_HEADS, blocks, 8, N_BLOCKS), lambda b: (0, b, 0, 0)),
        ],
        out_specs=pl.BlockSpec((MOBA_TOPK, SPLIT_HEADS, width), lambda b: (0, 0, b)),
        out_shape=jax.ShapeDtypeStruct((MOBA_TOPK, SPLIT_HEADS, SEQ), I32),
        compiler_params=_params(1),
        name="moba_route_pos",
    )(sel, rank, base_digits)


def _route_tables(cnt, split):
    cnt = cnt.reshape(SPLIT_HEADS, N_BLOCKS, N_BLOCKS)
    tiles = (cnt.sum(axis=1) + ROUTE_TILE - 1) // ROUTE_TILE
    tiles_flat = tiles.reshape(N_GROUPS)
    tile0 = jnp.cumsum(tiles_flat) - tiles_flat
    n_tiles = tiles_flat.sum().astype(I32)
    part_bounds = jnp.concatenate(
        [tile0[::PART_HEADS * N_BLOCKS].astype(I32), n_tiles.reshape(1)])
    within = jnp.cumsum(cnt, axis=1) - cnt
    base = tile0.reshape(SPLIT_HEADS, 1, N_BLOCKS) * ROUTE_TILE + within
    digits = jnp.stack([base // (DIGIT * DIGIT), (base // DIGIT) % DIGIT, base % DIGIT],
                       axis=2).astype(F32)
    digits = jnp.pad(digits, ((0, 0), (0, 0), (0, 8 - 3), (0, 0)))
    t = jnp.arange(MAX_TILES, dtype=I32)
    ended = (tile0 + tiles_flat)[None, :] <= t[:, None]
    group_of_tile = jnp.minimum(ended.sum(axis=1).astype(I32), N_GROUPS - 1)
    tile_h = group_of_tile // N_BLOCKS
    head = split * SPLIT_HEADS + tile_h
    tile_step = lax.bitcast_convert_type((127 + 7 - head) << 23, F32)
    return digits, tile_h, group_of_tile % N_BLOCKS, part_bounds, tile_step


def _sc_mesh():
    return plsc.VectorSubcoreMesh(core_axis_name="core", subcore_axis_name="subcore")


def _dispatch_call(rows, pos_by_slot, *, split):
    n_rows = SPLIT_HEADS * SEQ
    first_window = split * n_rows // SC_WINDOW

    @functools.partial(
        pl.kernel, mesh=_sc_mesh(), scratch_types=[],
        out_type=jax.ShapeDtypeStruct((ROUTE_ROWS, HEAD_PAD), F32))
    def dispatch(x_hbm, i0_hbm, i1_hbm, i2_hbm, o_hbm):
        def body(x_vmem, i0_vmem, i1_vmem, i2_vmem):
            pltpu.sync_copy(x_vmem, o_hbm.at[i0_vmem.at[0]])
            pltpu.sync_copy(x_vmem, o_hbm.at[i1_vmem.at[0]])
            pltpu.sync_copy(x_vmem, o_hbm.at[i2_vmem.at[0]])

        idx_spec = pl.BlockSpec((1, SC_WINDOW), lambda i: (0, i))
        pltpu.emit_pipeline(
            body,
            grid=(n_rows // SC_WINDOW,),
            in_specs=[pl.BlockSpec((SC_WINDOW, HEAD_PAD), lambda i: (first_window + i, 0)),
                      idx_spec, idx_spec, idx_spec],
            out_specs=[],
            core_axis_name=("core", "subcore"),
            dimension_semantics=(pltpu.PARALLEL,),
        )(x_hbm, i0_hbm, i1_hbm, i2_hbm)

    return dispatch(rows, *pos_by_slot)


def _collect_call(table, idx):
    n_rows = idx.shape[1]

    @functools.partial(
        pl.kernel, mesh=_sc_mesh(), scratch_types=[],
        out_type=jax.ShapeDtypeStruct((n_rows, HEAD_PAD), F32))
    def collect(x_hbm, i_hbm, o_hbm):
        def body(i_vmem, o_vmem):
            pltpu.sync_copy(x_hbm.at[i_vmem.at[0]], o_vmem)

        pltpu.emit_pipeline(
            body,
            grid=(n_rows // SC_WINDOW,),
            in_specs=[pl.BlockSpec((1, SC_WINDOW), lambda i: (0, i))],
            out_specs=[pl.BlockSpec((SC_WINDOW, HEAD_PAD), lambda i: (i, 0))],
            core_axis_name=("core", "subcore"),
            dimension_semantics=(pltpu.PARALLEL,),
        )(i_hbm, o_hbm)

    return collect(table, idx)


def _routed_attn_kernel(th_ref, tj_ref, bound_ref, step_ref, q_ref, *refs, part, tiles):
    kv_refs, o_ref = refs[0:tiles], refs[tiles]
    t0 = (bound_ref[part] // tiles + pl.program_id(0)) * tiles

    @pl.when(t0 < bound_ref[part + 1])
    def _():
        qs = [q_ref[u * ROUTE_TILE:(u + 1) * ROUTE_TILE, :].astype(BF16) for u in range(tiles)]
        adds = [step_ref[t0 + u] * tj_ref[t0 + u].astype(F32) for u in range(tiles)]
        o_ref[...] = _block_rows(qs, kv_refs, None, adds)


def _routed_attn_call(tile_h, tile_j, part_bounds, tile_step, routed_q, kv,
                      *, split, part, tiles=ROUTE_TILES_PER_STEP):
    head0 = split * SPLIT_HEADS

    def step_of(s, bounds):
        return bounds[part] // tiles + s

    def live(s, th, tj, bounds):
        return jnp.minimum(step_of(s, bounds), (bounds[part + 1] + tiles - 1) // tiles), 0

    def kv_spec(u):
        def slab(s, th, tj, bounds):
            last = jnp.maximum(bounds[part + 1] - 1, 0) // tiles
            t = jnp.minimum(step_of(s, bounds), last) * tiles + u
            return head0 + th[t], tj[t], 0, 0
        return pl.BlockSpec((None, None, SLAB_ROWS, MOBA_BLOCK), slab)

    grid_spec = pltpu.PrefetchScalarGridSpec(
        num_scalar_prefetch=3,
        grid=(MAX_PART_TILES // tiles + 1,),
        in_specs=([pl.BlockSpec(memory_space=pltpu.SMEM),
                   pl.BlockSpec((tiles * ROUTE_TILE, HEAD_PAD), live)]
                  + [kv_spec(u) for u in range(tiles)]),
        out_specs=pl.BlockSpec((tiles * ROUTE_TILE, HEAD_PAD), live),
    )
    return pl.pallas_call(
        functools.partial(_routed_attn_kernel, part=part, tiles=tiles),
        grid_spec=grid_spec,
        out_shape=jax.ShapeDtypeStruct((ROUTE_ROWS, HEAD_PAD), F32),
        compiler_params=_params(1),
        name=f"moba_routed_attn_{split}{'abcd'[part]}",
    )(tile_h, tile_j, part_bounds, tile_step, routed_q, *([kv] * tiles))


def _pm_out_kernel(slope_ref, x_ref, mod_ref, pooled_ref, *refs, blocks):
    n_got = D_HEADS // PART_HEADS
    own_refs, got_refs = refs[0:N_SPLITS], refs[N_SPLITS:N_SPLITS + n_got]
    wp_ref, wa_ref, o_ref = refs[N_SPLITS + n_got:]
    low = lax.broadcasted_iota(I32, (MOBA_BLOCK, HEAD_PAD), 1) < HEAD_DIM

    def lse_of(part):
        return jnp.where(low, pltpu.roll(part, HEAD_DIM, axis=1), part)

    y = jnp.dot(pooled_ref[...], wp_ref[...], preferred_element_type=F32)
    for hh in range(D_HEADS):
        own_ref, got_ref = own_refs[hh // SPLIT_HEADS], got_refs[hh // PART_HEADS]
        a, ag = hh % SPLIT_HEADS, hh % PART_HEADS
        atts = []
        for bb in range(blocks):
            own = pl.program_id(0) * blocks + bb
            rows = slice(bb * MOBA_BLOCK, (bb + 1) * MOBA_BLOCK)
            own_shift = slope_ref[hh] * (own * MOBA_BLOCK).astype(F32)
            parts = [own_ref[a, rows, :]]
            lses = [lse_of(parts[0])]
            for s in range(MOBA_TOPK):
                valid = s < own
                part = got_ref[s, ag, rows, :]
                parts.append(jnp.where(valid, part, 0.0))
                lses.append(jnp.where(valid, lse_of(part) - own_shift, NEG_BIG))
            top = functools.reduce(jnp.maximum, lses)
            num = jnp.zeros((MOBA_BLOCK, HEAD_PAD), F32)
            den = jnp.zeros((MOBA_BLOCK, HEAD_PAD), F32)
            for part, ls in zip(parts, lses):
                w = jnp.exp(ls - top)
                num = num + w * part
                den = den + w
            atts.append((num / den).astype(BF16))
        y = y + jnp.dot(jnp.concatenate(atts, axis=0), wa_ref[hh], preferred_element_type=F32)
    o_ref[...] = x_ref[...] + mod_ref[2:3, :] * y


def _pm_out_call(slopes, x, mod, pooled, own_parts, got_parts, wp, wa, *, layer, blocks=2):
    tm = blocks * MOBA_BLOCK
    return pl.pallas_call(
        functools.partial(_pm_out_kernel, blocks=blocks),
        grid=(SEQ // tm,),
        in_specs=[
            pl.BlockSpec(memory_space=pltpu.SMEM),
            pl.BlockSpec((tm, D_MODEL), lambda i: (i, 0)),
            _resident((None, None, 3, D_MODEL), lambda i: (layer, 1, 0, 0)),
            pl.BlockSpec((tm, C_WIDTH), lambda i: (i, 0)),
        ] + [pl.BlockSpec((SPLIT_HEADS, tm, HEAD_PAD), lambda i: (0, i, 0))] * N_SPLITS
        + [pl.BlockSpec((MOBA_TOPK, PART_HEADS, tm, HEAD_PAD), lambda i: (0, 0, i, 0))]
        * len(got_parts)
        + [
            _full((C_WIDTH, D_MODEL)),
            _full((D_HEADS, HEAD_PAD, D_MODEL)),
        ],
        out_specs=pl.BlockSpec((tm, D_MODEL), lambda i: (i, 0)),
        out_shape=jax.ShapeDtypeStruct((SEQ, D_MODEL), F32),
        compiler_params=_params(1),
        name="pool_moba_out",
    )(slopes, x, mod, pooled, *own_parts, *got_parts, wp, wa)


def _head_pad(w, axis):
    shape = list(w.shape)
    shape[axis:axis + 1] = [D_HEADS, HEAD_DIM]
    pad = [(0, 0)] * len(shape)
    pad[axis + 1] = (0, HEAD_PAD - HEAD_DIM)
    out = jnp.pad(w.reshape(shape), pad)
    shape[axis:axis + 2] = [D_HEADS * HEAD_PAD]
    return out.reshape(shape)


def kernel(x, c, ada_w, ada_b, ffn_norm, ffn_w_gate, ffn_w_up, ffn_w_down, mix_norm, conv_w_in,
           conv_a_w, conv_a_b, conv_a_ln_g, conv_a_ln_b, conv_b_w, conv_w_out, pm_w_in, pool_w,
           pool_b, pool_scale, pm_w_out, final_norm):
    assert x.shape == (1, SEQ, D_MODEL) and c.shape == (1, D_MODEL)
    xs = x.reshape(SEQ, D_MODEL)
    mod = _mod_call(c, ada_w, ada_b)

    ffn_g = ffn_norm.reshape(DEPTH, 2, 1, D_MODEL)
    mix_g = mix_norm.reshape(DEPTH, 1, D_MODEL)
    fin = final_norm.reshape(1, D_MODEL)
    ffn_stacks = (ffn_w_gate, ffn_w_up, ffn_w_down)

    def ffn(xs, weights, layer, sub):
        last = (layer, sub) == (DEPTH - 1, 1)
        after = None if last else ((layer + sub, 1 - sub), *ffn_stacks)
        return _ffn_call(xs, mod, ffn_g, weights, fin, after, layer=layer, sub=sub, final=last)

    xs, ffn_w = ffn(xs, tuple(w[0, 0].astype(BF16) for w in ffn_stacks), layer=0, sub=0)
    grp = np.arange(A_WIDTH) // (A_WIDTH // A_GROUPS)
    seg = jnp.asarray((grp[:, None] == grp[None, :]) / (A_WIDTH // A_GROUPS), dtype=BF16)
    xs = _conv_mix_call(
        xs, mod, mix_g, conv_w_in[0].astype(BF16), conv_a_w[0], conv_a_b[0].reshape(1, A_WIDTH),
        conv_a_ln_g[0].reshape(1, A_WIDTH), conv_a_ln_b[0].reshape(1, A_WIDTH), conv_b_w[0],
        conv_w_out[0].astype(BF16), seg, layer=0)
    xs, ffn_w = ffn(xs, ffn_w, layer=0, sub=1)

    xs, ffn_w = ffn(xs, ffn_w, layer=1, sub=0)
    w_in = pm_w_in[0].astype(BF16)
    w_u = w_in[:, 0:C_WIDTH]
    w_q = w_in[:, C_WIDTH:C_WIDTH + D_WIDTH]
    w_kvt = w_in[:, C_WIDTH + D_WIDTH:C_WIDTH + 3 * D_WIDTH].T
    pooled, qrows, kv, kmean = _pm_proj_call(
        xs, mod, mix_g, w_u, w_q, w_kvt, pool_w[0].astype(BF16),
        pool_b[0].reshape(1, C_WIDTH), pool_scale[0].reshape(1, C_WIDTH), layer=1)
    kmean_h = _head_pad(kmean, 1).reshape(N_BLOCKS, D_HEADS, HEAD_PAD).transpose(1, 0, 2)

    n_pair = MOBA_TOPK * MOBA_BLOCK
    tri = jnp.asarray(np.arange(n_pair)[:, None] < np.arange(n_pair)[None, :], dtype=BF16)
    q_flat = qrows.reshape(D_HEADS * SEQ, HEAD_PAD)
    own_parts, got_parts = [], []
    anchor = jnp.zeros((1,), I32)
    for split in range(N_SPLITS):
        sel, rank, cnt, own_part = _gate_own_call(anchor, qrows, kv, kmean_h, tri, split=split)
        base_digits, tile_h, tile_j, part_bounds, tile_step = _route_tables(cnt, split)
        pos = _route_pos_call(sel, rank, base_digits)
        anchor = pos[0, 0, 0:1]
        pos_by_slot = [pos[s].reshape(1, SPLIT_HEADS * SEQ) for s in range(MOBA_TOPK)]
        routed_q = _dispatch_call(q_flat, pos_by_slot, split=split)
        own_parts.append(own_part)
        for part in range(N_PARTS):
            routed_parts = _routed_attn_call(tile_h, tile_j, part_bounds, tile_step, routed_q, kv,
                                             split=split, part=part)
            part_pos = pos[:, part * PART_HEADS:(part + 1) * PART_HEADS, :]
            got = _collect_call(routed_parts, part_pos.reshape(1, N_PAIRS // N_PARTS))
            got_parts.append(got.reshape(MOBA_TOPK, PART_HEADS, SEQ, HEAD_PAD))

    slopes = jnp.asarray(2.0 ** (-8.0 * np.arange(1, D_HEADS + 1) / D_HEADS), dtype=F32)
    w_out = pm_w_out[0].astype(BF16)
    w_att = _head_pad(w_out[C_WIDTH:], 0).reshape(D_HEADS, HEAD_PAD, D_MODEL)
    xs = _pm_out_call(slopes, xs, mod, pooled, own_parts, got_parts, w_out[0:C_WIDTH], w_att,
                      layer=1)
    xs, _ = ffn(xs, ffn_w, layer=1, sub=1)
    return xs.reshape(1, SEQ, D_MODEL)
```

```python
import functools

import numpy as np
import jax
import jax.numpy as jnp
from jax import lax
from jax.experimental import pallas as pl
from jax.experimental.pallas import tpu as pltpu
from jax.experimental.pallas import tpu_sc as plsc

D_MODEL = 1024
SEQ = 16384
DEPTH = 2
N_SUBLAYERS = 3
D_FF = 2816
EPS = 1e-6

A_WIDTH = 512
A_GROUPS = 8
A_CONV = 31
B_WIDTH = 512
B_CONV = 3

POOL_WINDOWS = (2, 4, 8, 16)
C_WIDTH = 512
C_GROUP_DIM = 128
D_HEADS = 8
HEAD_DIM = 64
D_WIDTH = 512
MOBA_BLOCK = 256
MOBA_TOPK = 3
N_BLOCKS = SEQ // MOBA_BLOCK

V7X_SUBLANES = 8
V7X_BF16_SUBLANES = 16
V7X_VMEM_LIMIT_BYTES = 56 * 1024 * 1024

A_HALO = 32
B_HALO = 8
POOL_HALO = 16
NEG_BIG = -1e30

HEAD_PAD = 128
ALIBI_COL = HEAD_DIM
SLAB_K_ROWS = HEAD_DIM + V7X_BF16_SUBLANES
SLAB_ROWS = SLAB_K_ROWS + HEAD_DIM

SPLIT_HEADS = 4
N_SPLITS = D_HEADS // SPLIT_HEADS
ROUTE_TILE = 256
ROUTE_TILES_PER_STEP = 32
GATE_BLOCKS_PER_STEP = 4
N_PAIRS = SPLIT_HEADS * SEQ * MOBA_TOPK
N_GROUPS = SPLIT_HEADS * N_BLOCKS
MAX_TILES = N_PAIRS // ROUTE_TILE + N_GROUPS
TRASH_ROW0 = MAX_TILES * ROUTE_TILE
ROUTE_ROWS = (MAX_TILES + ROUTE_TILES_PER_STEP) * ROUTE_TILE
SC_WINDOW = 128
DIGIT = 128
assert MAX_TILES % ROUTE_TILES_PER_STEP == 0 and N_BLOCKS % GATE_BLOCKS_PER_STEP == 0
assert ROUTE_ROWS <= DIGIT ** 3 and MOBA_BLOCK <= ROUTE_TILE

BF16 = jnp.bfloat16
F32 = jnp.float32
I32 = jnp.int32


def _params(n_axes):
    return pltpu.CompilerParams(
        dimension_semantics=("arbitrary",) * n_axes,
        vmem_limit_bytes=V7X_VMEM_LIMIT_BYTES)


def _resident(block_shape, index_map):
    return pl.BlockSpec(block_shape, index_map, pipeline_mode=pl.Buffered(1))


def _full(shape):
    return _resident(shape, lambda *_: (0,) * len(shape))


def _sigmoid(v):
    return 1.0 / (1.0 + jnp.exp(-v))


def _norm_mod(x, g, mod):
    r = lax.rsqrt(jnp.mean(x * x, axis=-1, keepdims=True) + EPS)
    return (x * r) * (g * (1.0 + mod[1:2])) + mod[0:1]


def _split(v):
    hi = v.astype(BF16)
    return hi, (v - hi.astype(F32)).astype(BF16)


def _split_dot(v, w):
    hi, lo = _split(v)
    return (jnp.dot(hi, w, preferred_element_type=F32)
            + jnp.dot(lo, w, preferred_element_type=F32))


_NT = (((1,), (1,)), ((), ()))


def _mod_kernel(c_ref, w_ref, b_ref, o_ref):
    c = c_ref[...]
    cond = c * _sigmoid(c)
    o_ref[0] = jnp.sum(w_ref[0] * cond, axis=0, keepdims=True) + b_ref[0]


def _mod_call(c, ada_w, ada_b):
    n_out = N_SUBLAYERS * 3 * D_MODEL
    tn = D_MODEL
    out = pl.pallas_call(
        _mod_kernel,
        grid=(DEPTH, n_out // tn),
        in_specs=[
            pl.BlockSpec((D_MODEL, 1), lambda l, j: (0, 0)),
            pl.BlockSpec((1, D_MODEL, tn), lambda l, j: (l, 0, j)),
            pl.BlockSpec((1, 1, tn), lambda l, j: (l, 0, j)),
        ],
        out_specs=pl.BlockSpec((1, 1, tn), lambda l, j: (l, 0, j)),
        out_shape=jax.ShapeDtypeStruct((DEPTH, 1, n_out), F32),
        compiler_params=_params(2),
        name="adaln_mod",
    )(c.reshape(D_MODEL, 1), ada_w, ada_b.reshape(DEPTH, 1, n_out))
    return out.reshape(DEPTH, N_SUBLAYERS, 3, D_MODEL)


def _ffn_kernel(x_ref, mod_ref, g_ref, wg_ref, wu_ref, wd_ref, fin_ref, *refs, final):
    o_ref = refs[len(refs) // 2]
    for src_ref, dst_ref in zip(refs[:len(refs) // 2], refs[len(refs) // 2 + 1:]):
        dst_ref[...] = src_ref[...].astype(BF16)
    x = x_ref[...]
    mod = mod_ref[...]
    h = _norm_mod(x, g_ref[...], mod).astype(BF16)
    gate = jnp.dot(h, wg_ref[...], preferred_element_type=F32)
    up = jnp.dot(h, wu_ref[...], preferred_element_type=F32)
    act = (gate * _sigmoid(gate) * up).astype(BF16)
    y = jnp.dot(act, wd_ref[...], preferred_element_type=F32)
    xn = x + (0.5 * mod[2:3]) * y
    if final:
        r = lax.rsqrt(jnp.mean(xn * xn, axis=-1, keepdims=True) + EPS)
        xn = xn * r * fin_ref[...]
    o_ref[...] = xn


def _ffn_call(x, mod, norm_g, weights, fin, next_weights, *, layer, sub, final, tm=512):
    mod_sub = 0 if sub == 0 else 2
    steps = SEQ // tm
    in_specs = [
        pl.BlockSpec((tm, D_MODEL), lambda i: (i, 0)),
        _resident((None, None, 3, D_MODEL), lambda i: (layer, mod_sub, 0, 0)),
        _resident((None, None, 1, D_MODEL), lambda i: (layer, sub, 0, 0)),
        _full((D_MODEL, D_FF)),
        _full((D_MODEL, D_FF)),
        _full((D_FF, D_MODEL)),
        _full((1, D_MODEL)),
    ]
    out_specs = [pl.BlockSpec((tm, D_MODEL), lambda i: (i, 0))]
    out_shape = [jax.ShapeDtypeStruct((SEQ, D_MODEL), F32)]
    operands = [x, mod, norm_g, *weights, fin]
    if next_weights is not None:
        (nl, ns), *stacks = next_weights
        for w in stacks:
            n_rows, n_cols = w.shape[2:]
            share = 1 if n_rows % (steps * V7X_BF16_SUBLANES) == 0 else 2
            rows = n_rows * share // steps
            assert n_rows * share % steps == 0 and rows % V7X_BF16_SUBLANES == 0
            in_specs.append(pl.BlockSpec((None, None, rows, n_cols),
                                         lambda i, share=share: (nl, ns, i // share, 0)))
            out_specs.append(pl.BlockSpec((rows, n_cols), lambda i, share=share: (i // share, 0)))
            out_shape.append(jax.ShapeDtypeStruct((n_rows, n_cols), BF16))
            operands.append(w)
    out = pl.pallas_call(
        functools.partial(_ffn_kernel, final=final),
        grid=(steps,),
        in_specs=in_specs,
        out_specs=out_specs,
        out_shape=out_shape,
        compiler_params=_params(1),
        name=f"ffn_l{layer}s{sub}",
    )(*operands)
    return out[0], tuple(out[1:])


def _conv_mix_kernel(x_ref, mod_ref, g_ref, win_ref, caw_ref, cab_ref, lng_ref, lnb_ref,
                     cbw_ref, wout_ref, seg_ref, o_ref, abuf, bbuf, sbuf, *, tm):
    i = pl.program_id(0)

    @pl.when(i == 0)
    def _():
        abuf[0:A_HALO, :] = jnp.zeros((A_HALO, A_WIDTH), F32)
        bbuf[0:B_HALO, :] = jnp.zeros((B_HALO, B_WIDTH), F32)

    x = x_ref[...]
    mod = mod_ref[...]
    h = _norm_mod(x, g_ref[...], mod).astype(BF16)
    z = jnp.dot(h, win_ref[...], preferred_element_type=F32)

    a = z[:, 0:A_WIDTH] * _sigmoid(z[:, A_WIDTH:2 * A_WIDTH])
    abuf[A_HALO:A_HALO + tm, :] = a
    acc = jnp.zeros((tm, A_WIDTH), F32) + cab_ref[...]
    ext = tm + V7X_SUBLANES
    for r in range(V7X_SUBLANES):
        part = None
        for q in range((A_CONV - 1 - r) // V7X_SUBLANES + 1):
            k = A_CONV - 1 - (V7X_SUBLANES * q + r)
            term = caw_ref[k:k + 1, :] * abuf[pl.ds(A_HALO - V7X_SUBLANES * (q + 1), ext), :]
            part = term if part is None else part + term
        if r == 0:
            acc = acc + part[V7X_SUBLANES:, :]
        else:
            sbuf[r - 1] = part
            acc = acc + sbuf[r - 1, pl.ds(V7X_SUBLANES - r, tm), :]
    abuf[0:A_HALO, :] = abuf[tm:tm + A_HALO, :]
    seg = seg_ref[...]
    mu = _split_dot(acc, seg)
    d = acc - mu
    var = _split_dot(d * d, seg)
    yn = d * lax.rsqrt(var + EPS) * lng_ref[...] + lnb_ref[...]
    a_out = yn * _sigmoid(yn)

    off = 2 * A_WIDTH
    cv = z[:, off + B_WIDTH:off + 2 * B_WIDTH] * z[:, off + 2 * B_WIDTH:off + 3 * B_WIDTH]
    bbuf[B_HALO:B_HALO + tm, :] = cv
    bacc = jnp.zeros((tm, B_WIDTH), F32)
    for k in range(B_CONV):
        bacc = bacc + cbw_ref[k:k + 1, :] * bbuf[pl.ds(B_HALO - (B_CONV - 1) + k, tm), :]
    bbuf[0:B_HALO, :] = bbuf[tm:tm + B_HALO, :]
    bb = z[:, off:off + B_WIDTH] * bacc

    y = (jnp.dot(a_out.astype(BF16), wout_ref[0:A_WIDTH, :], preferred_element_type=F32)
         + jnp.dot(bb.astype(BF16), wout_ref[A_WIDTH:A_WIDTH + B_WIDTH, :],
                   preferred_element_type=F32))
    o_ref[...] = x + mod[2:3] * y


def _conv_mix_call(x, mod, norm_g, w_in, caw, cab, lng, lnb, cbw, w_out, seg, *, layer, tm=512):
    even_in = w_in.shape[1]
    return pl.pallas_call(
        functools.partial(_conv_mix_kernel, tm=tm),
        grid=(SEQ // tm,),
        in_specs=[
            pl.BlockSpec((tm, D_MODEL), lambda i: (i, 0)),
            _resident((None, None, 3, D_MODEL), lambda i: (layer, 1, 0, 0)),
            _resident((None, 1, D_MODEL), lambda i: (layer, 0, 0)),
            _full((D_MODEL, even_in)),
            _full((A_CONV, A_WIDTH)),
            _full((1, A_WIDTH)),
            _full((1, A_WIDTH)),
            _full((1, A_WIDTH)),
            _full((B_CONV, B_WIDTH)),
            _full((A_WIDTH + B_WIDTH, D_MODEL)),
            _full((A_WIDTH, A_WIDTH)),
        ],
        out_specs=pl.BlockSpec((tm, D_MODEL), lambda i: (i, 0)),
        out_shape=jax.ShapeDtypeStruct((SEQ, D_MODEL), F32),
        scratch_shapes=[pltpu.VMEM((A_HALO + tm, A_WIDTH), F32),
                        pltpu.VMEM((B_HALO + tm, B_WIDTH), F32),
                        pltpu.VMEM((V7X_SUBLANES - 1, tm + V7X_SUBLANES, A_WIDTH), F32)],
        compiler_params=_params(1),
        name="conv_mixers",
    )(x, mod, norm_g, w_in, caw, cab, lng, lnb, cbw, w_out, seg)


def _pm_proj_kernel(x_ref, mod_ref, g_ref, wu_ref, wq_ref, wkvt_ref, pw_ref,
                    pb_ref, ps_ref, pooled_ref, qrow_ref, kv_ref, km_ref,
                    ubuf, hmean, *, tm):
    i = pl.program_id(0)

    @pl.when(i == 0)
    def _():
        ubuf[0:POOL_HALO, :] = jnp.zeros((POOL_HALO, C_WIDTH), F32)

    x = x_ref[...]
    h = _norm_mod(x, g_ref[...], mod_ref[...]).astype(BF16)
    u = jnp.dot(h, wu_ref[...], preferred_element_type=F32)
    qr = jnp.dot(h, wq_ref[...], preferred_element_type=F32)
    kvt = lax.dot_general(wkvt_ref[...], h, _NT, preferred_element_type=F32)

    ubuf[POOL_HALO:POOL_HALO + tm, :] = u
    t1 = (i * tm + 1 + lax.broadcasted_iota(I32, (tm, C_GROUP_DIM), 0)).astype(F32)
    for gi, w in enumerate(POOL_WINDOWS):
        c0 = gi * C_GROUP_DIM
        s = u[:, c0:c0 + C_GROUP_DIM]
        for k in range(1, w):
            s = s + ubuf[pl.ds(POOL_HALO - k, tm), c0:c0 + C_GROUP_DIM]
        pooled = s / jnp.minimum(t1, float(w)) - u[:, c0:c0 + C_GROUP_DIM]
        mixed = (jnp.dot(pooled.astype(BF16), pw_ref[gi], preferred_element_type=F32)
                 + pb_ref[:, c0:c0 + C_GROUP_DIM])
        pooled_ref[:, c0:c0 + C_GROUP_DIM] = (
            mixed * ps_ref[:, c0:c0 + C_GROUP_DIM]).astype(BF16)
    ubuf[0:POOL_HALO, :] = ubuf[tm:tm + POOL_HALO, :]

    lane = lax.broadcasted_iota(I32, (tm, HEAD_PAD), 1)
    tail = jnp.where(lane == ALIBI_COL, 1.0, 0.0)
    qs = qr * (HEAD_DIM ** -0.5)
    for pair in range(D_HEADS // 2):
        both = qs[:, pair * HEAD_PAD:(pair + 1) * HEAD_PAD]
        qrow_ref[2 * pair] = jnp.where(lane < HEAD_DIM, both, tail)
        qrow_ref[2 * pair + 1] = jnp.where(
            lane < HEAD_DIM, pltpu.roll(both, HEAD_DIM, axis=1), tail)
    per_tile = tm // MOBA_BLOCK
    kvb = kvt.astype(BF16)
    n_alibi = SLAB_K_ROWS - HEAD_DIM
    first = lax.broadcasted_iota(I32, (n_alibi, MOBA_BLOCK), 0) == 0
    koff = lax.broadcasted_iota(I32, (n_alibi, MOBA_BLOCK), 1).astype(F32)
    for hh in range(D_HEADS):
        alibi = jnp.where(first, koff * (2.0 ** -(hh + 1)), 0.0).astype(BF16)
        for sb in range(per_tile):
            keys = slice(sb * MOBA_BLOCK, (sb + 1) * MOBA_BLOCK)
            kv_ref[hh, sb, 0:HEAD_DIM, :] = kvb[hh * HEAD_DIM:(hh + 1) * HEAD_DIM, keys]
            kv_ref[hh, sb, HEAD_DIM:SLAB_K_ROWS, :] = alibi
            kv_ref[hh, sb, SLAB_K_ROWS:SLAB_ROWS, :] = kvb[
                D_WIDTH + hh * HEAD_DIM:D_WIDTH + (hh + 1) * HEAD_DIM, keys]

    for sb in range(per_tile):
        blk = h[sb * MOBA_BLOCK:(sb + 1) * MOBA_BLOCK, :].astype(F32)
        hmean[pl.ds(i * per_tile + sb, 1), :] = jnp.mean(blk, axis=0, keepdims=True)

    @pl.when(i == pl.num_programs(0) - 1)
    def _():
        hi, lo = _split(hmean[...])
        wkt = wkvt_ref[0:D_WIDTH, :]
        km_ref[...] = (lax.dot_general(hi, wkt, _NT, preferred_element_type=F32)
                       + lax.dot_general(lo, wkt, _NT, preferred_element_type=F32))


def _pm_proj_call(x, mod, norm_g, wu, wq, wkvt, pw, pb, ps, *, layer, tm=512):
    return pl.pallas_call(
        functools.partial(_pm_proj_kernel, tm=tm),
        grid=(SEQ // tm,),
        in_specs=[
            pl.BlockSpec((tm, D_MODEL), lambda i: (i, 0)),
            _resident((None, None, 3, D_MODEL), lambda i: (layer, 1, 0, 0)),
            _resident((None, 1, D_MODEL), lambda i: (layer, 0, 0)),
            _full((D_MODEL, C_WIDTH)),
            _full((D_MODEL, D_WIDTH)),
            _full((2 * D_WIDTH, D_MODEL)),
            _full((len(POOL_WINDOWS), C_GROUP_DIM, C_GROUP_DIM)),
            _full((1, C_WIDTH)),
            _full((1, C_WIDTH)),
        ],
        out_specs=[
            pl.BlockSpec((tm, C_WIDTH), lambda i: (i, 0)),
            pl.BlockSpec((D_HEADS, tm, HEAD_PAD), lambda i: (0, i, 0)),
            pl.BlockSpec((D_HEADS, tm // MOBA_BLOCK, SLAB_ROWS, MOBA_BLOCK),
                         lambda i: (0, i, 0, 0)),
            pl.BlockSpec((N_BLOCKS, D_WIDTH), lambda i: (0, 0)),
        ],
        out_shape=[
            jax.ShapeDtypeStruct((SEQ, C_WIDTH), BF16),
            jax.ShapeDtypeStruct((D_HEADS, SEQ, HEAD_PAD), F32),
            jax.ShapeDtypeStruct((D_HEADS, N_BLOCKS, SLAB_ROWS, MOBA_BLOCK), BF16),
            jax.ShapeDtypeStruct((N_BLOCKS, D_WIDTH), F32),
        ],
        scratch_shapes=[pltpu.VMEM((POOL_HALO + tm, C_WIDTH), F32),
                        pltpu.VMEM((N_BLOCKS, D_MODEL), F32)],
        compiler_params=_params(1),
        name="pool_qkv_proj",
    )(x, mod, norm_g, wu, wq, wkvt, pw, pb, ps)


def _block_rows(qs, slabs, mask, lse_add):
    n = qs[0].shape[0]
    kpad = jnp.zeros((HEAD_PAD - SLAB_K_ROWS, MOBA_BLOCK), BF16)
    s = jnp.concatenate(
        [jnp.dot(q, jnp.concatenate([slab[0:SLAB_K_ROWS, :], kpad], axis=0),
                 preferred_element_type=F32)
         for q, slab in zip(qs, slabs)], axis=0)
    if mask is not None:
        s = jnp.where(mask, s, NEG_BIG)
    m = jnp.max(s, axis=1, keepdims=True)
    p = jnp.exp(s - m).astype(BF16)
    vpad = jnp.zeros((HEAD_PAD - HEAD_DIM, MOBA_BLOCK), BF16)
    ones = jnp.ones((HEAD_PAD, MOBA_BLOCK), BF16)
    ot = jnp.concatenate(
        [lax.dot_general(p[u * n:(u + 1) * n],
                         jnp.concatenate([slab[SLAB_K_ROWS:SLAB_ROWS, :], vpad, ones], axis=0),
                         _NT, preferred_element_type=F32)
         for u, slab in enumerate(slabs)], axis=0)
    o, tot = ot[:, 0:HEAD_PAD], ot[:, HEAD_PAD:2 * HEAD_PAD]
    lse = m + jnp.log(tot)
    if lse_add is not None:
        lse = jnp.concatenate(
            [lse[u * n:(u + 1) * n] + add for u, add in enumerate(lse_add)], axis=0)
    low = lax.broadcasted_iota(I32, o.shape, 1) < HEAD_DIM
    return jnp.where(low, o / tot, lse)


def _onehot_pairs(sel_rows):
    blk = lax.broadcasted_iota(I32, (N_BLOCKS, MOBA_BLOCK), 0)
    return jnp.concatenate(
        [jnp.where(blk == sel_rows[s], 1.0, 0.0) for s in range(MOBA_TOPK)], axis=1)


def _gate_own_kernel(anchor_ref, qrow_ref, kv_ref, km_ref, tri_ref,
                     sel_ref, rank_ref, cnt_ref, own_ref, *, heads, blocks):
    del anchor_ref
    n_pair = MOBA_TOPK * MOBA_BLOCK
    tiles = [(bb, a) for bb in range(blocks) for a in range(heads)]
    rows_of = lambda bb: slice(bb * MOBA_BLOCK, (bb + 1) * MOBA_BLOCK)
    qs = [qrow_ref[a, rows_of(bb), :].astype(BF16) for bb, a in tiles]

    kms = [_split(km_ref[a]) for a in range(heads)]
    gates = [lax.dot_general(kms[a][0], q, _NT, preferred_element_type=F32)
             + lax.dot_general(kms[a][1], q, _NT, preferred_element_type=F32)
             for (bb, a), q in zip(tiles, qs)]
    width = len(tiles) * MOBA_BLOCK
    blk = lax.broadcasted_iota(I32, (N_BLOCKS, width), 0)
    lane = lax.broadcasted_iota(I32, (1, width), 1)
    own = pl.program_id(0) * blocks + lane // (heads * MOBA_BLOCK)
    g = jnp.where(blk < own, jnp.concatenate(gates, axis=1), -jnp.inf)
    slots = []
    for s in range(MOBA_TOPK):
        best = jnp.max(g, axis=0, keepdims=True)
        idx = jnp.min(jnp.where(g == best, blk, N_BLOCKS), axis=0, keepdims=True)
        g = jnp.where(blk == idx, -jnp.inf, g)
        slots.append(jnp.where(s < own, idx, N_BLOCKS))

    onehots = []
    for t, (bb, a) in enumerate(tiles):
        sel_rows = [slot[:, t * MOBA_BLOCK:(t + 1) * MOBA_BLOCK] for slot in slots]
        sel_ref[a, :, rows_of(bb)] = jnp.concatenate(sel_rows, axis=0)
        onehots.append(_onehot_pairs(sel_rows))
    onehot = jnp.concatenate(onehots, axis=0)
    oh = onehot.astype(BF16)
    before = jnp.dot(oh, tri_ref[...], preferred_element_type=F32)
    hit = onehot * before
    ones = jnp.ones((8, n_pair), BF16)
    cnt = lax.dot_general(ones, oh, _NT, preferred_element_type=F32)
    for t, (bb, a) in enumerate(tiles):
        rank = jnp.sum(hit[t * N_BLOCKS:(t + 1) * N_BLOCKS], axis=0, keepdims=True).astype(I32)
        rank_ref[a, :, rows_of(bb)] = jnp.concatenate(
            [rank[:, s * MOBA_BLOCK:(s + 1) * MOBA_BLOCK] for s in range(MOBA_TOPK)], axis=0)
        cnt_ref[a, bb] = cnt[0:1, t * N_BLOCKS:(t + 1) * N_BLOCKS].astype(I32)

    qi = lax.broadcasted_iota(I32, (width, MOBA_BLOCK), 0) % MOBA_BLOCK
    ki = lax.broadcasted_iota(I32, (width, MOBA_BLOCK), 1)
    rows = _block_rows(qs, [kv_ref.at[a, bb] for bb, a in tiles], ki <= qi, None)
    for t, (bb, a) in enumerate(tiles):
        own_ref[a, rows_of(bb), :] = rows[t * MOBA_BLOCK:(t + 1) * MOBA_BLOCK]


def _gate_own_call(anchor, qrows, kv, kmean_h, tri, *, split, blocks=GATE_BLOCKS_PER_STEP):
    heads = SPLIT_HEADS
    rows = blocks * MOBA_BLOCK
    return pl.pallas_call(
        functools.partial(_gate_own_kernel, heads=heads, blocks=blocks),
        grid=(N_BLOCKS // blocks,),
        in_specs=[
            pl.BlockSpec(memory_space=pltpu.SMEM),
            pl.BlockSpec((heads, rows, HEAD_PAD), lambda b: (split, b, 0)),
            pl.BlockSpec((heads, blocks, SLAB_ROWS, MOBA_BLOCK), lambda b: (split, b, 0, 0)),
            pl.BlockSpec((heads, N_BLOCKS, HEAD_PAD), lambda b: (split, 0, 0)),
            _full((MOBA_TOPK * MOBA_BLOCK, MOBA_TOPK * MOBA_BLOCK)),
        ],
        out_specs=[
            pl.BlockSpec((heads, MOBA_TOPK, rows), lambda b: (0, 0, b)),
            pl.BlockSpec((heads, MOBA_TOPK, rows), lambda b: (0, 0, b)),
            pl.BlockSpec((heads, blocks, 1, N_BLOCKS), lambda b: (0, b, 0, 0)),
            pl.BlockSpec((heads, rows, HEAD_PAD), lambda b: (0, b, 0)),
        ],
        out_shape=[
            jax.ShapeDtypeStruct((heads, MOBA_TOPK, SEQ), I32),
            jax.ShapeDtypeStruct((heads, MOBA_TOPK, SEQ), I32),
            jax.ShapeDtypeStruct((heads, N_BLOCKS, 1, N_BLOCKS), I32),
            jax.ShapeDtypeStruct((heads, SEQ, HEAD_PAD), F32),
        ],
        compiler_params=_params(1),
        name=f"moba_gate_own_{split}",
    )(anchor, qrows, kv, kmean_h, tri)


def _route_pos_kernel(sel_ref, rank_ref, base_ref, pos_ref, *, blocks):
    lane = lax.broadcasted_iota(I32, (1, MOBA_BLOCK), 1)
    for bb in range(blocks):
        qs = slice(bb * MOBA_BLOCK, (bb + 1) * MOBA_BLOCK)
        rows = [[] for _ in range(MOBA_TOPK)]
        for a in range(SPLIT_HEADS):
            sel = sel_ref[a, :, qs]
            onehot = _onehot_pairs([sel[s:s + 1, :] for s in range(MOBA_TOPK)]).astype(BF16)
            dig = jnp.dot(base_ref[a, bb].astype(BF16), onehot,
                          preferred_element_type=F32)
            base = (dig[0:1] * float(DIGIT * DIGIT) + dig[1:2] * float(DIGIT)
                    + dig[2:3]).astype(I32)
            for s in range(MOBA_TOPK):
                p = base[:, s * MOBA_BLOCK:(s + 1) * MOBA_BLOCK] + rank_ref[a, s:s + 1, qs]
                rows[s].append(
                    jnp.where(sel[s:s + 1, :] < N_BLOCKS, p, TRASH_ROW0 + lane % SC_WINDOW))
        for s in range(MOBA_TOPK):
            pos_ref[s, :, qs] = jnp.concatenate(rows[s], axis=0)


def _route_pos_call(sel, rank, base_digits, *, blocks=8):
    width = blocks * MOBA_BLOCK
    return pl.pallas_call(
        functools.partial(_route_pos_kernel, blocks=blocks),
        grid=(N_BLOCKS // blocks,),
        in_specs=[
            pl.BlockSpec((SPLIT_HEADS, MOBA_TOPK, width), lambda b: (0, 0, b)),
            pl.BlockSpec((SPLIT_HEADS, MOBA_TOPK, width), lambda b: (0, 0, b)),
            pl.BlockSpec((SPLIT_HEADS, blocks, 8, N_BLOCKS), lambda b: (0, b, 0, 0)),
        ],
        out_specs=pl.BlockSpec((MOBA_TOPK, SPLIT_HEADS, width), lambda b: (0, 0, b)),
        out_shape=jax.ShapeDtypeStruct((MOBA_TOPK, SPLIT_HEADS, SEQ), I32),
        compiler_params=_params(1),
        name="moba_route_pos",
    )(sel, rank, base_digits)


def _route_tables(cnt, split):
    cnt = cnt.reshape(SPLIT_HEADS, N_BLOCKS, N_BLOCKS)
    tiles = (cnt.sum(axis=1) + ROUTE_TILE - 1) // ROUTE_TILE
    tiles_flat = tiles.reshape(N_GROUPS)
    tile0 = jnp.cumsum(tiles_flat) - tiles_flat
    n_tiles = tiles_flat.sum().astype(I32)
    within = jnp.cumsum(cnt, axis=1) - cnt
    base = tile0.reshape(SPLIT_HEADS, 1, N_BLOCKS) * ROUTE_TILE + within
    digits = jnp.stack([base // (DIGIT * DIGIT), (base // DIGIT) % DIGIT, base % DIGIT],
                       axis=2).astype(F32)
    digits = jnp.pad(digits, ((0, 0), (0, 0), (0, 8 - 3), (0, 0)))
    t = jnp.arange(MAX_TILES, dtype=I32)
    ended = (tile0 + tiles_flat)[None, :] <= t[:, None]
    group_of_tile = jnp.minimum(ended.sum(axis=1).astype(I32), N_GROUPS - 1)
    tile_h = group_of_tile // N_BLOCKS
    head = split * SPLIT_HEADS + tile_h
    tile_step = lax.bitcast_convert_type((127 + 7 - head) << 23, F32)
    return digits, tile_h, group_of_tile % N_BLOCKS, n_tiles.reshape(1), tile_step


def _sc_mesh():
    return plsc.VectorSubcoreMesh(core_axis_name="core", subcore_axis_name="subcore")


def _dispatch_call(rows, pos_by_slot, *, split):
    n_rows = SPLIT_HEADS * SEQ
    first_window = split * n_rows // SC_WINDOW

    @functools.partial(
        pl.kernel, mesh=_sc_mesh(), scratch_types=[],
        out_type=jax.ShapeDtypeStruct((ROUTE_ROWS, HEAD_PAD), F32))
    def dispatch(x_hbm, i0_hbm, i1_hbm, i2_hbm, o_hbm):
        def body(x_vmem, i0_vmem, i1_vmem, i2_vmem):
            pltpu.sync_copy(x_vmem, o_hbm.at[i0_vmem.at[0]])
            pltpu.sync_copy(x_vmem, o_hbm.at[i1_vmem.at[0]])
            pltpu.sync_copy(x_vmem, o_hbm.at[i2_vmem.at[0]])

        idx_spec = pl.BlockSpec((1, SC_WINDOW), lambda i: (0, i))
        pltpu.emit_pipeline(
            body,
            grid=(n_rows // SC_WINDOW,),
            in_specs=[pl.BlockSpec((SC_WINDOW, HEAD_PAD), lambda i: (first_window + i, 0)),
                      idx_spec, idx_spec, idx_spec],
            out_specs=[],
            core_axis_name=("core", "subcore"),
            dimension_semantics=(pltpu.PARALLEL,),
        )(x_hbm, i0_hbm, i1_hbm, i2_hbm)

    return dispatch(rows, *pos_by_slot)


def _collect_call(table, idx):
    n_rows = idx.shape[1]

    @functools.partial(
        pl.kernel, mesh=_sc_mesh(), scratch_types=[],
        out_type=jax.ShapeDtypeStruct((n_rows, HEAD_PAD), F32))
    def collect(x_hbm, i_hbm, o_hbm):
        def body(i_vmem, o_vmem):
            pltpu.sync_copy(x_hbm.at[i_vmem.at[0]], o_vmem)

        pltpu.emit_pipeline(
            body,
            grid=(n_rows // SC_WINDOW,),
            in_specs=[pl.BlockSpec((1, SC_WINDOW), lambda i: (0, i))],
            out_specs=[pl.BlockSpec((SC_WINDOW, HEAD_PAD), lambda i: (i, 0))],
            core_axis_name=("core", "subcore"),
            dimension_semantics=(pltpu.PARALLEL,),
        )(i_hbm, o_hbm)

    return collect(table, idx)


def _routed_attn_kernel(th_ref, tj_ref, nt_ref, step_ref, q_ref, *refs, tiles):
    kv_refs, o_ref = refs[0:tiles], refs[tiles]
    t0 = pl.program_id(0) * tiles

    @pl.when(t0 < nt_ref[0])
    def _():
        qs = [q_ref[u * ROUTE_TILE:(u + 1) * ROUTE_TILE, :].astype(BF16) for u in range(tiles)]
        adds = [step_ref[t0 + u] * tj_ref[t0 + u].astype(F32) for u in range(tiles)]
        o_ref[...] = _block_rows(qs, kv_refs, None, adds)

    @pl.when(t0 == MAX_TILES)
    def _():
        low = lax.broadcasted_iota(I32, o_ref.shape, 1) < HEAD_DIM
        o_ref[...] = jnp.where(low, 0.0, NEG_BIG)


def _routed_attn_call(tile_h, tile_j, n_tiles, tile_step, routed_q, kv,
                      *, split, tiles=ROUTE_TILES_PER_STEP):
    n_steps = MAX_TILES // tiles + 1
    assert TRASH_ROW0 == (n_steps - 1) * tiles * ROUTE_TILE
    live = lambda s, th, tj, nt: (jnp.minimum(s, (nt[0] + tiles - 1) // tiles), 0)
    live_or_last = lambda s, th, tj, nt: (
        jnp.where(s == n_steps - 1, s, jnp.minimum(s, (nt[0] + tiles - 1) // tiles)), 0)
    head0 = split * SPLIT_HEADS

    def kv_spec(u):
        def slab(s, th, tj, nt):
            t = jnp.minimum(s, n_steps - 2) * tiles + u
            return head0 + th[t], tj[t], 0, 0
        return pl.BlockSpec((None, None, SLAB_ROWS, MOBA_BLOCK), slab)

    grid_spec = pltpu.PrefetchScalarGridSpec(
        num_scalar_prefetch=3,
        grid=(n_steps,),
        in_specs=([pl.BlockSpec(memory_space=pltpu.SMEM),
                   pl.BlockSpec((tiles * ROUTE_TILE, HEAD_PAD), live)]
                  + [kv_spec(u) for u in range(tiles)]),
        out_specs=pl.BlockSpec((tiles * ROUTE_TILE, HEAD_PAD), live_or_last),
    )
    return pl.pallas_call(
        functools.partial(_routed_attn_kernel, tiles=tiles),
        grid_spec=grid_spec,
        out_shape=jax.ShapeDtypeStruct((ROUTE_ROWS, HEAD_PAD), F32),
        compiler_params=_params(1),
        name=f"moba_routed_attn_{split}",
    )(tile_h, tile_j, n_tiles, tile_step, routed_q, *([kv] * tiles))


def _pm_out_kernel(slope_ref, x_ref, mod_ref, pooled_ref, *refs, blocks):
    own_refs, got_refs = refs[0:N_SPLITS], refs[N_SPLITS:2 * N_SPLITS]
    wp_ref, wa_ref, o_ref = refs[2 * N_SPLITS:]
    low = lax.broadcasted_iota(I32, (MOBA_BLOCK, HEAD_PAD), 1) < HEAD_DIM

    def lse_of(part):
        return jnp.where(low, pltpu.roll(part, HEAD_DIM, axis=1), part)

    y = jnp.dot(pooled_ref[...], wp_ref[...], preferred_element_type=F32)
    for hh in range(D_HEADS):
        own_ref, got_ref = own_refs[hh // SPLIT_HEADS], got_refs[hh // SPLIT_HEADS]
        a = hh % SPLIT_HEADS
        atts = []
        for bb in range(blocks):
            own = pl.program_id(0) * blocks + bb
            rows = slice(bb * MOBA_BLOCK, (bb + 1) * MOBA_BLOCK)
            own_shift = slope_ref[hh] * (own * MOBA_BLOCK).astype(F32)
            parts = [own_ref[a, rows, :]] + [got_ref[s, a, rows, :] for s in range(MOBA_TOPK)]
            lses = [lse_of(parts[0]) + own_shift] + [lse_of(part) for part in parts[1:]]
            top = functools.reduce(jnp.maximum, lses)
            num = jnp.zeros((MOBA_BLOCK, HEAD_PAD), F32)
            den = jnp.zeros((MOBA_BLOCK, HEAD_PAD), F32)
            for part, ls in zip(parts, lses):
                w = jnp.exp(ls - top)
                num = num + w * part
                den = den + w
            atts.append((num / den).astype(BF16))
        y = y + jnp.dot(jnp.concatenate(atts, axis=0), wa_ref[hh], preferred_element_type=F32)
    o_ref[...] = x_ref[...] + mod_ref[2:3, :] * y


def _pm_out_call(slopes, x, mod, pooled, own_parts, got_parts, wp, wa, *, layer, blocks=2):
    tm = blocks * MOBA_BLOCK
    return pl.pallas_call(
        functools.partial(_pm_out_kernel, blocks=blocks),
        grid=(SEQ // tm,),
        in_specs=[
            pl.BlockSpec(memory_space=pltpu.SMEM),
            pl.BlockSpec((tm, D_MODEL), lambda i: (i, 0)),
            _resident((None, None, 3, D_MODEL), lambda i: (layer, 1, 0, 0)),
            pl.BlockSpec((tm, C_WIDTH), lambda i: (i, 0)),
        ] + [pl.BlockSpec((SPLIT_HEADS, tm, HEAD_PAD), lambda i: (0, i, 0))] * N_SPLITS
        + [pl.BlockSpec((MOBA_TOPK, SPLIT_HEADS, tm, HEAD_PAD), lambda i: (0, 0, i, 0))] * N_SPLITS
        + [
            _full((C_WIDTH, D_MODEL)),
            _full((D_HEADS, HEAD_PAD, D_MODEL)),
        ],
        out_specs=pl.BlockSpec((tm, D_MODEL), lambda i: (i, 0)),
        out_shape=jax.ShapeDtypeStruct((SEQ, D_MODEL), F32),
        compiler_params=_params(1),
        name="pool_moba_out",
    )(slopes, x, mod, pooled, *own_parts, *got_parts, wp, wa)


def _head_pad(w, axis):
    shape = list(w.shape)
    shape[axis:axis + 1] = [D_HEADS, HEAD_DIM]
    pad = [(0, 0)] * len(shape)
    pad[axis + 1] = (0, HEAD_PAD - HEAD_DIM)
    out = jnp.pad(w.reshape(shape), pad)
    shape[axis:axis + 2] = [D_HEADS * HEAD_PAD]
    return out.reshape(shape)


def kernel(x, c, ada_w, ada_b, ffn_norm, ffn_w_gate, ffn_w_up, ffn_w_down, mix_norm, conv_w_in,
           conv_a_w, conv_a_b, conv_a_ln_g, conv_a_ln_b, conv_b_w, conv_w_out, pm_w_in, pool_w,
           pool_b, pool_scale, pm_w_out, final_norm):
    assert x.shape == (1, SEQ, D_MODEL) and c.shape == (1, D_MODEL)
    xs = x.reshape(SEQ, D_MODEL)
    mod = _mod_call(c, ada_w, ada_b)

    ffn_g = ffn_norm.reshape(DEPTH, 2, 1, D_MODEL)
    mix_g = mix_norm.reshape(DEPTH, 1, D_MODEL)
    fin = final_norm.reshape(1, D_MODEL)
    ffn_stacks = (ffn_w_gate, ffn_w_up, ffn_w_down)

    def ffn(xs, weights, layer, sub):
        last = (layer, sub) == (DEPTH - 1, 1)
        after = None if last else ((layer + sub, 1 - sub), *ffn_stacks)
        return _ffn_call(xs, mod, ffn_g, weights, fin, after, layer=layer, sub=sub, final=last)

    xs, ffn_w = ffn(xs, tuple(w[0, 0].astype(BF16) for w in ffn_stacks), layer=0, sub=0)
    grp = np.arange(A_WIDTH) // (A_WIDTH // A_GROUPS)
    seg = jnp.asarray((grp[:, None] == grp[None, :]) / (A_WIDTH // A_GROUPS), dtype=BF16)
    xs = _conv_mix_call(
        xs, mod, mix_g, conv_w_in[0].astype(BF16), conv_a_w[0], conv_a_b[0].reshape(1, A_WIDTH),
        conv_a_ln_g[0].reshape(1, A_WIDTH), conv_a_ln_b[0].reshape(1, A_WIDTH), conv_b_w[0],
        conv_w_out[0].astype(BF16), seg, layer=0)
    xs, ffn_w = ffn(xs, ffn_w, layer=0, sub=1)

    xs, ffn_w = ffn(xs, ffn_w, layer=1, sub=0)
    w_in = pm_w_in[0].astype(BF16)
    w_u = w_in[:, 0:C_WIDTH]
    w_q = w_in[:, C_WIDTH:C_WIDTH + D_WIDTH]
    w_kvt = w_in[:, C_WIDTH + D_WIDTH:C_WIDTH + 3 * D_WIDTH].T
    pooled, qrows, kv, kmean = _pm_proj_call(
        xs, mod, mix_g, w_u, w_q, w_kvt, pool_w[0].astype(BF16),
        pool_b[0].reshape(1, C_WIDTH), pool_scale[0].reshape(1, C_WIDTH), layer=1)
    kmean_h = _head_pad(kmean, 1).reshape(N_BLOCKS, D_HEADS, HEAD_PAD).transpose(1, 0, 2)

    n_pair = MOBA_TOPK * MOBA_BLOCK
    tri = jnp.asarray(np.arange(n_pair)[:, None] < np.arange(n_pair)[None, :], dtype=BF16)
    q_flat = qrows.reshape(D_HEADS * SEQ, HEAD_PAD)
    own_parts, got_parts = [], []
    anchor = jnp.zeros((1,), I32)
    for split in range(N_SPLITS):
        sel, rank, cnt, own_part = _gate_own_call(anchor, qrows, kv, kmean_h, tri, split=split)
        base_digits, tile_h, tile_j, n_tiles, tile_step = _route_tables(cnt, split)
        pos = _route_pos_call(sel, rank, base_digits)
        anchor = pos[0, 0, 0:1]
        pos_by_slot = [pos[s].reshape(1, SPLIT_HEADS * SEQ) for s in range(MOBA_TOPK)]
        routed_q = _dispatch_call(q_flat, pos_by_slot, split=split)
        routed_parts = _routed_attn_call(tile_h, tile_j, n_tiles, tile_step, routed_q, kv,
                                         split=split)
        got = _collect_call(routed_parts, pos.reshape(1, N_PAIRS))
        own_parts.append(own_part)
        got_parts.append(got.reshape(MOBA_TOPK, SPLIT_HEADS, SEQ, HEAD_PAD))

    slopes = jnp.asarray(2.0 ** (-8.0 * np.arange(1, D_HEADS + 1) / D_HEADS), dtype=F32)
    w_out = pm_w_out[0].astype(BF16)
    w_att = _head_pad(w_out[C_WIDTH:], 0).reshape(D_HEADS, HEAD_PAD, D_MODEL)
    xs = _pm_out_call(slopes, xs, mod, pooled, own_parts, got_parts, w_out[0:C_WIDTH], w_att,
                      layer=1)
    xs, _ = ffn(xs, ffn_w, layer=1, sub=1)
    return xs.reshape(1, SEQ, D_MODEL)
```

```python
import functools

import numpy as np
import jax
import jax.numpy as jnp
from jax import lax
from jax.experimental import pallas as pl
from jax.experimental.pallas import tpu as pltpu
from jax.experimental.pallas import tpu_sc as plsc

D_MODEL = 1024
SEQ = 16384
DEPTH = 2
N_SUBLAYERS = 3
D_FF = 2816
EPS = 1e-6

A_WIDTH = 512
A_GROUPS = 8
A_CONV = 31
B_WIDTH = 512
B_CONV = 3

POOL_WINDOWS = (2, 4, 8, 16)
C_WIDTH = 512
C_GROUP_DIM = 128
D_HEADS = 8
HEAD_DIM = 64
D_WIDTH = 512
MOBA_BLOCK = 256
MOBA_TOPK = 3
N_BLOCKS = SEQ // MOBA_BLOCK

V7X_SUBLANES = 8
V7X_BF16_SUBLANES = 16
V7X_VMEM_LIMIT_BYTES = 56 * 1024 * 1024

A_HALO = 32
B_HALO = 8
POOL_HALO = 16
NEG_BIG = -1e30

HEAD_PAD = 128
ALIBI_COL = HEAD_DIM
SLAB_K_ROWS = HEAD_DIM + V7X_BF16_SUBLANES
SLAB_ROWS = SLAB_K_ROWS + HEAD_DIM

SPLIT_HEADS = 4
N_SPLITS = D_HEADS // SPLIT_HEADS
ROUTE_TILE = 256
ROUTE_TILES_PER_STEP = 32
GATE_BLOCKS_PER_STEP = 4
N_PAIRS = SPLIT_HEADS * SEQ * MOBA_TOPK
N_GROUPS = SPLIT_HEADS * N_BLOCKS
MAX_TILES = N_PAIRS // ROUTE_TILE + N_GROUPS
TRASH_ROW0 = MAX_TILES * ROUTE_TILE
ROUTE_ROWS = (MAX_TILES + ROUTE_TILES_PER_STEP) * ROUTE_TILE
SC_WINDOW = 128
DIGIT = 128
assert MAX_TILES % ROUTE_TILES_PER_STEP == 0 and N_BLOCKS % GATE_BLOCKS_PER_STEP == 0
assert ROUTE_ROWS <= DIGIT ** 3 and MOBA_BLOCK <= ROUTE_TILE

BF16 = jnp.bfloat16
F32 = jnp.float32
I32 = jnp.int32


def _params(n_axes):
    return pltpu.CompilerParams(
        dimension_semantics=("arbitrary",) * n_axes,
        vmem_limit_bytes=V7X_VMEM_LIMIT_BYTES)


def _resident(block_shape, index_map):
    return pl.BlockSpec(block_shape, index_map, pipeline_mode=pl.Buffered(1))


def _full(shape):
    return _resident(shape, lambda *_: (0,) * len(shape))


def _sigmoid(v):
    return 1.0 / (1.0 + jnp.exp(-v))


def _norm_mod(x, g, mod):
    r = lax.rsqrt(jnp.mean(x * x, axis=-1, keepdims=True) + EPS)
    return (x * r) * (g * (1.0 + mod[1:2])) + mod[0:1]


def _split(v):
    hi = v.astype(BF16)
    return hi, (v - hi.astype(F32)).astype(BF16)


def _split_dot(v, w):
    hi, lo = _split(v)
    return (jnp.dot(hi, w, preferred_element_type=F32)
            + jnp.dot(lo, w, preferred_element_type=F32))


_NT = (((1,), (1,)), ((), ()))


def _mod_kernel(c_ref, w_ref, b_ref, o_ref):
    c = c_ref[...]
    cond = c * _sigmoid(c)
    o_ref[0] = jnp.sum(w_ref[0] * cond, axis=0, keepdims=True) + b_ref[0]


def _mod_call(c, ada_w, ada_b):
    n_out = N_SUBLAYERS * 3 * D_MODEL
    tn = D_MODEL
    out = pl.pallas_call(
        _mod_kernel,
        grid=(DEPTH, n_out // tn),
        in_specs=[
            pl.BlockSpec((D_MODEL, 1), lambda l, j: (0, 0)),
            pl.BlockSpec((1, D_MODEL, tn), lambda l, j: (l, 0, j)),
            pl.BlockSpec((1, 1, tn), lambda l, j: (l, 0, j)),
        ],
        out_specs=pl.BlockSpec((1, 1, tn), lambda l, j: (l, 0, j)),
        out_shape=jax.ShapeDtypeStruct((DEPTH, 1, n_out), F32),
        compiler_params=_params(2),
        name="adaln_mod",
    )(c.reshape(D_MODEL, 1), ada_w, ada_b.reshape(DEPTH, 1, n_out))
    return out.reshape(DEPTH, N_SUBLAYERS, 3, D_MODEL)


def _ffn_kernel(x_ref, mod_ref, g_ref, wg_ref, wu_ref, wd_ref, fin_ref, *refs, final):
    o_ref = refs[len(refs) // 2]
    for src_ref, dst_ref in zip(refs[:len(refs) // 2], refs[len(refs) // 2 + 1:]):
        dst_ref[...] = src_ref[...].astype(BF16)
    x = x_ref[...]
    mod = mod_ref[...]
    h = _norm_mod(x, g_ref[...], mod).astype(BF16)
    gate = jnp.dot(h, wg_ref[...], preferred_element_type=F32)
    up = jnp.dot(h, wu_ref[...], preferred_element_type=F32)
    act = (gate * _sigmoid(gate) * up).astype(BF16)
    y = jnp.dot(act, wd_ref[...], preferred_element_type=F32)
    xn = x + (0.5 * mod[2:3]) * y
    if final:
        r = lax.rsqrt(jnp.mean(xn * xn, axis=-1, keepdims=True) + EPS)
        xn = xn * r * fin_ref[...]
    o_ref[...] = xn


def _ffn_call(x, mod, norm_g, weights, fin, next_weights, *, layer, sub, final, tm=512):
    mod_sub = 0 if sub == 0 else 2
    steps = SEQ // tm
    in_specs = [
        pl.BlockSpec((tm, D_MODEL), lambda i: (i, 0)),
        _resident((None, None, 3, D_MODEL), lambda i: (layer, mod_sub, 0, 0)),
        _resident((None, None, 1, D_MODEL), lambda i: (layer, sub, 0, 0)),
        _full((D_MODEL, D_FF)),
        _full((D_MODEL, D_FF)),
        _full((D_FF, D_MODEL)),
        _full((1, D_MODEL)),
    ]
    out_specs = [pl.BlockSpec((tm, D_MODEL), lambda i: (i, 0))]
    out_shape = [jax.ShapeDtypeStruct((SEQ, D_MODEL), F32)]
    operands = [x, mod, norm_g, *weights, fin]
    if next_weights is not None:
        (nl, ns), *stacks = next_weights
        for w in stacks:
            n_rows, n_cols = w.shape[2:]
            share = 1 if n_rows % (steps * V7X_BF16_SUBLANES) == 0 else 2
            rows = n_rows * share // steps
            assert n_rows * share % steps == 0 and rows % V7X_BF16_SUBLANES == 0
            in_specs.append(pl.BlockSpec((None, None, rows, n_cols),
                                         lambda i, share=share: (nl, ns, i // share, 0)))
            out_specs.append(pl.BlockSpec((rows, n_cols), lambda i, share=share: (i // share, 0)))
            out_shape.append(jax.ShapeDtypeStruct((n_rows, n_cols), BF16))
            operands.append(w)
    out = pl.pallas_call(
        functools.partial(_ffn_kernel, final=final),
        grid=(steps,),
        in_specs=in_specs,
        out_specs=out_specs,
        out_shape=out_shape,
        compiler_params=_params(1),
        name=f"ffn_l{layer}s{sub}",
    )(*operands)
    return out[0], tuple(out[1:])


def _conv_mix_kernel(x_ref, mod_ref, g_ref, win_ref, caw_ref, cab_ref, lng_ref, lnb_ref,
                     cbw_ref, wout_ref, seg_ref, o_ref, abuf, bbuf, *, tm):
    i = pl.program_id(0)

    @pl.when(i == 0)
    def _():
        abuf[0:A_HALO, :] = jnp.zeros((A_HALO, A_WIDTH), F32)
        bbuf[0:B_HALO, :] = jnp.zeros((B_HALO, B_WIDTH), F32)

    x = x_ref[...]
    mod = mod_ref[...]
    h = _norm_mod(x, g_ref[...], mod).astype(BF16)
    z = jnp.dot(h, win_ref[...], preferred_element_type=F32)

    a = z[:, 0:A_WIDTH] * _sigmoid(z[:, A_WIDTH:2 * A_WIDTH])
    abuf[A_HALO:A_HALO + tm, :] = a
    acc = jnp.zeros((tm, A_WIDTH), F32) + cab_ref[...]
    ext = tm + V7X_SUBLANES
    for r in range(V7X_SUBLANES):
        part = None
        for q in range((A_CONV - 1 - r) // V7X_SUBLANES + 1):
            k = A_CONV - 1 - (V7X_SUBLANES * q + r)
            term = caw_ref[k:k + 1, :] * abuf[pl.ds(A_HALO - V7X_SUBLANES * (q + 1), ext), :]
            part = term if part is None else part + term
        if r > 0:
            part = pltpu.roll(part, r, axis=0)
        acc = acc + part[V7X_SUBLANES:, :]
    abuf[0:A_HALO, :] = abuf[tm:tm + A_HALO, :]
    seg = seg_ref[...]
    mu = _split_dot(acc, seg)
    d = acc - mu
    var = _split_dot(d * d, seg)
    yn = d * lax.rsqrt(var + EPS) * lng_ref[...] + lnb_ref[...]
    a_out = yn * _sigmoid(yn)

    off = 2 * A_WIDTH
    cv = z[:, off + B_WIDTH:off + 2 * B_WIDTH] * z[:, off + 2 * B_WIDTH:off + 3 * B_WIDTH]
    bbuf[B_HALO:B_HALO + tm, :] = cv
    bacc = jnp.zeros((tm, B_WIDTH), F32)
    for k in range(B_CONV):
        bacc = bacc + cbw_ref[k:k + 1, :] * bbuf[pl.ds(B_HALO - (B_CONV - 1) + k, tm), :]
    bbuf[0:B_HALO, :] = bbuf[tm:tm + B_HALO, :]
    bb = z[:, off:off + B_WIDTH] * bacc

    y = (jnp.dot(a_out.astype(BF16), wout_ref[0:A_WIDTH, :], preferred_element_type=F32)
         + jnp.dot(bb.astype(BF16), wout_ref[A_WIDTH:A_WIDTH + B_WIDTH, :],
                   preferred_element_type=F32))
    o_ref[...] = x + mod[2:3] * y


def _conv_mix_call(x, mod, norm_g, w_in, caw, cab, lng, lnb, cbw, w_out, seg, *, layer, tm=512):
    even_in = w_in.shape[1]
    return pl.pallas_call(
        functools.partial(_conv_mix_kernel, tm=tm),
        grid=(SEQ // tm,),
        in_specs=[
            pl.BlockSpec((tm, D_MODEL), lambda i: (i, 0)),
            _resident((None, None, 3, D_MODEL), lambda i: (layer, 1, 0, 0)),
            _resident((None, 1, D_MODEL), lambda i: (layer, 0, 0)),
            _full((D_MODEL, even_in)),
            _full((A_CONV, A_WIDTH)),
            _full((1, A_WIDTH)),
            _full((1, A_WIDTH)),
            _full((1, A_WIDTH)),
            _full((B_CONV, B_WIDTH)),
            _full((A_WIDTH + B_WIDTH, D_MODEL)),
            _full((A_WIDTH, A_WIDTH)),
        ],
        out_specs=pl.BlockSpec((tm, D_MODEL), lambda i: (i, 0)),
        out_shape=jax.ShapeDtypeStruct((SEQ, D_MODEL), F32),
        scratch_shapes=[pltpu.VMEM((A_HALO + tm, A_WIDTH), F32),
                        pltpu.VMEM((B_HALO + tm, B_WIDTH), F32)],
        compiler_params=_params(1),
        name="conv_mixers",
    )(x, mod, norm_g, w_in, caw, cab, lng, lnb, cbw, w_out, seg)


def _pm_proj_kernel(x_ref, mod_ref, g_ref, wu_ref, wq_ref, wkvt_ref, pw_ref,
                    pb_ref, ps_ref, pooled_ref, qrow_ref, kv_ref, km_ref,
                    ubuf, hmean, *, tm):
    i = pl.program_id(0)

    @pl.when(i == 0)
    def _():
        ubuf[0:POOL_HALO, :] = jnp.zeros((POOL_HALO, C_WIDTH), F32)

    x = x_ref[...]
    h = _norm_mod(x, g_ref[...], mod_ref[...]).astype(BF16)
    u = jnp.dot(h, wu_ref[...], preferred_element_type=F32)
    qr = jnp.dot(h, wq_ref[...], preferred_element_type=F32)
    kvt = lax.dot_general(wkvt_ref[...], h, _NT, preferred_element_type=F32)

    ubuf[POOL_HALO:POOL_HALO + tm, :] = u
    t1 = (i * tm + 1 + lax.broadcasted_iota(I32, (tm, C_GROUP_DIM), 0)).astype(F32)
    for gi, w in enumerate(POOL_WINDOWS):
        c0 = gi * C_GROUP_DIM
        s = u[:, c0:c0 + C_GROUP_DIM]
        for k in range(1, w):
            s = s + ubuf[pl.ds(POOL_HALO - k, tm), c0:c0 + C_GROUP_DIM]
        pooled = s / jnp.minimum(t1, float(w)) - u[:, c0:c0 + C_GROUP_DIM]
        mixed = (jnp.dot(pooled.astype(BF16), pw_ref[gi], preferred_element_type=F32)
                 + pb_ref[:, c0:c0 + C_GROUP_DIM])
        pooled_ref[:, c0:c0 + C_GROUP_DIM] = (
            mixed * ps_ref[:, c0:c0 + C_GROUP_DIM]).astype(BF16)
    ubuf[0:POOL_HALO, :] = ubuf[tm:tm + POOL_HALO, :]

    lane = lax.broadcasted_iota(I32, (tm, HEAD_PAD), 1)
    tail = jnp.where(lane == ALIBI_COL, 1.0, 0.0)
    qs = qr * (HEAD_DIM ** -0.5)
    for pair in range(D_HEADS // 2):
        both = qs[:, pair * HEAD_PAD:(pair + 1) * HEAD_PAD]
        qrow_ref[2 * pair] = jnp.where(lane < HEAD_DIM, both, tail)
        qrow_ref[2 * pair + 1] = jnp.where(
            lane < HEAD_DIM, pltpu.roll(both, HEAD_DIM, axis=1), tail)
    per_tile = tm // MOBA_BLOCK
    kvb = kvt.astype(BF16)
    n_alibi = SLAB_K_ROWS - HEAD_DIM
    first = lax.broadcasted_iota(I32, (n_alibi, MOBA_BLOCK), 0) == 0
    koff = lax.broadcasted_iota(I32, (n_alibi, MOBA_BLOCK), 1).astype(F32)
    for hh in range(D_HEADS):
        alibi = jnp.where(first, koff * (2.0 ** -(hh + 1)), 0.0).astype(BF16)
        for sb in range(per_tile):
            keys = slice(sb * MOBA_BLOCK, (sb + 1) * MOBA_BLOCK)
            kv_ref[hh, sb, 0:HEAD_DIM, :] = kvb[hh * HEAD_DIM:(hh + 1) * HEAD_DIM, keys]
            kv_ref[hh, sb, HEAD_DIM:SLAB_K_ROWS, :] = alibi
            kv_ref[hh, sb, SLAB_K_ROWS:SLAB_ROWS, :] = kvb[
                D_WIDTH + hh * HEAD_DIM:D_WIDTH + (hh + 1) * HEAD_DIM, keys]

    for sb in range(per_tile):
        blk = h[sb * MOBA_BLOCK:(sb + 1) * MOBA_BLOCK, :].astype(F32)
        hmean[pl.ds(i * per_tile + sb, 1), :] = jnp.mean(blk, axis=0, keepdims=True)

    @pl.when(i == pl.num_programs(0) - 1)
    def _():
        hi, lo = _split(hmean[...])
        wkt = wkvt_ref[0:D_WIDTH, :]
        km_ref[...] = (lax.dot_general(hi, wkt, _NT, preferred_element_type=F32)
                       + lax.dot_general(lo, wkt, _NT, preferred_element_type=F32))


def _pm_proj_call(x, mod, norm_g, wu, wq, wkvt, pw, pb, ps, *, layer, tm=512):
    return pl.pallas_call(
        functools.partial(_pm_proj_kernel, tm=tm),
        grid=(SEQ // tm,),
        in_specs=[
            pl.BlockSpec((tm, D_MODEL), lambda i: (i, 0)),
            _resident((None, None, 3, D_MODEL), lambda i: (layer, 1, 0, 0)),
            _resident((None, 1, D_MODEL), lambda i: (layer, 0, 0)),
            _full((D_MODEL, C_WIDTH)),
            _full((D_MODEL, D_WIDTH)),
            _full((2 * D_WIDTH, D_MODEL)),
            _full((len(POOL_WINDOWS), C_GROUP_DIM, C_GROUP_DIM)),
            _full((1, C_WIDTH)),
            _full((1, C_WIDTH)),
        ],
        out_specs=[
            pl.BlockSpec((tm, C_WIDTH), lambda i: (i, 0)),
            pl.BlockSpec((D_HEADS, tm, HEAD_PAD), lambda i: (0, i, 0)),
            pl.BlockSpec((D_HEADS, tm // MOBA_BLOCK, SLAB_ROWS, MOBA_BLOCK),
                         lambda i: (0, i, 0, 0)),
            pl.BlockSpec((N_BLOCKS, D_WIDTH), lambda i: (0, 0)),
        ],
        out_shape=[
            jax.ShapeDtypeStruct((SEQ, C_WIDTH), BF16),
            jax.ShapeDtypeStruct((D_HEADS, SEQ, HEAD_PAD), F32),
            jax.ShapeDtypeStruct((D_HEADS, N_BLOCKS, SLAB_ROWS, MOBA_BLOCK), BF16),
            jax.ShapeDtypeStruct((N_BLOCKS, D_WIDTH), F32),
        ],
        scratch_shapes=[pltpu.VMEM((POOL_HALO + tm, C_WIDTH), F32),
                        pltpu.VMEM((N_BLOCKS, D_MODEL), F32)],
        compiler_params=_params(1),
        name="pool_qkv_proj",
    )(x, mod, norm_g, wu, wq, wkvt, pw, pb, ps)


def _block_rows(qs, slabs, mask, lse_add):
    n = qs[0].shape[0]
    kpad = jnp.zeros((HEAD_PAD - SLAB_K_ROWS, MOBA_BLOCK), BF16)
    s = jnp.concatenate(
        [jnp.dot(q, jnp.concatenate([slab[0:SLAB_K_ROWS, :], kpad], axis=0),
                 preferred_element_type=F32)
         for q, slab in zip(qs, slabs)], axis=0)
    if mask is not None:
        s = jnp.where(mask, s, NEG_BIG)
    m = jnp.max(s, axis=1, keepdims=True)
    p = jnp.exp(s - m).astype(BF16)
    vpad = jnp.zeros((HEAD_PAD - HEAD_DIM, MOBA_BLOCK), BF16)
    ones = jnp.ones((HEAD_PAD, MOBA_BLOCK), BF16)
    ot = jnp.concatenate(
        [lax.dot_general(p[u * n:(u + 1) * n],
                         jnp.concatenate([slab[SLAB_K_ROWS:SLAB_ROWS, :], vpad, ones], axis=0),
                         _NT, preferred_element_type=F32)
         for u, slab in enumerate(slabs)], axis=0)
    o, tot = ot[:, 0:HEAD_PAD], ot[:, HEAD_PAD:2 * HEAD_PAD]
    lse = m + jnp.log(tot)
    if lse_add is not None:
        lse = jnp.concatenate(
            [lse[u * n:(u + 1) * n] + add for u, add in enumerate(lse_add)], axis=0)
    low = lax.broadcasted_iota(I32, o.shape, 1) < HEAD_DIM
    return jnp.where(low, o / tot, lse)


def _onehot_pairs(sel_rows):
    blk = lax.broadcasted_iota(I32, (N_BLOCKS, MOBA_BLOCK), 0)
    return jnp.concatenate(
        [jnp.where(blk == sel_rows[s], 1.0, 0.0) for s in range(MOBA_TOPK)], axis=1)


def _gate_own_kernel(anchor_ref, qrow_ref, kv_ref, km_ref, tri_ref,
                     sel_ref, rank_ref, cnt_ref, own_ref, *, heads, blocks):
    del anchor_ref
    n_pair = MOBA_TOPK * MOBA_BLOCK
    tiles = [(bb, a) for bb in range(blocks) for a in range(heads)]
    rows_of = lambda bb: slice(bb * MOBA_BLOCK, (bb + 1) * MOBA_BLOCK)
    qs = [qrow_ref[a, rows_of(bb), :].astype(BF16) for bb, a in tiles]

    kms = [_split(km_ref[a]) for a in range(heads)]
    gates = [lax.dot_general(kms[a][0], q, _NT, preferred_element_type=F32)
             + lax.dot_general(kms[a][1], q, _NT, preferred_element_type=F32)
             for (bb, a), q in zip(tiles, qs)]
    width = len(tiles) * MOBA_BLOCK
    blk = lax.broadcasted_iota(I32, (N_BLOCKS, width), 0)
    lane = lax.broadcasted_iota(I32, (1, width), 1)
    own = pl.program_id(0) * blocks + lane // (heads * MOBA_BLOCK)
    g = jnp.where(blk < own, jnp.concatenate(gates, axis=1), -jnp.inf)
    slots = []
    for s in range(MOBA_TOPK):
        best = jnp.max(g, axis=0, keepdims=True)
        idx = jnp.min(jnp.where(g == best, blk, N_BLOCKS), axis=0, keepdims=True)
        g = jnp.where(blk == idx, -jnp.inf, g)
        slots.append(jnp.where(s < own, idx, N_BLOCKS))

    onehots = []
    for t, (bb, a) in enumerate(tiles):
        sel_rows = [slot[:, t * MOBA_BLOCK:(t + 1) * MOBA_BLOCK] for slot in slots]
        sel_ref[a, :, rows_of(bb)] = jnp.concatenate(sel_rows, axis=0)
        onehots.append(_onehot_pairs(sel_rows))
    onehot = jnp.concatenate(onehots, axis=0)
    oh = onehot.astype(BF16)
    before = jnp.dot(oh, tri_ref[...], preferred_element_type=F32)
    hit = onehot * before
    ones = jnp.ones((8, n_pair), BF16)
    cnt = lax.dot_general(ones, oh, _NT, preferred_element_type=F32)
    for t, (bb, a) in enumerate(tiles):
        rank = jnp.sum(hit[t * N_BLOCKS:(t + 1) * N_BLOCKS], axis=0, keepdims=True).astype(I32)
        rank_ref[a, :, rows_of(bb)] = jnp.concatenate(
            [rank[:, s * MOBA_BLOCK:(s + 1) * MOBA_BLOCK] for s in range(MOBA_TOPK)], axis=0)
        cnt_ref[a, bb] = cnt[0:1, t * N_BLOCKS:(t + 1) * N_BLOCKS].astype(I32)

    qi = lax.broadcasted_iota(I32, (width, MOBA_BLOCK), 0) % MOBA_BLOCK
    ki = lax.broadcasted_iota(I32, (width, MOBA_BLOCK), 1)
    rows = _block_rows(qs, [kv_ref.at[a, bb] for bb, a in tiles], ki <= qi, None)
    for t, (bb, a) in enumerate(tiles):
        own_ref[a, rows_of(bb), :] = rows[t * MOBA_BLOCK:(t + 1) * MOBA_BLOCK]


def _gate_own_call(anchor, qrows, kv, kmean_h, tri, *, split, blocks=GATE_BLOCKS_PER_STEP):
    heads = SPLIT_HEADS
    rows = blocks * MOBA_BLOCK
    return pl.pallas_call(
        functools.partial(_gate_own_kernel, heads=heads, blocks=blocks),
        grid=(N_BLOCKS // blocks,),
        in_specs=[
            pl.BlockSpec(memory_space=pltpu.SMEM),
            pl.BlockSpec((heads, rows, HEAD_PAD), lambda b: (split, b, 0)),
            pl.BlockSpec((heads, blocks, SLAB_ROWS, MOBA_BLOCK), lambda b: (split, b, 0, 0)),
            pl.BlockSpec((heads, N_BLOCKS, HEAD_PAD), lambda b: (split, 0, 0)),
            _full((MOBA_TOPK * MOBA_BLOCK, MOBA_TOPK * MOBA_BLOCK)),
        ],
        out_specs=[
            pl.BlockSpec((heads, MOBA_TOPK, rows), lambda b: (0, 0, b)),
            pl.BlockSpec((heads, MOBA_TOPK, rows), lambda b: (0, 0, b)),
            pl.BlockSpec((heads, blocks, 1, N_BLOCKS), lambda b: (0, b, 0, 0)),
            pl.BlockSpec((heads, rows, HEAD_PAD), lambda b: (0, b, 0)),
        ],
        out_shape=[
            jax.ShapeDtypeStruct((heads, MOBA_TOPK, SEQ), I32),
            jax.ShapeDtypeStruct((heads, MOBA_TOPK, SEQ), I32),
            jax.ShapeDtypeStruct((heads, N_BLOCKS, 1, N_BLOCKS), I32),
            jax.ShapeDtypeStruct((heads, SEQ, HEAD_PAD), F32),
        ],
        compiler_params=_params(1),
        name=f"moba_gate_own_{split}",
    )(anchor, qrows, kv, kmean_h, tri)


def _route_pos_kernel(sel_ref, rank_ref, base_ref, pos_ref, *, blocks):
    lane = lax.broadcasted_iota(I32, (1, MOBA_BLOCK), 1)
    for bb in range(blocks):
        qs = slice(bb * MOBA_BLOCK, (bb + 1) * MOBA_BLOCK)
        rows = [[] for _ in range(MOBA_TOPK)]
        for a in range(SPLIT_HEADS):
            sel = sel_ref[a, :, qs]
            onehot = _onehot_pairs([sel[s:s + 1, :] for s in range(MOBA_TOPK)]).astype(BF16)
            dig = jnp.dot(base_ref[a, bb].astype(BF16), onehot,
                          preferred_element_type=F32)
            base = (dig[0:1] * float(DIGIT * DIGIT) + dig[1:2] * float(DIGIT)
                    + dig[2:3]).astype(I32)
            for s in range(MOBA_TOPK):
                p = base[:, s * MOBA_BLOCK:(s + 1) * MOBA_BLOCK] + rank_ref[a, s:s + 1, qs]
                rows[s].append(
                    jnp.where(sel[s:s + 1, :] < N_BLOCKS, p, TRASH_ROW0 + lane % SC_WINDOW))
        for s in range(MOBA_TOPK):
            pos_ref[s, :, qs] = jnp.concatenate(rows[s], axis=0)


def _route_pos_call(sel, rank, base_digits, *, blocks=8):
    width = blocks * MOBA_BLOCK
    return pl.pallas_call(
        functools.partial(_route_pos_kernel, blocks=blocks),
        grid=(N_BLOCKS // blocks,),
        in_specs=[
            pl.BlockSpec((SPLIT_HEADS, MOBA_TOPK, width), lambda b: (0, 0, b)),
            pl.BlockSpec((SPLIT_HEADS, MOBA_TOPK, width), lambda b: (0, 0, b)),
            pl.BlockSpec((SPLIT_HEADS, blocks, 8, N_BLOCKS), lambda b: (0, b, 0, 0)),
        ],
        out_specs=pl.BlockSpec((MOBA_TOPK, SPLIT_HEADS, width), lambda b: (0, 0, b)),
        out_shape=jax.ShapeDtypeStruct((MOBA_TOPK, SPLIT_HEADS, SEQ), I32),
        compiler_params=_params(1),
        name="moba_route_pos",
    )(sel, rank, base_digits)


def _route_tables(cnt, split):
    cnt = cnt.reshape(SPLIT_HEADS, N_BLOCKS, N_BLOCKS)
    tiles = (cnt.sum(axis=1) + ROUTE_TILE - 1) // ROUTE_TILE
    tiles_flat = tiles.reshape(N_GROUPS)
    tile0 = jnp.cumsum(tiles_flat) - tiles_flat
    n_tiles = tiles_flat.sum().astype(I32)
    within = jnp.cumsum(cnt, axis=1) - cnt
    base = tile0.reshape(SPLIT_HEADS, 1, N_BLOCKS) * ROUTE_TILE + within
    digits = jnp.stack([base // (DIGIT * DIGIT), (base // DIGIT) % DIGIT, base % DIGIT],
                       axis=2).astype(F32)
    digits = jnp.pad(digits, ((0, 0), (0, 0), (0, 8 - 3), (0, 0)))
    t = jnp.arange(MAX_TILES, dtype=I32)
    ended = (tile0 + tiles_flat)[None, :] <= t[:, None]
    group_of_tile = jnp.minimum(ended.sum(axis=1).astype(I32), N_GROUPS - 1)
    tile_h = group_of_tile // N_BLOCKS
    head = split * SPLIT_HEADS + tile_h
    tile_step = lax.bitcast_convert_type((127 + 7 - head) << 23, F32)
    return digits, tile_h, group_of_tile % N_BLOCKS, n_tiles.reshape(1), tile_step


def _sc_mesh():
    return plsc.VectorSubcoreMesh(core_axis_name="core", subcore_axis_name="subcore")


def _dispatch_call(rows, pos_by_slot, *, split):
    n_rows = SPLIT_HEADS * SEQ
    first_window = split * n_rows // SC_WINDOW

    @functools.partial(
        pl.kernel, mesh=_sc_mesh(), scratch_types=[],
        out_type=jax.ShapeDtypeStruct((ROUTE_ROWS, HEAD_PAD), F32))
    def dispatch(x_hbm, i0_hbm, i1_hbm, i2_hbm, o_hbm):
        def body(x_vmem, i0_vmem, i1_vmem, i2_vmem):
            pltpu.sync_copy(x_vmem, o_hbm.at[i0_vmem.at[0]])
            pltpu.sync_copy(x_vmem, o_hbm.at[i1_vmem.at[0]])
            pltpu.sync_copy(x_vmem, o_hbm.at[i2_vmem.at[0]])

        idx_spec = pl.BlockSpec((1, SC_WINDOW), lambda i: (0, i))
        pltpu.emit_pipeline(
            body,
            grid=(n_rows // SC_WINDOW,),
            in_specs=[pl.BlockSpec((SC_WINDOW, HEAD_PAD), lambda i: (first_window + i, 0)),
                      idx_spec, idx_spec, idx_spec],
            out_specs=[],
            core_axis_name=("core", "subcore"),
            dimension_semantics=(pltpu.PARALLEL,),
        )(x_hbm, i0_hbm, i1_hbm, i2_hbm)

    return dispatch(rows, *pos_by_slot)


def _collect_call(table, idx):
    n_rows = idx.shape[1]

    @functools.partial(
        pl.kernel, mesh=_sc_mesh(), scratch_types=[],
        out_type=jax.ShapeDtypeStruct((n_rows, HEAD_PAD), F32))
    def collect(x_hbm, i_hbm, o_hbm):
        def body(i_vmem, o_vmem):
            pltpu.sync_copy(x_hbm.at[i_vmem.at[0]], o_vmem)

        pltpu.emit_pipeline(
            body,
            grid=(n_rows // SC_WINDOW,),
            in_specs=[pl.BlockSpec((1, SC_WINDOW), lambda i: (0, i))],
            out_specs=[pl.BlockSpec((SC_WINDOW, HEAD_PAD), lambda i: (i, 0))],
            core_axis_name=("core", "subcore"),
            dimension_semantics=(pltpu.PARALLEL,),
        )(i_hbm, o_hbm)

    return collect(table, idx)


def _routed_attn_kernel(th_ref, tj_ref, nt_ref, step_ref, q_ref, *refs, tiles):
    kv_refs, o_ref = refs[0:tiles], refs[tiles]
    t0 = pl.program_id(0) * tiles

    @pl.when(t0 < nt_ref[0])
    def _():
        qs = [q_ref[u * ROUTE_TILE:(u + 1) * ROUTE_TILE, :].astype(BF16) for u in range(tiles)]
        adds = [step_ref[t0 + u] * tj_ref[t0 + u].astype(F32) for u in range(tiles)]
        o_ref[...] = _block_rows(qs, kv_refs, None, adds)

    @pl.when(t0 == MAX_TILES)
    def _():
        low = lax.broadcasted_iota(I32, o_ref.shape, 1) < HEAD_DIM
        o_ref[...] = jnp.where(low, 0.0, NEG_BIG)


def _routed_attn_call(tile_h, tile_j, n_tiles, tile_step, routed_q, kv,
                      *, split, tiles=ROUTE_TILES_PER_STEP):
    n_steps = MAX_TILES // tiles + 1
    assert TRASH_ROW0 == (n_steps - 1) * tiles * ROUTE_TILE
    live = lambda s, th, tj, nt: (jnp.minimum(s, (nt[0] + tiles - 1) // tiles), 0)
    live_or_last = lambda s, th, tj, nt: (
        jnp.where(s == n_steps - 1, s, jnp.minimum(s, (nt[0] + tiles - 1) // tiles)), 0)
    head0 = split * SPLIT_HEADS

    def kv_spec(u):
        def slab(s, th, tj, nt):
            t = jnp.minimum(s, n_steps - 2) * tiles + u
            return head0 + th[t], tj[t], 0, 0
        return pl.BlockSpec((None, None, SLAB_ROWS, MOBA_BLOCK), slab)

    grid_spec = pltpu.PrefetchScalarGridSpec(
        num_scalar_prefetch=3,
        grid=(n_steps,),
        in_specs=([pl.BlockSpec(memory_space=pltpu.SMEM),
                   pl.BlockSpec((tiles * ROUTE_TILE, HEAD_PAD), live)]
                  + [kv_spec(u) for u in range(tiles)]),
        out_specs=pl.BlockSpec((tiles * ROUTE_TILE, HEAD_PAD), live_or_last),
    )
    return pl.pallas_call(
        functools.partial(_routed_attn_kernel, tiles=tiles),
        grid_spec=grid_spec,
        out_shape=jax.ShapeDtypeStruct((ROUTE_ROWS, HEAD_PAD), F32),
        compiler_params=_params(1),
        name=f"moba_routed_attn_{split}",
    )(tile_h, tile_j, n_tiles, tile_step, routed_q, *([kv] * tiles))


def _pm_out_kernel(slope_ref, x_ref, mod_ref, pooled_ref, *refs, blocks):
    own_refs, got_refs = refs[0:N_SPLITS], refs[N_SPLITS:2 * N_SPLITS]
    wp_ref, wa_ref, o_ref = refs[2 * N_SPLITS:]
    low = lax.broadcasted_iota(I32, (MOBA_BLOCK, HEAD_PAD), 1) < HEAD_DIM

    def lse_of(part):
        return jnp.where(low, pltpu.roll(part, HEAD_DIM, axis=1), part)

    y = jnp.dot(pooled_ref[...], wp_ref[...], preferred_element_type=F32)
    for hh in range(D_HEADS):
        own_ref, got_ref = own_refs[hh // SPLIT_HEADS], got_refs[hh // SPLIT_HEADS]
        a = hh % SPLIT_HEADS
        atts = []
        for bb in range(blocks):
            own = pl.program_id(0) * blocks + bb
            rows = slice(bb * MOBA_BLOCK, (bb + 1) * MOBA_BLOCK)
            own_shift = slope_ref[hh] * (own * MOBA_BLOCK).astype(F32)
            parts = [own_ref[a, rows, :]] + [got_ref[s, a, rows, :] for s in range(MOBA_TOPK)]
            lses = [lse_of(parts[0]) + own_shift] + [lse_of(part) for part in parts[1:]]
            top = functools.reduce(jnp.maximum, lses)
            num = jnp.zeros((MOBA_BLOCK, HEAD_PAD), F32)
            den = jnp.zeros((MOBA_BLOCK, HEAD_PAD), F32)
            for part, ls in zip(parts, lses):
                w = jnp.exp(ls - top)
                num = num + w * part
                den = den + w
            atts.append((num / den).astype(BF16))
        y = y + jnp.dot(jnp.concatenate(atts, axis=0), wa_ref[hh], preferred_element_type=F32)
    o_ref[...] = x_ref[...] + mod_ref[2:3, :] * y


def _pm_out_call(slopes, x, mod, pooled, own_parts, got_parts, wp, wa, *, layer, blocks=2):
    tm = blocks * MOBA_BLOCK
    return pl.pallas_call(
        functools.partial(_pm_out_kernel, blocks=blocks),
        grid=(SEQ // tm,),
        in_specs=[
            pl.BlockSpec(memory_space=pltpu.SMEM),
            pl.BlockSpec((tm, D_MODEL), lambda i: (i, 0)),
            _resident((None, None, 3, D_MODEL), lambda i: (layer, 1, 0, 0)),
            pl.BlockSpec((tm, C_WIDTH), lambda i: (i, 0)),
        ] + [pl.BlockSpec((SPLIT_HEADS, tm, HEAD_PAD), lambda i: (0, i, 0))] * N_SPLITS
        + [pl.BlockSpec((MOBA_TOPK, SPLIT_HEADS, tm, HEAD_PAD), lambda i: (0, 0, i, 0))] * N_SPLITS
        + [
            _full((C_WIDTH, D_MODEL)),
            _full((D_HEADS, HEAD_PAD, D_MODEL)),
        ],
        out_specs=pl.BlockSpec((tm, D_MODEL), lambda i: (i, 0)),
        out_shape=jax.ShapeDtypeStruct((SEQ, D_MODEL), F32),
        compiler_params=_params(1),
        name="pool_moba_out",
    )(slopes, x, mod, pooled, *own_parts, *got_parts, wp, wa)


def _head_pad(w, axis):
    shape = list(w.shape)
    shape[axis:axis + 1] = [D_HEADS, HEAD_DIM]
    pad = [(0, 0)] * len(shape)
    pad[axis + 1] = (0, HEAD_PAD - HEAD_DIM)
    out = jnp.pad(w.reshape(shape), pad)
    shape[axis:axis + 2] = [D_HEADS * HEAD_PAD]
    return out.reshape(shape)


def kernel(x, c, ada_w, ada_b, ffn_norm, ffn_w_gate, ffn_w_up, ffn_w_down, mix_norm, conv_w_in,
           conv_a_w, conv_a_b, conv_a_ln_g, conv_a_ln_b, conv_b_w, conv_w_out, pm_w_in, pool_w,
           pool_b, pool_scale, pm_w_out, final_norm):
    assert x.shape == (1, SEQ, D_MODEL) and c.shape == (1, D_MODEL)
    xs = x.reshape(SEQ, D_MODEL)
    mod = _mod_call(c, ada_w, ada_b)

    ffn_g = ffn_norm.reshape(DEPTH, 2, 1, D_MODEL)
    mix_g = mix_norm.reshape(DEPTH, 1, D_MODEL)
    fin = final_norm.reshape(1, D_MODEL)
    ffn_stacks = (ffn_w_gate, ffn_w_up, ffn_w_down)

    def ffn(xs, weights, layer, sub):
        last = (layer, sub) == (DEPTH - 1, 1)
        after = None if last else ((layer + sub, 1 - sub), *ffn_stacks)
        return _ffn_call(xs, mod, ffn_g, weights, fin, after, layer=layer, sub=sub, final=last)

    xs, ffn_w = ffn(xs, tuple(w[0, 0].astype(BF16) for w in ffn_stacks), layer=0, sub=0)
    grp = np.arange(A_WIDTH) // (A_WIDTH // A_GROUPS)
    seg = jnp.asarray((grp[:, None] == grp[None, :]) / (A_WIDTH // A_GROUPS), dtype=BF16)
    xs = _conv_mix_call(
        xs, mod, mix_g, conv_w_in[0].astype(BF16), conv_a_w[0], conv_a_b[0].reshape(1, A_WIDTH),
        conv_a_ln_g[0].reshape(1, A_WIDTH), conv_a_ln_b[0].reshape(1, A_WIDTH), conv_b_w[0],
        conv_w_out[0].astype(BF16), seg, layer=0)
    xs, ffn_w = ffn(xs, ffn_w, layer=0, sub=1)

    xs, ffn_w = ffn(xs, ffn_w, layer=1, sub=0)
    w_in = pm_w_in[0].astype(BF16)
    w_u = w_in[:, 0:C_WIDTH]
    w_q = w_in[:, C_WIDTH:C_WIDTH + D_WIDTH]
    w_kvt = w_in[:, C_WIDTH + D_WIDTH:C_WIDTH + 3 * D_WIDTH].T
    pooled, qrows, kv, kmean = _pm_proj_call(
        xs, mod, mix_g, w_u, w_q, w_kvt, pool_w[0].astype(BF16),
        pool_b[0].reshape(1, C_WIDTH), pool_scale[0].reshape(1, C_WIDTH), layer=1)
    kmean_h = _head_pad(kmean, 1).reshape(N_BLOCKS, D_HEADS, HEAD_PAD).transpose(1, 0, 2)

    n_pair = MOBA_TOPK * MOBA_BLOCK
    tri = jnp.asarray(np.arange(n_pair)[:, None] < np.arange(n_pair)[None, :], dtype=BF16)
    q_flat = qrows.reshape(D_HEADS * SEQ, HEAD_PAD)
    own_parts, got_parts = [], []
    anchor = jnp.zeros((1,), I32)
    for split in range(N_SPLITS):
        sel, rank, cnt, own_part = _gate_own_call(anchor, qrows, kv, kmean_h, tri, split=split)
        base_digits, tile_h, tile_j, n_tiles, tile_step = _route_tables(cnt, split)
        pos = _route_pos_call(sel, rank, base_digits)
        anchor = pos[0, 0, 0:1]
        pos_by_slot = [pos[s].reshape(1, SPLIT_HEADS * SEQ) for s in range(MOBA_TOPK)]
        routed_q = _dispatch_call(q_flat, pos_by_slot, split=split)
        routed_parts = _routed_attn_call(tile_h, tile_j, n_tiles, tile_step, routed_q, kv,
                                         split=split)
        got = _collect_call(routed_parts, pos.reshape(1, N_PAIRS))
        own_parts.append(own_part)
        got_parts.append(got.reshape(MOBA_TOPK, SPLIT_HEADS, SEQ, HEAD_PAD))

    slopes = jnp.asarray(2.0 ** (-8.0 * np.arange(1, D_HEADS + 1) / D_HEADS), dtype=F32)
    w_out = pm_w_out[0].astype(BF16)
    w_att = _head_pad(w_out[C_WIDTH:], 0).reshape(D_HEADS, HEAD_PAD, D_MODEL)
    xs = _pm_out_call(slopes, xs, mod, pooled, own_parts, got_parts, w_out[0:C_WIDTH], w_att,
                      layer=1)
    xs, _ = ffn(xs, ffn_w, layer=1, sub=1)
    return xs.reshape(1, SEQ, D_MODEL)
```

```python
import functools

import numpy as np
import jax
import jax.numpy as jnp
from jax import lax
from jax.experimental import pallas as pl
from jax.experimental.pallas import tpu as pltpu
from jax.experimental.pallas import tpu_sc as plsc

D_MODEL = 1024
SEQ = 16384
DEPTH = 2
N_SUBLAYERS = 3
D_FF = 2816
EPS = 1e-6

A_WIDTH = 512
A_GROUPS = 8
A_CONV = 31
B_WIDTH = 512
B_CONV = 3

POOL_WINDOWS = (2, 4, 8, 16)
C_WIDTH = 512
C_GROUP_DIM = 128
D_HEADS = 8
HEAD_DIM = 64
D_WIDTH = 512
MOBA_BLOCK = 256
MOBA_TOPK = 3
N_BLOCKS = SEQ // MOBA_BLOCK

V7X_SUBLANES = 8
V7X_BF16_SUBLANES = 16
V7X_VMEM_LIMIT_BYTES = 56 * 1024 * 1024

A_HALO = 32
B_HALO = 8
POOL_HALO = 16
NEG_BIG = -1e30

HEAD_PAD = 128
ALIBI_COL = HEAD_DIM
SLAB_K_ROWS = HEAD_DIM + V7X_BF16_SUBLANES
SLAB_ROWS = SLAB_K_ROWS + HEAD_DIM

SPLIT_HEADS = 4
N_SPLITS = D_HEADS // SPLIT_HEADS
ROUTE_TILE = 256
ROUTE_TILES_PER_STEP = 32
GATE_BLOCKS_PER_STEP = 4
N_PAIRS = SPLIT_HEADS * SEQ * MOBA_TOPK
N_GROUPS = SPLIT_HEADS * N_BLOCKS
MAX_TILES = N_PAIRS // ROUTE_TILE + N_GROUPS
TRASH_ROW0 = MAX_TILES * ROUTE_TILE
ROUTE_ROWS = (MAX_TILES + ROUTE_TILES_PER_STEP) * ROUTE_TILE
SC_WINDOW = 128
DIGIT = 128
assert MAX_TILES % ROUTE_TILES_PER_STEP == 0 and N_BLOCKS % GATE_BLOCKS_PER_STEP == 0
assert ROUTE_ROWS <= DIGIT ** 3 and MOBA_BLOCK <= ROUTE_TILE

BF16 = jnp.bfloat16
F32 = jnp.float32
I32 = jnp.int32


def _params(n_axes):
    return pltpu.CompilerParams(
        dimension_semantics=("arbitrary",) * n_axes,
        vmem_limit_bytes=V7X_VMEM_LIMIT_BYTES)


def _resident(block_shape, index_map):
    return pl.BlockSpec(block_shape, index_map, pipeline_mode=pl.Buffered(1))


def _full(shape):
    return _resident(shape, lambda *_: (0,) * len(shape))


def _sigmoid(v):
    return 1.0 / (1.0 + jnp.exp(-v))


def _norm_mod(x, g, mod):
    r = lax.rsqrt(jnp.mean(x * x, axis=-1, keepdims=True) + EPS)
    return (x * r) * (g * (1.0 + mod[1:2])) + mod[0:1]


def _split(v):
    hi = v.astype(BF16)
    return hi, (v - hi.astype(F32)).astype(BF16)


def _split_dot(v, w):
    hi, lo = _split(v)
    return (jnp.dot(hi, w, preferred_element_type=F32)
            + jnp.dot(lo, w, preferred_element_type=F32))


_NT = (((1,), (1,)), ((), ()))


def _mod_kernel(c_ref, w_ref, b_ref, o_ref):
    c = c_ref[...]
    cond = c * _sigmoid(c)
    o_ref[0] = jnp.sum(w_ref[0] * cond, axis=0, keepdims=True) + b_ref[0]


def _mod_call(c, ada_w, ada_b):
    n_out = N_SUBLAYERS * 3 * D_MODEL
    tn = D_MODEL
    out = pl.pallas_call(
        _mod_kernel,
        grid=(DEPTH, n_out // tn),
        in_specs=[
            pl.BlockSpec((D_MODEL, 1), lambda l, j: (0, 0)),
            pl.BlockSpec((1, D_MODEL, tn), lambda l, j: (l, 0, j)),
            pl.BlockSpec((1, 1, tn), lambda l, j: (l, 0, j)),
        ],
        out_specs=pl.BlockSpec((1, 1, tn), lambda l, j: (l, 0, j)),
        out_shape=jax.ShapeDtypeStruct((DEPTH, 1, n_out), F32),
        compiler_params=_params(2),
        name="adaln_mod",
    )(c.reshape(D_MODEL, 1), ada_w, ada_b.reshape(DEPTH, 1, n_out))
    return out.reshape(DEPTH, N_SUBLAYERS, 3, D_MODEL)


def _ffn_kernel(x_ref, mod_ref, g_ref, wg_ref, wu_ref, wd_ref, fin_ref, *refs, final):
    o_ref = refs[len(refs) // 2]
    for src_ref, dst_ref in zip(refs[:len(refs) // 2], refs[len(refs) // 2 + 1:]):
        dst_ref[...] = src_ref[...].astype(BF16)
    x = x_ref[...]
    mod = mod_ref[...]
    h = _norm_mod(x, g_ref[...], mod).astype(BF16)
    gate = jnp.dot(h, wg_ref[...], preferred_element_type=F32)
    up = jnp.dot(h, wu_ref[...], preferred_element_type=F32)
    act = (gate * _sigmoid(gate) * up).astype(BF16)
    y = jnp.dot(act, wd_ref[...], preferred_element_type=F32)
    xn = x + (0.5 * mod[2:3]) * y
    if final:
        r = lax.rsqrt(jnp.mean(xn * xn, axis=-1, keepdims=True) + EPS)
        xn = xn * r * fin_ref[...]
    o_ref[...] = xn


def _ffn_call(x, mod, norm_g, weights, fin, next_weights, *, layer, sub, final, tm=512):
    mod_sub = 0 if sub == 0 else 2
    steps = SEQ // tm
    in_specs = [
        pl.BlockSpec((tm, D_MODEL), lambda i: (i, 0)),
        _resident((None, None, 3, D_MODEL), lambda i: (layer, mod_sub, 0, 0)),
        _resident((None, None, 1, D_MODEL), lambda i: (layer, sub, 0, 0)),
        _full((D_MODEL, D_FF)),
        _full((D_MODEL, D_FF)),
        _full((D_FF, D_MODEL)),
        _full((1, D_MODEL)),
    ]
    out_specs = [pl.BlockSpec((tm, D_MODEL), lambda i: (i, 0))]
    out_shape = [jax.ShapeDtypeStruct((SEQ, D_MODEL), F32)]
    operands = [x, mod, norm_g, *weights, fin]
    if next_weights is not None:
        (nl, ns), *stacks = next_weights
        for w in stacks:
            n_rows, n_cols = w.shape[2:]
            share = 1 if n_rows % (steps * V7X_BF16_SUBLANES) == 0 else 2
            rows = n_rows * share // steps
            assert n_rows * share % steps == 0 and rows % V7X_BF16_SUBLANES == 0
            in_specs.append(pl.BlockSpec((None, None, rows, n_cols),
                                         lambda i, share=share: (nl, ns, i // share, 0)))
            out_specs.append(pl.BlockSpec((rows, n_cols), lambda i, share=share: (i // share, 0)))
            out_shape.append(jax.ShapeDtypeStruct((n_rows, n_cols), BF16))
            operands.append(w)
    out = pl.pallas_call(
        functools.partial(_ffn_kernel, final=final),
        grid=(steps,),
        in_specs=in_specs,
        out_specs=out_specs,
        out_shape=out_shape,
        compiler_params=_params(1),
        name=f"ffn_l{layer}s{sub}",
    )(*operands)
    return out[0], tuple(out[1:])


def _conv_mix_kernel(x_ref, mod_ref, g_ref, win_ref, caw_ref, cab_ref, lng_ref, lnb_ref,
                     cbw_ref, wout_ref, seg_ref, o_ref, abuf, bbuf, *, tm):
    i = pl.program_id(0)

    @pl.when(i == 0)
    def _():
        abuf[0:A_HALO, :] = jnp.zeros((A_HALO, A_WIDTH), F32)
        bbuf[0:B_HALO, :] = jnp.zeros((B_HALO, B_WIDTH), F32)

    x = x_ref[...]
    mod = mod_ref[...]
    h = _norm_mod(x, g_ref[...], mod).astype(BF16)
    z = jnp.dot(h, win_ref[...], preferred_element_type=F32)

    a = z[:, 0:A_WIDTH] * _sigmoid(z[:, A_WIDTH:2 * A_WIDTH])
    abuf[A_HALO:A_HALO + tm, :] = a
    acc = jnp.zeros((tm, A_WIDTH), F32) + cab_ref[...]
    ext = tm + V7X_SUBLANES
    for r in range(V7X_SUBLANES):
        part = None
        for q in range((A_CONV - 1 - r) // V7X_SUBLANES + 1):
            k = A_CONV - 1 - (V7X_SUBLANES * q + r)
            term = caw_ref[k:k + 1, :] * abuf[pl.ds(A_HALO - V7X_SUBLANES * (q + 1), ext), :]
            part = term if part is None else part + term
        if r > 0:
            part = pltpu.roll(part, r, axis=0)
        acc = acc + part[V7X_SUBLANES:, :]
    abuf[0:A_HALO, :] = abuf[tm:tm + A_HALO, :]
    seg = seg_ref[...]
    mu = _split_dot(acc, seg)
    d = acc - mu
    var = _split_dot(d * d, seg)
    yn = d * lax.rsqrt(var + EPS) * lng_ref[...] + lnb_ref[...]
    a_out = yn * _sigmoid(yn)

    off = 2 * A_WIDTH
    cv = z[:, off + B_WIDTH:off + 2 * B_WIDTH] * z[:, off + 2 * B_WIDTH:off + 3 * B_WIDTH]
    bbuf[B_HALO:B_HALO + tm, :] = cv
    bacc = jnp.zeros((tm, B_WIDTH), F32)
    for k in range(B_CONV):
        bacc = bacc + cbw_ref[k:k + 1, :] * bbuf[pl.ds(B_HALO - (B_CONV - 1) + k, tm), :]
    bbuf[0:B_HALO, :] = bbuf[tm:tm + B_HALO, :]
    bb = z[:, off:off + B_WIDTH] * bacc

    y = (jnp.dot(a_out.astype(BF16), wout_ref[0:A_WIDTH, :], preferred_element_type=F32)
         + jnp.dot(bb.astype(BF16), wout_ref[A_WIDTH:A_WIDTH + B_WIDTH, :],
                   preferred_element_type=F32))
    o_ref[...] = x + mod[2:3] * y


def _conv_mix_call(x, mod, norm_g, w_in, caw, cab, lng, lnb, cbw, w_out, seg, *, layer, tm=512):
    even_in = w_in.shape[1]
    return pl.pallas_call(
        functools.partial(_conv_mix_kernel, tm=tm),
        grid=(SEQ // tm,),
        in_specs=[
            pl.BlockSpec((tm, D_MODEL), lambda i: (i, 0)),
            _resident((None, None, 3, D_MODEL), lambda i: (layer, 1, 0, 0)),
            _resident((None, 1, D_MODEL), lambda i: (layer, 0, 0)),
            _full((D_MODEL, even_in)),
            _full((A_CONV, A_WIDTH)),
            _full((1, A_WIDTH)),
            _full((1, A_WIDTH)),
            _full((1, A_WIDTH)),
            _full((B_CONV, B_WIDTH)),
            _full((A_WIDTH + B_WIDTH, D_MODEL)),
            _full((A_WIDTH, A_WIDTH)),
        ],
        out_specs=pl.BlockSpec((tm, D_MODEL), lambda i: (i, 0)),
        out_shape=jax.ShapeDtypeStruct((SEQ, D_MODEL), F32),
        scratch_shapes=[pltpu.VMEM((A_HALO + tm, A_WIDTH), F32),
                        pltpu.VMEM((B_HALO + tm, B_WIDTH), F32)],
        compiler_params=_params(1),
        name="conv_mixers",
    )(x, mod, norm_g, w_in, caw, cab, lng, lnb, cbw, w_out, seg)


def _pm_proj_kernel(x_ref, mod_ref, g_ref, wu_ref, wq_ref, wkvt_ref, pw_ref,
                    pb_ref, ps_ref, pooled_ref, qrow_ref, kv_ref, km_ref,
                    ubuf, hmean, *, tm):
    i = pl.program_id(0)

    @pl.when(i == 0)
    def _():
        ubuf[0:POOL_HALO, :] = jnp.zeros((POOL_HALO, C_WIDTH), F32)

    x = x_ref[...]
    h = _norm_mod(x, g_ref[...], mod_ref[...]).astype(BF16)
    u = jnp.dot(h, wu_ref[...], preferred_element_type=F32)
    qr = jnp.dot(h, wq_ref[...], preferred_element_type=F32)
    kvt = lax.dot_general(wkvt_ref[...], h, _NT, preferred_element_type=F32)

    ubuf[POOL_HALO:POOL_HALO + tm, :] = u
    t1 = (i * tm + 1 + lax.broadcasted_iota(I32, (tm, C_GROUP_DIM), 0)).astype(F32)
    for gi, w in enumerate(POOL_WINDOWS):
        c0 = gi * C_GROUP_DIM
        s = ubuf[:, c0:c0 + C_GROUP_DIM]
        span = 1
        while span < w:
            s = s + pltpu.roll(s, span, axis=0)
            span *= 2
        assert span == w < POOL_HALO + 1
        s = s[POOL_HALO:, :]
        pooled = s / jnp.minimum(t1, float(w)) - u[:, c0:c0 + C_GROUP_DIM]
        mixed = (jnp.dot(pooled.astype(BF16), pw_ref[gi], preferred_element_type=F32)
                 + pb_ref[:, c0:c0 + C_GROUP_DIM])
        pooled_ref[:, c0:c0 + C_GROUP_DIM] = (
            mixed * ps_ref[:, c0:c0 + C_GROUP_DIM]).astype(BF16)
    ubuf[0:POOL_HALO, :] = ubuf[tm:tm + POOL_HALO, :]

    lane = lax.broadcasted_iota(I32, (tm, HEAD_PAD), 1)
    tail = jnp.where(lane == ALIBI_COL, 1.0, 0.0)
    qs = qr * (HEAD_DIM ** -0.5)
    for pair in range(D_HEADS // 2):
        both = qs[:, pair * HEAD_PAD:(pair + 1) * HEAD_PAD]
        qrow_ref[2 * pair] = jnp.where(lane < HEAD_DIM, both, tail)
        qrow_ref[2 * pair + 1] = jnp.where(
            lane < HEAD_DIM, pltpu.roll(both, HEAD_DIM, axis=1), tail)
    per_tile = tm // MOBA_BLOCK
    kvb = kvt.astype(BF16)
    n_alibi = SLAB_K_ROWS - HEAD_DIM
    first = lax.broadcasted_iota(I32, (n_alibi, MOBA_BLOCK), 0) == 0
    koff = lax.broadcasted_iota(I32, (n_alibi, MOBA_BLOCK), 1).astype(F32)
    for hh in range(D_HEADS):
        alibi = jnp.where(first, koff * (2.0 ** -(hh + 1)), 0.0).astype(BF16)
        for sb in range(per_tile):
            keys = slice(sb * MOBA_BLOCK, (sb + 1) * MOBA_BLOCK)
            kv_ref[hh, sb, 0:HEAD_DIM, :] = kvb[hh * HEAD_DIM:(hh + 1) * HEAD_DIM, keys]
            kv_ref[hh, sb, HEAD_DIM:SLAB_K_ROWS, :] = alibi
            kv_ref[hh, sb, SLAB_K_ROWS:SLAB_ROWS, :] = kvb[
                D_WIDTH + hh * HEAD_DIM:D_WIDTH + (hh + 1) * HEAD_DIM, keys]

    for sb in range(per_tile):
        blk = h[sb * MOBA_BLOCK:(sb + 1) * MOBA_BLOCK, :].astype(F32)
        hmean[pl.ds(i * per_tile + sb, 1), :] = jnp.mean(blk, axis=0, keepdims=True)

    @pl.when(i == pl.num_programs(0) - 1)
    def _():
        hi, lo = _split(hmean[...])
        wkt = wkvt_ref[0:D_WIDTH, :]
        km_ref[...] = (lax.dot_general(hi, wkt, _NT, preferred_element_type=F32)
                       + lax.dot_general(lo, wkt, _NT, preferred_element_type=F32))


def _pm_proj_call(x, mod, norm_g, wu, wq, wkvt, pw, pb, ps, *, layer, tm=512):
    return pl.pallas_call(
        functools.partial(_pm_proj_kernel, tm=tm),
        grid=(SEQ // tm,),
        in_specs=[
            pl.BlockSpec((tm, D_MODEL), lambda i: (i, 0)),
            _resident((None, None, 3, D_MODEL), lambda i: (layer, 1, 0, 0)),
            _resident((None, 1, D_MODEL), lambda i: (layer, 0, 0)),
            _full((D_MODEL, C_WIDTH)),
            _full((D_MODEL, D_WIDTH)),
            _full((2 * D_WIDTH, D_MODEL)),
            _full((len(POOL_WINDOWS), C_GROUP_DIM, C_GROUP_DIM)),
            _full((1, C_WIDTH)),
            _full((1, C_WIDTH)),
        ],
        out_specs=[
            pl.BlockSpec((tm, C_WIDTH), lambda i: (i, 0)),
            pl.BlockSpec((D_HEADS, tm, HEAD_PAD), lambda i: (0, i, 0)),
            pl.BlockSpec((D_HEADS, tm // MOBA_BLOCK, SLAB_ROWS, MOBA_BLOCK),
                         lambda i: (0, i, 0, 0)),
            pl.BlockSpec((N_BLOCKS, D_WIDTH), lambda i: (0, 0)),
        ],
        out_shape=[
            jax.ShapeDtypeStruct((SEQ, C_WIDTH), BF16),
            jax.ShapeDtypeStruct((D_HEADS, SEQ, HEAD_PAD), F32),
            jax.ShapeDtypeStruct((D_HEADS, N_BLOCKS, SLAB_ROWS, MOBA_BLOCK), BF16),
            jax.ShapeDtypeStruct((N_BLOCKS, D_WIDTH), F32),
        ],
        scratch_shapes=[pltpu.VMEM((POOL_HALO + tm, C_WIDTH), F32),
                        pltpu.VMEM((N_BLOCKS, D_MODEL), F32)],
        compiler_params=_params(1),
        name="pool_qkv_proj",
    )(x, mod, norm_g, wu, wq, wkvt, pw, pb, ps)


def _block_rows(qs, slabs, mask, lse_add):
    n = qs[0].shape[0]
    kpad = jnp.zeros((HEAD_PAD - SLAB_K_ROWS, MOBA_BLOCK), BF16)
    s = jnp.concatenate(
        [jnp.dot(q, jnp.concatenate([slab[0:SLAB_K_ROWS, :], kpad], axis=0),
                 preferred_element_type=F32)
         for q, slab in zip(qs, slabs)], axis=0)
    if mask is not None:
        s = jnp.where(mask, s, NEG_BIG)
    m = jnp.max(s, axis=1, keepdims=True)
    p = jnp.exp(s - m).astype(BF16)
    vpad = jnp.zeros((HEAD_PAD - HEAD_DIM, MOBA_BLOCK), BF16)
    ones = jnp.ones((HEAD_PAD, MOBA_BLOCK), BF16)
    ot = jnp.concatenate(
        [lax.dot_general(p[u * n:(u + 1) * n],
                         jnp.concatenate([slab[SLAB_K_ROWS:SLAB_ROWS, :], vpad, ones], axis=0),
                         _NT, preferred_element_type=F32)
         for u, slab in enumerate(slabs)], axis=0)
    o, tot = ot[:, 0:HEAD_PAD], ot[:, HEAD_PAD:2 * HEAD_PAD]
    lse = m + jnp.log(tot)
    if lse_add is not None:
        lse = jnp.concatenate(
            [lse[u * n:(u + 1) * n] + add for u, add in enumerate(lse_add)], axis=0)
    low = lax.broadcasted_iota(I32, o.shape, 1) < HEAD_DIM
    return jnp.where(low, o / tot, lse)


def _onehot_pairs(sel_rows):
    blk = lax.broadcasted_iota(I32, (N_BLOCKS, MOBA_BLOCK), 0)
    return jnp.concatenate(
        [jnp.where(blk == sel_rows[s], 1.0, 0.0) for s in range(MOBA_TOPK)], axis=1)


def _gate_own_kernel(anchor_ref, qrow_ref, kv_ref, km_ref, tri_ref,
                     sel_ref, rank_ref, cnt_ref, own_ref, *, heads, blocks):
    del anchor_ref
    n_pair = MOBA_TOPK * MOBA_BLOCK
    tiles = [(bb, a) for bb in range(blocks) for a in range(heads)]
    rows_of = lambda bb: slice(bb * MOBA_BLOCK, (bb + 1) * MOBA_BLOCK)
    qs = [qrow_ref[a, rows_of(bb), :].astype(BF16) for bb, a in tiles]

    kms = [_split(km_ref[a]) for a in range(heads)]
    gates = [lax.dot_general(kms[a][0], q, _NT, preferred_element_type=F32)
             + lax.dot_general(kms[a][1], q, _NT, preferred_element_type=F32)
             for (bb, a), q in zip(tiles, qs)]
    width = len(tiles) * MOBA_BLOCK
    blk = lax.broadcasted_iota(I32, (N_BLOCKS, width), 0)
    lane = lax.broadcasted_iota(I32, (1, width), 1)
    own = pl.program_id(0) * blocks + lane // (heads * MOBA_BLOCK)
    g = jnp.where(blk < own, jnp.concatenate(gates, axis=1), -jnp.inf)
    slots = []
    for s in range(MOBA_TOPK):
        best = jnp.max(g, axis=0, keepdims=True)
        idx = jnp.min(jnp.where(g == best, blk, N_BLOCKS), axis=0, keepdims=True)
        g = jnp.where(blk == idx, -jnp.inf, g)
        slots.append(jnp.where(s < own, idx, N_BLOCKS))

    onehots = []
    for t, (bb, a) in enumerate(tiles):
        sel_rows = [slot[:, t * MOBA_BLOCK:(t + 1) * MOBA_BLOCK] for slot in slots]
        sel_ref[a, :, rows_of(bb)] = jnp.concatenate(sel_rows, axis=0)
        onehots.append(_onehot_pairs(sel_rows))
    onehot = jnp.concatenate(onehots, axis=0)
    oh = onehot.astype(BF16)
    before = jnp.dot(oh, tri_ref[...], preferred_element_type=F32)
    hit = onehot * before
    ones = jnp.ones((8, n_pair), BF16)
    cnt = lax.dot_general(ones, oh, _NT, preferred_element_type=F32)
    for t, (bb, a) in enumerate(tiles):
        rank = jnp.sum(hit[t * N_BLOCKS:(t + 1) * N_BLOCKS], axis=0, keepdims=True).astype(I32)
        rank_ref[a, :, rows_of(bb)] = jnp.concatenate(
            [rank[:, s * MOBA_BLOCK:(s + 1) * MOBA_BLOCK] for s in range(MOBA_TOPK)], axis=0)
        cnt_ref[a, bb] = cnt[0:1, t * N_BLOCKS:(t + 1) * N_BLOCKS].astype(I32)

    qi = lax.broadcasted_iota(I32, (width, MOBA_BLOCK), 0) % MOBA_BLOCK
    ki = lax.broadcasted_iota(I32, (width, MOBA_BLOCK), 1)
    rows = _block_rows(qs, [kv_ref.at[a, bb] for bb, a in tiles], ki <= qi, None)
    for t, (bb, a) in enumerate(tiles):
        own_ref[a, rows_of(bb), :] = rows[t * MOBA_BLOCK:(t + 1) * MOBA_BLOCK]


def _gate_own_call(anchor, qrows, kv, kmean_h, tri, *, split, blocks=GATE_BLOCKS_PER_STEP):
    heads = SPLIT_HEADS
    rows = blocks * MOBA_BLOCK
    return pl.pallas_call(
        functools.partial(_gate_own_kernel, heads=heads, blocks=blocks),
        grid=(N_BLOCKS // blocks,),
        in_specs=[
            pl.BlockSpec(memory_space=pltpu.SMEM),
            pl.BlockSpec((heads, rows, HEAD_PAD), lambda b: (split, b, 0)),
            pl.BlockSpec((heads, blocks, SLAB_ROWS, MOBA_BLOCK), lambda b: (split, b, 0, 0)),
            pl.BlockSpec((heads, N_BLOCKS, HEAD_PAD), lambda b: (split, 0, 0)),
            _full((MOBA_TOPK * MOBA_BLOCK, MOBA_TOPK * MOBA_BLOCK)),
        ],
        out_specs=[
            pl.BlockSpec((heads, MOBA_TOPK, rows), lambda b: (0, 0, b)),
            pl.BlockSpec((heads, MOBA_TOPK, rows), lambda b: (0, 0, b)),
            pl.BlockSpec((heads, blocks, 1, N_BLOCKS), lambda b: (0, b, 0, 0)),
            pl.BlockSpec((heads, rows, HEAD_PAD), lambda b: (0, b, 0)),
        ],
        out_shape=[
            jax.ShapeDtypeStruct((heads, MOBA_TOPK, SEQ), I32),
            jax.ShapeDtypeStruct((heads, MOBA_TOPK, SEQ), I32),
            jax.ShapeDtypeStruct((heads, N_BLOCKS, 1, N_BLOCKS), I32),
            jax.ShapeDtypeStruct((heads, SEQ, HEAD_PAD), F32),
        ],
        compiler_params=_params(1),
        name=f"moba_gate_own_{split}",
    )(anchor, qrows, kv, kmean_h, tri)


def _route_pos_kernel(sel_ref, rank_ref, base_ref, pos_ref, *, blocks):
    lane = lax.broadcasted_iota(I32, (1, MOBA_BLOCK), 1)
    for bb in range(blocks):
        qs = slice(bb * MOBA_BLOCK, (bb + 1) * MOBA_BLOCK)
        rows = [[] for _ in range(MOBA_TOPK)]
        for a in range(SPLIT_HEADS):
            sel = sel_ref[a, :, qs]
            onehot = _onehot_pairs([sel[s:s + 1, :] for s in range(MOBA_TOPK)]).astype(BF16)
            dig = jnp.dot(base_ref[a, bb].astype(BF16), onehot,
                          preferred_element_type=F32)
            base = (dig[0:1] * float(DIGIT * DIGIT) + dig[1:2] * float(DIGIT)
                    + dig[2:3]).astype(I32)
            for s in range(MOBA_TOPK):
                p = base[:, s * MOBA_BLOCK:(s + 1) * MOBA_BLOCK] + rank_ref[a, s:s + 1, qs]
                rows[s].append(
                    jnp.where(sel[s:s + 1, :] < N_BLOCKS, p, TRASH_ROW0 + lane % SC_WINDOW))
        for s in range(MOBA_TOPK):
            pos_ref[s, :, qs] = jnp.concatenate(rows[s], axis=0)


def _route_pos_call(sel, rank, base_digits, *, blocks=8):
    width = blocks * MOBA_BLOCK
    return pl.pallas_call(
        functools.partial(_route_pos_kernel, blocks=blocks),
        grid=(N_BLOCKS // blocks,),
        in_specs=[
            pl.BlockSpec((SPLIT_HEADS, MOBA_TOPK, width), lambda b: (0, 0, b)),
            pl.BlockSpec((SPLIT_HEADS, MOBA_TOPK, width), lambda b: (0, 0, b)),
            pl.BlockSpec((SPLIT_HEADS, blocks, 8, N_BLOCKS), lambda b: (0, b, 0, 0)),
        ],
        out_specs=pl.BlockSpec((MOBA_TOPK, SPLIT_HEADS, width), lambda b: (0, 0, b)),
        out_shape=jax.ShapeDtypeStruct((MOBA_TOPK, SPLIT_HEADS, SEQ), I32),
        compiler_params=_params(1),
        name="moba_route_pos",
    )(sel, rank, base_digits)


def _route_tables(cnt, split):
    cnt = cnt.reshape(SPLIT_HEADS, N_BLOCKS, N_BLOCKS)
    tiles = (cnt.sum(axis=1) + ROUTE_TILE - 1) // ROUTE_TILE
    tiles_flat = tiles.reshape(N_GROUPS)
    tile0 = jnp.cumsum(tiles_flat) - tiles_flat
    n_tiles = tiles_flat.sum().astype(I32)
    within = jnp.cumsum(cnt, axis=1) - cnt
    base = tile0.reshape(SPLIT_HEADS, 1, N_BLOCKS) * ROUTE_TILE + within
    digits = jnp.stack([base // (DIGIT * DIGIT), (base // DIGIT) % DIGIT, base % DIGIT],
                       axis=2).astype(F32)
    digits = jnp.pad(digits, ((0, 0), (0, 0), (0, 8 - 3), (0, 0)))
    t = jnp.arange(MAX_TILES, dtype=I32)
    ended = (tile0 + tiles_flat)[None, :] <= t[:, None]
    group_of_tile = jnp.minimum(ended.sum(axis=1).astype(I32), N_GROUPS - 1)
    tile_h = group_of_tile // N_BLOCKS
    head = split * SPLIT_HEADS + tile_h
    tile_step = lax.bitcast_convert_type((127 + 7 - head) << 23, F32)
    return digits, tile_h, group_of_tile % N_BLOCKS, n_tiles.reshape(1), tile_step


def _sc_mesh():
    return plsc.VectorSubcoreMesh(core_axis_name="core", subcore_axis_name="subcore")


def _dispatch_call(rows, pos_by_slot, *, split):
    n_rows = SPLIT_HEADS * SEQ
    first_window = split * n_rows // SC_WINDOW

    @functools.partial(
        pl.kernel, mesh=_sc_mesh(), scratch_types=[],
        out_type=jax.ShapeDtypeStruct((ROUTE_ROWS, HEAD_PAD), F32))
    def dispatch(x_hbm, i0_hbm, i1_hbm, i2_hbm, o_hbm):
        def body(x_vmem, i0_vmem, i1_vmem, i2_vmem):
            pltpu.sync_copy(x_vmem, o_hbm.at[i0_vmem.at[0]])
            pltpu.sync_copy(x_vmem, o_hbm.at[i1_vmem.at[0]])
            pltpu.sync_copy(x_vmem, o_hbm.at[i2_vmem.at[0]])

        idx_spec = pl.BlockSpec((1, SC_WINDOW), lambda i: (0, i))
        pltpu.emit_pipeline(
            body,
            grid=(n_rows // SC_WINDOW,),
            in_specs=[pl.BlockSpec((SC_WINDOW, HEAD_PAD), lambda i: (first_window + i, 0)),
                      idx_spec, idx_spec, idx_spec],
            out_specs=[],
            core_axis_name=("core", "subcore"),
            dimension_semantics=(pltpu.PARALLEL,),
        )(x_hbm, i0_hbm, i1_hbm, i2_hbm)

    return dispatch(rows, *pos_by_slot)


def _collect_call(table, idx):
    n_rows = idx.shape[1]

    @functools.partial(
        pl.kernel, mesh=_sc_mesh(), scratch_types=[],
        out_type=jax.ShapeDtypeStruct((n_rows, HEAD_PAD), F32))
    def collect(x_hbm, i_hbm, o_hbm):
        def body(i_vmem, o_vmem):
            pltpu.sync_copy(x_hbm.at[i_vmem.at[0]], o_vmem)

        pltpu.emit_pipeline(
            body,
            grid=(n_rows // SC_WINDOW,),
            in_specs=[pl.BlockSpec((1, SC_WINDOW), lambda i: (0, i))],
            out_specs=[pl.BlockSpec((SC_WINDOW, HEAD_PAD), lambda i: (i, 0))],
            core_axis_name=("core", "subcore"),
            dimension_semantics=(pltpu.PARALLEL,),
        )(i_hbm, o_hbm)

    return collect(table, idx)


def _routed_attn_kernel(th_ref, tj_ref, nt_ref, step_ref, q_ref, *refs, tiles):
    kv_refs, o_ref = refs[0:tiles], refs[tiles]
    t0 = pl.program_id(0) * tiles

    @pl.when(t0 < nt_ref[0])
    def _():
        qs = [q_ref[u * ROUTE_TILE:(u + 1) * ROUTE_TILE, :].astype(BF16) for u in range(tiles)]
        adds = [step_ref[t0 + u] * tj_ref[t0 + u].astype(F32) for u in range(tiles)]
        o_ref[...] = _block_rows(qs, kv_refs, None, adds)

    @pl.when(t0 == MAX_TILES)
    def _():
        low = lax.broadcasted_iota(I32, o_ref.shape, 1) < HEAD_DIM
        o_ref[...] = jnp.where(low, 0.0, NEG_BIG)


def _routed_attn_call(tile_h, tile_j, n_tiles, tile_step, routed_q, kv,
                      *, split, tiles=ROUTE_TILES_PER_STEP):
    n_steps = MAX_TILES // tiles + 1
    assert TRASH_ROW0 == (n_steps - 1) * tiles * ROUTE_TILE
    live = lambda s, th, tj, nt: (jnp.minimum(s, (nt[0] + tiles - 1) // tiles), 0)
    live_or_last = lambda s, th, tj, nt: (
        jnp.where(s == n_steps - 1, s, jnp.minimum(s, (nt[0] + tiles - 1) // tiles)), 0)
    head0 = split * SPLIT_HEADS

    def kv_spec(u):
        def slab(s, th, tj, nt):
            t = jnp.minimum(s, n_steps - 2) * tiles + u
            return head0 + th[t], tj[t], 0, 0
        return pl.BlockSpec((None, None, SLAB_ROWS, MOBA_BLOCK), slab)

    grid_spec = pltpu.PrefetchScalarGridSpec(
        num_scalar_prefetch=3,
        grid=(n_steps,),
        in_specs=([pl.BlockSpec(memory_space=pltpu.SMEM),
                   pl.BlockSpec((tiles * ROUTE_TILE, HEAD_PAD), live)]
                  + [kv_spec(u) for u in range(tiles)]),
        out_specs=pl.BlockSpec((tiles * ROUTE_TILE, HEAD_PAD), live_or_last),
    )
    return pl.pallas_call(
        functools.partial(_routed_attn_kernel, tiles=tiles),
        grid_spec=grid_spec,
        out_shape=jax.ShapeDtypeStruct((ROUTE_ROWS, HEAD_PAD), F32),
        compiler_params=_params(1),
        name=f"moba_routed_attn_{split}",
    )(tile_h, tile_j, n_tiles, tile_step, routed_q, *([kv] * tiles))


def _pm_out_kernel(slope_ref, x_ref, mod_ref, pooled_ref, *refs, blocks):
    own_refs, got_refs = refs[0:N_SPLITS], refs[N_SPLITS:2 * N_SPLITS]
    wp_ref, wa_ref, o_ref = refs[2 * N_SPLITS:]
    low = lax.broadcasted_iota(I32, (MOBA_BLOCK, HEAD_PAD), 1) < HEAD_DIM

    def lse_of(part):
        return jnp.where(low, pltpu.roll(part, HEAD_DIM, axis=1), part)

    y = jnp.dot(pooled_ref[...], wp_ref[...], preferred_element_type=F32)
    for hh in range(D_HEADS):
        own_ref, got_ref = own_refs[hh // SPLIT_HEADS], got_refs[hh // SPLIT_HEADS]
        a = hh % SPLIT_HEADS
        atts = []
        for bb in range(blocks):
            own = pl.program_id(0) * blocks + bb
            rows = slice(bb * MOBA_BLOCK, (bb + 1) * MOBA_BLOCK)
            own_shift = slope_ref[hh] * (own * MOBA_BLOCK).astype(F32)
            parts = [own_ref[a, rows, :]] + [got_ref[s, a, rows, :] for s in range(MOBA_TOPK)]
            lses = [lse_of(parts[0]) + own_shift] + [lse_of(part) for part in parts[1:]]
            top = functools.reduce(jnp.maximum, lses)
            num = jnp.zeros((MOBA_BLOCK, HEAD_PAD), F32)
            den = jnp.zeros((MOBA_BLOCK, HEAD_PAD), F32)
            for part, ls in zip(parts, lses):
                w = jnp.exp(ls - top)
                num = num + w * part
                den = den + w
            atts.append((num / den).astype(BF16))
        y = y + jnp.dot(jnp.concatenate(atts, axis=0), wa_ref[hh], preferred_element_type=F32)
    o_ref[...] = x_ref[...] + mod_ref[2:3, :] * y


def _pm_out_call(slopes, x, mod, pooled, own_parts, got_parts, wp, wa, *, layer, blocks=2):
    tm = blocks * MOBA_BLOCK
    return pl.pallas_call(
        functools.partial(_pm_out_kernel, blocks=blocks),
        grid=(SEQ // tm,),
        in_specs=[
            pl.BlockSpec(memory_space=pltpu.SMEM),
            pl.BlockSpec((tm, D_MODEL), lambda i: (i, 0)),
            _resident((None, None, 3, D_MODEL), lambda i: (layer, 1, 0, 0)),
            pl.BlockSpec((tm, C_WIDTH), lambda i: (i, 0)),
        ] + [pl.BlockSpec((SPLIT_HEADS, tm, HEAD_PAD), lambda i: (0, i, 0))] * N_SPLITS
        + [pl.BlockSpec((MOBA_TOPK, SPLIT_HEADS, tm, HEAD_PAD), lambda i: (0, 0, i, 0))] * N_SPLITS
        + [
            _full((C_WIDTH, D_MODEL)),
            _full((D_HEADS, HEAD_PAD, D_MODEL)),
        ],
        out_specs=pl.BlockSpec((tm, D_MODEL), lambda i: (i, 0)),
        out_shape=jax.ShapeDtypeStruct((SEQ, D_MODEL), F32),
        compiler_params=_params(1),
        name="pool_moba_out",
    )(slopes, x, mod, pooled, *own_parts, *got_parts, wp, wa)


def _head_pad(w, axis):
    shape = list(w.shape)
    shape[axis:axis + 1] = [D_HEADS, HEAD_DIM]
    pad = [(0, 0)] * len(shape)
    pad[axis + 1] = (0, HEAD_PAD - HEAD_DIM)
    out = jnp.pad(w.reshape(shape), pad)
    shape[axis:axis + 2] = [D_HEADS * HEAD_PAD]
    return out.reshape(shape)


def kernel(x, c, ada_w, ada_b, ffn_norm, ffn_w_gate, ffn_w_up, ffn_w_down, mix_norm, conv_w_in,
           conv_a_w, conv_a_b, conv_a_ln_g, conv_a_ln_b, conv_b_w, conv_w_out, pm_w_in, pool_w,
           pool_b, pool_scale, pm_w_out, final_norm):
    assert x.shape == (1, SEQ, D_MODEL) and c.shape == (1, D_MODEL)
    xs = x.reshape(SEQ, D_MODEL)
    mod = _mod_call(c, ada_w, ada_b)

    ffn_g = ffn_norm.reshape(DEPTH, 2, 1, D_MODEL)
    mix_g = mix_norm.reshape(DEPTH, 1, D_MODEL)
    fin = final_norm.reshape(1, D_MODEL)
    ffn_stacks = (ffn_w_gate, ffn_w_up, ffn_w_down)

    def ffn(xs, weights, layer, sub):
        last = (layer, sub) == (DEPTH - 1, 1)
        after = None if last else ((layer + sub, 1 - sub), *ffn_stacks)
        return _ffn_call(xs, mod, ffn_g, weights, fin, after, layer=layer, sub=sub, final=last)

    xs, ffn_w = ffn(xs, tuple(w[0, 0].astype(BF16) for w in ffn_stacks), layer=0, sub=0)
    grp = np.arange(A_WIDTH) // (A_WIDTH // A_GROUPS)
    seg = jnp.asarray((grp[:, None] == grp[None, :]) / (A_WIDTH // A_GROUPS), dtype=BF16)
    xs = _conv_mix_call(
        xs, mod, mix_g, conv_w_in[0].astype(BF16), conv_a_w[0], conv_a_b[0].reshape(1, A_WIDTH),
        conv_a_ln_g[0].reshape(1, A_WIDTH), conv_a_ln_b[0].reshape(1, A_WIDTH), conv_b_w[0],
        conv_w_out[0].astype(BF16), seg, layer=0)
    xs, ffn_w = ffn(xs, ffn_w, layer=0, sub=1)

    xs, ffn_w = ffn(xs, ffn_w, layer=1, sub=0)
    w_in = pm_w_in[0].astype(BF16)
    w_u = w_in[:, 0:C_WIDTH]
    w_q = w_in[:, C_WIDTH:C_WIDTH + D_WIDTH]
    w_kvt = w_in[:, C_WIDTH + D_WIDTH:C_WIDTH + 3 * D_WIDTH].T
    pooled, qrows, kv, kmean = _pm_proj_call(
        xs, mod, mix_g, w_u, w_q, w_kvt, pool_w[0].astype(BF16),
        pool_b[0].reshape(1, C_WIDTH), pool_scale[0].reshape(1, C_WIDTH), layer=1)
    kmean_h = _head_pad(kmean, 1).reshape(N_BLOCKS, D_HEADS, HEAD_PAD).transpose(1, 0, 2)

    n_pair = MOBA_TOPK * MOBA_BLOCK
    tri = jnp.asarray(np.arange(n_pair)[:, None] < np.arange(n_pair)[None, :], dtype=BF16)
    q_flat = qrows.reshape(D_HEADS * SEQ, HEAD_PAD)
    own_parts, got_parts = [], []
    anchor = jnp.zeros((1,), I32)
    for split in range(N_SPLITS):
        sel, rank, cnt, own_part = _gate_own_call(anchor, qrows, kv, kmean_h, tri, split=split)
        base_digits, tile_h, tile_j, n_tiles, tile_step = _route_tables(cnt, split)
        pos = _route_pos_call(sel, rank, base_digits)
        anchor = pos[0, 0, 0:1]
        pos_by_slot = [pos[s].reshape(1, SPLIT_HEADS * SEQ) for s in range(MOBA_TOPK)]
        routed_q = _dispatch_call(q_flat, pos_by_slot, split=split)
        routed_parts = _routed_attn_call(tile_h, tile_j, n_tiles, tile_step, routed_q, kv,
                                         split=split)
        got = _collect_call(routed_parts, pos.reshape(1, N_PAIRS))
        own_parts.append(own_part)
        got_parts.append(got.reshape(MOBA_TOPK, SPLIT_HEADS, SEQ, HEAD_PAD))

    slopes = jnp.asarray(2.0 ** (-8.0 * np.arange(1, D_HEADS + 1) / D_HEADS), dtype=F32)
    w_out = pm_w_out[0].astype(BF16)
    w_att = _head_pad(w_out[C_WIDTH:], 0).reshape(D_HEADS, HEAD_PAD, D_MODEL)
    xs = _pm_out_call(slopes, xs, mod, pooled, own_parts, got_parts, w_out[0:C_WIDTH], w_att,
                      layer=1)
    xs, _ = ffn(xs, ffn_w, layer=1, sub=1)
    return xs.reshape(1, SEQ, D_MODEL)
```

```python
import functools

import numpy as np
import jax
import jax.numpy as jnp
from jax import lax
from jax.experimental import pallas as pl
from jax.experimental.pallas import tpu as pltpu
from jax.experimental.pallas import tpu_sc as plsc

D_MODEL = 1024
SEQ = 16384
DEPTH = 2
N_SUBLAYERS = 3
D_FF = 2816
EPS = 1e-6

A_WIDTH = 512
A_GROUPS = 8
A_CONV = 31
B_WIDTH = 512
B_CONV = 3

POOL_WINDOWS = (2, 4, 8, 16)
C_WIDTH = 512
C_GROUP_DIM = 128
D_HEADS = 8
HEAD_DIM = 64
D_WIDTH = 512
MOBA_BLOCK = 256
MOBA_TOPK = 3
N_BLOCKS = SEQ // MOBA_BLOCK

V7X_SUBLANES = 8
V7X_BF16_SUBLANES = 16
V7X_VMEM_LIMIT_BYTES = 56 * 1024 * 1024

A_HALO = 32
B_HALO = 8
POOL_HALO = 16
NEG_BIG = -1e30

HEAD_PAD = 128
ALIBI_COL = HEAD_DIM
SLAB_K_ROWS = HEAD_DIM + V7X_BF16_SUBLANES
SLAB_ROWS = SLAB_K_ROWS + HEAD_DIM

SPLIT_HEADS = 4
N_SPLITS = D_HEADS // SPLIT_HEADS
ROUTE_TILE = 256
ROUTE_TILES_PER_STEP = 32
GATE_BLOCKS_PER_STEP = 8
N_PAIRS = SPLIT_HEADS * SEQ * MOBA_TOPK
N_GROUPS = SPLIT_HEADS * N_BLOCKS
MAX_TILES = N_PAIRS // ROUTE_TILE + N_GROUPS
TRASH_ROW0 = MAX_TILES * ROUTE_TILE
ROUTE_ROWS = (MAX_TILES + ROUTE_TILES_PER_STEP) * ROUTE_TILE
SC_WINDOW = 128
DIGIT = 128
assert MAX_TILES % ROUTE_TILES_PER_STEP == 0 and N_BLOCKS % GATE_BLOCKS_PER_STEP == 0
assert ROUTE_ROWS <= DIGIT ** 3 and MOBA_BLOCK <= ROUTE_TILE

BF16 = jnp.bfloat16
F32 = jnp.float32
I32 = jnp.int32


def _params(n_axes):
    return pltpu.CompilerParams(
        dimension_semantics=("arbitrary",) * n_axes,
        vmem_limit_bytes=V7X_VMEM_LIMIT_BYTES)


def _resident(block_shape, index_map):
    return pl.BlockSpec(block_shape, index_map, pipeline_mode=pl.Buffered(1))


def _full(shape):
    return _resident(shape, lambda *_: (0,) * len(shape))


def _sigmoid(v):
    return 1.0 / (1.0 + jnp.exp(-v))


def _norm_mod(x, g, mod):
    r = lax.rsqrt(jnp.mean(x * x, axis=-1, keepdims=True) + EPS)
    return (x * r) * (g * (1.0 + mod[1:2])) + mod[0:1]


def _split(v):
    hi = v.astype(BF16)
    return hi, (v - hi.astype(F32)).astype(BF16)


def _split_dot(v, w):
    hi, lo = _split(v)
    return (jnp.dot(hi, w, preferred_element_type=F32)
            + jnp.dot(lo, w, preferred_element_type=F32))


_NT = (((1,), (1,)), ((), ()))


def _mod_kernel(c_ref, w_ref, b_ref, o_ref):
    c = c_ref[...]
    cond = c * _sigmoid(c)
    o_ref[0] = jnp.sum(w_ref[0] * cond, axis=0, keepdims=True) + b_ref[0]


def _mod_call(c, ada_w, ada_b):
    n_out = N_SUBLAYERS * 3 * D_MODEL
    tn = 3 * D_MODEL
    out = pl.pallas_call(
        _mod_kernel,
        grid=(DEPTH, n_out // tn),
        in_specs=[
            pl.BlockSpec((D_MODEL, 1), lambda l, j: (0, 0)),
            pl.BlockSpec((1, D_MODEL, tn), lambda l, j: (l, 0, j)),
            pl.BlockSpec((1, 1, tn), lambda l, j: (l, 0, j)),
        ],
        out_specs=pl.BlockSpec((1, 1, tn), lambda l, j: (l, 0, j)),
        out_shape=jax.ShapeDtypeStruct((DEPTH, 1, n_out), F32),
        compiler_params=_params(2),
        name="adaln_mod",
    )(c.reshape(D_MODEL, 1), ada_w, ada_b.reshape(DEPTH, 1, n_out))
    return out.reshape(DEPTH, N_SUBLAYERS, 3, D_MODEL)


def _ffn_kernel(x_ref, mod_ref, g_ref, wg_ref, wu_ref, wd_ref, fin_ref, *refs, final):
    o_ref = refs[len(refs) // 2]
    for src_ref, dst_ref in zip(refs[:len(refs) // 2], refs[len(refs) // 2 + 1:]):
        dst_ref[...] = src_ref[...].astype(BF16)
    x = x_ref[...]
    mod = mod_ref[...]
    h = _norm_mod(x, g_ref[...], mod).astype(BF16)
    gate = jnp.dot(h, wg_ref[...], preferred_element_type=F32)
    up = jnp.dot(h, wu_ref[...], preferred_element_type=F32)
    act = (gate * _sigmoid(gate) * up).astype(BF16)
    y = jnp.dot(act, wd_ref[...], preferred_element_type=F32)
    xn = x + (0.5 * mod[2:3]) * y
    if final:
        r = lax.rsqrt(jnp.mean(xn * xn, axis=-1, keepdims=True) + EPS)
        xn = xn * r * fin_ref[...]
    o_ref[...] = xn


def _ffn_call(x, mod, norm_g, weights, fin, next_weights, *, layer, sub, final, tm=512):
    mod_sub = 0 if sub == 0 else 2
    steps = SEQ // tm
    in_specs = [
        pl.BlockSpec((tm, D_MODEL), lambda i: (i, 0)),
        _resident((None, None, 3, D_MODEL), lambda i: (layer, mod_sub, 0, 0)),
        _resident((None, None, 1, D_MODEL), lambda i: (layer, sub, 0, 0)),
        _full((D_MODEL, D_FF)),
        _full((D_MODEL, D_FF)),
        _full((D_FF, D_MODEL)),
        _full((1, D_MODEL)),
    ]
    out_specs = [pl.BlockSpec((tm, D_MODEL), lambda i: (i, 0))]
    out_shape = [jax.ShapeDtypeStruct((SEQ, D_MODEL), F32)]
    operands = [x, mod, norm_g, *weights, fin]
    if next_weights is not None:
        (nl, ns), *stacks = next_weights
        for w in stacks:
            n_rows, n_cols = w.shape[2:]
            share = 1 if n_rows % (steps * V7X_BF16_SUBLANES) == 0 else 2
            rows = n_rows * share // steps
            assert n_rows * share % steps == 0 and rows % V7X_BF16_SUBLANES == 0
            in_specs.append(pl.BlockSpec((None, None, rows, n_cols),
                                         lambda i, share=share: (nl, ns, i // share, 0)))
            out_specs.append(pl.BlockSpec((rows, n_cols), lambda i, share=share: (i // share, 0)))
            out_shape.append(jax.ShapeDtypeStruct((n_rows, n_cols), BF16))
            operands.append(w)
    out = pl.pallas_call(
        functools.partial(_ffn_kernel, final=final),
        grid=(steps,),
        in_specs=in_specs,
        out_specs=out_specs,
        out_shape=out_shape,
        compiler_params=_params(1),
        name=f"ffn_l{layer}s{sub}",
    )(*operands)
    return out[0], tuple(out[1:])


def _conv_mix_kernel(x_ref, mod_ref, g_ref, win_ref, caw_ref, cab_ref, lng_ref, lnb_ref,
                     cbw_ref, wout_ref, seg_ref, o_ref, abuf, bbuf, *, tm):
    i = pl.program_id(0)

    @pl.when(i == 0)
    def _():
        abuf[0:A_HALO, :] = jnp.zeros((A_HALO, A_WIDTH), F32)
        bbuf[0:B_HALO, :] = jnp.zeros((B_HALO, B_WIDTH), F32)

    x = x_ref[...]
    mod = mod_ref[...]
    h = _norm_mod(x, g_ref[...], mod).astype(BF16)
    z = jnp.dot(h, win_ref[...], preferred_element_type=F32)

    a = z[:, 0:A_WIDTH] * _sigmoid(z[:, A_WIDTH:2 * A_WIDTH])
    abuf[A_HALO:A_HALO + tm, :] = a
    acc = jnp.zeros((tm, A_WIDTH), F32) + cab_ref[...]
    ext = tm + V7X_SUBLANES
    for r in range(V7X_SUBLANES):
        part = None
        for q in range((A_CONV - 1 - r) // V7X_SUBLANES + 1):
            k = A_CONV - 1 - (V7X_SUBLANES * q + r)
            term = caw_ref[k:k + 1, :] * abuf[pl.ds(A_HALO - V7X_SUBLANES * (q + 1), ext), :]
            part = term if part is None else part + term
        if r > 0:
            part = pltpu.roll(part, r, axis=0)
        acc = acc + part[V7X_SUBLANES:, :]
    abuf[0:A_HALO, :] = abuf[tm:tm + A_HALO, :]
    seg = seg_ref[...]
    mu = _split_dot(acc, seg)
    d = acc - mu
    var = _split_dot(d * d, seg)
    yn = d * lax.rsqrt(var + EPS) * lng_ref[...] + lnb_ref[...]
    a_out = yn * _sigmoid(yn)

    off = 2 * A_WIDTH
    cv = z[:, off + B_WIDTH:off + 2 * B_WIDTH] * z[:, off + 2 * B_WIDTH:off + 3 * B_WIDTH]
    bbuf[B_HALO:B_HALO + tm, :] = cv
    bacc = jnp.zeros((tm, B_WIDTH), F32)
    for k in range(B_CONV):
        bacc = bacc + cbw_ref[k:k + 1, :] * bbuf[pl.ds(B_HALO - (B_CONV - 1) + k, tm), :]
    bbuf[0:B_HALO, :] = bbuf[tm:tm + B_HALO, :]
    bb = z[:, off:off + B_WIDTH] * bacc

    y = (jnp.dot(a_out.astype(BF16), wout_ref[0:A_WIDTH, :], preferred_element_type=F32)
         + jnp.dot(bb.astype(BF16), wout_ref[A_WIDTH:A_WIDTH + B_WIDTH, :],
                   preferred_element_type=F32))
    o_ref[...] = x + mod[2:3] * y


def _conv_mix_call(x, mod, norm_g, w_in, caw, cab, lng, lnb, cbw, w_out, seg, *, layer, tm=512):
    even_in = w_in.shape[1]
    return pl.pallas_call(
        functools.partial(_conv_mix_kernel, tm=tm),
        grid=(SEQ // tm,),
        in_specs=[
            pl.BlockSpec((tm, D_MODEL), lambda i: (i, 0)),
            _resident((None, None, 3, D_MODEL), lambda i: (layer, 1, 0, 0)),
            _resident((None, 1, D_MODEL), lambda i: (layer, 0, 0)),
            _full((D_MODEL, even_in)),
            _full((A_CONV, A_WIDTH)),
            _full((1, A_WIDTH)),
            _full((1, A_WIDTH)),
            _full((1, A_WIDTH)),
            _full((B_CONV, B_WIDTH)),
            _full((A_WIDTH + B_WIDTH, D_MODEL)),
            _full((A_WIDTH, A_WIDTH)),
        ],
        out_specs=pl.BlockSpec((tm, D_MODEL), lambda i: (i, 0)),
        out_shape=jax.ShapeDtypeStruct((SEQ, D_MODEL), F32),
        scratch_shapes=[pltpu.VMEM((A_HALO + tm, A_WIDTH), F32),
                        pltpu.VMEM((B_HALO + tm, B_WIDTH), F32)],
        compiler_params=_params(1),
        name="conv_mixers",
    )(x, mod, norm_g, w_in, caw, cab, lng, lnb, cbw, w_out, seg)


def _pm_proj_kernel(x_ref, mod_ref, g_ref, wu_ref, wq_ref, wkvt_ref, pw_ref,
                    pb_ref, ps_ref, pooled_ref, qrow_ref, kv_ref, km_ref,
                    ubuf, hmean, *, tm):
    i = pl.program_id(0)

    @pl.when(i == 0)
    def _():
        ubuf[0:POOL_HALO, :] = jnp.zeros((POOL_HALO, C_WIDTH), F32)

    x = x_ref[...]
    h = _norm_mod(x, g_ref[...], mod_ref[...]).astype(BF16)
    u = jnp.dot(h, wu_ref[...], preferred_element_type=F32)
    qr = jnp.dot(h, wq_ref[...], preferred_element_type=F32)
    kvt = lax.dot_general(wkvt_ref[...], h, _NT, preferred_element_type=F32)

    ubuf[POOL_HALO:POOL_HALO + tm, :] = u
    t1 = (i * tm + 1 + lax.broadcasted_iota(I32, (tm, C_GROUP_DIM), 0)).astype(F32)
    for gi, w in enumerate(POOL_WINDOWS):
        c0 = gi * C_GROUP_DIM
        s = ubuf[:, c0:c0 + C_GROUP_DIM]
        span = 1
        while span < w:
            s = s + pltpu.roll(s, span, axis=0)
            span *= 2
        assert span == w < POOL_HALO + 1
        s = s[POOL_HALO:, :]
        pooled = s / jnp.minimum(t1, float(w)) - u[:, c0:c0 + C_GROUP_DIM]
        mixed = (jnp.dot(pooled.astype(BF16), pw_ref[gi], preferred_element_type=F32)
                 + pb_ref[:, c0:c0 + C_GROUP_DIM])
        pooled_ref[:, c0:c0 + C_GROUP_DIM] = (
            mixed * ps_ref[:, c0:c0 + C_GROUP_DIM]).astype(BF16)
    ubuf[0:POOL_HALO, :] = ubuf[tm:tm + POOL_HALO, :]

    lane = lax.broadcasted_iota(I32, (tm, HEAD_PAD), 1)
    tail = jnp.where(lane == ALIBI_COL, 1.0, 0.0)
    qs = qr * (HEAD_DIM ** -0.5)
    for pair in range(D_HEADS // 2):
        both = qs[:, pair * HEAD_PAD:(pair + 1) * HEAD_PAD]
        qrow_ref[2 * pair] = jnp.where(lane < HEAD_DIM, both, tail)
        qrow_ref[2 * pair + 1] = jnp.where(
            lane < HEAD_DIM, pltpu.roll(both, HEAD_DIM, axis=1), tail)
    per_tile = tm // MOBA_BLOCK
    kvb = kvt.astype(BF16)
    n_alibi = SLAB_K_ROWS - HEAD_DIM
    first = lax.broadcasted_iota(I32, (n_alibi, MOBA_BLOCK), 0) == 0
    koff = lax.broadcasted_iota(I32, (n_alibi, MOBA_BLOCK), 1).astype(F32)
    for hh in range(D_HEADS):
        alibi = jnp.where(first, koff * (2.0 ** -(hh + 1)), 0.0).astype(BF16)
        for sb in range(per_tile):
            keys = slice(sb * MOBA_BLOCK, (sb + 1) * MOBA_BLOCK)
            kv_ref[hh, sb, 0:HEAD_DIM, :] = kvb[hh * HEAD_DIM:(hh + 1) * HEAD_DIM, keys]
            kv_ref[hh, sb, HEAD_DIM:SLAB_K_ROWS, :] = alibi
            kv_ref[hh, sb, SLAB_K_ROWS:SLAB_ROWS, :] = kvb[
                D_WIDTH + hh * HEAD_DIM:D_WIDTH + (hh + 1) * HEAD_DIM, keys]

    for sb in range(per_tile):
        blk = h[sb * MOBA_BLOCK:(sb + 1) * MOBA_BLOCK, :].astype(F32)
        hmean[pl.ds(i * per_tile + sb, 1), :] = jnp.mean(blk, axis=0, keepdims=True)

    @pl.when(i == pl.num_programs(0) - 1)
    def _():
        hi, lo = _split(hmean[...])
        wkt = wkvt_ref[0:D_WIDTH, :]
        km_ref[...] = (lax.dot_general(hi, wkt, _NT, preferred_element_type=F32)
                       + lax.dot_general(lo, wkt, _NT, preferred_element_type=F32))


def _pm_proj_call(x, mod, norm_g, wu, wq, wkvt, pw, pb, ps, *, layer, tm=512):
    return pl.pallas_call(
        functools.partial(_pm_proj_kernel, tm=tm),
        grid=(SEQ // tm,),
        in_specs=[
            pl.BlockSpec((tm, D_MODEL), lambda i: (i, 0)),
            _resident((None, None, 3, D_MODEL), lambda i: (layer, 1, 0, 0)),
            _resident((None, 1, D_MODEL), lambda i: (layer, 0, 0)),
            _full((D_MODEL, C_WIDTH)),
            _full((D_MODEL, D_WIDTH)),
            _full((2 * D_WIDTH, D_MODEL)),
            _full((len(POOL_WINDOWS), C_GROUP_DIM, C_GROUP_DIM)),
            _full((1, C_WIDTH)),
            _full((1, C_WIDTH)),
        ],
        out_specs=[
            pl.BlockSpec((tm, C_WIDTH), lambda i: (i, 0)),
            pl.BlockSpec((D_HEADS, tm, HEAD_PAD), lambda i: (0, i, 0)),
            pl.BlockSpec((D_HEADS, tm // MOBA_BLOCK, SLAB_ROWS, MOBA_BLOCK),
                         lambda i: (0, i, 0, 0)),
            pl.BlockSpec((N_BLOCKS, D_WIDTH), lambda i: (0, 0)),
        ],
        out_shape=[
            jax.ShapeDtypeStruct((SEQ, C_WIDTH), BF16),
            jax.ShapeDtypeStruct((D_HEADS, SEQ, HEAD_PAD), F32),
            jax.ShapeDtypeStruct((D_HEADS, N_BLOCKS, SLAB_ROWS, MOBA_BLOCK), BF16),
            jax.ShapeDtypeStruct((N_BLOCKS, D_WIDTH), F32),
        ],
        scratch_shapes=[pltpu.VMEM((POOL_HALO + tm, C_WIDTH), F32),
                        pltpu.VMEM((N_BLOCKS, D_MODEL), F32)],
        compiler_params=_params(1),
        name="pool_qkv_proj",
    )(x, mod, norm_g, wu, wq, wkvt, pw, pb, ps)


def _block_rows(qs, slabs, mask, lse_add):
    n = qs[0].shape[0]
    kpad = jnp.zeros((HEAD_PAD - SLAB_K_ROWS, MOBA_BLOCK), BF16)
    s = jnp.concatenate(
        [jnp.dot(q, jnp.concatenate([slab[0:SLAB_K_ROWS, :], kpad], axis=0),
                 preferred_element_type=F32)
         for q, slab in zip(qs, slabs)], axis=0)
    if mask is not None:
        s = jnp.where(mask, s, NEG_BIG)
    m = jnp.max(s, axis=1, keepdims=True)
    p = jnp.exp(s - m).astype(BF16)
    vpad = jnp.zeros((HEAD_PAD - HEAD_DIM, MOBA_BLOCK), BF16)
    ones = jnp.ones((HEAD_PAD, MOBA_BLOCK), BF16)
    ot = jnp.concatenate(
        [lax.dot_general(p[u * n:(u + 1) * n],
                         jnp.concatenate([slab[SLAB_K_ROWS:SLAB_ROWS, :], vpad, ones], axis=0),
                         _NT, preferred_element_type=F32)
         for u, slab in enumerate(slabs)], axis=0)
    o, tot = ot[:, 0:HEAD_PAD], ot[:, HEAD_PAD:2 * HEAD_PAD]
    lse = m + jnp.log(tot)
    if lse_add is not None:
        lse = jnp.concatenate(
            [lse[u * n:(u + 1) * n] + add for u, add in enumerate(lse_add)], axis=0)
    low = lax.broadcasted_iota(I32, o.shape, 1) < HEAD_DIM
    return jnp.where(low, o / tot, lse)


def _onehot_pairs(sel_rows):
    blk = lax.broadcasted_iota(I32, (N_BLOCKS, MOBA_BLOCK), 0)
    return jnp.concatenate(
        [jnp.where(blk == sel_rows[s], 1.0, 0.0) for s in range(MOBA_TOPK)], axis=1)


def _gate_own_kernel(anchor_ref, qrow_ref, kv_ref, km_ref, tri_ref,
                     sel_ref, rank_ref, cnt_ref, own_ref, *, heads, blocks):
    del anchor_ref
    n_pair = MOBA_TOPK * MOBA_BLOCK
    tiles = [(bb, a) for bb in range(blocks) for a in range(heads)]
    rows_of = lambda bb: slice(bb * MOBA_BLOCK, (bb + 1) * MOBA_BLOCK)
    qs = [qrow_ref[a, rows_of(bb), :].astype(BF16) for bb, a in tiles]

    kms = [_split(km_ref[a]) for a in range(heads)]
    gates = [lax.dot_general(kms[a][0], q, _NT, preferred_element_type=F32)
             + lax.dot_general(kms[a][1], q, _NT, preferred_element_type=F32)
             for (bb, a), q in zip(tiles, qs)]
    width = len(tiles) * MOBA_BLOCK
    blk = lax.broadcasted_iota(I32, (N_BLOCKS, width), 0)
    lane = lax.broadcasted_iota(I32, (1, width), 1)
    own = pl.program_id(0) * blocks + lane // (heads * MOBA_BLOCK)
    g = jnp.where(blk < own, jnp.concatenate(gates, axis=1), -jnp.inf)
    slots = []
    for s in range(MOBA_TOPK):
        best = jnp.max(g, axis=0, keepdims=True)
        idx = jnp.min(jnp.where(g == best, blk, N_BLOCKS), axis=0, keepdims=True)
        g = jnp.where(blk == idx, -jnp.inf, g)
        slots.append(jnp.where(s < own, idx, N_BLOCKS))

    onehots = []
    for t, (bb, a) in enumerate(tiles):
        sel_rows = [slot[:, t * MOBA_BLOCK:(t + 1) * MOBA_BLOCK] for slot in slots]
        sel_ref[a, :, rows_of(bb)] = jnp.concatenate(sel_rows, axis=0)
        onehots.append(_onehot_pairs(sel_rows))
    onehot = jnp.concatenate(onehots, axis=0)
    oh = onehot.astype(BF16)
    before = jnp.dot(oh, tri_ref[...], preferred_element_type=F32)
    hit = onehot * before
    ones = jnp.ones((8, n_pair), BF16)
    cnt = lax.dot_general(ones, oh, _NT, preferred_element_type=F32)
    for t, (bb, a) in enumerate(tiles):
        rank = jnp.sum(hit[t * N_BLOCKS:(t + 1) * N_BLOCKS], axis=0, keepdims=True).astype(I32)
        rank_ref[a, :, rows_of(bb)] = jnp.concatenate(
            [rank[:, s * MOBA_BLOCK:(s + 1) * MOBA_BLOCK] for s in range(MOBA_TOPK)], axis=0)
        cnt_ref[a, bb] = cnt[0:1, t * N_BLOCKS:(t + 1) * N_BLOCKS].astype(I32)

    qi = lax.broadcasted_iota(I32, (width, MOBA_BLOCK), 0) % MOBA_BLOCK
    ki = lax.broadcasted_iota(I32, (width, MOBA_BLOCK), 1)
    rows = _block_rows(qs, [kv_ref.at[a, bb] for bb, a in tiles], ki <= qi, None)
    for t, (bb, a) in enumerate(tiles):
        own_ref[a, rows_of(bb), :] = rows[t * MOBA_BLOCK:(t + 1) * MOBA_BLOCK]


def _gate_own_call(anchor, qrows, kv, kmean_h, tri, *, split, blocks=GATE_BLOCKS_PER_STEP):
    heads = SPLIT_HEADS
    rows = blocks * MOBA_BLOCK
    return pl.pallas_call(
        functools.partial(_gate_own_kernel, heads=heads, blocks=blocks),
        grid=(N_BLOCKS // blocks,),
        in_specs=[
            pl.BlockSpec(memory_space=pltpu.SMEM),
            pl.BlockSpec((heads, rows, HEAD_PAD), lambda b: (split, b, 0)),
            pl.BlockSpec((heads, blocks, SLAB_ROWS, MOBA_BLOCK), lambda b: (split, b, 0, 0)),
            pl.BlockSpec((heads, N_BLOCKS, HEAD_PAD), lambda b: (split, 0, 0)),
            _full((MOBA_TOPK * MOBA_BLOCK, MOBA_TOPK * MOBA_BLOCK)),
        ],
        out_specs=[
            pl.BlockSpec((heads, MOBA_TOPK, rows), lambda b: (0, 0, b)),
            pl.BlockSpec((heads, MOBA_TOPK, rows), lambda b: (0, 0, b)),
            pl.BlockSpec((heads, blocks, 1, N_BLOCKS), lambda b: (0, b, 0, 0)),
            pl.BlockSpec((heads, rows, HEAD_PAD), lambda b: (0, b, 0)),
        ],
        out_shape=[
            jax.ShapeDtypeStruct((heads, MOBA_TOPK, SEQ), I32),
            jax.ShapeDtypeStruct((heads, MOBA_TOPK, SEQ), I32),
            jax.ShapeDtypeStruct((heads, N_BLOCKS, 1, N_BLOCKS), I32),
            jax.ShapeDtypeStruct((heads, SEQ, HEAD_PAD), F32),
        ],
        compiler_params=_params(1),
        name=f"moba_gate_own_{split}",
    )(anchor, qrows, kv, kmean_h, tri)


def _route_pos_kernel(sel_ref, rank_ref, base_ref, pos_ref, *, blocks):
    lane = lax.broadcasted_iota(I32, (1, MOBA_BLOCK), 1)
    for bb in range(blocks):
        qs = slice(bb * MOBA_BLOCK, (bb + 1) * MOBA_BLOCK)
        rows = [[] for _ in range(MOBA_TOPK)]
        for a in range(SPLIT_HEADS):
            sel = sel_ref[a, :, qs]
            onehot = _onehot_pairs([sel[s:s + 1, :] for s in range(MOBA_TOPK)]).astype(BF16)
            dig = jnp.dot(base_ref[a, bb].astype(BF16), onehot,
                          preferred_element_type=F32)
            base = (dig[0:1] * float(DIGIT * DIGIT) + dig[1:2] * float(DIGIT)
                    + dig[2:3]).astype(I32)
            for s in range(MOBA_TOPK):
                p = base[:, s * MOBA_BLOCK:(s + 1) * MOBA_BLOCK] + rank_ref[a, s:s + 1, qs]
                rows[s].append(
                    jnp.where(sel[s:s + 1, :] < N_BLOCKS, p, TRASH_ROW0 + lane % SC_WINDOW))
        for s in range(MOBA_TOPK):
            pos_ref[s, :, qs] = jnp.concatenate(rows[s], axis=0)


def _route_pos_call(sel, rank, base_digits, *, blocks=8):
    width = blocks * MOBA_BLOCK
    return pl.pallas_call(
        functools.partial(_route_pos_kernel, blocks=blocks),
        grid=(N_BLOCKS // blocks,),
        in_specs=[
            pl.BlockSpec((SPLIT_HEADS, MOBA_TOPK, width), lambda b: (0, 0, b)),
            pl.BlockSpec((SPLIT_HEADS, MOBA_TOPK, width), lambda b: (0, 0, b)),
            pl.BlockSpec((SPLIT_HEADS, blocks, 8, N_BLOCKS), lambda b: (0, b, 0, 0)),
        ],
        out_specs=pl.BlockSpec((MOBA_TOPK, SPLIT_HEADS, width), lambda b: (0, 0, b)),
        out_shape=jax.ShapeDtypeStruct((MOBA_TOPK, SPLIT_HEADS, SEQ), I32),
        compiler_params=_params(1),
        name="moba_route_pos",
    )(sel, rank, base_digits)


def _route_tables(cnt, split):
    cnt = cnt.reshape(SPLIT_HEADS, N_BLOCKS, N_BLOCKS)
    tiles = (cnt.sum(axis=1) + ROUTE_TILE - 1) // ROUTE_TILE
    tiles_flat = tiles.reshape(N_GROUPS)
    tile0 = jnp.cumsum(tiles_flat) - tiles_flat
    n_tiles = tiles_flat.sum().astype(I32)
    within = jnp.cumsum(cnt, axis=1) - cnt
    base = tile0.reshape(SPLIT_HEADS, 1, N_BLOCKS) * ROUTE_TILE + within
    digits = jnp.stack([base // (DIGIT * DIGIT), (base // DIGIT) % DIGIT, base % DIGIT],
                       axis=2).astype(F32)
    digits = jnp.pad(digits, ((0, 0), (0, 0), (0, 8 - 3), (0, 0)))
    t = jnp.arange(MAX_TILES, dtype=I32)
    ended = (tile0 + tiles_flat)[None, :] <= t[:, None]
    group_of_tile = jnp.minimum(ended.sum(axis=1).astype(I32), N_GROUPS - 1)
    tile_h = group_of_tile // N_BLOCKS
    head = split * SPLIT_HEADS + tile_h
    tile_step = lax.bitcast_convert_type((127 + 7 - head) << 23, F32)
    return digits, tile_h, group_of_tile % N_BLOCKS, n_tiles.reshape(1), tile_step


def _sc_mesh():
    return plsc.VectorSubcoreMesh(core_axis_name="core", subcore_axis_name="subcore")


def _dispatch_call(rows, pos_by_slot, *, split):
    n_rows = SPLIT_HEADS * SEQ
    first_window = split * n_rows // SC_WINDOW

    @functools.partial(
        pl.kernel, mesh=_sc_mesh(), scratch_types=[],
        out_type=jax.ShapeDtypeStruct((ROUTE_ROWS, HEAD_PAD), F32))
    def dispatch(x_hbm, i0_hbm, i1_hbm, i2_hbm, o_hbm):
        def body(x_vmem, i0_vmem, i1_vmem, i2_vmem):
            pltpu.sync_copy(x_vmem, o_hbm.at[i0_vmem.at[0]])
            pltpu.sync_copy(x_vmem, o_hbm.at[i1_vmem.at[0]])
            pltpu.sync_copy(x_vmem, o_hbm.at[i2_vmem.at[0]])

        idx_spec = pl.BlockSpec((1, SC_WINDOW), lambda i: (0, i))
        pltpu.emit_pipeline(
            body,
            grid=(n_rows // SC_WINDOW,),
            in_specs=[pl.BlockSpec((SC_WINDOW, HEAD_PAD), lambda i: (first_window + i, 0)),
                      idx_spec, idx_spec, idx_spec],
            out_specs=[],
            core_axis_name=("core", "subcore"),
            dimension_semantics=(pltpu.PARALLEL,),
        )(x_hbm, i0_hbm, i1_hbm, i2_hbm)

    return dispatch(rows, *pos_by_slot)


def _collect_call(table, idx):
    n_rows = idx.shape[1]

    @functools.partial(
        pl.kernel, mesh=_sc_mesh(), scratch_types=[],
        out_type=jax.ShapeDtypeStruct((n_rows, HEAD_PAD), F32))
    def collect(x_hbm, i_hbm, o_hbm):
        def body(i_vmem, o_vmem):
            pltpu.sync_copy(x_hbm.at[i_vmem.at[0]], o_vmem)

        pltpu.emit_pipeline(
            body,
            grid=(n_rows // SC_WINDOW,),
            in_specs=[pl.BlockSpec((1, SC_WINDOW), lambda i: (0, i))],
            out_specs=[pl.BlockSpec((SC_WINDOW, HEAD_PAD), lambda i: (i, 0))],
            core_axis_name=("core", "subcore"),
            dimension_semantics=(pltpu.PARALLEL,),
        )(i_hbm, o_hbm)

    return collect(table, idx)


def _routed_attn_kernel(th_ref, tj_ref, nt_ref, step_ref, q_ref, *refs, tiles):
    kv_refs, o_ref = refs[0:tiles], refs[tiles]
    t0 = pl.program_id(0) * tiles

    @pl.when(t0 < nt_ref[0])
    def _():
        qs = [q_ref[u * ROUTE_TILE:(u + 1) * ROUTE_TILE, :].astype(BF16) for u in range(tiles)]
        adds = [step_ref[t0 + u] * tj_ref[t0 + u].astype(F32) for u in range(tiles)]
        o_ref[...] = _block_rows(qs, kv_refs, None, adds)

    @pl.when(t0 == MAX_TILES)
    def _():
        low = lax.broadcasted_iota(I32, o_ref.shape, 1) < HEAD_DIM
        o_ref[...] = jnp.where(low, 0.0, NEG_BIG)


def _routed_attn_call(tile_h, tile_j, n_tiles, tile_step, routed_q, kv,
                      *, split, tiles=ROUTE_TILES_PER_STEP):
    n_steps = MAX_TILES // tiles + 1
    assert TRASH_ROW0 == (n_steps - 1) * tiles * ROUTE_TILE
    live = lambda s, th, tj, nt: (jnp.minimum(s, (nt[0] + tiles - 1) // tiles), 0)
    live_or_last = lambda s, th, tj, nt: (
        jnp.where(s == n_steps - 1, s, jnp.minimum(s, (nt[0] + tiles - 1) // tiles)), 0)
    head0 = split * SPLIT_HEADS

    def kv_spec(u):
        def slab(s, th, tj, nt):
            t = jnp.minimum(s, n_steps - 2) * tiles + u
            return head0 + th[t], tj[t], 0, 0
        return pl.BlockSpec((None, None, SLAB_ROWS, MOBA_BLOCK), slab)

    grid_spec = pltpu.PrefetchScalarGridSpec(
        num_scalar_prefetch=3,
        grid=(n_steps,),
        in_specs=([pl.BlockSpec(memory_space=pltpu.SMEM),
                   pl.BlockSpec((tiles * ROUTE_TILE, HEAD_PAD), live)]
                  + [kv_spec(u) for u in range(tiles)]),
        out_specs=pl.BlockSpec((tiles * ROUTE_TILE, HEAD_PAD), live_or_last),
    )
    return pl.pallas_call(
        functools.partial(_routed_attn_kernel, tiles=tiles),
        grid_spec=grid_spec,
        out_shape=jax.ShapeDtypeStruct((ROUTE_ROWS, HEAD_PAD), F32),
        compiler_params=_params(1),
        name=f"moba_routed_attn_{split}",
    )(tile_h, tile_j, n_tiles, tile_step, routed_q, *([kv] * tiles))


def _pm_out_kernel(slope_ref, x_ref, mod_ref, pooled_ref, *refs, blocks):
    own_refs, got_refs = refs[0:N_SPLITS], refs[N_SPLITS:2 * N_SPLITS]
    wp_ref, wa_ref, o_ref = refs[2 * N_SPLITS:]
    low = lax.broadcasted_iota(I32, (MOBA_BLOCK, HEAD_PAD), 1) < HEAD_DIM

    def lse_of(part):
        return jnp.where(low, pltpu.roll(part, HEAD_DIM, axis=1), part)

    y = jnp.dot(pooled_ref[...], wp_ref[...], preferred_element_type=F32)
    for hh in range(D_HEADS):
        own_ref, got_ref = own_refs[hh // SPLIT_HEADS], got_refs[hh // SPLIT_HEADS]
        a = hh % SPLIT_HEADS
        atts = []
        for bb in range(blocks):
            own = pl.program_id(0) * blocks + bb
            rows = slice(bb * MOBA_BLOCK, (bb + 1) * MOBA_BLOCK)
            own_shift = slope_ref[hh] * (own * MOBA_BLOCK).astype(F32)
            parts = [own_ref[a, rows, :]] + [got_ref[s, a, rows, :] for s in range(MOBA_TOPK)]
            lses = [lse_of(parts[0]) + own_shift] + [lse_of(part) for part in parts[1:]]
            top = functools.reduce(jnp.maximum, lses)
            num = jnp.zeros((MOBA_BLOCK, HEAD_PAD), F32)
            den = jnp.zeros((MOBA_BLOCK, HEAD_PAD), F32)
            for part, ls in zip(parts, lses):
                w = jnp.exp(ls - top)
                num = num + w * part
                den = den + w
            atts.append((num / den).astype(BF16))
        y = y + jnp.dot(jnp.concatenate(atts, axis=0), wa_ref[hh], preferred_element_type=F32)
    o_ref[...] = x_ref[...] + mod_ref[2:3, :] * y


def _pm_out_call(slopes, x, mod, pooled, own_parts, got_parts, wp, wa, *, layer, blocks=2):
    tm = blocks * MOBA_BLOCK
    return pl.pallas_call(
        functools.partial(_pm_out_kernel, blocks=blocks),
        grid=(SEQ // tm,),
        in_specs=[
            pl.BlockSpec(memory_space=pltpu.SMEM),
            pl.BlockSpec((tm, D_MODEL), lambda i: (i, 0)),
            _resident((None, None, 3, D_MODEL), lambda i: (layer, 1, 0, 0)),
            pl.BlockSpec((tm, C_WIDTH), lambda i: (i, 0)),
        ] + [pl.BlockSpec((SPLIT_HEADS, tm, HEAD_PAD), lambda i: (0, i, 0))] * N_SPLITS
        + [pl.BlockSpec((MOBA_TOPK, SPLIT_HEADS, tm, HEAD_PAD), lambda i: (0, 0, i, 0))] * N_SPLITS
        + [
            _full((C_WIDTH, D_MODEL)),
            _full((D_HEADS, HEAD_PAD, D_MODEL)),
        ],
        out_specs=pl.BlockSpec((tm, D_MODEL), lambda i: (i, 0)),
        out_shape=jax.ShapeDtypeStruct((SEQ, D_MODEL), F32),
        compiler_params=_params(1),
        name="pool_moba_out",
    )(slopes, x, mod, pooled, *own_parts, *got_parts, wp, wa)


def _head_pad(w, axis):
    shape = list(w.shape)
    shape[axis:axis + 1] = [D_HEADS, HEAD_DIM]
    pad = [(0, 0)] * len(shape)
    pad[axis + 1] = (0, HEAD_PAD - HEAD_DIM)
    out = jnp.pad(w.reshape(shape), pad)
    shape[axis:axis + 2] = [D_HEADS * HEAD_PAD]
    return out.reshape(shape)


def kernel(x, c, ada_w, ada_b, ffn_norm, ffn_w_gate, ffn_w_up, ffn_w_down, mix_norm, conv_w_in,
           conv_a_w, conv_a_b, conv_a_ln_g, conv_a_ln_b, conv_b_w, conv_w_out, pm_w_in, pool_w,
           pool_b, pool_scale, pm_w_out, final_norm):
    assert x.shape == (1, SEQ, D_MODEL) and c.shape == (1, D_MODEL)
    xs = x.reshape(SEQ, D_MODEL)
    mod = _mod_call(c, ada_w, ada_b)

    ffn_g = ffn_norm.reshape(DEPTH, 2, 1, D_MODEL)
    mix_g = mix_norm.reshape(DEPTH, 1, D_MODEL)
    fin = final_norm.reshape(1, D_MODEL)
    ffn_stacks = (ffn_w_gate, ffn_w_up, ffn_w_down)

    def ffn(xs, weights, layer, sub):
        last = (layer, sub) == (DEPTH - 1, 1)
        after = None if last else ((layer + sub, 1 - sub), *ffn_stacks)
        return _ffn_call(xs, mod, ffn_g, weights, fin, after, layer=layer, sub=sub, final=last)

    xs, ffn_w = ffn(xs, tuple(w[0, 0].astype(BF16) for w in ffn_stacks), layer=0, sub=0)
    grp = np.arange(A_WIDTH) // (A_WIDTH // A_GROUPS)
    seg = jnp.asarray((grp[:, None] == grp[None, :]) / (A_WIDTH // A_GROUPS), dtype=BF16)
    xs = _conv_mix_call(
        xs, mod, mix_g, conv_w_in[0].astype(BF16), conv_a_w[0], conv_a_b[0].reshape(1, A_WIDTH),
        conv_a_ln_g[0].reshape(1, A_WIDTH), conv_a_ln_b[0].reshape(1, A_WIDTH), conv_b_w[0],
        conv_w_out[0].astype(BF16), seg, layer=0)
    xs, ffn_w = ffn(xs, ffn_w, layer=0, sub=1)

    xs, ffn_w = ffn(xs, ffn_w, layer=1, sub=0)
    w_in = pm_w_in[0].astype(BF16)
    w_u = w_in[:, 0:C_WIDTH]
    w_q = w_in[:, C_WIDTH:C_WIDTH + D_WIDTH]
    w_kvt = w_in[:, C_WIDTH + D_WIDTH:C_WIDTH + 3 * D_WIDTH].T
    pooled, qrows, kv, kmean = _pm_proj_call(
        xs, mod, mix_g, w_u, w_q, w_kvt, pool_w[0].astype(BF16),
        pool_b[0].reshape(1, C_WIDTH), pool_scale[0].reshape(1, C_WIDTH), layer=1)
    kmean_h = _head_pad(kmean, 1).reshape(N_BLOCKS, D_HEADS, HEAD_PAD).transpose(1, 0, 2)

    n_pair = MOBA_TOPK * MOBA_BLOCK
    tri = jnp.asarray(np.arange(n_pair)[:, None] < np.arange(n_pair)[None, :], dtype=BF16)
    q_flat = qrows.reshape(D_HEADS * SEQ, HEAD_PAD)
    own_parts, got_parts = [], []
    anchor = jnp.zeros((1,), I32)
    for split in range(N_SPLITS):
        sel, rank, cnt, own_part = _gate_own_call(anchor, qrows, kv, kmean_h, tri, split=split)
        base_digits, tile_h, tile_j, n_tiles, tile_step = _route_tables(cnt, split)
        pos = _route_pos_call(sel, rank, base_digits)
        anchor = pos[0, 0, 0:1]
        pos_by_slot = [pos[s].reshape(1, SPLIT_HEADS * SEQ) for s in range(MOBA_TOPK)]
        routed_q = _dispatch_call(q_flat, pos_by_slot, split=split)
        routed_parts = _routed_attn_call(tile_h, tile_j, n_tiles, tile_step, routed_q, kv,
                                         split=split)
        got = _collect_call(routed_parts, pos.reshape(1, N_PAIRS))
        own_parts.append(own_part)
        got_parts.append(got.reshape(MOBA_TOPK, SPLIT_HEADS, SEQ, HEAD_PAD))

    slopes = jnp.asarray(2.0 ** (-8.0 * np.arange(1, D_HEADS + 1) / D_HEADS), dtype=F32)
    w_out = pm_w_out[0].astype(BF16)
    w_att = _head_pad(w_out[C_WIDTH:], 0).reshape(D_HEADS, HEAD_PAD, D_MODEL)
    xs = _pm_out_call(slopes, xs, mod, pooled, own_parts, got_parts, w_out[0:C_WIDTH], w_att,
                      layer=1)
    xs, _ = ffn(xs, ffn_w, layer=1, sub=1)
    return xs.reshape(1, SEQ, D_MODEL)
```

```python
import functools

import numpy as np
import jax
import jax.numpy as jnp
from jax import lax
from jax.experimental import pallas as pl
from jax.experimental.pallas import tpu as pltpu
from jax.experimental.pallas import tpu_sc as plsc

D_MODEL = 1024
SEQ = 16384
DEPTH = 2
N_SUBLAYERS = 3
D_FF = 2816
EPS = 1e-6

A_WIDTH = 512
A_GROUPS = 8
A_CONV = 31
B_WIDTH = 512
B_CONV = 3

POOL_WINDOWS = (2, 4, 8, 16)
C_WIDTH = 512
C_GROUP_DIM = 128
D_HEADS = 8
HEAD_DIM = 64
D_WIDTH = 512
MOBA_BLOCK = 256
MOBA_TOPK = 3
N_BLOCKS = SEQ // MOBA_BLOCK

V7X_SUBLANES = 8
V7X_BF16_SUBLANES = 16
V7X_VMEM_LIMIT_BYTES = 56 * 1024 * 1024

A_HALO = 32
B_HALO = 8
POOL_HALO = 16
NEG_BIG = -1e30

HEAD_PAD = 128
ALIBI_COL = HEAD_DIM
SLAB_K_ROWS = HEAD_DIM + V7X_BF16_SUBLANES
SLAB_ROWS = SLAB_K_ROWS + HEAD_DIM

SPLIT_HEADS = 4
N_SPLITS = D_HEADS // SPLIT_HEADS
ROUTE_TILE = 256
ROUTE_TILES_PER_STEP = 32
GATE_BLOCKS_PER_STEP = 4
N_PAIRS = SPLIT_HEADS * SEQ * MOBA_TOPK
N_GROUPS = SPLIT_HEADS * N_BLOCKS
MAX_TILES = N_PAIRS // ROUTE_TILE + N_GROUPS
TRASH_ROW0 = MAX_TILES * ROUTE_TILE
ROUTE_ROWS = (MAX_TILES + ROUTE_TILES_PER_STEP) * ROUTE_TILE
SC_WINDOW = 128
DIGIT = 128
assert MAX_TILES % ROUTE_TILES_PER_STEP == 0 and N_BLOCKS % GATE_BLOCKS_PER_STEP == 0
assert ROUTE_ROWS <= DIGIT ** 3 and MOBA_BLOCK <= ROUTE_TILE

BF16 = jnp.bfloat16
F32 = jnp.float32
I32 = jnp.int32


def _params(n_axes):
    return pltpu.CompilerParams(
        dimension_semantics=("arbitrary",) * n_axes,
        vmem_limit_bytes=V7X_VMEM_LIMIT_BYTES)


def _resident(block_shape, index_map):
    return pl.BlockSpec(block_shape, index_map, pipeline_mode=pl.Buffered(1))


def _full(shape):
    return _resident(shape, lambda *_: (0,) * len(shape))


def _sigmoid(v):
    return 1.0 / (1.0 + jnp.exp(-v))


def _norm_mod(x, g, mod):
    r = lax.rsqrt(jnp.mean(x * x, axis=-1, keepdims=True) + EPS)
    return (x * r) * (g * (1.0 + mod[1:2])) + mod[0:1]


def _split(v):
    hi = v.astype(BF16)
    return hi, (v - hi.astype(F32)).astype(BF16)


def _split_dot(v, w):
    hi, lo = _split(v)
    return (jnp.dot(hi, w, preferred_element_type=F32)
            + jnp.dot(lo, w, preferred_element_type=F32))


_NT = (((1,), (1,)), ((), ()))


def _mod_kernel(c_ref, w_ref, b_ref, o_ref):
    c = c_ref[...]
    cond = c * _sigmoid(c)
    o_ref[0] = jnp.sum(w_ref[0] * cond, axis=0, keepdims=True) + b_ref[0]


def _mod_call(c, ada_w, ada_b):
    n_out = N_SUBLAYERS * 3 * D_MODEL
    tn = D_MODEL
    out = pl.pallas_call(
        _mod_kernel,
        grid=(DEPTH, n_out // tn),
        in_specs=[
            pl.BlockSpec((D_MODEL, 1), lambda l, j: (0, 0)),
            pl.BlockSpec((1, D_MODEL, tn), lambda l, j: (l, 0, j)),
            pl.BlockSpec((1, 1, tn), lambda l, j: (l, 0, j)),
        ],
        out_specs=pl.BlockSpec((1, 1, tn), lambda l, j: (l, 0, j)),
        out_shape=jax.ShapeDtypeStruct((DEPTH, 1, n_out), F32),
        compiler_params=_params(2),
        name="adaln_mod",
    )(c.reshape(D_MODEL, 1), ada_w, ada_b.reshape(DEPTH, 1, n_out))
    return out.reshape(DEPTH, N_SUBLAYERS, 3, D_MODEL)


def _ffn_kernel(x_ref, mod_ref, g_ref, wg_ref, wu_ref, wd_ref, fin_ref, *refs, final):
    o_ref = refs[len(refs) // 2]
    for src_ref, dst_ref in zip(refs[:len(refs) // 2], refs[len(refs) // 2 + 1:]):
        dst_ref[...] = src_ref[...].astype(BF16)
    x = x_ref[...]
    mod = mod_ref[...]
    h = _norm_mod(x, g_ref[...], mod).astype(BF16)
    gate = jnp.dot(h, wg_ref[...], preferred_element_type=F32)
    up = jnp.dot(h, wu_ref[...], preferred_element_type=F32)
    act = (gate * _sigmoid(gate) * up).astype(BF16)
    y = jnp.dot(act, wd_ref[...], preferred_element_type=F32)
    xn = x + (0.5 * mod[2:3]) * y
    if final:
        r = lax.rsqrt(jnp.mean(xn * xn, axis=-1, keepdims=True) + EPS)
        xn = xn * r * fin_ref[...]
    o_ref[...] = xn


def _ffn_call(x, mod, norm_g, weights, fin, to_cast, *, layer, sub, final, tm=512):
    mod_sub = 0 if sub == 0 else 2
    steps = SEQ // tm
    in_specs = [
        pl.BlockSpec((tm, D_MODEL), lambda i: (i, 0)),
        _resident((None, None, 3, D_MODEL), lambda i: (layer, mod_sub, 0, 0)),
        _resident((None, None, 1, D_MODEL), lambda i: (layer, sub, 0, 0)),
        _full((D_MODEL, D_FF)),
        _full((D_MODEL, D_FF)),
        _full((D_FF, D_MODEL)),
        _full((1, D_MODEL)),
    ]
    out_specs = [pl.BlockSpec((tm, D_MODEL), lambda i: (i, 0))]
    out_shape = [jax.ShapeDtypeStruct((SEQ, D_MODEL), F32)]
    operands = [x, mod, norm_g, *weights, fin]
    for lead, w in to_cast:
        n_rows, n_cols = w.shape[len(lead):]
        share = 1 if n_rows % (steps * V7X_BF16_SUBLANES) == 0 else 2
        rows = n_rows * share // steps
        assert n_rows * share % steps == 0 and rows % V7X_BF16_SUBLANES == 0
        in_specs.append(pl.BlockSpec(
            (None,) * len(lead) + (rows, n_cols),
            lambda i, lead=lead, share=share: (*lead, i // share, 0)))
        out_specs.append(pl.BlockSpec((rows, n_cols), lambda i, share=share: (i // share, 0)))
        out_shape.append(jax.ShapeDtypeStruct((n_rows, n_cols), BF16))
        operands.append(w)
    out = pl.pallas_call(
        functools.partial(_ffn_kernel, final=final),
        grid=(steps,),
        in_specs=in_specs,
        out_specs=out_specs,
        out_shape=out_shape,
        compiler_params=_params(1),
        name=f"ffn_l{layer}s{sub}",
    )(*operands)
    return out[0], tuple(out[1:])


def _conv_mix_kernel(x_ref, mod_ref, g_ref, win_ref, caw_ref, cab_ref, lng_ref, lnb_ref,
                     cbw_ref, wout_ref, seg_ref, o_ref, abuf, bbuf, *, tm):
    i = pl.program_id(0)

    @pl.when(i == 0)
    def _():
        abuf[0:A_HALO, :] = jnp.zeros((A_HALO, A_WIDTH), F32)
        bbuf[0:B_HALO, :] = jnp.zeros((B_HALO, B_WIDTH), F32)

    x = x_ref[...]
    mod = mod_ref[...]
    h = _norm_mod(x, g_ref[...], mod).astype(BF16)
    z = jnp.dot(h, win_ref[...], preferred_element_type=F32)

    a = z[:, 0:A_WIDTH] * _sigmoid(z[:, A_WIDTH:2 * A_WIDTH])
    abuf[A_HALO:A_HALO + tm, :] = a
    acc = jnp.zeros((tm, A_WIDTH), F32) + cab_ref[...]
    ext = tm + V7X_SUBLANES
    for r in range(V7X_SUBLANES):
        part = None
        for q in range((A_CONV - 1 - r) // V7X_SUBLANES + 1):
            k = A_CONV - 1 - (V7X_SUBLANES * q + r)
            term = caw_ref[k:k + 1, :] * abuf[pl.ds(A_HALO - V7X_SUBLANES * (q + 1), ext), :]
            part = term if part is None else part + term
        if r > 0:
            part = pltpu.roll(part, r, axis=0)
        acc = acc + part[V7X_SUBLANES:, :]
    abuf[0:A_HALO, :] = abuf[tm:tm + A_HALO, :]
    seg = seg_ref[...]
    mu = _split_dot(acc, seg)
    d = acc - mu
    var = _split_dot(d * d, seg)
    yn = d * lax.rsqrt(var + EPS) * lng_ref[...] + lnb_ref[...]
    a_out = yn * _sigmoid(yn)

    off = 2 * A_WIDTH
    cv = z[:, off + B_WIDTH:off + 2 * B_WIDTH] * z[:, off + 2 * B_WIDTH:off + 3 * B_WIDTH]
    bbuf[B_HALO:B_HALO + tm, :] = cv
    bacc = jnp.zeros((tm, B_WIDTH), F32)
    for k in range(B_CONV):
        bacc = bacc + cbw_ref[k:k + 1, :] * bbuf[pl.ds(B_HALO - (B_CONV - 1) + k, tm), :]
    bbuf[0:B_HALO, :] = bbuf[tm:tm + B_HALO, :]
    bb = z[:, off:off + B_WIDTH] * bacc

    y = (jnp.dot(a_out.astype(BF16), wout_ref[0:A_WIDTH, :], preferred_element_type=F32)
         + jnp.dot(bb.astype(BF16), wout_ref[A_WIDTH:A_WIDTH + B_WIDTH, :],
                   preferred_element_type=F32))
    o_ref[...] = x + mod[2:3] * y


def _conv_mix_call(x, mod, norm_g, w_in, caw, cab, lng, lnb, cbw, w_out, seg, *, layer, tm=512):
    even_in = w_in.shape[1]
    return pl.pallas_call(
        functools.partial(_conv_mix_kernel, tm=tm),
        grid=(SEQ // tm,),
        in_specs=[
            pl.BlockSpec((tm, D_MODEL), lambda i: (i, 0)),
            _resident((None, None, 3, D_MODEL), lambda i: (layer, 1, 0, 0)),
            _resident((None, 1, D_MODEL), lambda i: (layer, 0, 0)),
            _full((D_MODEL, even_in)),
            _full((A_CONV, A_WIDTH)),
            _full((1, A_WIDTH)),
            _full((1, A_WIDTH)),
            _full((1, A_WIDTH)),
            _full((B_CONV, B_WIDTH)),
            _full((A_WIDTH + B_WIDTH, D_MODEL)),
            _full((A_WIDTH, A_WIDTH)),
        ],
        out_specs=pl.BlockSpec((tm, D_MODEL), lambda i: (i, 0)),
        out_shape=jax.ShapeDtypeStruct((SEQ, D_MODEL), F32),
        scratch_shapes=[pltpu.VMEM((A_HALO + tm, A_WIDTH), F32),
                        pltpu.VMEM((B_HALO + tm, B_WIDTH), F32)],
        compiler_params=_params(1),
        name="conv_mixers",
    )(x, mod, norm_g, w_in, caw, cab, lng, lnb, cbw, w_out, seg)


def _pm_proj_kernel(x_ref, mod_ref, g_ref, wu_ref, wq_ref, wkvt_ref, pw_ref,
                    pb_ref, ps_ref, pooled_ref, qrow_ref, kv_ref, km_ref,
                    ubuf, hmean, *, tm):
    i = pl.program_id(0)

    @pl.when(i == 0)
    def _():
        ubuf[0:POOL_HALO, :] = jnp.zeros((POOL_HALO, C_WIDTH), F32)

    x = x_ref[...]
    h = _norm_mod(x, g_ref[...], mod_ref[...]).astype(BF16)
    u = jnp.dot(h, wu_ref[...], preferred_element_type=F32)
    qr = jnp.dot(h, wq_ref[...], preferred_element_type=F32)
    kvt = lax.dot_general(wkvt_ref[...], h, _NT, preferred_element_type=F32)

    ubuf[POOL_HALO:POOL_HALO + tm, :] = u
    t1 = (i * tm + 1 + lax.broadcasted_iota(I32, (tm, C_GROUP_DIM), 0)).astype(F32)
    for gi, w in enumerate(POOL_WINDOWS):
        c0 = gi * C_GROUP_DIM
        s = ubuf[:, c0:c0 + C_GROUP_DIM]
        span = 1
        while span < w:
            s = s + pltpu.roll(s, span, axis=0)
            span *= 2
        assert span == w < POOL_HALO + 1
        s = s[POOL_HALO:, :]
        pooled = s / jnp.minimum(t1, float(w)) - u[:, c0:c0 + C_GROUP_DIM]
        mixed = (jnp.dot(pooled.astype(BF16), pw_ref[gi], preferred_element_type=F32)
                 + pb_ref[:, c0:c0 + C_GROUP_DIM])
        pooled_ref[:, c0:c0 + C_GROUP_DIM] = (
            mixed * ps_ref[:, c0:c0 + C_GROUP_DIM]).astype(BF16)
    ubuf[0:POOL_HALO, :] = ubuf[tm:tm + POOL_HALO, :]

    lane = lax.broadcasted_iota(I32, (tm, HEAD_PAD), 1)
    tail = jnp.where(lane == ALIBI_COL, 1.0, 0.0)
    qs = qr * (HEAD_DIM ** -0.5)
    for pair in range(D_HEADS // 2):
        both = qs[:, pair * HEAD_PAD:(pair + 1) * HEAD_PAD]
        qrow_ref[2 * pair] = jnp.where(lane < HEAD_DIM, both, tail)
        qrow_ref[2 * pair + 1] = jnp.where(
            lane < HEAD_DIM, pltpu.roll(both, HEAD_DIM, axis=1), tail)
    per_tile = tm // MOBA_BLOCK
    kvb = kvt.astype(BF16)
    n_alibi = SLAB_K_ROWS - HEAD_DIM
    first = lax.broadcasted_iota(I32, (n_alibi, MOBA_BLOCK), 0) == 0
    koff = lax.broadcasted_iota(I32, (n_alibi, MOBA_BLOCK), 1).astype(F32)
    for hh in range(D_HEADS):
        alibi = jnp.where(first, koff * (2.0 ** -(hh + 1)), 0.0).astype(BF16)
        for sb in range(per_tile):
            keys = slice(sb * MOBA_BLOCK, (sb + 1) * MOBA_BLOCK)
            kv_ref[hh, sb, 0:HEAD_DIM, :] = kvb[hh * HEAD_DIM:(hh + 1) * HEAD_DIM, keys]
            kv_ref[hh, sb, HEAD_DIM:SLAB_K_ROWS, :] = alibi
            kv_ref[hh, sb, SLAB_K_ROWS:SLAB_ROWS, :] = kvb[
                D_WIDTH + hh * HEAD_DIM:D_WIDTH + (hh + 1) * HEAD_DIM, keys]

    for sb in range(per_tile):
        blk = h[sb * MOBA_BLOCK:(sb + 1) * MOBA_BLOCK, :].astype(F32)
        hmean[pl.ds(i * per_tile + sb, 1), :] = jnp.mean(blk, axis=0, keepdims=True)

    @pl.when(i == pl.num_programs(0) - 1)
    def _():
        hi, lo = _split(hmean[...])
        wkt = wkvt_ref[0:D_WIDTH, :]
        km_ref[...] = (lax.dot_general(hi, wkt, _NT, preferred_element_type=F32)
                       + lax.dot_general(lo, wkt, _NT, preferred_element_type=F32))


def _pm_proj_call(x, mod, norm_g, wu, wq, wkvt, pw, pb, ps, *, layer, tm=512):
    return pl.pallas_call(
        functools.partial(_pm_proj_kernel, tm=tm),
        grid=(SEQ // tm,),
        in_specs=[
            pl.BlockSpec((tm, D_MODEL), lambda i: (i, 0)),
            _resident((None, None, 3, D_MODEL), lambda i: (layer, 1, 0, 0)),
            _resident((None, 1, D_MODEL), lambda i: (layer, 0, 0)),
            _full((D_MODEL, C_WIDTH)),
            _full((D_MODEL, D_WIDTH)),
            _full((2 * D_WIDTH, D_MODEL)),
            _full((len(POOL_WINDOWS), C_GROUP_DIM, C_GROUP_DIM)),
            _full((1, C_WIDTH)),
            _full((1, C_WIDTH)),
        ],
        out_specs=[
            pl.BlockSpec((tm, C_WIDTH), lambda i: (i, 0)),
            pl.BlockSpec((D_HEADS, tm, HEAD_PAD), lambda i: (0, i, 0)),
            pl.BlockSpec((D_HEADS, tm // MOBA_BLOCK, SLAB_ROWS, MOBA_BLOCK),
                         lambda i: (0, i, 0, 0)),
            pl.BlockSpec((N_BLOCKS, D_WIDTH), lambda i: (0, 0)),
        ],
        out_shape=[
            jax.ShapeDtypeStruct((SEQ, C_WIDTH), BF16),
            jax.ShapeDtypeStruct((D_HEADS, SEQ, HEAD_PAD), F32),
            jax.ShapeDtypeStruct((D_HEADS, N_BLOCKS, SLAB_ROWS, MOBA_BLOCK), BF16),
            jax.ShapeDtypeStruct((N_BLOCKS, D_WIDTH), F32),
        ],
        scratch_shapes=[pltpu.VMEM((POOL_HALO + tm, C_WIDTH), F32),
                        pltpu.VMEM((N_BLOCKS, D_MODEL), F32)],
        compiler_params=_params(1),
        name="pool_qkv_proj",
    )(x, mod, norm_g, wu, wq, wkvt, pw, pb, ps)


def _block_rows(qs, slabs, mask, lse_add):
    n = qs[0].shape[0]
    kpad = jnp.zeros((HEAD_PAD - SLAB_K_ROWS, MOBA_BLOCK), BF16)
    s = jnp.concatenate(
        [jnp.dot(q, jnp.concatenate([slab[0:SLAB_K_ROWS, :], kpad], axis=0),
                 preferred_element_type=F32)
         for q, slab in zip(qs, slabs)], axis=0)
    if mask is not None:
        s = jnp.where(mask, s, NEG_BIG)
    m = jnp.max(s, axis=1, keepdims=True)
    p = jnp.exp(s - m).astype(BF16)
    vpad = jnp.zeros((HEAD_PAD - HEAD_DIM, MOBA_BLOCK), BF16)
    ones = jnp.ones((HEAD_PAD, MOBA_BLOCK), BF16)
    ot = jnp.concatenate(
        [lax.dot_general(p[u * n:(u + 1) * n],
                         jnp.concatenate([slab[SLAB_K_ROWS:SLAB_ROWS, :], vpad, ones], axis=0),
                         _NT, preferred_element_type=F32)
         for u, slab in enumerate(slabs)], axis=0)
    o, tot = ot[:, 0:HEAD_PAD], ot[:, HEAD_PAD:2 * HEAD_PAD]
    lse = m + jnp.log(tot)
    if lse_add is not None:
        lse = jnp.concatenate(
            [lse[u * n:(u + 1) * n] + add for u, add in enumerate(lse_add)], axis=0)
    low = lax.broadcasted_iota(I32, o.shape, 1) < HEAD_DIM
    return jnp.where(low, o / tot, lse)


def _onehot_pairs(sel_rows):
    blk = lax.broadcasted_iota(I32, (N_BLOCKS, MOBA_BLOCK), 0)
    return jnp.concatenate(
        [jnp.where(blk == sel_rows[s], 1.0, 0.0) for s in range(MOBA_TOPK)], axis=1)


def _gate_own_kernel(anchor_ref, qrow_ref, kv_ref, km_ref, tri_ref,
                     sel_ref, rank_ref, cnt_ref, own_ref, *, heads, blocks):
    del anchor_ref
    n_pair = MOBA_TOPK * MOBA_BLOCK
    tiles = [(bb, a) for bb in range(blocks) for a in range(heads)]
    rows_of = lambda bb: slice(bb * MOBA_BLOCK, (bb + 1) * MOBA_BLOCK)
    qs = [qrow_ref[a, rows_of(bb), :].astype(BF16) for bb, a in tiles]

    kms = [_split(km_ref[a]) for a in range(heads)]
    gates = [lax.dot_general(kms[a][0], q, _NT, preferred_element_type=F32)
             + lax.dot_general(kms[a][1], q, _NT, preferred_element_type=F32)
             for (bb, a), q in zip(tiles, qs)]
    width = len(tiles) * MOBA_BLOCK
    blk = lax.broadcasted_iota(I32, (N_BLOCKS, width), 0)
    lane = lax.broadcasted_iota(I32, (1, width), 1)
    own = pl.program_id(0) * blocks + lane // (heads * MOBA_BLOCK)
    g = jnp.where(blk < own, jnp.concatenate(gates, axis=1), -jnp.inf)
    slots = []
    for s in range(MOBA_TOPK):
        best = jnp.max(g, axis=0, keepdims=True)
        idx = jnp.min(jnp.where(g == best, blk, N_BLOCKS), axis=0, keepdims=True)
        g = jnp.where(blk == idx, -jnp.inf, g)
        slots.append(jnp.where(s < own, idx, N_BLOCKS))

    onehots = []
    for t, (bb, a) in enumerate(tiles):
        sel_rows = [slot[:, t * MOBA_BLOCK:(t + 1) * MOBA_BLOCK] for slot in slots]
        sel_ref[a, :, rows_of(bb)] = jnp.concatenate(sel_rows, axis=0)
        onehots.append(_onehot_pairs(sel_rows))
    onehot = jnp.concatenate(onehots, axis=0)
    oh = onehot.astype(BF16)
    before = jnp.dot(oh, tri_ref[...], preferred_element_type=F32)
    hit = onehot * before
    ones = jnp.ones((8, n_pair), BF16)
    cnt = lax.dot_general(ones, oh, _NT, preferred_element_type=F32)
    for t, (bb, a) in enumerate(tiles):
        rank = jnp.sum(hit[t * N_BLOCKS:(t + 1) * N_BLOCKS], axis=0, keepdims=True).astype(I32)
        rank_ref[a, :, rows_of(bb)] = jnp.concatenate(
            [rank[:, s * MOBA_BLOCK:(s + 1) * MOBA_BLOCK] for s in range(MOBA_TOPK)], axis=0)
        cnt_ref[a, bb] = cnt[0:1, t * N_BLOCKS:(t + 1) * N_BLOCKS].astype(I32)

    qi = lax.broadcasted_iota(I32, (width, MOBA_BLOCK), 0) % MOBA_BLOCK
    ki = lax.broadcasted_iota(I32, (width, MOBA_BLOCK), 1)
    rows = _block_rows(qs, [kv_ref.at[a, bb] for bb, a in tiles], ki <= qi, None)
    for t, (bb, a) in enumerate(tiles):
        own_ref[a, rows_of(bb), :] = rows[t * MOBA_BLOCK:(t + 1) * MOBA_BLOCK]


def _gate_own_call(anchor, qrows, kv, kmean_h, tri, *, split, blocks=GATE_BLOCKS_PER_STEP):
    heads = SPLIT_HEADS
    rows = blocks * MOBA_BLOCK
    return pl.pallas_call(
        functools.partial(_gate_own_kernel, heads=heads, blocks=blocks),
        grid=(N_BLOCKS // blocks,),
        in_specs=[
            pl.BlockSpec(memory_space=pltpu.SMEM),
            pl.BlockSpec((heads, rows, HEAD_PAD), lambda b: (split, b, 0)),
            pl.BlockSpec((heads, blocks, SLAB_ROWS, MOBA_BLOCK), lambda b: (split, b, 0, 0)),
            pl.BlockSpec((heads, N_BLOCKS, HEAD_PAD), lambda b: (split, 0, 0)),
            _full((MOBA_TOPK * MOBA_BLOCK, MOBA_TOPK * MOBA_BLOCK)),
        ],
        out_specs=[
            pl.BlockSpec((heads, MOBA_TOPK, rows), lambda b: (0, 0, b)),
            pl.BlockSpec((heads, MOBA_TOPK, rows), lambda b: (0, 0, b)),
            pl.BlockSpec((heads, blocks, 1, N_BLOCKS), lambda b: (0, b, 0, 0)),
            pl.BlockSpec((heads, rows, HEAD_PAD), lambda b: (0, b, 0)),
        ],
        out_shape=[
            jax.ShapeDtypeStruct((heads, MOBA_TOPK, SEQ), I32),
            jax.ShapeDtypeStruct((heads, MOBA_TOPK, SEQ), I32),
            jax.ShapeDtypeStruct((heads, N_BLOCKS, 1, N_BLOCKS), I32),
            jax.ShapeDtypeStruct((heads, SEQ, HEAD_PAD), F32),
        ],
        compiler_params=_params(1),
        name=f"moba_gate_own_{split}",
    )(anchor, qrows, kv, kmean_h, tri)


def _route_pos_kernel(sel_ref, rank_ref, base_ref, pos_ref, *, blocks):
    lane = lax.broadcasted_iota(I32, (1, MOBA_BLOCK), 1)
    for bb in range(blocks):
        qs = slice(bb * MOBA_BLOCK, (bb + 1) * MOBA_BLOCK)
        rows = [[] for _ in range(MOBA_TOPK)]
        for a in range(SPLIT_HEADS):
            sel = sel_ref[a, :, qs]
            onehot = _onehot_pairs([sel[s:s + 1, :] for s in range(MOBA_TOPK)]).astype(BF16)
            dig = jnp.dot(base_ref[a, bb].astype(BF16), onehot,
                          preferred_element_type=F32)
            base = (dig[0:1] * float(DIGIT * DIGIT) + dig[1:2] * float(DIGIT)
                    + dig[2:3]).astype(I32)
            for s in range(MOBA_TOPK):
                p = base[:, s * MOBA_BLOCK:(s + 1) * MOBA_BLOCK] + rank_ref[a, s:s + 1, qs]
                rows[s].append(
                    jnp.where(sel[s:s + 1, :] < N_BLOCKS, p, TRASH_ROW0 + lane % SC_WINDOW))
        for s in range(MOBA_TOPK):
            pos_ref[s, :, qs] = jnp.concatenate(rows[s], axis=0)


def _route_pos_call(sel, rank, base_digits, *, blocks=8):
    width = blocks * MOBA_BLOCK
    return pl.pallas_call(
        functools.partial(_route_pos_kernel, blocks=blocks),
        grid=(N_BLOCKS // blocks,),
        in_specs=[
            pl.BlockSpec((SPLIT_HEADS, MOBA_TOPK, width), lambda b: (0, 0, b)),
            pl.BlockSpec((SPLIT_HEADS, MOBA_TOPK, width), lambda b: (0, 0, b)),
            pl.BlockSpec((SPLIT_HEADS, blocks, 8, N_BLOCKS), lambda b: (0, b, 0, 0)),
        ],
        out_specs=pl.BlockSpec((MOBA_TOPK, SPLIT_HEADS, width), lambda b: (0, 0, b)),
        out_shape=jax.ShapeDtypeStruct((MOBA_TOPK, SPLIT_HEADS, SEQ), I32),
        compiler_params=_params(1),
        name="moba_route_pos",
    )(sel, rank, base_digits)


def _route_tables(cnt, split):
    cnt = cnt.reshape(SPLIT_HEADS, N_BLOCKS, N_BLOCKS)
    tiles = (cnt.sum(axis=1) + ROUTE_TILE - 1) // ROUTE_TILE
    tiles_flat = tiles.reshape(N_GROUPS)
    tile0 = jnp.cumsum(tiles_flat) - tiles_flat
    n_tiles = tiles_flat.sum().astype(I32)
    within = jnp.cumsum(cnt, axis=1) - cnt
    base = tile0.reshape(SPLIT_HEADS, 1, N_BLOCKS) * ROUTE_TILE + within
    digits = jnp.stack([base // (DIGIT * DIGIT), (base // DIGIT) % DIGIT, base % DIGIT],
                       axis=2).astype(F32)
    digits = jnp.pad(digits, ((0, 0), (0, 0), (0, 8 - 3), (0, 0)))
    t = jnp.arange(MAX_TILES, dtype=I32)
    ended = (tile0 + tiles_flat)[None, :] <= t[:, None]
    group_of_tile = jnp.minimum(ended.sum(axis=1).astype(I32), N_GROUPS - 1)
    tile_h = group_of_tile // N_BLOCKS
    head = split * SPLIT_HEADS + tile_h
    tile_step = lax.bitcast_convert_type((127 + 7 - head) << 23, F32)
    return digits, tile_h, group_of_tile % N_BLOCKS, n_tiles.reshape(1), tile_step


def _sc_mesh():
    return plsc.VectorSubcoreMesh(core_axis_name="core", subcore_axis_name="subcore")


def _dispatch_call(rows, pos_by_slot, *, split):
    n_rows = SPLIT_HEADS * SEQ
    first_window = split * n_rows // SC_WINDOW

    @functools.partial(
        pl.kernel, mesh=_sc_mesh(), scratch_types=[],
        out_type=jax.ShapeDtypeStruct((ROUTE_ROWS, HEAD_PAD), F32))
    def dispatch(x_hbm, i0_hbm, i1_hbm, i2_hbm, o_hbm):
        def body(x_vmem, i0_vmem, i1_vmem, i2_vmem):
            pltpu.sync_copy(x_vmem, o_hbm.at[i0_vmem.at[0]])
            pltpu.sync_copy(x_vmem, o_hbm.at[i1_vmem.at[0]])
            pltpu.sync_copy(x_vmem, o_hbm.at[i2_vmem.at[0]])

        idx_spec = pl.BlockSpec((1, SC_WINDOW), lambda i: (0, i))
        pltpu.emit_pipeline(
            body,
            grid=(n_rows // SC_WINDOW,),
            in_specs=[pl.BlockSpec((SC_WINDOW, HEAD_PAD), lambda i: (first_window + i, 0)),
                      idx_spec, idx_spec, idx_spec],
            out_specs=[],
            core_axis_name=("core", "subcore"),
            dimension_semantics=(pltpu.PARALLEL,),
        )(x_hbm, i0_hbm, i1_hbm, i2_hbm)

    return dispatch(rows, *pos_by_slot)


def _collect_call(table, idx):
    n_rows = idx.shape[1]

    @functools.partial(
        pl.kernel, mesh=_sc_mesh(), scratch_types=[],
        out_type=jax.ShapeDtypeStruct((n_rows, HEAD_PAD), F32))
    def collect(x_hbm, i_hbm, o_hbm):
        def body(i_vmem, o_vmem):
            pltpu.sync_copy(x_hbm.at[i_vmem.at[0]], o_vmem)

        pltpu.emit_pipeline(
            body,
            grid=(n_rows // SC_WINDOW,),
            in_specs=[pl.BlockSpec((1, SC_WINDOW), lambda i: (0, i))],
            out_specs=[pl.BlockSpec((SC_WINDOW, HEAD_PAD), lambda i: (i, 0))],
            core_axis_name=("core", "subcore"),
            dimension_semantics=(pltpu.PARALLEL,),
        )(i_hbm, o_hbm)

    return collect(table, idx)


def _routed_attn_kernel(th_ref, tj_ref, nt_ref, step_ref, q_ref, *refs, tiles):
    kv_refs, o_ref = refs[0:tiles], refs[tiles]
    t0 = pl.program_id(0) * tiles

    @pl.when(t0 < nt_ref[0])
    def _():
        qs = [q_ref[u * ROUTE_TILE:(u + 1) * ROUTE_TILE, :].astype(BF16) for u in range(tiles)]
        adds = [step_ref[t0 + u] * tj_ref[t0 + u].astype(F32) for u in range(tiles)]
        o_ref[...] = _block_rows(qs, kv_refs, None, adds)

    @pl.when(t0 == MAX_TILES)
    def _():
        low = lax.broadcasted_iota(I32, o_ref.shape, 1) < HEAD_DIM
        o_ref[...] = jnp.where(low, 0.0, NEG_BIG)


def _routed_attn_call(tile_h, tile_j, n_tiles, tile_step, routed_q, kv,
                      *, split, tiles=ROUTE_TILES_PER_STEP):
    n_steps = MAX_TILES // tiles + 1
    assert TRASH_ROW0 == (n_steps - 1) * tiles * ROUTE_TILE
    live = lambda s, th, tj, nt: (jnp.minimum(s, (nt[0] + tiles - 1) // tiles), 0)
    live_or_last = lambda s, th, tj, nt: (
        jnp.where(s == n_steps - 1, s, jnp.minimum(s, (nt[0] + tiles - 1) // tiles)), 0)
    head0 = split * SPLIT_HEADS

    def kv_spec(u):
        def slab(s, th, tj, nt):
            t = jnp.minimum(s, n_steps - 2) * tiles + u
            return head0 + th[t], tj[t], 0, 0
        return pl.BlockSpec((None, None, SLAB_ROWS, MOBA_BLOCK), slab)

    grid_spec = pltpu.PrefetchScalarGridSpec(
        num_scalar_prefetch=3,
        grid=(n_steps,),
        in_specs=([pl.BlockSpec(memory_space=pltpu.SMEM),
                   pl.BlockSpec((tiles * ROUTE_TILE, HEAD_PAD), live)]
                  + [kv_spec(u) for u in range(tiles)]),
        out_specs=pl.BlockSpec((tiles * ROUTE_TILE, HEAD_PAD), live_or_last),
    )
    return pl.pallas_call(
        functools.partial(_routed_attn_kernel, tiles=tiles),
        grid_spec=grid_spec,
        out_shape=jax.ShapeDtypeStruct((ROUTE_ROWS, HEAD_PAD), F32),
        compiler_params=_params(1),
        name=f"moba_routed_attn_{split}",
    )(tile_h, tile_j, n_tiles, tile_step, routed_q, *([kv] * tiles))


def _pm_out_kernel(slope_ref, x_ref, mod_ref, pooled_ref, *refs, blocks):
    own_refs, got_refs = refs[0:N_SPLITS], refs[N_SPLITS:2 * N_SPLITS]
    wp_ref, wa_ref, o_ref = refs[2 * N_SPLITS:]
    low = lax.broadcasted_iota(I32, (MOBA_BLOCK, HEAD_PAD), 1) < HEAD_DIM

    def lse_of(part):
        return jnp.where(low, pltpu.roll(part, HEAD_DIM, axis=1), part)

    y = jnp.dot(pooled_ref[...], wp_ref[...], preferred_element_type=F32)
    for hh in range(D_HEADS):
        own_ref, got_ref = own_refs[hh // SPLIT_HEADS], got_refs[hh // SPLIT_HEADS]
        a = hh % SPLIT_HEADS
        atts = []
        for bb in range(blocks):
            own = pl.program_id(0) * blocks + bb
            rows = slice(bb * MOBA_BLOCK, (bb + 1) * MOBA_BLOCK)
            own_shift = slope_ref[hh] * (own * MOBA_BLOCK).astype(F32)
            parts = [own_ref[a, rows, :]] + [got_ref[s, a, rows, :] for s in range(MOBA_TOPK)]
            lses = [lse_of(parts[0]) + own_shift] + [lse_of(part) for part in parts[1:]]
            top = functools.reduce(jnp.maximum, lses)
            num = jnp.zeros((MOBA_BLOCK, HEAD_PAD), F32)
            den = jnp.zeros((MOBA_BLOCK, HEAD_PAD), F32)
            for part, ls in zip(parts, lses):
                w = jnp.exp(ls - top)
                num = num + w * part
                den = den + w
            atts.append((num / den).astype(BF16))
        y = y + jnp.dot(jnp.concatenate(atts, axis=0), wa_ref[hh], preferred_element_type=F32)
    o_ref[...] = x_ref[...] + mod_ref[2:3, :] * y


def _pm_out_call(slopes, x, mod, pooled, own_parts, got_parts, wp, wa, *, layer, blocks=2):
    tm = blocks * MOBA_BLOCK
    return pl.pallas_call(
        functools.partial(_pm_out_kernel, blocks=blocks),
        grid=(SEQ // tm,),
        in_specs=[
            pl.BlockSpec(memory_space=pltpu.SMEM),
            pl.BlockSpec((tm, D_MODEL), lambda i: (i, 0)),
            _resident((None, None, 3, D_MODEL), lambda i: (layer, 1, 0, 0)),
            pl.BlockSpec((tm, C_WIDTH), lambda i: (i, 0)),
        ] + [pl.BlockSpec((SPLIT_HEADS, tm, HEAD_PAD), lambda i: (0, i, 0))] * N_SPLITS
        + [pl.BlockSpec((MOBA_TOPK, SPLIT_HEADS, tm, HEAD_PAD), lambda i: (0, 0, i, 0))] * N_SPLITS
        + [
            _full((C_WIDTH, D_MODEL)),
            _full((D_HEADS, HEAD_PAD, D_MODEL)),
        ],
        out_specs=pl.BlockSpec((tm, D_MODEL), lambda i: (i, 0)),
        out_shape=jax.ShapeDtypeStruct((SEQ, D_MODEL), F32),
        compiler_params=_params(1),
        name="pool_moba_out",
    )(slopes, x, mod, pooled, *own_parts, *got_parts, wp, wa)


def _head_pad(w, axis):
    shape = list(w.shape)
    shape[axis:axis + 1] = [D_HEADS, HEAD_DIM]
    pad = [(0, 0)] * len(shape)
    pad[axis + 1] = (0, HEAD_PAD - HEAD_DIM)
    out = jnp.pad(w.reshape(shape), pad)
    shape[axis:axis + 2] = [D_HEADS * HEAD_PAD]
    return out.reshape(shape)


def kernel(x, c, ada_w, ada_b, ffn_norm, ffn_w_gate, ffn_w_up, ffn_w_down, mix_norm, conv_w_in,
           conv_a_w, conv_a_b, conv_a_ln_g, conv_a_ln_b, conv_b_w, conv_w_out, pm_w_in, pool_w,
           pool_b, pool_scale, pm_w_out, final_norm):
    assert x.shape == (1, SEQ, D_MODEL) and c.shape == (1, D_MODEL)
    xs = x.reshape(SEQ, D_MODEL)
    mod = _mod_call(c, ada_w, ada_b)

    ffn_g = ffn_norm.reshape(DEPTH, 2, 1, D_MODEL)
    mix_g = mix_norm.reshape(DEPTH, 1, D_MODEL)
    fin = final_norm.reshape(1, D_MODEL)
    ffn_stacks = (ffn_w_gate, ffn_w_up, ffn_w_down)

    def ffn(xs, weights, layer, sub, mixer_weights=()):
        last = (layer, sub) == (DEPTH - 1, 1)
        to_cast = [] if last else [((layer + sub, 1 - sub), w) for w in ffn_stacks]
        to_cast += [((0,), w) for w in mixer_weights]
        xs, cast = _ffn_call(xs, mod, ffn_g, weights, fin, to_cast,
                             layer=layer, sub=sub, final=last)
        return xs, cast[:len(ffn_stacks)], cast[len(ffn_stacks):]

    xs, ffn_w, (cw_in, cw_out) = ffn(xs, tuple(w[0, 0].astype(BF16) for w in ffn_stacks),
                                     layer=0, sub=0, mixer_weights=(conv_w_in, conv_w_out))
    grp = np.arange(A_WIDTH) // (A_WIDTH // A_GROUPS)
    seg = jnp.asarray((grp[:, None] == grp[None, :]) / (A_WIDTH // A_GROUPS), dtype=BF16)
    xs = _conv_mix_call(
        xs, mod, mix_g, cw_in, conv_a_w[0], conv_a_b[0].reshape(1, A_WIDTH),
        conv_a_ln_g[0].reshape(1, A_WIDTH), conv_a_ln_b[0].reshape(1, A_WIDTH), conv_b_w[0],
        cw_out, seg, layer=0)
    xs, ffn_w, (w_in, w_out) = ffn(xs, ffn_w, layer=0, sub=1, mixer_weights=(pm_w_in, pm_w_out))

    xs, ffn_w, _ = ffn(xs, ffn_w, layer=1, sub=0)
    w_u = w_in[:, 0:C_WIDTH]
    w_q = w_in[:, C_WIDTH:C_WIDTH + D_WIDTH]
    w_kvt = w_in[:, C_WIDTH + D_WIDTH:C_WIDTH + 3 * D_WIDTH].T
    pooled, qrows, kv, kmean = _pm_proj_call(
        xs, mod, mix_g, w_u, w_q, w_kvt, pool_w[0].astype(BF16),
        pool_b[0].reshape(1, C_WIDTH), pool_scale[0].reshape(1, C_WIDTH), layer=1)
    kmean_h = _head_pad(kmean, 1).reshape(N_BLOCKS, D_HEADS, HEAD_PAD).transpose(1, 0, 2)

    n_pair = MOBA_TOPK * MOBA_BLOCK
    tri = jnp.asarray(np.arange(n_pair)[:, None] < np.arange(n_pair)[None, :], dtype=BF16)
    q_flat = qrows.reshape(D_HEADS * SEQ, HEAD_PAD)
    own_parts, got_parts = [], []
    anchor = jnp.zeros((1,), I32)
    for split in range(N_SPLITS):
        sel, rank, cnt, own_part = _gate_own_call(anchor, qrows, kv, kmean_h, tri, split=split)
        base_digits, tile_h, tile_j, n_tiles, tile_step = _route_tables(cnt, split)
        pos = _route_pos_call(sel, rank, base_digits)
        anchor = pos[0, 0, 0:1]
        pos_by_slot = [pos[s].reshape(1, SPLIT_HEADS * SEQ) for s in range(MOBA_TOPK)]
        routed_q = _dispatch_call(q_flat, pos_by_slot, split=split)
        routed_parts = _routed_attn_call(tile_h, tile_j, n_tiles, tile_step, routed_q, kv,
                                         split=split)
        got = _collect_call(routed_parts, pos.reshape(1, N_PAIRS))
        own_parts.append(own_part)
        got_parts.append(got.reshape(MOBA_TOPK, SPLIT_HEADS, SEQ, HEAD_PAD))

    slopes = jnp.asarray(2.0 ** (-8.0 * np.arange(1, D_HEADS + 1) / D_HEADS), dtype=F32)
    w_att = _head_pad(w_out[C_WIDTH:], 0).reshape(D_HEADS, HEAD_PAD, D_MODEL)
    xs = _pm_out_call(slopes, xs, mod, pooled, own_parts, got_parts, w_out[0:C_WIDTH], w_att,
                      layer=1)
    xs, _, _ = ffn(xs, ffn_w, layer=1, sub=1)
    return xs.reshape(1, SEQ, D_MODEL)
```

```python
import functools

import numpy as np
import jax
import jax.numpy as jnp
from jax import lax
from jax.experimental import pallas as pl
from jax.experimental.pallas import tpu as pltpu
from jax.experimental.pallas import tpu_sc as plsc

D_MODEL = 1024
SEQ = 16384
DEPTH = 2
N_SUBLAYERS = 3
D_FF = 2816
EPS = 1e-6

A_WIDTH = 512
A_GROUPS = 8
A_CONV = 31
B_WIDTH = 512
B_CONV = 3

POOL_WINDOWS = (2, 4, 8, 16)
C_WIDTH = 512
C_GROUP_DIM = 128
D_HEADS = 8
HEAD_DIM = 64
D_WIDTH = 512
MOBA_BLOCK = 256
MOBA_TOPK = 3
N_BLOCKS = SEQ // MOBA_BLOCK

V7X_SUBLANES = 8
V7X_BF16_SUBLANES = 16
V7X_VMEM_LIMIT_BYTES = 56 * 1024 * 1024

A_HALO = 32
B_HALO = 8
POOL_HALO = 16
NEG_BIG = -1e30

HEAD_PAD = 128
ALIBI_COL = HEAD_DIM
SLAB_K_ROWS = HEAD_DIM + V7X_BF16_SUBLANES
SLAB_ROWS = SLAB_K_ROWS + HEAD_DIM

SPLIT_HEADS = 4
N_SPLITS = D_HEADS // SPLIT_HEADS
ROUTE_TILE = 256
ROUTE_TILES_PER_STEP = 32
GATE_BLOCKS_PER_STEP = 4
N_PAIRS = SPLIT_HEADS * SEQ * MOBA_TOPK
N_GROUPS = SPLIT_HEADS * N_BLOCKS
MAX_TILES = N_PAIRS // ROUTE_TILE + N_GROUPS
TRASH_ROW0 = MAX_TILES * ROUTE_TILE
ROUTE_ROWS = (MAX_TILES + ROUTE_TILES_PER_STEP) * ROUTE_TILE
SC_WINDOW = 128
DIGIT = 128
assert MAX_TILES % ROUTE_TILES_PER_STEP == 0 and N_BLOCKS % GATE_BLOCKS_PER_STEP == 0
assert ROUTE_ROWS <= DIGIT ** 3 and MOBA_BLOCK <= ROUTE_TILE

BF16 = jnp.bfloat16
F32 = jnp.float32
I32 = jnp.int32


def _params(n_axes):
    return pltpu.CompilerParams(
        dimension_semantics=("arbitrary",) * n_axes,
        vmem_limit_bytes=V7X_VMEM_LIMIT_BYTES)


def _resident(block_shape, index_map):
    return pl.BlockSpec(block_shape, index_map, pipeline_mode=pl.Buffered(1))


def _full(shape):
    return _resident(shape, lambda *_: (0,) * len(shape))


def _sigmoid(v):
    return 1.0 / (1.0 + jnp.exp(-v))


def _norm_mod(x, g, mod):
    r = lax.rsqrt(jnp.mean(x * x, axis=-1, keepdims=True) + EPS)
    return (x * r) * (g * (1.0 + mod[1:2])) + mod[0:1]


def _split(v):
    hi = v.astype(BF16)
    return hi, (v - hi.astype(F32)).astype(BF16)


def _split_dot(v, w):
    hi, lo = _split(v)
    return (jnp.dot(hi, w, preferred_element_type=F32)
            + jnp.dot(lo, w, preferred_element_type=F32))


_NT = (((1,), (1,)), ((), ()))


def _mod_kernel(c_ref, w_ref, b_ref, o_ref):
    c = c_ref[...]
    cond = c * _sigmoid(c)
    o_ref[0] = jnp.sum(w_ref[0] * cond, axis=0, keepdims=True) + b_ref[0]


def _mod_call(c, ada_w, ada_b):
    n_out = N_SUBLAYERS * 3 * D_MODEL
    tn = D_MODEL
    out = pl.pallas_call(
        _mod_kernel,
        grid=(DEPTH, n_out // tn),
        in_specs=[
            pl.BlockSpec((D_MODEL, 1), lambda l, j: (0, 0)),
            pl.BlockSpec((1, D_MODEL, tn), lambda l, j: (l, 0, j)),
            pl.BlockSpec((1, 1, tn), lambda l, j: (l, 0, j)),
        ],
        out_specs=pl.BlockSpec((1, 1, tn), lambda l, j: (l, 0, j)),
        out_shape=jax.ShapeDtypeStruct((DEPTH, 1, n_out), F32),
        compiler_params=_params(2),
        name="adaln_mod",
    )(c.reshape(D_MODEL, 1), ada_w, ada_b.reshape(DEPTH, 1, n_out))
    return out.reshape(DEPTH, N_SUBLAYERS, 3, D_MODEL)


def _ffn_kernel(x_ref, *refs, final, with_delta):
    if with_delta:
        d_ref, *refs = refs
    mod_ref, g_ref, wg_ref, wu_ref, wd_ref, fin_ref, *refs = refs
    o_ref = refs[len(refs) // 2]
    for src_ref, dst_ref in zip(refs[:len(refs) // 2], refs[len(refs) // 2 + 1:]):
        dst_ref[...] = src_ref[...].astype(BF16)
    x = x_ref[...]
    if with_delta:
        x = x + d_ref[...]
    mod = mod_ref[...]
    h = _norm_mod(x, g_ref[...], mod).astype(BF16)
    gate = jnp.dot(h, wg_ref[...], preferred_element_type=F32)
    up = jnp.dot(h, wu_ref[...], preferred_element_type=F32)
    act = (gate * _sigmoid(gate) * up).astype(BF16)
    y = jnp.dot(act, wd_ref[...], preferred_element_type=F32)
    xn = x + (0.5 * mod[2:3]) * y
    if final:
        r = lax.rsqrt(jnp.mean(xn * xn, axis=-1, keepdims=True) + EPS)
        xn = xn * r * fin_ref[...]
    o_ref[...] = xn


def _ffn_call(x, mod, norm_g, weights, fin, next_weights, *, layer, sub, final, delta=None,
              tm=512):
    mod_sub = 0 if sub == 0 else 2
    steps = SEQ // tm
    rows_in = [x] if delta is None else [x, delta]
    in_specs = [pl.BlockSpec((tm, D_MODEL), lambda i: (i, 0)) for _ in rows_in] + [
        _resident((None, None, 3, D_MODEL), lambda i: (layer, mod_sub, 0, 0)),
        _resident((None, None, 1, D_MODEL), lambda i: (layer, sub, 0, 0)),
        _full((D_MODEL, D_FF)),
        _full((D_MODEL, D_FF)),
        _full((D_FF, D_MODEL)),
        _full((1, D_MODEL)),
    ]
    out_specs = [pl.BlockSpec((tm, D_MODEL), lambda i: (i, 0))]
    out_shape = [jax.ShapeDtypeStruct((SEQ, D_MODEL), F32)]
    operands = [*rows_in, mod, norm_g, *weights, fin]
    if next_weights is not None:
        (nl, ns), *stacks = next_weights
        for w in stacks:
            n_rows, n_cols = w.shape[2:]
            share = 1 if n_rows % (steps * V7X_BF16_SUBLANES) == 0 else 2
            rows = n_rows * share // steps
            assert n_rows * share % steps == 0 and rows % V7X_BF16_SUBLANES == 0
            in_specs.append(pl.BlockSpec((None, None, rows, n_cols),
                                         lambda i, share=share: (nl, ns, i // share, 0)))
            out_specs.append(pl.BlockSpec((rows, n_cols), lambda i, share=share: (i // share, 0)))
            out_shape.append(jax.ShapeDtypeStruct((n_rows, n_cols), BF16))
            operands.append(w)
    out = pl.pallas_call(
        functools.partial(_ffn_kernel, final=final, with_delta=delta is not None),
        grid=(steps,),
        in_specs=in_specs,
        out_specs=out_specs,
        out_shape=out_shape,
        compiler_params=_params(1),
        name=f"ffn_l{layer}s{sub}",
    )(*operands)
    return out[0], tuple(out[1:])


def _conv_mix_kernel(x_ref, mod_ref, g_ref, win_ref, caw_ref, cab_ref, lng_ref, lnb_ref,
                     cbw_ref, wout_ref, seg_ref, o_ref, abuf, bbuf, *, tm):
    i = pl.program_id(0)

    @pl.when(i == 0)
    def _():
        abuf[0:A_HALO, :] = jnp.zeros((A_HALO, A_WIDTH), F32)
        bbuf[0:B_HALO, :] = jnp.zeros((B_HALO, B_WIDTH), F32)

    x = x_ref[...]
    mod = mod_ref[...]
    h = _norm_mod(x, g_ref[...], mod).astype(BF16)
    z = jnp.dot(h, win_ref[...], preferred_element_type=F32)

    a = z[:, 0:A_WIDTH] * _sigmoid(z[:, A_WIDTH:2 * A_WIDTH])
    abuf[A_HALO:A_HALO + tm, :] = a
    acc = jnp.zeros((tm, A_WIDTH), F32) + cab_ref[...]
    ext = tm + V7X_SUBLANES
    for r in range(V7X_SUBLANES):
        part = None
        for q in range((A_CONV - 1 - r) // V7X_SUBLANES + 1):
            k = A_CONV - 1 - (V7X_SUBLANES * q + r)
            term = caw_ref[k:k + 1, :] * abuf[pl.ds(A_HALO - V7X_SUBLANES * (q + 1), ext), :]
            part = term if part is None else part + term
        if r > 0:
            part = pltpu.roll(part, r, axis=0)
        acc = acc + part[V7X_SUBLANES:, :]
    abuf[0:A_HALO, :] = abuf[tm:tm + A_HALO, :]
    seg = seg_ref[...]
    mu = _split_dot(acc, seg)
    d = acc - mu
    var = _split_dot(d * d, seg)
    yn = d * lax.rsqrt(var + EPS) * lng_ref[...] + lnb_ref[...]
    a_out = yn * _sigmoid(yn)

    off = 2 * A_WIDTH
    cv = z[:, off + B_WIDTH:off + 2 * B_WIDTH] * z[:, off + 2 * B_WIDTH:off + 3 * B_WIDTH]
    bbuf[B_HALO:B_HALO + tm, :] = cv
    bacc = jnp.zeros((tm, B_WIDTH), F32)
    for k in range(B_CONV):
        bacc = bacc + cbw_ref[k:k + 1, :] * bbuf[pl.ds(B_HALO - (B_CONV - 1) + k, tm), :]
    bbuf[0:B_HALO, :] = bbuf[tm:tm + B_HALO, :]
    bb = z[:, off:off + B_WIDTH] * bacc

    y = (jnp.dot(a_out.astype(BF16), wout_ref[0:A_WIDTH, :], preferred_element_type=F32)
         + jnp.dot(bb.astype(BF16), wout_ref[A_WIDTH:A_WIDTH + B_WIDTH, :],
                   preferred_element_type=F32))
    o_ref[...] = x + mod[2:3] * y


def _conv_mix_call(x, mod, norm_g, w_in, caw, cab, lng, lnb, cbw, w_out, seg, *, layer, tm=512):
    even_in = w_in.shape[1]
    return pl.pallas_call(
        functools.partial(_conv_mix_kernel, tm=tm),
        grid=(SEQ // tm,),
        in_specs=[
            pl.BlockSpec((tm, D_MODEL), lambda i: (i, 0)),
            _resident((None, None, 3, D_MODEL), lambda i: (layer, 1, 0, 0)),
            _resident((None, 1, D_MODEL), lambda i: (layer, 0, 0)),
            _full((D_MODEL, even_in)),
            _full((A_CONV, A_WIDTH)),
            _full((1, A_WIDTH)),
            _full((1, A_WIDTH)),
            _full((1, A_WIDTH)),
            _full((B_CONV, B_WIDTH)),
            _full((A_WIDTH + B_WIDTH, D_MODEL)),
            _full((A_WIDTH, A_WIDTH)),
        ],
        out_specs=pl.BlockSpec((tm, D_MODEL), lambda i: (i, 0)),
        out_shape=jax.ShapeDtypeStruct((SEQ, D_MODEL), F32),
        scratch_shapes=[pltpu.VMEM((A_HALO + tm, A_WIDTH), F32),
                        pltpu.VMEM((B_HALO + tm, B_WIDTH), F32)],
        compiler_params=_params(1),
        name="conv_mixers",
    )(x, mod, norm_g, w_in, caw, cab, lng, lnb, cbw, w_out, seg)


def _pm_proj_kernel(x_ref, mod_ref, g_ref, wu_ref, wq_ref, wkvt_ref, pw_ref,
                    pb_ref, ps_ref, pooled_ref, qrow_ref, kv_ref, km_ref,
                    ubuf, hmean, *, tm):
    i = pl.program_id(0)

    @pl.when(i == 0)
    def _():
        ubuf[0:POOL_HALO, :] = jnp.zeros((POOL_HALO, C_WIDTH), F32)

    x = x_ref[...]
    h = _norm_mod(x, g_ref[...], mod_ref[...]).astype(BF16)
    u = jnp.dot(h, wu_ref[...], preferred_element_type=F32)
    qr = jnp.dot(h, wq_ref[...], preferred_element_type=F32)
    kvt = lax.dot_general(wkvt_ref[...], h, _NT, preferred_element_type=F32)

    ubuf[POOL_HALO:POOL_HALO + tm, :] = u
    t1 = (i * tm + 1 + lax.broadcasted_iota(I32, (tm, C_GROUP_DIM), 0)).astype(F32)
    for gi, w in enumerate(POOL_WINDOWS):
        c0 = gi * C_GROUP_DIM
        s = ubuf[:, c0:c0 + C_GROUP_DIM]
        span = 1
        while span < w:
            s = s + pltpu.roll(s, span, axis=0)
            span *= 2
        assert span == w < POOL_HALO + 1
        s = s[POOL_HALO:, :]
        pooled = s / jnp.minimum(t1, float(w)) - u[:, c0:c0 + C_GROUP_DIM]
        mixed = (jnp.dot(pooled.astype(BF16), pw_ref[gi], preferred_element_type=F32)
                 + pb_ref[:, c0:c0 + C_GROUP_DIM])
        pooled_ref[:, c0:c0 + C_GROUP_DIM] = (
            mixed * ps_ref[:, c0:c0 + C_GROUP_DIM]).astype(BF16)
    ubuf[0:POOL_HALO, :] = ubuf[tm:tm + POOL_HALO, :]

    lane = lax.broadcasted_iota(I32, (tm, HEAD_PAD), 1)
    tail = jnp.where(lane == ALIBI_COL, 1.0, 0.0)
    qs = qr * (HEAD_DIM ** -0.5)
    for pair in range(D_HEADS // 2):
        both = qs[:, pair * HEAD_PAD:(pair + 1) * HEAD_PAD]
        qrow_ref[2 * pair] = jnp.where(lane < HEAD_DIM, both, tail)
        qrow_ref[2 * pair + 1] = jnp.where(
            lane < HEAD_DIM, pltpu.roll(both, HEAD_DIM, axis=1), tail)
    per_tile = tm // MOBA_BLOCK
    kvb = kvt.astype(BF16)
    n_alibi = SLAB_K_ROWS - HEAD_DIM
    first = lax.broadcasted_iota(I32, (n_alibi, MOBA_BLOCK), 0) == 0
    koff = lax.broadcasted_iota(I32, (n_alibi, MOBA_BLOCK), 1).astype(F32)
    for hh in range(D_HEADS):
        alibi = jnp.where(first, koff * (2.0 ** -(hh + 1)), 0.0).astype(BF16)
        for sb in range(per_tile):
            keys = slice(sb * MOBA_BLOCK, (sb + 1) * MOBA_BLOCK)
            kv_ref[hh, sb, 0:HEAD_DIM, :] = kvb[hh * HEAD_DIM:(hh + 1) * HEAD_DIM, keys]
            kv_ref[hh, sb, HEAD_DIM:SLAB_K_ROWS, :] = alibi
            kv_ref[hh, sb, SLAB_K_ROWS:SLAB_ROWS, :] = kvb[
                D_WIDTH + hh * HEAD_DIM:D_WIDTH + (hh + 1) * HEAD_DIM, keys]

    for sb in range(per_tile):
        blk = h[sb * MOBA_BLOCK:(sb + 1) * MOBA_BLOCK, :].astype(F32)
        hmean[pl.ds(i * per_tile + sb, 1), :] = jnp.mean(blk, axis=0, keepdims=True)

    @pl.when(i == pl.num_programs(0) - 1)
    def _():
        hi, lo = _split(hmean[...])
        wkt = wkvt_ref[0:D_WIDTH, :]
        km_ref[...] = (lax.dot_general(hi, wkt, _NT, preferred_element_type=F32)
                       + lax.dot_general(lo, wkt, _NT, preferred_element_type=F32))


def _pm_proj_call(x, mod, norm_g, wu, wq, wkvt, pw, pb, ps, *, layer, tm=512):
    return pl.pallas_call(
        functools.partial(_pm_proj_kernel, tm=tm),
        grid=(SEQ // tm,),
        in_specs=[
            pl.BlockSpec((tm, D_MODEL), lambda i: (i, 0)),
            _resident((None, None, 3, D_MODEL), lambda i: (layer, 1, 0, 0)),
            _resident((None, 1, D_MODEL), lambda i: (layer, 0, 0)),
            _full((D_MODEL, C_WIDTH)),
            _full((D_MODEL, D_WIDTH)),
            _full((2 * D_WIDTH, D_MODEL)),
            _full((len(POOL_WINDOWS), C_GROUP_DIM, C_GROUP_DIM)),
            _full((1, C_WIDTH)),
            _full((1, C_WIDTH)),
        ],
        out_specs=[
            pl.BlockSpec((tm, C_WIDTH), lambda i: (i, 0)),
            pl.BlockSpec((D_HEADS, tm, HEAD_PAD), lambda i: (0, i, 0)),
            pl.BlockSpec((D_HEADS, tm // MOBA_BLOCK, SLAB_ROWS, MOBA_BLOCK),
                         lambda i: (0, i, 0, 0)),
            pl.BlockSpec((N_BLOCKS, D_WIDTH), lambda i: (0, 0)),
        ],
        out_shape=[
            jax.ShapeDtypeStruct((SEQ, C_WIDTH), BF16),
            jax.ShapeDtypeStruct((D_HEADS, SEQ, HEAD_PAD), F32),
            jax.ShapeDtypeStruct((D_HEADS, N_BLOCKS, SLAB_ROWS, MOBA_BLOCK), BF16),
            jax.ShapeDtypeStruct((N_BLOCKS, D_WIDTH), F32),
        ],
        scratch_shapes=[pltpu.VMEM((POOL_HALO + tm, C_WIDTH), F32),
                        pltpu.VMEM((N_BLOCKS, D_MODEL), F32)],
        compiler_params=_params(1),
        name="pool_qkv_proj",
    )(x, mod, norm_g, wu, wq, wkvt, pw, pb, ps)


def _block_rows(qs, slabs, mask, lse_add):
    n = qs[0].shape[0]
    kpad = jnp.zeros((HEAD_PAD - SLAB_K_ROWS, MOBA_BLOCK), BF16)
    s = jnp.concatenate(
        [jnp.dot(q, jnp.concatenate([slab[0:SLAB_K_ROWS, :], kpad], axis=0),
                 preferred_element_type=F32)
         for q, slab in zip(qs, slabs)], axis=0)
    if mask is not None:
        s = jnp.where(mask, s, NEG_BIG)
    m = jnp.max(s, axis=1, keepdims=True)
    p = jnp.exp(s - m).astype(BF16)
    vpad = jnp.zeros((HEAD_PAD - HEAD_DIM, MOBA_BLOCK), BF16)
    ones = jnp.ones((HEAD_PAD, MOBA_BLOCK), BF16)
    ot = jnp.concatenate(
        [lax.dot_general(p[u * n:(u + 1) * n],
                         jnp.concatenate([slab[SLAB_K_ROWS:SLAB_ROWS, :], vpad, ones], axis=0),
                         _NT, preferred_element_type=F32)
         for u, slab in enumerate(slabs)], axis=0)
    o, tot = ot[:, 0:HEAD_PAD], ot[:, HEAD_PAD:2 * HEAD_PAD]
    lse = m + jnp.log(tot)
    if lse_add is not None:
        lse = jnp.concatenate(
            [lse[u * n:(u + 1) * n] + add for u, add in enumerate(lse_add)], axis=0)
    low = lax.broadcasted_iota(I32, o.shape, 1) < HEAD_DIM
    return jnp.where(low, o / tot, lse)


def _onehot_pairs(sel_rows):
    blk = lax.broadcasted_iota(I32, (N_BLOCKS, MOBA_BLOCK), 0)
    return jnp.concatenate(
        [jnp.where(blk == sel_rows[s], 1.0, 0.0) for s in range(MOBA_TOPK)], axis=1)


def _gate_own_kernel(anchor_ref, qrow_ref, kv_ref, km_ref, tri_ref,
                     sel_ref, rank_ref, cnt_ref, own_ref, *, heads, blocks):
    del anchor_ref
    n_pair = MOBA_TOPK * MOBA_BLOCK
    tiles = [(bb, a) for bb in range(blocks) for a in range(heads)]
    rows_of = lambda bb: slice(bb * MOBA_BLOCK, (bb + 1) * MOBA_BLOCK)
    qs = [qrow_ref[a, rows_of(bb), :].astype(BF16) for bb, a in tiles]

    kms = [_split(km_ref[a]) for a in range(heads)]
    gates = [lax.dot_general(kms[a][0], q, _NT, preferred_element_type=F32)
             + lax.dot_general(kms[a][1], q, _NT, preferred_element_type=F32)
             for (bb, a), q in zip(tiles, qs)]
    width = len(tiles) * MOBA_BLOCK
    blk = lax.broadcasted_iota(I32, (N_BLOCKS, width), 0)
    lane = lax.broadcasted_iota(I32, (1, width), 1)
    own = pl.program_id(0) * blocks + lane // (heads * MOBA_BLOCK)
    g = jnp.where(blk < own, jnp.concatenate(gates, axis=1), -jnp.inf)
    slots = []
    for s in range(MOBA_TOPK):
        best = jnp.max(g, axis=0, keepdims=True)
        idx = jnp.min(jnp.where(g == best, blk, N_BLOCKS), axis=0, keepdims=True)
        g = jnp.where(blk == idx, -jnp.inf, g)
        slots.append(jnp.where(s < own, idx, N_BLOCKS))

    onehots = []
    for t, (bb, a) in enumerate(tiles):
        sel_rows = [slot[:, t * MOBA_BLOCK:(t + 1) * MOBA_BLOCK] for slot in slots]
        sel_ref[a, :, rows_of(bb)] = jnp.concatenate(sel_rows, axis=0)
        onehots.append(_onehot_pairs(sel_rows))
    onehot = jnp.concatenate(onehots, axis=0)
    oh = onehot.astype(BF16)
    before = jnp.dot(oh, tri_ref[...], preferred_element_type=F32)
    hit = onehot * before
    ones = jnp.ones((8, n_pair), BF16)
    cnt = lax.dot_general(ones, oh, _NT, preferred_element_type=F32)
    for t, (bb, a) in enumerate(tiles):
        rank = jnp.sum(hit[t * N_BLOCKS:(t + 1) * N_BLOCKS], axis=0, keepdims=True).astype(I32)
        rank_ref[a, :, rows_of(bb)] = jnp.concatenate(
            [rank[:, s * MOBA_BLOCK:(s + 1) * MOBA_BLOCK] for s in range(MOBA_TOPK)], axis=0)
        cnt_ref[a, bb] = cnt[0:1, t * N_BLOCKS:(t + 1) * N_BLOCKS].astype(I32)

    qi = lax.broadcasted_iota(I32, (width, MOBA_BLOCK), 0) % MOBA_BLOCK
    ki = lax.broadcasted_iota(I32, (width, MOBA_BLOCK), 1)
    rows = _block_rows(qs, [kv_ref.at[a, bb] for bb, a in tiles], ki <= qi, None)
    for t, (bb, a) in enumerate(tiles):
        own_ref[a, rows_of(bb), :] = rows[t * MOBA_BLOCK:(t + 1) * MOBA_BLOCK]


def _gate_own_call(anchor, qrows, kv, kmean_h, tri, *, split, blocks=GATE_BLOCKS_PER_STEP):
    heads = SPLIT_HEADS
    rows = blocks * MOBA_BLOCK
    return pl.pallas_call(
        functools.partial(_gate_own_kernel, heads=heads, blocks=blocks),
        grid=(N_BLOCKS // blocks,),
        in_specs=[
            pl.BlockSpec(memory_space=pltpu.SMEM),
            pl.BlockSpec((heads, rows, HEAD_PAD), lambda b: (split, b, 0)),
            pl.BlockSpec((heads, blocks, SLAB_ROWS, MOBA_BLOCK), lambda b: (split, b, 0, 0)),
            pl.BlockSpec((heads, N_BLOCKS, HEAD_PAD), lambda b: (split, 0, 0)),
            _full((MOBA_TOPK * MOBA_BLOCK, MOBA_TOPK * MOBA_BLOCK)),
        ],
        out_specs=[
            pl.BlockSpec((heads, MOBA_TOPK, rows), lambda b: (0, 0, b)),
            pl.BlockSpec((heads, MOBA_TOPK, rows), lambda b: (0, 0, b)),
            pl.BlockSpec((heads, blocks, 1, N_BLOCKS), lambda b: (0, b, 0, 0)),
            pl.BlockSpec((heads, rows, HEAD_PAD), lambda b: (0, b, 0)),
        ],
        out_shape=[
            jax.ShapeDtypeStruct((heads, MOBA_TOPK, SEQ), I32),
            jax.ShapeDtypeStruct((heads, MOBA_TOPK, SEQ), I32),
            jax.ShapeDtypeStruct((heads, N_BLOCKS, 1, N_BLOCKS), I32),
            jax.ShapeDtypeStruct((heads, SEQ, HEAD_PAD), F32),
        ],
        compiler_params=_params(1),
        name=f"moba_gate_own_{split}",
    )(anchor, qrows, kv, kmean_h, tri)


def _route_pos_kernel(sel_ref, rank_ref, base_ref, pos_ref, *, blocks):
    lane = lax.broadcasted_iota(I32, (1, MOBA_BLOCK), 1)
    for bb in range(blocks):
        qs = slice(bb * MOBA_BLOCK, (bb + 1) * MOBA_BLOCK)
        rows = [[] for _ in range(MOBA_TOPK)]
        for a in range(SPLIT_HEADS):
            sel = sel_ref[a, :, qs]
            onehot = _onehot_pairs([sel[s:s + 1, :] for s in range(MOBA_TOPK)]).astype(BF16)
            dig = jnp.dot(base_ref[a, bb].astype(BF16), onehot,
                          preferred_element_type=F32)
            base = (dig[0:1] * float(DIGIT * DIGIT) + dig[1:2] * float(DIGIT)
                    + dig[2:3]).astype(I32)
            for s in range(MOBA_TOPK):
                p = base[:, s * MOBA_BLOCK:(s + 1) * MOBA_BLOCK] + rank_ref[a, s:s + 1, qs]
                rows[s].append(
                    jnp.where(sel[s:s + 1, :] < N_BLOCKS, p, TRASH_ROW0 + lane % SC_WINDOW))
        for s in range(MOBA_TOPK):
            pos_ref[s, :, qs] = jnp.concatenate(rows[s], axis=0)


def _route_pos_call(sel, rank, base_digits, *, blocks=8):
    width = blocks * MOBA_BLOCK
    return pl.pallas_call(
        functools.partial(_route_pos_kernel, blocks=blocks),
        grid=(N_BLOCKS // blocks,),
        in_specs=[
            pl.BlockSpec((SPLIT_HEADS, MOBA_TOPK, width), lambda b: (0, 0, b)),
            pl.BlockSpec((SPLIT_HEADS, MOBA_TOPK, width), lambda b: (0, 0, b)),
            pl.BlockSpec((SPLIT_HEADS, blocks, 8, N_BLOCKS), lambda b: (0, b, 0, 0)),
        ],
        out_specs=pl.BlockSpec((MOBA_TOPK, SPLIT_HEADS, width), lambda b: (0, 0, b)),
        out_shape=jax.ShapeDtypeStruct((MOBA_TOPK, SPLIT_HEADS, SEQ), I32),
        compiler_params=_params(1),
        name="moba_route_pos",
    )(sel, rank, base_digits)


def _route_tables(cnt, split):
    cnt = cnt.reshape(SPLIT_HEADS, N_BLOCKS, N_BLOCKS)
    tiles = (cnt.sum(axis=1) + ROUTE_TILE - 1) // ROUTE_TILE
    tiles_flat = tiles.reshape(N_GROUPS)
    tile0 = jnp.cumsum(tiles_flat) - tiles_flat
    n_tiles = tiles_flat.sum().astype(I32)
    within = jnp.cumsum(cnt, axis=1) - cnt
    base = tile0.reshape(SPLIT_HEADS, 1, N_BLOCKS) * ROUTE_TILE + within
    digits = jnp.stack([base // (DIGIT * DIGIT), (base // DIGIT) % DIGIT, base % DIGIT],
                       axis=2).astype(F32)
    digits = jnp.pad(digits, ((0, 0), (0, 0), (0, 8 - 3), (0, 0)))
    t = jnp.arange(MAX_TILES, dtype=I32)
    ended = (tile0 + tiles_flat)[None, :] <= t[:, None]
    group_of_tile = jnp.minimum(ended.sum(axis=1).astype(I32), N_GROUPS - 1)
    tile_h = group_of_tile // N_BLOCKS
    head = split * SPLIT_HEADS + tile_h
    tile_step = lax.bitcast_convert_type((127 + 7 - head) << 23, F32)
    return digits, tile_h, group_of_tile % N_BLOCKS, n_tiles.reshape(1), tile_step


def _sc_mesh():
    return plsc.VectorSubcoreMesh(core_axis_name="core", subcore_axis_name="subcore")


def _dispatch_call(rows, pos_by_slot, *, split):
    n_rows = SPLIT_HEADS * SEQ
    first_window = split * n_rows // SC_WINDOW

    @functools.partial(
        pl.kernel, mesh=_sc_mesh(), scratch_types=[],
        out_type=jax.ShapeDtypeStruct((ROUTE_ROWS, HEAD_PAD), F32))
    def dispatch(x_hbm, i0_hbm, i1_hbm, i2_hbm, o_hbm):
        def body(x_vmem, i0_vmem, i1_vmem, i2_vmem):
            pltpu.sync_copy(x_vmem, o_hbm.at[i0_vmem.at[0]])
            pltpu.sync_copy(x_vmem, o_hbm.at[i1_vmem.at[0]])
            pltpu.sync_copy(x_vmem, o_hbm.at[i2_vmem.at[0]])

        idx_spec = pl.BlockSpec((1, SC_WINDOW), lambda i: (0, i))
        pltpu.emit_pipeline(
            body,
            grid=(n_rows // SC_WINDOW,),
            in_specs=[pl.BlockSpec((SC_WINDOW, HEAD_PAD), lambda i: (first_window + i, 0)),
                      idx_spec, idx_spec, idx_spec],
            out_specs=[],
            core_axis_name=("core", "subcore"),
            dimension_semantics=(pltpu.PARALLEL,),
        )(x_hbm, i0_hbm, i1_hbm, i2_hbm)

    return dispatch(rows, *pos_by_slot)


def _collect_call(table, idx):
    n_rows = idx.shape[1]

    @functools.partial(
        pl.kernel, mesh=_sc_mesh(), scratch_types=[],
        out_type=jax.ShapeDtypeStruct((n_rows, HEAD_PAD), F32))
    def collect(x_hbm, i_hbm, o_hbm):
        def body(i_vmem, o_vmem):
            pltpu.sync_copy(x_hbm.at[i_vmem.at[0]], o_vmem)

        pltpu.emit_pipeline(
            body,
            grid=(n_rows // SC_WINDOW,),
            in_specs=[pl.BlockSpec((1, SC_WINDOW), lambda i: (0, i))],
            out_specs=[pl.BlockSpec((SC_WINDOW, HEAD_PAD), lambda i: (i, 0))],
            core_axis_name=("core", "subcore"),
            dimension_semantics=(pltpu.PARALLEL,),
        )(i_hbm, o_hbm)

    return collect(table, idx)


def _routed_attn_kernel(th_ref, tj_ref, nt_ref, step_ref, q_ref, *refs, tiles):
    kv_refs, o_ref = refs[0:tiles], refs[tiles]
    t0 = pl.program_id(0) * tiles

    @pl.when(t0 < nt_ref[0])
    def _():
        qs = [q_ref[u * ROUTE_TILE:(u + 1) * ROUTE_TILE, :].astype(BF16) for u in range(tiles)]
        adds = [step_ref[t0 + u] * tj_ref[t0 + u].astype(F32) for u in range(tiles)]
        o_ref[...] = _block_rows(qs, kv_refs, None, adds)

    @pl.when(t0 == MAX_TILES)
    def _():
        low = lax.broadcasted_iota(I32, o_ref.shape, 1) < HEAD_DIM
        o_ref[...] = jnp.where(low, 0.0, NEG_BIG)


def _routed_attn_call(tile_h, tile_j, n_tiles, tile_step, routed_q, kv,
                      *, split, tiles=ROUTE_TILES_PER_STEP):
    n_steps = MAX_TILES // tiles + 1
    assert TRASH_ROW0 == (n_steps - 1) * tiles * ROUTE_TILE
    live = lambda s, th, tj, nt: (jnp.minimum(s, (nt[0] + tiles - 1) // tiles), 0)
    live_or_last = lambda s, th, tj, nt: (
        jnp.where(s == n_steps - 1, s, jnp.minimum(s, (nt[0] + tiles - 1) // tiles)), 0)
    head0 = split * SPLIT_HEADS

    def kv_spec(u):
        def slab(s, th, tj, nt):
            t = jnp.minimum(s, n_steps - 2) * tiles + u
            return head0 + th[t], tj[t], 0, 0
        return pl.BlockSpec((None, None, SLAB_ROWS, MOBA_BLOCK), slab)

    grid_spec = pltpu.PrefetchScalarGridSpec(
        num_scalar_prefetch=3,
        grid=(n_steps,),
        in_specs=([pl.BlockSpec(memory_space=pltpu.SMEM),
                   pl.BlockSpec((tiles * ROUTE_TILE, HEAD_PAD), live)]
                  + [kv_spec(u) for u in range(tiles)]),
        out_specs=pl.BlockSpec((tiles * ROUTE_TILE, HEAD_PAD), live_or_last),
    )
    return pl.pallas_call(
        functools.partial(_routed_attn_kernel, tiles=tiles),
        grid_spec=grid_spec,
        out_shape=jax.ShapeDtypeStruct((ROUTE_ROWS, HEAD_PAD), F32),
        compiler_params=_params(1),
        name=f"moba_routed_attn_{split}",
    )(tile_h, tile_j, n_tiles, tile_step, routed_q, *([kv] * tiles))


def _pm_out_kernel(slope_ref, mod_ref, pooled_ref, *refs, blocks):
    own_refs, got_refs = refs[0:N_SPLITS], refs[N_SPLITS:2 * N_SPLITS]
    wp_ref, wa_ref, o_ref = refs[2 * N_SPLITS:]
    low = lax.broadcasted_iota(I32, (MOBA_BLOCK, HEAD_PAD), 1) < HEAD_DIM

    def lse_of(part):
        return jnp.where(low, pltpu.roll(part, HEAD_DIM, axis=1), part)

    y = jnp.dot(pooled_ref[...], wp_ref[...], preferred_element_type=F32)
    for hh in range(D_HEADS):
        own_ref, got_ref = own_refs[hh // SPLIT_HEADS], got_refs[hh // SPLIT_HEADS]
        a = hh % SPLIT_HEADS
        atts = []
        for bb in range(blocks):
            own = pl.program_id(0) * blocks + bb
            rows = slice(bb * MOBA_BLOCK, (bb + 1) * MOBA_BLOCK)
            own_shift = slope_ref[hh] * (own * MOBA_BLOCK).astype(F32)
            parts = [own_ref[a, rows, :]] + [got_ref[s, a, rows, :] for s in range(MOBA_TOPK)]
            lses = [lse_of(parts[0]) + own_shift] + [lse_of(part) for part in parts[1:]]
            top = functools.reduce(jnp.maximum, lses)
            num = jnp.zeros((MOBA_BLOCK, HEAD_PAD), F32)
            den = jnp.zeros((MOBA_BLOCK, HEAD_PAD), F32)
            for part, ls in zip(parts, lses):
                w = jnp.exp(ls - top)
                num = num + w * part
                den = den + w
            atts.append((num / den).astype(BF16))
        y = y + jnp.dot(jnp.concatenate(atts, axis=0), wa_ref[hh], preferred_element_type=F32)
    o_ref[...] = mod_ref[2:3, :] * y


def _pm_out_call(slopes, mod, pooled, own_parts, got_parts, wp, wa, *, layer, blocks=2):
    tm = blocks * MOBA_BLOCK
    return pl.pallas_call(
        functools.partial(_pm_out_kernel, blocks=blocks),
        grid=(SEQ // tm,),
        in_specs=[
            pl.BlockSpec(memory_space=pltpu.SMEM),
            _resident((None, None, 3, D_MODEL), lambda i: (layer, 1, 0, 0)),
            pl.BlockSpec((tm, C_WIDTH), lambda i: (i, 0)),
        ] + [pl.BlockSpec((SPLIT_HEADS, tm, HEAD_PAD), lambda i: (0, i, 0))] * N_SPLITS
        + [pl.BlockSpec((MOBA_TOPK, SPLIT_HEADS, tm, HEAD_PAD), lambda i: (0, 0, i, 0))] * N_SPLITS
        + [
            _full((C_WIDTH, D_MODEL)),
            _full((D_HEADS, HEAD_PAD, D_MODEL)),
        ],
        out_specs=pl.BlockSpec((tm, D_MODEL), lambda i: (i, 0)),
        out_shape=jax.ShapeDtypeStruct((SEQ, D_MODEL), F32),
        compiler_params=_params(1),
        name="pool_moba_out",
    )(slopes, mod, pooled, *own_parts, *got_parts, wp, wa)


def _head_pad(w, axis):
    shape = list(w.shape)
    shape[axis:axis + 1] = [D_HEADS, HEAD_DIM]
    pad = [(0, 0)] * len(shape)
    pad[axis + 1] = (0, HEAD_PAD - HEAD_DIM)
    out = jnp.pad(w.reshape(shape), pad)
    shape[axis:axis + 2] = [D_HEADS * HEAD_PAD]
    return out.reshape(shape)


def kernel(x, c, ada_w, ada_b, ffn_norm, ffn_w_gate, ffn_w_up, ffn_w_down, mix_norm, conv_w_in,
           conv_a_w, conv_a_b, conv_a_ln_g, conv_a_ln_b, conv_b_w, conv_w_out, pm_w_in, pool_w,
           pool_b, pool_scale, pm_w_out, final_norm):
    assert x.shape == (1, SEQ, D_MODEL) and c.shape == (1, D_MODEL)
    xs = x.reshape(SEQ, D_MODEL)
    mod = _mod_call(c, ada_w, ada_b)

    ffn_g = ffn_norm.reshape(DEPTH, 2, 1, D_MODEL)
    mix_g = mix_norm.reshape(DEPTH, 1, D_MODEL)
    fin = final_norm.reshape(1, D_MODEL)
    ffn_stacks = (ffn_w_gate, ffn_w_up, ffn_w_down)

    def ffn(xs, weights, layer, sub, delta=None):
        last = (layer, sub) == (DEPTH - 1, 1)
        after = None if last else ((layer + sub, 1 - sub), *ffn_stacks)
        return _ffn_call(xs, mod, ffn_g, weights, fin, after, layer=layer, sub=sub, final=last,
                         delta=delta)

    xs, ffn_w = ffn(xs, tuple(w[0, 0].astype(BF16) for w in ffn_stacks), layer=0, sub=0)
    grp = np.arange(A_WIDTH) // (A_WIDTH // A_GROUPS)
    seg = jnp.asarray((grp[:, None] == grp[None, :]) / (A_WIDTH // A_GROUPS), dtype=BF16)
    xs = _conv_mix_call(
        xs, mod, mix_g, conv_w_in[0].astype(BF16), conv_a_w[0], conv_a_b[0].reshape(1, A_WIDTH),
        conv_a_ln_g[0].reshape(1, A_WIDTH), conv_a_ln_b[0].reshape(1, A_WIDTH), conv_b_w[0],
        conv_w_out[0].astype(BF16), seg, layer=0)
    xs, ffn_w = ffn(xs, ffn_w, layer=0, sub=1)

    xs, ffn_w = ffn(xs, ffn_w, layer=1, sub=0)
    w_in = pm_w_in[0].astype(BF16)
    w_u = w_in[:, 0:C_WIDTH]
    w_q = w_in[:, C_WIDTH:C_WIDTH + D_WIDTH]
    w_kvt = w_in[:, C_WIDTH + D_WIDTH:C_WIDTH + 3 * D_WIDTH].T
    pooled, qrows, kv, kmean = _pm_proj_call(
        xs, mod, mix_g, w_u, w_q, w_kvt, pool_w[0].astype(BF16),
        pool_b[0].reshape(1, C_WIDTH), pool_scale[0].reshape(1, C_WIDTH), layer=1)
    kmean_h = _head_pad(kmean, 1).reshape(N_BLOCKS, D_HEADS, HEAD_PAD).transpose(1, 0, 2)

    n_pair = MOBA_TOPK * MOBA_BLOCK
    tri = jnp.asarray(np.arange(n_pair)[:, None] < np.arange(n_pair)[None, :], dtype=BF16)
    q_flat = qrows.reshape(D_HEADS * SEQ, HEAD_PAD)
    own_parts, got_parts = [], []
    anchor = jnp.zeros((1,), I32)
    for split in range(N_SPLITS):
        sel, rank, cnt, own_part = _gate_own_call(anchor, qrows, kv, kmean_h, tri, split=split)
        base_digits, tile_h, tile_j, n_tiles, tile_step = _route_tables(cnt, split)
        pos = _route_pos_call(sel, rank, base_digits)
        anchor = pos[0, 0, 0:1]
        pos_by_slot = [pos[s].reshape(1, SPLIT_HEADS * SEQ) for s in range(MOBA_TOPK)]
        routed_q = _dispatch_call(q_flat, pos_by_slot, split=split)
        routed_parts = _routed_attn_call(tile_h, tile_j, n_tiles, tile_step, routed_q, kv,
                                         split=split)
        got = _collect_call(routed_parts, pos.reshape(1, N_PAIRS))
        own_parts.append(own_part)
        got_parts.append(got.reshape(MOBA_TOPK, SPLIT_HEADS, SEQ, HEAD_PAD))

    slopes = jnp.asarray(2.0 ** (-8.0 * np.arange(1, D_HEADS + 1) / D_HEADS), dtype=F32)
    w_out = pm_w_out[0].astype(BF16)
    w_att = _head_pad(w_out[C_WIDTH:], 0).reshape(D_HEADS, HEAD_PAD, D_MODEL)
    mixed = _pm_out_call(slopes, mod, pooled, own_parts, got_parts, w_out[0:C_WIDTH], w_att,
                         layer=1)
    xs, _ = ffn(xs, ffn_w, layer=1, sub=1, delta=mixed)
    return xs.reshape(1, SEQ, D_MODEL)
```

```python
import functools

import numpy as np
import jax
import jax.numpy as jnp
from jax import lax
from jax.experimental import pallas as pl
from jax.experimental.pallas import tpu as pltpu
from jax.experimental.pallas import tpu_sc as plsc

D_MODEL = 1024
SEQ = 16384
DEPTH = 2
N_SUBLAYERS = 3
D_FF = 2816
EPS = 1e-6

A_WIDTH = 512
A_GROUPS = 8
A_CONV = 31
B_WIDTH = 512
B_CONV = 3

POOL_WINDOWS = (2, 4, 8, 16)
C_WIDTH = 512
C_GROUP_DIM = 128
D_HEADS = 8
HEAD_DIM = 64
D_WIDTH = 512
MOBA_BLOCK = 256
MOBA_TOPK = 3
N_BLOCKS = SEQ // MOBA_BLOCK

V7X_SUBLANES = 8
V7X_BF16_SUBLANES = 16
V7X_VMEM_LIMIT_BYTES = 56 * 1024 * 1024

A_HALO = 32
B_HALO = 8
POOL_HALO = 16
NEG_BIG = -1e30

HEAD_PAD = 128
ALIBI_COL = HEAD_DIM
SLAB_K_ROWS = HEAD_DIM + V7X_BF16_SUBLANES
SLAB_ROWS = SLAB_K_ROWS + HEAD_DIM

SPLIT_HEADS = 4
N_SPLITS = D_HEADS // SPLIT_HEADS
ROUTE_TILE = 256
ROUTE_TILES_PER_STEP = 32
GATE_BLOCKS_PER_STEP = 4
N_PAIRS = SPLIT_HEADS * SEQ * MOBA_TOPK
N_GROUPS = SPLIT_HEADS * N_BLOCKS
MAX_TILES = N_PAIRS // ROUTE_TILE + N_GROUPS
TRASH_ROW0 = MAX_TILES * ROUTE_TILE
ROUTE_ROWS = (MAX_TILES + ROUTE_TILES_PER_STEP) * ROUTE_TILE
SC_WINDOW = 128
DIGIT = 128
assert MAX_TILES % ROUTE_TILES_PER_STEP == 0 and N_BLOCKS % GATE_BLOCKS_PER_STEP == 0
assert ROUTE_ROWS <= DIGIT ** 3 and MOBA_BLOCK <= ROUTE_TILE

BF16 = jnp.bfloat16
F32 = jnp.float32
I32 = jnp.int32


def _params(n_axes):
    return pltpu.CompilerParams(
        dimension_semantics=("arbitrary",) * n_axes,
        vmem_limit_bytes=V7X_VMEM_LIMIT_BYTES)


def _resident(block_shape, index_map):
    return pl.BlockSpec(block_shape, index_map, pipeline_mode=pl.Buffered(1))


def _full(shape):
    return _resident(shape, lambda *_: (0,) * len(shape))


def _sigmoid(v):
    return 1.0 / (1.0 + jnp.exp(-v))


def _norm_mod(x, g, mod):
    r = lax.rsqrt(jnp.mean(x * x, axis=-1, keepdims=True) + EPS)
    return (x * r) * (g * (1.0 + mod[1:2])) + mod[0:1]


def _split(v):
    hi = v.astype(BF16)
    return hi, (v - hi.astype(F32)).astype(BF16)


def _split_dot(v, w):
    hi, lo = _split(v)
    return (jnp.dot(hi, w, preferred_element_type=F32)
            + jnp.dot(lo, w, preferred_element_type=F32))


_NT = (((1,), (1,)), ((), ()))


def _mod_kernel(c_ref, w_ref, b_ref, o_ref):
    c = c_ref[...]
    cond = c * _sigmoid(c)
    o_ref[0] = jnp.sum(w_ref[0] * cond, axis=0, keepdims=True) + b_ref[0]


def _mod_call(c, ada_w, ada_b):
    n_out = N_SUBLAYERS * 3 * D_MODEL
    tn = D_MODEL
    out = pl.pallas_call(
        _mod_kernel,
        grid=(DEPTH, n_out // tn),
        in_specs=[
            pl.BlockSpec((D_MODEL, 1), lambda l, j: (0, 0)),
            pl.BlockSpec((1, D_MODEL, tn), lambda l, j: (l, 0, j)),
            pl.BlockSpec((1, 1, tn), lambda l, j: (l, 0, j)),
        ],
        out_specs=pl.BlockSpec((1, 1, tn), lambda l, j: (l, 0, j)),
        out_shape=jax.ShapeDtypeStruct((DEPTH, 1, n_out), F32),
        compiler_params=_params(2),
        name="adaln_mod",
    )(c.reshape(D_MODEL, 1), ada_w, ada_b.reshape(DEPTH, 1, n_out))
    return out.reshape(DEPTH, N_SUBLAYERS, 3, D_MODEL)


def _ffn_kernel(x_ref, *refs, final, with_delta):
    if with_delta:
        d_ref, *refs = refs
    mod_ref, g_ref, wg_ref, wu_ref, wd_ref, fin_ref, *refs = refs
    o_ref = refs[len(refs) // 2]
    for src_ref, dst_ref in zip(refs[:len(refs) // 2], refs[len(refs) // 2 + 1:]):
        dst_ref[...] = src_ref[...].astype(BF16)
    x = x_ref[...]
    if with_delta:
        x = x + d_ref[...]
    mod = mod_ref[...]
    h = _norm_mod(x, g_ref[...], mod).astype(BF16)
    gate = jnp.dot(h, wg_ref[...], preferred_element_type=F32)
    up = jnp.dot(h, wu_ref[...], preferred_element_type=F32)
    act = (gate * _sigmoid(gate) * up).astype(BF16)
    y = jnp.dot(act, wd_ref[...], preferred_element_type=F32)
    xn = x + (0.5 * mod[2:3]) * y
    if final:
        r = lax.rsqrt(jnp.mean(xn * xn, axis=-1, keepdims=True) + EPS)
        xn = xn * r * fin_ref[...]
    o_ref[...] = xn


def _ffn_call(x, mod, norm_g, weights, fin, next_weights, *, layer, sub, final, delta=None,
              tm=512):
    mod_sub = 0 if sub == 0 else 2
    steps = SEQ // tm
    rows_in = [x] if delta is None else [x, delta]
    in_specs = [pl.BlockSpec((tm, D_MODEL), lambda i: (i, 0)) for _ in rows_in] + [
        _resident((None, None, 3, D_MODEL), lambda i: (layer, mod_sub, 0, 0)),
        _resident((None, None, 1, D_MODEL), lambda i: (layer, sub, 0, 0)),
        _full((D_MODEL, D_FF)),
        _full((D_MODEL, D_FF)),
        _full((D_FF, D_MODEL)),
        _full((1, D_MODEL)),
    ]
    out_specs = [pl.BlockSpec((tm, D_MODEL), lambda i: (i, 0))]
    out_shape = [jax.ShapeDtypeStruct((SEQ, D_MODEL), F32)]
    operands = [*rows_in, mod, norm_g, *weights, fin]
    if next_weights is not None:
        (nl, ns), *stacks = next_weights
        for w in stacks:
            n_rows, n_cols = w.shape[2:]
            share = 1 if n_rows % (steps * V7X_BF16_SUBLANES) == 0 else 2
            rows = n_rows * share // steps
            assert n_rows * share % steps == 0 and rows % V7X_BF16_SUBLANES == 0
            in_specs.append(pl.BlockSpec((None, None, rows, n_cols),
                                         lambda i, share=share: (nl, ns, i // share, 0)))
            out_specs.append(pl.BlockSpec((rows, n_cols), lambda i, share=share: (i // share, 0)))
            out_shape.append(jax.ShapeDtypeStruct((n_rows, n_cols), BF16))
            operands.append(w)
    out = pl.pallas_call(
        functools.partial(_ffn_kernel, final=final, with_delta=delta is not None),
        grid=(steps,),
        in_specs=in_specs,
        out_specs=out_specs,
        out_shape=out_shape,
        compiler_params=_params(1),
        name=f"ffn_l{layer}s{sub}",
    )(*operands)
    return out[0], tuple(out[1:])


def _conv_mix_kernel(x_ref, mod_ref, g_ref, win_ref, caw_ref, cab_ref, lng_ref, lnb_ref,
                     cbw_ref, wout_ref, seg_ref, o_ref, abuf, bbuf, *, tm):
    i = pl.program_id(0)

    @pl.when(i == 0)
    def _():
        abuf[0:A_HALO, :] = jnp.zeros((A_HALO, A_WIDTH), F32)
        bbuf[0:B_HALO, :] = jnp.zeros((B_HALO, B_WIDTH), F32)

    x = x_ref[...]
    mod = mod_ref[...]
    h = _norm_mod(x, g_ref[...], mod).astype(BF16)
    z = jnp.dot(h, win_ref[...], preferred_element_type=F32)

    a = z[:, 0:A_WIDTH] * _sigmoid(z[:, A_WIDTH:2 * A_WIDTH])
    abuf[A_HALO:A_HALO + tm, :] = a
    acc = jnp.zeros((tm, A_WIDTH), F32) + cab_ref[...]
    ext = tm + V7X_SUBLANES
    for r in range(V7X_SUBLANES):
        part = None
        for q in range((A_CONV - 1 - r) // V7X_SUBLANES + 1):
            k = A_CONV - 1 - (V7X_SUBLANES * q + r)
            term = caw_ref[k:k + 1, :] * abuf[pl.ds(A_HALO - V7X_SUBLANES * (q + 1), ext), :]
            part = term if part is None else part + term
        if r > 0:
            part = pltpu.roll(part, r, axis=0)
        acc = acc + part[V7X_SUBLANES:, :]
    abuf[0:A_HALO, :] = abuf[tm:tm + A_HALO, :]
    seg = seg_ref[...]
    mu = _split_dot(acc, seg)
    d = acc - mu
    var = _split_dot(d * d, seg)
    yn = d * lax.rsqrt(var + EPS) * lng_ref[...] + lnb_ref[...]
    a_out = yn * _sigmoid(yn)

    off = 2 * A_WIDTH
    cv = z[:, off + B_WIDTH:off + 2 * B_WIDTH] * z[:, off + 2 * B_WIDTH:off + 3 * B_WIDTH]
    bbuf[B_HALO:B_HALO + tm, :] = cv
    bacc = jnp.zeros((tm, B_WIDTH), F32)
    for k in range(B_CONV):
        bacc = bacc + cbw_ref[k:k + 1, :] * bbuf[pl.ds(B_HALO - (B_CONV - 1) + k, tm), :]
    bbuf[0:B_HALO, :] = bbuf[tm:tm + B_HALO, :]
    bb = z[:, off:off + B_WIDTH] * bacc

    y = (jnp.dot(a_out.astype(BF16), wout_ref[0:A_WIDTH, :], preferred_element_type=F32)
         + jnp.dot(bb.astype(BF16), wout_ref[A_WIDTH:A_WIDTH + B_WIDTH, :],
                   preferred_element_type=F32))
    o_ref[...] = x + mod[2:3] * y


def _conv_mix_call(x, mod, norm_g, w_in, caw, cab, lng, lnb, cbw, w_out, seg, *, layer, tm=512):
    even_in = w_in.shape[1]
    return pl.pallas_call(
        functools.partial(_conv_mix_kernel, tm=tm),
        grid=(SEQ // tm,),
        in_specs=[
            pl.BlockSpec((tm, D_MODEL), lambda i: (i, 0)),
            _resident((None, None, 3, D_MODEL), lambda i: (layer, 1, 0, 0)),
            _resident((None, 1, D_MODEL), lambda i: (layer, 0, 0)),
            _full((D_MODEL, even_in)),
            _full((A_CONV, A_WIDTH)),
            _full((1, A_WIDTH)),
            _full((1, A_WIDTH)),
            _full((1, A_WIDTH)),
            _full((B_CONV, B_WIDTH)),
            _full((A_WIDTH + B_WIDTH, D_MODEL)),
            _full((A_WIDTH, A_WIDTH)),
        ],
        out_specs=pl.BlockSpec((tm, D_MODEL), lambda i: (i, 0)),
        out_shape=jax.ShapeDtypeStruct((SEQ, D_MODEL), F32),
        scratch_shapes=[pltpu.VMEM((A_HALO + tm, A_WIDTH), F32),
                        pltpu.VMEM((B_HALO + tm, B_WIDTH), F32)],
        compiler_params=_params(1),
        name="conv_mixers",
    )(x, mod, norm_g, w_in, caw, cab, lng, lnb, cbw, w_out, seg)


def _pm_proj_kernel(x_ref, mod_ref, g_ref, wu_ref, wq_ref, wkvt_ref, pw_ref,
                    pb_ref, ps_ref, pooled_ref, qrow_ref, kv_ref, km_ref,
                    ubuf, hmean, *, tm):
    i = pl.program_id(0)

    @pl.when(i == 0)
    def _():
        ubuf[0:POOL_HALO, :] = jnp.zeros((POOL_HALO, C_WIDTH), F32)

    x = x_ref[...]
    h = _norm_mod(x, g_ref[...], mod_ref[...]).astype(BF16)
    u = jnp.dot(h, wu_ref[...], preferred_element_type=F32)
    qr = jnp.dot(h, wq_ref[...], preferred_element_type=F32)
    kvt = lax.dot_general(wkvt_ref[...], h, _NT, preferred_element_type=F32)

    ubuf[POOL_HALO:POOL_HALO + tm, :] = u
    t1 = (i * tm + 1 + lax.broadcasted_iota(I32, (tm, C_GROUP_DIM), 0)).astype(F32)
    for gi, w in enumerate(POOL_WINDOWS):
        c0 = gi * C_GROUP_DIM
        s = ubuf[:, c0:c0 + C_GROUP_DIM]
        span = 1
        while span < w:
            s = s + pltpu.roll(s, span, axis=0)
            span *= 2
        assert span == w < POOL_HALO + 1
        s = s[POOL_HALO:, :]
        pooled = s / jnp.minimum(t1, float(w)) - u[:, c0:c0 + C_GROUP_DIM]
        mixed = (jnp.dot(pooled.astype(BF16), pw_ref[gi], preferred_element_type=F32)
                 + pb_ref[:, c0:c0 + C_GROUP_DIM])
        pooled_ref[:, c0:c0 + C_GROUP_DIM] = (
            mixed * ps_ref[:, c0:c0 + C_GROUP_DIM]).astype(BF16)
    ubuf[0:POOL_HALO, :] = ubuf[tm:tm + POOL_HALO, :]

    lane = lax.broadcasted_iota(I32, (tm, HEAD_PAD), 1)
    tail = jnp.where(lane == ALIBI_COL, 1.0, 0.0)
    qs = qr * (HEAD_DIM ** -0.5)
    for pair in range(D_HEADS // 2):
        both = qs[:, pair * HEAD_PAD:(pair + 1) * HEAD_PAD]
        qrow_ref[2 * pair] = jnp.where(lane < HEAD_DIM, both, tail)
        qrow_ref[2 * pair + 1] = jnp.where(
            lane < HEAD_DIM, pltpu.roll(both, HEAD_DIM, axis=1), tail)
    per_tile = tm // MOBA_BLOCK
    kvb = kvt.astype(BF16)
    n_alibi = SLAB_K_ROWS - HEAD_DIM
    first = lax.broadcasted_iota(I32, (n_alibi, MOBA_BLOCK), 0) == 0
    koff = lax.broadcasted_iota(I32, (n_alibi, MOBA_BLOCK), 1).astype(F32)
    for hh in range(D_HEADS):
        alibi = jnp.where(first, koff * (2.0 ** -(hh + 1)), 0.0).astype(BF16)
        for sb in range(per_tile):
            keys = slice(sb * MOBA_BLOCK, (sb + 1) * MOBA_BLOCK)
            kv_ref[hh, sb, 0:HEAD_DIM, :] = kvb[hh * HEAD_DIM:(hh + 1) * HEAD_DIM, keys]
            kv_ref[hh, sb, HEAD_DIM:SLAB_K_ROWS, :] = alibi
            kv_ref[hh, sb, SLAB_K_ROWS:SLAB_ROWS, :] = kvb[
                D_WIDTH + hh * HEAD_DIM:D_WIDTH + (hh + 1) * HEAD_DIM, keys]

    for sb in range(per_tile):
        blk = h[sb * MOBA_BLOCK:(sb + 1) * MOBA_BLOCK, :].astype(F32)
        hmean[pl.ds(i * per_tile + sb, 1), :] = jnp.mean(blk, axis=0, keepdims=True)

    @pl.when(i == pl.num_programs(0) - 1)
    def _():
        hi, lo = _split(hmean[...])
        wkt = wkvt_ref[0:D_WIDTH, :]
        km_ref[...] = (lax.dot_general(hi, wkt, _NT, preferred_element_type=F32)
                       + lax.dot_general(lo, wkt, _NT, preferred_element_type=F32))


def _pm_proj_call(x, mod, norm_g, wu, wq, wkvt, pw, pb, ps, *, layer, tm=1024):
    return pl.pallas_call(
        functools.partial(_pm_proj_kernel, tm=tm),
        grid=(SEQ // tm,),
        in_specs=[
            pl.BlockSpec((tm, D_MODEL), lambda i: (i, 0)),
            _resident((None, None, 3, D_MODEL), lambda i: (layer, 1, 0, 0)),
            _resident((None, 1, D_MODEL), lambda i: (layer, 0, 0)),
            _full((D_MODEL, C_WIDTH)),
            _full((D_MODEL, D_WIDTH)),
            _full((2 * D_WIDTH, D_MODEL)),
            _full((len(POOL_WINDOWS), C_GROUP_DIM, C_GROUP_DIM)),
            _full((1, C_WIDTH)),
            _full((1, C_WIDTH)),
        ],
        out_specs=[
            pl.BlockSpec((tm, C_WIDTH), lambda i: (i, 0)),
            pl.BlockSpec((D_HEADS, tm, HEAD_PAD), lambda i: (0, i, 0)),
            pl.BlockSpec((D_HEADS, tm // MOBA_BLOCK, SLAB_ROWS, MOBA_BLOCK),
                         lambda i: (0, i, 0, 0)),
            pl.BlockSpec((N_BLOCKS, D_WIDTH), lambda i: (0, 0)),
        ],
        out_shape=[
            jax.ShapeDtypeStruct((SEQ, C_WIDTH), BF16),
            jax.ShapeDtypeStruct((D_HEADS, SEQ, HEAD_PAD), F32),
            jax.ShapeDtypeStruct((D_HEADS, N_BLOCKS, SLAB_ROWS, MOBA_BLOCK), BF16),
            jax.ShapeDtypeStruct((N_BLOCKS, D_WIDTH), F32),
        ],
        scratch_shapes=[pltpu.VMEM((POOL_HALO + tm, C_WIDTH), F32),
                        pltpu.VMEM((N_BLOCKS, D_MODEL), F32)],
        compiler_params=_params(1),
        name="pool_qkv_proj",
    )(x, mod, norm_g, wu, wq, wkvt, pw, pb, ps)


def _block_rows(qs, slabs, mask, lse_add):
    n = qs[0].shape[0]
    kpad = jnp.zeros((HEAD_PAD - SLAB_K_ROWS, MOBA_BLOCK), BF16)
    s = jnp.concatenate(
        [jnp.dot(q, jnp.concatenate([slab[0:SLAB_K_ROWS, :], kpad], axis=0),
                 preferred_element_type=F32)
         for q, slab in zip(qs, slabs)], axis=0)
    if mask is not None:
        s = jnp.where(mask, s, NEG_BIG)
    m = jnp.max(s, axis=1, keepdims=True)
    p = jnp.exp(s - m).astype(BF16)
    vpad = jnp.zeros((HEAD_PAD - HEAD_DIM, MOBA_BLOCK), BF16)
    ones = jnp.ones((HEAD_PAD, MOBA_BLOCK), BF16)
    ot = jnp.concatenate(
        [lax.dot_general(p[u * n:(u + 1) * n],
                         jnp.concatenate([slab[SLAB_K_ROWS:SLAB_ROWS, :], vpad, ones], axis=0),
                         _NT, preferred_element_type=F32)
         for u, slab in enumerate(slabs)], axis=0)
    o, tot = ot[:, 0:HEAD_PAD], ot[:, HEAD_PAD:2 * HEAD_PAD]
    lse = m + jnp.log(tot)
    if lse_add is not None:
        lse = jnp.concatenate(
            [lse[u * n:(u + 1) * n] + add for u, add in enumerate(lse_add)], axis=0)
    low = lax.broadcasted_iota(I32, o.shape, 1) < HEAD_DIM
    return jnp.where(low, o / tot, lse)


def _onehot_pairs(sel_rows):
    blk = lax.broadcasted_iota(I32, (N_BLOCKS, MOBA_BLOCK), 0)
    return jnp.concatenate(
        [jnp.where(blk == sel_rows[s], 1.0, 0.0) for s in range(MOBA_TOPK)], axis=1)


def _gate_own_kernel(anchor_ref, qrow_ref, kv_ref, km_ref, tri_ref,
                     sel_ref, rank_ref, cnt_ref, own_ref, *, heads, blocks):
    del anchor_ref
    n_pair = MOBA_TOPK * MOBA_BLOCK
    tiles = [(bb, a) for bb in range(blocks) for a in range(heads)]
    rows_of = lambda bb: slice(bb * MOBA_BLOCK, (bb + 1) * MOBA_BLOCK)
    qs = [qrow_ref[a, rows_of(bb), :].astype(BF16) for bb, a in tiles]

    kms = [_split(km_ref[a]) for a in range(heads)]
    gates = [lax.dot_general(kms[a][0], q, _NT, preferred_element_type=F32)
             + lax.dot_general(kms[a][1], q, _NT, preferred_element_type=F32)
             for (bb, a), q in zip(tiles, qs)]
    width = len(tiles) * MOBA_BLOCK
    blk = lax.broadcasted_iota(I32, (N_BLOCKS, width), 0)
    lane = lax.broadcasted_iota(I32, (1, width), 1)
    own = pl.program_id(0) * blocks + lane // (heads * MOBA_BLOCK)
    g = jnp.where(blk < own, jnp.concatenate(gates, axis=1), -jnp.inf)
    slots = []
    for s in range(MOBA_TOPK):
        best = jnp.max(g, axis=0, keepdims=True)
        idx = jnp.min(jnp.where(g == best, blk, N_BLOCKS), axis=0, keepdims=True)
        g = jnp.where(blk == idx, -jnp.inf, g)
        slots.append(jnp.where(s < own, idx, N_BLOCKS))

    onehots = []
    for t, (bb, a) in enumerate(tiles):
        sel_rows = [slot[:, t * MOBA_BLOCK:(t + 1) * MOBA_BLOCK] for slot in slots]
        sel_ref[a, :, rows_of(bb)] = jnp.concatenate(sel_rows, axis=0)
        onehots.append(_onehot_pairs(sel_rows))
    onehot = jnp.concatenate(onehots, axis=0)
    oh = onehot.astype(BF16)
    before = jnp.dot(oh, tri_ref[...], preferred_element_type=F32)
    hit = onehot * before
    ones = jnp.ones((8, n_pair), BF16)
    cnt = lax.dot_general(ones, oh, _NT, preferred_element_type=F32)
    for t, (bb, a) in enumerate(tiles):
        rank = jnp.sum(hit[t * N_BLOCKS:(t + 1) * N_BLOCKS], axis=0, keepdims=True).astype(I32)
        rank_ref[a, :, rows_of(bb)] = jnp.concatenate(
            [rank[:, s * MOBA_BLOCK:(s + 1) * MOBA_BLOCK] for s in range(MOBA_TOPK)], axis=0)
        cnt_ref[a, bb] = cnt[0:1, t * N_BLOCKS:(t + 1) * N_BLOCKS].astype(I32)

    qi = lax.broadcasted_iota(I32, (width, MOBA_BLOCK), 0) % MOBA_BLOCK
    ki = lax.broadcasted_iota(I32, (width, MOBA_BLOCK), 1)
    rows = _block_rows(qs, [kv_ref.at[a, bb] for bb, a in tiles], ki <= qi, None)
    for t, (bb, a) in enumerate(tiles):
        own_ref[a, rows_of(bb), :] = rows[t * MOBA_BLOCK:(t + 1) * MOBA_BLOCK]


def _gate_own_call(anchor, qrows, kv, kmean_h, tri, *, split, blocks=GATE_BLOCKS_PER_STEP):
    heads = SPLIT_HEADS
    rows = blocks * MOBA_BLOCK
    return pl.pallas_call(
        functools.partial(_gate_own_kernel, heads=heads, blocks=blocks),
        grid=(N_BLOCKS // blocks,),
        in_specs=[
            pl.BlockSpec(memory_space=pltpu.SMEM),
            pl.BlockSpec((heads, rows, HEAD_PAD), lambda b: (split, b, 0)),
            pl.BlockSpec((heads, blocks, SLAB_ROWS, MOBA_BLOCK), lambda b: (split, b, 0, 0)),
            pl.BlockSpec((heads, N_BLOCKS, HEAD_PAD), lambda b: (split, 0, 0)),
            _full((MOBA_TOPK * MOBA_BLOCK, MOBA_TOPK * MOBA_BLOCK)),
        ],
        out_specs=[
            pl.BlockSpec((heads, MOBA_TOPK, rows), lambda b: (0, 0, b)),
            pl.BlockSpec((heads, MOBA_TOPK, rows), lambda b: (0, 0, b)),
            pl.BlockSpec((heads, blocks, 1, N_BLOCKS), lambda b: (0, b, 0, 0)),
            pl.BlockSpec((heads, rows, HEAD_PAD), lambda b: (0, b, 0)),
        ],
        out_shape=[
            jax.ShapeDtypeStruct((heads, MOBA_TOPK, SEQ), I32),
            jax.ShapeDtypeStruct((heads, MOBA_TOPK, SEQ), I32),
            jax.ShapeDtypeStruct((heads, N_BLOCKS, 1, N_BLOCKS), I32),
            jax.ShapeDtypeStruct((heads, SEQ, HEAD_PAD), F32),
        ],
        compiler_params=_params(1),
        name=f"moba_gate_own_{split}",
    )(anchor, qrows, kv, kmean_h, tri)


def _route_pos_kernel(sel_ref, rank_ref, base_ref, pos_ref, *, blocks):
    lane = lax.broadcasted_iota(I32, (1, MOBA_BLOCK), 1)
    for bb in range(blocks):
        qs = slice(bb * MOBA_BLOCK, (bb + 1) * MOBA_BLOCK)
        rows = [[] for _ in range(MOBA_TOPK)]
        for a in range(SPLIT_HEADS):
            sel = sel_ref[a, :, qs]
            onehot = _onehot_pairs([sel[s:s + 1, :] for s in range(MOBA_TOPK)]).astype(BF16)
            dig = jnp.dot(base_ref[a, bb].astype(BF16), onehot,
                          preferred_element_type=F32)
            base = (dig[0:1] * float(DIGIT * DIGIT) + dig[1:2] * float(DIGIT)
                    + dig[2:3]).astype(I32)
            for s in range(MOBA_TOPK):
                p = base[:, s * MOBA_BLOCK:(s + 1) * MOBA_BLOCK] + rank_ref[a, s:s + 1, qs]
                rows[s].append(
                    jnp.where(sel[s:s + 1, :] < N_BLOCKS, p, TRASH_ROW0 + lane % SC_WINDOW))
        for s in range(MOBA_TOPK):
            pos_ref[s, :, qs] = jnp.concatenate(rows[s], axis=0)


def _route_pos_call(sel, rank, base_digits, *, blocks=8):
    width = blocks * MOBA_BLOCK
    return pl.pallas_call(
        functools.partial(_route_pos_kernel, blocks=blocks),
        grid=(N_BLOCKS // blocks,),
        in_specs=[
            pl.BlockSpec((SPLIT_HEADS, MOBA_TOPK, width), lambda b: (0, 0, b)),
            pl.BlockSpec((SPLIT_HEADS, MOBA_TOPK, width), lambda b: (0, 0, b)),
            pl.BlockSpec((SPLIT_HEADS, blocks, 8, N_BLOCKS), lambda b: (0, b, 0, 0)),
        ],
        out_specs=pl.BlockSpec((MOBA_TOPK, SPLIT_HEADS, width), lambda b: (0, 0, b)),
        out_shape=jax.ShapeDtypeStruct((MOBA_TOPK, SPLIT_HEADS, SEQ), I32),
        compiler_params=_params(1),
        name="moba_route_pos",
    )(sel, rank, base_digits)


def _route_tables(cnt, split):
    cnt = cnt.reshape(SPLIT_HEADS, N_BLOCKS, N_BLOCKS)
    tiles = (cnt.sum(axis=1) + ROUTE_TILE - 1) // ROUTE_TILE
    tiles_flat = tiles.reshape(N_GROUPS)
    tile0 = jnp.cumsum(tiles_flat) - tiles_flat
    n_tiles = tiles_flat.sum().astype(I32)
    within = jnp.cumsum(cnt, axis=1) - cnt
    base = tile0.reshape(SPLIT_HEADS, 1, N_BLOCKS) * ROUTE_TILE + within
    digits = jnp.stack([base // (DIGIT * DIGIT), (base // DIGIT) % DIGIT, base % DIGIT],
                       axis=2).astype(F32)
    digits = jnp.pad(digits, ((0, 0), (0, 0), (0, 8 - 3), (0, 0)))
    t = jnp.arange(MAX_TILES, dtype=I32)
    ended = (tile0 + tiles_flat)[None, :] <= t[:, None]
    group_of_tile = jnp.minimum(ended.sum(axis=1).astype(I32), N_GROUPS - 1)
    tile_h = group_of_tile // N_BLOCKS
    head = split * SPLIT_HEADS + tile_h
    tile_step = lax.bitcast_convert_type((127 + 7 - head) << 23, F32)
    return digits, tile_h, group_of_tile % N_BLOCKS, n_tiles.reshape(1), tile_step


def _sc_mesh():
    return plsc.VectorSubcoreMesh(core_axis_name="core", subcore_axis_name="subcore")


def _dispatch_call(rows, pos_by_slot, *, split):
    n_rows = SPLIT_HEADS * SEQ
    first_window = split * n_rows // SC_WINDOW

    @functools.partial(
        pl.kernel, mesh=_sc_mesh(), scratch_types=[],
        out_type=jax.ShapeDtypeStruct((ROUTE_ROWS, HEAD_PAD), F32))
    def dispatch(x_hbm, i0_hbm, i1_hbm, i2_hbm, o_hbm):
        def body(x_vmem, i0_vmem, i1_vmem, i2_vmem):
            pltpu.sync_copy(x_vmem, o_hbm.at[i0_vmem.at[0]])
            pltpu.sync_copy(x_vmem, o_hbm.at[i1_vmem.at[0]])
            pltpu.sync_copy(x_vmem, o_hbm.at[i2_vmem.at[0]])

        idx_spec = pl.BlockSpec((1, SC_WINDOW), lambda i: (0, i))
        pltpu.emit_pipeline(
            body,
            grid=(n_rows // SC_WINDOW,),
            in_specs=[pl.BlockSpec((SC_WINDOW, HEAD_PAD), lambda i: (first_window + i, 0)),
                      idx_spec, idx_spec, idx_spec],
            out_specs=[],
            core_axis_name=("core", "subcore"),
            dimension_semantics=(pltpu.PARALLEL,),
        )(x_hbm, i0_hbm, i1_hbm, i2_hbm)

    return dispatch(rows, *pos_by_slot)


def _collect_call(table, idx):
    n_rows = idx.shape[1]

    @functools.partial(
        pl.kernel, mesh=_sc_mesh(), scratch_types=[],
        out_type=jax.ShapeDtypeStruct((n_rows, HEAD_PAD), F32))
    def collect(x_hbm, i_hbm, o_hbm):
        def body(i_vmem, o_vmem):
            pltpu.sync_copy(x_hbm.at[i_vmem.at[0]], o_vmem)

        pltpu.emit_pipeline(
            body,
            grid=(n_rows // SC_WINDOW,),
            in_specs=[pl.BlockSpec((1, SC_WINDOW), lambda i: (0, i))],
            out_specs=[pl.BlockSpec((SC_WINDOW, HEAD_PAD), lambda i: (i, 0))],
            core_axis_name=("core", "subcore"),
            dimension_semantics=(pltpu.PARALLEL,),
        )(i_hbm, o_hbm)

    return collect(table, idx)


def _routed_attn_kernel(th_ref, tj_ref, nt_ref, step_ref, q_ref, *refs, tiles):
    kv_refs, o_ref = refs[0:tiles], refs[tiles]
    t0 = pl.program_id(0) * tiles

    @pl.when(t0 < nt_ref[0])
    def _():
        qs = [q_ref[u * ROUTE_TILE:(u + 1) * ROUTE_TILE, :].astype(BF16) for u in range(tiles)]
        adds = [step_ref[t0 + u] * tj_ref[t0 + u].astype(F32) for u in range(tiles)]
        o_ref[...] = _block_rows(qs, kv_refs, None, adds)

    @pl.when(t0 == MAX_TILES)
    def _():
        low = lax.broadcasted_iota(I32, o_ref.shape, 1) < HEAD_DIM
        o_ref[...] = jnp.where(low, 0.0, NEG_BIG)


def _routed_attn_call(tile_h, tile_j, n_tiles, tile_step, routed_q, kv,
                      *, split, tiles=ROUTE_TILES_PER_STEP):
    n_steps = MAX_TILES // tiles + 1
    assert TRASH_ROW0 == (n_steps - 1) * tiles * ROUTE_TILE
    live = lambda s, th, tj, nt: (jnp.minimum(s, (nt[0] + tiles - 1) // tiles), 0)
    live_or_last = lambda s, th, tj, nt: (
        jnp.where(s == n_steps - 1, s, jnp.minimum(s, (nt[0] + tiles - 1) // tiles)), 0)
    head0 = split * SPLIT_HEADS

    def kv_spec(u):
        def slab(s, th, tj, nt):
            t = jnp.minimum(s, n_steps - 2) * tiles + u
            return head0 + th[t], tj[t], 0, 0
        return pl.BlockSpec((None, None, SLAB_ROWS, MOBA_BLOCK), slab)

    grid_spec = pltpu.PrefetchScalarGridSpec(
        num_scalar_prefetch=3,
        grid=(n_steps,),
        in_specs=([pl.BlockSpec(memory_space=pltpu.SMEM),
                   pl.BlockSpec((tiles * ROUTE_TILE, HEAD_PAD), live)]
                  + [kv_spec(u) for u in range(tiles)]),
        out_specs=pl.BlockSpec((tiles * ROUTE_TILE, HEAD_PAD), live_or_last),
    )
    return pl.pallas_call(
        functools.partial(_routed_attn_kernel, tiles=tiles),
        grid_spec=grid_spec,
        out_shape=jax.ShapeDtypeStruct((ROUTE_ROWS, HEAD_PAD), F32),
        compiler_params=_params(1),
        name=f"moba_routed_attn_{split}",
    )(tile_h, tile_j, n_tiles, tile_step, routed_q, *([kv] * tiles))


def _pm_out_kernel(slope_ref, mod_ref, pooled_ref, *refs, blocks):
    own_refs, got_refs = refs[0:N_SPLITS], refs[N_SPLITS:2 * N_SPLITS]
    wp_ref, wa_ref, o_ref = refs[2 * N_SPLITS:]
    low = lax.broadcasted_iota(I32, (MOBA_BLOCK, HEAD_PAD), 1) < HEAD_DIM

    def lse_of(part):
        return jnp.where(low, pltpu.roll(part, HEAD_DIM, axis=1), part)

    y = jnp.dot(pooled_ref[...], wp_ref[...], preferred_element_type=F32)
    for hh in range(D_HEADS):
        own_ref, got_ref = own_refs[hh // SPLIT_HEADS], got_refs[hh // SPLIT_HEADS]
        a = hh % SPLIT_HEADS
        atts = []
        for bb in range(blocks):
            own = pl.program_id(0) * blocks + bb
            rows = slice(bb * MOBA_BLOCK, (bb + 1) * MOBA_BLOCK)
            own_shift = slope_ref[hh] * (own * MOBA_BLOCK).astype(F32)
            parts = [own_ref[a, rows, :]] + [got_ref[s, a, rows, :] for s in range(MOBA_TOPK)]
            lses = [lse_of(parts[0]) + own_shift] + [lse_of(part) for part in parts[1:]]
            top = functools.reduce(jnp.maximum, lses)
            num = jnp.zeros((MOBA_BLOCK, HEAD_PAD), F32)
            den = jnp.zeros((MOBA_BLOCK, HEAD_PAD), F32)
            for part, ls in zip(parts, lses):
                w = jnp.exp(ls - top)
                num = num + w * part
                den = den + w
            atts.append((num / den).astype(BF16))
        y = y + jnp.dot(jnp.concatenate(atts, axis=0), wa_ref[hh], preferred_element_type=F32)
    o_ref[...] = mod_ref[2:3, :] * y


def _pm_out_call(slopes, mod, pooled, own_parts, got_parts, wp, wa, *, layer, blocks=2):
    tm = blocks * MOBA_BLOCK
    return pl.pallas_call(
        functools.partial(_pm_out_kernel, blocks=blocks),
        grid=(SEQ // tm,),
        in_specs=[
            pl.BlockSpec(memory_space=pltpu.SMEM),
            _resident((None, None, 3, D_MODEL), lambda i: (layer, 1, 0, 0)),
            pl.BlockSpec((tm, C_WIDTH), lambda i: (i, 0)),
        ] + [pl.BlockSpec((SPLIT_HEADS, tm, HEAD_PAD), lambda i: (0, i, 0))] * N_SPLITS
        + [pl.BlockSpec((MOBA_TOPK, SPLIT_HEADS, tm, HEAD_PAD), lambda i: (0, 0, i, 0))] * N_SPLITS
        + [
            _full((C_WIDTH, D_MODEL)),
            _full((D_HEADS, HEAD_PAD, D_MODEL)),
        ],
        out_specs=pl.BlockSpec((tm, D_MODEL), lambda i: (i, 0)),
        out_shape=jax.ShapeDtypeStruct((SEQ, D_MODEL), F32),
        compiler_params=_params(1),
        name="pool_moba_out",
    )(slopes, mod, pooled, *own_parts, *got_parts, wp, wa)


def _head_pad(w, axis):
    shape = list(w.shape)
    shape[axis:axis + 1] = [D_HEADS, HEAD_DIM]
    pad = [(0, 0)] * len(shape)
    pad[axis + 1] = (0, HEAD_PAD - HEAD_DIM)
    out = jnp.pad(w.reshape(shape), pad)
    shape[axis:axis + 2] = [D_HEADS * HEAD_PAD]
    return out.reshape(shape)


def kernel(x, c, ada_w, ada_b, ffn_norm, ffn_w_gate, ffn_w_up, ffn_w_down, mix_norm, conv_w_in,
           conv_a_w, conv_a_b, conv_a_ln_g, conv_a_ln_b, conv_b_w, conv_w_out, pm_w_in, pool_w,
           pool_b, pool_scale, pm_w_out, final_norm):
    assert x.shape == (1, SEQ, D_MODEL) and c.shape == (1, D_MODEL)
    xs = x.reshape(SEQ, D_MODEL)
    mod = _mod_call(c, ada_w, ada_b)

    ffn_g = ffn_norm.reshape(DEPTH, 2, 1, D_MODEL)
    mix_g = mix_norm.reshape(DEPTH, 1, D_MODEL)
    fin = final_norm.reshape(1, D_MODEL)
    ffn_stacks = (ffn_w_gate, ffn_w_up, ffn_w_down)

    def ffn(xs, weights, layer, sub, delta=None):
        last = (layer, sub) == (DEPTH - 1, 1)
        after = None if last else ((layer + sub, 1 - sub), *ffn_stacks)
        return _ffn_call(xs, mod, ffn_g, weights, fin, after, layer=layer, sub=sub, final=last,
                         delta=delta)

    xs, ffn_w = ffn(xs, tuple(w[0, 0].astype(BF16) for w in ffn_stacks), layer=0, sub=0)
    grp = np.arange(A_WIDTH) // (A_WIDTH // A_GROUPS)
    seg = jnp.asarray((grp[:, None] == grp[None, :]) / (A_WIDTH // A_GROUPS), dtype=BF16)
    xs = _conv_mix_call(
        xs, mod, mix_g, conv_w_in[0].astype(BF16), conv_a_w[0], conv_a_b[0].reshape(1, A_WIDTH),
        conv_a_ln_g[0].reshape(1, A_WIDTH), conv_a_ln_b[0].reshape(1, A_WIDTH), conv_b_w[0],
        conv_w_out[0].astype(BF16), seg, layer=0)
    xs, ffn_w = ffn(xs, ffn_w, layer=0, sub=1)

    xs, ffn_w = ffn(xs, ffn_w, layer=1, sub=0)
    w_in = pm_w_in[0].astype(BF16)
    w_u = w_in[:, 0:C_WIDTH]
    w_q = w_in[:, C_WIDTH:C_WIDTH + D_WIDTH]
    w_kvt = w_in[:, C_WIDTH + D_WIDTH:C_WIDTH + 3 * D_WIDTH].T
    pooled, qrows, kv, kmean = _pm_proj_call(
        xs, mod, mix_g, w_u, w_q, w_kvt, pool_w[0].astype(BF16),
        pool_b[0].reshape(1, C_WIDTH), pool_scale[0].reshape(1, C_WIDTH), layer=1)
    kmean_h = _head_pad(kmean, 1).reshape(N_BLOCKS, D_HEADS, HEAD_PAD).transpose(1, 0, 2)

    n_pair = MOBA_TOPK * MOBA_BLOCK
    tri = jnp.asarray(np.arange(n_pair)[:, None] < np.arange(n_pair)[None, :], dtype=BF16)
    q_flat = qrows.reshape(D_HEADS * SEQ, HEAD_PAD)
    own_parts, got_parts = [], []
    anchor = jnp.zeros((1,), I32)
    for split in range(N_SPLITS):
        sel, rank, cnt, own_part = _gate_own_call(anchor, qrows, kv, kmean_h, tri, split=split)
        base_digits, tile_h, tile_j, n_tiles, tile_step = _route_tables(cnt, split)
        pos = _route_pos_call(sel, rank, base_digits)
        anchor = pos[0, 0, 0:1]
        pos_by_slot = [pos[s].reshape(1, SPLIT_HEADS * SEQ) for s in range(MOBA_TOPK)]
        routed_q = _dispatch_call(q_flat, pos_by_slot, split=split)
        routed_parts = _routed_attn_call(tile_h, tile_j, n_tiles, tile_step, routed_q, kv,
                                         split=split)
        got = _collect_call(routed_parts, pos.reshape(1, N_PAIRS))
        own_parts.append(own_part)
        got_parts.append(got.reshape(MOBA_TOPK, SPLIT_HEADS, SEQ, HEAD_PAD))

    slopes = jnp.asarray(2.0 ** (-8.0 * np.arange(1, D_HEADS + 1) / D_HEADS), dtype=F32)
    w_out = pm_w_out[0].astype(BF16)
    w_att = _head_pad(w_out[C_WIDTH:], 0).reshape(D_HEADS, HEAD_PAD, D_MODEL)
    mixed = _pm_out_call(slopes, mod, pooled, own_parts, got_parts, w_out[0:C_WIDTH], w_att,
                         layer=1)
    xs, _ = ffn(xs, ffn_w, layer=1, sub=1, delta=mixed)
    return xs.reshape(1, SEQ, D_MODEL)
```

```python
import functools

import numpy as np
import jax
import jax.numpy as jnp
from jax import lax
from jax.experimental import pallas as pl
from jax.experimental.pallas import tpu as pltpu
from jax.experimental.pallas import tpu_sc as plsc

D_MODEL = 1024
SEQ = 16384
DEPTH = 2
N_SUBLAYERS = 3
D_FF = 2816
EPS = 1e-6

A_WIDTH = 512
A_GROUPS = 8
A_CONV = 31
B_WIDTH = 512
B_CONV = 3

POOL_WINDOWS = (2, 4, 8, 16)
C_WIDTH = 512
C_GROUP_DIM = 128
D_HEADS = 8
HEAD_DIM = 64
D_WIDTH = 512
MOBA_BLOCK = 256
MOBA_TOPK = 3
N_BLOCKS = SEQ // MOBA_BLOCK

V7X_SUBLANES = 8
V7X_BF16_SUBLANES = 16
V7X_VMEM_LIMIT_BYTES = 56 * 1024 * 1024

A_HALO = 32
B_HALO = 8
POOL_HALO = 16
NEG_BIG = -1e30

HEAD_PAD = 128
ALIBI_COL = HEAD_DIM
SLAB_K_ROWS = HEAD_DIM + V7X_BF16_SUBLANES
SLAB_ROWS = SLAB_K_ROWS + HEAD_DIM

SPLIT_HEADS = 4
N_SPLITS = D_HEADS // SPLIT_HEADS
ROUTE_TILE = 256
ROUTE_TILES_PER_STEP = 32
GATE_BLOCKS_PER_STEP = 4
N_PAIRS = SPLIT_HEADS * SEQ * MOBA_TOPK
N_GROUPS = SPLIT_HEADS * N_BLOCKS
MAX_TILES = N_PAIRS // ROUTE_TILE + N_GROUPS
TRASH_ROW0 = MAX_TILES * ROUTE_TILE
ROUTE_ROWS = (MAX_TILES + ROUTE_TILES_PER_STEP) * ROUTE_TILE
SC_WINDOW = 128
DIGIT = 128
assert MAX_TILES % ROUTE_TILES_PER_STEP == 0 and N_BLOCKS % GATE_BLOCKS_PER_STEP == 0
assert ROUTE_ROWS <= DIGIT ** 3 and MOBA_BLOCK <= ROUTE_TILE

BF16 = jnp.bfloat16
F32 = jnp.float32
I32 = jnp.int32


def _params(n_axes):
    return pltpu.CompilerParams(
        dimension_semantics=("arbitrary",) * n_axes,
        vmem_limit_bytes=V7X_VMEM_LIMIT_BYTES)


def _resident(block_shape, index_map):
    return pl.BlockSpec(block_shape, index_map, pipeline_mode=pl.Buffered(1))


def _full(shape):
    return _resident(shape, lambda *_: (0,) * len(shape))


def _sigmoid(v):
    return 1.0 / (1.0 + jnp.exp(-v))


def _norm_mod(x, g, mod):
    r = lax.rsqrt(jnp.mean(x * x, axis=-1, keepdims=True) + EPS)
    return (x * r) * (g * (1.0 + mod[1:2])) + mod[0:1]


def _split(v):
    hi = v.astype(BF16)
    return hi, (v - hi.astype(F32)).astype(BF16)


def _split_dot(v, w):
    hi, lo = _split(v)
    return (jnp.dot(hi, w, preferred_element_type=F32)
            + jnp.dot(lo, w, preferred_element_type=F32))


_NT = (((1,), (1,)), ((), ()))


def _mod_kernel(c_ref, w_ref, b_ref, o_ref):
    c = c_ref[...]
    cond = c * _sigmoid(c)
    o_ref[0] = jnp.sum(w_ref[0] * cond, axis=0, keepdims=True) + b_ref[0]


def _mod_call(c, ada_w, ada_b):
    n_out = N_SUBLAYERS * 3 * D_MODEL
    tn = D_MODEL
    out = pl.pallas_call(
        _mod_kernel,
        grid=(DEPTH, n_out // tn),
        in_specs=[
            pl.BlockSpec((D_MODEL, 1), lambda l, j: (0, 0)),
            pl.BlockSpec((1, D_MODEL, tn), lambda l, j: (l, 0, j)),
            pl.BlockSpec((1, 1, tn), lambda l, j: (l, 0, j)),
        ],
        out_specs=pl.BlockSpec((1, 1, tn), lambda l, j: (l, 0, j)),
        out_shape=jax.ShapeDtypeStruct((DEPTH, 1, n_out), F32),
        compiler_params=_params(2),
        name="adaln_mod",
    )(c.reshape(D_MODEL, 1), ada_w, ada_b.reshape(DEPTH, 1, n_out))
    return out.reshape(DEPTH, N_SUBLAYERS, 3, D_MODEL)


def _ffn_kernel(x_ref, *refs, final, with_delta):
    if with_delta:
        d_ref, *refs = refs
    mod_ref, g_ref, wg_ref, wu_ref, wd_ref, fin_ref, *refs = refs
    o_ref = refs[len(refs) // 2]
    for src_ref, dst_ref in zip(refs[:len(refs) // 2], refs[len(refs) // 2 + 1:]):
        dst_ref[...] = src_ref[...].astype(BF16)
    x = x_ref[...]
    if with_delta:
        x = x + d_ref[...]
    mod = mod_ref[...]
    h = _norm_mod(x, g_ref[...], mod).astype(BF16)
    gate = jnp.dot(h, wg_ref[...], preferred_element_type=F32)
    up = jnp.dot(h, wu_ref[...], preferred_element_type=F32)
    act = (gate * _sigmoid(gate) * up).astype(BF16)
    y = jnp.dot(act, wd_ref[...], preferred_element_type=F32)
    xn = x + (0.5 * mod[2:3]) * y
    if final:
        r = lax.rsqrt(jnp.mean(xn * xn, axis=-1, keepdims=True) + EPS)
        xn = xn * r * fin_ref[...]
    o_ref[...] = xn


def _ffn_call(x, mod, norm_g, weights, fin, next_weights, *, layer, sub, final, delta=None,
              tm=512):
    mod_sub = 0 if sub == 0 else 2
    steps = SEQ // tm
    rows_in = [x] if delta is None else [x, delta]
    in_specs = [pl.BlockSpec((tm, D_MODEL), lambda i: (i, 0)) for _ in rows_in] + [
        _resident((None, None, 3, D_MODEL), lambda i: (layer, mod_sub, 0, 0)),
        _resident((None, None, 1, D_MODEL), lambda i: (layer, sub, 0, 0)),
        _full((D_MODEL, D_FF)),
        _full((D_MODEL, D_FF)),
        _full((D_FF, D_MODEL)),
        _full((1, D_MODEL)),
    ]
    out_specs = [pl.BlockSpec((tm, D_MODEL), lambda i: (i, 0))]
    out_shape = [jax.ShapeDtypeStruct((SEQ, D_MODEL), F32)]
    operands = [*rows_in, mod, norm_g, *weights, fin]
    if next_weights is not None:
        (nl, ns), *stacks = next_weights
        for w in stacks:
            n_rows, n_cols = w.shape[2:]
            share = 1 if n_rows % (steps * V7X_BF16_SUBLANES) == 0 else 2
            rows = n_rows * share // steps
            assert n_rows * share % steps == 0 and rows % V7X_BF16_SUBLANES == 0
            in_specs.append(pl.BlockSpec((None, None, rows, n_cols),
                                         lambda i, share=share: (nl, ns, i // share, 0)))
            out_specs.append(pl.BlockSpec((rows, n_cols), lambda i, share=share: (i // share, 0)))
            out_shape.append(jax.ShapeDtypeStruct((n_rows, n_cols), BF16))
            operands.append(w)
    out = pl.pallas_call(
        functools.partial(_ffn_kernel, final=final, with_delta=delta is not None),
        grid=(steps,),
        in_specs=in_specs,
        out_specs=out_specs,
        out_shape=out_shape,
        compiler_params=_params(1),
        name=f"ffn_l{layer}s{sub}",
    )(*operands)
    return out[0], tuple(out[1:])


def _conv_mix_kernel(x_ref, mod_ref, g_ref, win_ref, caw_ref, cab_ref, lng_ref, lnb_ref,
                     cbw_ref, wout_ref, seg_ref, o_ref, abuf, bbuf, *, tm):
    i = pl.program_id(0)

    @pl.when(i == 0)
    def _():
        abuf[0:A_HALO, :] = jnp.zeros((A_HALO, A_WIDTH), F32)
        bbuf[0:B_HALO, :] = jnp.zeros((B_HALO, B_WIDTH), F32)

    x = x_ref[...]
    mod = mod_ref[...]
    h = _norm_mod(x, g_ref[...], mod).astype(BF16)
    z = jnp.dot(h, win_ref[...], preferred_element_type=F32)

    a = z[:, 0:A_WIDTH] * _sigmoid(z[:, A_WIDTH:2 * A_WIDTH])
    abuf[A_HALO:A_HALO + tm, :] = a
    acc = jnp.zeros((tm, A_WIDTH), F32) + cab_ref[...]
    ext = tm + V7X_SUBLANES
    for r in range(V7X_SUBLANES):
        part = None
        for q in range((A_CONV - 1 - r) // V7X_SUBLANES + 1):
            k = A_CONV - 1 - (V7X_SUBLANES * q + r)
            term = caw_ref[k:k + 1, :] * abuf[pl.ds(A_HALO - V7X_SUBLANES * (q + 1), ext), :]
            part = term if part is None else part + term
        if r > 0:
            part = pltpu.roll(part, r, axis=0)
        acc = acc + part[V7X_SUBLANES:, :]
    abuf[0:A_HALO, :] = abuf[tm:tm + A_HALO, :]
    seg = seg_ref[...]
    mu = _split_dot(acc, seg)
    d = acc - mu
    var = _split_dot(d * d, seg)
    yn = d * lax.rsqrt(var + EPS) * lng_ref[...] + lnb_ref[...]
    a_out = yn * _sigmoid(yn)

    off = 2 * A_WIDTH
    cv = z[:, off + B_WIDTH:off + 2 * B_WIDTH] * z[:, off + 2 * B_WIDTH:off + 3 * B_WIDTH]
    bbuf[B_HALO:B_HALO + tm, :] = cv
    bacc = jnp.zeros((tm, B_WIDTH), F32)
    for k in range(B_CONV):
        bacc = bacc + cbw_ref[k:k + 1, :] * bbuf[pl.ds(B_HALO - (B_CONV - 1) + k, tm), :]
    bbuf[0:B_HALO, :] = bbuf[tm:tm + B_HALO, :]
    bb = z[:, off:off + B_WIDTH] * bacc

    y = (jnp.dot(a_out.astype(BF16), wout_ref[0:A_WIDTH, :], preferred_element_type=F32)
         + jnp.dot(bb.astype(BF16), wout_ref[A_WIDTH:A_WIDTH + B_WIDTH, :],
                   preferred_element_type=F32))
    o_ref[...] = x + mod[2:3] * y


def _conv_mix_call(x, mod, norm_g, w_in, caw, cab, lng, lnb, cbw, w_out, seg, *, layer, tm=512):
    even_in = w_in.shape[1]
    return pl.pallas_call(
        functools.partial(_conv_mix_kernel, tm=tm),
        grid=(SEQ // tm,),
        in_specs=[
            pl.BlockSpec((tm, D_MODEL), lambda i: (i, 0)),
            _resident((None, None, 3, D_MODEL), lambda i: (layer, 1, 0, 0)),
            _resident((None, 1, D_MODEL), lambda i: (layer, 0, 0)),
            _full((D_MODEL, even_in)),
            _full((A_CONV, A_WIDTH)),
            _full((1, A_WIDTH)),
            _full((1, A_WIDTH)),
            _full((1, A_WIDTH)),
            _full((B_CONV, B_WIDTH)),
            _full((A_WIDTH + B_WIDTH, D_MODEL)),
            _full((A_WIDTH, A_WIDTH)),
        ],
        out_specs=pl.BlockSpec((tm, D_MODEL), lambda i: (i, 0)),
        out_shape=jax.ShapeDtypeStruct((SEQ, D_MODEL), F32),
        scratch_shapes=[pltpu.VMEM((A_HALO + tm, A_WIDTH), F32),
                        pltpu.VMEM((B_HALO + tm, B_WIDTH), F32)],
        compiler_params=_params(1),
        name="conv_mixers",
    )(x, mod, norm_g, w_in, caw, cab, lng, lnb, cbw, w_out, seg)


def _pm_proj_kernel(x_ref, mod_ref, g_ref, wu_ref, wq_ref, wkvt_ref, pw_ref,
                    pb_ref, ps_ref, pooled_ref, qrow_ref, kv_ref, km_ref,
                    ubuf, hmean, *, tm):
    i = pl.program_id(0)

    @pl.when(i == 0)
    def _():
        ubuf[0:POOL_HALO, :] = jnp.zeros((POOL_HALO, C_WIDTH), F32)

    x = x_ref[...]
    h = _norm_mod(x, g_ref[...], mod_ref[...]).astype(BF16)
    u = jnp.dot(h, wu_ref[...], preferred_element_type=F32)
    qr = jnp.dot(h, wq_ref[...], preferred_element_type=F32)
    kvt = lax.dot_general(wkvt_ref[...], h, _NT, preferred_element_type=F32)

    ubuf[POOL_HALO:POOL_HALO + tm, :] = u
    t1 = (i * tm + 1 + lax.broadcasted_iota(I32, (tm, C_GROUP_DIM), 0)).astype(F32)
    for gi, w in enumerate(POOL_WINDOWS):
        c0 = gi * C_GROUP_DIM
        s = ubuf[:, c0:c0 + C_GROUP_DIM]
        span = 1
        while span < w:
            s = s + pltpu.roll(s, span, axis=0)
            span *= 2
        assert span == w < POOL_HALO + 1
        s = s[POOL_HALO:, :]
        pooled = s / jnp.minimum(t1, float(w)) - u[:, c0:c0 + C_GROUP_DIM]
        mixed = (jnp.dot(pooled.astype(BF16), pw_ref[gi], preferred_element_type=F32)
                 + pb_ref[:, c0:c0 + C_GROUP_DIM])
        pooled_ref[:, c0:c0 + C_GROUP_DIM] = (
            mixed * ps_ref[:, c0:c0 + C_GROUP_DIM]).astype(BF16)
    ubuf[0:POOL_HALO, :] = ubuf[tm:tm + POOL_HALO, :]

    lane = lax.broadcasted_iota(I32, (tm, HEAD_PAD), 1)
    tail = jnp.where(lane == ALIBI_COL, 1.0, 0.0)
    qs = qr * (HEAD_DIM ** -0.5)
    for pair in range(D_HEADS // 2):
        both = qs[:, pair * HEAD_PAD:(pair + 1) * HEAD_PAD]
        qrow_ref[2 * pair] = jnp.where(lane < HEAD_DIM, both, tail)
        qrow_ref[2 * pair + 1] = jnp.where(
            lane < HEAD_DIM, pltpu.roll(both, HEAD_DIM, axis=1), tail)
    per_tile = tm // MOBA_BLOCK
    kvb = kvt.astype(BF16)
    n_alibi = SLAB_K_ROWS - HEAD_DIM
    first = lax.broadcasted_iota(I32, (n_alibi, MOBA_BLOCK), 0) == 0
    koff = lax.broadcasted_iota(I32, (n_alibi, MOBA_BLOCK), 1).astype(F32)
    for hh in range(D_HEADS):
        alibi = jnp.where(first, koff * (2.0 ** -(hh + 1)), 0.0).astype(BF16)
        for sb in range(per_tile):
            keys = slice(sb * MOBA_BLOCK, (sb + 1) * MOBA_BLOCK)
            kv_ref[hh, sb, 0:HEAD_DIM, :] = kvb[hh * HEAD_DIM:(hh + 1) * HEAD_DIM, keys]
            kv_ref[hh, sb, HEAD_DIM:SLAB_K_ROWS, :] = alibi
            kv_ref[hh, sb, SLAB_K_ROWS:SLAB_ROWS, :] = kvb[
                D_WIDTH + hh * HEAD_DIM:D_WIDTH + (hh + 1) * HEAD_DIM, keys]

    for sb in range(per_tile):
        blk = h[sb * MOBA_BLOCK:(sb + 1) * MOBA_BLOCK, :].astype(F32)
        hmean[pl.ds(i * per_tile + sb, 1), :] = jnp.mean(blk, axis=0, keepdims=True)

    @pl.when(i == pl.num_programs(0) - 1)
    def _():
        hi, lo = _split(hmean[...])
        wkt = wkvt_ref[0:D_WIDTH, :]
        km_ref[...] = (lax.dot_general(hi, wkt, _NT, preferred_element_type=F32)
                       + lax.dot_general(lo, wkt, _NT, preferred_element_type=F32))


def _pm_proj_call(x, mod, norm_g, wu, wq, wkvt, pw, pb, ps, *, layer, tm=512):
    return pl.pallas_call(
        functools.partial(_pm_proj_kernel, tm=tm),
        grid=(SEQ // tm,),
        in_specs=[
            pl.BlockSpec((tm, D_MODEL), lambda i: (i, 0)),
            _resident((None, None, 3, D_MODEL), lambda i: (layer, 1, 0, 0)),
            _resident((None, 1, D_MODEL), lambda i: (layer, 0, 0)),
            _full((D_MODEL, C_WIDTH)),
            _full((D_MODEL, D_WIDTH)),
            _full((2 * D_WIDTH, D_MODEL)),
            _full((len(POOL_WINDOWS), C_GROUP_DIM, C_GROUP_DIM)),
            _full((1, C_WIDTH)),
            _full((1, C_WIDTH)),
        ],
        out_specs=[
            pl.BlockSpec((tm, C_WIDTH), lambda i: (i, 0)),
            pl.BlockSpec((D_HEADS, tm, HEAD_PAD), lambda i: (0, i, 0)),
            pl.BlockSpec((D_HEADS, tm // MOBA_BLOCK, SLAB_ROWS, MOBA_BLOCK),
                         lambda i: (0, i, 0, 0)),
            pl.BlockSpec((N_BLOCKS, D_WIDTH), lambda i: (0, 0)),
        ],
        out_shape=[
            jax.ShapeDtypeStruct((SEQ, C_WIDTH), BF16),
            jax.ShapeDtypeStruct((D_HEADS, SEQ, HEAD_PAD), F32),
            jax.ShapeDtypeStruct((D_HEADS, N_BLOCKS, SLAB_ROWS, MOBA_BLOCK), BF16),
            jax.ShapeDtypeStruct((N_BLOCKS, D_WIDTH), F32),
        ],
        scratch_shapes=[pltpu.VMEM((POOL_HALO + tm, C_WIDTH), F32),
                        pltpu.VMEM((N_BLOCKS, D_MODEL), F32)],
        compiler_params=_params(1),
        name="pool_qkv_proj",
    )(x, mod, norm_g, wu, wq, wkvt, pw, pb, ps)


def _block_rows(qs, slabs, mask, lse_add):
    n = qs[0].shape[0]
    kpad = jnp.zeros((HEAD_PAD - SLAB_K_ROWS, MOBA_BLOCK), BF16)
    s = jnp.concatenate(
        [jnp.dot(q, jnp.concatenate([slab[0:SLAB_K_ROWS, :], kpad], axis=0),
                 preferred_element_type=F32)
         for q, slab in zip(qs, slabs)], axis=0)
    if mask is not None:
        s = jnp.where(mask, s, NEG_BIG)
    m = jnp.max(s, axis=1, keepdims=True)
    p = jnp.exp(s - m).astype(BF16)
    vpad = jnp.zeros((HEAD_PAD - HEAD_DIM, MOBA_BLOCK), BF16)
    ones = jnp.ones((HEAD_PAD, MOBA_BLOCK), BF16)
    ot = jnp.concatenate(
        [lax.dot_general(p[u * n:(u + 1) * n],
                         jnp.concatenate([slab[SLAB_K_ROWS:SLAB_ROWS, :], vpad, ones], axis=0),
                         _NT, preferred_element_type=F32)
         for u, slab in enumerate(slabs)], axis=0)
    o, tot = ot[:, 0:HEAD_PAD], ot[:, HEAD_PAD:2 * HEAD_PAD]
    lse = m + jnp.log(tot)
    if lse_add is not None:
        lse = jnp.concatenate(
            [lse[u * n:(u + 1) * n] + add for u, add in enumerate(lse_add)], axis=0)
    low = lax.broadcasted_iota(I32, o.shape, 1) < HEAD_DIM
    return jnp.where(low, o / tot, lse)


def _onehot_pairs(sel_rows):
    blk = lax.broadcasted_iota(I32, (N_BLOCKS, MOBA_BLOCK), 0)
    return jnp.concatenate(
        [jnp.where(blk == sel_rows[s], 1.0, 0.0) for s in range(MOBA_TOPK)], axis=1)


def _gate_own_kernel(anchor_ref, qrow_ref, kv_ref, km_ref, tri_ref,
                     sel_ref, rank_ref, cnt_ref, own_ref, *, heads, blocks):
    del anchor_ref
    n_pair = MOBA_TOPK * MOBA_BLOCK
    tiles = [(bb, a) for bb in range(blocks) for a in range(heads)]
    rows_of = lambda bb: slice(bb * MOBA_BLOCK, (bb + 1) * MOBA_BLOCK)
    qs = [qrow_ref[a, rows_of(bb), :].astype(BF16) for bb, a in tiles]

    kms = [_split(km_ref[a]) for a in range(heads)]
    gates = [lax.dot_general(kms[a][0], q, _NT, preferred_element_type=F32)
             + lax.dot_general(kms[a][1], q, _NT, preferred_element_type=F32)
             for (bb, a), q in zip(tiles, qs)]
    width = len(tiles) * MOBA_BLOCK
    blk = lax.broadcasted_iota(I32, (N_BLOCKS, width), 0)
    lane = lax.broadcasted_iota(I32, (1, width), 1)
    own = pl.program_id(0) * blocks + lane // (heads * MOBA_BLOCK)
    g = jnp.where(blk < own, jnp.concatenate(gates, axis=1), -jnp.inf)
    slots = []
    for s in range(MOBA_TOPK):
        best = jnp.max(g, axis=0, keepdims=True)
        idx = jnp.min(jnp.where(g == best, blk, N_BLOCKS), axis=0, keepdims=True)
        g = jnp.where(blk == idx, -jnp.inf, g)
        slots.append(jnp.where(s < own, idx, N_BLOCKS))

    onehots = []
    for t, (bb, a) in enumerate(tiles):
        sel_rows = [slot[:, t * MOBA_BLOCK:(t + 1) * MOBA_BLOCK] for slot in slots]
        sel_ref[a, :, rows_of(bb)] = jnp.concatenate(sel_rows, axis=0)
        onehots.append(_onehot_pairs(sel_rows))
    onehot = jnp.concatenate(onehots, axis=0)
    oh = onehot.astype(BF16)
    before = jnp.dot(oh, tri_ref[...], preferred_element_type=F32)
    hit = onehot * before
    ones = jnp.ones((8, n_pair), BF16)
    cnt = lax.dot_general(ones, oh, _NT, preferred_element_type=F32)
    for t, (bb, a) in enumerate(tiles):
        rank = jnp.sum(hit[t * N_BLOCKS:(t + 1) * N_BLOCKS], axis=0, keepdims=True).astype(I32)
        rank_ref[a, :, rows_of(bb)] = jnp.concatenate(
            [rank[:, s * MOBA_BLOCK:(s + 1) * MOBA_BLOCK] for s in range(MOBA_TOPK)], axis=0)
        cnt_ref[a, bb] = cnt[0:1, t * N_BLOCKS:(t + 1) * N_BLOCKS].astype(I32)

    qi = lax.broadcasted_iota(I32, (width, MOBA_BLOCK), 0) % MOBA_BLOCK
    ki = lax.broadcasted_iota(I32, (width, MOBA_BLOCK), 1)
    rows = _block_rows(qs, [kv_ref.at[a, bb] for bb, a in tiles], ki <= qi, None)
    for t, (bb, a) in enumerate(tiles):
        own_ref[a, rows_of(bb), :] = rows[t * MOBA_BLOCK:(t + 1) * MOBA_BLOCK]


def _gate_own_call(anchor, qrows, kv, kmean_h, tri, *, split, blocks=GATE_BLOCKS_PER_STEP):
    heads = SPLIT_HEADS
    rows = blocks * MOBA_BLOCK
    return pl.pallas_call(
        functools.partial(_gate_own_kernel, heads=heads, blocks=blocks),
        grid=(N_BLOCKS // blocks,),
        in_specs=[
            pl.BlockSpec(memory_space=pltpu.SMEM),
            pl.BlockSpec((heads, rows, HEAD_PAD), lambda b: (split, b, 0)),
            pl.BlockSpec((heads, blocks, SLAB_ROWS, MOBA_BLOCK), lambda b: (split, b, 0, 0)),
            pl.BlockSpec((heads, N_BLOCKS, HEAD_PAD), lambda b: (split, 0, 0)),
            _full((MOBA_TOPK * MOBA_BLOCK, MOBA_TOPK * MOBA_BLOCK)),
        ],
        out_specs=[
            pl.BlockSpec((heads, MOBA_TOPK, rows), lambda b: (0, 0, b)),
            pl.BlockSpec((heads, MOBA_TOPK, rows), lambda b: (0, 0, b)),
            pl.BlockSpec((heads, blocks, 1, N_BLOCKS), lambda b: (0, b, 0, 0)),
            pl.BlockSpec((heads, rows, HEAD_PAD), lambda b: (0, b, 0)),
        ],
        out_shape=[
            jax.ShapeDtypeStruct((heads, MOBA_TOPK, SEQ), I32),
            jax.ShapeDtypeStruct((heads, MOBA_TOPK, SEQ), I32),
            jax.ShapeDtypeStruct((heads, N_BLOCKS, 1, N_BLOCKS), I32),
            jax.ShapeDtypeStruct((heads, SEQ, HEAD_PAD), F32),
        ],
        compiler_params=_params(1),
        name=f"moba_gate_own_{split}",
    )(anchor, qrows, kv, kmean_h, tri)


def _route_pos_kernel(sel_ref, rank_ref, base_ref, pos_ref, *, blocks):
    lane = lax.broadcasted_iota(I32, (1, MOBA_BLOCK), 1)
    for bb in range(blocks):
        qs = slice(bb * MOBA_BLOCK, (bb + 1) * MOBA_BLOCK)
        rows = [[] for _ in range(MOBA_TOPK)]
        for a in range(SPLIT_HEADS):
            sel = sel_ref[a, :, qs]
            onehot = _onehot_pairs([sel[s:s + 1, :] for s in range(MOBA_TOPK)]).astype(BF16)
            dig = jnp.dot(base_ref[a, bb].astype(BF16), onehot,
                          preferred_element_type=F32)
            base = (dig[0:1] * float(DIGIT * DIGIT) + dig[1:2] * float(DIGIT)
                    + dig[2:3]).astype(I32)
            for s in range(MOBA_TOPK):
                p = base[:, s * MOBA_BLOCK:(s + 1) * MOBA_BLOCK] + rank_ref[a, s:s + 1, qs]
                rows[s].append(
                    jnp.where(sel[s:s + 1, :] < N_BLOCKS, p, TRASH_ROW0 + lane % SC_WINDOW))
        for s in range(MOBA_TOPK):
            pos_ref[s, :, qs] = jnp.concatenate(rows[s], axis=0)


def _route_pos_call(sel, rank, base_digits, *, blocks=8):
    width = blocks * MOBA_BLOCK
    return pl.pallas_call(
        functools.partial(_route_pos_kernel, blocks=blocks),
        grid=(N_BLOCKS // blocks,),
        in_specs=[
            pl.BlockSpec((SPLIT_HEADS, MOBA_TOPK, width), lambda b: (0, 0, b)),
            pl.BlockSpec((SPLIT_HEADS, MOBA_TOPK, width), lambda b: (0, 0, b)),
            pl.BlockSpec((SPLIT_HEADS, blocks, 8, N_BLOCKS), lambda b: (0, b, 0, 0)),
        ],
        out_specs=pl.BlockSpec((MOBA_TOPK, SPLIT_HEADS, width), lambda b: (0, 0, b)),
        out_shape=jax.ShapeDtypeStruct((MOBA_TOPK, SPLIT_HEADS, SEQ), I32),
        compiler_params=_params(1),
        name="moba_route_pos",
    )(sel, rank, base_digits)


def _route_tables(cnt, split):
    cnt = cnt.reshape(SPLIT_HEADS, N_BLOCKS, N_BLOCKS)
    tiles = (cnt.sum(axis=1) + ROUTE_TILE - 1) // ROUTE_TILE
    tiles_flat = tiles.reshape(N_GROUPS)
    tile0 = jnp.cumsum(tiles_flat) - tiles_flat
    n_tiles = tiles_flat.sum().astype(I32)
    within = jnp.cumsum(cnt, axis=1) - cnt
    base = tile0.reshape(SPLIT_HEADS, 1, N_BLOCKS) * ROUTE_TILE + within
    digits = jnp.stack([base // (DIGIT * DIGIT), (base // DIGIT) % DIGIT, base % DIGIT],
                       axis=2).astype(F32)
    digits = jnp.pad(digits, ((0, 0), (0, 0), (0, 8 - 3), (0, 0)))
    t = jnp.arange(MAX_TILES, dtype=I32)
    ended = (tile0 + tiles_flat)[None, :] <= t[:, None]
    group_of_tile = jnp.minimum(ended.sum(axis=1).astype(I32), N_GROUPS - 1)
    tile_h = group_of_tile // N_BLOCKS
    head = split * SPLIT_HEADS + tile_h
    tile_step = lax.bitcast_convert_type((127 + 7 - head) << 23, F32)
    return digits, tile_h, group_of_tile % N_BLOCKS, n_tiles.reshape(1), tile_step


def _sc_mesh():
    return plsc.VectorSubcoreMesh(core_axis_name="core", subcore_axis_name="subcore")


def _dispatch_call(rows, pos_by_slot, *, split):
    n_rows = SPLIT_HEADS * SEQ
    first_window = split * n_rows // SC_WINDOW

    @functools.partial(
        pl.kernel, mesh=_sc_mesh(), scratch_types=[],
        out_type=jax.ShapeDtypeStruct((ROUTE_ROWS, HEAD_PAD), F32))
    def dispatch(x_hbm, i0_hbm, i1_hbm, i2_hbm, o_hbm):
        def body(x_vmem, i0_vmem, i1_vmem, i2_vmem):
            pltpu.sync_copy(x_vmem, o_hbm.at[i0_vmem.at[0]])
            pltpu.sync_copy(x_vmem, o_hbm.at[i1_vmem.at[0]])
            pltpu.sync_copy(x_vmem, o_hbm.at[i2_vmem.at[0]])

        idx_spec = pl.BlockSpec((1, SC_WINDOW), lambda i: (0, i))
        pltpu.emit_pipeline(
            body,
            grid=(n_rows // SC_WINDOW,),
            in_specs=[pl.BlockSpec((SC_WINDOW, HEAD_PAD), lambda i: (first_window + i, 0)),
                      idx_spec, idx_spec, idx_spec],
            out_specs=[],
            core_axis_name=("core", "subcore"),
            dimension_semantics=(pltpu.PARALLEL,),
        )(x_hbm, i0_hbm, i1_hbm, i2_hbm)

    return dispatch(rows, *pos_by_slot)


def _collect_call(table, idx):
    n_rows = idx.shape[1]

    @functools.partial(
        pl.kernel, mesh=_sc_mesh(), scratch_types=[],
        out_type=jax.ShapeDtypeStruct((n_rows, HEAD_PAD), F32))
    def collect(x_hbm, i_hbm, o_hbm):
        def body(i_vmem, o_vmem):
            pltpu.sync_copy(x_hbm.at[i_vmem.at[0]], o_vmem)

        pltpu.emit_pipeline(
            body,
            grid=(n_rows // SC_WINDOW,),
            in_specs=[pl.BlockSpec((1, SC_WINDOW), lambda i: (0, i))],
            out_specs=[pl.BlockSpec((SC_WINDOW, HEAD_PAD), lambda i: (i, 0))],
            core_axis_name=("core", "subcore"),
            dimension_semantics=(pltpu.PARALLEL,),
        )(i_hbm, o_hbm)

    return collect(table, idx)


def _routed_attn_kernel(th_ref, tj_ref, nt_ref, step_ref, q_ref, *refs, tiles):
    kv_refs, o_ref = refs[0:tiles], refs[tiles]
    t0 = pl.program_id(0) * tiles

    @pl.when(t0 < nt_ref[0])
    def _():
        qs = [q_ref[u * ROUTE_TILE:(u + 1) * ROUTE_TILE, :].astype(BF16) for u in range(tiles)]
        adds = [step_ref[t0 + u] * tj_ref[t0 + u].astype(F32) for u in range(tiles)]
        o_ref[...] = _block_rows(qs, kv_refs, None, adds)

    @pl.when(t0 == MAX_TILES)
    def _():
        low = lax.broadcasted_iota(I32, o_ref.shape, 1) < HEAD_DIM
        o_ref[...] = jnp.where(low, 0.0, NEG_BIG)


def _routed_attn_call(tile_h, tile_j, n_tiles, tile_step, routed_q, kv,
                      *, split, tiles=ROUTE_TILES_PER_STEP):
    n_steps = MAX_TILES // tiles + 1
    assert TRASH_ROW0 == (n_steps - 1) * tiles * ROUTE_TILE
    live = lambda s, th, tj, nt: (jnp.minimum(s, (nt[0] + tiles - 1) // tiles), 0)
    live_or_last = lambda s, th, tj, nt: (
        jnp.where(s == n_steps - 1, s, jnp.minimum(s, (nt[0] + tiles - 1) // tiles)), 0)
    head0 = split * SPLIT_HEADS

    def kv_spec(u):
        def slab(s, th, tj, nt):
            t = jnp.minimum(s, n_steps - 2) * tiles + u
            return head0 + th[t], tj[t], 0, 0
        return pl.BlockSpec((None, None, SLAB_ROWS, MOBA_BLOCK), slab)

    grid_spec = pltpu.PrefetchScalarGridSpec(
        num_scalar_prefetch=3,
        grid=(n_steps,),
        in_specs=([pl.BlockSpec(memory_space=pltpu.SMEM),
                   pl.BlockSpec((tiles * ROUTE_TILE, HEAD_PAD), live)]
                  + [kv_spec(u) for u in range(tiles)]),
        out_specs=pl.BlockSpec((tiles * ROUTE_TILE, HEAD_PAD), live_or_last),
    )
    return pl.pallas_call(
        functools.partial(_routed_attn_kernel, tiles=tiles),
        grid_spec=grid_spec,
        out_shape=jax.ShapeDtypeStruct((ROUTE_ROWS, HEAD_PAD), F32),
        compiler_params=_params(1),
        name=f"moba_routed_attn_{split}",
    )(tile_h, tile_j, n_tiles, tile_step, routed_q, *([kv] * tiles))


def _pm_out_kernel(slope_ref, mod_ref, pooled_ref, *refs, blocks):
    own_refs, got_refs = refs[0:N_SPLITS], refs[N_SPLITS:2 * N_SPLITS]
    wp_ref, wa_ref, o_ref = refs[2 * N_SPLITS:]
    low = lax.broadcasted_iota(I32, (MOBA_BLOCK, HEAD_PAD), 1) < HEAD_DIM

    def lse_of(part):
        return jnp.where(low, pltpu.roll(part, HEAD_DIM, axis=1), part)

    y = jnp.dot(pooled_ref[...], wp_ref[...], preferred_element_type=F32)
    for hh in range(D_HEADS):
        own_ref, got_ref = own_refs[hh // SPLIT_HEADS], got_refs[hh // SPLIT_HEADS]
        a = hh % SPLIT_HEADS
        atts = []
        for bb in range(blocks):
            own = pl.program_id(0) * blocks + bb
            rows = slice(bb * MOBA_BLOCK, (bb + 1) * MOBA_BLOCK)
            own_shift = slope_ref[hh] * (own * MOBA_BLOCK).astype(F32)
            parts = [own_ref[a, rows, :]] + [got_ref[s, a, rows, :] for s in range(MOBA_TOPK)]
            lses = [lse_of(parts[0]) + own_shift] + [lse_of(part) for part in parts[1:]]
            top = functools.reduce(jnp.maximum, lses)
            num = jnp.zeros((MOBA_BLOCK, HEAD_PAD), F32)
            den = jnp.zeros((MOBA_BLOCK, HEAD_PAD), F32)
            for part, ls in zip(parts, lses):
                w = jnp.exp(ls - top)
                num = num + w * part
                den = den + w
            atts.append((num / den).astype(BF16))
        y = y + jnp.dot(jnp.concatenate(atts, axis=0), wa_ref[hh], preferred_element_type=F32)
    o_ref[...] = mod_ref[2:3, :] * y


def _pm_out_call(slopes, mod, pooled, own_parts, got_parts, wp, wa, *, layer, blocks=2):
    tm = blocks * MOBA_BLOCK
    return pl.pallas_call(
        functools.partial(_pm_out_kernel, blocks=blocks),
        grid=(SEQ // tm,),
        in_specs=[
            pl.BlockSpec(memory_space=pltpu.SMEM),
            _resident((None, None, 3, D_MODEL), lambda i: (layer, 1, 0, 0)),
            pl.BlockSpec((tm, C_WIDTH), lambda i: (i, 0)),
        ] + [pl.BlockSpec((SPLIT_HEADS, tm, HEAD_PAD), lambda i: (0, i, 0))] * N_SPLITS
        + [pl.BlockSpec((MOBA_TOPK, SPLIT_HEADS, tm, HEAD_PAD), lambda i: (0, 0, i, 0))] * N_SPLITS
        + [
            _full((C_WIDTH, D_MODEL)),
            _full((D_HEADS, HEAD_PAD, D_MODEL)),
        ],
        out_specs=pl.BlockSpec((tm, D_MODEL), lambda i: (i, 0)),
        out_shape=jax.ShapeDtypeStruct((SEQ, D_MODEL), F32),
        compiler_params=_params(1),
        name="pool_moba_out",
    )(slopes, mod, pooled, *own_parts, *got_parts, wp, wa)


def _head_pad(w, axis):
    shape = list(w.shape)
    shape[axis:axis + 1] = [D_HEADS, HEAD_DIM]
    pad = [(0, 0)] * len(shape)
    pad[axis + 1] = (0, HEAD_PAD - HEAD_DIM)
    out = jnp.pad(w.reshape(shape), pad)
    shape[axis:axis + 2] = [D_HEADS * HEAD_PAD]
    return out.reshape(shape)


def kernel(x, c, ada_w, ada_b, ffn_norm, ffn_w_gate, ffn_w_up, ffn_w_down, mix_norm, conv_w_in,
           conv_a_w, conv_a_b, conv_a_ln_g, conv_a_ln_b, conv_b_w, conv_w_out, pm_w_in, pool_w,
           pool_b, pool_scale, pm_w_out, final_norm):
    assert x.shape == (1, SEQ, D_MODEL) and c.shape == (1, D_MODEL)
    xs = x.reshape(SEQ, D_MODEL)
    mod = _mod_call(c, ada_w, ada_b)

    ffn_g = ffn_norm.reshape(DEPTH, 2, 1, D_MODEL)
    mix_g = mix_norm.reshape(DEPTH, 1, D_MODEL)
    fin = final_norm.reshape(1, D_MODEL)
    ffn_stacks = (ffn_w_gate, ffn_w_up, ffn_w_down)

    def ffn(xs, weights, layer, sub, delta=None):
        last = (layer, sub) == (DEPTH - 1, 1)
        after = None if last else ((layer + sub, 1 - sub), *ffn_stacks)
        return _ffn_call(xs, mod, ffn_g, weights, fin, after, layer=layer, sub=sub, final=last,
                         delta=delta)

    xs, ffn_w = ffn(xs, tuple(w[0, 0].astype(BF16) for w in ffn_stacks), layer=0, sub=0)
    grp = np.arange(A_WIDTH) // (A_WIDTH // A_GROUPS)
    seg = jnp.asarray((grp[:, None] == grp[None, :]) / (A_WIDTH // A_GROUPS), dtype=BF16)
    xs = _conv_mix_call(
        xs, mod, mix_g, conv_w_in[0].astype(BF16), conv_a_w[0], conv_a_b[0].reshape(1, A_WIDTH),
        conv_a_ln_g[0].reshape(1, A_WIDTH), conv_a_ln_b[0].reshape(1, A_WIDTH), conv_b_w[0],
        conv_w_out[0].astype(BF16), seg, layer=0)
    xs, ffn_w = ffn(xs, ffn_w, layer=0, sub=1)

    xs, ffn_w = ffn(xs, ffn_w, layer=1, sub=0)
    w_in = pm_w_in[0].astype(BF16)
    w_u = w_in[:, 0:C_WIDTH]
    w_q = w_in[:, C_WIDTH:C_WIDTH + D_WIDTH]
    w_kvt = w_in[:, C_WIDTH + D_WIDTH:C_WIDTH + 3 * D_WIDTH].T
    pooled, qrows, kv, kmean = _pm_proj_call(
        xs, mod, mix_g, w_u, w_q, w_kvt, pool_w[0].astype(BF16),
        pool_b[0].reshape(1, C_WIDTH), pool_scale[0].reshape(1, C_WIDTH), layer=1)
    kmean_h = _head_pad(kmean, 1).reshape(N_BLOCKS, D_HEADS, HEAD_PAD).transpose(1, 0, 2)

    n_pair = MOBA_TOPK * MOBA_BLOCK
    tri = jnp.asarray(np.arange(n_pair)[:, None] < np.arange(n_pair)[None, :], dtype=BF16)
    q_flat = qrows.reshape(D_HEADS * SEQ, HEAD_PAD)
    own_parts, got_parts = [], []
    anchor = jnp.zeros((1,), I32)
    for split in range(N_SPLITS):
        sel, rank, cnt, own_part = _gate_own_call(anchor, qrows, kv, kmean_h, tri, split=split)
        base_digits, tile_h, tile_j, n_tiles, tile_step = _route_tables(cnt, split)
        pos = _route_pos_call(sel, rank, base_digits)
        anchor = pos[0, 0, 0:1]
        pos_by_slot = [pos[s].reshape(1, SPLIT_HEADS * SEQ) for s in range(MOBA_TOPK)]
        routed_q = _dispatch_call(q_flat, pos_by_slot, split=split)
        routed_parts = _routed_attn_call(tile_h, tile_j, n_tiles, tile_step, routed_q, kv,
                                         split=split)
        got = _collect_call(routed_parts, pos.reshape(1, N_PAIRS))
        own_parts.append(own_part)
        got_parts.append(got.reshape(MOBA_TOPK, SPLIT_HEADS, SEQ, HEAD_PAD))

    slopes = jnp.asarray(2.0 ** (-8.0 * np.arange(1, D_HEADS + 1) / D_HEADS), dtype=F32)
    w_out = pm_w_out[0].astype(BF16)
    w_att = _head_pad(w_out[C_WIDTH:], 0).reshape(D_HEADS, HEAD_PAD, D_MODEL)
    mixed = _pm_out_call(slopes, mod, pooled, own_parts, got_parts, w_out[0:C_WIDTH], w_att,
                         layer=1)
    xs, _ = ffn(xs, ffn_w, layer=1, sub=1, delta=mixed)
    return xs.reshape(1, SEQ, D_MODEL)
```

```python
import functools

import numpy as np
import jax
import jax.numpy as jnp
from jax import lax
from jax.experimental import pallas as pl
from jax.experimental.pallas import tpu as pltpu
from jax.experimental.pallas import tpu_sc as plsc

D_MODEL = 1024
SEQ = 16384
DEPTH = 2
N_SUBLAYERS = 3
D_FF = 2816
EPS = 1e-6

A_WIDTH = 512
A_GROUPS = 8
A_CONV = 31
B_WIDTH = 512
B_CONV = 3

POOL_WINDOWS = (2, 4, 8, 16)
C_WIDTH = 512
C_GROUP_DIM = 128
D_HEADS = 8
HEAD_DIM = 64
D_WIDTH = 512
MOBA_BLOCK = 256
MOBA_TOPK = 3
N_BLOCKS = SEQ // MOBA_BLOCK

V7X_SUBLANES = 8
V7X_BF16_SUBLANES = 16
V7X_VMEM_LIMIT_BYTES = 56 * 1024 * 1024

A_HALO = 32
B_HALO = 8
POOL_HALO = 16
NEG_BIG = -1e30

HEAD_PAD = 128
ALIBI_COL = HEAD_DIM
SLAB_K_ROWS = HEAD_DIM + V7X_BF16_SUBLANES
SLAB_ROWS = SLAB_K_ROWS + HEAD_DIM

SPLIT_HEADS = 4
N_SPLITS = D_HEADS // SPLIT_HEADS
ROUTE_TILE = 256
ROUTE_TILES_PER_STEP = 32
GATE_BLOCKS_PER_STEP = 4
N_PAIRS = SPLIT_HEADS * SEQ * MOBA_TOPK
N_GROUPS = SPLIT_HEADS * N_BLOCKS
MAX_TILES = N_PAIRS // ROUTE_TILE + N_GROUPS
TRASH_ROW0 = MAX_TILES * ROUTE_TILE
ROUTE_ROWS = (MAX_TILES + ROUTE_TILES_PER_STEP) * ROUTE_TILE
SC_WINDOW = 128
DIGIT = 128
assert MAX_TILES % ROUTE_TILES_PER_STEP == 0 and N_BLOCKS % GATE_BLOCKS_PER_STEP == 0
assert ROUTE_ROWS <= DIGIT ** 3 and MOBA_BLOCK <= ROUTE_TILE

BF16 = jnp.bfloat16
F32 = jnp.float32
I32 = jnp.int32


def _params(n_axes):
    return pltpu.CompilerParams(
        dimension_semantics=("arbitrary",) * n_axes,
        vmem_limit_bytes=V7X_VMEM_LIMIT_BYTES)


def _resident(block_shape, index_map):
    return pl.BlockSpec(block_shape, index_map, pipeline_mode=pl.Buffered(1))


def _full(shape):
    return _resident(shape, lambda *_: (0,) * len(shape))


def _sigmoid(v):
    return 1.0 / (1.0 + jnp.exp(-v))


def _norm_mod(x, g, mod):
    r = lax.rsqrt(jnp.mean(x * x, axis=-1, keepdims=True) + EPS)
    return (x * r) * (g * (1.0 + mod[1:2])) + mod[0:1]


def _split(v):
    hi = v.astype(BF16)
    return hi, (v - hi.astype(F32)).astype(BF16)


def _split_dot(v, w):
    hi, lo = _split(v)
    return (jnp.dot(hi, w, preferred_element_type=F32)
            + jnp.dot(lo, w, preferred_element_type=F32))


_NT = (((1,), (1,)), ((), ()))


def _mod_kernel(c_ref, w_ref, b_ref, o_ref):
    c = c_ref[...]
    cond = c * _sigmoid(c)
    o_ref[0] = jnp.sum(w_ref[0] * cond, axis=0, keepdims=True) + b_ref[0]


def _mod_call(c, ada_w, ada_b):
    n_out = N_SUBLAYERS * 3 * D_MODEL
    tn = D_MODEL
    out = pl.pallas_call(
        _mod_kernel,
        grid=(DEPTH, n_out // tn),
        in_specs=[
            pl.BlockSpec((D_MODEL, 1), lambda l, j: (0, 0)),
            pl.BlockSpec((1, D_MODEL, tn), lambda l, j: (l, 0, j)),
            pl.BlockSpec((1, 1, tn), lambda l, j: (l, 0, j)),
        ],
        out_specs=pl.BlockSpec((1, 1, tn), lambda l, j: (l, 0, j)),
        out_shape=jax.ShapeDtypeStruct((DEPTH, 1, n_out), F32),
        compiler_params=_params(2),
        name="adaln_mod",
    )(c.reshape(D_MODEL, 1), ada_w, ada_b.reshape(DEPTH, 1, n_out))
    return out.reshape(DEPTH, N_SUBLAYERS, 3, D_MODEL)


def _ffn_kernel(x_ref, *refs, final, with_delta):
    if with_delta:
        d_ref, *refs = refs
    mod_ref, g_ref, wg_ref, wu_ref, wd_ref, fin_ref, *refs = refs
    o_ref = refs[len(refs) // 2]
    for src_ref, dst_ref in zip(refs[:len(refs) // 2], refs[len(refs) // 2 + 1:]):
        dst_ref[...] = src_ref[...].astype(BF16)
    x = x_ref[...]
    if with_delta:
        x = x + d_ref[...]
    mod = mod_ref[...]
    h = _norm_mod(x, g_ref[...], mod).astype(BF16)
    gate = jnp.dot(h, wg_ref[...], preferred_element_type=F32)
    up = jnp.dot(h, wu_ref[...], preferred_element_type=F32)
    act = (gate * _sigmoid(gate) * up).astype(BF16)
    y = jnp.dot(act, wd_ref[...], preferred_element_type=F32)
    xn = x + (0.5 * mod[2:3]) * y
    if final:
        r = lax.rsqrt(jnp.mean(xn * xn, axis=-1, keepdims=True) + EPS)
        xn = xn * r * fin_ref[...]
    o_ref[...] = xn


def _ffn_call(x, mod, norm_g, weights, fin, next_weights, *, layer, sub, final, delta=None,
              tm=512):
    mod_sub = 0 if sub == 0 else 2
    steps = SEQ // tm
    rows_in = [x] if delta is None else [x, delta]
    in_specs = [pl.BlockSpec((tm, D_MODEL), lambda i: (i, 0)) for _ in rows_in] + [
        _resident((None, None, 3, D_MODEL), lambda i: (layer, mod_sub, 0, 0)),
        _resident((None, None, 1, D_MODEL), lambda i: (layer, sub, 0, 0)),
        _full((D_MODEL, D_FF)),
        _full((D_MODEL, D_FF)),
        _full((D_FF, D_MODEL)),
        _full((1, D_MODEL)),
    ]
    out_specs = [pl.BlockSpec((tm, D_MODEL), lambda i: (i, 0))]
    out_shape = [jax.ShapeDtypeStruct((SEQ, D_MODEL), F32)]
    operands = [*rows_in, mod, norm_g, *weights, fin]
    if next_weights is not None:
        (nl, ns), *stacks = next_weights
        for w in stacks:
            n_rows, n_cols = w.shape[2:]
            share = 1 if n_rows % (steps * V7X_BF16_SUBLANES) == 0 else 2
            rows = n_rows * share // steps
            assert n_rows * share % steps == 0 and rows % V7X_BF16_SUBLANES == 0
            in_specs.append(pl.BlockSpec((None, None, rows, n_cols),
                                         lambda i, share=share: (nl, ns, i // share, 0)))
            out_specs.append(pl.BlockSpec((rows, n_cols), lambda i, share=share: (i // share, 0)))
            out_shape.append(jax.ShapeDtypeStruct((n_rows, n_cols), BF16))
            operands.append(w)
    out = pl.pallas_call(
        functools.partial(_ffn_kernel, final=final, with_delta=delta is not None),
        grid=(steps,),
        in_specs=in_specs,
        out_specs=out_specs,
        out_shape=out_shape,
        compiler_params=_params(1),
        name=f"ffn_l{layer}s{sub}",
    )(*operands)
    return out[0], tuple(out[1:])


def _conv_mix_kernel(x_ref, mod_ref, g_ref, win_ref, caw_ref, cab_ref, lng_ref, lnb_ref,
                     cbw_ref, wout_ref, seg_ref, o_ref, abuf, bbuf, *, tm):
    i = pl.program_id(0)

    @pl.when(i == 0)
    def _():
        abuf[0:A_HALO, :] = jnp.zeros((A_HALO, A_WIDTH), F32)
        bbuf[0:B_HALO, :] = jnp.zeros((B_HALO, B_WIDTH), F32)

    x = x_ref[...]
    mod = mod_ref[...]
    h = _norm_mod(x, g_ref[...], mod).astype(BF16)
    z = jnp.dot(h, win_ref[...], preferred_element_type=F32)

    a = z[:, 0:A_WIDTH] * _sigmoid(z[:, A_WIDTH:2 * A_WIDTH])
    abuf[A_HALO:A_HALO + tm, :] = a
    acc = jnp.zeros((tm, A_WIDTH), F32) + cab_ref[...]
    ext = tm + V7X_SUBLANES
    for r in range(V7X_SUBLANES):
        part = None
        for q in range((A_CONV - 1 - r) // V7X_SUBLANES + 1):
            k = A_CONV - 1 - (V7X_SUBLANES * q + r)
            term = caw_ref[k:k + 1, :] * abuf[pl.ds(A_HALO - V7X_SUBLANES * (q + 1), ext), :]
            part = term if part is None else part + term
        if r > 0:
            part = pltpu.roll(part, r, axis=0)
        acc = acc + part[V7X_SUBLANES:, :]
    abuf[0:A_HALO, :] = abuf[tm:tm + A_HALO, :]
    seg = seg_ref[...]
    mu = _split_dot(acc, seg)
    d = acc - mu
    var = _split_dot(d * d, seg)
    yn = d * lax.rsqrt(var + EPS) * lng_ref[...] + lnb_ref[...]
    a_out = yn * _sigmoid(yn)

    off = 2 * A_WIDTH
    cv = z[:, off + B_WIDTH:off + 2 * B_WIDTH] * z[:, off + 2 * B_WIDTH:off + 3 * B_WIDTH]
    bbuf[B_HALO:B_HALO + tm, :] = cv
    bacc = jnp.zeros((tm, B_WIDTH), F32)
    for k in range(B_CONV):
        bacc = bacc + cbw_ref[k:k + 1, :] * bbuf[pl.ds(B_HALO - (B_CONV - 1) + k, tm), :]
    bbuf[0:B_HALO, :] = bbuf[tm:tm + B_HALO, :]
    bb = z[:, off:off + B_WIDTH] * bacc

    y = (jnp.dot(a_out.astype(BF16), wout_ref[0:A_WIDTH, :], preferred_element_type=F32)
         + jnp.dot(bb.astype(BF16), wout_ref[A_WIDTH:A_WIDTH + B_WIDTH, :],
                   preferred_element_type=F32))
    o_ref[...] = x + mod[2:3] * y


def _conv_mix_call(x, mod, norm_g, w_in, caw, cab, lng, lnb, cbw, w_out, seg, *, layer, tm=512):
    even_in = w_in.shape[1]
    return pl.pallas_call(
        functools.partial(_conv_mix_kernel, tm=tm),
        grid=(SEQ // tm,),
        in_specs=[
            pl.BlockSpec((tm, D_MODEL), lambda i: (i, 0)),
            _resident((None, None, 3, D_MODEL), lambda i: (layer, 1, 0, 0)),
            _resident((None, 1, D_MODEL), lambda i: (layer, 0, 0)),
            _full((D_MODEL, even_in)),
            _full((A_CONV, A_WIDTH)),
            _full((1, A_WIDTH)),
            _full((1, A_WIDTH)),
            _full((1, A_WIDTH)),
            _full((B_CONV, B_WIDTH)),
            _full((A_WIDTH + B_WIDTH, D_MODEL)),
            _full((A_WIDTH, A_WIDTH)),
        ],
        out_specs=pl.BlockSpec((tm, D_MODEL), lambda i: (i, 0)),
        out_shape=jax.ShapeDtypeStruct((SEQ, D_MODEL), F32),
        scratch_shapes=[pltpu.VMEM((A_HALO + tm, A_WIDTH), F32),
                        pltpu.VMEM((B_HALO + tm, B_WIDTH), F32)],
        compiler_params=_params(1),
        name="conv_mixers",
    )(x, mod, norm_g, w_in, caw, cab, lng, lnb, cbw, w_out, seg)


def _pm_proj_kernel(x_ref, mod_ref, g_ref, wu_ref, wq_ref, wkvt_ref, pw_ref,
                    pb_ref, ps_ref, pooled_ref, qrow_ref, kv_ref, km_ref,
                    ubuf, hmean, *, tm):
    i = pl.program_id(0)

    @pl.when(i == 0)
    def _():
        ubuf[0:POOL_HALO, :] = jnp.zeros((POOL_HALO, C_WIDTH), F32)

    x = x_ref[...]
    h = _norm_mod(x, g_ref[...], mod_ref[...]).astype(BF16)
    u = jnp.dot(h, wu_ref[...], preferred_element_type=F32)
    qr = jnp.dot(h, wq_ref[...], preferred_element_type=F32)
    kvt = lax.dot_general(wkvt_ref[...], h, _NT, preferred_element_type=F32)

    ubuf[POOL_HALO:POOL_HALO + tm, :] = u
    t1 = (i * tm + 1 + lax.broadcasted_iota(I32, (tm, C_GROUP_DIM), 0)).astype(F32)
    for gi, w in enumerate(POOL_WINDOWS):
        c0 = gi * C_GROUP_DIM
        s = ubuf[:, c0:c0 + C_GROUP_DIM]
        span = 1
        while span < w:
            s = s + pltpu.roll(s, span, axis=0)
            span *= 2
        assert span == w < POOL_HALO + 1
        s = s[POOL_HALO:, :]
        pooled = s / jnp.minimum(t1, float(w)) - u[:, c0:c0 + C_GROUP_DIM]
        mixed = (jnp.dot(pooled.astype(BF16), pw_ref[gi], preferred_element_type=F32)
                 + pb_ref[:, c0:c0 + C_GROUP_DIM])
        pooled_ref[:, c0:c0 + C_GROUP_DIM] = (
            mixed * ps_ref[:, c0:c0 + C_GROUP_DIM]).astype(BF16)
    ubuf[0:POOL_HALO, :] = ubuf[tm:tm + POOL_HALO, :]

    lane = lax.broadcasted_iota(I32, (tm, HEAD_PAD), 1)
    tail = jnp.where(lane == ALIBI_COL, 1.0, 0.0)
    qs = qr * (HEAD_DIM ** -0.5)
    for pair in range(D_HEADS // 2):
        both = qs[:, pair * HEAD_PAD:(pair + 1) * HEAD_PAD]
        qrow_ref[2 * pair] = jnp.where(lane < HEAD_DIM, both, tail)
        qrow_ref[2 * pair + 1] = jnp.where(
            lane < HEAD_DIM, pltpu.roll(both, HEAD_DIM, axis=1), tail)
    per_tile = tm // MOBA_BLOCK
    kvb = kvt.astype(BF16)
    n_alibi = SLAB_K_ROWS - HEAD_DIM
    first = lax.broadcasted_iota(I32, (n_alibi, MOBA_BLOCK), 0) == 0
    koff = lax.broadcasted_iota(I32, (n_alibi, MOBA_BLOCK), 1).astype(F32)
    for hh in range(D_HEADS):
        alibi = jnp.where(first, koff * (2.0 ** -(hh + 1)), 0.0).astype(BF16)
        for sb in range(per_tile):
            keys = slice(sb * MOBA_BLOCK, (sb + 1) * MOBA_BLOCK)
            kv_ref[hh, sb, 0:HEAD_DIM, :] = kvb[hh * HEAD_DIM:(hh + 1) * HEAD_DIM, keys]
            kv_ref[hh, sb, HEAD_DIM:SLAB_K_ROWS, :] = alibi
            kv_ref[hh, sb, SLAB_K_ROWS:SLAB_ROWS, :] = kvb[
                D_WIDTH + hh * HEAD_DIM:D_WIDTH + (hh + 1) * HEAD_DIM, keys]

    for sb in range(per_tile):
        blk = h[sb * MOBA_BLOCK:(sb + 1) * MOBA_BLOCK, :].astype(F32)
        hmean[pl.ds(i * per_tile + sb, 1), :] = jnp.mean(blk, axis=0, keepdims=True)

    @pl.when(i == pl.num_programs(0) - 1)
    def _():
        hi, lo = _split(hmean[...])
        wkt = wkvt_ref[0:D_WIDTH, :]
        km_ref[...] = (lax.dot_general(hi, wkt, _NT, preferred_element_type=F32)
                       + lax.dot_general(lo, wkt, _NT, preferred_element_type=F32))


def _pm_proj_call(x, mod, norm_g, wu, wq, wkvt, pw, pb, ps, *, layer, tm=512):
    return pl.pallas_call(
        functools.partial(_pm_proj_kernel, tm=tm),
        grid=(SEQ // tm,),
        in_specs=[
            pl.BlockSpec((tm, D_MODEL), lambda i: (i, 0)),
            _resident((None, None, 3, D_MODEL), lambda i: (layer, 1, 0, 0)),
            _resident((None, 1, D_MODEL), lambda i: (layer, 0, 0)),
            _full((D_MODEL, C_WIDTH)),
            _full((D_MODEL, D_WIDTH)),
            _full((2 * D_WIDTH, D_MODEL)),
            _full((len(POOL_WINDOWS), C_GROUP_DIM, C_GROUP_DIM)),
            _full((1, C_WIDTH)),
            _full((1, C_WIDTH)),
        ],
        out_specs=[
            pl.BlockSpec((tm, C_WIDTH), lambda i: (i, 0)),
            pl.BlockSpec((D_HEADS, tm, HEAD_PAD), lambda i: (0, i, 0)),
            pl.BlockSpec((D_HEADS, tm // MOBA_BLOCK, SLAB_ROWS, MOBA_BLOCK),
                         lambda i: (0, i, 0, 0)),
            pl.BlockSpec((N_BLOCKS, D_WIDTH), lambda i: (0, 0)),
        ],
        out_shape=[
            jax.ShapeDtypeStruct((SEQ, C_WIDTH), BF16),
            jax.ShapeDtypeStruct((D_HEADS, SEQ, HEAD_PAD), F32),
            jax.ShapeDtypeStruct((D_HEADS, N_BLOCKS, SLAB_ROWS, MOBA_BLOCK), BF16),
            jax.ShapeDtypeStruct((N_BLOCKS, D_WIDTH), F32),
        ],
        scratch_shapes=[pltpu.VMEM((POOL_HALO + tm, C_WIDTH), F32),
                        pltpu.VMEM((N_BLOCKS, D_MODEL), F32)],
        compiler_params=_params(1),
        name="pool_qkv_proj",
    )(x, mod, norm_g, wu, wq, wkvt, pw, pb, ps)


def _block_rows(qs, slabs, mask, lse_add, n_keys=MOBA_BLOCK):
    n = qs[0].shape[0]
    kpad = jnp.zeros((HEAD_PAD - SLAB_K_ROWS, n_keys), BF16)
    s = jnp.concatenate(
        [jnp.dot(q, jnp.concatenate([slab[0:SLAB_K_ROWS, 0:n_keys], kpad], axis=0),
                 preferred_element_type=F32)
         for q, slab in zip(qs, slabs)], axis=0)
    if mask is not None:
        s = jnp.where(mask, s, NEG_BIG)
    m = jnp.max(s, axis=1, keepdims=True)
    p = jnp.exp(s - m).astype(BF16)
    vpad = jnp.zeros((HEAD_PAD - HEAD_DIM, n_keys), BF16)
    ones = jnp.ones((HEAD_PAD, n_keys), BF16)
    ot = jnp.concatenate(
        [lax.dot_general(p[u * n:(u + 1) * n],
                         jnp.concatenate([slab[SLAB_K_ROWS:SLAB_ROWS, 0:n_keys], vpad, ones],
                                         axis=0),
                         _NT, preferred_element_type=F32)
         for u, slab in enumerate(slabs)], axis=0)
    o, tot = ot[:, 0:HEAD_PAD], ot[:, HEAD_PAD:2 * HEAD_PAD]
    lse = m + jnp.log(tot)
    if lse_add is not None:
        lse = jnp.concatenate(
            [lse[u * n:(u + 1) * n] + add for u, add in enumerate(lse_add)], axis=0)
    low = lax.broadcasted_iota(I32, o.shape, 1) < HEAD_DIM
    return jnp.where(low, o / tot, lse)


def _onehot_pairs(sel_rows):
    blk = lax.broadcasted_iota(I32, (N_BLOCKS, MOBA_BLOCK), 0)
    return jnp.concatenate(
        [jnp.where(blk == sel_rows[s], 1.0, 0.0) for s in range(MOBA_TOPK)], axis=1)


def _gate_own_kernel(anchor_ref, qrow_ref, kv_ref, km_ref, tri_ref,
                     sel_ref, rank_ref, cnt_ref, own_ref, *, heads, blocks):
    del anchor_ref
    n_pair = MOBA_TOPK * MOBA_BLOCK
    tiles = [(bb, a) for bb in range(blocks) for a in range(heads)]
    rows_of = lambda bb: slice(bb * MOBA_BLOCK, (bb + 1) * MOBA_BLOCK)
    qs = [qrow_ref[a, rows_of(bb), :].astype(BF16) for bb, a in tiles]

    kms = [_split(km_ref[a]) for a in range(heads)]
    gates = [lax.dot_general(kms[a][0], q, _NT, preferred_element_type=F32)
             + lax.dot_general(kms[a][1], q, _NT, preferred_element_type=F32)
             for (bb, a), q in zip(tiles, qs)]
    width = len(tiles) * MOBA_BLOCK
    blk = lax.broadcasted_iota(I32, (N_BLOCKS, width), 0)
    lane = lax.broadcasted_iota(I32, (1, width), 1)
    own = pl.program_id(0) * blocks + lane // (heads * MOBA_BLOCK)
    g = jnp.where(blk < own, jnp.concatenate(gates, axis=1), -jnp.inf)
    slots = []
    for s in range(MOBA_TOPK):
        best = jnp.max(g, axis=0, keepdims=True)
        idx = jnp.min(jnp.where(g == best, blk, N_BLOCKS), axis=0, keepdims=True)
        g = jnp.where(blk == idx, -jnp.inf, g)
        slots.append(jnp.where(s < own, idx, N_BLOCKS))

    onehots = []
    for t, (bb, a) in enumerate(tiles):
        sel_rows = [slot[:, t * MOBA_BLOCK:(t + 1) * MOBA_BLOCK] for slot in slots]
        sel_ref[a, :, rows_of(bb)] = jnp.concatenate(sel_rows, axis=0)
        onehots.append(_onehot_pairs(sel_rows))
    onehot = jnp.concatenate(onehots, axis=0)
    oh = onehot.astype(BF16)
    before = jnp.dot(oh, tri_ref[...], preferred_element_type=F32)
    hit = onehot * before
    ones = jnp.ones((8, n_pair), BF16)
    cnt = lax.dot_general(ones, oh, _NT, preferred_element_type=F32)
    for t, (bb, a) in enumerate(tiles):
        rank = jnp.sum(hit[t * N_BLOCKS:(t + 1) * N_BLOCKS], axis=0, keepdims=True).astype(I32)
        rank_ref[a, :, rows_of(bb)] = jnp.concatenate(
            [rank[:, s * MOBA_BLOCK:(s + 1) * MOBA_BLOCK] for s in range(MOBA_TOPK)], axis=0)
        cnt_ref[a, bb] = cnt[0:1, t * N_BLOCKS:(t + 1) * N_BLOCKS].astype(I32)

    half = MOBA_BLOCK // 2
    slabs = [kv_ref.at[a, bb] for bb, a in tiles]
    qi = lax.broadcasted_iota(I32, (width // 2, MOBA_BLOCK), 0) % half
    ki = lax.broadcasted_iota(I32, (width // 2, MOBA_BLOCK), 1)
    first = _block_rows([q[0:half] for q in qs], slabs, (ki <= qi)[:, 0:half], None, n_keys=half)
    second = _block_rows([q[half:] for q in qs], slabs, ki <= qi + half, None)
    for t, (bb, a) in enumerate(tiles):
        own_ref[a, pl.ds(bb * MOBA_BLOCK, half), :] = first[t * half:(t + 1) * half]
        own_ref[a, pl.ds(bb * MOBA_BLOCK + half, half), :] = second[t * half:(t + 1) * half]


def _gate_own_call(anchor, qrows, kv, kmean_h, tri, *, split, blocks=GATE_BLOCKS_PER_STEP):
    heads = SPLIT_HEADS
    rows = blocks * MOBA_BLOCK
    return pl.pallas_call(
        functools.partial(_gate_own_kernel, heads=heads, blocks=blocks),
        grid=(N_BLOCKS // blocks,),
        in_specs=[
            pl.BlockSpec(memory_space=pltpu.SMEM),
            pl.BlockSpec((heads, rows, HEAD_PAD), lambda b: (split, b, 0)),
            pl.BlockSpec((heads, blocks, SLAB_ROWS, MOBA_BLOCK), lambda b: (split, b, 0, 0)),
            pl.BlockSpec((heads, N_BLOCKS, HEAD_PAD), lambda b: (split, 0, 0)),
            _full((MOBA_TOPK * MOBA_BLOCK, MOBA_TOPK * MOBA_BLOCK)),
        ],
        out_specs=[
            pl.BlockSpec((heads, MOBA_TOPK, rows), lambda b: (0, 0, b)),
            pl.BlockSpec((heads, MOBA_TOPK, rows), lambda b: (0, 0, b)),
            pl.BlockSpec((heads, blocks, 1, N_BLOCKS), lambda b: (0, b, 0, 0)),
            pl.BlockSpec((heads, rows, HEAD_PAD), lambda b: (0, b, 0)),
        ],
        out_shape=[
            jax.ShapeDtypeStruct((heads, MOBA_TOPK, SEQ), I32),
            jax.ShapeDtypeStruct((heads, MOBA_TOPK, SEQ), I32),
            jax.ShapeDtypeStruct((heads, N_BLOCKS, 1, N_BLOCKS), I32),
            jax.ShapeDtypeStruct((heads, SEQ, HEAD_PAD), F32),
        ],
        compiler_params=_params(1),
        name=f"moba_gate_own_{split}",
    )(anchor, qrows, kv, kmean_h, tri)


def _route_pos_kernel(sel_ref, rank_ref, base_ref, pos_ref, *, blocks):
    lane = lax.broadcasted_iota(I32, (1, MOBA_BLOCK), 1)
    for bb in range(blocks):
        qs = slice(bb * MOBA_BLOCK, (bb + 1) * MOBA_BLOCK)
        rows = [[] for _ in range(MOBA_TOPK)]
        for a in range(SPLIT_HEADS):
            sel = sel_ref[a, :, qs]
            onehot = _onehot_pairs([sel[s:s + 1, :] for s in range(MOBA_TOPK)]).astype(BF16)
            dig = jnp.dot(base_ref[a, bb].astype(BF16), onehot,
                          preferred_element_type=F32)
            base = (dig[0:1] * float(DIGIT * DIGIT) + dig[1:2] * float(DIGIT)
                    + dig[2:3]).astype(I32)
            for s in range(MOBA_TOPK):
                p = base[:, s * MOBA_BLOCK:(s + 1) * MOBA_BLOCK] + rank_ref[a, s:s + 1, qs]
                rows[s].append(
                    jnp.where(sel[s:s + 1, :] < N_BLOCKS, p, TRASH_ROW0 + lane % SC_WINDOW))
        for s in range(MOBA_TOPK):
            pos_ref[s, :, qs] = jnp.concatenate(rows[s], axis=0)


def _route_pos_call(sel, rank, base_digits, *, blocks=8):
    width = blocks * MOBA_BLOCK
    return pl.pallas_call(
        functools.partial(_route_pos_kernel, blocks=blocks),
        grid=(N_BLOCKS // blocks,),
        in_specs=[
            pl.BlockSpec((SPLIT_HEADS, MOBA_TOPK, width), lambda b: (0, 0, b)),
            pl.BlockSpec((SPLIT_HEADS, MOBA_TOPK, width), lambda b: (0, 0, b)),
            pl.BlockSpec((SPLIT_HEADS, blocks, 8, N_BLOCKS), lambda b: (0, b, 0, 0)),
        ],
        out_specs=pl.BlockSpec((MOBA_TOPK, SPLIT_HEADS, width), lambda b: (0, 0, b)),
        out_shape=jax.ShapeDtypeStruct((MOBA_TOPK, SPLIT_HEADS, SEQ), I32),
        compiler_params=_params(1),
        name="moba_route_pos",
    )(sel, rank, base_digits)


def _route_tables(cnt, split):
    cnt = cnt.reshape(SPLIT_HEADS, N_BLOCKS, N_BLOCKS)
    tiles = (cnt.sum(axis=1) + ROUTE_TILE - 1) // ROUTE_TILE
    tiles_flat = tiles.reshape(N_GROUPS)
    tile0 = jnp.cumsum(tiles_flat) - tiles_flat
    n_tiles = tiles_flat.sum().astype(I32)
    within = jnp.cumsum(cnt, axis=1) - cnt
    base = tile0.reshape(SPLIT_HEADS, 1, N_BLOCKS) * ROUTE_TILE + within
    digits = jnp.stack([base // (DIGIT * DIGIT), (base // DIGIT) % DIGIT, base % DIGIT],
                       axis=2).astype(F32)
    digits = jnp.pad(digits, ((0, 0), (0, 0), (0, 8 - 3), (0, 0)))
    t = jnp.arange(MAX_TILES, dtype=I32)
    ended = (tile0 + tiles_flat)[None, :] <= t[:, None]
    group_of_tile = jnp.minimum(ended.sum(axis=1).astype(I32), N_GROUPS - 1)
    tile_h = group_of_tile // N_BLOCKS
    head = split * SPLIT_HEADS + tile_h
    tile_step = lax.bitcast_convert_type((127 + 7 - head) << 23, F32)
    return digits, tile_h, group_of_tile % N_BLOCKS, n_tiles.reshape(1), tile_step


def _sc_mesh():
    return plsc.VectorSubcoreMesh(core_axis_name="core", subcore_axis_name="subcore")


def _dispatch_call(rows, pos_by_slot, *, split):
    n_rows = SPLIT_HEADS * SEQ
    first_window = split * n_rows // SC_WINDOW

    @functools.partial(
        pl.kernel, mesh=_sc_mesh(), scratch_types=[],
        out_type=jax.ShapeDtypeStruct((ROUTE_ROWS, HEAD_PAD), F32))
    def dispatch(x_hbm, i0_hbm, i1_hbm, i2_hbm, o_hbm):
        def body(x_vmem, i0_vmem, i1_vmem, i2_vmem):
            pltpu.sync_copy(x_vmem, o_hbm.at[i0_vmem.at[0]])
            pltpu.sync_copy(x_vmem, o_hbm.at[i1_vmem.at[0]])
            pltpu.sync_copy(x_vmem, o_hbm.at[i2_vmem.at[0]])

        idx_spec = pl.BlockSpec((1, SC_WINDOW), lambda i: (0, i))
        pltpu.emit_pipeline(
            body,
            grid=(n_rows // SC_WINDOW,),
            in_specs=[pl.BlockSpec((SC_WINDOW, HEAD_PAD), lambda i: (first_window + i, 0)),
                      idx_spec, idx_spec, idx_spec],
            out_specs=[],
            core_axis_name=("core", "subcore"),
            dimension_semantics=(pltpu.PARALLEL,),
        )(x_hbm, i0_hbm, i1_hbm, i2_hbm)

    return dispatch(rows, *pos_by_slot)


def _collect_call(table, idx):
    n_rows = idx.shape[1]

    @functools.partial(
        pl.kernel, mesh=_sc_mesh(), scratch_types=[],
        out_type=jax.ShapeDtypeStruct((n_rows, HEAD_PAD), F32))
    def collect(x_hbm, i_hbm, o_hbm):
        def body(i_vmem, o_vmem):
            pltpu.sync_copy(x_hbm.at[i_vmem.at[0]], o_vmem)

        pltpu.emit_pipeline(
            body,
            grid=(n_rows // SC_WINDOW,),
            in_specs=[pl.BlockSpec((1, SC_WINDOW), lambda i: (0, i))],
            out_specs=[pl.BlockSpec((SC_WINDOW, HEAD_PAD), lambda i: (i, 0))],
            core_axis_name=("core", "subcore"),
            dimension_semantics=(pltpu.PARALLEL,),
        )(i_hbm, o_hbm)

    return collect(table, idx)


def _routed_attn_kernel(th_ref, tj_ref, nt_ref, step_ref, q_ref, *refs, tiles):
    kv_refs, o_ref = refs[0:tiles], refs[tiles]
    t0 = pl.program_id(0) * tiles

    @pl.when(t0 < nt_ref[0])
    def _():
        qs = [q_ref[u * ROUTE_TILE:(u + 1) * ROUTE_TILE, :].astype(BF16) for u in range(tiles)]
        adds = [step_ref[t0 + u] * tj_ref[t0 + u].astype(F32) for u in range(tiles)]
        o_ref[...] = _block_rows(qs, kv_refs, None, adds)

    @pl.when(t0 == MAX_TILES)
    def _():
        low = lax.broadcasted_iota(I32, o_ref.shape, 1) < HEAD_DIM
        o_ref[...] = jnp.where(low, 0.0, NEG_BIG)


def _routed_attn_call(tile_h, tile_j, n_tiles, tile_step, routed_q, kv,
                      *, split, tiles=ROUTE_TILES_PER_STEP):
    n_steps = MAX_TILES // tiles + 1
    assert TRASH_ROW0 == (n_steps - 1) * tiles * ROUTE_TILE
    live = lambda s, th, tj, nt: (jnp.minimum(s, (nt[0] + tiles - 1) // tiles), 0)
    live_or_last = lambda s, th, tj, nt: (
        jnp.where(s == n_steps - 1, s, jnp.minimum(s, (nt[0] + tiles - 1) // tiles)), 0)
    head0 = split * SPLIT_HEADS

    def kv_spec(u):
        def slab(s, th, tj, nt):
            t = jnp.minimum(s, n_steps - 2) * tiles + u
            return head0 + th[t], tj[t], 0, 0
        return pl.BlockSpec((None, None, SLAB_ROWS, MOBA_BLOCK), slab)

    grid_spec = pltpu.PrefetchScalarGridSpec(
        num_scalar_prefetch=3,
        grid=(n_steps,),
        in_specs=([pl.BlockSpec(memory_space=pltpu.SMEM),
                   pl.BlockSpec((tiles * ROUTE_TILE, HEAD_PAD), live)]
                  + [kv_spec(u) for u in range(tiles)]),
        out_specs=pl.BlockSpec((tiles * ROUTE_TILE, HEAD_PAD), live_or_last),
    )
    return pl.pallas_call(
        functools.partial(_routed_attn_kernel, tiles=tiles),
        grid_spec=grid_spec,
        out_shape=jax.ShapeDtypeStruct((ROUTE_ROWS, HEAD_PAD), F32),
        compiler_params=_params(1),
        name=f"moba_routed_attn_{split}",
    )(tile_h, tile_j, n_tiles, tile_step, routed_q, *([kv] * tiles))


def _pm_out_kernel(slope_ref, mod_ref, pooled_ref, *refs, blocks):
    own_refs, got_refs = refs[0:N_SPLITS], refs[N_SPLITS:2 * N_SPLITS]
    wp_ref, wa_ref, o_ref = refs[2 * N_SPLITS:]
    low = lax.broadcasted_iota(I32, (MOBA_BLOCK, HEAD_PAD), 1) < HEAD_DIM

    def lse_of(part):
        return jnp.where(low, pltpu.roll(part, HEAD_DIM, axis=1), part)

    y = jnp.dot(pooled_ref[...], wp_ref[...], preferred_element_type=F32)
    for hh in range(D_HEADS):
        own_ref, got_ref = own_refs[hh // SPLIT_HEADS], got_refs[hh // SPLIT_HEADS]
        a = hh % SPLIT_HEADS
        atts = []
        for bb in range(blocks):
            own = pl.program_id(0) * blocks + bb
            rows = slice(bb * MOBA_BLOCK, (bb + 1) * MOBA_BLOCK)
            own_shift = slope_ref[hh] * (own * MOBA_BLOCK).astype(F32)
            parts = [own_ref[a, rows, :]] + [got_ref[s, a, rows, :] for s in range(MOBA_TOPK)]
            lses = [lse_of(parts[0]) + own_shift] + [lse_of(part) for part in parts[1:]]
            top = functools.reduce(jnp.maximum, lses)
            num = jnp.zeros((MOBA_BLOCK, HEAD_PAD), F32)
            den = jnp.zeros((MOBA_BLOCK, HEAD_PAD), F32)
            for part, ls in zip(parts, lses):
                w = jnp.exp(ls - top)
                num = num + w * part
                den = den + w
            atts.append((num / den).astype(BF16))
        y = y + jnp.dot(jnp.concatenate(atts, axis=0), wa_ref[hh], preferred_element_type=F32)
    o_ref[...] = mod_ref[2:3, :] * y


def _pm_out_call(slopes, mod, pooled, own_parts, got_parts, wp, wa, *, layer, blocks=2):
    tm = blocks * MOBA_BLOCK
    return pl.pallas_call(
        functools.partial(_pm_out_kernel, blocks=blocks),
        grid=(SEQ // tm,),
        in_specs=[
            pl.BlockSpec(memory_space=pltpu.SMEM),
            _resident((None, None, 3, D_MODEL), lambda i: (layer, 1, 0, 0)),
            pl.BlockSpec((tm, C_WIDTH), lambda i: (i, 0)),
        ] + [pl.BlockSpec((SPLIT_HEADS, tm, HEAD_PAD), lambda i: (0, i, 0))] * N_SPLITS
        + [pl.BlockSpec((MOBA_TOPK, SPLIT_HEADS, tm, HEAD_PAD), lambda i: (0, 0, i, 0))] * N_SPLITS
        + [
            _full((C_WIDTH, D_MODEL)),
            _full((D_HEADS, HEAD_PAD, D_MODEL)),
        ],
        out_specs=pl.BlockSpec((tm, D_MODEL), lambda i: (i, 0)),
        out_shape=jax.ShapeDtypeStruct((SEQ, D_MODEL), F32),
        compiler_params=_params(1),
        name="pool_moba_out",
    )(slopes, mod, pooled, *own_parts, *got_parts, wp, wa)


def _head_pad(w, axis):
    shape = list(w.shape)
    shape[axis:axis + 1] = [D_HEADS, HEAD_DIM]
    pad = [(0, 0)] * len(shape)
    pad[axis + 1] = (0, HEAD_PAD - HEAD_DIM)
    out = jnp.pad(w.reshape(shape), pad)
    shape[axis:axis + 2] = [D_HEADS * HEAD_PAD]
    return out.reshape(shape)


def kernel(x, c, ada_w, ada_b, ffn_norm, ffn_w_gate, ffn_w_up, ffn_w_down, mix_norm, conv_w_in,
           conv_a_w, conv_a_b, conv_a_ln_g, conv_a_ln_b, conv_b_w, conv_w_out, pm_w_in, pool_w,
           pool_b, pool_scale, pm_w_out, final_norm):
    assert x.shape == (1, SEQ, D_MODEL) and c.shape == (1, D_MODEL)
    xs = x.reshape(SEQ, D_MODEL)
    mod = _mod_call(c, ada_w, ada_b)

    ffn_g = ffn_norm.reshape(DEPTH, 2, 1, D_MODEL)
    mix_g = mix_norm.reshape(DEPTH, 1, D_MODEL)
    fin = final_norm.reshape(1, D_MODEL)
    ffn_stacks = (ffn_w_gate, ffn_w_up, ffn_w_down)

    def ffn(xs, weights, layer, sub, delta=None):
        last = (layer, sub) == (DEPTH - 1, 1)
        after = None if last else ((layer + sub, 1 - sub), *ffn_stacks)
        return _ffn_call(xs, mod, ffn_g, weights, fin, after, layer=layer, sub=sub, final=last,
                         delta=delta)

    xs, ffn_w = ffn(xs, tuple(w[0, 0].astype(BF16) for w in ffn_stacks), layer=0, sub=0)
    grp = np.arange(A_WIDTH) // (A_WIDTH // A_GROUPS)
    seg = jnp.asarray((grp[:, None] == grp[None, :]) / (A_WIDTH // A_GROUPS), dtype=BF16)
    xs = _conv_mix_call(
        xs, mod, mix_g, conv_w_in[0].astype(BF16), conv_a_w[0], conv_a_b[0].reshape(1, A_WIDTH),
        conv_a_ln_g[0].reshape(1, A_WIDTH), conv_a_ln_b[0].reshape(1, A_WIDTH), conv_b_w[0],
        conv_w_out[0].astype(BF16), seg, layer=0)
    xs, ffn_w = ffn(xs, ffn_w, layer=0, sub=1)

    xs, ffn_w = ffn(xs, ffn_w, layer=1, sub=0)
    w_in = pm_w_in[0].astype(BF16)
    w_u = w_in[:, 0:C_WIDTH]
    w_q = w_in[:, C_WIDTH:C_WIDTH + D_WIDTH]
    w_kvt = w_in[:, C_WIDTH + D_WIDTH:C_WIDTH + 3 * D_WIDTH].T
    pooled, qrows, kv, kmean = _pm_proj_call(
        xs, mod, mix_g, w_u, w_q, w_kvt, pool_w[0].astype(BF16),
        pool_b[0].reshape(1, C_WIDTH), pool_scale[0].reshape(1, C_WIDTH), layer=1)
    kmean_h = _head_pad(kmean, 1).reshape(N_BLOCKS, D_HEADS, HEAD_PAD).transpose(1, 0, 2)

    n_pair = MOBA_TOPK * MOBA_BLOCK
    tri = jnp.asarray(np.arange(n_pair)[:, None] < np.arange(n_pair)[None, :], dtype=BF16)
    q_flat = qrows.reshape(D_HEADS * SEQ, HEAD_PAD)
    own_parts, got_parts = [], []
    anchor = jnp.zeros((1,), I32)
    for split in range(N_SPLITS):
        sel, rank, cnt, own_part = _gate_own_call(anchor, qrows, kv, kmean_h, tri, split=split)
        base_digits, tile_h, tile_j, n_tiles, tile_step = _route_tables(cnt, split)
        pos = _route_pos_call(sel, rank, base_digits)
        anchor = pos[0, 0, 0:1]
        pos_by_slot = [pos[s].reshape(1, SPLIT_HEADS * SEQ) for s in range(MOBA_TOPK)]
        routed_q = _dispatch_call(q_flat, pos_by_slot, split=split)
        routed_parts = _routed_attn_call(tile_h, tile_j, n_tiles, tile_step, routed_q, kv,
                                         split=split)
        got = _collect_call(routed_parts, pos.reshape(1, N_PAIRS))
        own_parts.append(own_part)
        got_parts.append(got.reshape(MOBA_TOPK, SPLIT_HEADS, SEQ, HEAD_PAD))

    slopes = jnp.asarray(2.0 ** (-8.0 * np.arange(1, D_HEADS + 1) / D_HEADS), dtype=F32)
    w_out = pm_w_out[0].astype(BF16)
    w_att = _head_pad(w_out[C_WIDTH:], 0).reshape(D_HEADS, HEAD_PAD, D_MODEL)
    mixed = _pm_out_call(slopes, mod, pooled, own_parts, got_parts, w_out[0:C_WIDTH], w_att,
                         layer=1)
    xs, _ = ffn(xs, ffn_w, layer=1, sub=1, delta=mixed)
    return xs.reshape(1, SEQ, D_MODEL)
```

```python
import functools

import numpy as np
import jax
import jax.numpy as jnp
from jax import lax
from jax.experimental import pallas as pl
from jax.experimental.pallas import tpu as pltpu
from jax.experimental.pallas import tpu_sc as plsc

D_MODEL = 1024
SEQ = 16384
DEPTH = 2
N_SUBLAYERS = 3
D_FF = 2816
EPS = 1e-6

A_WIDTH = 512
A_GROUPS = 8
A_CONV = 31
B_WIDTH = 512
B_CONV = 3

POOL_WINDOWS = (2, 4, 8, 16)
C_WIDTH = 512
C_GROUP_DIM = 128
D_HEADS = 8
HEAD_DIM = 64
D_WIDTH = 512
MOBA_BLOCK = 256
MOBA_TOPK = 3
N_BLOCKS = SEQ // MOBA_BLOCK

V7X_SUBLANES = 8
V7X_BF16_SUBLANES = 16
V7X_VMEM_LIMIT_BYTES = 56 * 1024 * 1024

A_HALO = 32
B_HALO = 8
POOL_HALO = 16
NEG_BIG = -1e30

HEAD_PAD = 128
ALIBI_COL = HEAD_DIM
SLAB_K_ROWS = HEAD_DIM + V7X_BF16_SUBLANES
SLAB_ROWS = SLAB_K_ROWS + HEAD_DIM

SPLIT_HEADS = 4
N_SPLITS = D_HEADS // SPLIT_HEADS
ROUTE_TILE = 256
ROUTE_TILES_PER_STEP = 32
GATE_BLOCKS_PER_STEP = 4
N_PAIRS = SPLIT_HEADS * SEQ * MOBA_TOPK
N_GROUPS = SPLIT_HEADS * N_BLOCKS
MAX_TILES = N_PAIRS // ROUTE_TILE + N_GROUPS
TRASH_ROW0 = MAX_TILES * ROUTE_TILE
ROUTE_ROWS = (MAX_TILES + ROUTE_TILES_PER_STEP) * ROUTE_TILE
SC_WINDOW = 128
DIGIT = 128
assert MAX_TILES % ROUTE_TILES_PER_STEP == 0 and N_BLOCKS % GATE_BLOCKS_PER_STEP == 0
assert ROUTE_ROWS <= DIGIT ** 3 and MOBA_BLOCK <= ROUTE_TILE

BF16 = jnp.bfloat16
F32 = jnp.float32
I32 = jnp.int32


def _params(n_axes):
    return pltpu.CompilerParams(
        dimension_semantics=("arbitrary",) * n_axes,
        vmem_limit_bytes=V7X_VMEM_LIMIT_BYTES)


def _resident(block_shape, index_map):
    return pl.BlockSpec(block_shape, index_map, pipeline_mode=pl.Buffered(1))


def _full(shape):
    return _resident(shape, lambda *_: (0,) * len(shape))


def _sigmoid(v):
    return 1.0 / (1.0 + jnp.exp(-v))


def _norm_mod(x, g, mod):
    r = lax.rsqrt(jnp.mean(x * x, axis=-1, keepdims=True) + EPS)
    return (x * r) * (g * (1.0 + mod[1:2])) + mod[0:1]


def _split(v):
    hi = v.astype(BF16)
    return hi, (v - hi.astype(F32)).astype(BF16)


def _split_dot(v, w):
    hi, lo = _split(v)
    return (jnp.dot(hi, w, preferred_element_type=F32)
            + jnp.dot(lo, w, preferred_element_type=F32))


_NT = (((1,), (1,)), ((), ()))


def _mod_kernel(c_ref, w_ref, b_ref, o_ref):
    c = c_ref[...]
    cond = c * _sigmoid(c)
    o_ref[0] = jnp.sum(w_ref[0] * cond, axis=0, keepdims=True) + b_ref[0]


def _mod_call(c, ada_w, ada_b):
    n_out = N_SUBLAYERS * 3 * D_MODEL
    tn = D_MODEL
    out = pl.pallas_call(
        _mod_kernel,
        grid=(DEPTH, n_out // tn),
        in_specs=[
            pl.BlockSpec((D_MODEL, 1), lambda l, j: (0, 0)),
            pl.BlockSpec((1, D_MODEL, tn), lambda l, j: (l, 0, j)),
            pl.BlockSpec((1, 1, tn), lambda l, j: (l, 0, j)),
        ],
        out_specs=pl.BlockSpec((1, 1, tn), lambda l, j: (l, 0, j)),
        out_shape=jax.ShapeDtypeStruct((DEPTH, 1, n_out), F32),
        compiler_params=_params(2),
        name="adaln_mod",
    )(c.reshape(D_MODEL, 1), ada_w, ada_b.reshape(DEPTH, 1, n_out))
    return out.reshape(DEPTH, N_SUBLAYERS, 3, D_MODEL)


def _ffn_kernel(x_ref, *refs, final, with_delta, n_cast):
    if with_delta:
        d_ref, *refs = refs
    mod_ref, g_ref, wg_hbm, wu_hbm, wd_hbm, fin_ref, *refs = refs
    cast_src, o_ref, cast_dst = refs[:n_cast], refs[n_cast], refs[n_cast + 1:2 * n_cast + 1]
    wg_ref, wu_ref, wd_ref, sem = refs[2 * n_cast + 1:]
    copies = [pltpu.make_async_copy(src, dst, sem.at[k]) for k, (src, dst) in enumerate(
        [(wg_hbm, wg_ref), (wu_hbm, wu_ref), (wd_hbm, wd_ref)])]

    def step(first):
        if first:
            for copy in copies:
                copy.start()
        for src_ref, dst_ref in zip(cast_src, cast_dst):
            dst_ref[...] = src_ref[...].astype(BF16)
        x = x_ref[...]
        if with_delta:
            x = x + d_ref[...]
        mod = mod_ref[...]
        h = _norm_mod(x, g_ref[...], mod).astype(BF16)
        if first:
            copies[0].wait()
        gate = jnp.dot(h, wg_ref[...], preferred_element_type=F32)
        if first:
            copies[1].wait()
        up = jnp.dot(h, wu_ref[...], preferred_element_type=F32)
        act = (gate * _sigmoid(gate) * up).astype(BF16)
        if first:
            copies[2].wait()
        y = jnp.dot(act, wd_ref[...], preferred_element_type=F32)
        xn = x + (0.5 * mod[2:3]) * y
        if final:
            r = lax.rsqrt(jnp.mean(xn * xn, axis=-1, keepdims=True) + EPS)
            xn = xn * r * fin_ref[...]
        o_ref[...] = xn

    pl.when(pl.program_id(0) == 0)(functools.partial(step, True))
    pl.when(pl.program_id(0) > 0)(functools.partial(step, False))


def _ffn_call(x, mod, norm_g, weights, fin, next_weights, *, layer, sub, final, delta=None,
              tm=512):
    mod_sub = 0 if sub == 0 else 2
    steps = SEQ // tm
    rows_in = [x] if delta is None else [x, delta]
    in_specs = [pl.BlockSpec((tm, D_MODEL), lambda i: (i, 0)) for _ in rows_in] + [
        _resident((None, None, 3, D_MODEL), lambda i: (layer, mod_sub, 0, 0)),
        _resident((None, None, 1, D_MODEL), lambda i: (layer, sub, 0, 0)),
        pl.BlockSpec(memory_space=pl.ANY),
        pl.BlockSpec(memory_space=pl.ANY),
        pl.BlockSpec(memory_space=pl.ANY),
        _full((1, D_MODEL)),
    ]
    assert [w.shape for w in weights] == [(D_MODEL, D_FF), (D_MODEL, D_FF), (D_FF, D_MODEL)]
    out_specs = [pl.BlockSpec((tm, D_MODEL), lambda i: (i, 0))]
    out_shape = [jax.ShapeDtypeStruct((SEQ, D_MODEL), F32)]
    operands = [*rows_in, mod, norm_g, *weights, fin]
    if next_weights is not None:
        (nl, ns), *stacks = next_weights
        for w in stacks:
            n_rows, n_cols = w.shape[2:]
            share = 1 if n_rows % (steps * V7X_BF16_SUBLANES) == 0 else 2
            rows = n_rows * share // steps
            assert n_rows * share % steps == 0 and rows % V7X_BF16_SUBLANES == 0
            in_specs.append(pl.BlockSpec((None, None, rows, n_cols),
                                         lambda i, share=share: (nl, ns, i // share, 0)))
            out_specs.append(pl.BlockSpec((rows, n_cols), lambda i, share=share: (i // share, 0)))
            out_shape.append(jax.ShapeDtypeStruct((n_rows, n_cols), BF16))
            operands.append(w)
    out = pl.pallas_call(
        functools.partial(_ffn_kernel, final=final, with_delta=delta is not None,
                          n_cast=len(out_specs) - 1),
        grid=(steps,),
        in_specs=in_specs,
        out_specs=out_specs,
        out_shape=out_shape,
        scratch_shapes=[pltpu.VMEM(w.shape, BF16) for w in weights]
        + [pltpu.SemaphoreType.DMA((len(weights),))],
        compiler_params=_params(1),
        name=f"ffn_l{layer}s{sub}",
    )(*operands)
    return out[0], tuple(out[1:])


def _conv_mix_kernel(x_ref, mod_ref, g_ref, win_ref, caw_ref, cab_ref, lng_ref, lnb_ref,
                     cbw_ref, wout_ref, seg_ref, o_ref, abuf, bbuf, *, tm):
    i = pl.program_id(0)

    @pl.when(i == 0)
    def _():
        abuf[0:A_HALO, :] = jnp.zeros((A_HALO, A_WIDTH), F32)
        bbuf[0:B_HALO, :] = jnp.zeros((B_HALO, B_WIDTH), F32)

    x = x_ref[...]
    mod = mod_ref[...]
    h = _norm_mod(x, g_ref[...], mod).astype(BF16)
    z = jnp.dot(h, win_ref[...], preferred_element_type=F32)

    a = z[:, 0:A_WIDTH] * _sigmoid(z[:, A_WIDTH:2 * A_WIDTH])
    abuf[A_HALO:A_HALO + tm, :] = a
    acc = jnp.zeros((tm, A_WIDTH), F32) + cab_ref[...]
    ext = tm + V7X_SUBLANES
    for r in range(V7X_SUBLANES):
        part = None
        for q in range((A_CONV - 1 - r) // V7X_SUBLANES + 1):
            k = A_CONV - 1 - (V7X_SUBLANES * q + r)
            term = caw_ref[k:k + 1, :] * abuf[pl.ds(A_HALO - V7X_SUBLANES * (q + 1), ext), :]
            part = term if part is None else part + term
        if r > 0:
            part = pltpu.roll(part, r, axis=0)
        acc = acc + part[V7X_SUBLANES:, :]
    abuf[0:A_HALO, :] = abuf[tm:tm + A_HALO, :]
    seg = seg_ref[...]
    mu = _split_dot(acc, seg)
    d = acc - mu
    var = _split_dot(d * d, seg)
    yn = d * lax.rsqrt(var + EPS) * lng_ref[...] + lnb_ref[...]
    a_out = yn * _sigmoid(yn)

    off = 2 * A_WIDTH
    cv = z[:, off + B_WIDTH:off + 2 * B_WIDTH] * z[:, off + 2 * B_WIDTH:off + 3 * B_WIDTH]
    bbuf[B_HALO:B_HALO + tm, :] = cv
    bacc = jnp.zeros((tm, B_WIDTH), F32)
    for k in range(B_CONV):
        bacc = bacc + cbw_ref[k:k + 1, :] * bbuf[pl.ds(B_HALO - (B_CONV - 1) + k, tm), :]
    bbuf[0:B_HALO, :] = bbuf[tm:tm + B_HALO, :]
    bb = z[:, off:off + B_WIDTH] * bacc

    y = (jnp.dot(a_out.astype(BF16), wout_ref[0:A_WIDTH, :], preferred_element_type=F32)
         + jnp.dot(bb.astype(BF16), wout_ref[A_WIDTH:A_WIDTH + B_WIDTH, :],
                   preferred_element_type=F32))
    o_ref[...] = x + mod[2:3] * y


def _conv_mix_call(x, mod, norm_g, w_in, caw, cab, lng, lnb, cbw, w_out, seg, *, layer, tm=512):
    even_in = w_in.shape[1]
    return pl.pallas_call(
        functools.partial(_conv_mix_kernel, tm=tm),
        grid=(SEQ // tm,),
        in_specs=[
            pl.BlockSpec((tm, D_MODEL), lambda i: (i, 0)),
            _resident((None, None, 3, D_MODEL), lambda i: (layer, 1, 0, 0)),
            _resident((None, 1, D_MODEL), lambda i: (layer, 0, 0)),
            _full((D_MODEL, even_in)),
            _full((A_CONV, A_WIDTH)),
            _full((1, A_WIDTH)),
            _full((1, A_WIDTH)),
            _full((1, A_WIDTH)),
            _full((B_CONV, B_WIDTH)),
            _full((A_WIDTH + B_WIDTH, D_MODEL)),
            _full((A_WIDTH, A_WIDTH)),
        ],
        out_specs=pl.BlockSpec((tm, D_MODEL), lambda i: (i, 0)),
        out_shape=jax.ShapeDtypeStruct((SEQ, D_MODEL), F32),
        scratch_shapes=[pltpu.VMEM((A_HALO + tm, A_WIDTH), F32),
                        pltpu.VMEM((B_HALO + tm, B_WIDTH), F32)],
        compiler_params=_params(1),
        name="conv_mixers",
    )(x, mod, norm_g, w_in, caw, cab, lng, lnb, cbw, w_out, seg)


def _pm_proj_kernel(x_ref, mod_ref, g_ref, wu_ref, wq_ref, wkvt_ref, pw_ref,
                    pb_ref, ps_ref, pooled_ref, qrow_ref, kv_ref, km_ref,
                    ubuf, hmean, *, tm):
    i = pl.program_id(0)

    @pl.when(i == 0)
    def _():
        ubuf[0:POOL_HALO, :] = jnp.zeros((POOL_HALO, C_WIDTH), F32)

    x = x_ref[...]
    h = _norm_mod(x, g_ref[...], mod_ref[...]).astype(BF16)
    u = jnp.dot(h, wu_ref[...], preferred_element_type=F32)
    qr = jnp.dot(h, wq_ref[...], preferred_element_type=F32)
    kvt = lax.dot_general(wkvt_ref[...], h, _NT, preferred_element_type=F32)

    ubuf[POOL_HALO:POOL_HALO + tm, :] = u
    t1 = (i * tm + 1 + lax.broadcasted_iota(I32, (tm, C_GROUP_DIM), 0)).astype(F32)
    for gi, w in enumerate(POOL_WINDOWS):
        c0 = gi * C_GROUP_DIM
        s = ubuf[:, c0:c0 + C_GROUP_DIM]
        span = 1
        while span < w:
            s = s + pltpu.roll(s, span, axis=0)
            span *= 2
        assert span == w < POOL_HALO + 1
        s = s[POOL_HALO:, :]
        pooled = s / jnp.minimum(t1, float(w)) - u[:, c0:c0 + C_GROUP_DIM]
        mixed = (jnp.dot(pooled.astype(BF16), pw_ref[gi], preferred_element_type=F32)
                 + pb_ref[:, c0:c0 + C_GROUP_DIM])
        pooled_ref[:, c0:c0 + C_GROUP_DIM] = (
            mixed * ps_ref[:, c0:c0 + C_GROUP_DIM]).astype(BF16)
    ubuf[0:POOL_HALO, :] = ubuf[tm:tm + POOL_HALO, :]

    lane = lax.broadcasted_iota(I32, (tm, HEAD_PAD), 1)
    tail = jnp.where(lane == ALIBI_COL, 1.0, 0.0)
    qs = qr * (HEAD_DIM ** -0.5)
    for pair in range(D_HEADS // 2):
        both = qs[:, pair * HEAD_PAD:(pair + 1) * HEAD_PAD]
        qrow_ref[2 * pair] = jnp.where(lane < HEAD_DIM, both, tail)
        qrow_ref[2 * pair + 1] = jnp.where(
            lane < HEAD_DIM, pltpu.roll(both, HEAD_DIM, axis=1), tail)
    per_tile = tm // MOBA_BLOCK
    kvb = kvt.astype(BF16)
    n_alibi = SLAB_K_ROWS - HEAD_DIM
    first = lax.broadcasted_iota(I32, (n_alibi, MOBA_BLOCK), 0) == 0
    koff = lax.broadcasted_iota(I32, (n_alibi, MOBA_BLOCK), 1).astype(F32)
    for hh in range(D_HEADS):
        alibi = jnp.where(first, koff * (2.0 ** -(hh + 1)), 0.0).astype(BF16)
        for sb in range(per_tile):
            keys = slice(sb * MOBA_BLOCK, (sb + 1) * MOBA_BLOCK)
            kv_ref[hh, sb, 0:HEAD_DIM, :] = kvb[hh * HEAD_DIM:(hh + 1) * HEAD_DIM, keys]
            kv_ref[hh, sb, HEAD_DIM:SLAB_K_ROWS, :] = alibi
            kv_ref[hh, sb, SLAB_K_ROWS:SLAB_ROWS, :] = kvb[
                D_WIDTH + hh * HEAD_DIM:D_WIDTH + (hh + 1) * HEAD_DIM, keys]

    for sb in range(per_tile):
        blk = h[sb * MOBA_BLOCK:(sb + 1) * MOBA_BLOCK, :].astype(F32)
        hmean[pl.ds(i * per_tile + sb, 1), :] = jnp.mean(blk, axis=0, keepdims=True)

    @pl.when(i == pl.num_programs(0) - 1)
    def _():
        hi, lo = _split(hmean[...])
        wkt = wkvt_ref[0:D_WIDTH, :]
        km_ref[...] = (lax.dot_general(hi, wkt, _NT, preferred_element_type=F32)
                       + lax.dot_general(lo, wkt, _NT, preferred_element_type=F32))


def _pm_proj_call(x, mod, norm_g, wu, wq, wkvt, pw, pb, ps, *, layer, tm=512):
    return pl.pallas_call(
        functools.partial(_pm_proj_kernel, tm=tm),
        grid=(SEQ // tm,),
        in_specs=[
            pl.BlockSpec((tm, D_MODEL), lambda i: (i, 0)),
            _resident((None, None, 3, D_MODEL), lambda i: (layer, 1, 0, 0)),
            _resident((None, 1, D_MODEL), lambda i: (layer, 0, 0)),
            _full((D_MODEL, C_WIDTH)),
            _full((D_MODEL, D_WIDTH)),
            _full((2 * D_WIDTH, D_MODEL)),
            _full((len(POOL_WINDOWS), C_GROUP_DIM, C_GROUP_DIM)),
            _full((1, C_WIDTH)),
            _full((1, C_WIDTH)),
        ],
        out_specs=[
            pl.BlockSpec((tm, C_WIDTH), lambda i: (i, 0)),
            pl.BlockSpec((D_HEADS, tm, HEAD_PAD), lambda i: (0, i, 0)),
            pl.BlockSpec((D_HEADS, tm // MOBA_BLOCK, SLAB_ROWS, MOBA_BLOCK),
                         lambda i: (0, i, 0, 0)),
            pl.BlockSpec((N_BLOCKS, D_WIDTH), lambda i: (0, 0)),
        ],
        out_shape=[
            jax.ShapeDtypeStruct((SEQ, C_WIDTH), BF16),
            jax.ShapeDtypeStruct((D_HEADS, SEQ, HEAD_PAD), F32),
            jax.ShapeDtypeStruct((D_HEADS, N_BLOCKS, SLAB_ROWS, MOBA_BLOCK), BF16),
            jax.ShapeDtypeStruct((N_BLOCKS, D_WIDTH), F32),
        ],
        scratch_shapes=[pltpu.VMEM((POOL_HALO + tm, C_WIDTH), F32),
                        pltpu.VMEM((N_BLOCKS, D_MODEL), F32)],
        compiler_params=_params(1),
        name="pool_qkv_proj",
    )(x, mod, norm_g, wu, wq, wkvt, pw, pb, ps)


def _block_rows(qs, slabs, mask, lse_add):
    n = qs[0].shape[0]
    kpad = jnp.zeros((HEAD_PAD - SLAB_K_ROWS, MOBA_BLOCK), BF16)
    s = jnp.concatenate(
        [jnp.dot(q, jnp.concatenate([slab[0:SLAB_K_ROWS, :], kpad], axis=0),
                 preferred_element_type=F32)
         for q, slab in zip(qs, slabs)], axis=0)
    if mask is not None:
        s = jnp.where(mask, s, NEG_BIG)
    m = jnp.max(s, axis=1, keepdims=True)
    p = jnp.exp(s - m).astype(BF16)
    vpad = jnp.zeros((HEAD_PAD - HEAD_DIM, MOBA_BLOCK), BF16)
    ones = jnp.ones((HEAD_PAD, MOBA_BLOCK), BF16)
    ot = jnp.concatenate(
        [lax.dot_general(p[u * n:(u + 1) * n],
                         jnp.concatenate([slab[SLAB_K_ROWS:SLAB_ROWS, :], vpad, ones], axis=0),
                         _NT, preferred_element_type=F32)
         for u, slab in enumerate(slabs)], axis=0)
    o, tot = ot[:, 0:HEAD_PAD], ot[:, HEAD_PAD:2 * HEAD_PAD]
    lse = m + jnp.log(tot)
    if lse_add is not None:
        lse = jnp.concatenate(
            [lse[u * n:(u + 1) * n] + add for u, add in enumerate(lse_add)], axis=0)
    low = lax.broadcasted_iota(I32, o.shape, 1) < HEAD_DIM
    return jnp.where(low, o / tot, lse)


def _onehot_pairs(sel_rows):
    blk = lax.broadcasted_iota(I32, (N_BLOCKS, MOBA_BLOCK), 0)
    return jnp.concatenate(
        [jnp.where(blk == sel_rows[s], 1.0, 0.0) for s in range(MOBA_TOPK)], axis=1)


def _gate_own_kernel(anchor_ref, qrow_ref, kv_ref, km_ref, tri_ref,
                     sel_ref, rank_ref, cnt_ref, own_ref, *, heads, blocks):
    del anchor_ref
    n_pair = MOBA_TOPK * MOBA_BLOCK
    tiles = [(bb, a) for bb in range(blocks) for a in range(heads)]
    rows_of = lambda bb: slice(bb * MOBA_BLOCK, (bb + 1) * MOBA_BLOCK)
    qs = [qrow_ref[a, rows_of(bb), :].astype(BF16) for bb, a in tiles]

    kms = [_split(km_ref[a]) for a in range(heads)]
    gates = [lax.dot_general(kms[a][0], q, _NT, preferred_element_type=F32)
             + lax.dot_general(kms[a][1], q, _NT, preferred_element_type=F32)
             for (bb, a), q in zip(tiles, qs)]
    width = len(tiles) * MOBA_BLOCK
    blk = lax.broadcasted_iota(I32, (N_BLOCKS, width), 0)
    lane = lax.broadcasted_iota(I32, (1, width), 1)
    own = pl.program_id(0) * blocks + lane // (heads * MOBA_BLOCK)
    g = jnp.where(blk < own, jnp.concatenate(gates, axis=1), -jnp.inf)
    slots = []
    for s in range(MOBA_TOPK):
        best = jnp.max(g, axis=0, keepdims=True)
        idx = jnp.min(jnp.where(g == best, blk, N_BLOCKS), axis=0, keepdims=True)
        g = jnp.where(blk == idx, -jnp.inf, g)
        slots.append(jnp.where(s < own, idx, N_BLOCKS))

    onehots = []
    for t, (bb, a) in enumerate(tiles):
        sel_rows = [slot[:, t * MOBA_BLOCK:(t + 1) * MOBA_BLOCK] for slot in slots]
        sel_ref[a, :, rows_of(bb)] = jnp.concatenate(sel_rows, axis=0)
        onehots.append(_onehot_pairs(sel_rows))
    onehot = jnp.concatenate(onehots, axis=0)
    oh = onehot.astype(BF16)
    before = jnp.dot(oh, tri_ref[...], preferred_element_type=F32)
    hit = onehot * before
    ones = jnp.ones((8, n_pair), BF16)
    cnt = lax.dot_general(ones, oh, _NT, preferred_element_type=F32)
    for t, (bb, a) in enumerate(tiles):
        rank = jnp.sum(hit[t * N_BLOCKS:(t + 1) * N_BLOCKS], axis=0, keepdims=True).astype(I32)
        rank_ref[a, :, rows_of(bb)] = jnp.concatenate(
            [rank[:, s * MOBA_BLOCK:(s + 1) * MOBA_BLOCK] for s in range(MOBA_TOPK)], axis=0)
        cnt_ref[a, bb] = cnt[0:1, t * N_BLOCKS:(t + 1) * N_BLOCKS].astype(I32)

    qi = lax.broadcasted_iota(I32, (width, MOBA_BLOCK), 0) % MOBA_BLOCK
    ki = lax.broadcasted_iota(I32, (width, MOBA_BLOCK), 1)
    rows = _block_rows(qs, [kv_ref.at[a, bb] for bb, a in tiles], ki <= qi, None)
    for t, (bb, a) in enumerate(tiles):
        own_ref[a, rows_of(bb), :] = rows[t * MOBA_BLOCK:(t + 1) * MOBA_BLOCK]


def _gate_own_call(anchor, qrows, kv, kmean_h, tri, *, split, blocks=GATE_BLOCKS_PER_STEP):
    heads = SPLIT_HEADS
    rows = blocks * MOBA_BLOCK
    return pl.pallas_call(
        functools.partial(_gate_own_kernel, heads=heads, blocks=blocks),
        grid=(N_BLOCKS // blocks,),
        in_specs=[
            pl.BlockSpec(memory_space=pltpu.SMEM),
            pl.BlockSpec((heads, rows, HEAD_PAD), lambda b: (split, b, 0)),
            pl.BlockSpec((heads, blocks, SLAB_ROWS, MOBA_BLOCK), lambda b: (split, b, 0, 0)),
            pl.BlockSpec((heads, N_BLOCKS, HEAD_PAD), lambda b: (split, 0, 0)),
            _full((MOBA_TOPK * MOBA_BLOCK, MOBA_TOPK * MOBA_BLOCK)),
        ],
        out_specs=[
            pl.BlockSpec((heads, MOBA_TOPK, rows), lambda b: (0, 0, b)),
            pl.BlockSpec((heads, MOBA_TOPK, rows), lambda b: (0, 0, b)),
            pl.BlockSpec((heads, blocks, 1, N_BLOCKS), lambda b: (0, b, 0, 0)),
            pl.BlockSpec((heads, rows, HEAD_PAD), lambda b: (0, b, 0)),
        ],
        out_shape=[
            jax.ShapeDtypeStruct((heads, MOBA_TOPK, SEQ), I32),
            jax.ShapeDtypeStruct((heads, MOBA_TOPK, SEQ), I32),
            jax.ShapeDtypeStruct((heads, N_BLOCKS, 1, N_BLOCKS), I32),
            jax.ShapeDtypeStruct((heads, SEQ, HEAD_PAD), F32),
        ],
        compiler_params=_params(1),
        name=f"moba_gate_own_{split}",
    )(anchor, qrows, kv, kmean_h, tri)


def _route_pos_kernel(sel_ref, rank_ref, base_ref, pos_ref, *, blocks):
    lane = lax.broadcasted_iota(I32, (1, MOBA_BLOCK), 1)
    for bb in range(blocks):
        qs = slice(bb * MOBA_BLOCK, (bb + 1) * MOBA_BLOCK)
        rows = [[] for _ in range(MOBA_TOPK)]
        for a in range(SPLIT_HEADS):
            sel = sel_ref[a, :, qs]
            onehot = _onehot_pairs([sel[s:s + 1, :] for s in range(MOBA_TOPK)]).astype(BF16)
            dig = jnp.dot(base_ref[a, bb].astype(BF16), onehot,
                          preferred_element_type=F32)
            base = (dig[0:1] * float(DIGIT * DIGIT) + dig[1:2] * float(DIGIT)
                    + dig[2:3]).astype(I32)
            for s in range(MOBA_TOPK):
                p = base[:, s * MOBA_BLOCK:(s + 1) * MOBA_BLOCK] + rank_ref[a, s:s + 1, qs]
                rows[s].append(
                    jnp.where(sel[s:s + 1, :] < N_BLOCKS, p, TRASH_ROW0 + lane % SC_WINDOW))
        for s in range(MOBA_TOPK):
            pos_ref[s, :, qs] = jnp.concatenate(rows[s], axis=0)


def _route_pos_call(sel, rank, base_digits, *, blocks=8):
    width = blocks * MOBA_BLOCK
    return pl.pallas_call(
        functools.partial(_route_pos_kernel, blocks=blocks),
        grid=(N_BLOCKS // blocks,),
        in_specs=[
            pl.BlockSpec((SPLIT_HEADS, MOBA_TOPK, width), lambda b: (0, 0, b)),
            pl.BlockSpec((SPLIT_HEADS, MOBA_TOPK, width), lambda b: (0, 0, b)),
            pl.BlockSpec((SPLIT_HEADS, blocks, 8, N_BLOCKS), lambda b: (0, b, 0, 0)),
        ],
        out_specs=pl.BlockSpec((MOBA_TOPK, SPLIT_HEADS, width), lambda b: (0, 0, b)),
        out_shape=jax.ShapeDtypeStruct((MOBA_TOPK, SPLIT_HEADS, SEQ), I32),
        compiler_params=_params(1),
        name="moba_route_pos",
    )(sel, rank, base_digits)


def _route_tables(cnt, split):
    cnt = cnt.reshape(SPLIT_HEADS, N_BLOCKS, N_BLOCKS)
    tiles = (cnt.sum(axis=1) + ROUTE_TILE - 1) // ROUTE_TILE
    tiles_flat = tiles.reshape(N_GROUPS)
    tile0 = jnp.cumsum(tiles_flat) - tiles_flat
    n_tiles = tiles_flat.sum().astype(I32)
    within = jnp.cumsum(cnt, axis=1) - cnt
    base = tile0.reshape(SPLIT_HEADS, 1, N_BLOCKS) * ROUTE_TILE + within
    digits = jnp.stack([base // (DIGIT * DIGIT), (base // DIGIT) % DIGIT, base % DIGIT],
                       axis=2).astype(F32)
    digits = jnp.pad(digits, ((0, 0), (0, 0), (0, 8 - 3), (0, 0)))
    t = jnp.arange(MAX_TILES, dtype=I32)
    ended = (tile0 + tiles_flat)[None, :] <= t[:, None]
    group_of_tile = jnp.minimum(ended.sum(axis=1).astype(I32), N_GROUPS - 1)
    tile_h = group_of_tile // N_BLOCKS
    head = split * SPLIT_HEADS + tile_h
    tile_step = lax.bitcast_convert_type((127 + 7 - head) << 23, F32)
    return digits, tile_h, group_of_tile % N_BLOCKS, n_tiles.reshape(1), tile_step


def _sc_mesh():
    return plsc.VectorSubcoreMesh(core_axis_name="core", subcore_axis_name="subcore")


def _dispatch_call(rows, pos_by_slot, *, split):
    n_rows = SPLIT_HEADS * SEQ
    first_window = split * n_rows // SC_WINDOW

    @functools.partial(
        pl.kernel, mesh=_sc_mesh(), scratch_types=[],
        out_type=jax.ShapeDtypeStruct((ROUTE_ROWS, HEAD_PAD), F32))
    def dispatch(x_hbm, i0_hbm, i1_hbm, i2_hbm, o_hbm):
        def body(x_vmem, i0_vmem, i1_vmem, i2_vmem):
            pltpu.sync_copy(x_vmem, o_hbm.at[i0_vmem.at[0]])
            pltpu.sync_copy(x_vmem, o_hbm.at[i1_vmem.at[0]])
            pltpu.sync_copy(x_vmem, o_hbm.at[i2_vmem.at[0]])

        idx_spec = pl.BlockSpec((1, SC_WINDOW), lambda i: (0, i))
        pltpu.emit_pipeline(
            body,
            grid=(n_rows // SC_WINDOW,),
            in_specs=[pl.BlockSpec((SC_WINDOW, HEAD_PAD), lambda i: (first_window + i, 0)),
                      idx_spec, idx_spec, idx_spec],
            out_specs=[],
            core_axis_name=("core", "subcore"),
            dimension_semantics=(pltpu.PARALLEL,),
        )(x_hbm, i0_hbm, i1_hbm, i2_hbm)

    return dispatch(rows, *pos_by_slot)


def _collect_call(table, idx):
    n_rows = idx.shape[1]

    @functools.partial(
        pl.kernel, mesh=_sc_mesh(), scratch_types=[],
        out_type=jax.ShapeDtypeStruct((n_rows, HEAD_PAD), F32))
    def collect(x_hbm, i_hbm, o_hbm):
        def body(i_vmem, o_vmem):
            pltpu.sync_copy(x_hbm.at[i_vmem.at[0]], o_vmem)

        pltpu.emit_pipeline(
            body,
            grid=(n_rows // SC_WINDOW,),
            in_specs=[pl.BlockSpec((1, SC_WINDOW), lambda i: (0, i))],
            out_specs=[pl.BlockSpec((SC_WINDOW, HEAD_PAD), lambda i: (i, 0))],
            core_axis_name=("core", "subcore"),
            dimension_semantics=(pltpu.PARALLEL,),
        )(i_hbm, o_hbm)

    return collect(table, idx)


def _routed_attn_kernel(th_ref, tj_ref, nt_ref, step_ref, q_ref, *refs, tiles):
    kv_refs, o_ref = refs[0:tiles], refs[tiles]
    t0 = pl.program_id(0) * tiles

    @pl.when(t0 < nt_ref[0])
    def _():
        qs = [q_ref[u * ROUTE_TILE:(u + 1) * ROUTE_TILE, :].astype(BF16) for u in range(tiles)]
        adds = [step_ref[t0 + u] * tj_ref[t0 + u].astype(F32) for u in range(tiles)]
        o_ref[...] = _block_rows(qs, kv_refs, None, adds)

    @pl.when(t0 == MAX_TILES)
    def _():
        low = lax.broadcasted_iota(I32, o_ref.shape, 1) < HEAD_DIM
        o_ref[...] = jnp.where(low, 0.0, NEG_BIG)


def _routed_attn_call(tile_h, tile_j, n_tiles, tile_step, routed_q, kv,
                      *, split, tiles=ROUTE_TILES_PER_STEP):
    n_steps = MAX_TILES // tiles + 1
    assert TRASH_ROW0 == (n_steps - 1) * tiles * ROUTE_TILE
    live = lambda s, th, tj, nt: (jnp.minimum(s, (nt[0] + tiles - 1) // tiles), 0)
    live_or_last = lambda s, th, tj, nt: (
        jnp.where(s == n_steps - 1, s, jnp.minimum(s, (nt[0] + tiles - 1) // tiles)), 0)
    head0 = split * SPLIT_HEADS

    def kv_spec(u):
        def slab(s, th, tj, nt):
            t = jnp.minimum(s, n_steps - 2) * tiles + u
            return head0 + th[t], tj[t], 0, 0
        return pl.BlockSpec((None, None, SLAB_ROWS, MOBA_BLOCK), slab)

    grid_spec = pltpu.PrefetchScalarGridSpec(
        num_scalar_prefetch=3,
        grid=(n_steps,),
        in_specs=([pl.BlockSpec(memory_space=pltpu.SMEM),
                   pl.BlockSpec((tiles * ROUTE_TILE, HEAD_PAD), live)]
                  + [kv_spec(u) for u in range(tiles)]),
        out_specs=pl.BlockSpec((tiles * ROUTE_TILE, HEAD_PAD), live_or_last),
    )
    return pl.pallas_call(
        functools.partial(_routed_attn_kernel, tiles=tiles),
        grid_spec=grid_spec,
        out_shape=jax.ShapeDtypeStruct((ROUTE_ROWS, HEAD_PAD), F32),
        compiler_params=_params(1),
        name=f"moba_routed_attn_{split}",
    )(tile_h, tile_j, n_tiles, tile_step, routed_q, *([kv] * tiles))


def _pm_out_kernel(slope_ref, mod_ref, pooled_ref, *refs, blocks):
    own_refs, got_refs = refs[0:N_SPLITS], refs[N_SPLITS:2 * N_SPLITS]
    wp_ref, wa_ref, o_ref = refs[2 * N_SPLITS:]
    low = lax.broadcasted_iota(I32, (MOBA_BLOCK, HEAD_PAD), 1) < HEAD_DIM

    def lse_of(part):
        return jnp.where(low, pltpu.roll(part, HEAD_DIM, axis=1), part)

    y = jnp.dot(pooled_ref[...], wp_ref[...], preferred_element_type=F32)
    for hh in range(D_HEADS):
        own_ref, got_ref = own_refs[hh // SPLIT_HEADS], got_refs[hh // SPLIT_HEADS]
        a = hh % SPLIT_HEADS
        atts = []
        for bb in range(blocks):
            own = pl.program_id(0) * blocks + bb
            rows = slice(bb * MOBA_BLOCK, (bb + 1) * MOBA_BLOCK)
            own_shift = slope_ref[hh] * (own * MOBA_BLOCK).astype(F32)
            parts = [own_ref[a, rows, :]] + [got_ref[s, a, rows, :] for s in range(MOBA_TOPK)]
            lses = [lse_of(parts[0]) + own_shift] + [lse_of(part) for part in parts[1:]]
            top = functools.reduce(jnp.maximum, lses)
            num = jnp.zeros((MOBA_BLOCK, HEAD_PAD), F32)
            den = jnp.zeros((MOBA_BLOCK, HEAD_PAD), F32)
            for part, ls in zip(parts, lses):
                w = jnp.exp(ls - top)
                num = num + w * part
                den = den + w
            atts.append((num / den).astype(BF16))
        y = y + jnp.dot(jnp.concatenate(atts, axis=0), wa_ref[hh], preferred_element_type=F32)
    o_ref[...] = mod_ref[2:3, :] * y


def _pm_out_call(slopes, mod, pooled, own_parts, got_parts, wp, wa, *, layer, blocks=2):
    tm = blocks * MOBA_BLOCK
    return pl.pallas_call(
        functools.partial(_pm_out_kernel, blocks=blocks),
        grid=(SEQ // tm,),
        in_specs=[
            pl.BlockSpec(memory_space=pltpu.SMEM),
            _resident((None, None, 3, D_MODEL), lambda i: (layer, 1, 0, 0)),
            pl.BlockSpec((tm, C_WIDTH), lambda i: (i, 0)),
        ] + [pl.BlockSpec((SPLIT_HEADS, tm, HEAD_PAD), lambda i: (0, i, 0))] * N_SPLITS
        + [pl.BlockSpec((MOBA_TOPK, SPLIT_HEADS, tm, HEAD_PAD), lambda i: (0, 0, i, 0))] * N_SPLITS
        + [
            _full((C_WIDTH, D_MODEL)),
            _full((D_HEADS, HEAD_PAD, D_MODEL)),
        ],
        out_specs=pl.BlockSpec((tm, D_MODEL), lambda i: (i, 0)),
        out_shape=jax.ShapeDtypeStruct((SEQ, D_MODEL), F32),
        compiler_params=_params(1),
        name="pool_moba_out",
    )(slopes, mod, pooled, *own_parts, *got_parts, wp, wa)


def _head_pad(w, axis):
    shape = list(w.shape)
    shape[axis:axis + 1] = [D_HEADS, HEAD_DIM]
    pad = [(0, 0)] * len(shape)
    pad[axis + 1] = (0, HEAD_PAD - HEAD_DIM)
    out = jnp.pad(w.reshape(shape), pad)
    shape[axis:axis + 2] = [D_HEADS * HEAD_PAD]
    return out.reshape(shape)


def kernel(x, c, ada_w, ada_b, ffn_norm, ffn_w_gate, ffn_w_up, ffn_w_down, mix_norm, conv_w_in,
           conv_a_w, conv_a_b, conv_a_ln_g, conv_a_ln_b, conv_b_w, conv_w_out, pm_w_in, pool_w,
           pool_b, pool_scale, pm_w_out, final_norm):
    assert x.shape == (1, SEQ, D_MODEL) and c.shape == (1, D_MODEL)
    xs = x.reshape(SEQ, D_MODEL)
    mod = _mod_call(c, ada_w, ada_b)

    ffn_g = ffn_norm.reshape(DEPTH, 2, 1, D_MODEL)
    mix_g = mix_norm.reshape(DEPTH, 1, D_MODEL)
    fin = final_norm.reshape(1, D_MODEL)
    ffn_stacks = (ffn_w_gate, ffn_w_up, ffn_w_down)

    def ffn(xs, weights, layer, sub, delta=None):
        last = (layer, sub) == (DEPTH - 1, 1)
        after = None if last else ((layer + sub, 1 - sub), *ffn_stacks)
        return _ffn_call(xs, mod, ffn_g, weights, fin, after, layer=layer, sub=sub, final=last,
                         delta=delta)

    xs, ffn_w = ffn(xs, tuple(w[0, 0].astype(BF16) for w in ffn_stacks), layer=0, sub=0)
    grp = np.arange(A_WIDTH) // (A_WIDTH // A_GROUPS)
    seg = jnp.asarray((grp[:, None] == grp[None, :]) / (A_WIDTH // A_GROUPS), dtype=BF16)
    xs = _conv_mix_call(
        xs, mod, mix_g, conv_w_in[0].astype(BF16), conv_a_w[0], conv_a_b[0].reshape(1, A_WIDTH),
        conv_a_ln_g[0].reshape(1, A_WIDTH), conv_a_ln_b[0].reshape(1, A_WIDTH), conv_b_w[0],
        conv_w_out[0].astype(BF16), seg, layer=0)
    xs, ffn_w = ffn(xs, ffn_w, layer=0, sub=1)

    xs, ffn_w = ffn(xs, ffn_w, layer=1, sub=0)
    w_in = pm_w_in[0].astype(BF16)
    w_u = w_in[:, 0:C_WIDTH]
    w_q = w_in[:, C_WIDTH:C_WIDTH + D_WIDTH]
    w_kvt = w_in[:, C_WIDTH + D_WIDTH:C_WIDTH + 3 * D_WIDTH].T
    pooled, qrows, kv, kmean = _pm_proj_call(
        xs, mod, mix_g, w_u, w_q, w_kvt, pool_w[0].astype(BF16),
        pool_b[0].reshape(1, C_WIDTH), pool_scale[0].reshape(1, C_WIDTH), layer=1)
    kmean_h = _head_pad(kmean, 1).reshape(N_BLOCKS, D_HEADS, HEAD_PAD).transpose(1, 0, 2)

    n_pair = MOBA_TOPK * MOBA_BLOCK
    tri = jnp.asarray(np.arange(n_pair)[:, None] < np.arange(n_pair)[None, :], dtype=BF16)
    q_flat = qrows.reshape(D_HEADS * SEQ, HEAD_PAD)
    own_parts, got_parts = [], []
    anchor = jnp.zeros((1,), I32)
    for split in range(N_SPLITS):
        sel, rank, cnt, own_part = _gate_own_call(anchor, qrows, kv, kmean_h, tri, split=split)
        base_digits, tile_h, tile_j, n_tiles, tile_step = _route_tables(cnt, split)
        pos = _route_pos_call(sel, rank, base_digits)
        anchor = pos[0, 0, 0:1]
        pos_by_slot = [pos[s].reshape(1, SPLIT_HEADS * SEQ) for s in range(MOBA_TOPK)]
        routed_q = _dispatch_call(q_flat, pos_by_slot, split=split)
        routed_parts = _routed_attn_call(tile_h, tile_j, n_tiles, tile_step, routed_q, kv,
                                         split=split)
        got = _collect_call(routed_parts, pos.reshape(1, N_PAIRS))
        own_parts.append(own_part)
        got_parts.append(got.reshape(MOBA_TOPK, SPLIT_HEADS, SEQ, HEAD_PAD))

    slopes = jnp.asarray(2.0 ** (-8.0 * np.arange(1, D_HEADS + 1) / D_HEADS), dtype=F32)
    w_out = pm_w_out[0].astype(BF16)
    w_att = _head_pad(w_out[C_WIDTH:], 0).reshape(D_HEADS, HEAD_PAD, D_MODEL)
    mixed = _pm_out_call(slopes, mod, pooled, own_parts, got_parts, w_out[0:C_WIDTH], w_att,
                         layer=1)
    xs, _ = ffn(xs, ffn_w, layer=1, sub=1, delta=mixed)
    return xs.reshape(1, SEQ, D_MODEL)
```

```python
import functools

import numpy as np
import jax
import jax.numpy as jnp
from jax import lax
from jax.experimental import pallas as pl
from jax.experimental.pallas import tpu as pltpu
from jax.experimental.pallas import tpu_sc as plsc

D_MODEL = 1024
SEQ = 16384
DEPTH = 2
N_SUBLAYERS = 3
D_FF = 2816
EPS = 1e-6

A_WIDTH = 512
A_GROUPS = 8
A_CONV = 31
B_WIDTH = 512
B_CONV = 3

POOL_WINDOWS = (2, 4, 8, 16)
C_WIDTH = 512
C_GROUP_DIM = 128
D_HEADS = 8
HEAD_DIM = 64
D_WIDTH = 512
MOBA_BLOCK = 256
MOBA_TOPK = 3
N_BLOCKS = SEQ // MOBA_BLOCK

V7X_SUBLANES = 8
V7X_BF16_SUBLANES = 16
V7X_VMEM_LIMIT_BYTES = 56 * 1024 * 1024

A_HALO = 32
B_HALO = 8
POOL_HALO = 16
NEG_BIG = -1e30

HEAD_PAD = 128
ALIBI_COL = HEAD_DIM
SLAB_K_ROWS = HEAD_DIM + V7X_BF16_SUBLANES
SLAB_ROWS = SLAB_K_ROWS + HEAD_DIM

SPLIT_HEADS = 4
N_SPLITS = D_HEADS // SPLIT_HEADS
ROUTE_TILE = 256
ROUTE_TILES_PER_STEP = 32
GATE_BLOCKS_PER_STEP = 4
N_PAIRS = SPLIT_HEADS * SEQ * MOBA_TOPK
N_GROUPS = SPLIT_HEADS * N_BLOCKS
MAX_TILES = N_PAIRS // ROUTE_TILE + N_GROUPS
TRASH_ROW0 = MAX_TILES * ROUTE_TILE
ROUTE_ROWS = (MAX_TILES + ROUTE_TILES_PER_STEP) * ROUTE_TILE
SC_WINDOW = 128
DIGIT = 128
assert MAX_TILES % ROUTE_TILES_PER_STEP == 0 and N_BLOCKS % GATE_BLOCKS_PER_STEP == 0
assert ROUTE_ROWS <= DIGIT ** 3 and MOBA_BLOCK <= ROUTE_TILE

BF16 = jnp.bfloat16
F32 = jnp.float32
I32 = jnp.int32


def _params(n_axes):
    return pltpu.CompilerParams(
        dimension_semantics=("arbitrary",) * n_axes,
        vmem_limit_bytes=V7X_VMEM_LIMIT_BYTES)


def _resident(block_shape, index_map):
    return pl.BlockSpec(block_shape, index_map, pipeline_mode=pl.Buffered(1))


def _full(shape):
    return _resident(shape, lambda *_: (0,) * len(shape))


def _sigmoid(v):
    return 1.0 / (1.0 + jnp.exp(-v))


def _norm_mod(x, g, mod):
    r = lax.rsqrt(jnp.mean(x * x, axis=-1, keepdims=True) + EPS)
    return (x * r) * (g * (1.0 + mod[1:2])) + mod[0:1]


def _split(v):
    hi = v.astype(BF16)
    return hi, (v - hi.astype(F32)).astype(BF16)


def _split_dot(v, w):
    hi, lo = _split(v)
    return (jnp.dot(hi, w, preferred_element_type=F32)
            + jnp.dot(lo, w, preferred_element_type=F32))


_NT = (((1,), (1,)), ((), ()))


def _mod_kernel(c_ref, w_ref, b_ref, o_ref):
    c = c_ref[...]
    cond = c * _sigmoid(c)
    o_ref[0] = jnp.sum(w_ref[0] * cond, axis=0, keepdims=True) + b_ref[0]


def _mod_call(c, ada_w, ada_b):
    n_out = N_SUBLAYERS * 3 * D_MODEL
    tn = D_MODEL
    out = pl.pallas_call(
        _mod_kernel,
        grid=(DEPTH, n_out // tn),
        in_specs=[
            pl.BlockSpec((D_MODEL, 1), lambda l, j: (0, 0)),
            pl.BlockSpec((1, D_MODEL, tn), lambda l, j: (l, 0, j)),
            pl.BlockSpec((1, 1, tn), lambda l, j: (l, 0, j)),
        ],
        out_specs=pl.BlockSpec((1, 1, tn), lambda l, j: (l, 0, j)),
        out_shape=jax.ShapeDtypeStruct((DEPTH, 1, n_out), F32),
        compiler_params=_params(2),
        name="adaln_mod",
    )(c.reshape(D_MODEL, 1), ada_w, ada_b.reshape(DEPTH, 1, n_out))
    return out.reshape(DEPTH, N_SUBLAYERS, 3, D_MODEL)


def _ffn_kernel(x_ref, *refs, final, with_delta, n_cast):
    if with_delta:
        d_ref, *refs = refs
    mod_ref, g_ref, wg_hbm, wu_hbm, wd_hbm, fin_ref, *refs = refs
    cast_src, o_ref, cast_dst = refs[:n_cast], refs[n_cast], refs[n_cast + 1:2 * n_cast + 1]
    wg_ref, wu_ref, wd_ref, sem = refs[2 * n_cast + 1:]
    copies = [pltpu.make_async_copy(src, dst, sem.at[k]) for k, (src, dst) in enumerate(
        [(wg_hbm, wg_ref), (wu_hbm, wu_ref), (wd_hbm, wd_ref)])]

    def step(first):
        if first:
            for k, copy in enumerate(copies):
                copy.start(priority=min(k, 1))
        for src_ref, dst_ref in zip(cast_src, cast_dst):
            dst_ref[...] = src_ref[...].astype(BF16)
        x = x_ref[...]
        if with_delta:
            x = x + d_ref[...]
        mod = mod_ref[...]
        h = _norm_mod(x, g_ref[...], mod).astype(BF16)
        if first:
            copies[0].wait()
        gate = jnp.dot(h, wg_ref[...], preferred_element_type=F32)
        if first:
            copies[1].wait()
        up = jnp.dot(h, wu_ref[...], preferred_element_type=F32)
        act = (gate * _sigmoid(gate) * up).astype(BF16)
        if first:
            copies[2].wait()
        y = jnp.dot(act, wd_ref[...], preferred_element_type=F32)
        xn = x + (0.5 * mod[2:3]) * y
        if final:
            r = lax.rsqrt(jnp.mean(xn * xn, axis=-1, keepdims=True) + EPS)
            xn = xn * r * fin_ref[...]
        o_ref[...] = xn

    pl.when(pl.program_id(0) == 0)(functools.partial(step, True))
    pl.when(pl.program_id(0) > 0)(functools.partial(step, False))


def _ffn_call(x, mod, norm_g, weights, fin, next_weights, *, layer, sub, final, delta=None,
              tm=512):
    mod_sub = 0 if sub == 0 else 2
    steps = SEQ // tm
    rows_in = [x] if delta is None else [x, delta]
    in_specs = [pl.BlockSpec((tm, D_MODEL), lambda i: (i, 0)) for _ in rows_in] + [
        _resident((None, None, 3, D_MODEL), lambda i: (layer, mod_sub, 0, 0)),
        _resident((None, None, 1, D_MODEL), lambda i: (layer, sub, 0, 0)),
        pl.BlockSpec(memory_space=pl.ANY),
        pl.BlockSpec(memory_space=pl.ANY),
        pl.BlockSpec(memory_space=pl.ANY),
        _full((1, D_MODEL)),
    ]
    assert [w.shape for w in weights] == [(D_MODEL, D_FF), (D_MODEL, D_FF), (D_FF, D_MODEL)]
    out_specs = [pl.BlockSpec((tm, D_MODEL), lambda i: (i, 0))]
    out_shape = [jax.ShapeDtypeStruct((SEQ, D_MODEL), F32)]
    operands = [*rows_in, mod, norm_g, *weights, fin]
    if next_weights is not None:
        (nl, ns), *stacks = next_weights
        for w in stacks:
            n_rows, n_cols = w.shape[2:]
            share = 1 if n_rows % (steps * V7X_BF16_SUBLANES) == 0 else 2
            rows = n_rows * share // steps
            assert n_rows * share % steps == 0 and rows % V7X_BF16_SUBLANES == 0
            in_specs.append(pl.BlockSpec((None, None, rows, n_cols),
                                         lambda i, share=share: (nl, ns, i // share, 0)))
            out_specs.append(pl.BlockSpec((rows, n_cols), lambda i, share=share: (i // share, 0)))
            out_shape.append(jax.ShapeDtypeStruct((n_rows, n_cols), BF16))
            operands.append(w)
    out = pl.pallas_call(
        functools.partial(_ffn_kernel, final=final, with_delta=delta is not None,
                          n_cast=len(out_specs) - 1),
        grid=(steps,),
        in_specs=in_specs,
        out_specs=out_specs,
        out_shape=out_shape,
        scratch_shapes=[pltpu.VMEM(w.shape, BF16) for w in weights]
        + [pltpu.SemaphoreType.DMA((len(weights),))],
        compiler_params=_params(1),
        name=f"ffn_l{layer}s{sub}",
    )(*operands)
    return out[0], tuple(out[1:])


def _conv_mix_kernel(x_ref, mod_ref, g_ref, win_ref, caw_ref, cab_ref, lng_ref, lnb_ref,
                     cbw_ref, wout_ref, seg_ref, o_ref, abuf, bbuf, *, tm):
    i = pl.program_id(0)

    @pl.when(i == 0)
    def _():
        abuf[0:A_HALO, :] = jnp.zeros((A_HALO, A_WIDTH), F32)
        bbuf[0:B_HALO, :] = jnp.zeros((B_HALO, B_WIDTH), F32)

    x = x_ref[...]
    mod = mod_ref[...]
    h = _norm_mod(x, g_ref[...], mod).astype(BF16)
    z = jnp.dot(h, win_ref[...], preferred_element_type=F32)

    a = z[:, 0:A_WIDTH] * _sigmoid(z[:, A_WIDTH:2 * A_WIDTH])
    abuf[A_HALO:A_HALO + tm, :] = a
    acc = jnp.zeros((tm, A_WIDTH), F32) + cab_ref[...]
    ext = tm + V7X_SUBLANES
    for r in range(V7X_SUBLANES):
        part = None
        for q in range((A_CONV - 1 - r) // V7X_SUBLANES + 1):
            k = A_CONV - 1 - (V7X_SUBLANES * q + r)
            term = caw_ref[k:k + 1, :] * abuf[pl.ds(A_HALO - V7X_SUBLANES * (q + 1), ext), :]
            part = term if part is None else part + term
        if r > 0:
            part = pltpu.roll(part, r, axis=0)
        acc = acc + part[V7X_SUBLANES:, :]
    abuf[0:A_HALO, :] = abuf[tm:tm + A_HALO, :]
    seg = seg_ref[...]
    mu = _split_dot(acc, seg)
    d = acc - mu
    var = _split_dot(d * d, seg)
    yn = d * lax.rsqrt(var + EPS) * lng_ref[...] + lnb_ref[...]
    a_out = yn * _sigmoid(yn)

    off = 2 * A_WIDTH
    cv = z[:, off + B_WIDTH:off + 2 * B_WIDTH] * z[:, off + 2 * B_WIDTH:off + 3 * B_WIDTH]
    bbuf[B_HALO:B_HALO + tm, :] = cv
    bacc = jnp.zeros((tm, B_WIDTH), F32)
    for k in range(B_CONV):
        bacc = bacc + cbw_ref[k:k + 1, :] * bbuf[pl.ds(B_HALO - (B_CONV - 1) + k, tm), :]
    bbuf[0:B_HALO, :] = bbuf[tm:tm + B_HALO, :]
    bb = z[:, off:off + B_WIDTH] * bacc

    y = (jnp.dot(a_out.astype(BF16), wout_ref[0:A_WIDTH, :], preferred_element_type=F32)
         + jnp.dot(bb.astype(BF16), wout_ref[A_WIDTH:A_WIDTH + B_WIDTH, :],
                   preferred_element_type=F32))
    o_ref[...] = x + mod[2:3] * y


def _conv_mix_call(x, mod, norm_g, w_in, caw, cab, lng, lnb, cbw, w_out, seg, *, layer, tm=512):
    even_in = w_in.shape[1]
    return pl.pallas_call(
        functools.partial(_conv_mix_kernel, tm=tm),
        grid=(SEQ // tm,),
        in_specs=[
            pl.BlockSpec((tm, D_MODEL), lambda i: (i, 0)),
            _resident((None, None, 3, D_MODEL), lambda i: (layer, 1, 0, 0)),
            _resident((None, 1, D_MODEL), lambda i: (layer, 0, 0)),
            _full((D_MODEL, even_in)),
            _full((A_CONV, A_WIDTH)),
            _full((1, A_WIDTH)),
            _full((1, A_WIDTH)),
            _full((1, A_WIDTH)),
            _full((B_CONV, B_WIDTH)),
            _full((A_WIDTH + B_WIDTH, D_MODEL)),
            _full((A_WIDTH, A_WIDTH)),
        ],
        out_specs=pl.BlockSpec((tm, D_MODEL), lambda i: (i, 0)),
        out_shape=jax.ShapeDtypeStruct((SEQ, D_MODEL), F32),
        scratch_shapes=[pltpu.VMEM((A_HALO + tm, A_WIDTH), F32),
                        pltpu.VMEM((B_HALO + tm, B_WIDTH), F32)],
        compiler_params=_params(1),
        name="conv_mixers",
    )(x, mod, norm_g, w_in, caw, cab, lng, lnb, cbw, w_out, seg)


def _pm_proj_kernel(x_ref, mod_ref, g_ref, wu_ref, wq_ref, wkvt_ref, pw_ref,
                    pb_ref, ps_ref, pooled_ref, qrow_ref, kv_ref, km_ref,
                    ubuf, hmean, *, tm):
    i = pl.program_id(0)

    @pl.when(i == 0)
    def _():
        ubuf[0:POOL_HALO, :] = jnp.zeros((POOL_HALO, C_WIDTH), F32)

    x = x_ref[...]
    h = _norm_mod(x, g_ref[...], mod_ref[...]).astype(BF16)
    u = jnp.dot(h, wu_ref[...], preferred_element_type=F32)
    qr = jnp.dot(h, wq_ref[...], preferred_element_type=F32)
    kvt = lax.dot_general(wkvt_ref[...], h, _NT, preferred_element_type=F32)

    ubuf[POOL_HALO:POOL_HALO + tm, :] = u
    t1 = (i * tm + 1 + lax.broadcasted_iota(I32, (tm, C_GROUP_DIM), 0)).astype(F32)
    for gi, w in enumerate(POOL_WINDOWS):
        c0 = gi * C_GROUP_DIM
        s = ubuf[:, c0:c0 + C_GROUP_DIM]
        span = 1
        while span < w:
            s = s + pltpu.roll(s, span, axis=0)
            span *= 2
        assert span == w < POOL_HALO + 1
        s = s[POOL_HALO:, :]
        pooled = s / jnp.minimum(t1, float(w)) - u[:, c0:c0 + C_GROUP_DIM]
        mixed = (jnp.dot(pooled.astype(BF16), pw_ref[gi], preferred_element_type=F32)
                 + pb_ref[:, c0:c0 + C_GROUP_DIM])
        pooled_ref[:, c0:c0 + C_GROUP_DIM] = (
            mixed * ps_ref[:, c0:c0 + C_GROUP_DIM]).astype(BF16)
    ubuf[0:POOL_HALO, :] = ubuf[tm:tm + POOL_HALO, :]

    lane = lax.broadcasted_iota(I32, (tm, HEAD_PAD), 1)
    tail = jnp.where(lane == ALIBI_COL, 1.0, 0.0)
    qs = qr * (HEAD_DIM ** -0.5)
    for pair in range(D_HEADS // 2):
        both = qs[:, pair * HEAD_PAD:(pair + 1) * HEAD_PAD]
        qrow_ref[2 * pair] = jnp.where(lane < HEAD_DIM, both, tail)
        qrow_ref[2 * pair + 1] = jnp.where(
            lane < HEAD_DIM, pltpu.roll(both, HEAD_DIM, axis=1), tail)
    per_tile = tm // MOBA_BLOCK
    kvb = kvt.astype(BF16)
    n_alibi = SLAB_K_ROWS - HEAD_DIM
    first = lax.broadcasted_iota(I32, (n_alibi, MOBA_BLOCK), 0) == 0
    koff = lax.broadcasted_iota(I32, (n_alibi, MOBA_BLOCK), 1).astype(F32)
    for hh in range(D_HEADS):
        alibi = jnp.where(first, koff * (2.0 ** -(hh + 1)), 0.0).astype(BF16)
        for sb in range(per_tile):
            keys = slice(sb * MOBA_BLOCK, (sb + 1) * MOBA_BLOCK)
            kv_ref[hh, sb, 0:HEAD_DIM, :] = kvb[hh * HEAD_DIM:(hh + 1) * HEAD_DIM, keys]
            kv_ref[hh, sb, HEAD_DIM:SLAB_K_ROWS, :] = alibi
            kv_ref[hh, sb, SLAB_K_ROWS:SLAB_ROWS, :] = kvb[
                D_WIDTH + hh * HEAD_DIM:D_WIDTH + (hh + 1) * HEAD_DIM, keys]

    for sb in range(per_tile):
        blk = h[sb * MOBA_BLOCK:(sb + 1) * MOBA_BLOCK, :].astype(F32)
        hmean[pl.ds(i * per_tile + sb, 1), :] = jnp.mean(blk, axis=0, keepdims=True)

    @pl.when(i == pl.num_programs(0) - 1)
    def _():
        hi, lo = _split(hmean[...])
        wkt = wkvt_ref[0:D_WIDTH, :]
        km_ref[...] = (lax.dot_general(hi, wkt, _NT, preferred_element_type=F32)
                       + lax.dot_general(lo, wkt, _NT, preferred_element_type=F32))


def _pm_proj_call(x, mod, norm_g, wu, wq, wkvt, pw, pb, ps, *, layer, tm=512):
    return pl.pallas_call(
        functools.partial(_pm_proj_kernel, tm=tm),
        grid=(SEQ // tm,),
        in_specs=[
            pl.BlockSpec((tm, D_MODEL), lambda i: (i, 0)),
            _resident((None, None, 3, D_MODEL), lambda i: (layer, 1, 0, 0)),
            _resident((None, 1, D_MODEL), lambda i: (layer, 0, 0)),
            _full((D_MODEL, C_WIDTH)),
            _full((D_MODEL, D_WIDTH)),
            _full((2 * D_WIDTH, D_MODEL)),
            _full((len(POOL_WINDOWS), C_GROUP_DIM, C_GROUP_DIM)),
            _full((1, C_WIDTH)),
            _full((1, C_WIDTH)),
        ],
        out_specs=[
            pl.BlockSpec((tm, C_WIDTH), lambda i: (i, 0)),
            pl.BlockSpec((D_HEADS, tm, HEAD_PAD), lambda i: (0, i, 0)),
            pl.BlockSpec((D_HEADS, tm // MOBA_BLOCK, SLAB_ROWS, MOBA_BLOCK),
                         lambda i: (0, i, 0, 0)),
            pl.BlockSpec((N_BLOCKS, D_WIDTH), lambda i: (0, 0)),
        ],
        out_shape=[
            jax.ShapeDtypeStruct((SEQ, C_WIDTH), BF16),
            jax.ShapeDtypeStruct((D_HEADS, SEQ, HEAD_PAD), F32),
            jax.ShapeDtypeStruct((D_HEADS, N_BLOCKS, SLAB_ROWS, MOBA_BLOCK), BF16),
            jax.ShapeDtypeStruct((N_BLOCKS, D_WIDTH), F32),
        ],
        scratch_shapes=[pltpu.VMEM((POOL_HALO + tm, C_WIDTH), F32),
                        pltpu.VMEM((N_BLOCKS, D_MODEL), F32)],
        compiler_params=_params(1),
        name="pool_qkv_proj",
    )(x, mod, norm_g, wu, wq, wkvt, pw, pb, ps)


def _block_rows(qs, slabs, mask, lse_add):
    n = qs[0].shape[0]
    kpad = jnp.zeros((HEAD_PAD - SLAB_K_ROWS, MOBA_BLOCK), BF16)
    s = jnp.concatenate(
        [jnp.dot(q, jnp.concatenate([slab[0:SLAB_K_ROWS, :], kpad], axis=0),
                 preferred_element_type=F32)
         for q, slab in zip(qs, slabs)], axis=0)
    if mask is not None:
        s = jnp.where(mask, s, NEG_BIG)
    m = jnp.max(s, axis=1, keepdims=True)
    p = jnp.exp(s - m).astype(BF16)
    vpad = jnp.zeros((HEAD_PAD - HEAD_DIM, MOBA_BLOCK), BF16)
    ones = jnp.ones((HEAD_PAD, MOBA_BLOCK), BF16)
    ot = jnp.concatenate(
        [lax.dot_general(p[u * n:(u + 1) * n],
                         jnp.concatenate([slab[SLAB_K_ROWS:SLAB_ROWS, :], vpad, ones], axis=0),
                         _NT, preferred_element_type=F32)
         for u, slab in enumerate(slabs)], axis=0)
    o, tot = ot[:, 0:HEAD_PAD], ot[:, HEAD_PAD:2 * HEAD_PAD]
    lse = m + jnp.log(tot)
    if lse_add is not None:
        lse = jnp.concatenate(
            [lse[u * n:(u + 1) * n] + add for u, add in enumerate(lse_add)], axis=0)
    low = lax.broadcasted_iota(I32, o.shape, 1) < HEAD_DIM
    return jnp.where(low, o / tot, lse)


def _onehot_pairs(sel_rows):
    blk = lax.broadcasted_iota(I32, (N_BLOCKS, MOBA_BLOCK), 0)
    return jnp.concatenate(
        [jnp.where(blk == sel_rows[s], 1.0, 0.0) for s in range(MOBA_TOPK)], axis=1)


def _gate_own_kernel(anchor_ref, qrow_ref, kv_ref, km_ref, tri_ref,
                     sel_ref, rank_ref, cnt_ref, own_ref, *, heads, blocks):
    del anchor_ref
    n_pair = MOBA_TOPK * MOBA_BLOCK
    tiles = [(bb, a) for bb in range(blocks) for a in range(heads)]
    rows_of = lambda bb: slice(bb * MOBA_BLOCK, (bb + 1) * MOBA_BLOCK)
    qs = [qrow_ref[a, rows_of(bb), :].astype(BF16) for bb, a in tiles]

    kms = [_split(km_ref[a]) for a in range(heads)]
    gates = [lax.dot_general(kms[a][0], q, _NT, preferred_element_type=F32)
             + lax.dot_general(kms[a][1], q, _NT, preferred_element_type=F32)
             for (bb, a), q in zip(tiles, qs)]
    width = len(tiles) * MOBA_BLOCK
    blk = lax.broadcasted_iota(I32, (N_BLOCKS, width), 0)
    lane = lax.broadcasted_iota(I32, (1, width), 1)
    own = pl.program_id(0) * blocks + lane // (heads * MOBA_BLOCK)
    g = jnp.where(blk < own, jnp.concatenate(gates, axis=1), -jnp.inf)
    slots = []
    for s in range(MOBA_TOPK):
        best = jnp.max(g, axis=0, keepdims=True)
        idx = jnp.min(jnp.where(g == best, blk, N_BLOCKS), axis=0, keepdims=True)
        g = jnp.where(blk == idx, -jnp.inf, g)
        slots.append(jnp.where(s < own, idx, N_BLOCKS))

    onehots = []
    for t, (bb, a) in enumerate(tiles):
        sel_rows = [slot[:, t * MOBA_BLOCK:(t + 1) * MOBA_BLOCK] for slot in slots]
        sel_ref[a, :, rows_of(bb)] = jnp.concatenate(sel_rows, axis=0)
        onehots.append(_onehot_pairs(sel_rows))
    onehot = jnp.concatenate(onehots, axis=0)
    oh = onehot.astype(BF16)
    before = jnp.dot(oh, tri_ref[...], preferred_element_type=F32)
    hit = onehot * before
    ones = jnp.ones((8, n_pair), BF16)
    cnt = lax.dot_general(ones, oh, _NT, preferred_element_type=F32)
    for t, (bb, a) in enumerate(tiles):
        rank = jnp.sum(hit[t * N_BLOCKS:(t + 1) * N_BLOCKS], axis=0, keepdims=True).astype(I32)
        rank_ref[a, :, rows_of(bb)] = jnp.concatenate(
            [rank[:, s * MOBA_BLOCK:(s + 1) * MOBA_BLOCK] for s in range(MOBA_TOPK)], axis=0)
        cnt_ref[a, bb] = cnt[0:1, t * N_BLOCKS:(t + 1) * N_BLOCKS].astype(I32)

    qi = lax.broadcasted_iota(I32, (width, MOBA_BLOCK), 0) % MOBA_BLOCK
    ki = lax.broadcasted_iota(I32, (width, MOBA_BLOCK), 1)
    rows = _block_rows(qs, [kv_ref.at[a, bb] for bb, a in tiles], ki <= qi, None)
    for t, (bb, a) in enumerate(tiles):
        own_ref[a, rows_of(bb), :] = rows[t * MOBA_BLOCK:(t + 1) * MOBA_BLOCK]


def _gate_own_call(anchor, qrows, kv, kmean_h, tri, *, split, blocks=GATE_BLOCKS_PER_STEP):
    heads = SPLIT_HEADS
    rows = blocks * MOBA_BLOCK
    return pl.pallas_call(
        functools.partial(_gate_own_kernel, heads=heads, blocks=blocks),
        grid=(N_BLOCKS // blocks,),
        in_specs=[
            pl.BlockSpec(memory_space=pltpu.SMEM),
            pl.BlockSpec((heads, rows, HEAD_PAD), lambda b: (split, b, 0)),
            pl.BlockSpec((heads, blocks, SLAB_ROWS, MOBA_BLOCK), lambda b: (split, b, 0, 0)),
            pl.BlockSpec((heads, N_BLOCKS, HEAD_PAD), lambda b: (split, 0, 0)),
            _full((MOBA_TOPK * MOBA_BLOCK, MOBA_TOPK * MOBA_BLOCK)),
        ],
        out_specs=[
            pl.BlockSpec((heads, MOBA_TOPK, rows), lambda b: (0, 0, b)),
            pl.BlockSpec((heads, MOBA_TOPK, rows), lambda b: (0, 0, b)),
            pl.BlockSpec((heads, blocks, 1, N_BLOCKS), lambda b: (0, b, 0, 0)),
            pl.BlockSpec((heads, rows, HEAD_PAD), lambda b: (0, b, 0)),
        ],
        out_shape=[
            jax.ShapeDtypeStruct((heads, MOBA_TOPK, SEQ), I32),
            jax.ShapeDtypeStruct((heads, MOBA_TOPK, SEQ), I32),
            jax.ShapeDtypeStruct((heads, N_BLOCKS, 1, N_BLOCKS), I32),
            jax.ShapeDtypeStruct((heads, SEQ, HEAD_PAD), F32),
        ],
        compiler_params=_params(1),
        name=f"moba_gate_own_{split}",
    )(anchor, qrows, kv, kmean_h, tri)


def _route_pos_kernel(sel_ref, rank_ref, base_ref, pos_ref, *, blocks):
    lane = lax.broadcasted_iota(I32, (1, MOBA_BLOCK), 1)
    for bb in range(blocks):
        qs = slice(bb * MOBA_BLOCK, (bb + 1) * MOBA_BLOCK)
        rows = [[] for _ in range(MOBA_TOPK)]
        for a in range(SPLIT_HEADS):
            sel = sel_ref[a, :, qs]
            onehot = _onehot_pairs([sel[s:s + 1, :] for s in range(MOBA_TOPK)]).astype(BF16)
            dig = jnp.dot(base_ref[a, bb].astype(BF16), onehot,
                          preferred_element_type=F32)
            base = (dig[0:1] * float(DIGIT * DIGIT) + dig[1:2] * float(DIGIT)
                    + dig[2:3]).astype(I32)
            for s in range(MOBA_TOPK):
                p = base[:, s * MOBA_BLOCK:(s + 1) * MOBA_BLOCK] + rank_ref[a, s:s + 1, qs]
                rows[s].append(
                    jnp.where(sel[s:s + 1, :] < N_BLOCKS, p, TRASH_ROW0 + lane % SC_WINDOW))
        for s in range(MOBA_TOPK):
            pos_ref[s, :, qs] = jnp.concatenate(rows[s], axis=0)


def _route_pos_call(sel, rank, base_digits, *, blocks=8):
    width = blocks * MOBA_BLOCK
    return pl.pallas_call(
        functools.partial(_route_pos_kernel, blocks=blocks),
        grid=(N_BLOCKS // blocks,),
        in_specs=[
            pl.BlockSpec((SPLIT_HEADS, MOBA_TOPK, width), lambda b: (0, 0, b)),
            pl.BlockSpec((SPLIT_HEADS, MOBA_TOPK, width), lambda b: (0, 0, b)),
            pl.BlockSpec((SPLIT_HEADS, blocks, 8, N_BLOCKS), lambda b: (0, b, 0, 0)),
        ],
        out_specs=pl.BlockSpec((MOBA_TOPK, SPLIT_HEADS, width), lambda b: (0, 0, b)),
        out_shape=jax.ShapeDtypeStruct((MOBA_TOPK, SPLIT_HEADS, SEQ), I32),
        compiler_params=_params(1),
        name="moba_route_pos",
    )(sel, rank, base_digits)


def _route_tables(cnt, split):
    cnt = cnt.reshape(SPLIT_HEADS, N_BLOCKS, N_BLOCKS)
    tiles = (cnt.sum(axis=1) + ROUTE_TILE - 1) // ROUTE_TILE
    tiles_flat = tiles.reshape(N_GROUPS)
    tile0 = jnp.cumsum(tiles_flat) - tiles_flat
    n_tiles = tiles_flat.sum().astype(I32)
    within = jnp.cumsum(cnt, axis=1) - cnt
    base = tile0.reshape(SPLIT_HEADS, 1, N_BLOCKS) * ROUTE_TILE + within
    digits = jnp.stack([base // (DIGIT * DIGIT), (base // DIGIT) % DIGIT, base % DIGIT],
                       axis=2).astype(F32)
    digits = jnp.pad(digits, ((0, 0), (0, 0), (0, 8 - 3), (0, 0)))
    t = jnp.arange(MAX_TILES, dtype=I32)
    ended = (tile0 + tiles_flat)[None, :] <= t[:, None]
    group_of_tile = jnp.minimum(ended.sum(axis=1).astype(I32), N_GROUPS - 1)
    tile_h = group_of_tile // N_BLOCKS
    head = split * SPLIT_HEADS + tile_h
    tile_step = lax.bitcast_convert_type((127 + 7 - head) << 23, F32)
    return digits, tile_h, group_of_tile % N_BLOCKS, n_tiles.reshape(1), tile_step


def _sc_mesh():
    return plsc.VectorSubcoreMesh(core_axis_name="core", subcore_axis_name="subcore")


def _dispatch_call(rows, pos_by_slot, *, split):
    n_rows = SPLIT_HEADS * SEQ
    first_window = split * n_rows // SC_WINDOW

    @functools.partial(
        pl.kernel, mesh=_sc_mesh(), scratch_types=[],
        out_type=jax.ShapeDtypeStruct((ROUTE_ROWS, HEAD_PAD), F32))
    def dispatch(x_hbm, i0_hbm, i1_hbm, i2_hbm, o_hbm):
        def body(x_vmem, i0_vmem, i1_vmem, i2_vmem):
            pltpu.sync_copy(x_vmem, o_hbm.at[i0_vmem.at[0]])
            pltpu.sync_copy(x_vmem, o_hbm.at[i1_vmem.at[0]])
            pltpu.sync_copy(x_vmem, o_hbm.at[i2_vmem.at[0]])

        idx_spec = pl.BlockSpec((1, SC_WINDOW), lambda i: (0, i))
        pltpu.emit_pipeline(
            body,
            grid=(n_rows // SC_WINDOW,),
            in_specs=[pl.BlockSpec((SC_WINDOW, HEAD_PAD), lambda i: (first_window + i, 0)),
                      idx_spec, idx_spec, idx_spec],
            out_specs=[],
            core_axis_name=("core", "subcore"),
            dimension_semantics=(pltpu.PARALLEL,),
        )(x_hbm, i0_hbm, i1_hbm, i2_hbm)

    return dispatch(rows, *pos_by_slot)


def _collect_call(table, idx):
    n_rows = idx.shape[1]

    @functools.partial(
        pl.kernel, mesh=_sc_mesh(), scratch_types=[],
        out_type=jax.ShapeDtypeStruct((n_rows, HEAD_PAD), F32))
    def collect(x_hbm, i_hbm, o_hbm):
        def body(i_vmem, o_vmem):
            pltpu.sync_copy(x_hbm.at[i_vmem.at[0]], o_vmem)

        pltpu.emit_pipeline(
            body,
            grid=(n_rows // SC_WINDOW,),
            in_specs=[pl.BlockSpec((1, SC_WINDOW), lambda i: (0, i))],
            out_specs=[pl.BlockSpec((SC_WINDOW, HEAD_PAD), lambda i: (i, 0))],
            core_axis_name=("core", "subcore"),
            dimension_semantics=(pltpu.PARALLEL,),
        )(i_hbm, o_hbm)

    return collect(table, idx)


def _routed_attn_kernel(th_ref, tj_ref, nt_ref, step_ref, q_ref, *refs, tiles):
    kv_refs, o_ref = refs[0:tiles], refs[tiles]
    t0 = pl.program_id(0) * tiles

    @pl.when(t0 < nt_ref[0])
    def _():
        qs = [q_ref[u * ROUTE_TILE:(u + 1) * ROUTE_TILE, :].astype(BF16) for u in range(tiles)]
        adds = [step_ref[t0 + u] * tj_ref[t0 + u].astype(F32) for u in range(tiles)]
        o_ref[...] = _block_rows(qs, kv_refs, None, adds)

    @pl.when(t0 == MAX_TILES)
    def _():
        low = lax.broadcasted_iota(I32, o_ref.shape, 1) < HEAD_DIM
        o_ref[...] = jnp.where(low, 0.0, NEG_BIG)


def _routed_attn_call(tile_h, tile_j, n_tiles, tile_step, routed_q, kv,
                      *, split, tiles=ROUTE_TILES_PER_STEP):
    n_steps = MAX_TILES // tiles + 1
    assert TRASH_ROW0 == (n_steps - 1) * tiles * ROUTE_TILE
    live = lambda s, th, tj, nt: (jnp.minimum(s, (nt[0] + tiles - 1) // tiles), 0)
    live_or_last = lambda s, th, tj, nt: (
        jnp.where(s == n_steps - 1, s, jnp.minimum(s, (nt[0] + tiles - 1) // tiles)), 0)
    head0 = split * SPLIT_HEADS

    def kv_spec(u):
        def slab(s, th, tj, nt):
            t = jnp.minimum(s, n_steps - 2) * tiles + u
            return head0 + th[t], tj[t], 0, 0
        return pl.BlockSpec((None, None, SLAB_ROWS, MOBA_BLOCK), slab)

    grid_spec = pltpu.PrefetchScalarGridSpec(
        num_scalar_prefetch=3,
        grid=(n_steps,),
        in_specs=([pl.BlockSpec(memory_space=pltpu.SMEM),
                   pl.BlockSpec((tiles * ROUTE_TILE, HEAD_PAD), live)]
                  + [kv_spec(u) for u in range(tiles)]),
        out_specs=pl.BlockSpec((tiles * ROUTE_TILE, HEAD_PAD), live_or_last),
    )
    return pl.pallas_call(
        functools.partial(_routed_attn_kernel, tiles=tiles),
        grid_spec=grid_spec,
        out_shape=jax.ShapeDtypeStruct((ROUTE_ROWS, HEAD_PAD), F32),
        compiler_params=_params(1),
        name=f"moba_routed_attn_{split}",
    )(tile_h, tile_j, n_tiles, tile_step, routed_q, *([kv] * tiles))


def _pm_out_kernel(slope_ref, mod_ref, pooled_ref, *refs, blocks):
    own_refs, got_refs = refs[0:N_SPLITS], refs[N_SPLITS:2 * N_SPLITS]
    wp_ref, wa_ref, o_ref = refs[2 * N_SPLITS:]
    low = lax.broadcasted_iota(I32, (MOBA_BLOCK, HEAD_PAD), 1) < HEAD_DIM

    def lse_of(part):
        return jnp.where(low, pltpu.roll(part, HEAD_DIM, axis=1), part)

    y = jnp.dot(pooled_ref[...], wp_ref[...], preferred_element_type=F32)
    for hh in range(D_HEADS):
        own_ref, got_ref = own_refs[hh // SPLIT_HEADS], got_refs[hh // SPLIT_HEADS]
        a = hh % SPLIT_HEADS
        atts = []
        for bb in range(blocks):
            own = pl.program_id(0) * blocks + bb
            rows = slice(bb * MOBA_BLOCK, (bb + 1) * MOBA_BLOCK)
            own_shift = slope_ref[hh] * (own * MOBA_BLOCK).astype(F32)
            parts = [own_ref[a, rows, :]] + [got_ref[s, a, rows, :] for s in range(MOBA_TOPK)]
            lses = [lse_of(parts[0]) + own_shift] + [lse_of(part) for part in parts[1:]]
            top = functools.reduce(jnp.maximum, lses)
            num = jnp.zeros((MOBA_BLOCK, HEAD_PAD), F32)
            den = jnp.zeros((MOBA_BLOCK, HEAD_PAD), F32)
            for part, ls in zip(parts, lses):
                w = jnp.exp(ls - top)
                num = num + w * part
                den = den + w
            atts.append((num / den).astype(BF16))
        y = y + jnp.dot(jnp.concatenate(atts, axis=0), wa_ref[hh], preferred_element_type=F32)
    o_ref[...] = mod_ref[2:3, :] * y


def _pm_out_call(slopes, mod, pooled, own_parts, got_parts, wp, wa, *, layer, blocks=2):
    tm = blocks * MOBA_BLOCK
    return pl.pallas_call(
        functools.partial(_pm_out_kernel, blocks=blocks),
        grid=(SEQ // tm,),
        in_specs=[
            pl.BlockSpec(memory_space=pltpu.SMEM),
            _resident((None, None, 3, D_MODEL), lambda i: (layer, 1, 0, 0)),
            pl.BlockSpec((tm, C_WIDTH), lambda i: (i, 0)),
        ] + [pl.BlockSpec((SPLIT_HEADS, tm, HEAD_PAD), lambda i: (0, i, 0))] * N_SPLITS
        + [pl.BlockSpec((MOBA_TOPK, SPLIT_HEADS, tm, HEAD_PAD), lambda i: (0, 0, i, 0))] * N_SPLITS
        + [
            _full((C_WIDTH, D_MODEL)),
            _full((D_HEADS, HEAD_PAD, D_MODEL)),
        ],
        out_specs=pl.BlockSpec((tm, D_MODEL), lambda i: (i, 0)),
        out_shape=jax.ShapeDtypeStruct((SEQ, D_MODEL), F32),
        compiler_params=_params(1),
        name="pool_moba_out",
    )(slopes, mod, pooled, *own_parts, *got_parts, wp, wa)


def _head_pad(w, axis):
    shape = list(w.shape)
    shape[axis:axis + 1] = [D_HEADS, HEAD_DIM]
    pad = [(0, 0)] * len(shape)
    pad[axis + 1] = (0, HEAD_PAD - HEAD_DIM)
    out = jnp.pad(w.reshape(shape), pad)
    shape[axis:axis + 2] = [D_HEADS * HEAD_PAD]
    return out.reshape(shape)


def kernel(x, c, ada_w, ada_b, ffn_norm, ffn_w_gate, ffn_w_up, ffn_w_down, mix_norm, conv_w_in,
           conv_a_w, conv_a_b, conv_a_ln_g, conv_a_ln_b, conv_b_w, conv_w_out, pm_w_in, pool_w,
           pool_b, pool_scale, pm_w_out, final_norm):
    assert x.shape == (1, SEQ, D_MODEL) and c.shape == (1, D_MODEL)
    xs = x.reshape(SEQ, D_MODEL)
    mod = _mod_call(c, ada_w, ada_b)

    ffn_g = ffn_norm.reshape(DEPTH, 2, 1, D_MODEL)
    mix_g = mix_norm.reshape(DEPTH, 1, D_MODEL)
    fin = final_norm.reshape(1, D_MODEL)
    ffn_stacks = (ffn_w_gate, ffn_w_up, ffn_w_down)

    def ffn(xs, weights, layer, sub, delta=None):
        last = (layer, sub) == (DEPTH - 1, 1)
        after = None if last else ((layer + sub, 1 - sub), *ffn_stacks)
        return _ffn_call(xs, mod, ffn_g, weights, fin, after, layer=layer, sub=sub, final=last,
                         delta=delta)

    xs, ffn_w = ffn(xs, tuple(w[0, 0].astype(BF16) for w in ffn_stacks), layer=0, sub=0)
    grp = np.arange(A_WIDTH) // (A_WIDTH // A_GROUPS)
    seg = jnp.asarray((grp[:, None] == grp[None, :]) / (A_WIDTH // A_GROUPS), dtype=BF16)
    xs = _conv_mix_call(
        xs, mod, mix_g, conv_w_in[0].astype(BF16), conv_a_w[0], conv_a_b[0].reshape(1, A_WIDTH),
        conv_a_ln_g[0].reshape(1, A_WIDTH), conv_a_ln_b[0].reshape(1, A_WIDTH), conv_b_w[0],
        conv_w_out[0].astype(BF16), seg, layer=0)
    xs, ffn_w = ffn(xs, ffn_w, layer=0, sub=1)

    xs, ffn_w = ffn(xs, ffn_w, layer=1, sub=0)
    w_in = pm_w_in[0].astype(BF16)
    w_u = w_in[:, 0:C_WIDTH]
    w_q = w_in[:, C_WIDTH:C_WIDTH + D_WIDTH]
    w_kvt = w_in[:, C_WIDTH + D_WIDTH:C_WIDTH + 3 * D_WIDTH].T
    pooled, qrows, kv, kmean = _pm_proj_call(
        xs, mod, mix_g, w_u, w_q, w_kvt, pool_w[0].astype(BF16),
        pool_b[0].reshape(1, C_WIDTH), pool_scale[0].reshape(1, C_WIDTH), layer=1)
    kmean_h = _head_pad(kmean, 1).reshape(N_BLOCKS, D_HEADS, HEAD_PAD).transpose(1, 0, 2)

    n_pair = MOBA_TOPK * MOBA_BLOCK
    tri = jnp.asarray(np.arange(n_pair)[:, None] < np.arange(n_pair)[None, :], dtype=BF16)
    q_flat = qrows.reshape(D_HEADS * SEQ, HEAD_PAD)
    own_parts, got_parts = [], []
    anchor = jnp.zeros((1,), I32)
    for split in range(N_SPLITS):
        sel, rank, cnt, own_part = _gate_own_call(anchor, qrows, kv, kmean_h, tri, split=split)
        base_digits, tile_h, tile_j, n_tiles, tile_step = _route_tables(cnt, split)
        pos = _route_pos_call(sel, rank, base_digits)
        anchor = pos[0, 0, 0:1]
        pos_by_slot = [pos[s].reshape(1, SPLIT_HEADS * SEQ) for s in range(MOBA_TOPK)]
        routed_q = _dispatch_call(q_flat, pos_by_slot, split=split)
        routed_parts = _routed_attn_call(tile_h, tile_j, n_tiles, tile_step, routed_q, kv,
                                         split=split)
        got = _collect_call(routed_parts, pos.reshape(1, N_PAIRS))
        own_parts.append(own_part)
        got_parts.append(got.reshape(MOBA_TOPK, SPLIT_HEADS, SEQ, HEAD_PAD))

    slopes = jnp.asarray(2.0 ** (-8.0 * np.arange(1, D_HEADS + 1) / D_HEADS), dtype=F32)
    w_out = pm_w_out[0].astype(BF16)
    w_att = _head_pad(w_out[C_WIDTH:], 0).reshape(D_HEADS, HEAD_PAD, D_MODEL)
    mixed = _pm_out_call(slopes, mod, pooled, own_parts, got_parts, w_out[0:C_WIDTH], w_att,
                         layer=1)
    xs, _ = ffn(xs, ffn_w, layer=1, sub=1, delta=mixed)
    return xs.reshape(1, SEQ, D_MODEL)
```
